```python
import jax, jax.numpy as jnp
from jax import lax
import numpy as np

D_MODEL = 2048
BATCH = 16
SEQ = 2048
DEPTH = 1

MIX_WIDTH = D_MODEL
GMLP_WIDTH = MIX_WIDTH // 2
GMLP_GROUP_DIM = 128
GMLP_GROUPS = GMLP_WIDTH // GMLP_GROUP_DIM
GMLP_CHUNK = 128
V_HEAD_DIM = 128
MLA_HEADS = (MIX_WIDTH - GMLP_WIDTH) // V_HEAD_DIM
QK_NOPE_DIM = 128
QK_ROPE_DIM = 64
QK_HEAD_DIM = QK_NOPE_DIM + QK_ROPE_DIM
Q_LORA_RANK = 512
KV_LORA_RANK = 512
ROPE_THETA = 10000.0
ATTN_BLOCK = 128
IN_COLS = 2 * GMLP_WIDTH + Q_LORA_RANK + KV_LORA_RANK + QK_ROPE_DIM
D_FF = 5504
EPS = 1e-6

kernel_name = "hymba_macaron_gmlp_mla_layer"


def rmsnorm(x, g):
    x32 = x.astype(jnp.float32)
    y = x32 * lax.rsqrt(jnp.mean(x32 * x32, axis=-1, keepdims=True) + EPS)
    return (y * g.astype(jnp.float32)).astype(x.dtype)


def swiglu(x, w_gate, w_up, w_down):
    return (jax.nn.silu(x @ w_gate) * (x @ w_up)) @ w_down


def rope_tables(positions):
    half = QK_ROPE_DIM // 2
    inv_freq = 1.0 / (ROPE_THETA ** (jnp.arange(half, dtype=jnp.float32) / half))
    ang = positions.astype(jnp.float32)[..., None] * inv_freq
    return jnp.cos(ang)[:, :, None, :], jnp.sin(ang)[:, :, None, :]


def apply_rope(x, cos, sin):
    x1, x2 = jnp.split(x.astype(jnp.float32), 2, axis=-1)
    return jnp.concatenate([x1 * cos - x2 * sin, x2 * cos + x1 * sin], axis=-1).astype(x.dtype)


def gmlp_mixer(z, v_norm_g, w_s, b_s):
    B, S, _ = z.shape
    z = jax.nn.gelu(z, approximate=False)
    u, v = jnp.split(z, 2, axis=-1)
    v = rmsnorm(v, v_norm_g)
    n_chunks = S // GMLP_CHUNK
    v = v.reshape(B, n_chunks, GMLP_CHUNK, GMLP_GROUPS, GMLP_GROUP_DIM)
    w_causal = jnp.tril(w_s)
    mixed = jnp.einsum('gts,bcsgd->bctgd', w_causal, v) + b_s.T[None, None, :, :, None]
    return u * mixed.reshape(B, S, GMLP_WIDTH)


def mla_mixer(c_q, c_kv, k_rope, cos, sin, q_norm_g, w_q_up, kv_norm_g, w_kv_up,
              q_head_g, k_head_g):
    B, S, _ = c_q.shape
    q = (rmsnorm(c_q, q_norm_g) @ w_q_up).reshape(B, S, MLA_HEADS, QK_HEAD_DIM)
    kv = (rmsnorm(c_kv, kv_norm_g) @ w_kv_up).reshape(B, S, MLA_HEADS, QK_NOPE_DIM + V_HEAD_DIM)
    k_nope, v = jnp.split(kv, [QK_NOPE_DIM], axis=-1)
    k_r = jnp.broadcast_to(k_rope[:, :, None, :], (B, S, MLA_HEADS, QK_ROPE_DIM))
    k = jnp.concatenate([k_nope, k_r], axis=-1)
    q = rmsnorm(q, q_head_g)
    k = rmsnorm(k, k_head_g)
    q = jnp.concatenate([q[..., :QK_NOPE_DIM], apply_rope(q[..., QK_NOPE_DIM:], cos, sin)], axis=-1)
    k = jnp.concatenate([k[..., :QK_NOPE_DIM], apply_rope(k[..., QK_NOPE_DIM:], cos, sin)], axis=-1)
    scale = QK_HEAD_DIM ** -0.5
    outs = []
    for i in range(S // ATTN_BLOCK):
        q_blk = q[:, i * ATTN_BLOCK:(i + 1) * ATTN_BLOCK]
        kv_len = (i + 1) * ATTN_BLOCK
        s = jnp.einsum('bqhd,bkhd->bhqk', q_blk, k[:, :kv_len]).astype(jnp.float32) * scale
        q_pos = i * ATTN_BLOCK + jnp.arange(ATTN_BLOCK)
        mask = q_pos[:, None] >= jnp.arange(kv_len)[None, :]
        s = jnp.where(mask[None, None], s, jnp.float32(-1e30))
        p = jax.nn.softmax(s, axis=-1).astype(v.dtype)
        outs.append(jnp.einsum('bhqk,bkhd->bqhd', p, v[:, :kv_len]))
    return jnp.concatenate(outs, axis=1)


def _fwd_setup_inputs(seed: int = 0) -> dict:
    key = jax.random.key(seed)
    ks = jax.random.split(key, 32)
    f32 = jnp.float32

    def normal(k, shape, scale):
        return jax.random.normal(k, shape, f32) * scale

    def gain(k, shape):
        return 1.0 + 0.05 * jax.random.normal(k, shape, f32)

    L = DEPTH
    x = jax.random.normal(ks[0], (BATCH, SEQ, D_MODEL), f32)
    offsets = jax.random.randint(ks[1], (BATCH, 1), 0, 4096, dtype=jnp.int32)
    positions = (offsets + jnp.arange(SEQ, dtype=jnp.int32)[None, :]).astype(jnp.int32)
    return {
        "x": x,
        "positions": positions,
        "ffn1_norm_g": gain(ks[2], (L, D_MODEL)),
        "ffn1_w_gate": normal(ks[3], (L, D_MODEL, D_FF), D_MODEL ** -0.5),
        "ffn1_w_up": normal(ks[4], (L, D_MODEL, D_FF), D_MODEL ** -0.5),
        "ffn1_w_down": normal(ks[5], (L, D_FF, D_MODEL), D_FF ** -0.5),
        "mix_norm_g": gain(ks[6], (L, D_MODEL)),
        "w_in": normal(ks[7], (L, D_MODEL, IN_COLS), D_MODEL ** -0.5),
        "gmlp_v_norm_g": gain(ks[8], (L, GMLP_WIDTH)),
        "gmlp_w_s": normal(ks[9], (L, GMLP_GROUPS, GMLP_CHUNK, GMLP_CHUNK), 0.5 * GMLP_CHUNK ** -0.5),
        "gmlp_b_s": 1.0 + 0.1 * jax.random.normal(ks[10], (L, GMLP_GROUPS, GMLP_CHUNK), f32),
        "mla_q_norm_g": gain(ks[11], (L, Q_LORA_RANK)),
        "mla_w_q_up": normal(ks[12], (L, Q_LORA_RANK, MLA_HEADS * QK_HEAD_DIM), Q_LORA_RANK ** -0.5),
        "mla_kv_norm_g": gain(ks[13], (L, KV_LORA_RANK)),
        "mla_w_kv_up": normal(ks[14], (L, KV_LORA_RANK, MLA_HEADS * (QK_NOPE_DIM + V_HEAD_DIM)), KV_LORA_RANK ** -0.5),
        "mla_q_head_g": gain(ks[15], (L, QK_HEAD_DIM)),
        "mla_k_head_g": gain(ks[16], (L, QK_HEAD_DIM)),
        "gmlp_out_g": gain(ks[17], (L, GMLP_GROUPS, GMLP_GROUP_DIM)),
        "mla_out_g": gain(ks[18], (L, MLA_HEADS, V_HEAD_DIM)),
        "w_out": normal(ks[19], (L, MIX_WIDTH, D_MODEL), MIX_WIDTH ** -0.5),
        "ffn2_norm_g": gain(ks[20], (L, D_MODEL)),
        "ffn2_w_gate": normal(ks[21], (L, D_MODEL, D_FF), D_MODEL ** -0.5),
        "ffn2_w_up": normal(ks[22], (L, D_MODEL, D_FF), D_MODEL ** -0.5),
        "ffn2_w_down": normal(ks[23], (L, D_FF, D_MODEL), D_FF ** -0.5),
    }


def _fwd_reference(x, positions, ffn1_norm_g, ffn1_w_gate, ffn1_w_up, ffn1_w_down, mix_norm_g, w_in,
              gmlp_v_norm_g, gmlp_w_s, gmlp_b_s, mla_q_norm_g, mla_w_q_up, mla_kv_norm_g,
              mla_w_kv_up, mla_q_head_g, mla_k_head_g, gmlp_out_g, mla_out_g, w_out,
              ffn2_norm_g, ffn2_w_gate, ffn2_w_up, ffn2_w_down):
    B, S, _ = x.shape
    cos, sin = rope_tables(positions)
    split_pts = [2 * GMLP_WIDTH, 2 * GMLP_WIDTH + Q_LORA_RANK,
                 2 * GMLP_WIDTH + Q_LORA_RANK + KV_LORA_RANK]
    for l in range(DEPTH):
        x = x + 0.5 * swiglu(rmsnorm(x, ffn1_norm_g[l]), ffn1_w_gate[l], ffn1_w_up[l], ffn1_w_down[l])
        h = rmsnorm(x, mix_norm_g[l])
        z = h @ w_in[l]
        z_a, c_q, c_kv, k_rope = jnp.split(z, split_pts, axis=-1)
        y_a = gmlp_mixer(z_a, gmlp_v_norm_g[l], gmlp_w_s[l], gmlp_b_s[l])
        y_a = rmsnorm(y_a.reshape(B, S, GMLP_GROUPS, GMLP_GROUP_DIM), gmlp_out_g[l])
        y_b = mla_mixer(c_q, c_kv, k_rope, cos, sin, mla_q_norm_g[l], mla_w_q_up[l],
                        mla_kv_norm_g[l], mla_w_kv_up[l], mla_q_head_g[l], mla_k_head_g[l])
        y_b = rmsnorm(y_b, mla_out_g[l])
        y = jnp.concatenate([y_a.reshape(B, S, GMLP_WIDTH), y_b.reshape(B, S, MLA_HEADS * V_HEAD_DIM)], axis=-1)
        x = x + y @ w_out[l]
        x = x + 0.5 * swiglu(rmsnorm(x, ffn2_norm_g[l]), ffn2_w_gate[l], ffn2_w_up[l], ffn2_w_down[l])
    return x


import jax as _jax
import jax.numpy as _jnp

TWIN_FORMAT = 'train_step'
FWD_PARAMS = ['x', 'positions', 'ffn1_norm_g', 'ffn1_w_gate', 'ffn1_w_up', 'ffn1_w_down', 'mix_norm_g', 'w_in', 'gmlp_v_norm_g', 'gmlp_w_s', 'gmlp_b_s', 'mla_q_norm_g', 'mla_w_q_up', 'mla_kv_norm_g', 'mla_w_kv_up', 'mla_q_head_g', 'mla_k_head_g', 'gmlp_out_g', 'mla_out_g', 'w_out', 'ffn2_norm_g', 'ffn2_w_gate', 'ffn2_w_up', 'ffn2_w_down']
TWIN_WEIGHTS = ['ffn1_norm_g', 'ffn1_w_gate', 'ffn1_w_up', 'ffn1_w_down', 'mix_norm_g', 'w_in', 'gmlp_v_norm_g', 'gmlp_w_s', 'gmlp_b_s', 'mla_q_norm_g', 'mla_w_q_up', 'mla_kv_norm_g', 'mla_w_kv_up', 'mla_q_head_g', 'mla_k_head_g', 'gmlp_out_g', 'mla_out_g', 'w_out', 'ffn2_norm_g', 'ffn2_w_gate', 'ffn2_w_up', 'ffn2_w_down']
TWIN_DIFF_INPUT = 'x'
TWIN_INPUTS = ['x', 'positions', 'ffn1_norm_g', 'ffn1_w_gate', 'ffn1_w_up', 'ffn1_w_down', 'mix_norm_g', 'w_in', 'gmlp_v_norm_g', 'gmlp_w_s', 'gmlp_b_s', 'mla_q_norm_g', 'mla_w_q_up', 'mla_kv_norm_g', 'mla_w_kv_up', 'mla_q_head_g', 'mla_k_head_g', 'gmlp_out_g', 'mla_out_g', 'w_out', 'ffn2_norm_g', 'ffn2_w_gate', 'ffn2_w_up', 'ffn2_w_down', 'loss_target', 'm_ffn1_norm_g', 'm_ffn1_w_gate', 'm_ffn1_w_up', 'm_ffn1_w_down', 'm_mix_norm_g', 'm_w_in', 'm_gmlp_v_norm_g', 'm_gmlp_w_s', 'm_gmlp_b_s', 'm_mla_q_norm_g', 'm_mla_w_q_up', 'm_mla_kv_norm_g', 'm_mla_w_kv_up', 'm_mla_q_head_g', 'm_mla_k_head_g', 'm_gmlp_out_g', 'm_mla_out_g', 'm_w_out', 'm_ffn2_norm_g', 'm_ffn2_w_gate', 'm_ffn2_w_up', 'm_ffn2_w_down', 'v_ffn1_norm_g', 'v_ffn1_w_gate', 'v_ffn1_w_up', 'v_ffn1_w_down', 'v_mix_norm_g', 'v_w_in', 'v_gmlp_v_norm_g', 'v_gmlp_w_s', 'v_gmlp_b_s', 'v_mla_q_norm_g', 'v_mla_w_q_up', 'v_mla_kv_norm_g', 'v_mla_w_kv_up', 'v_mla_q_head_g', 'v_mla_k_head_g', 'v_gmlp_out_g', 'v_mla_out_g', 'v_w_out', 'v_ffn2_norm_g', 'v_ffn2_w_gate', 'v_ffn2_w_up', 'v_ffn2_w_down']
TWIN_OUTPUTS = ['loss', 'grad_x', 'grad_ffn1_norm_g', 'grad_ffn1_w_gate', 'grad_ffn1_w_up', 'grad_ffn1_w_down', 'grad_mix_norm_g', 'grad_w_in', 'grad_gmlp_v_norm_g', 'grad_gmlp_w_s', 'grad_gmlp_b_s', 'grad_mla_q_norm_g', 'grad_mla_w_q_up', 'grad_mla_kv_norm_g', 'grad_mla_w_kv_up', 'grad_mla_q_head_g', 'grad_mla_k_head_g', 'grad_gmlp_out_g', 'grad_mla_out_g', 'grad_w_out', 'grad_ffn2_norm_g', 'grad_ffn2_w_gate', 'grad_ffn2_w_up', 'grad_ffn2_w_down', 'delta_ffn1_norm_g', 'delta_ffn1_w_gate', 'delta_ffn1_w_up', 'delta_ffn1_w_down', 'delta_mix_norm_g', 'delta_w_in', 'delta_gmlp_v_norm_g', 'delta_gmlp_w_s', 'delta_gmlp_b_s', 'delta_mla_q_norm_g', 'delta_mla_w_q_up', 'delta_mla_kv_norm_g', 'delta_mla_w_kv_up', 'delta_mla_q_head_g', 'delta_mla_k_head_g', 'delta_gmlp_out_g', 'delta_mla_out_g', 'delta_w_out', 'delta_ffn2_norm_g', 'delta_ffn2_w_gate', 'delta_ffn2_w_up', 'delta_ffn2_w_down', 'new_m_ffn1_norm_g', 'new_m_ffn1_w_gate', 'new_m_ffn1_w_up', 'new_m_ffn1_w_down', 'new_m_mix_norm_g', 'new_m_w_in', 'new_m_gmlp_v_norm_g', 'new_m_gmlp_w_s', 'new_m_gmlp_b_s', 'new_m_mla_q_norm_g', 'new_m_mla_w_q_up', 'new_m_mla_kv_norm_g', 'new_m_mla_w_kv_up', 'new_m_mla_q_head_g', 'new_m_mla_k_head_g', 'new_m_gmlp_out_g', 'new_m_mla_out_g', 'new_m_w_out', 'new_m_ffn2_norm_g', 'new_m_ffn2_w_gate', 'new_m_ffn2_w_up', 'new_m_ffn2_w_down', 'new_v_ffn1_norm_g', 'new_v_ffn1_w_gate', 'new_v_ffn1_w_up', 'new_v_ffn1_w_down', 'new_v_mix_norm_g', 'new_v_w_in', 'new_v_gmlp_v_norm_g', 'new_v_gmlp_w_s', 'new_v_gmlp_b_s', 'new_v_mla_q_norm_g', 'new_v_mla_w_q_up', 'new_v_mla_kv_norm_g', 'new_v_mla_w_kv_up', 'new_v_mla_q_head_g', 'new_v_mla_k_head_g', 'new_v_gmlp_out_g', 'new_v_mla_out_g', 'new_v_w_out', 'new_v_ffn2_norm_g', 'new_v_ffn2_w_gate', 'new_v_ffn2_w_up', 'new_v_ffn2_w_down']
TWIN_LEAF_KINDS = {'loss': 'loss', 'grad_x': 'grad_x', 'grad_ffn1_norm_g': 'grad_w', 'grad_ffn1_w_gate': 'grad_w', 'grad_ffn1_w_up': 'grad_w', 'grad_ffn1_w_down': 'grad_w', 'grad_mix_norm_g': 'grad_w', 'grad_w_in': 'grad_w', 'grad_gmlp_v_norm_g': 'grad_w', 'grad_gmlp_w_s': 'grad_w', 'grad_gmlp_b_s': 'grad_w', 'grad_mla_q_norm_g': 'grad_w', 'grad_mla_w_q_up': 'grad_w', 'grad_mla_kv_norm_g': 'grad_w', 'grad_mla_w_kv_up': 'grad_w', 'grad_mla_q_head_g': 'grad_w', 'grad_mla_k_head_g': 'grad_w', 'grad_gmlp_out_g': 'grad_w', 'grad_mla_out_g': 'grad_w', 'grad_w_out': 'grad_w', 'grad_ffn2_norm_g': 'grad_w', 'grad_ffn2_w_gate': 'grad_w', 'grad_ffn2_w_up': 'grad_w', 'grad_ffn2_w_down': 'grad_w', 'delta_ffn1_norm_g': 'delta_w', 'delta_ffn1_w_gate': 'delta_w', 'delta_ffn1_w_up': 'delta_w', 'delta_ffn1_w_down': 'delta_w', 'delta_mix_norm_g': 'delta_w', 'delta_w_in': 'delta_w', 'delta_gmlp_v_norm_g': 'delta_w', 'delta_gmlp_w_s': 'delta_w', 'delta_gmlp_b_s': 'delta_w', 'delta_mla_q_norm_g': 'delta_w', 'delta_mla_w_q_up': 'delta_w', 'delta_mla_kv_norm_g': 'delta_w', 'delta_mla_w_kv_up': 'delta_w', 'delta_mla_q_head_g': 'delta_w', 'delta_mla_k_head_g': 'delta_w', 'delta_gmlp_out_g': 'delta_w', 'delta_mla_out_g': 'delta_w', 'delta_w_out': 'delta_w', 'delta_ffn2_norm_g': 'delta_w', 'delta_ffn2_w_gate': 'delta_w', 'delta_ffn2_w_up': 'delta_w', 'delta_ffn2_w_down': 'delta_w', 'new_m_ffn1_norm_g': 'new_m', 'new_m_ffn1_w_gate': 'new_m', 'new_m_ffn1_w_up': 'new_m', 'new_m_ffn1_w_down': 'new_m', 'new_m_mix_norm_g': 'new_m', 'new_m_w_in': 'new_m', 'new_m_gmlp_v_norm_g': 'new_m', 'new_m_gmlp_w_s': 'new_m', 'new_m_gmlp_b_s': 'new_m', 'new_m_mla_q_norm_g': 'new_m', 'new_m_mla_w_q_up': 'new_m', 'new_m_mla_kv_norm_g': 'new_m', 'new_m_mla_w_kv_up': 'new_m', 'new_m_mla_q_head_g': 'new_m', 'new_m_mla_k_head_g': 'new_m', 'new_m_gmlp_out_g': 'new_m', 'new_m_mla_out_g': 'new_m', 'new_m_w_out': 'new_m', 'new_m_ffn2_norm_g': 'new_m', 'new_m_ffn2_w_gate': 'new_m', 'new_m_ffn2_w_up': 'new_m', 'new_m_ffn2_w_down': 'new_m', 'new_v_ffn1_norm_g': 'new_v', 'new_v_ffn1_w_gate': 'new_v', 'new_v_ffn1_w_up': 'new_v', 'new_v_ffn1_w_down': 'new_v', 'new_v_mix_norm_g': 'new_v', 'new_v_w_in': 'new_v', 'new_v_gmlp_v_norm_g': 'new_v', 'new_v_gmlp_w_s': 'new_v', 'new_v_gmlp_b_s': 'new_v', 'new_v_mla_q_norm_g': 'new_v', 'new_v_mla_w_q_up': 'new_v', 'new_v_mla_kv_norm_g': 'new_v', 'new_v_mla_w_kv_up': 'new_v', 'new_v_mla_q_head_g': 'new_v', 'new_v_mla_k_head_g': 'new_v', 'new_v_gmlp_out_g': 'new_v', 'new_v_mla_out_g': 'new_v', 'new_v_w_out': 'new_v', 'new_v_ffn2_norm_g': 'new_v', 'new_v_ffn2_w_gate': 'new_v', 'new_v_ffn2_w_up': 'new_v', 'new_v_ffn2_w_down': 'new_v'}


def _forward(args):
    return _fwd_reference(*[args[k] for k in FWD_PARAMS])


def _output_shape():
    out = _jax.eval_shape(lambda: _forward(_fwd_setup_inputs(0)))
    return out.shape, out.dtype

N_MICROBATCH = 1
ADAM_LR = 0.001
ADAM_B1 = 0.9
ADAM_B2 = 0.999
ADAM_EPS = 1e-08
ADAM_WD = 0.01
ADAM_STEP = 10
PER_EXAMPLE_BATCH_AXIS = {'x': 0, 'positions': 0, 'loss_target': 0}
SHARED_INPUTS = []
_WEIGHT_DTYPES = {'ffn1_norm_g': _jnp.float32, 'ffn1_w_gate': _jnp.float32, 'ffn1_w_up': _jnp.float32, 'ffn1_w_down': _jnp.float32, 'mix_norm_g': _jnp.float32, 'w_in': _jnp.float32, 'gmlp_v_norm_g': _jnp.float32, 'gmlp_w_s': _jnp.float32, 'gmlp_b_s': _jnp.float32, 'mla_q_norm_g': _jnp.float32, 'mla_w_q_up': _jnp.float32, 'mla_kv_norm_g': _jnp.float32, 'mla_w_kv_up': _jnp.float32, 'mla_q_head_g': _jnp.float32, 'mla_k_head_g': _jnp.float32, 'gmlp_out_g': _jnp.float32, 'mla_out_g': _jnp.float32, 'w_out': _jnp.float32, 'ffn2_norm_g': _jnp.float32, 'ffn2_w_gate': _jnp.float32, 'ffn2_w_up': _jnp.float32, 'ffn2_w_down': _jnp.float32}
MOMENT_SCALE = {'ffn1_norm_g': 3.144738e+00, 'ffn1_w_gate': 1.776826e-01, 'ffn1_w_up': 1.767356e-01, 'ffn1_w_down': 2.944189e-01, 'mix_norm_g': 8.350343e-01, 'w_in': 6.433799e-01, 'gmlp_v_norm_g': 9.894891e-02, 'gmlp_w_s': 1.043510e-01, 'gmlp_b_s': 5.664382e-02, 'mla_q_norm_g': 3.819233e-01, 'mla_w_q_up': 1.971367e-01, 'mla_kv_norm_g': 2.016672e+00, 'mla_w_kv_up': 8.300787e-01, 'mla_q_head_g': 6.221944e-01, 'mla_k_head_g': 5.872905e-01, 'gmlp_out_g': 1.618426e+01, 'mla_out_g': 1.615879e+01, 'w_out': 1.613646e+00, 'ffn2_norm_g': 3.144236e+00, 'ffn2_w_gate': 8.775674e-02, 'ffn2_w_up': 8.327399e-02, 'ffn2_w_down': 1.300356e-01}


def _to_microbatches(a, axis):
    t = _jnp.moveaxis(a, axis, 0)
    t = t.reshape((N_MICROBATCH, t.shape[0] // N_MICROBATCH) + t.shape[1:])
    return _jnp.moveaxis(t, 1, axis + 1)


def setup_inputs(seed: int = 0) -> dict:
    inp = _fwd_setup_inputs(seed)
    key = _jax.random.fold_in(_jax.random.key(seed), 7919)
    shape, _ = _output_shape()
    out = dict(inp)
    out["loss_target"] = _jax.random.normal(_jax.random.fold_in(key, 0), shape, _jnp.float32)
    for i, name in enumerate(TWIN_WEIGHTS):
        w = inp[name].astype(_jnp.float32)
        if MOMENT_SCALE is None:
            s = _jnp.sqrt(_jnp.mean(_jnp.square(w)) + 1e-30)
        else:
            s = MOMENT_SCALE[name]
        km, kv = _jax.random.split(_jax.random.fold_in(key, i + 1))
        out[name] = w
        out["m_" + name] = s * _jax.random.normal(km, w.shape, _jnp.float32)
        out["v_" + name] = (s * s) * _jax.random.uniform(kv, w.shape, _jnp.float32, 0.5, 1.5)
    if N_MICROBATCH > 1:
        for name, axis in PER_EXAMPLE_BATCH_AXIS.items():
            out[name] = _to_microbatches(out[name], axis)
    return {'x': out['x'], 'positions': out['positions'], 'ffn1_norm_g': out['ffn1_norm_g'], 'ffn1_w_gate': out['ffn1_w_gate'], 'ffn1_w_up': out['ffn1_w_up'], 'ffn1_w_down': out['ffn1_w_down'], 'mix_norm_g': out['mix_norm_g'], 'w_in': out['w_in'], 'gmlp_v_norm_g': out['gmlp_v_norm_g'], 'gmlp_w_s': out['gmlp_w_s'], 'gmlp_b_s': out['gmlp_b_s'], 'mla_q_norm_g': out['mla_q_norm_g'], 'mla_w_q_up': out['mla_w_q_up'], 'mla_kv_norm_g': out['mla_kv_norm_g'], 'mla_w_kv_up': out['mla_w_kv_up'], 'mla_q_head_g': out['mla_q_head_g'], 'mla_k_head_g': out['mla_k_head_g'], 'gmlp_out_g': out['gmlp_out_g'], 'mla_out_g': out['mla_out_g'], 'w_out': out['w_out'], 'ffn2_norm_g': out['ffn2_norm_g'], 'ffn2_w_gate': out['ffn2_w_gate'], 'ffn2_w_up': out['ffn2_w_up'], 'ffn2_w_down': out['ffn2_w_down'], 'loss_target': out['loss_target'], 'm_ffn1_norm_g': out['m_ffn1_norm_g'], 'm_ffn1_w_gate': out['m_ffn1_w_gate'], 'm_ffn1_w_up': out['m_ffn1_w_up'], 'm_ffn1_w_down': out['m_ffn1_w_down'], 'm_mix_norm_g': out['m_mix_norm_g'], 'm_w_in': out['m_w_in'], 'm_gmlp_v_norm_g': out['m_gmlp_v_norm_g'], 'm_gmlp_w_s': out['m_gmlp_w_s'], 'm_gmlp_b_s': out['m_gmlp_b_s'], 'm_mla_q_norm_g': out['m_mla_q_norm_g'], 'm_mla_w_q_up': out['m_mla_w_q_up'], 'm_mla_kv_norm_g': out['m_mla_kv_norm_g'], 'm_mla_w_kv_up': out['m_mla_w_kv_up'], 'm_mla_q_head_g': out['m_mla_q_head_g'], 'm_mla_k_head_g': out['m_mla_k_head_g'], 'm_gmlp_out_g': out['m_gmlp_out_g'], 'm_mla_out_g': out['m_mla_out_g'], 'm_w_out': out['m_w_out'], 'm_ffn2_norm_g': out['m_ffn2_norm_g'], 'm_ffn2_w_gate': out['m_ffn2_w_gate'], 'm_ffn2_w_up': out['m_ffn2_w_up'], 'm_ffn2_w_down': out['m_ffn2_w_down'], 'v_ffn1_norm_g': out['v_ffn1_norm_g'], 'v_ffn1_w_gate': out['v_ffn1_w_gate'], 'v_ffn1_w_up': out['v_ffn1_w_up'], 'v_ffn1_w_down': out['v_ffn1_w_down'], 'v_mix_norm_g': out['v_mix_norm_g'], 'v_w_in': out['v_w_in'], 'v_gmlp_v_norm_g': out['v_gmlp_v_norm_g'], 'v_gmlp_w_s': out['v_gmlp_w_s'], 'v_gmlp_b_s': out['v_gmlp_b_s'], 'v_mla_q_norm_g': out['v_mla_q_norm_g'], 'v_mla_w_q_up': out['v_mla_w_q_up'], 'v_mla_kv_norm_g': out['v_mla_kv_norm_g'], 'v_mla_w_kv_up': out['v_mla_w_kv_up'], 'v_mla_q_head_g': out['v_mla_q_head_g'], 'v_mla_k_head_g': out['v_mla_k_head_g'], 'v_gmlp_out_g': out['v_gmlp_out_g'], 'v_mla_out_g': out['v_mla_out_g'], 'v_w_out': out['v_w_out'], 'v_ffn2_norm_g': out['v_ffn2_norm_g'], 'v_ffn2_w_gate': out['v_ffn2_w_gate'], 'v_ffn2_w_up': out['v_ffn2_w_up'], 'v_ffn2_w_down': out['v_ffn2_w_down']}


def _loss(weights, diff, rest, loss_target):
    with _jax.named_scope("forward"):
        args = {**rest, TWIN_DIFF_INPUT: diff, **{k: w.astype(_WEIGHT_DTYPES[k]) for k, w in weights.items()}}
        y = _forward(args)
    with _jax.named_scope("loss_head"):
        err = _jnp.square(y.astype(_jnp.float32) - loss_target)
        return 0.5 * _jnp.sum(_jnp.mean(err, axis=-1)) if err.ndim else 0.5 * err


def _adamw(w, g, m, v):
    m = ADAM_B1 * m + (1.0 - ADAM_B1) * g
    v = ADAM_B2 * v + (1.0 - ADAM_B2) * _jnp.square(g)
    m_hat = m / (1.0 - ADAM_B1 ** ADAM_STEP)
    v_hat = v / (1.0 - ADAM_B2 ** ADAM_STEP)
    delta = -ADAM_LR * (m_hat / (_jnp.sqrt(v_hat) + ADAM_EPS) + ADAM_WD * w)
    return delta, m, v


def reference(x, positions, ffn1_norm_g, ffn1_w_gate, ffn1_w_up, ffn1_w_down, mix_norm_g, w_in, gmlp_v_norm_g, gmlp_w_s, gmlp_b_s, mla_q_norm_g, mla_w_q_up, mla_kv_norm_g, mla_w_kv_up, mla_q_head_g, mla_k_head_g, gmlp_out_g, mla_out_g, w_out, ffn2_norm_g, ffn2_w_gate, ffn2_w_up, ffn2_w_down, loss_target, m_ffn1_norm_g, m_ffn1_w_gate, m_ffn1_w_up, m_ffn1_w_down, m_mix_norm_g, m_w_in, m_gmlp_v_norm_g, m_gmlp_w_s, m_gmlp_b_s, m_mla_q_norm_g, m_mla_w_q_up, m_mla_kv_norm_g, m_mla_w_kv_up, m_mla_q_head_g, m_mla_k_head_g, m_gmlp_out_g, m_mla_out_g, m_w_out, m_ffn2_norm_g, m_ffn2_w_gate, m_ffn2_w_up, m_ffn2_w_down, v_ffn1_norm_g, v_ffn1_w_gate, v_ffn1_w_up, v_ffn1_w_down, v_mix_norm_g, v_w_in, v_gmlp_v_norm_g, v_gmlp_w_s, v_gmlp_b_s, v_mla_q_norm_g, v_mla_w_q_up, v_mla_kv_norm_g, v_mla_w_kv_up, v_mla_q_head_g, v_mla_k_head_g, v_gmlp_out_g, v_mla_out_g, v_w_out, v_ffn2_norm_g, v_ffn2_w_gate, v_ffn2_w_up, v_ffn2_w_down):
    given = dict(x=x, positions=positions, ffn1_norm_g=ffn1_norm_g, ffn1_w_gate=ffn1_w_gate, ffn1_w_up=ffn1_w_up, ffn1_w_down=ffn1_w_down, mix_norm_g=mix_norm_g, w_in=w_in, gmlp_v_norm_g=gmlp_v_norm_g, gmlp_w_s=gmlp_w_s, gmlp_b_s=gmlp_b_s, mla_q_norm_g=mla_q_norm_g, mla_w_q_up=mla_w_q_up, mla_kv_norm_g=mla_kv_norm_g, mla_w_kv_up=mla_w_kv_up, mla_q_head_g=mla_q_head_g, mla_k_head_g=mla_k_head_g, gmlp_out_g=gmlp_out_g, mla_out_g=mla_out_g, w_out=w_out, ffn2_norm_g=ffn2_norm_g, ffn2_w_gate=ffn2_w_gate, ffn2_w_up=ffn2_w_up, ffn2_w_down=ffn2_w_down, loss_target=loss_target, m_ffn1_norm_g=m_ffn1_norm_g, m_ffn1_w_gate=m_ffn1_w_gate, m_ffn1_w_up=m_ffn1_w_up, m_ffn1_w_down=m_ffn1_w_down, m_mix_norm_g=m_mix_norm_g, m_w_in=m_w_in, m_gmlp_v_norm_g=m_gmlp_v_norm_g, m_gmlp_w_s=m_gmlp_w_s, m_gmlp_b_s=m_gmlp_b_s, m_mla_q_norm_g=m_mla_q_norm_g, m_mla_w_q_up=m_mla_w_q_up, m_mla_kv_norm_g=m_mla_kv_norm_g, m_mla_w_kv_up=m_mla_w_kv_up, m_mla_q_head_g=m_mla_q_head_g, m_mla_k_head_g=m_mla_k_head_g, m_gmlp_out_g=m_gmlp_out_g, m_mla_out_g=m_mla_out_g, m_w_out=m_w_out, m_ffn2_norm_g=m_ffn2_norm_g, m_ffn2_w_gate=m_ffn2_w_gate, m_ffn2_w_up=m_ffn2_w_up, m_ffn2_w_down=m_ffn2_w_down, v_ffn1_norm_g=v_ffn1_norm_g, v_ffn1_w_gate=v_ffn1_w_gate, v_ffn1_w_up=v_ffn1_w_up, v_ffn1_w_down=v_ffn1_w_down, v_mix_norm_g=v_mix_norm_g, v_w_in=v_w_in, v_gmlp_v_norm_g=v_gmlp_v_norm_g, v_gmlp_w_s=v_gmlp_w_s, v_gmlp_b_s=v_gmlp_b_s, v_mla_q_norm_g=v_mla_q_norm_g, v_mla_w_q_up=v_mla_w_q_up, v_mla_kv_norm_g=v_mla_kv_norm_g, v_mla_w_kv_up=v_mla_w_kv_up, v_mla_q_head_g=v_mla_q_head_g, v_mla_k_head_g=v_mla_k_head_g, v_gmlp_out_g=v_gmlp_out_g, v_mla_out_g=v_mla_out_g, v_w_out=v_w_out, v_ffn2_norm_g=v_ffn2_norm_g, v_ffn2_w_gate=v_ffn2_w_gate, v_ffn2_w_up=v_ffn2_w_up, v_ffn2_w_down=v_ffn2_w_down)
    weights = {n: given[n] for n in TWIN_WEIGHTS}
    shared = {n: given[n] for n in SHARED_INPUTS}
    per_example = {n: given[n] for n in ['x', 'positions']}
    grad_fn = _jax.value_and_grad(_loss, argnums=(0, 1))

    def one_microbatch(ex, loss_target):
        ex = dict(ex)
        diff = ex.pop(TWIN_DIFF_INPUT)
        return grad_fn(weights, diff, {**shared, **ex}, loss_target)

    if N_MICROBATCH == 1:
        loss, (grad_w, grad_x) = one_microbatch(per_example, given["loss_target"])
    else:
        def body(carry, xs):
            loss_sum, grad_sum = carry
            l_k, (gw_k, gx_k) = one_microbatch(xs[0], xs[1])
            with _jax.named_scope("update"):
                return (loss_sum + l_k, _jax.tree.map(_jnp.add, grad_sum, gw_k)), gx_k

        init = (_jnp.zeros((), _jnp.float32), _jax.tree.map(_jnp.zeros_like, weights))
        (loss, grad_w), grad_x = _jax.lax.scan(body, init, (per_example, given["loss_target"]))
    with _jax.named_scope("update"):
        delta_w, new_m, new_v = {}, {}, {}
        for n in TWIN_WEIGHTS:
            delta_w[n], new_m[n], new_v[n] = _adamw(weights[n], grad_w[n], given["m_" + n], given["v_" + n])
    return (loss, grad_x, *[grad_w[n] for n in TWIN_WEIGHTS], *[delta_w[n] for n in TWIN_WEIGHTS],
            *[new_m[n] for n in TWIN_WEIGHTS], *[new_v[n] for n in TWIN_WEIGHTS])
```

```python
import functools

import jax
import jax.numpy as jnp
from jax import lax
from jax.experimental import pallas as pl
from jax.experimental.pallas import tpu as pltpu

F32 = jnp.float32
BF16 = jnp.bfloat16
EPS = 1e-6
LANE = 128
SUBLANE = 8
N_DEV = 8
VMEM_LIMIT = 60 * 1024 * 1024
NOPE = 128
ROPE = 64
VHEAD = 128
QK = NOPE + ROPE
HEADW = 2 * LANE
CHUNK = 128
ROPE_THETA = 10000.0
ADAM_LR, ADAM_B1, ADAM_B2, ADAM_EPS, ADAM_WD, ADAM_STEP = 0.001, 0.9, 0.999, 1e-08, 0.01, 10
MESH = pl.DeviceIdType.MESH
ANY = pl.BlockSpec(memory_space=pl.ANY)
WHOLE_VMEM = pl.BlockSpec(memory_space=pltpu.VMEM)


def _params(sem=None):
    return pltpu.CompilerParams(dimension_semantics=sem, vmem_limit_bytes=VMEM_LIMIT)


def _round_up(n, m):
    return -(-n // m) * m


def _row_block(rows, target):
    best = rows
    for cand in range(SUBLANE, min(rows, target) + 1, SUBLANE):
        if rows % cand == 0:
            best = cand
    return best if best <= target else rows


def _nn(a, b):
    return jnp.dot(a, b, preferred_element_type=F32)


def _nt(a, b):
    return lax.dot_general(a, b, (((1,), (1,)), ((), ())), preferred_element_type=F32)


def _tn(a, b):
    return lax.dot_general(a, b, (((0,), (0,)), ((), ())), preferred_element_type=F32)


def _rstd(x, n):
    return lax.rsqrt(jnp.sum(x * x, axis=-1, keepdims=True) * (1.0 / n) + EPS)


def _rms_fwd(x, g, n):
    return x * _rstd(x, n) * g


def _rms_bwd(x, g, dy, n):
    r = _rstd(x, n)
    xh = x * r
    dyg = dy * g
    dx = r * (dyg - xh * (jnp.sum(dyg * xh, axis=-1, keepdims=True) * (1.0 / n)))
    return dx, jnp.sum(dy * xh, axis=0, keepdims=True)


def _gelu(x):
    return 0.5 * x * (1.0 + lax.erf(x * 0.7071067811865476))


def _gelu_grad(x):
    return 0.5 * (1.0 + lax.erf(x * 0.7071067811865476)) + x * jnp.exp(-0.5 * x * x) * 0.3989422804014327


def _ffn_fwd(x, g, wg_t, wu_t, wd, tm, name):
    t, d = x.shape
    nb, fp, _ = wg_t.shape

    def body(x_ref, g_ref, wg_ref, wu_ref, wd_ref, xo_ref, xn_ref, a_ref, b_ref, acc):
        j = pl.program_id(1)

        @pl.when(j == 0)
        def _():
            xn_ref[...] = _rms_fwd(x_ref[...], g_ref[...], d).astype(BF16)
            acc[...] = jnp.zeros_like(acc)

        xn = xn_ref[...]
        a = _nt(xn, wg_ref[0]).astype(BF16)
        b = _nt(xn, wu_ref[0]).astype(BF16)
        a_ref[...] = a
        b_ref[...] = b
        a = a.astype(F32)
        h = (a * jax.nn.sigmoid(a)) * b.astype(F32)
        acc[...] += _nn(h.astype(BF16), wd_ref[0])

        @pl.when(j == nb - 1)
        def _():
            xo_ref[...] = x_ref[...] + 0.5 * acc[...]

    wspec = pl.BlockSpec((1, fp, d), lambda i, j: (j, 0, 0))
    row = pl.BlockSpec((tm, d), lambda i, j: (i, 0))
    ff = pl.BlockSpec((tm, fp), lambda i, j: (i, j))
    return pl.pallas_call(
        body, name=name, grid=(t // tm, nb),
        in_specs=[row, pl.BlockSpec((1, d), lambda i, j: (0, 0)), wspec, wspec, wspec],
        out_specs=[row, row, ff, ff],
        out_shape=[jax.ShapeDtypeStruct((t, d), F32), jax.ShapeDtypeStruct((t, d), BF16),
                   jax.ShapeDtypeStruct((t, nb * fp), BF16), jax.ShapeDtypeStruct((t, nb * fp), BF16)],
        scratch_shapes=[pltpu.VMEM((tm, d), F32)],
        compiler_params=_params(("arbitrary", "arbitrary")),
    )(x, g, wg_t, wu_t, wd)


def _ffn_bwd(dout, x, g, a, b, wg_t, wu_t, wd, tm, name):
    t, d = x.shape
    nb, fp, _ = wg_t.shape

    def body(do_ref, x_ref, g_ref, a_ref, b_ref, wg_ref, wu_ref, wd_ref,
             dx_ref, dg_ref, dy_ref, da_ref, db_ref, h_ref, acc):
        i, j = pl.program_id(0), pl.program_id(1)

        @pl.when(j == 0)
        def _():
            dy_ref[...] = (0.5 * do_ref[...]).astype(BF16)
            acc[...] = jnp.zeros_like(acc)

        @pl.when((i == 0) & (j == 0))
        def _():
            dg_ref[...] = jnp.zeros_like(dg_ref)

        dh = _nt(dy_ref[...], wd_ref[0])
        a = a_ref[...].astype(F32)
        bb = b_ref[...].astype(F32)
        s = jax.nn.sigmoid(a)
        sl = a * s
        h_ref[...] = (sl * bb).astype(BF16)
        da = (dh * bb * (s * (1.0 + a * (1.0 - s)))).astype(BF16)
        db = (dh * sl).astype(BF16)
        da_ref[...] = da
        db_ref[...] = db
        acc[...] += _nn(da, wg_ref[0]) + _nn(db, wu_ref[0])

        @pl.when(j == nb - 1)
        def _():
            dxn, dg = _rms_bwd(x_ref[...], g_ref[...], acc[...], d)
            dx_ref[...] = do_ref[...] + dxn
            dg_ref[...] += dg

    wspec = pl.BlockSpec((1, fp, d), lambda i, j: (j, 0, 0))
    row = pl.BlockSpec((tm, d), lambda i, j: (i, 0))
    vec = pl.BlockSpec((1, d), lambda i, j: (0, 0))
    ff = pl.BlockSpec((tm, fp), lambda i, j: (i, j))
    ffs = jax.ShapeDtypeStruct((t, nb * fp), BF16)
    return pl.pallas_call(
        body, name=name, grid=(t // tm, nb),
        in_specs=[row, row, vec, ff, ff, wspec, wspec, wspec],
        out_specs=[row, vec, row, ff, ff, ff],
        out_shape=[jax.ShapeDtypeStruct((t, d), F32), jax.ShapeDtypeStruct((1, d), F32),
                   jax.ShapeDtypeStruct((t, d), BF16), ffs, ffs, ffs],
        scratch_shapes=[pltpu.VMEM((tm, d), F32)],
        compiler_params=_params(("arbitrary", "arbitrary")),
    )(dout, x, g, a, b, wg_t, wu_t, wd)


def _matmul_tn(lhs, rhs, bm, bn, tk, out_dtype, name):
    t, m = lhs.shape
    n = rhs.shape[1]
    nk = t // tk

    def body(l_ref, r_ref, o_ref, acc):
        k = pl.program_id(2)

        @pl.when(k == 0)
        def _():
            acc[...] = jnp.zeros_like(acc)

        acc[...] += _tn(l_ref[...].astype(BF16), r_ref[...].astype(BF16))

        @pl.when(k == nk - 1)
        def _():
            o_ref[...] = acc[...].astype(out_dtype)

    return pl.pallas_call(
        body, name=name, grid=(m // bm, n // bn, nk),
        in_specs=[pl.BlockSpec((tk, bm), lambda i, j, k: (k, i)), pl.BlockSpec((tk, bn), lambda i, j, k: (k, j))],
        out_specs=pl.BlockSpec((bm, bn), lambda i, j, k: (i, j)),
        out_shape=jax.ShapeDtypeStruct((m, n), out_dtype),
        scratch_shapes=[pltpu.VMEM((bm, bn), F32)],
        compiler_params=_params(("arbitrary", "arbitrary", "arbitrary")),
    )(lhs, rhs)


def _inproj_fwd(x, g, w_t, splits, tm):
    t, d = x.shape
    offs = [sum(splits[:k]) for k in range(len(splits))]

    def body(x_ref, g_ref, w_ref, hn_ref, *z_refs):
        hn = _rms_fwd(x_ref[...], g_ref[...], d).astype(BF16)
        hn_ref[...] = hn
        for z_ref, o, n in zip(z_refs, offs, splits):
            z_ref[...] = _nt(hn, w_ref[o:o + n, :])

    row = pl.BlockSpec((tm, d), lambda i: (i, 0))
    return pl.pallas_call(
        body, name="inproj_fwd", grid=(t // tm,),
        in_specs=[row, pl.BlockSpec((1, d), lambda i: (0, 0)), WHOLE_VMEM],
        out_specs=[row] + [pl.BlockSpec((tm, n), lambda i: (i, 0)) for n in splits],
        out_shape=[jax.ShapeDtypeStruct((t, d), BF16)] + [jax.ShapeDtypeStruct((t, n), F32) for n in splits],
        compiler_params=_params(("arbitrary",)),
    )(x, g, w_t)


def _inproj_bwd(dzs, x, g, w_t, dres, splits, tm):
    t, d = x.shape
    offs = [sum(splits[:k]) for k in range(len(splits))]
    ni = sum(splits)
    nz = len(splits)

    def body(*refs):
        dz_refs = refs[:nz]
        x_ref, g_ref, w_ref, dres_ref, dx_ref, dg_ref, dzc_ref = refs[nz:]
        dhn = jnp.zeros((tm, d), F32)
        for dz_ref, o, n in zip(dz_refs, offs, splits):
            dz = dz_ref[...].astype(BF16)
            dzc_ref[:, o:o + n] = dz
            dhn += _nn(dz, w_ref[o:o + n, :])
        dx, dg = _rms_bwd(x_ref[...], g_ref[...], dhn, d)
        dx_ref[...] = dres_ref[...] + dx

        @pl.when(pl.program_id(0) == 0)
        def _():
            dg_ref[...] = jnp.zeros_like(dg_ref)

        dg_ref[...] += dg

    row = pl.BlockSpec((tm, d), lambda i: (i, 0))
    vec = pl.BlockSpec((1, d), lambda i: (0, 0))
    return pl.pallas_call(
        body, name="inproj_bwd", grid=(t // tm,),
        in_specs=[pl.BlockSpec((tm, n), lambda i: (i, 0)) for n in splits] + [row, vec, WHOLE_VMEM, row],
        out_specs=[row, vec, pl.BlockSpec((tm, ni), lambda i: (i, 0))],
        out_shape=[jax.ShapeDtypeStruct((t, d), F32), jax.ShapeDtypeStruct((1, d), F32),
                   jax.ShapeDtypeStruct((t, ni), BF16)],
        compiler_params=_params(("arbitrary",)),
    )(*dzs, x, g, w_t, dres)


def _tril_bf16(ws_ref, grp):
    rows = lax.broadcasted_iota(jnp.int32, (CHUNK, CHUNK), 0)
    cols = lax.broadcasted_iota(jnp.int32, (CHUNK, CHUNK), 1)
    return jnp.where(rows >= cols, ws_ref[grp], 0.0).astype(BF16)


def _gmlp_mix(zuv_ref, gv_ref, ws_ref, bias_ref, v_scr, mixed_scr, tm, w, groups):
    u = _gelu(zuv_ref[:, 0:w])
    v0 = _gelu(zuv_ref[:, w:2 * w])
    v_scr[...] = _rms_fwd(v0, gv_ref[...], w).astype(BF16)
    for grp in range(groups):
        wsm = _tril_bf16(ws_ref, grp)
        lanes = slice(grp * CHUNK, (grp + 1) * CHUNK)
        for c in range(tm // CHUNK):
            rows = slice(c * CHUNK, (c + 1) * CHUNK)
            mixed_scr[rows, lanes] = _nn(wsm, v_scr[rows, lanes]) + bias_ref[:, lanes]
    return u, v0


def _gmlp_fwd(zuv, gv, ws, bias, gout, tm):
    t, w2 = zuv.shape
    w = w2 // 2
    groups = ws.shape[0]

    def body(zuv_ref, gv_ref, ws_ref, bias_ref, go_ref, y_ref, v_scr, mixed_scr):
        u, _ = _gmlp_mix(zuv_ref, gv_ref, ws_ref, bias_ref, v_scr, mixed_scr, tm, w, groups)
        ya = u * mixed_scr[...]
        for grp in range(groups):
            lanes = slice(grp * CHUNK, (grp + 1) * CHUNK)
            y_ref[:, lanes] = _rms_fwd(ya[:, lanes], go_ref[:, lanes], CHUNK).astype(BF16)

    const2 = lambda i: (0, 0)
    return pl.pallas_call(
        body, name="gmlp_fwd", grid=(t // tm,),
        in_specs=[pl.BlockSpec((tm, w2), lambda i: (i, 0)), pl.BlockSpec((1, w), const2),
                  pl.BlockSpec((groups, CHUNK, CHUNK), lambda i: (0, 0, 0)),
                  pl.BlockSpec((CHUNK, w), const2), pl.BlockSpec((1, w), const2)],
        out_specs=pl.BlockSpec((tm, w), lambda i: (i, 0)),
        out_shape=jax.ShapeDtypeStruct((t, w), BF16),
        scratch_shapes=[pltpu.VMEM((tm, w), BF16), pltpu.VMEM((tm, w), F32)],
        compiler_params=_params(("arbitrary",)),
    )(zuv, gv, ws, bias, gout)


def _gmlp_bwd(dy, zuv, gv, ws, bias, gout, tm):
    t, w2 = zuv.shape
    w = w2 // 2
    groups = ws.shape[0]

    def body(dy_ref, zuv_ref, gv_ref, ws_ref, bias_ref, go_ref,
             dz_ref, dws_ref, dbias_ref, dgv_ref, dgo_ref, v_scr, mixed_scr, dmix_scr, dv_scr):
        @pl.when(pl.program_id(0) == 0)
        def _():
            dws_ref[...] = jnp.zeros_like(dws_ref)
            dbias_ref[...] = jnp.zeros_like(dbias_ref)
            dgv_ref[...] = jnp.zeros_like(dgv_ref)
            dgo_ref[...] = jnp.zeros_like(dgo_ref)

        u, v0 = _gmlp_mix(zuv_ref, gv_ref, ws_ref, bias_ref, v_scr, mixed_scr, tm, w, groups)
        mixed = mixed_scr[...]
        ya = u * mixed
        for grp in range(groups):
            lanes = slice(grp * CHUNK, (grp + 1) * CHUNK)
            dya, dgo = _rms_bwd(ya[:, lanes], go_ref[:, lanes], dy_ref[:, lanes], CHUNK)
            dgo_ref[:, lanes] += dgo
            dz_ref[:, lanes] = dya * mixed[:, lanes] * _gelu_grad(zuv_ref[:, lanes])
            dmix_scr[:, lanes] = dya * u[:, lanes]
        for grp in range(groups):
            wsm = _tril_bf16(ws_ref, grp)
            lanes = slice(grp * CHUNK, (grp + 1) * CHUNK)
            dws = jnp.zeros((CHUNK, CHUNK), F32)
            dbias = jnp.zeros((CHUNK, CHUNK), F32)
            for c in range(tm // CHUNK):
                rows = slice(c * CHUNK, (c + 1) * CHUNK)
                dm = dmix_scr[rows, lanes]
                dmb = dm.astype(BF16)
                dv_scr[rows, lanes] = _tn(wsm, dmb)
                dws += _nt(dmb, v_scr[rows, lanes])
                dbias += dm
            rr = lax.broadcasted_iota(jnp.int32, (CHUNK, CHUNK), 0)
            cc = lax.broadcasted_iota(jnp.int32, (CHUNK, CHUNK), 1)
            dws_ref[grp] += jnp.where(rr >= cc, dws, 0.0)
            dbias_ref[grp] += jnp.sum(dbias, axis=1, keepdims=True)
        dv0, dgv = _rms_bwd(v0, gv_ref[...], dv_scr[...], w)
        dgv_ref[...] += dgv
        dz_ref[:, w:2 * w] = dv0 * _gelu_grad(zuv_ref[:, w:2 * w])

    const2 = lambda i: (0, 0)
    const3 = lambda i: (0, 0, 0)
    return pl.pallas_call(
        body, name="gmlp_bwd", grid=(t // tm,),
        in_specs=[pl.BlockSpec((tm, w), lambda i: (i, 0)), pl.BlockSpec((tm, w2), lambda i: (i, 0)),
                  pl.BlockSpec((1, w), const2), pl.BlockSpec((groups, CHUNK, CHUNK), const3),
                  pl.BlockSpec((CHUNK, w), const2), pl.BlockSpec((1, w), const2)],
        out_specs=[pl.BlockSpec((tm, w2), lambda i: (i, 0)), pl.BlockSpec((groups, CHUNK, CHUNK), const3),
                   pl.BlockSpec((groups, CHUNK, 1), const3), pl.BlockSpec((1, w), const2), pl.BlockSpec((1, w), const2)],
        out_shape=[jax.ShapeDtypeStruct((t, w2), F32), jax.ShapeDtypeStruct((groups, CHUNK, CHUNK), F32),
                   jax.ShapeDtypeStruct((groups, CHUNK, 1), F32), jax.ShapeDtypeStruct((1, w), F32),
                   jax.ShapeDtypeStruct((1, w), F32)],
        scratch_shapes=[pltpu.VMEM((tm, w), BF16), pltpu.VMEM((tm, w), F32),
                        pltpu.VMEM((tm, w), F32), pltpu.VMEM((tm, w), F32)],
        compiler_params=_params(("arbitrary",)),
    )(dy, zuv, gv, ws, bias, gout)


def _rot(x, m_lo, m_hi):
    return pltpu.roll(x, LANE - ROPE // 2, 1) * m_lo + pltpu.roll(x, ROPE // 2, 1) * m_hi


def _rope_tables(pos_ref, freq_ref):
    ang = pos_ref[...] * freq_ref[...]
    return jnp.cos(ang), jnp.sin(ang)


def _mla_proj_fwd(cq, ckv, krw, pos, freq, masks, gq, gkv, wq_t, wkv_t, gqh, gkh, tm):
    t, rq = cq.shape
    rkv = ckv.shape[1]
    heads = wq_t.shape[0]

    def body(cq_ref, ckv_ref, kr_ref, pos_ref, freq_ref, mk_ref, gq_ref, gkv_ref, wq_ref, wkv_ref,
             gqh_ref, gkh_ref, q_ref, k_ref, v_ref):
        cos, sin = _rope_tables(pos_ref, freq_ref)
        m_lo, m_hi = mk_ref[0:1, :], mk_ref[1:2, :]
        cqn = _rms_fwd(cq_ref[...], gq_ref[...], rq).astype(BF16)
        ckvn = _rms_fwd(ckv_ref[...], gkv_ref[...], rkv).astype(BF16)
        kr = kr_ref[...]
        kr_ss = jnp.sum(kr * kr, axis=-1, keepdims=True)
        for h in range(heads):
            qh = _nt(cqn, wq_ref[h])
            qn = qh * _rstd(qh, QK) * gqh_ref[...]
            qr = qn[:, LANE:]
            q_ref[h, :, 0:LANE] = qn[:, 0:LANE].astype(BF16)
            q_ref[h, :, LANE:] = (qr * cos + _rot(qr, m_lo, m_hi) * sin).astype(BF16)
            kvh = _nt(ckvn, wkv_ref[h])
            kn = kvh[:, 0:LANE]
            rk = lax.rsqrt((jnp.sum(kn * kn, axis=-1, keepdims=True) + kr_ss) * (1.0 / QK) + EPS)
            k_ref[h, :, 0:LANE] = (kn * rk * gkh_ref[:, 0:LANE]).astype(BF16)
            krn = kr * rk * gkh_ref[:, LANE:]
            k_ref[h, :, LANE:] = (krn * cos + _rot(krn, m_lo, m_hi) * sin).astype(BF16)
            v_ref[h] = kvh[:, LANE:].astype(BF16)

    c2 = lambda i: (0, 0)
    c3 = lambda i: (0, 0, 0)
    return pl.pallas_call(
        body, name="mla_proj_fwd", grid=(t // tm,),
        in_specs=[pl.BlockSpec((tm, rq), lambda i: (i, 0)), pl.BlockSpec((tm, rkv), lambda i: (i, 0)),
                  pl.BlockSpec((tm, LANE), lambda i: (i, 0)), pl.BlockSpec((tm, 1), lambda i: (i, 0)),
                  pl.BlockSpec((1, LANE), c2), pl.BlockSpec((2, LANE), c2),
                  pl.BlockSpec((1, rq), c2), pl.BlockSpec((1, rkv), c2),
                  pl.BlockSpec((heads, HEADW, rq), c3), pl.BlockSpec((heads, HEADW, rkv), c3),
                  pl.BlockSpec((1, HEADW), c2), pl.BlockSpec((1, HEADW), c2)],
        out_specs=[pl.BlockSpec((heads, tm, HEADW), lambda i: (0, i, 0)),
                   pl.BlockSpec((heads, tm, HEADW), lambda i: (0, i, 0)),
                   pl.BlockSpec((heads, tm, VHEAD), lambda i: (0, i, 0))],
        out_shape=[jax.ShapeDtypeStruct((heads, t, HEADW), BF16), jax.ShapeDtypeStruct((heads, t, HEADW), BF16),
                   jax.ShapeDtypeStruct((heads, t, VHEAD), BF16)],
        compiler_params=_params(("arbitrary",)),
    )(cq, ckv, krw, pos, freq, masks, gq, gkv, wq_t, wkv_t, gqh, gkh)


def _mla_proj_bwd(dq, dk, dv, cq, ckv, krw, pos, freq, masks, gq, gkv, wq_t, wkv_t, gqh, gkh, tm):
    t, rq = cq.shape
    rkv = ckv.shape[1]
    heads = wq_t.shape[0]

    def body(dq_ref, dk_ref, dv_ref, cq_ref, ckv_ref, kr_ref, pos_ref, freq_ref, mk_ref, gq_ref, gkv_ref,
             wq_ref, wkv_ref, gqh_ref, gkh_ref,
             dcq_ref, dckv_ref, dkr_ref, dwq_ref, dwkv_ref, dgq_ref, dgkv_ref, dgqh_ref, dgkh_ref):
        @pl.when(pl.program_id(0) == 0)
        def _():
            for r in (dwq_ref, dwkv_ref, dgq_ref, dgkv_ref, dgqh_ref, dgkh_ref):
                r[...] = jnp.zeros_like(r)

        cos, sin = _rope_tables(pos_ref, freq_ref)
        m_lo, m_hi = mk_ref[0:1, :], mk_ref[1:2, :]

        def unrope(dy):
            return dy * cos - _rot(dy * sin, m_lo, m_hi)

        cqn = _rms_fwd(cq_ref[...], gq_ref[...], rq).astype(BF16)
        ckvn = _rms_fwd(ckv_ref[...], gkv_ref[...], rkv).astype(BF16)
        kr = kr_ref[...]
        kr_ss = jnp.sum(kr * kr, axis=-1, keepdims=True)
        dcqn = jnp.zeros((tm, rq), F32)
        dckvn = jnp.zeros((tm, rkv), F32)
        dkr = jnp.zeros((tm, LANE), F32)
        for h in range(heads):
            qh = _nt(cqn, wq_ref[h])
            dqn = jnp.concatenate([dq_ref[h, :, 0:LANE], unrope(dq_ref[h, :, LANE:])], axis=1)
            dqh, dg = _rms_bwd(qh, gqh_ref[...], dqn, QK)
            dgqh_ref[...] += dg
            dqh = dqh.astype(BF16)
            dcqn += _nn(dqh, wq_ref[h])
            dwq_ref[h] += _tn(dqh, cqn)

            kvh = _nt(ckvn, wkv_ref[h])
            kn = kvh[:, 0:LANE]
            rk = lax.rsqrt((jnp.sum(kn * kn, axis=-1, keepdims=True) + kr_ss) * (1.0 / QK) + EPS)
            dkn_n = dk_ref[h, :, 0:LANE]
            dkr_n = unrope(dk_ref[h, :, LANE:])
            knh, krh = kn * rk, kr * rk
            dgkh_ref[:, 0:LANE] += jnp.sum(dkn_n * knh, axis=0, keepdims=True)
            dgkh_ref[:, LANE:] += jnp.sum(dkr_n * krh, axis=0, keepdims=True)
            dkn_g, dkr_g = dkn_n * gkh_ref[:, 0:LANE], dkr_n * gkh_ref[:, LANE:]
            proj = (jnp.sum(dkn_g * knh, axis=-1, keepdims=True)
                    + jnp.sum(dkr_g * krh, axis=-1, keepdims=True)) * (1.0 / QK)
            dkr += rk * (dkr_g - krh * proj)
            dkvh = jnp.concatenate([rk * (dkn_g - knh * proj), dv_ref[h]], axis=1).astype(BF16)
            dckvn += _nn(dkvh, wkv_ref[h])
            dwkv_ref[h] += _tn(dkvh, ckvn)
        dkr_ref[...] = dkr
        dcq, dg = _rms_bwd(cq_ref[...], gq_ref[...], dcqn, rq)
        dcq_ref[...] = dcq
        dgq_ref[...] += dg
        dckv, dg = _rms_bwd(ckv_ref[...], gkv_ref[...], dckvn, rkv)
        dckv_ref[...] = dckv
        dgkv_ref[...] += dg

    c2 = lambda i: (0, 0)
    c3 = lambda i: (0, 0, 0)
    hq = pl.BlockSpec((heads, tm, HEADW), lambda i: (0, i, 0))
    return pl.pallas_call(
        body, name="mla_proj_bwd", grid=(t // tm,),
        in_specs=[hq, hq, pl.BlockSpec((heads, tm, VHEAD), lambda i: (0, i, 0)),
                  pl.BlockSpec((tm, rq), lambda i: (i, 0)), pl.BlockSpec((tm, rkv), lambda i: (i, 0)),
                  pl.BlockSpec((tm, LANE), lambda i: (i, 0)), pl.BlockSpec((tm, 1), lambda i: (i, 0)),
                  pl.BlockSpec((1, LANE), c2), pl.BlockSpec((2, LANE), c2),
                  pl.BlockSpec((1, rq), c2), pl.BlockSpec((1, rkv), c2),
                  pl.BlockSpec((heads, HEADW, rq), c3), pl.BlockSpec((heads, HEADW, rkv), c3),
                  pl.BlockSpec((1, HEADW), c2), pl.BlockSpec((1, HEADW), c2)],
        out_specs=[pl.BlockSpec((tm, rq), lambda i: (i, 0)), pl.BlockSpec((tm, rkv), lambda i: (i, 0)),
                   pl.BlockSpec((tm, LANE), lambda i: (i, 0)),
                   pl.BlockSpec((heads, HEADW, rq), c3), pl.BlockSpec((heads, HEADW, rkv), c3),
                   pl.BlockSpec((1, rq), c2), pl.BlockSpec((1, rkv), c2),
                   pl.BlockSpec((1, HEADW), c2), pl.BlockSpec((1, HEADW), c2)],
        out_shape=[jax.ShapeDtypeStruct((t, rq), F32), jax.ShapeDtypeStruct((t, rkv), F32),
                   jax.ShapeDtypeStruct((t, LANE), F32),
                   jax.ShapeDtypeStruct((heads, HEADW, rq), F32), jax.ShapeDtypeStruct((heads, HEADW, rkv), F32),
                   jax.ShapeDtypeStruct((1, rq), F32), jax.ShapeDtypeStruct((1, rkv), F32),
                   jax.ShapeDtypeStruct((1, HEADW), F32), jax.ShapeDtypeStruct((1, HEADW), F32)],
        compiler_params=_params(("arbitrary",)),
    )(dq, dk, dv, cq, ckv, krw, pos, freq, masks, gq, gkv, wq_t, wkv_t, gqh, gkh)


def _causal_mask(qi, ki, blk):
    rows = qi * blk + lax.broadcasted_iota(jnp.int32, (blk, blk), 0)
    cols = ki * blk + lax.broadcasted_iota(jnp.int32, (blk, blk), 1)
    return rows >= cols


def _attn_fwd(q, k, v, seq, blk):
    heads, t, _ = q.shape
    scale = QK ** -0.5
    nblk = seq // blk

    def body(q_ref, k_ref, v_ref, o_ref, lse_ref):
        for qi in range(nblk):
            rows = slice(qi * blk, (qi + 1) * blk)
            qb = q_ref[0, rows, :]

            def step(ki, carry):
                m, l, acc = carry
                ks = pl.ds(pl.multiple_of(ki * blk, blk), blk)
                s = _nt(qb, k_ref[0, ks, :]) * scale
                s = jnp.where(_causal_mask(qi, ki, blk), s, -1e30)
                m_new = jnp.maximum(m, jnp.max(s, axis=-1, keepdims=True))
                p = jnp.exp(s - m_new)
                alpha = jnp.exp(m - m_new)
                l = alpha * l + jnp.sum(p, axis=-1, keepdims=True)
                acc = alpha * acc + _nn(p.astype(BF16), v_ref[0, ks, :])
                return m_new, l, acc

            init = (jnp.full((blk, 1), -1e30, F32), jnp.zeros((blk, 1), F32), jnp.zeros((blk, VHEAD), F32))
            m, l, acc = lax.fori_loop(0, qi + 1, step, init)
            o_ref[0, rows, :] = acc / l
            lse_ref[0, rows, :] = m + jnp.log(l)

    return pl.pallas_call(
        body, name="attn_fwd", grid=(heads, t // seq),
        in_specs=[pl.BlockSpec((1, seq, HEADW), lambda h, b: (h, b, 0)),
                  pl.BlockSpec((1, seq, HEADW), lambda h, b: (h, b, 0)),
                  pl.BlockSpec((1, seq, VHEAD), lambda h, b: (h, b, 0))],
        out_specs=[pl.BlockSpec((1, seq, VHEAD), lambda h, b: (h, b, 0)),
                   pl.BlockSpec((1, seq, 1), lambda h, b: (h, b, 0))],
        out_shape=[jax.ShapeDtypeStruct((heads, t, VHEAD), F32), jax.ShapeDtypeStruct((heads, t, 1), F32)],
        compiler_params=_params(("arbitrary", "arbitrary")),
    )(q, k, v)


def _attn_bwd(q, k, v, do, lse, delta, seq, blk):
    heads, t, _ = q.shape
    scale = QK ** -0.5
    nblk = seq // blk

    def body(q_ref, k_ref, v_ref, do_ref, lse_ref, dl_ref, dq_ref, dk_ref, dv_ref):
        dk_ref[...] = jnp.zeros_like(dk_ref)
        dv_ref[...] = jnp.zeros_like(dv_ref)
        for qi in range(nblk):
            rows = slice(qi * blk, (qi + 1) * blk)
            qb = q_ref[0, rows, :]
            dob = do_ref[0, rows, :]
            lse_b = lse_ref[0, rows, :]
            dl_b = dl_ref[0, rows, :]

            def step(ki, dq_acc):
                ks = pl.ds(pl.multiple_of(ki * blk, blk), blk)
                kb = k_ref[0, ks, :]
                s = _nt(qb, kb) * scale
                p = jnp.where(_causal_mask(qi, ki, blk), jnp.exp(s - lse_b), 0.0)
                dp = _nt(dob, v_ref[0, ks, :])
                ds = (p * (dp - dl_b) * scale).astype(BF16)
                dv_ref[0, ks, :] += _tn(p.astype(BF16), dob)
                dk_ref[0, ks, :] += _tn(ds, qb)
                return dq_acc + _nn(ds, kb)

            dq_ref[0, rows, :] = lax.fori_loop(0, qi + 1, step, jnp.zeros((blk, HEADW), F32))

    hq = pl.BlockSpec((1, seq, HEADW), lambda h, b: (h, b, 0))
    hv = pl.BlockSpec((1, seq, VHEAD), lambda h, b: (h, b, 0))
    h1 = pl.BlockSpec((1, seq, 1), lambda h, b: (h, b, 0))
    return pl.pallas_call(
        body, name="attn_bwd", grid=(heads, t // seq),
        in_specs=[hq, hq, hv, hv, h1, h1],
        out_specs=[hq, hq, hv],
        out_shape=[jax.ShapeDtypeStruct((heads, t, HEADW), F32), jax.ShapeDtypeStruct((heads, t, HEADW), F32),
                   jax.ShapeDtypeStruct((heads, t, VHEAD), F32)],
        compiler_params=_params(("arbitrary", "arbitrary")),
    )(q, k, v, do, lse, delta)


def _out_fwd(ya, o, gb, w_out, x1, tm):
    t, w = ya.shape
    heads = o.shape[0]
    d = x1.shape[1]

    def body(ya_ref, o_ref, gb_ref, w_ref, x1_ref, x2_ref, yc_ref):
        yc_ref[:, 0:w] = ya_ref[...]
        for h in range(heads):
            lanes = slice(h * VHEAD, (h + 1) * VHEAD)
            yc_ref[:, w + h * VHEAD:w + (h + 1) * VHEAD] = _rms_fwd(o_ref[h], gb_ref[:, lanes], VHEAD).astype(BF16)
        x2_ref[...] = x1_ref[...] + _nn(yc_ref[...], w_ref[...])

    wy = w + heads * VHEAD
    row = pl.BlockSpec((tm, d), lambda i: (i, 0))
    return pl.pallas_call(
        body, name="out_fwd", grid=(t // tm,),
        in_specs=[pl.BlockSpec((tm, w), lambda i: (i, 0)), pl.BlockSpec((heads, tm, VHEAD), lambda i: (0, i, 0)),
                  pl.BlockSpec((1, heads * VHEAD), lambda i: (0, 0)), WHOLE_VMEM, row],
        out_specs=[row, pl.BlockSpec((tm, wy), lambda i: (i, 0))],
        out_shape=[jax.ShapeDtypeStruct((t, d), F32), jax.ShapeDtypeStruct((t, wy), BF16)],
        compiler_params=_params(("arbitrary",)),
    )(ya, o, gb, w_out, x1)


def _out_bwd(dx2, o, gb, w_out, w, tm):
    t, d = dx2.shape
    heads = o.shape[0]

    def body(dx_ref, o_ref, gb_ref, w_ref, dya_ref, do_ref, dl_ref, dgb_ref):
        @pl.when(pl.program_id(0) == 0)
        def _():
            dgb_ref[...] = jnp.zeros_like(dgb_ref)

        dyc = _nt(dx_ref[...].astype(BF16), w_ref[...])
        dya_ref[...] = dyc[:, 0:w]
        for h in range(heads):
            lanes = slice(h * VHEAD, (h + 1) * VHEAD)
            oh = o_ref[h]
            doh, dg = _rms_bwd(oh, gb_ref[:, lanes], dyc[:, w + h * VHEAD:w + (h + 1) * VHEAD], VHEAD)
            dgb_ref[:, lanes] += dg
            do_ref[h] = doh.astype(BF16)
            dl_ref[h] = jnp.sum(doh * oh, axis=-1, keepdims=True)

    ho = pl.BlockSpec((heads, tm, VHEAD), lambda i: (0, i, 0))
    vec = pl.BlockSpec((1, heads * VHEAD), lambda i: (0, 0))
    return pl.pallas_call(
        body, name="out_bwd", grid=(t // tm,),
        in_specs=[pl.BlockSpec((tm, d), lambda i: (i, 0)), ho, vec, WHOLE_VMEM],
        out_specs=[pl.BlockSpec((tm, w), lambda i: (i, 0)), ho, pl.BlockSpec((heads, tm, 1), lambda i: (0, i, 0)), vec],
        out_shape=[jax.ShapeDtypeStruct((t, w), F32), jax.ShapeDtypeStruct((heads, t, VHEAD), BF16),
                   jax.ShapeDtypeStruct((heads, t, 1), F32), jax.ShapeDtypeStruct((1, heads * VHEAD), F32)],
        compiler_params=_params(("arbitrary",)),
    )(dx2, o, gb, w_out)


def _loss_head(y, target, tm):
    t, d = y.shape

    def body(y_ref, t_ref, dy_ref, loss_ref):
        @pl.when(pl.program_id(0) == 0)
        def _():
            loss_ref[...] = jnp.zeros_like(loss_ref)

        err = y_ref[...] - t_ref[...]
        dy_ref[...] = err * (1.0 / d)
        part = jnp.sum(jnp.sum(err * err, axis=-1, keepdims=True) * (1.0 / d), axis=0, keepdims=True)
        loss_ref[...] += 0.5 * part

    row = pl.BlockSpec((tm, d), lambda i: (i, 0))
    return pl.pallas_call(
        body, name="loss_head", grid=(t // tm,),
        in_specs=[row, row], out_specs=[row, pl.BlockSpec((1, 1), lambda i: (0, 0))],
        out_shape=[jax.ShapeDtypeStruct((t, d), F32), jax.ShapeDtypeStruct((1, 1), F32)],
        compiler_params=_params(("arbitrary",)),
    )(y, target)


def _place():
    return lax.axis_index("x"), lax.axis_index("y"), lax.axis_index("c")


def _all_gather(arrs, name):
    n = len(arrs)

    def body(*refs):
        ins, outs = refs[:n], refs[n:2 * n]
        send_sems, recv_sems, local_sems = refs[2 * n:]
        x, y, c = _place()
        sibling = (x, y, 1 - c)
        chips = [(1 - x, y), (x, 1 - y), (1 - x, 1 - y)]

        def blk(a, px, py, pc):
            return outs[a].at[4 * px + 2 * py + pc]

        def copy(a, k, block, to, src=None):
            return pltpu.make_async_remote_copy(
                src_ref=blk(a, *block) if src is None else src, dst_ref=blk(a, *block),
                send_sem=send_sems.at[a, k], recv_sem=recv_sems.at[a, k], device_id=to, device_id_type=MESH)

        mine = [pltpu.make_async_copy(ins[a], blk(a, x, y, c), local_sems.at[a]) for a in range(n)]
        for cp in mine:
            cp.start()
        first = []
        for a in range(n):
            first.append(copy(a, 0, (x, y, c), sibling, src=ins[a]))
            first += [copy(a, 1 + j, (x, y, c), (*chip, c), src=ins[a]) for j, chip in enumerate(chips)]
        for cp in first:
            cp.start()
        passed = []
        for j, chip in enumerate(chips):
            for a in range(n):
                copy(a, 1 + j, (*chip, c), (x, y, c)).wait_recv()
                cp = copy(a, 4 + j, (*chip, c), sibling)
                cp.start()
                passed.append(cp)
        for a in range(n):
            copy(a, 0, sibling, (x, y, c)).wait_recv()
            for j, chip in enumerate(chips):
                copy(a, 4 + j, (*chip, 1 - c), (x, y, c)).wait_recv()
        for cp in first + passed:
            cp.wait_send()
        for cp in mine:
            cp.wait()

    return pl.pallas_call(
        body, name=name,
        in_specs=[ANY] * n, out_specs=[ANY] * n,
        out_shape=[jax.ShapeDtypeStruct((N_DEV,) + a.shape, a.dtype) for a in arrs],
        scratch_shapes=[pltpu.SemaphoreType.DMA((n, 7)), pltpu.SemaphoreType.DMA((n, 7)),
                        pltpu.SemaphoreType.DMA((n,))],
    )(*arrs)


def _sibling_exchange(arrs, name):
    n = len(arrs)

    def body(*refs):
        ins, outs = refs[:n], refs[n:2 * n]
        send_sems, recv_sems = refs[2 * n:]
        x, y, c = _place()
        copies = []
        for a in range(n):
            for s in range(4):
                copies.append(pltpu.make_async_remote_copy(
                    src_ref=ins[a].at[2 * s + (1 - c)], dst_ref=outs[a].at[s],
                    send_sem=send_sems.at[a, s], recv_sem=recv_sems.at[a, s],
                    device_id=(x, y, 1 - c), device_id_type=MESH))
        for cp in copies:
            cp.start()
        for cp in copies:
            cp.wait()

    return pl.pallas_call(
        body, name=name,
        in_specs=[ANY] * n, out_specs=[ANY] * n,
        out_shape=[jax.ShapeDtypeStruct((4,) + a.shape[1:], a.dtype) for a in arrs],
        scratch_shapes=[pltpu.SemaphoreType.DMA((n, 4)), pltpu.SemaphoreType.DMA((n, 4))],
    )(*arrs)


def _chip_exchange(arrs, name):
    n = len(arrs)

    def body(*refs):
        ins, outs = refs[:n], refs[n:2 * n]
        send_sems, recv_sems = refs[2 * n:]
        x, y, c = _place()
        chips = [(1 - x, y), (x, 1 - y), (1 - x, 1 - y)]
        copies = []
        for a in range(n):
            for k, (px, py) in enumerate(chips):
                copies.append(pltpu.make_async_remote_copy(
                    src_ref=ins[a].at[2 * px + py], dst_ref=outs[a].at[k],
                    send_sem=send_sems.at[a, k], recv_sem=recv_sems.at[a, k],
                    device_id=(px, py, c), device_id_type=MESH))
        for cp in copies:
            cp.start()
        for cp in copies:
            cp.wait()

    return pl.pallas_call(
        body, name=name,
        in_specs=[ANY] * n, out_specs=[ANY] * n,
        out_shape=[jax.ShapeDtypeStruct((3,) + a.shape[1:], a.dtype) for a in arrs],
        scratch_shapes=[pltpu.SemaphoreType.DMA((n, 3)), pltpu.SemaphoreType.DMA((n, 3))],
    )(*arrs)


def _pair_add(full, got, core, name):
    _, r, cdim = full.shape
    br = _row_block(r, 512)

    def body(c_ref, f_ref, g_ref, o_ref):
        o_ref[...] = (f_ref[...].astype(F32) + g_ref[...].astype(F32)).astype(o_ref.dtype)

    return pl.pallas_call(
        body, name=name,
        grid_spec=pltpu.PrefetchScalarGridSpec(
            num_scalar_prefetch=1, grid=(4, r // br),
            in_specs=[pl.BlockSpec((1, br, cdim), lambda s, i, c_ref: (2 * s + c_ref[0], i, 0)),
                      pl.BlockSpec((1, br, cdim), lambda s, i, c_ref: (s, i, 0))],
            out_specs=pl.BlockSpec((1, br, cdim), lambda s, i, c_ref: (s, i, 0))),
        out_shape=jax.ShapeDtypeStruct((4, r, cdim), full.dtype),
        compiler_params=_params(("arbitrary", "arbitrary")),
    )(core, full, got)


def _sum_owned(chip, got, slot, rows, name):
    cdim = chip.shape[2]
    br = _row_block(rows, 512)

    def body(s_ref, c_ref, g_ref, o_ref):
        acc = c_ref[0].astype(F32)
        for k in range(3):
            acc = acc + g_ref[k].astype(F32)
        o_ref[...] = acc

    return pl.pallas_call(
        body, name=name,
        grid_spec=pltpu.PrefetchScalarGridSpec(
            num_scalar_prefetch=1, grid=(rows // br,),
            in_specs=[pl.BlockSpec((1, br, cdim), lambda i, s_ref: (s_ref[0], i, 0)),
                      pl.BlockSpec((3, br, cdim), lambda i, s_ref: (0, i, 0))],
            out_specs=pl.BlockSpec((br, cdim), lambda i, s_ref: (i, 0))),
        out_shape=jax.ShapeDtypeStruct((rows, cdim), F32),
        compiler_params=_params(("arbitrary",)),
    )(slot, chip, got)


def _sum_devices(stack):
    _, r, cdim = stack.shape

    def body(s_ref, o_ref):
        acc = s_ref[0]
        for k in range(1, N_DEV):
            acc = acc + s_ref[k]
        o_ref[...] = acc

    return pl.pallas_call(
        body, name="sum_devices", out_shape=jax.ShapeDtypeStruct((r, cdim), F32),
        compiler_params=_params(),
    )(stack)


def _adamw(w, g, m, v, name):
    r, cdim = w.shape
    br = _row_block(r, 256)

    def body(w_ref, g_ref, m_ref, v_ref, d_ref, nm_ref, nv_ref):
        g = g_ref[...]
        nm = ADAM_B1 * m_ref[...] + (1.0 - ADAM_B1) * g
        nv = ADAM_B2 * v_ref[...] + (1.0 - ADAM_B2) * (g * g)
        m_hat = nm / (1.0 - ADAM_B1 ** ADAM_STEP)
        v_hat = nv / (1.0 - ADAM_B2 ** ADAM_STEP)
        d_ref[...] = -ADAM_LR * (m_hat / (jnp.sqrt(v_hat) + ADAM_EPS) + ADAM_WD * w_ref[...])
        nm_ref[...] = nm
        nv_ref[...] = nv

    spec = pl.BlockSpec((br, cdim), lambda i: (i, 0))
    shape = jax.ShapeDtypeStruct((r, cdim), F32)
    return pl.pallas_call(
        body, name=name, grid=(r // br,), in_specs=[spec] * 4, out_specs=[spec] * 3,
        out_shape=[shape] * 3, compiler_params=_params(("arbitrary",)),
    )(w, g, m, v)


WEIGHTS = ("ffn1_norm_g", "ffn1_w_gate", "ffn1_w_up", "ffn1_w_down", "mix_norm_g", "w_in", "gmlp_v_norm_g",
           "gmlp_w_s", "gmlp_b_s", "mla_q_norm_g", "mla_w_q_up", "mla_kv_norm_g", "mla_w_kv_up", "mla_q_head_g",
           "mla_k_head_g", "gmlp_out_g", "mla_out_g", "w_out", "ffn2_norm_g", "ffn2_w_gate", "ffn2_w_up",
           "ffn2_w_down")
SHARDED = {"ffn1_w_gate": True, "ffn1_w_up": True, "ffn1_w_down": False, "w_in": True, "mla_w_q_up": True,
           "mla_w_kv_up": True, "w_out": False, "ffn2_w_gate": True, "ffn2_w_up": True, "ffn2_w_down": False}


def _col_block(m, target):
    best = LANE
    for cand in range(LANE, min(m, target) + 1, LANE):
        if m % cand == 0:
            best = cand
    return best


def _shard_rows(w, transposed, pad_to=None):
    rows = (w[0].T if transposed else w[0]).astype(BF16)
    if pad_to is not None and pad_to != rows.shape[0]:
        rows = jnp.pad(rows, ((0, pad_to - rows.shape[0]), (0, 0)))
    return rows


def _pack(parts):
    flat = []
    for p in parts:
        f = p.reshape(-1).astype(F32)
        flat.append(jnp.pad(f, (0, _round_up(f.size, LANE) - f.size)))
    flat = jnp.concatenate(flat)
    rows = _round_up(flat.size // LANE, SUBLANE)
    return jnp.pad(flat, (0, rows * LANE - flat.size)).reshape(rows, LANE)


def _unpack(packed, shapes):
    out, row = [], 0
    for shp in shapes:
        size = 1
        for s in shp:
            size *= s
        nrows = _round_up(size, LANE) // LANE
        out.append(packed[row:row + nrows].reshape(-1)[:size].reshape(shp))
        row += nrows
    return out


def kernel(x, positions, ffn1_norm_g, ffn1_w_gate, ffn1_w_up, ffn1_w_down, mix_norm_g, w_in, gmlp_v_norm_g, gmlp_w_s, gmlp_b_s, mla_q_norm_g, mla_w_q_up, mla_kv_norm_g, mla_w_kv_up, mla_q_head_g, mla_k_head_g, gmlp_out_g, mla_out_g, w_out, ffn2_norm_g, ffn2_w_gate, ffn2_w_up, ffn2_w_down, loss_target, m_ffn1_norm_g, m_ffn1_w_gate, m_ffn1_w_up, m_ffn1_w_down, m_mix_norm_g, m_w_in, m_gmlp_v_norm_g, m_gmlp_w_s, m_gmlp_b_s, m_mla_q_norm_g, m_mla_w_q_up, m_mla_kv_norm_g, m_mla_w_kv_up, m_mla_q_head_g, m_mla_k_head_g, m_gmlp_out_g, m_mla_out_g, m_w_out, m_ffn2_norm_g, m_ffn2_w_gate, m_ffn2_w_up, m_ffn2_w_down, v_ffn1_norm_g, v_ffn1_w_gate, v_ffn1_w_up, v_ffn1_w_down, v_mix_norm_g, v_w_in, v_gmlp_v_norm_g, v_gmlp_w_s, v_gmlp_b_s, v_mla_q_norm_g, v_mla_w_q_up, v_mla_kv_norm_g, v_mla_w_kv_up, v_mla_q_head_g, v_mla_k_head_g, v_gmlp_out_g, v_mla_out_g, v_w_out, v_ffn2_norm_g, v_ffn2_w_gate, v_ffn2_w_up, v_ffn2_w_down):
    wts = dict(zip(WEIGHTS, (ffn1_norm_g, ffn1_w_gate, ffn1_w_up, ffn1_w_down, mix_norm_g, w_in, gmlp_v_norm_g, gmlp_w_s, gmlp_b_s, mla_q_norm_g, mla_w_q_up, mla_kv_norm_g, mla_w_kv_up, mla_q_head_g, mla_k_head_g, gmlp_out_g, mla_out_g, w_out, ffn2_norm_g, ffn2_w_gate, ffn2_w_up, ffn2_w_down)))
    mom1 = dict(zip(WEIGHTS, (m_ffn1_norm_g, m_ffn1_w_gate, m_ffn1_w_up, m_ffn1_w_down, m_mix_norm_g, m_w_in, m_gmlp_v_norm_g, m_gmlp_w_s, m_gmlp_b_s, m_mla_q_norm_g, m_mla_w_q_up, m_mla_kv_norm_g, m_mla_w_kv_up, m_mla_q_head_g, m_mla_k_head_g, m_gmlp_out_g, m_mla_out_g, m_w_out, m_ffn2_norm_g, m_ffn2_w_gate, m_ffn2_w_up, m_ffn2_w_down)))
    mom2 = dict(zip(WEIGHTS, (v_ffn1_norm_g, v_ffn1_w_gate, v_ffn1_w_up, v_ffn1_w_down, v_mix_norm_g, v_w_in, v_gmlp_v_norm_g, v_gmlp_w_s, v_gmlp_b_s, v_mla_q_norm_g, v_mla_w_q_up, v_mla_kv_norm_g, v_mla_w_kv_up, v_mla_q_head_g, v_mla_k_head_g, v_gmlp_out_g, v_mla_out_g, v_w_out, v_ffn2_norm_g, v_ffn2_w_gate, v_ffn2_w_up, v_ffn2_w_down)))

    b_loc, seq, d = x.shape
    t = b_loc * seq
    ffs = ffn1_w_gate.shape[2]
    fp = _round_up(ffs, LANE)
    wg = gmlp_v_norm_g.shape[1]
    groups = gmlp_w_s.shape[1]
    rq, rkv = mla_q_norm_g.shape[1], mla_kv_norm_g.shape[1]
    heads = mla_out_g.shape[1]
    assert w_in.shape[2] * N_DEV == 2 * wg + rq + rkv + ROPE and mla_w_kv_up.shape[2] * N_DEV == heads * HEADW
    tm = min(512, t)
    tm_mix = min(256, t)
    blk = min(256, seq)

    xf = x.reshape(t, d)
    target = loss_target.reshape(t, d)
    pos = positions.reshape(t, 1).astype(F32)
    half = ROPE // 2
    inv_freq = 1.0 / (ROPE_THETA ** (jnp.arange(half, dtype=F32) / half))
    freq = jnp.concatenate([inv_freq, inv_freq, jnp.zeros((LANE - ROPE,), F32)])[None, :]
    lane = jnp.arange(LANE)
    masks = jnp.stack([jnp.where(lane < half, -1.0, 0.0),
                       jnp.where((lane >= half) & (lane < ROPE), 1.0, 0.0)]).astype(F32)
    gqh = jnp.pad(mla_q_head_g, ((0, 0), (0, HEADW - QK)))
    gkh = jnp.pad(mla_k_head_g, ((0, 0), (0, HEADW - QK)))
    bias = jnp.repeat(gmlp_b_s[0].T, CHUNK, axis=1)
    gouta = gmlp_out_g.reshape(1, wg)
    goutb = mla_out_g.reshape(1, heads * VHEAD)
    ws = gmlp_w_s[0]

    order = [n for n in WEIGHTS if n in SHARDED]
    shards = [_shard_rows(wts[n], SHARDED[n], fp if n.startswith("ffn") else None) for n in order]
    full = dict(zip(order, _all_gather(shards, "gather_weights")))
    win_t = jnp.pad(full["w_in"].reshape(-1, d), ((0, LANE - ROPE), (0, 0)))
    splits = (2 * wg, rq, rkv, LANE)
    wq_t = jnp.pad(full["mla_w_q_up"].reshape(heads, QK, rq), ((0, 0), (0, HEADW - QK), (0, 0)))
    wkv_t = full["mla_w_kv_up"].reshape(heads, HEADW, rkv)
    wout = full["w_out"].reshape(-1, d)

    x1, xn1, a1, b1 = _ffn_fwd(xf, ffn1_norm_g, full["ffn1_w_gate"], full["ffn1_w_up"], full["ffn1_w_down"], tm,
                               "ffn1_fwd")
    hn, zuv, cq, ckv, krw = _inproj_fwd(x1, mix_norm_g, win_t, splits, tm)
    ya = _gmlp_fwd(zuv, gmlp_v_norm_g, ws, bias, gouta, tm_mix)
    q, k, vv = _mla_proj_fwd(cq, ckv, krw, pos, freq, masks, mla_q_norm_g, mla_kv_norm_g, wq_t, wkv_t, gqh, gkh,
                             tm_mix)
    o, lse = _attn_fwd(q, k, vv, seq, blk)
    x2, ycat = _out_fwd(ya, o, goutb, wout, x1, tm)
    x3, xn2, a2, b2 = _ffn_fwd(x2, ffn2_norm_g, full["ffn2_w_gate"], full["ffn2_w_up"], full["ffn2_w_down"], tm,
                               "ffn2_fwd")
    dx3, loss_part = _loss_head(x3, target, tm)

    tk = min(512, t)
    grads = {}
    small = {}
    dx2, small["ffn2_norm_g"], dy2, da2, db2, h2 = _ffn_bwd(
        dx3, x2, ffn2_norm_g, a2, b2, full["ffn2_w_gate"], full["ffn2_w_up"], full["ffn2_w_down"], tm_mix,
        "ffn2_bwd")
    grads["ffn2_w_gate"] = _matmul_tn(da2, xn2, fp, d, tk, BF16, "dw_ffn2_gate").reshape(N_DEV, fp, d)
    grads["ffn2_w_up"] = _matmul_tn(db2, xn2, fp, d, tk, BF16, "dw_ffn2_up").reshape(N_DEV, fp, d)
    grads["ffn2_w_down"] = _matmul_tn(h2, dy2, fp, d, tk, BF16, "dw_ffn2_down").reshape(N_DEV, fp, d)
    dya, do, delta, small["mla_out_g"] = _out_bwd(dx2, o, goutb, wout, wg, tm)
    grads["w_out"] = _matmul_tn(ycat, dx2, _col_block(ycat.shape[1], 768), d, tk, BF16, "dw_out").reshape(
        N_DEV, -1, d)
    dq, dk, dv = _attn_bwd(q, k, vv, do, lse, delta, seq, blk)
    (dcq, dckv, dkrw, dwq, dwkv, small["mla_q_norm_g"], small["mla_kv_norm_g"], dgqh, dgkh) = _mla_proj_bwd(
        dq, dk, dv, cq, ckv, krw, pos, freq, masks, mla_q_norm_g, mla_kv_norm_g, wq_t, wkv_t, gqh, gkh, tm_mix)
    small["mla_q_head_g"], small["mla_k_head_g"] = dgqh[:, :QK], dgkh[:, :QK]
    grads["mla_w_q_up"] = dwq[:, :QK].astype(BF16).reshape(N_DEV, -1, rq)
    grads["mla_w_kv_up"] = dwkv.astype(BF16).reshape(N_DEV, -1, rkv)
    dzuv, small["gmlp_w_s"], dbs, small["gmlp_v_norm_g"], small["gmlp_out_g"] = _gmlp_bwd(
        dya, zuv, gmlp_v_norm_g, ws, bias, gouta, tm_mix)
    small["gmlp_b_s"] = dbs[:, :, 0]
    dx1, small["mix_norm_g"], dzc = _inproj_bwd([dzuv, dcq, dckv, dkrw], x1, mix_norm_g, win_t, dx2, splits,
                                                tm_mix)
    dwin = _matmul_tn(dzc, hn, _col_block(dzc.shape[1], 768), d, tk, BF16, "dw_in")
    grads["w_in"] = dwin[:N_DEV * w_in.shape[2]].reshape(N_DEV, -1, d)
    dx0, small["ffn1_norm_g"], dy1, da1, db1, h1 = _ffn_bwd(
        dx1, xf, ffn1_norm_g, a1, b1, full["ffn1_w_gate"], full["ffn1_w_up"], full["ffn1_w_down"], tm_mix,
        "ffn1_bwd")
    grads["ffn1_w_gate"] = _matmul_tn(da1, xn1, fp, d, tk, BF16, "dw_ffn1_gate").reshape(N_DEV, fp, d)
    grads["ffn1_w_up"] = _matmul_tn(db1, xn1, fp, d, tk, BF16, "dw_ffn1_up").reshape(N_DEV, fp, d)
    grads["ffn1_w_down"] = _matmul_tn(h1, dy1, fp, d, tk, BF16, "dw_ffn1_down").reshape(N_DEV, fp, d)

    px, py, pc = _place()
    core = pc.astype(jnp.int32).reshape(1)
    slot = (2 * px + py).astype(jnp.int32).reshape(1)
    from_sibling = _sibling_exchange([grads[n] for n in order], "reduce_sibling")
    chip = [_pair_add(grads[n], got, core, "pair_add_" + n) for n, got in zip(order, from_sibling)]
    from_chips = _chip_exchange(chip, "reduce_chips")
    outs_g, outs_d, outs_m, outs_v = {}, {}, {}, {}
    for n, cp, got in zip(order, chip, from_chips):
        rows = wts[n].shape[2] if SHARDED[n] else wts[n].shape[1]
        g = _sum_owned(cp, got, slot, rows, "sum_owned_" + n)
        g = g.T if SHARDED[n] else g
        dlt, nm, nv = _adamw(wts[n][0], g, mom1[n][0], mom2[n][0], "adamw_" + n)
        outs_g[n], outs_d[n], outs_m[n], outs_v[n] = g[None], dlt[None], nm[None], nv[None]

    rep = [n for n in WEIGHTS if n not in SHARDED]
    packed = _pack([small[n] for n in rep] + [loss_part])
    total = _sum_devices(_all_gather([packed], "gather_small")[0])
    zero = jnp.zeros((1,), F32)
    dlt, nm, nv = _adamw(_pack([wts[n] for n in rep] + [zero]), total, _pack([mom1[n] for n in rep] + [zero]),
                         _pack([mom2[n] for n in rep] + [zero]), "adamw_small")
    shapes = [wts[n].shape for n in rep] + [(1,)]
    for n, g, dl, m1, m2 in zip(rep, _unpack(total, shapes), _unpack(dlt, shapes), _unpack(nm, shapes),
                                _unpack(nv, shapes)):
        outs_g[n], outs_d[n], outs_m[n], outs_v[n] = g, dl, m1, m2
    loss = _unpack(total, shapes)[-1].reshape(())

    return (loss, dx0.reshape(b_loc, seq, d), *[outs_g[n] for n in WEIGHTS], *[outs_d[n] for n in WEIGHTS],
            *[outs_m[n] for n in WEIGHTS], *[outs_v[n] for n in WEIGHTS])
```

```python
import functools

import jax
import jax.numpy as jnp
from jax import lax
from jax.experimental import pallas as pl
from jax.experimental.pallas import tpu as pltpu

F32 = jnp.float32
BF16 = jnp.bfloat16
EPS = 1e-6
LANE = 128
SUBLANE = 8
N_DEV = 8
VMEM_LIMIT = 60 * 1024 * 1024
NOPE = 128
ROPE = 64
VHEAD = 128
QK = NOPE + ROPE
HEADW = 2 * LANE
CHUNK = 128
ROPE_THETA = 10000.0
ADAM_LR, ADAM_B1, ADAM_B2, ADAM_EPS, ADAM_WD, ADAM_STEP = 0.001, 0.9, 0.999, 1e-08, 0.01, 10
MESH = pl.DeviceIdType.MESH
ANY = pl.BlockSpec(memory_space=pl.ANY)
WHOLE_VMEM = pl.BlockSpec(memory_space=pltpu.VMEM)


def _params(sem=None):
    return pltpu.CompilerParams(dimension_semantics=sem, vmem_limit_bytes=VMEM_LIMIT)


def _round_up(n, m):
    return -(-n // m) * m


def _row_block(rows, target):
    best = rows
    for cand in range(SUBLANE, min(rows, target) + 1, SUBLANE):
        if rows % cand == 0:
            best = cand
    return best if best <= target else rows


def _nn(a, b):
    return jnp.dot(a, b, preferred_element_type=F32)


def _nt(a, b):
    return lax.dot_general(a, b, (((1,), (1,)), ((), ())), preferred_element_type=F32)


def _tn(a, b):
    return lax.dot_general(a, b, (((0,), (0,)), ((), ())), preferred_element_type=F32)


def _rstd(x, n):
    return lax.rsqrt(jnp.sum(x * x, axis=-1, keepdims=True) * (1.0 / n) + EPS)


def _rms_fwd(x, g, n):
    return x * _rstd(x, n) * g


def _rms_bwd(x, g, dy, n):
    r = _rstd(x, n)
    xh = x * r
    dyg = dy * g
    dx = r * (dyg - xh * (jnp.sum(dyg * xh, axis=-1, keepdims=True) * (1.0 / n)))
    return dx, jnp.sum(dy * xh, axis=0, keepdims=True)


def _gelu(x):
    return 0.5 * x * (1.0 + lax.erf(x * 0.7071067811865476))


def _gelu_grad(x):
    return 0.5 * (1.0 + lax.erf(x * 0.7071067811865476)) + x * jnp.exp(-0.5 * x * x) * 0.3989422804014327


def _ffn_fwd(x, g, wg_t, wu_t, wd, tm, name):
    t, d = x.shape
    nb, fp, _ = wg_t.shape

    def body(x_ref, g_ref, wg_ref, wu_ref, wd_ref, xo_ref, xn_ref, a_ref, b_ref, acc):
        j = pl.program_id(1)

        @pl.when(j == 0)
        def _():
            xn_ref[...] = _rms_fwd(x_ref[...], g_ref[...], d).astype(BF16)
            acc[...] = jnp.zeros_like(acc)

        xn = xn_ref[...]
        a = _nt(xn, wg_ref[0]).astype(BF16)
        b = _nt(xn, wu_ref[0]).astype(BF16)
        a_ref[...] = a
        b_ref[...] = b
        a = a.astype(F32)
        h = (a * jax.nn.sigmoid(a)) * b.astype(F32)
        acc[...] += _nn(h.astype(BF16), wd_ref[0])

        @pl.when(j == nb - 1)
        def _():
            xo_ref[...] = x_ref[...] + 0.5 * acc[...]

    wspec = pl.BlockSpec((1, fp, d), lambda i, j: (j, 0, 0))
    row = pl.BlockSpec((tm, d), lambda i, j: (i, 0))
    ff = pl.BlockSpec((tm, fp), lambda i, j: (i, j))
    return pl.pallas_call(
        body, name=name, grid=(t // tm, nb),
        in_specs=[row, pl.BlockSpec((1, d), lambda i, j: (0, 0)), wspec, wspec, wspec],
        out_specs=[row, row, ff, ff],
        out_shape=[jax.ShapeDtypeStruct((t, d), F32), jax.ShapeDtypeStruct((t, d), BF16),
                   jax.ShapeDtypeStruct((t, nb * fp), BF16), jax.ShapeDtypeStruct((t, nb * fp), BF16)],
        scratch_shapes=[pltpu.VMEM((tm, d), F32)],
        compiler_params=_params(("arbitrary", "arbitrary")),
    )(x, g, wg_t, wu_t, wd)


def _ffn_bwd(dout, x, g, a, b, wg_t, wu_t, wd, tm, name):
    t, d = x.shape
    nb, fp, _ = wg_t.shape

    def body(do_ref, x_ref, g_ref, a_ref, b_ref, wg_ref, wu_ref, wd_ref,
             dx_ref, dg_ref, dy_ref, da_ref, db_ref, h_ref, acc):
        i, j = pl.program_id(0), pl.program_id(1)

        @pl.when(j == 0)
        def _():
            dy_ref[...] = (0.5 * do_ref[...]).astype(BF16)
            acc[...] = jnp.zeros_like(acc)

        @pl.when((i == 0) & (j == 0))
        def _():
            dg_ref[...] = jnp.zeros_like(dg_ref)

        dh = _nt(dy_ref[...], wd_ref[0])
        a = a_ref[...].astype(F32)
        bb = b_ref[...].astype(F32)
        s = jax.nn.sigmoid(a)
        sl = a * s
        h_ref[...] = (sl * bb).astype(BF16)
        da = (dh * bb * (s * (1.0 + a * (1.0 - s)))).astype(BF16)
        db = (dh * sl).astype(BF16)
        da_ref[...] = da
        db_ref[...] = db
        acc[...] += _nn(da, wg_ref[0]) + _nn(db, wu_ref[0])

        @pl.when(j == nb - 1)
        def _():
            dxn, dg = _rms_bwd(x_ref[...], g_ref[...], acc[...], d)
            dx_ref[...] = do_ref[...] + dxn
            dg_ref[...] += dg

    wspec = pl.BlockSpec((1, fp, d), lambda i, j: (j, 0, 0))
    row = pl.BlockSpec((tm, d), lambda i, j: (i, 0))
    vec = pl.BlockSpec((1, d), lambda i, j: (0, 0))
    ff = pl.BlockSpec((tm, fp), lambda i, j: (i, j))
    ffs = jax.ShapeDtypeStruct((t, nb * fp), BF16)
    return pl.pallas_call(
        body, name=name, grid=(t // tm, nb),
        in_specs=[row, row, vec, ff, ff, wspec, wspec, wspec],
        out_specs=[row, vec, row, ff, ff, ff],
        out_shape=[jax.ShapeDtypeStruct((t, d), F32), jax.ShapeDtypeStruct((1, d), F32),
                   jax.ShapeDtypeStruct((t, d), BF16), ffs, ffs, ffs],
        scratch_shapes=[pltpu.VMEM((tm, d), F32)],
        compiler_params=_params(("arbitrary", "arbitrary")),
    )(dout, x, g, a, b, wg_t, wu_t, wd)


def _matmul_tn(lhs, rhs, bm, bn, tk, out_dtype, name):
    t, m = lhs.shape
    n = rhs.shape[1]
    nk = t // tk

    def body(l_ref, r_ref, o_ref, acc):
        k = pl.program_id(2)

        @pl.when(k == 0)
        def _():
            acc[...] = jnp.zeros_like(acc)

        acc[...] += _tn(l_ref[...].astype(BF16), r_ref[...].astype(BF16))

        @pl.when(k == nk - 1)
        def _():
            o_ref[...] = acc[...].astype(out_dtype)

    return pl.pallas_call(
        body, name=name, grid=(m // bm, n // bn, nk),
        in_specs=[pl.BlockSpec((tk, bm), lambda i, j, k: (k, i)), pl.BlockSpec((tk, bn), lambda i, j, k: (k, j))],
        out_specs=pl.BlockSpec((bm, bn), lambda i, j, k: (i, j)),
        out_shape=jax.ShapeDtypeStruct((m, n), out_dtype),
        scratch_shapes=[pltpu.VMEM((bm, bn), F32)],
        compiler_params=_params(("arbitrary", "arbitrary", "arbitrary")),
    )(lhs, rhs)


def _inproj_fwd(x, g, w_t, splits, tm):
    t, d = x.shape
    offs = [sum(splits[:k]) for k in range(len(splits))]

    def body(x_ref, g_ref, w_ref, hn_ref, *z_refs):
        hn = _rms_fwd(x_ref[...], g_ref[...], d).astype(BF16)
        hn_ref[...] = hn
        for z_ref, o, n in zip(z_refs, offs, splits):
            z_ref[...] = _nt(hn, w_ref[o:o + n, :])

    row = pl.BlockSpec((tm, d), lambda i: (i, 0))
    return pl.pallas_call(
        body, name="inproj_fwd", grid=(t // tm,),
        in_specs=[row, pl.BlockSpec((1, d), lambda i: (0, 0)), WHOLE_VMEM],
        out_specs=[row] + [pl.BlockSpec((tm, n), lambda i: (i, 0)) for n in splits],
        out_shape=[jax.ShapeDtypeStruct((t, d), BF16)] + [jax.ShapeDtypeStruct((t, n), F32) for n in splits],
        compiler_params=_params(("arbitrary",)),
    )(x, g, w_t)


def _inproj_bwd(dzs, x, g, w_t, dres, splits, tm):
    t, d = x.shape
    offs = [sum(splits[:k]) for k in range(len(splits))]
    ni = sum(splits)
    nz = len(splits)

    def body(*refs):
        dz_refs = refs[:nz]
        x_ref, g_ref, w_ref, dres_ref, dx_ref, dg_ref, dzc_ref = refs[nz:]
        dhn = jnp.zeros((tm, d), F32)
        for dz_ref, o, n in zip(dz_refs, offs, splits):
            dz = dz_ref[...].astype(BF16)
            dzc_ref[:, o:o + n] = dz
            dhn += _nn(dz, w_ref[o:o + n, :])
        dx, dg = _rms_bwd(x_ref[...], g_ref[...], dhn, d)
        dx_ref[...] = dres_ref[...] + dx

        @pl.when(pl.program_id(0) == 0)
        def _():
            dg_ref[...] = jnp.zeros_like(dg_ref)

        dg_ref[...] += dg

    row = pl.BlockSpec((tm, d), lambda i: (i, 0))
    vec = pl.BlockSpec((1, d), lambda i: (0, 0))
    return pl.pallas_call(
        body, name="inproj_bwd", grid=(t // tm,),
        in_specs=[pl.BlockSpec((tm, n), lambda i: (i, 0)) for n in splits] + [row, vec, WHOLE_VMEM, row],
        out_specs=[row, vec, pl.BlockSpec((tm, ni), lambda i: (i, 0))],
        out_shape=[jax.ShapeDtypeStruct((t, d), F32), jax.ShapeDtypeStruct((1, d), F32),
                   jax.ShapeDtypeStruct((t, ni), BF16)],
        compiler_params=_params(("arbitrary",)),
    )(*dzs, x, g, w_t, dres)


def _tril_bf16(ws_ref, grp):
    rows = lax.broadcasted_iota(jnp.int32, (CHUNK, CHUNK), 0)
    cols = lax.broadcasted_iota(jnp.int32, (CHUNK, CHUNK), 1)
    return jnp.where(rows >= cols, ws_ref[grp], 0.0).astype(BF16)


def _gmlp_mix(zuv_ref, gv_ref, ws_ref, bias_ref, v_scr, mixed_scr, tm, w, groups):
    u = _gelu(zuv_ref[:, 0:w])
    v0 = _gelu(zuv_ref[:, w:2 * w])
    v_scr[...] = _rms_fwd(v0, gv_ref[...], w).astype(BF16)
    for grp in range(groups):
        wsm = _tril_bf16(ws_ref, grp)
        lanes = slice(grp * CHUNK, (grp + 1) * CHUNK)
        for c in range(tm // CHUNK):
            rows = slice(c * CHUNK, (c + 1) * CHUNK)
            mixed_scr[rows, lanes] = _nn(wsm, v_scr[rows, lanes]) + bias_ref[:, lanes]
    return u, v0


def _gmlp_fwd(zuv, gv, ws, bias, gout, tm):
    t, w2 = zuv.shape
    w = w2 // 2
    groups = ws.shape[0]

    def body(zuv_ref, gv_ref, ws_ref, bias_ref, go_ref, y_ref, v_scr, mixed_scr):
        u, _ = _gmlp_mix(zuv_ref, gv_ref, ws_ref, bias_ref, v_scr, mixed_scr, tm, w, groups)
        ya = u * mixed_scr[...]
        for grp in range(groups):
            lanes = slice(grp * CHUNK, (grp + 1) * CHUNK)
            y_ref[:, lanes] = _rms_fwd(ya[:, lanes], go_ref[:, lanes], CHUNK).astype(BF16)

    const2 = lambda i: (0, 0)
    return pl.pallas_call(
        body, name="gmlp_fwd", grid=(t // tm,),
        in_specs=[pl.BlockSpec((tm, w2), lambda i: (i, 0)), pl.BlockSpec((1, w), const2),
                  pl.BlockSpec((groups, CHUNK, CHUNK), lambda i: (0, 0, 0)),
                  pl.BlockSpec((CHUNK, w), const2), pl.BlockSpec((1, w), const2)],
        out_specs=pl.BlockSpec((tm, w), lambda i: (i, 0)),
        out_shape=jax.ShapeDtypeStruct((t, w), BF16),
        scratch_shapes=[pltpu.VMEM((tm, w), BF16), pltpu.VMEM((tm, w), F32)],
        compiler_params=_params(("arbitrary",)),
    )(zuv, gv, ws, bias, gout)


def _gmlp_bwd(dy, zuv, gv, ws, bias, gout, tm):
    t, w2 = zuv.shape
    w = w2 // 2
    groups = ws.shape[0]

    def body(dy_ref, zuv_ref, gv_ref, ws_ref, bias_ref, go_ref,
             dz_ref, dws_ref, dbias_ref, dgv_ref, dgo_ref, v_scr, mixed_scr, dmix_scr, dv_scr):
        @pl.when(pl.program_id(0) == 0)
        def _():
            dws_ref[...] = jnp.zeros_like(dws_ref)
            dbias_ref[...] = jnp.zeros_like(dbias_ref)
            dgv_ref[...] = jnp.zeros_like(dgv_ref)
            dgo_ref[...] = jnp.zeros_like(dgo_ref)

        u, v0 = _gmlp_mix(zuv_ref, gv_ref, ws_ref, bias_ref, v_scr, mixed_scr, tm, w, groups)
        mixed = mixed_scr[...]
        ya = u * mixed
        for grp in range(groups):
            lanes = slice(grp * CHUNK, (grp + 1) * CHUNK)
            dya, dgo = _rms_bwd(ya[:, lanes], go_ref[:, lanes], dy_ref[:, lanes], CHUNK)
            dgo_ref[:, lanes] += dgo
            dz_ref[:, lanes] = dya * mixed[:, lanes] * _gelu_grad(zuv_ref[:, lanes])
            dmix_scr[:, lanes] = dya * u[:, lanes]
        for grp in range(groups):
            wsm = _tril_bf16(ws_ref, grp)
            lanes = slice(grp * CHUNK, (grp + 1) * CHUNK)
            dws = jnp.zeros((CHUNK, CHUNK), F32)
            dbias = jnp.zeros((CHUNK, CHUNK), F32)
            for c in range(tm // CHUNK):
                rows = slice(c * CHUNK, (c + 1) * CHUNK)
                dm = dmix_scr[rows, lanes]
                dmb = dm.astype(BF16)
                dv_scr[rows, lanes] = _tn(wsm, dmb)
                dws += _nt(dmb, v_scr[rows, lanes])
                dbias += dm
            rr = lax.broadcasted_iota(jnp.int32, (CHUNK, CHUNK), 0)
            cc = lax.broadcasted_iota(jnp.int32, (CHUNK, CHUNK), 1)
            dws_ref[grp] += jnp.where(rr >= cc, dws, 0.0)
            dbias_ref[grp] += jnp.sum(dbias, axis=1, keepdims=True)
        dv0, dgv = _rms_bwd(v0, gv_ref[...], dv_scr[...], w)
        dgv_ref[...] += dgv
        dz_ref[:, w:2 * w] = dv0 * _gelu_grad(zuv_ref[:, w:2 * w])

    const2 = lambda i: (0, 0)
    const3 = lambda i: (0, 0, 0)
    return pl.pallas_call(
        body, name="gmlp_bwd", grid=(t // tm,),
        in_specs=[pl.BlockSpec((tm, w), lambda i: (i, 0)), pl.BlockSpec((tm, w2), lambda i: (i, 0)),
                  pl.BlockSpec((1, w), const2), pl.BlockSpec((groups, CHUNK, CHUNK), const3),
                  pl.BlockSpec((CHUNK, w), const2), pl.BlockSpec((1, w), const2)],
        out_specs=[pl.BlockSpec((tm, w2), lambda i: (i, 0)), pl.BlockSpec((groups, CHUNK, CHUNK), const3),
                   pl.BlockSpec((groups, CHUNK, 1), const3), pl.BlockSpec((1, w), const2), pl.BlockSpec((1, w), const2)],
        out_shape=[jax.ShapeDtypeStruct((t, w2), F32), jax.ShapeDtypeStruct((groups, CHUNK, CHUNK), F32),
                   jax.ShapeDtypeStruct((groups, CHUNK, 1), F32), jax.ShapeDtypeStruct((1, w), F32),
                   jax.ShapeDtypeStruct((1, w), F32)],
        scratch_shapes=[pltpu.VMEM((tm, w), BF16), pltpu.VMEM((tm, w), F32),
                        pltpu.VMEM((tm, w), F32), pltpu.VMEM((tm, w), F32)],
        compiler_params=_params(("arbitrary",)),
    )(dy, zuv, gv, ws, bias, gout)


def _rot(x, m_lo, m_hi):
    return pltpu.roll(x, LANE - ROPE // 2, 1) * m_lo + pltpu.roll(x, ROPE // 2, 1) * m_hi


def _rope_tables(pos_ref, freq_ref):
    ang = pos_ref[...] * freq_ref[...]
    return jnp.cos(ang), jnp.sin(ang)


def _mla_proj_fwd(cq, ckv, krw, pos, freq, masks, gq, gkv, wq_t, wkv_t, gqh, gkh, tm):
    t, rq = cq.shape
    rkv = ckv.shape[1]
    heads = wq_t.shape[0]

    def body(cq_ref, ckv_ref, kr_ref, pos_ref, freq_ref, mk_ref, gq_ref, gkv_ref, wq_ref, wkv_ref,
             gqh_ref, gkh_ref, q_ref, k_ref, v_ref):
        cos, sin = _rope_tables(pos_ref, freq_ref)
        m_lo, m_hi = mk_ref[0:1, :], mk_ref[1:2, :]
        cqn = _rms_fwd(cq_ref[...], gq_ref[...], rq).astype(BF16)
        ckvn = _rms_fwd(ckv_ref[...], gkv_ref[...], rkv).astype(BF16)
        kr = kr_ref[...]
        kr_ss = jnp.sum(kr * kr, axis=-1, keepdims=True)
        for h in range(heads):
            qh = _nt(cqn, wq_ref[h])
            qn = qh * _rstd(qh, QK) * gqh_ref[...]
            qr = qn[:, LANE:]
            q_ref[h, :, 0:LANE] = qn[:, 0:LANE].astype(BF16)
            q_ref[h, :, LANE:] = (qr * cos + _rot(qr, m_lo, m_hi) * sin).astype(BF16)
            kvh = _nt(ckvn, wkv_ref[h])
            kn = kvh[:, 0:LANE]
            rk = lax.rsqrt((jnp.sum(kn * kn, axis=-1, keepdims=True) + kr_ss) * (1.0 / QK) + EPS)
            k_ref[h, :, 0:LANE] = (kn * rk * gkh_ref[:, 0:LANE]).astype(BF16)
            krn = kr * rk * gkh_ref[:, LANE:]
            k_ref[h, :, LANE:] = (krn * cos + _rot(krn, m_lo, m_hi) * sin).astype(BF16)
            v_ref[h] = kvh[:, LANE:].astype(BF16)

    c2 = lambda i: (0, 0)
    c3 = lambda i: (0, 0, 0)
    return pl.pallas_call(
        body, name="mla_proj_fwd", grid=(t // tm,),
        in_specs=[pl.BlockSpec((tm, rq), lambda i: (i, 0)), pl.BlockSpec((tm, rkv), lambda i: (i, 0)),
                  pl.BlockSpec((tm, LANE), lambda i: (i, 0)), pl.BlockSpec((tm, 1), lambda i: (i, 0)),
                  pl.BlockSpec((1, LANE), c2), pl.BlockSpec((2, LANE), c2),
                  pl.BlockSpec((1, rq), c2), pl.BlockSpec((1, rkv), c2),
                  pl.BlockSpec((heads, HEADW, rq), c3), pl.BlockSpec((heads, HEADW, rkv), c3),
                  pl.BlockSpec((1, HEADW), c2), pl.BlockSpec((1, HEADW), c2)],
        out_specs=[pl.BlockSpec((heads, tm, HEADW), lambda i: (0, i, 0)),
                   pl.BlockSpec((heads, tm, HEADW), lambda i: (0, i, 0)),
                   pl.BlockSpec((heads, tm, VHEAD), lambda i: (0, i, 0))],
        out_shape=[jax.ShapeDtypeStruct((heads, t, HEADW), BF16), jax.ShapeDtypeStruct((heads, t, HEADW), BF16),
                   jax.ShapeDtypeStruct((heads, t, VHEAD), BF16)],
        compiler_params=_params(("arbitrary",)),
    )(cq, ckv, krw, pos, freq, masks, gq, gkv, wq_t, wkv_t, gqh, gkh)


def _mla_proj_bwd(dq, dk, dv, cq, ckv, krw, pos, freq, masks, gq, gkv, wq_t, wkv_t, gqh, gkh, tm):
    t, rq = cq.shape
    rkv = ckv.shape[1]
    heads = wq_t.shape[0]

    def body(dq_ref, dk_ref, dv_ref, cq_ref, ckv_ref, kr_ref, pos_ref, freq_ref, mk_ref, gq_ref, gkv_ref,
             wq_ref, wkv_ref, gqh_ref, gkh_ref,
             dcq_ref, dckv_ref, dkr_ref, dwq_ref, dwkv_ref, dgq_ref, dgkv_ref, dgqh_ref, dgkh_ref):
        @pl.when(pl.program_id(0) == 0)
        def _():
            for r in (dwq_ref, dwkv_ref, dgq_ref, dgkv_ref, dgqh_ref, dgkh_ref):
                r[...] = jnp.zeros_like(r)

        cos, sin = _rope_tables(pos_ref, freq_ref)
        m_lo, m_hi = mk_ref[0:1, :], mk_ref[1:2, :]

        def unrope(dy):
            return dy * cos - _rot(dy * sin, m_lo, m_hi)

        cqn = _rms_fwd(cq_ref[...], gq_ref[...], rq).astype(BF16)
        ckvn = _rms_fwd(ckv_ref[...], gkv_ref[...], rkv).astype(BF16)
        kr = kr_ref[...]
        kr_ss = jnp.sum(kr * kr, axis=-1, keepdims=True)
        dcqn = jnp.zeros((tm, rq), F32)
        dckvn = jnp.zeros((tm, rkv), F32)
        dkr = jnp.zeros((tm, LANE), F32)
        for h in range(heads):
            qh = _nt(cqn, wq_ref[h])
            dqn = jnp.concatenate([dq_ref[h, :, 0:LANE], unrope(dq_ref[h, :, LANE:])], axis=1)
            dqh, dg = _rms_bwd(qh, gqh_ref[...], dqn, QK)
            dgqh_ref[...] += dg
            dqh = dqh.astype(BF16)
            dcqn += _nn(dqh, wq_ref[h])
            dwq_ref[h] += _tn(dqh, cqn)

            kvh = _nt(ckvn, wkv_ref[h])
            kn = kvh[:, 0:LANE]
            rk = lax.rsqrt((jnp.sum(kn * kn, axis=-1, keepdims=True) + kr_ss) * (1.0 / QK) + EPS)
            dkn_n = dk_ref[h, :, 0:LANE]
            dkr_n = unrope(dk_ref[h, :, LANE:])
            knh, krh = kn * rk, kr * rk
            dgkh_ref[:, 0:LANE] += jnp.sum(dkn_n * knh, axis=0, keepdims=True)
            dgkh_ref[:, LANE:] += jnp.sum(dkr_n * krh, axis=0, keepdims=True)
            dkn_g, dkr_g = dkn_n * gkh_ref[:, 0:LANE], dkr_n * gkh_ref[:, LANE:]
            proj = (jnp.sum(dkn_g * knh, axis=-1, keepdims=True)
                    + jnp.sum(dkr_g * krh, axis=-1, keepdims=True)) * (1.0 / QK)
            dkr += rk * (dkr_g - krh * proj)
            dkvh = jnp.concatenate([rk * (dkn_g - knh * proj), dv_ref[h]], axis=1).astype(BF16)
            dckvn += _nn(dkvh, wkv_ref[h])
            dwkv_ref[h] += _tn(dkvh, ckvn)
        dkr_ref[...] = dkr
        dcq, dg = _rms_bwd(cq_ref[...], gq_ref[...], dcqn, rq)
        dcq_ref[...] = dcq
        dgq_ref[...] += dg
        dckv, dg = _rms_bwd(ckv_ref[...], gkv_ref[...], dckvn, rkv)
        dckv_ref[...] = dckv
        dgkv_ref[...] += dg

    c2 = lambda i: (0, 0)
    c3 = lambda i: (0, 0, 0)
    hq = pl.BlockSpec((heads, tm, HEADW), lambda i: (0, i, 0))
    return pl.pallas_call(
        body, name="mla_proj_bwd", grid=(t // tm,),
        in_specs=[hq, hq, pl.BlockSpec((heads, tm, VHEAD), lambda i: (0, i, 0)),
                  pl.BlockSpec((tm, rq), lambda i: (i, 0)), pl.BlockSpec((tm, rkv), lambda i: (i, 0)),
                  pl.BlockSpec((tm, LANE), lambda i: (i, 0)), pl.BlockSpec((tm, 1), lambda i: (i, 0)),
                  pl.BlockSpec((1, LANE), c2), pl.BlockSpec((2, LANE), c2),
                  pl.BlockSpec((1, rq), c2), pl.BlockSpec((1, rkv), c2),
                  pl.BlockSpec((heads, HEADW, rq), c3), pl.BlockSpec((heads, HEADW, rkv), c3),
                  pl.BlockSpec((1, HEADW), c2), pl.BlockSpec((1, HEADW), c2)],
        out_specs=[pl.BlockSpec((tm, rq), lambda i: (i, 0)), pl.BlockSpec((tm, rkv), lambda i: (i, 0)),
                   pl.BlockSpec((tm, LANE), lambda i: (i, 0)),
                   pl.BlockSpec((heads, HEADW, rq), c3), pl.BlockSpec((heads, HEADW, rkv), c3),
                   pl.BlockSpec((1, rq), c2), pl.BlockSpec((1, rkv), c2),
                   pl.BlockSpec((1, HEADW), c2), pl.BlockSpec((1, HEADW), c2)],
        out_shape=[jax.ShapeDtypeStruct((t, rq), F32), jax.ShapeDtypeStruct((t, rkv), F32),
                   jax.ShapeDtypeStruct((t, LANE), F32),
                   jax.ShapeDtypeStruct((heads, HEADW, rq), F32), jax.ShapeDtypeStruct((heads, HEADW, rkv), F32),
                   jax.ShapeDtypeStruct((1, rq), F32), jax.ShapeDtypeStruct((1, rkv), F32),
                   jax.ShapeDtypeStruct((1, HEADW), F32), jax.ShapeDtypeStruct((1, HEADW), F32)],
        compiler_params=_params(("arbitrary",)),
    )(dq, dk, dv, cq, ckv, krw, pos, freq, masks, gq, gkv, wq_t, wkv_t, gqh, gkh)


def _causal_mask(qi, ki, blk):
    rows = qi * blk + lax.broadcasted_iota(jnp.int32, (blk, blk), 0)
    cols = ki * blk + lax.broadcasted_iota(jnp.int32, (blk, blk), 1)
    return rows >= cols


def _attn_fwd(q, k, v, seq, blk):
    heads, t, _ = q.shape
    scale = QK ** -0.5
    nblk = seq // blk

    def body(q_ref, k_ref, v_ref, o_ref, lse_ref):
        for qi in range(nblk):
            rows = slice(qi * blk, (qi + 1) * blk)
            qb = q_ref[0, rows, :]

            def step(ki, carry):
                m, l, acc = carry
                ks = pl.ds(pl.multiple_of(ki * blk, blk), blk)
                s = _nt(qb, k_ref[0, ks, :]) * scale
                s = jnp.where(_causal_mask(qi, ki, blk), s, -1e30)
                m_new = jnp.maximum(m, jnp.max(s, axis=-1, keepdims=True))
                p = jnp.exp(s - m_new)
                alpha = jnp.exp(m - m_new)
                l = alpha * l + jnp.sum(p, axis=-1, keepdims=True)
                acc = alpha * acc + _nn(p.astype(BF16), v_ref[0, ks, :])
                return m_new, l, acc

            init = (jnp.full((blk, 1), -1e30, F32), jnp.zeros((blk, 1), F32), jnp.zeros((blk, VHEAD), F32))
            m, l, acc = lax.fori_loop(0, qi + 1, step, init)
            o_ref[0, rows, :] = acc / l
            lse_ref[0, rows, :] = m + jnp.log(l)

    return pl.pallas_call(
        body, name="attn_fwd", grid=(heads, t // seq),
        in_specs=[pl.BlockSpec((1, seq, HEADW), lambda h, b: (h, b, 0)),
                  pl.BlockSpec((1, seq, HEADW), lambda h, b: (h, b, 0)),
                  pl.BlockSpec((1, seq, VHEAD), lambda h, b: (h, b, 0))],
        out_specs=[pl.BlockSpec((1, seq, VHEAD), lambda h, b: (h, b, 0)),
                   pl.BlockSpec((1, seq, 1), lambda h, b: (h, b, 0))],
        out_shape=[jax.ShapeDtypeStruct((heads, t, VHEAD), F32), jax.ShapeDtypeStruct((heads, t, 1), F32)],
        compiler_params=_params(("arbitrary", "arbitrary")),
    )(q, k, v)


def _attn_bwd(q, k, v, do, lse, delta, seq, blk):
    heads, t, _ = q.shape
    scale = QK ** -0.5
    nblk = seq // blk

    def body(q_ref, k_ref, v_ref, do_ref, lse_ref, dl_ref, dq_ref, dk_ref, dv_ref):
        dk_ref[...] = jnp.zeros_like(dk_ref)
        dv_ref[...] = jnp.zeros_like(dv_ref)
        for qi in range(nblk):
            rows = slice(qi * blk, (qi + 1) * blk)
            qb = q_ref[0, rows, :]
            dob = do_ref[0, rows, :]
            lse_b = lse_ref[0, rows, :]
            dl_b = dl_ref[0, rows, :]

            def step(ki, dq_acc):
                ks = pl.ds(pl.multiple_of(ki * blk, blk), blk)
                kb = k_ref[0, ks, :]
                s = _nt(qb, kb) * scale
                p = jnp.where(_causal_mask(qi, ki, blk), jnp.exp(s - lse_b), 0.0)
                dp = _nt(dob, v_ref[0, ks, :])
                ds = (p * (dp - dl_b) * scale).astype(BF16)
                dv_ref[0, ks, :] += _tn(p.astype(BF16), dob)
                dk_ref[0, ks, :] += _tn(ds, qb)
                return dq_acc + _nn(ds, kb)

            dq_ref[0, rows, :] = lax.fori_loop(0, qi + 1, step, jnp.zeros((blk, HEADW), F32))

    hq = pl.BlockSpec((1, seq, HEADW), lambda h, b: (h, b, 0))
    hv = pl.BlockSpec((1, seq, VHEAD), lambda h, b: (h, b, 0))
    h1 = pl.BlockSpec((1, seq, 1), lambda h, b: (h, b, 0))
    return pl.pallas_call(
        body, name="attn_bwd", grid=(heads, t // seq),
        in_specs=[hq, hq, hv, hv, h1, h1],
        out_specs=[hq, hq, hv],
        out_shape=[jax.ShapeDtypeStruct((heads, t, HEADW), F32), jax.ShapeDtypeStruct((heads, t, HEADW), F32),
                   jax.ShapeDtypeStruct((heads, t, VHEAD), F32)],
        compiler_params=_params(("arbitrary", "arbitrary")),
    )(q, k, v, do, lse, delta)


def _out_fwd(ya, o, gb, w_out, x1, tm):
    t, w = ya.shape
    heads = o.shape[0]
    d = x1.shape[1]

    def body(ya_ref, o_ref, gb_ref, w_ref, x1_ref, x2_ref, yc_ref):
        yc_ref[:, 0:w] = ya_ref[...]
        for h in range(heads):
            lanes = slice(h * VHEAD, (h + 1) * VHEAD)
            yc_ref[:, w + h * VHEAD:w + (h + 1) * VHEAD] = _rms_fwd(o_ref[h], gb_ref[:, lanes], VHEAD).astype(BF16)
        x2_ref[...] = x1_ref[...] + _nn(yc_ref[...], w_ref[...])

    wy = w + heads * VHEAD
    row = pl.BlockSpec((tm, d), lambda i: (i, 0))
    return pl.pallas_call(
        body, name="out_fwd", grid=(t // tm,),
        in_specs=[pl.BlockSpec((tm, w), lambda i: (i, 0)), pl.BlockSpec((heads, tm, VHEAD), lambda i: (0, i, 0)),
                  pl.BlockSpec((1, heads * VHEAD), lambda i: (0, 0)), WHOLE_VMEM, row],
        out_specs=[row, pl.BlockSpec((tm, wy), lambda i: (i, 0))],
        out_shape=[jax.ShapeDtypeStruct((t, d), F32), jax.ShapeDtypeStruct((t, wy), BF16)],
        compiler_params=_params(("arbitrary",)),
    )(ya, o, gb, w_out, x1)


def _out_bwd(dx2, o, gb, w_out, w, tm):
    t, d = dx2.shape
    heads = o.shape[0]

    def body(dx_ref, o_ref, gb_ref, w_ref, dya_ref, do_ref, dl_ref, dgb_ref):
        @pl.when(pl.program_id(0) == 0)
        def _():
            dgb_ref[...] = jnp.zeros_like(dgb_ref)

        dyc = _nt(dx_ref[...].astype(BF16), w_ref[...])
        dya_ref[...] = dyc[:, 0:w]
        for h in range(heads):
            lanes = slice(h * VHEAD, (h + 1) * VHEAD)
            oh = o_ref[h]
            doh, dg = _rms_bwd(oh, gb_ref[:, lanes], dyc[:, w + h * VHEAD:w + (h + 1) * VHEAD], VHEAD)
            dgb_ref[:, lanes] += dg
            do_ref[h] = doh.astype(BF16)
            dl_ref[h] = jnp.sum(doh * oh, axis=-1, keepdims=True)

    ho = pl.BlockSpec((heads, tm, VHEAD), lambda i: (0, i, 0))
    vec = pl.BlockSpec((1, heads * VHEAD), lambda i: (0, 0))
    return pl.pallas_call(
        body, name="out_bwd", grid=(t // tm,),
        in_specs=[pl.BlockSpec((tm, d), lambda i: (i, 0)), ho, vec, WHOLE_VMEM],
        out_specs=[pl.BlockSpec((tm, w), lambda i: (i, 0)), ho, pl.BlockSpec((heads, tm, 1), lambda i: (0, i, 0)), vec],
        out_shape=[jax.ShapeDtypeStruct((t, w), F32), jax.ShapeDtypeStruct((heads, t, VHEAD), BF16),
                   jax.ShapeDtypeStruct((heads, t, 1), F32), jax.ShapeDtypeStruct((1, heads * VHEAD), F32)],
        compiler_params=_params(("arbitrary",)),
    )(dx2, o, gb, w_out)


def _loss_head(y, target, tm):
    t, d = y.shape

    def body(y_ref, t_ref, dy_ref, loss_ref):
        @pl.when(pl.program_id(0) == 0)
        def _():
            loss_ref[...] = jnp.zeros_like(loss_ref)

        err = y_ref[...] - t_ref[...]
        dy_ref[...] = err * (1.0 / d)
        part = jnp.sum(jnp.sum(err * err, axis=-1, keepdims=True) * (1.0 / d), axis=0, keepdims=True)
        loss_ref[...] += 0.5 * part

    row = pl.BlockSpec((tm, d), lambda i: (i, 0))
    return pl.pallas_call(
        body, name="loss_head", grid=(t // tm,),
        in_specs=[row, row], out_specs=[row, pl.BlockSpec((1, 1), lambda i: (0, 0))],
        out_shape=[jax.ShapeDtypeStruct((t, d), F32), jax.ShapeDtypeStruct((1, 1), F32)],
        compiler_params=_params(("arbitrary",)),
    )(y, target)


def _place():
    return lax.axis_index("x"), lax.axis_index("y"), lax.axis_index("c")


def _all_gather(arrs, name):
    n = len(arrs)

    def body(*refs):
        ins, outs = refs[:n], refs[n:2 * n]
        send_sems, recv_sems, local_sems = refs[2 * n:]
        x, y, c = _place()
        sibling = (x, y, 1 - c)
        chips = [(1 - x, y), (x, 1 - y), (1 - x, 1 - y)]

        def blk(a, px, py, pc):
            return outs[a].at[4 * px + 2 * py + pc]

        def copy(a, k, block, to, src=None):
            return pltpu.make_async_remote_copy(
                src_ref=blk(a, *block) if src is None else src, dst_ref=blk(a, *block),
                send_sem=send_sems.at[a, k], recv_sem=recv_sems.at[a, k], device_id=to, device_id_type=MESH)

        mine = [pltpu.make_async_copy(ins[a], blk(a, x, y, c), local_sems.at[a]) for a in range(n)]
        for cp in mine:
            cp.start()
        first = []
        for a in range(n):
            first.append(copy(a, 0, (x, y, c), sibling, src=ins[a]))
            first += [copy(a, 1 + j, (x, y, c), (*chip, c), src=ins[a]) for j, chip in enumerate(chips)]
        for cp in first:
            cp.start()
        passed = []
        for j, chip in enumerate(chips):
            for a in range(n):
                copy(a, 1 + j, (*chip, c), (x, y, c)).wait_recv()
                cp = copy(a, 4 + j, (*chip, c), sibling)
                cp.start()
                passed.append(cp)
        for a in range(n):
            copy(a, 0, sibling, (x, y, c)).wait_recv()
            for j, chip in enumerate(chips):
                copy(a, 4 + j, (*chip, 1 - c), (x, y, c)).wait_recv()
        for cp in first + passed:
            cp.wait_send()
        for cp in mine:
            cp.wait()

    return pl.pallas_call(
        body, name=name,
        in_specs=[ANY] * n, out_specs=[ANY] * n,
        out_shape=[jax.ShapeDtypeStruct((N_DEV,) + a.shape, a.dtype) for a in arrs],
        scratch_shapes=[pltpu.SemaphoreType.DMA((n, 7)), pltpu.SemaphoreType.DMA((n, 7)),
                        pltpu.SemaphoreType.DMA((n,))],
    )(*arrs)


HBM = pl.BlockSpec(memory_space=pltpu.HBM)
SEM = pl.BlockSpec(memory_space=pltpu.SEMAPHORE)
DATAFLOW = pltpu.SideEffectType.DATAFLOW_SIDE_EFFECTING


def _plan_copies(plan, refs, send_sems, recv_sems):
    cps = []
    for i, (sb, sblk, db, dblk, dev) in enumerate(plan(*_place())):
        cps.append(pltpu.make_async_remote_copy(
            src_ref=refs[sb] if sblk is None else refs[sb].at[sblk], dst_ref=refs[db].at[dblk],
            send_sem=send_sems.at[i], recv_sem=recv_sems.at[i], device_id=dev, device_id_type=MESH))
    return cps


def _push_start(bufs, plan, ncopy, name, deps=()):
    nb = len(bufs)

    def body(*refs):
        outs = refs[nb + len(deps):]
        for cp in _plan_copies(plan, refs[:nb], outs[0], outs[1]):
            cp.start()
        outs[-1][...] = jnp.zeros_like(outs[-1])

    res = pl.pallas_call(
        body, name=name,
        out_shape=(pltpu.SemaphoreType.DMA((ncopy,)), pltpu.SemaphoreType.DMA((ncopy,)),
                   *[pltpu.HBM(b.shape, b.dtype) for b in bufs], jax.ShapeDtypeStruct((SUBLANE, LANE), F32)),
        in_specs=[HBM] * nb + [ANY] * len(deps),
        out_specs=(SEM, SEM, *[HBM] * nb, WHOLE_VMEM),
        input_output_aliases={i: 2 + i for i in range(nb)},
        compiler_params=pltpu.CompilerParams(has_side_effects=DATAFLOW),
    )(*[pltpu.with_memory_space_constraint(b, pltpu.HBM) for b in bufs], *deps)
    return res[0], res[1], list(res[2:2 + nb]), res[-1]


def _push_wait(send_sems, recv_sems, bufs, plan, after, name):
    nb = len(bufs)

    def body(*refs):
        for cp in _plan_copies(plan, refs[:nb], refs[nb], refs[nb + 1]):
            cp.wait_send()
            cp.wait_recv()

    res = pl.pallas_call(
        body, name=name,
        out_shape=[pltpu.HBM(b.shape, b.dtype) for b in bufs],
        in_specs=[HBM] * nb + [SEM, SEM, ANY], out_specs=[HBM] * nb,
        input_output_aliases={i: i for i in range(nb)},
        compiler_params=pltpu.CompilerParams(has_side_effects=DATAFLOW),
    )(*bufs, send_sems, recv_sems, after)
    return list(res)


def _other_chips(x, y):
    return ((1 - x, y), (x, 1 - y), (1 - x, 1 - y))


class _Exchange:
    def __init__(self, bufs, plan, ncopy, name, deps=()):
        self.plan, self.name = plan, name
        self.send, self.recv, self.bufs, self.token = _push_start(bufs, plan, ncopy, name + "_start", deps)

    def wait(self, after):
        return _push_wait(self.send, self.recv, self.bufs, self.plan, after, self.name + "_wait")


def _gather_ici(shards, me, name, deps=()):
    n = len(shards)
    lands = [lax.dynamic_update_slice(lax.empty((N_DEV,) + s.shape, s.dtype), s[None], (me, 0, 0)) for s in shards]

    def plan(x, y, c):
        return [(a, None, n + a, 4 * x + 2 * y + c, (px, py, c)) for a in range(n) for px, py in _other_chips(x, y)]

    return _Exchange(list(shards) + lands, plan, 3 * n, name, deps)


def _gather_d2d(lands, name, deps=()):
    n = len(lands)

    def plan(x, y, c):
        blocks = [4 * x + 2 * y + c] + [4 * px + 2 * py + c for px, py in _other_chips(x, y)]
        return [(a, b, a, b, (x, y, 1 - c)) for a in range(n) for b in blocks]

    return _Exchange(list(lands), plan, 4 * n, name, deps)


def _reduce_d2d(grads, name, deps=()):
    n = len(grads)
    lands = [lax.empty((4,) + g.shape[1:], g.dtype) for g in grads]

    def plan(x, y, c):
        return [(a, 2 * s + (1 - c), n + a, s, (x, y, 1 - c)) for a in range(n) for s in range(4)]

    return _Exchange(list(grads) + lands, plan, 4 * n, name, deps)


def _reduce_ici(chip, name, deps=()):
    n = len(chip)
    lands = [lax.empty((3,) + g.shape[1:], g.dtype) for g in chip]

    def plan(x, y, c):
        return [(a, 2 * px + py, n + a, k, (px, py, c))
                for a in range(n) for k, (px, py) in enumerate(_other_chips(x, y))]

    return _Exchange(list(chip) + lands, plan, 3 * n, name, deps)


def _pair_add(full, got, core, name):
    _, r, cdim = full.shape
    br = _row_block(r, 512)

    def body(c_ref, f_ref, g_ref, o_ref):
        o_ref[...] = (f_ref[...].astype(F32) + g_ref[...].astype(F32)).astype(o_ref.dtype)

    return pl.pallas_call(
        body, name=name,
        grid_spec=pltpu.PrefetchScalarGridSpec(
            num_scalar_prefetch=1, grid=(4, r // br),
            in_specs=[pl.BlockSpec((1, br, cdim), lambda s, i, c_ref: (2 * s + c_ref[0], i, 0)),
                      pl.BlockSpec((1, br, cdim), lambda s, i, c_ref: (s, i, 0))],
            out_specs=pl.BlockSpec((1, br, cdim), lambda s, i, c_ref: (s, i, 0))),
        out_shape=jax.ShapeDtypeStruct((4, r, cdim), full.dtype),
        compiler_params=_params(("arbitrary", "arbitrary")),
    )(core, full, got)


def _sum_owned(chip, got, slot, rows, name):
    cdim = chip.shape[2]
    br = _row_block(rows, 512)

    def body(s_ref, c_ref, g_ref, o_ref):
        acc = c_ref[0].astype(F32)
        for k in range(3):
            acc = acc + g_ref[k].astype(F32)
        o_ref[...] = acc

    return pl.pallas_call(
        body, name=name,
        grid_spec=pltpu.PrefetchScalarGridSpec(
            num_scalar_prefetch=1, grid=(rows // br,),
            in_specs=[pl.BlockSpec((1, br, cdim), lambda i, s_ref: (s_ref[0], i, 0)),
                      pl.BlockSpec((3, br, cdim), lambda i, s_ref: (0, i, 0))],
            out_specs=pl.BlockSpec((br, cdim), lambda i, s_ref: (i, 0))),
        out_shape=jax.ShapeDtypeStruct((rows, cdim), F32),
        compiler_params=_params(("arbitrary",)),
    )(slot, chip, got)


def _sum_devices(stack):
    _, r, cdim = stack.shape

    def body(s_ref, o_ref):
        acc = s_ref[0]
        for k in range(1, N_DEV):
            acc = acc + s_ref[k]
        o_ref[...] = acc

    return pl.pallas_call(
        body, name="sum_devices", out_shape=jax.ShapeDtypeStruct((r, cdim), F32),
        compiler_params=_params(),
    )(stack)


def _adamw(w, g, m, v, name):
    r, cdim = w.shape
    br = _row_block(r, 256)

    def body(w_ref, g_ref, m_ref, v_ref, d_ref, nm_ref, nv_ref):
        g = g_ref[...]
        nm = ADAM_B1 * m_ref[...] + (1.0 - ADAM_B1) * g
        nv = ADAM_B2 * v_ref[...] + (1.0 - ADAM_B2) * (g * g)
        m_hat = nm / (1.0 - ADAM_B1 ** ADAM_STEP)
        v_hat = nv / (1.0 - ADAM_B2 ** ADAM_STEP)
        d_ref[...] = -ADAM_LR * (m_hat / (jnp.sqrt(v_hat) + ADAM_EPS) + ADAM_WD * w_ref[...])
        nm_ref[...] = nm
        nv_ref[...] = nv

    spec = pl.BlockSpec((br, cdim), lambda i: (i, 0))
    shape = jax.ShapeDtypeStruct((r, cdim), F32)
    return pl.pallas_call(
        body, name=name, grid=(r // br,), in_specs=[spec] * 4, out_specs=[spec] * 3,
        out_shape=[shape] * 3, compiler_params=_params(("arbitrary",)),
    )(w, g, m, v)


WEIGHTS = ("ffn1_norm_g", "ffn1_w_gate", "ffn1_w_up", "ffn1_w_down", "mix_norm_g", "w_in", "gmlp_v_norm_g",
           "gmlp_w_s", "gmlp_b_s", "mla_q_norm_g", "mla_w_q_up", "mla_kv_norm_g", "mla_w_kv_up", "mla_q_head_g",
           "mla_k_head_g", "gmlp_out_g", "mla_out_g", "w_out", "ffn2_norm_g", "ffn2_w_gate", "ffn2_w_up",
           "ffn2_w_down")
SHARDED = {"ffn1_w_gate": True, "ffn1_w_up": True, "ffn1_w_down": False, "w_in": True, "mla_w_q_up": True,
           "mla_w_kv_up": True, "w_out": False, "ffn2_w_gate": True, "ffn2_w_up": True, "ffn2_w_down": False}


def _col_block(m, target):
    best = LANE
    for cand in range(LANE, min(m, target) + 1, LANE):
        if m % cand == 0:
            best = cand
    return best


def _shard_rows(w, transposed, pad_to=None):
    rows = (w[0].T if transposed else w[0]).astype(BF16)
    if pad_to is not None and pad_to != rows.shape[0]:
        rows = jnp.pad(rows, ((0, pad_to - rows.shape[0]), (0, 0)))
    return rows


def _pack(parts):
    flat = []
    for p in parts:
        f = p.reshape(-1).astype(F32)
        flat.append(jnp.pad(f, (0, _round_up(f.size, LANE) - f.size)))
    flat = jnp.concatenate(flat)
    rows = _round_up(flat.size // LANE, SUBLANE)
    return jnp.pad(flat, (0, rows * LANE - flat.size)).reshape(rows, LANE)


def _unpack(packed, shapes):
    out, row = [], 0
    for shp in shapes:
        size = 1
        for s in shp:
            size *= s
        nrows = _round_up(size, LANE) // LANE
        out.append(packed[row:row + nrows].reshape(-1)[:size].reshape(shp))
        row += nrows
    return out


def kernel(x, positions, ffn1_norm_g, ffn1_w_gate, ffn1_w_up, ffn1_w_down, mix_norm_g, w_in, gmlp_v_norm_g, gmlp_w_s, gmlp_b_s, mla_q_norm_g, mla_w_q_up, mla_kv_norm_g, mla_w_kv_up, mla_q_head_g, mla_k_head_g, gmlp_out_g, mla_out_g, w_out, ffn2_norm_g, ffn2_w_gate, ffn2_w_up, ffn2_w_down, loss_target, m_ffn1_norm_g, m_ffn1_w_gate, m_ffn1_w_up, m_ffn1_w_down, m_mix_norm_g, m_w_in, m_gmlp_v_norm_g, m_gmlp_w_s, m_gmlp_b_s, m_mla_q_norm_g, m_mla_w_q_up, m_mla_kv_norm_g, m_mla_w_kv_up, m_mla_q_head_g, m_mla_k_head_g, m_gmlp_out_g, m_mla_out_g, m_w_out, m_ffn2_norm_g, m_ffn2_w_gate, m_ffn2_w_up, m_ffn2_w_down, v_ffn1_norm_g, v_ffn1_w_gate, v_ffn1_w_up, v_ffn1_w_down, v_mix_norm_g, v_w_in, v_gmlp_v_norm_g, v_gmlp_w_s, v_gmlp_b_s, v_mla_q_norm_g, v_mla_w_q_up, v_mla_kv_norm_g, v_mla_w_kv_up, v_mla_q_head_g, v_mla_k_head_g, v_gmlp_out_g, v_mla_out_g, v_w_out, v_ffn2_norm_g, v_ffn2_w_gate, v_ffn2_w_up, v_ffn2_w_down):
    wts = dict(zip(WEIGHTS, (ffn1_norm_g, ffn1_w_gate, ffn1_w_up, ffn1_w_down, mix_norm_g, w_in, gmlp_v_norm_g, gmlp_w_s, gmlp_b_s, mla_q_norm_g, mla_w_q_up, mla_kv_norm_g, mla_w_kv_up, mla_q_head_g, mla_k_head_g, gmlp_out_g, mla_out_g, w_out, ffn2_norm_g, ffn2_w_gate, ffn2_w_up, ffn2_w_down)))
    mom1 = dict(zip(WEIGHTS, (m_ffn1_norm_g, m_ffn1_w_gate, m_ffn1_w_up, m_ffn1_w_down, m_mix_norm_g, m_w_in, m_gmlp_v_norm_g, m_gmlp_w_s, m_gmlp_b_s, m_mla_q_norm_g, m_mla_w_q_up, m_mla_kv_norm_g, m_mla_w_kv_up, m_mla_q_head_g, m_mla_k_head_g, m_gmlp_out_g, m_mla_out_g, m_w_out, m_ffn2_norm_g, m_ffn2_w_gate, m_ffn2_w_up, m_ffn2_w_down)))
    mom2 = dict(zip(WEIGHTS, (v_ffn1_norm_g, v_ffn1_w_gate, v_ffn1_w_up, v_ffn1_w_down, v_mix_norm_g, v_w_in, v_gmlp_v_norm_g, v_gmlp_w_s, v_gmlp_b_s, v_mla_q_norm_g, v_mla_w_q_up, v_mla_kv_norm_g, v_mla_w_kv_up, v_mla_q_head_g, v_mla_k_head_g, v_gmlp_out_g, v_mla_out_g, v_w_out, v_ffn2_norm_g, v_ffn2_w_gate, v_ffn2_w_up, v_ffn2_w_down)))

    b_loc, seq, d = x.shape
    t = b_loc * seq
    ffs = ffn1_w_gate.shape[2]
    fp = _round_up(ffs, LANE)
    wg = gmlp_v_norm_g.shape[1]
    groups = gmlp_w_s.shape[1]
    rq, rkv = mla_q_norm_g.shape[1], mla_kv_norm_g.shape[1]
    heads = mla_out_g.shape[1]
    assert w_in.shape[2] * N_DEV == 2 * wg + rq + rkv + ROPE and mla_w_kv_up.shape[2] * N_DEV == heads * HEADW
    tm = min(512, t)
    tm_mix = min(256, t)
    blk = min(256, seq)

    xf = x.reshape(t, d)
    target = loss_target.reshape(t, d)
    pos = positions.reshape(t, 1).astype(F32)
    half = ROPE // 2
    inv_freq = 1.0 / (ROPE_THETA ** (jnp.arange(half, dtype=F32) / half))
    freq = jnp.concatenate([inv_freq, inv_freq, jnp.zeros((LANE - ROPE,), F32)])[None, :]
    lane = jnp.arange(LANE)
    masks = jnp.stack([jnp.where(lane < half, -1.0, 0.0),
                       jnp.where((lane >= half) & (lane < ROPE), 1.0, 0.0)]).astype(F32)
    gqh = jnp.pad(mla_q_head_g, ((0, 0), (0, HEADW - QK)))
    gkh = jnp.pad(mla_k_head_g, ((0, 0), (0, HEADW - QK)))
    bias = jnp.repeat(gmlp_b_s[0].T, CHUNK, axis=1)
    gouta = gmlp_out_g.reshape(1, wg)
    goutb = mla_out_g.reshape(1, heads * VHEAD)
    ws = gmlp_w_s[0]

    px, py, pc = _place()
    me = 4 * px + 2 * py + pc
    core = pc.astype(jnp.int32).reshape(1)
    slot = (2 * px + py).astype(jnp.int32).reshape(1)
    order = [n for n in WEIGHTS if n in SHARDED]
    group = {"ffn1": [n for n in order if n.startswith("ffn1")], "ffn2": [n for n in order if n.startswith("ffn2")],
             "mix": [n for n in order if not n.startswith("ffn")]}
    shard = {n: _shard_rows(wts[n], SHARDED[n], fp if n.startswith("ffn") else None) for n in order}

    def tied(arr, token):
        return arr + token[0, 0].astype(arr.dtype)

    ici1 = _gather_ici([shard[n] for n in group["ffn1"]], me, "gather_ffn1_ici")
    lands1 = ici1.wait(ici1.token)[3:]
    d2d1 = _gather_d2d(lands1, "gather_ffn1_d2d")
    ici2 = _gather_ici([shard[n] for n in group["mix"]], me, "gather_mix_ici", deps=(d2d1.token,))
    ici3 = _gather_ici([shard[n] for n in group["ffn2"]], me, "gather_ffn2_ici", deps=(ici2.token,))
    full = dict(zip(group["ffn1"], d2d1.wait(ici3.token)))

    x1, xn1, a1, b1 = _ffn_fwd(xf, ffn1_norm_g, full["ffn1_w_gate"], full["ffn1_w_up"], full["ffn1_w_down"], tm,
                               "ffn1_fwd")
    d2d2 = _gather_d2d(ici2.wait(x1)[len(group["mix"]):], "gather_mix_d2d")
    full.update(zip(group["mix"], d2d2.wait(d2d2.token)))
    win_t = jnp.pad(full["w_in"].reshape(-1, d), ((0, LANE - ROPE), (0, 0)))
    splits = (2 * wg, rq, rkv, LANE)
    wq_t = jnp.pad(full["mla_w_q_up"].reshape(heads, QK, rq), ((0, 0), (0, HEADW - QK), (0, 0)))
    wkv_t = full["mla_w_kv_up"].reshape(heads, HEADW, rkv)
    wout = full["w_out"].reshape(-1, d)
    hn, zuv, cq, ckv, krw = _inproj_fwd(x1, mix_norm_g, win_t, splits, tm)
    ya = _gmlp_fwd(zuv, gmlp_v_norm_g, ws, bias, gouta, tm_mix)
    q, k, vv = _mla_proj_fwd(cq, ckv, krw, pos, freq, masks, mla_q_norm_g, mla_kv_norm_g, wq_t, wkv_t, gqh, gkh,
                             tm_mix)
    d2d3 = _gather_d2d(ici3.wait(q)[3:], "gather_ffn2_d2d")
    o, lse = _attn_fwd(q, k, vv, seq, blk)
    x2, ycat = _out_fwd(ya, o, tied(goutb, d2d3.token), wout, x1, tm)
    full.update(zip(group["ffn2"], d2d3.wait(x2)))
    x3, xn2, a2, b2 = _ffn_fwd(x2, ffn2_norm_g, full["ffn2_w_gate"], full["ffn2_w_up"], full["ffn2_w_down"], tm,
                               "ffn2_fwd")
    dx3, loss_part = _loss_head(x3, target, tm)

    outs_g, outs_d, outs_m, outs_v = {}, {}, {}, {}

    def finish(names, chip, got):
        for n, cp, gt in zip(names, chip, got):
            rows = wts[n].shape[2] if SHARDED[n] else wts[n].shape[1]
            g = _sum_owned(cp, gt, slot, rows, "sum_owned_" + n)
            g = g.T if SHARDED[n] else g
            dlt, nm, nv = _adamw(wts[n][0], g, mom1[n][0], mom2[n][0], "adamw_" + n)
            outs_g[n], outs_d[n], outs_m[n], outs_v[n] = g[None], dlt[None], nm[None], nv[None]

    def chip_sums(names, ex, after):
        res = ex.wait(after)
        return [_pair_add(f, gt, core, "pair_add_" + n) for n, f, gt in zip(names, res[:len(names)], res[len(names):])]

    tk = min(512, t)
    grads = {}
    small = {}
    dx2, small["ffn2_norm_g"], dy2, da2, db2, h2 = _ffn_bwd(
        dx3, x2, ffn2_norm_g, a2, b2, full["ffn2_w_gate"], full["ffn2_w_up"], full["ffn2_w_down"], tm_mix,
        "ffn2_bwd")
    grads["ffn2_w_gate"] = _matmul_tn(da2, xn2, fp, d, tk, BF16, "dw_ffn2_gate").reshape(N_DEV, fp, d)
    grads["ffn2_w_up"] = _matmul_tn(db2, xn2, fp, d, tk, BF16, "dw_ffn2_up").reshape(N_DEV, fp, d)
    grads["ffn2_w_down"] = _matmul_tn(h2, dy2, fp, d, tk, BF16, "dw_ffn2_down").reshape(N_DEV, fp, d)
    red_a2 = _reduce_d2d([grads[n] for n in group["ffn2"]], "reduce_ffn2_d2d")
    dya, do, delta, small["mla_out_g"] = _out_bwd(dx2, o, tied(goutb, red_a2.token), wout, wg, tm)
    grads["w_out"] = _matmul_tn(ycat, dx2, _col_block(ycat.shape[1], 768), d, tk, BF16, "dw_out").reshape(
        N_DEV, -1, d)
    chip2 = chip_sums(group["ffn2"], red_a2, grads["w_out"])
    red_b2 = _reduce_ici(chip2, "reduce_ffn2_ici")
    dq, dk, dv = _attn_bwd(q, k, vv, do, lse, tied(delta, red_b2.token), seq, blk)
    (dcq, dckv, dkrw, dwq, dwkv, small["mla_q_norm_g"], small["mla_kv_norm_g"], dgqh, dgkh) = _mla_proj_bwd(
        dq, dk, dv, cq, ckv, krw, pos, freq, masks, mla_q_norm_g, mla_kv_norm_g, wq_t, wkv_t, gqh, gkh, tm_mix)
    small["mla_q_head_g"], small["mla_k_head_g"] = dgqh[:, :QK], dgkh[:, :QK]
    grads["mla_w_q_up"] = dwq[:, :QK].astype(BF16).reshape(N_DEV, -1, rq)
    grads["mla_w_kv_up"] = dwkv.astype(BF16).reshape(N_DEV, -1, rkv)
    dzuv, small["gmlp_w_s"], dbs, small["gmlp_v_norm_g"], small["gmlp_out_g"] = _gmlp_bwd(
        dya, zuv, gmlp_v_norm_g, ws, bias, gouta, tm_mix)
    small["gmlp_b_s"] = dbs[:, :, 0]
    dx1, small["mix_norm_g"], dzc = _inproj_bwd([dzuv, dcq, dckv, dkrw], x1, mix_norm_g, win_t, dx2, splits,
                                                tm_mix)
    dwin = _matmul_tn(dzc, hn, _col_block(dzc.shape[1], 768), d, tk, BF16, "dw_in")
    grads["w_in"] = dwin[:N_DEV * w_in.shape[2]].reshape(N_DEV, -1, d)
    res = red_b2.wait(grads["w_in"])
    red_am = _reduce_d2d([grads[n] for n in group["mix"]], "reduce_mix_d2d")
    dx0, small["ffn1_norm_g"], dy1, da1, db1, h1 = _ffn_bwd(
        dx1, xf, tied(ffn1_norm_g, red_am.token), a1, b1, full["ffn1_w_gate"], full["ffn1_w_up"],
        full["ffn1_w_down"], tm_mix, "ffn1_bwd")
    finish(group["ffn2"], res[:3], res[3:])
    chipm = chip_sums(group["mix"], red_am, dx0)
    red_bm = _reduce_ici(chipm, "reduce_mix_ici")
    operands = {"ffn1_w_gate": (da1, xn1), "ffn1_w_up": (db1, xn1), "ffn1_w_down": (h1, dy1)}
    token, red_a, red_b = red_bm.token, None, []
    for n in group["ffn1"]:
        lhs, rhs = operands[n]
        gr = _matmul_tn(lhs, rhs, fp, d, tk, BF16, "dw_" + n).reshape(N_DEV, fp, d)
        if red_a is not None:
            prev, ex = red_a
            red_b.append((prev, _reduce_ici(chip_sums([prev], ex, gr), "reduce_" + prev + "_ici")))
            token = red_b[-1][1].token
        red_a = (n, _reduce_d2d([gr], "reduce_" + n + "_d2d", deps=(token,)))
    prev, ex = red_a
    red_b.append((prev, _reduce_ici(chip_sums([prev], ex, ex.token), "reduce_" + prev + "_ici")))
    res = red_bm.wait(red_b[-1][1].token)
    nm_ = len(group["mix"])
    finish(group["mix"], res[:nm_], res[nm_:])
    for n, ex in red_b:
        res = ex.wait(outs_g[group["mix"][-1]])
        finish([n], res[:1], res[1:])

    rep = [n for n in WEIGHTS if n not in SHARDED]
    packed = _pack([small[n] for n in rep] + [loss_part])
    total = _sum_devices(_all_gather([packed], "gather_small")[0])
    zero = jnp.zeros((1,), F32)
    dlt, nm, nv = _adamw(_pack([wts[n] for n in rep] + [zero]), total, _pack([mom1[n] for n in rep] + [zero]),
                         _pack([mom2[n] for n in rep] + [zero]), "adamw_small")
    shapes = [wts[n].shape for n in rep] + [(1,)]
    for n, g, dl, m1, m2 in zip(rep, _unpack(total, shapes), _unpack(dlt, shapes), _unpack(nm, shapes),
                                _unpack(nv, shapes)):
        outs_g[n], outs_d[n], outs_m[n], outs_v[n] = g, dl, m1, m2
    loss = _unpack(total, shapes)[-1].reshape(())

    return (loss, dx0.reshape(b_loc, seq, d), *[outs_g[n] for n in WEIGHTS], *[outs_d[n] for n in WEIGHTS],
            *[outs_m[n] for n in WEIGHTS], *[outs_v[n] for n in WEIGHTS])
```

```python
import functools

import jax
import jax.numpy as jnp
from jax import lax
from jax.experimental import pallas as pl
from jax.experimental.pallas import tpu as pltpu

F32 = jnp.float32
BF16 = jnp.bfloat16
EPS = 1e-6
LANE = 128
SUBLANE = 8
N_DEV = 8
VMEM_LIMIT = 60 * 1024 * 1024
NOPE = 128
ROPE = 64
VHEAD = 128
QK = NOPE + ROPE
HEADW = 2 * LANE
CHUNK = 128
ROPE_THETA = 10000.0
ADAM_LR, ADAM_B1, ADAM_B2, ADAM_EPS, ADAM_WD, ADAM_STEP = 0.001, 0.9, 0.999, 1e-08, 0.01, 10
MESH = pl.DeviceIdType.MESH
ANY = pl.BlockSpec(memory_space=pl.ANY)
WHOLE_VMEM = pl.BlockSpec(memory_space=pltpu.VMEM)


def _params(sem=None):
    return pltpu.CompilerParams(dimension_semantics=sem, vmem_limit_bytes=VMEM_LIMIT)


def _round_up(n, m):
    return -(-n // m) * m


def _row_block(rows, target):
    best = rows
    for cand in range(SUBLANE, min(rows, target) + 1, SUBLANE):
        if rows % cand == 0:
            best = cand
    return best if best <= target else rows


def _nn(a, b):
    return jnp.dot(a, b, preferred_element_type=F32)


def _nt(a, b):
    return lax.dot_general(a, b, (((1,), (1,)), ((), ())), preferred_element_type=F32)


def _tn(a, b):
    return lax.dot_general(a, b, (((0,), (0,)), ((), ())), preferred_element_type=F32)


def _rstd(x, n):
    return lax.rsqrt(jnp.sum(x * x, axis=-1, keepdims=True) * (1.0 / n) + EPS)


def _rms_fwd(x, g, n):
    return x * _rstd(x, n) * g


def _rms_bwd(x, g, dy, n):
    r = _rstd(x, n)
    xh = x * r
    dyg = dy * g
    dx = r * (dyg - xh * (jnp.sum(dyg * xh, axis=-1, keepdims=True) * (1.0 / n)))
    return dx, jnp.sum(dy * xh, axis=0, keepdims=True)


def _gelu(x):
    return 0.5 * x * (1.0 + lax.erf(x * 0.7071067811865476))


def _gelu_grad(x):
    return 0.5 * (1.0 + lax.erf(x * 0.7071067811865476)) + x * jnp.exp(-0.5 * x * x) * 0.3989422804014327


def _ffn_fwd(x, g, wg_t, wu_t, wd, tm, name):
    t, d = x.shape
    nb, fp, _ = wg_t.shape

    def body(x_ref, g_ref, wg_ref, wu_ref, wd_ref, xo_ref, xn_ref, a_ref, b_ref, acc):
        j = pl.program_id(1)

        @pl.when(j == 0)
        def _():
            xn_ref[...] = _rms_fwd(x_ref[...], g_ref[...], d).astype(BF16)
            acc[...] = jnp.zeros_like(acc)

        xn = xn_ref[...]
        a = _nt(xn, wg_ref[0]).astype(BF16)
        b = _nt(xn, wu_ref[0]).astype(BF16)
        a_ref[...] = a
        b_ref[...] = b
        a = a.astype(F32)
        h = (a * jax.nn.sigmoid(a)) * b.astype(F32)
        acc[...] += _nn(h.astype(BF16), wd_ref[0])

        @pl.when(j == nb - 1)
        def _():
            xo_ref[...] = x_ref[...] + 0.5 * acc[...]

    wspec = pl.BlockSpec((1, fp, d), lambda i, j: (j, 0, 0))
    row = pl.BlockSpec((tm, d), lambda i, j: (i, 0))
    ff = pl.BlockSpec((tm, fp), lambda i, j: (i, j))
    return pl.pallas_call(
        body, name=name, grid=(t // tm, nb),
        in_specs=[row, pl.BlockSpec((1, d), lambda i, j: (0, 0)), wspec, wspec, wspec],
        out_specs=[row, row, ff, ff],
        out_shape=[jax.ShapeDtypeStruct((t, d), F32), jax.ShapeDtypeStruct((t, d), BF16),
                   jax.ShapeDtypeStruct((t, nb * fp), BF16), jax.ShapeDtypeStruct((t, nb * fp), BF16)],
        scratch_shapes=[pltpu.VMEM((tm, d), F32)],
        compiler_params=_params(("arbitrary", "arbitrary")),
    )(x, g, wg_t, wu_t, wd)


def _ffn_bwd(dout, x, g, a, b, wg_t, wu_t, wd, tm, name):
    t, d = x.shape
    nb, fp, _ = wg_t.shape

    def body(do_hbm, x_hbm, g_ref, a_ref, b_ref, wg_ref, wu_ref, wd_ref,
             dx_ref, dg_ref, da_ref, db_ref, h_ref, acc, rowbuf, dy_scr, sem):
        i, j = pl.program_id(0), pl.program_id(1)
        rows = pl.ds(pl.multiple_of(i * tm, tm), tm)

        def fetch(src):
            cp = pltpu.make_async_copy(src.at[rows, :], rowbuf, sem)
            cp.start()
            cp.wait()

        @pl.when(j == 0)
        def _():
            fetch(do_hbm)
            dy_scr[...] = (0.5 * rowbuf[...]).astype(BF16)
            acc[...] = jnp.zeros_like(acc)

        @pl.when((i == 0) & (j == 0))
        def _():
            dg_ref[...] = jnp.zeros_like(dg_ref)

        dh = _nt(dy_scr[...], wd_ref[0])
        a = a_ref[...].astype(F32)
        bb = b_ref[...].astype(F32)
        s = jax.nn.sigmoid(a)
        sl = a * s
        h_ref[...] = (sl * bb).astype(BF16)
        da = (dh * bb * (s * (1.0 + a * (1.0 - s)))).astype(BF16)
        db = (dh * sl).astype(BF16)
        da_ref[...] = da
        db_ref[...] = db
        acc[...] += _nn(da, wg_ref[0]) + _nn(db, wu_ref[0])

        @pl.when(j == nb - 1)
        def _():
            fetch(x_hbm)
            dxn, dg = _rms_bwd(rowbuf[...], g_ref[...], acc[...], d)
            dg_ref[...] += dg
            dx_ref[...] = dxn
            fetch(do_hbm)
            dx_ref[...] += rowbuf[...]

    wspec = pl.BlockSpec((1, fp, d), lambda i, j: (j, 0, 0))
    row = pl.BlockSpec((tm, d), lambda i, j: (i, 0))
    vec = pl.BlockSpec((1, d), lambda i, j: (0, 0))
    ff = pl.BlockSpec((tm, fp), lambda i, j: (i, j))
    ffs = jax.ShapeDtypeStruct((t, nb * fp), BF16)
    return pl.pallas_call(
        body, name=name, grid=(t // tm, nb),
        in_specs=[ANY, ANY, vec, ff, ff, wspec, wspec, wspec],
        out_specs=[row, vec, ff, ff, ff],
        out_shape=[jax.ShapeDtypeStruct((t, d), F32), jax.ShapeDtypeStruct((1, d), F32), ffs, ffs, ffs],
        scratch_shapes=[pltpu.VMEM((tm, d), F32), pltpu.VMEM((tm, d), F32), pltpu.VMEM((tm, d), BF16),
                        pltpu.SemaphoreType.DMA],
        compiler_params=_params(("arbitrary", "arbitrary")),
    )(dout, x, g, a, b, wg_t, wu_t, wd)


def _matmul_tn(lhs, rhs, bm, bn, tk, out_dtype, name, rhs_scale=None):
    t, m = lhs.shape
    n = rhs.shape[1]
    nk = t // tk

    def body(l_ref, r_ref, o_ref, acc):
        k = pl.program_id(2)

        @pl.when(k == 0)
        def _():
            acc[...] = jnp.zeros_like(acc)

        r = r_ref[...] if rhs_scale is None else rhs_scale * r_ref[...]
        acc[...] += _tn(l_ref[...].astype(BF16), r.astype(BF16))

        @pl.when(k == nk - 1)
        def _():
            o_ref[...] = acc[...].astype(out_dtype)

    return pl.pallas_call(
        body, name=name, grid=(m // bm, n // bn, nk),
        in_specs=[pl.BlockSpec((tk, bm), lambda i, j, k: (k, i)), pl.BlockSpec((tk, bn), lambda i, j, k: (k, j))],
        out_specs=pl.BlockSpec((bm, bn), lambda i, j, k: (i, j)),
        out_shape=jax.ShapeDtypeStruct((m, n), out_dtype),
        scratch_shapes=[pltpu.VMEM((bm, bn), F32)],
        compiler_params=_params(("arbitrary", "arbitrary", "arbitrary")),
    )(lhs, rhs)


def _inproj_fwd(x, g, w_t, splits, tm):
    t, d = x.shape
    offs = [sum(splits[:k]) for k in range(len(splits))]

    def body(x_ref, g_ref, w_ref, hn_ref, *z_refs):
        hn = _rms_fwd(x_ref[...], g_ref[...], d).astype(BF16)
        hn_ref[...] = hn
        for z_ref, o, n in zip(z_refs, offs, splits):
            z_ref[...] = _nt(hn, w_ref[o:o + n, :])

    row = pl.BlockSpec((tm, d), lambda i: (i, 0))
    return pl.pallas_call(
        body, name="inproj_fwd", grid=(t // tm,),
        in_specs=[row, pl.BlockSpec((1, d), lambda i: (0, 0)), WHOLE_VMEM],
        out_specs=[row] + [pl.BlockSpec((tm, n), lambda i: (i, 0)) for n in splits],
        out_shape=[jax.ShapeDtypeStruct((t, d), BF16)] + [jax.ShapeDtypeStruct((t, n), F32) for n in splits],
        compiler_params=_params(("arbitrary",)),
    )(x, g, w_t)


def _inproj_bwd(dzs, x, g, w_t, dres, splits, tm):
    t, d = x.shape
    offs = [sum(splits[:k]) for k in range(len(splits))]
    ni = sum(splits)
    nz = len(splits)

    def body(*refs):
        dz_refs = refs[:nz]
        x_ref, g_ref, w_ref, dres_ref, dx_ref, dg_ref, dzc_ref = refs[nz:]
        dhn = jnp.zeros((tm, d), F32)
        for dz_ref, o, n in zip(dz_refs, offs, splits):
            dz = dz_ref[...].astype(BF16)
            dzc_ref[:, o:o + n] = dz
            dhn += _nn(dz, w_ref[o:o + n, :])
        dx, dg = _rms_bwd(x_ref[...], g_ref[...], dhn, d)
        dx_ref[...] = dres_ref[...] + dx

        @pl.when(pl.program_id(0) == 0)
        def _():
            dg_ref[...] = jnp.zeros_like(dg_ref)

        dg_ref[...] += dg

    row = pl.BlockSpec((tm, d), lambda i: (i, 0))
    vec = pl.BlockSpec((1, d), lambda i: (0, 0))
    return pl.pallas_call(
        body, name="inproj_bwd", grid=(t // tm,),
        in_specs=[pl.BlockSpec((tm, n), lambda i: (i, 0)) for n in splits] + [row, vec, WHOLE_VMEM, row],
        out_specs=[row, vec, pl.BlockSpec((tm, ni), lambda i: (i, 0))],
        out_shape=[jax.ShapeDtypeStruct((t, d), F32), jax.ShapeDtypeStruct((1, d), F32),
                   jax.ShapeDtypeStruct((t, ni), BF16)],
        compiler_params=_params(("arbitrary",)),
    )(*dzs, x, g, w_t, dres)


def _tril_bf16(ws_ref, grp):
    rows = lax.broadcasted_iota(jnp.int32, (CHUNK, CHUNK), 0)
    cols = lax.broadcasted_iota(jnp.int32, (CHUNK, CHUNK), 1)
    return jnp.where(rows >= cols, ws_ref[grp], 0.0).astype(BF16)


def _gmlp_mix(zuv_ref, gv_ref, ws_ref, bias_ref, v_scr, mixed_scr, tm, w, groups):
    u = _gelu(zuv_ref[:, 0:w])
    v0 = _gelu(zuv_ref[:, w:2 * w])
    v_scr[...] = _rms_fwd(v0, gv_ref[...], w).astype(BF16)
    for grp in range(groups):
        wsm = _tril_bf16(ws_ref, grp)
        lanes = slice(grp * CHUNK, (grp + 1) * CHUNK)
        for c in range(tm // CHUNK):
            rows = slice(c * CHUNK, (c + 1) * CHUNK)
            mixed_scr[rows, lanes] = _nn(wsm, v_scr[rows, lanes]) + bias_ref[:, lanes]
    return u, v0


def _gmlp_fwd(zuv, gv, ws, bias, gout, tm):
    t, w2 = zuv.shape
    w = w2 // 2
    groups = ws.shape[0]

    def body(zuv_ref, gv_ref, ws_ref, bias_ref, go_ref, y_ref, v_scr, mixed_scr):
        u, _ = _gmlp_mix(zuv_ref, gv_ref, ws_ref, bias_ref, v_scr, mixed_scr, tm, w, groups)
        ya = u * mixed_scr[...]
        for grp in range(groups):
            lanes = slice(grp * CHUNK, (grp + 1) * CHUNK)
            y_ref[:, lanes] = _rms_fwd(ya[:, lanes], go_ref[:, lanes], CHUNK).astype(BF16)

    const2 = lambda i: (0, 0)
    return pl.pallas_call(
        body, name="gmlp_fwd", grid=(t // tm,),
        in_specs=[pl.BlockSpec((tm, w2), lambda i: (i, 0)), pl.BlockSpec((1, w), const2),
                  pl.BlockSpec((groups, CHUNK, CHUNK), lambda i: (0, 0, 0)),
                  pl.BlockSpec((CHUNK, w), const2), pl.BlockSpec((1, w), const2)],
        out_specs=pl.BlockSpec((tm, w), lambda i: (i, 0)),
        out_shape=jax.ShapeDtypeStruct((t, w), BF16),
        scratch_shapes=[pltpu.VMEM((tm, w), BF16), pltpu.VMEM((tm, w), F32)],
        compiler_params=_params(("arbitrary",)),
    )(zuv, gv, ws, bias, gout)


def _gmlp_bwd(dy, zuv, gv, ws, bias, gout, tm):
    t, w2 = zuv.shape
    w = w2 // 2
    groups = ws.shape[0]

    def body(dy_ref, zuv_ref, gv_ref, ws_ref, bias_ref, go_ref,
             dz_ref, dws_ref, dbias_ref, dgv_ref, dgo_ref, v_scr, mixed_scr, dmix_scr, dv_scr):
        @pl.when(pl.program_id(0) == 0)
        def _():
            dws_ref[...] = jnp.zeros_like(dws_ref)
            dbias_ref[...] = jnp.zeros_like(dbias_ref)
            dgv_ref[...] = jnp.zeros_like(dgv_ref)
            dgo_ref[...] = jnp.zeros_like(dgo_ref)

        u, v0 = _gmlp_mix(zuv_ref, gv_ref, ws_ref, bias_ref, v_scr, mixed_scr, tm, w, groups)
        mixed = mixed_scr[...]
        ya = u * mixed
        for grp in range(groups):
            lanes = slice(grp * CHUNK, (grp + 1) * CHUNK)
            dya, dgo = _rms_bwd(ya[:, lanes], go_ref[:, lanes], dy_ref[:, lanes], CHUNK)
            dgo_ref[:, lanes] += dgo
            dz_ref[:, lanes] = dya * mixed[:, lanes] * _gelu_grad(zuv_ref[:, lanes])
            dmix_scr[:, lanes] = dya * u[:, lanes]
        for grp in range(groups):
            wsm = _tril_bf16(ws_ref, grp)
            lanes = slice(grp * CHUNK, (grp + 1) * CHUNK)
            dws = jnp.zeros((CHUNK, CHUNK), F32)
            dbias = jnp.zeros((CHUNK, CHUNK), F32)
            for c in range(tm // CHUNK):
                rows = slice(c * CHUNK, (c + 1) * CHUNK)
                dm = dmix_scr[rows, lanes]
                dmb = dm.astype(BF16)
                dv_scr[rows, lanes] = _tn(wsm, dmb)
                dws += _nt(dmb, v_scr[rows, lanes])
                dbias += dm
            rr = lax.broadcasted_iota(jnp.int32, (CHUNK, CHUNK), 0)
            cc = lax.broadcasted_iota(jnp.int32, (CHUNK, CHUNK), 1)
            dws_ref[grp] += jnp.where(rr >= cc, dws, 0.0)
            dbias_ref[grp] += jnp.sum(dbias, axis=1, keepdims=True)
        dv0, dgv = _rms_bwd(v0, gv_ref[...], dv_scr[...], w)
        dgv_ref[...] += dgv
        dz_ref[:, w:2 * w] = dv0 * _gelu_grad(zuv_ref[:, w:2 * w])

    const2 = lambda i: (0, 0)
    const3 = lambda i: (0, 0, 0)
    return pl.pallas_call(
        body, name="gmlp_bwd", grid=(t // tm,),
        in_specs=[pl.BlockSpec((tm, w), lambda i: (i, 0)), pl.BlockSpec((tm, w2), lambda i: (i, 0)),
                  pl.BlockSpec((1, w), const2), pl.BlockSpec((groups, CHUNK, CHUNK), const3),
                  pl.BlockSpec((CHUNK, w), const2), pl.BlockSpec((1, w), const2)],
        out_specs=[pl.BlockSpec((tm, w2), lambda i: (i, 0)), pl.BlockSpec((groups, CHUNK, CHUNK), const3),
                   pl.BlockSpec((groups, CHUNK, 1), const3), pl.BlockSpec((1, w), const2), pl.BlockSpec((1, w), const2)],
        out_shape=[jax.ShapeDtypeStruct((t, w2), F32), jax.ShapeDtypeStruct((groups, CHUNK, CHUNK), F32),
                   jax.ShapeDtypeStruct((groups, CHUNK, 1), F32), jax.ShapeDtypeStruct((1, w), F32),
                   jax.ShapeDtypeStruct((1, w), F32)],
        scratch_shapes=[pltpu.VMEM((tm, w), BF16), pltpu.VMEM((tm, w), F32),
                        pltpu.VMEM((tm, w), F32), pltpu.VMEM((tm, w), F32)],
        compiler_params=_params(("arbitrary",)),
    )(dy, zuv, gv, ws, bias, gout)


def _rot(x, m_lo, m_hi):
    return pltpu.roll(x, LANE - ROPE // 2, 1) * m_lo + pltpu.roll(x, ROPE // 2, 1) * m_hi


def _rope_tables(pos_ref, freq_ref):
    ang = pos_ref[...] * freq_ref[...]
    return jnp.cos(ang), jnp.sin(ang)


def _mla_proj_fwd(cq, ckv, krw, pos, freq, masks, gq, gkv, wq_t, wkv_t, gqh, gkh, tm):
    t, rq = cq.shape
    rkv = ckv.shape[1]
    heads = wq_t.shape[0]

    def body(cq_ref, ckv_ref, kr_ref, pos_ref, freq_ref, mk_ref, gq_ref, gkv_ref, wq_ref, wkv_ref,
             gqh_ref, gkh_ref, q_ref, k_ref, v_ref):
        cos, sin = _rope_tables(pos_ref, freq_ref)
        m_lo, m_hi = mk_ref[0:1, :], mk_ref[1:2, :]
        cqn = _rms_fwd(cq_ref[...], gq_ref[...], rq).astype(BF16)
        ckvn = _rms_fwd(ckv_ref[...], gkv_ref[...], rkv).astype(BF16)
        kr = kr_ref[...]
        kr_ss = jnp.sum(kr * kr, axis=-1, keepdims=True)
        for h in range(heads):
            qh = _nt(cqn, wq_ref[h])
            qn = qh * _rstd(qh, QK) * gqh_ref[...]
            qr = qn[:, LANE:]
            q_ref[h, :, 0:LANE] = qn[:, 0:LANE].astype(BF16)
            q_ref[h, :, LANE:] = (qr * cos + _rot(qr, m_lo, m_hi) * sin).astype(BF16)
            kvh = _nt(ckvn, wkv_ref[h])
            kn = kvh[:, 0:LANE]
            rk = lax.rsqrt((jnp.sum(kn * kn, axis=-1, keepdims=True) + kr_ss) * (1.0 / QK) + EPS)
            k_ref[h, :, 0:LANE] = (kn * rk * gkh_ref[:, 0:LANE]).astype(BF16)
            krn = kr * rk * gkh_ref[:, LANE:]
            k_ref[h, :, LANE:] = (krn * cos + _rot(krn, m_lo, m_hi) * sin).astype(BF16)
            v_ref[h] = kvh[:, LANE:].astype(BF16)

    c2 = lambda i: (0, 0)
    c3 = lambda i: (0, 0, 0)
    return pl.pallas_call(
        body, name="mla_proj_fwd", grid=(t // tm,),
        in_specs=[pl.BlockSpec((tm, rq), lambda i: (i, 0)), pl.BlockSpec((tm, rkv), lambda i: (i, 0)),
                  pl.BlockSpec((tm, LANE), lambda i: (i, 0)), pl.BlockSpec((tm, 1), lambda i: (i, 0)),
                  pl.BlockSpec((1, LANE), c2), pl.BlockSpec((2, LANE), c2),
                  pl.BlockSpec((1, rq), c2), pl.BlockSpec((1, rkv), c2),
                  pl.BlockSpec((heads, HEADW, rq), c3), pl.BlockSpec((heads, HEADW, rkv), c3),
                  pl.BlockSpec((1, HEADW), c2), pl.BlockSpec((1, HEADW), c2)],
        out_specs=[pl.BlockSpec((heads, tm, HEADW), lambda i: (0, i, 0)),
                   pl.BlockSpec((heads, tm, HEADW), lambda i: (0, i, 0)),
                   pl.BlockSpec((heads, tm, VHEAD), lambda i: (0, i, 0))],
        out_shape=[jax.ShapeDtypeStruct((heads, t, HEADW), BF16), jax.ShapeDtypeStruct((heads, t, HEADW), BF16),
                   jax.ShapeDtypeStruct((heads, t, VHEAD), BF16)],
        compiler_params=_params(("arbitrary",)),
    )(cq, ckv, krw, pos, freq, masks, gq, gkv, wq_t, wkv_t, gqh, gkh)


def _mla_proj_bwd(dq, dk, dv, cq, ckv, krw, pos, freq, masks, gq, gkv, wq_t, wkv_t, gqh, gkh, tm):
    t, rq = cq.shape
    rkv = ckv.shape[1]
    heads = wq_t.shape[0]

    def body(dq_ref, dk_ref, dv_ref, cq_ref, ckv_ref, kr_ref, pos_ref, freq_ref, mk_ref, gq_ref, gkv_ref,
             wq_ref, wkv_ref, gqh_ref, gkh_ref,
             dcq_ref, dckv_ref, dkr_ref, dwq_ref, dwkv_ref, dgq_ref, dgkv_ref, dgqh_ref, dgkh_ref):
        @pl.when(pl.program_id(0) == 0)
        def _():
            for r in (dwq_ref, dwkv_ref, dgq_ref, dgkv_ref, dgqh_ref, dgkh_ref):
                r[...] = jnp.zeros_like(r)

        cos, sin = _rope_tables(pos_ref, freq_ref)
        m_lo, m_hi = mk_ref[0:1, :], mk_ref[1:2, :]

        def unrope(dy):
            return dy * cos - _rot(dy * sin, m_lo, m_hi)

        cqn = _rms_fwd(cq_ref[...], gq_ref[...], rq).astype(BF16)
        ckvn = _rms_fwd(ckv_ref[...], gkv_ref[...], rkv).astype(BF16)
        kr = kr_ref[...]
        kr_ss = jnp.sum(kr * kr, axis=-1, keepdims=True)
        dcqn = jnp.zeros((tm, rq), F32)
        dckvn = jnp.zeros((tm, rkv), F32)
        dkr = jnp.zeros((tm, LANE), F32)
        for h in range(heads):
            qh = _nt(cqn, wq_ref[h])
            dqn = jnp.concatenate([dq_ref[h, :, 0:LANE], unrope(dq_ref[h, :, LANE:])], axis=1)
            dqh, dg = _rms_bwd(qh, gqh_ref[...], dqn, QK)
            dgqh_ref[...] += dg
            dqh = dqh.astype(BF16)
            dcqn += _nn(dqh, wq_ref[h])
            dwq_ref[h] += _tn(dqh, cqn)

            kvh = _nt(ckvn, wkv_ref[h])
            kn = kvh[:, 0:LANE]
            rk = lax.rsqrt((jnp.sum(kn * kn, axis=-1, keepdims=True) + kr_ss) * (1.0 / QK) + EPS)
            dkn_n = dk_ref[h, :, 0:LANE]
            dkr_n = unrope(dk_ref[h, :, LANE:])
            knh, krh = kn * rk, kr * rk
            dgkh_ref[:, 0:LANE] += jnp.sum(dkn_n * knh, axis=0, keepdims=True)
            dgkh_ref[:, LANE:] += jnp.sum(dkr_n * krh, axis=0, keepdims=True)
            dkn_g, dkr_g = dkn_n * gkh_ref[:, 0:LANE], dkr_n * gkh_ref[:, LANE:]
            proj = (jnp.sum(dkn_g * knh, axis=-1, keepdims=True)
                    + jnp.sum(dkr_g * krh, axis=-1, keepdims=True)) * (1.0 / QK)
            dkr += rk * (dkr_g - krh * proj)
            dkvh = jnp.concatenate([rk * (dkn_g - knh * proj), dv_ref[h]], axis=1).astype(BF16)
            dckvn += _nn(dkvh, wkv_ref[h])
            dwkv_ref[h] += _tn(dkvh, ckvn)
        dkr_ref[...] = dkr
        dcq, dg = _rms_bwd(cq_ref[...], gq_ref[...], dcqn, rq)
        dcq_ref[...] = dcq
        dgq_ref[...] += dg
        dckv, dg = _rms_bwd(ckv_ref[...], gkv_ref[...], dckvn, rkv)
        dckv_ref[...] = dckv
        dgkv_ref[...] += dg

    c2 = lambda i: (0, 0)
    c3 = lambda i: (0, 0, 0)
    hq = pl.BlockSpec((heads, tm, HEADW), lambda i: (0, i, 0))
    return pl.pallas_call(
        body, name="mla_proj_bwd", grid=(t // tm,),
        in_specs=[hq, hq, pl.BlockSpec((heads, tm, VHEAD), lambda i: (0, i, 0)),
                  pl.BlockSpec((tm, rq), lambda i: (i, 0)), pl.BlockSpec((tm, rkv), lambda i: (i, 0)),
                  pl.BlockSpec((tm, LANE), lambda i: (i, 0)), pl.BlockSpec((tm, 1), lambda i: (i, 0)),
                  pl.BlockSpec((1, LANE), c2), pl.BlockSpec((2, LANE), c2),
                  pl.BlockSpec((1, rq), c2), pl.BlockSpec((1, rkv), c2),
                  pl.BlockSpec((heads, HEADW, rq), c3), pl.BlockSpec((heads, HEADW, rkv), c3),
                  pl.BlockSpec((1, HEADW), c2), pl.BlockSpec((1, HEADW), c2)],
        out_specs=[pl.BlockSpec((tm, rq), lambda i: (i, 0)), pl.BlockSpec((tm, rkv), lambda i: (i, 0)),
                   pl.BlockSpec((tm, LANE), lambda i: (i, 0)),
                   pl.BlockSpec((heads, HEADW, rq), c3), pl.BlockSpec((heads, HEADW, rkv), c3),
                   pl.BlockSpec((1, rq), c2), pl.BlockSpec((1, rkv), c2),
                   pl.BlockSpec((1, HEADW), c2), pl.BlockSpec((1, HEADW), c2)],
        out_shape=[jax.ShapeDtypeStruct((t, rq), F32), jax.ShapeDtypeStruct((t, rkv), F32),
                   jax.ShapeDtypeStruct((t, LANE), F32),
                   jax.ShapeDtypeStruct((heads, HEADW, rq), F32), jax.ShapeDtypeStruct((heads, HEADW, rkv), F32),
                   jax.ShapeDtypeStruct((1, rq), F32), jax.ShapeDtypeStruct((1, rkv), F32),
                   jax.ShapeDtypeStruct((1, HEADW), F32), jax.ShapeDtypeStruct((1, HEADW), F32)],
        compiler_params=_params(("arbitrary",)),
    )(dq, dk, dv, cq, ckv, krw, pos, freq, masks, gq, gkv, wq_t, wkv_t, gqh, gkh)


def _lower_triangle(blk):
    return lax.broadcasted_iota(jnp.int32, (blk, blk), 0) >= lax.broadcasted_iota(jnp.int32, (blk, blk), 1)


def _attn_fwd(q, k, v, seq, blk):
    heads, t, _ = q.shape
    scale = QK ** -0.5
    nblk = seq // blk

    def body(q_ref, k_ref, v_ref, o_ref, lse_ref):
        tri = _lower_triangle(blk)
        for qi in range(nblk):
            rows = slice(qi * blk, (qi + 1) * blk)
            before = slice(0, qi * blk)
            qb = q_ref[0, rows, :]
            s_d = jnp.where(tri, _nt(qb, k_ref[0, rows, :]) * scale, -1e30)
            m = jnp.max(s_d, axis=-1, keepdims=True)
            if qi:
                s_b = _nt(qb, k_ref[0, before, :]) * scale
                m = jnp.maximum(m, jnp.max(s_b, axis=-1, keepdims=True))
                p_b = jnp.exp(s_b - m)
            p_d = jnp.exp(s_d - m)
            l = jnp.sum(p_d, axis=-1, keepdims=True)
            acc = _nn(p_d.astype(BF16), v_ref[0, rows, :])
            if qi:
                l += jnp.sum(p_b, axis=-1, keepdims=True)
                acc += _nn(p_b.astype(BF16), v_ref[0, before, :])
            o_ref[0, rows, :] = acc / l
            lse_ref[0, rows, :] = m + jnp.log(l)

    return pl.pallas_call(
        body, name="attn_fwd", grid=(heads, t // seq),
        in_specs=[pl.BlockSpec((1, seq, HEADW), lambda h, b: (h, b, 0)),
                  pl.BlockSpec((1, seq, HEADW), lambda h, b: (h, b, 0)),
                  pl.BlockSpec((1, seq, VHEAD), lambda h, b: (h, b, 0))],
        out_specs=[pl.BlockSpec((1, seq, VHEAD), lambda h, b: (h, b, 0)),
                   pl.BlockSpec((1, seq, 1), lambda h, b: (h, b, 0))],
        out_shape=[jax.ShapeDtypeStruct((heads, t, VHEAD), F32), jax.ShapeDtypeStruct((heads, t, 1), F32)],
        compiler_params=_params(("arbitrary", "arbitrary")),
    )(q, k, v)


def _attn_bwd(q, k, v, do, lse, delta, seq, blk):
    heads, t, _ = q.shape
    scale = QK ** -0.5
    nblk = seq // blk

    def body(q_ref, k_ref, v_ref, do_ref, lse_ref, dl_ref, dq_ref, dk_ref, dv_ref):
        tri = _lower_triangle(blk)
        dk_ref[...] = jnp.zeros_like(dk_ref)
        dv_ref[...] = jnp.zeros_like(dv_ref)
        for qi in range(nblk):
            rows = slice(qi * blk, (qi + 1) * blk)
            qb = q_ref[0, rows, :]
            dob = do_ref[0, rows, :]
            lse_b = lse_ref[0, rows, :]
            dl_b = dl_ref[0, rows, :]
            dq = jnp.zeros((blk, HEADW), F32)
            for keys, masked in ((slice(0, qi * blk), False), (rows, True)):
                if keys.stop == keys.start:
                    continue
                kb = k_ref[0, keys, :]
                p = jnp.exp(_nt(qb, kb) * scale - lse_b)
                if masked:
                    p = jnp.where(tri, p, 0.0)
                dp = _nt(dob, v_ref[0, keys, :])
                ds = (p * (dp - dl_b) * scale).astype(BF16)
                dv_ref[0, keys, :] += _tn(p.astype(BF16), dob)
                dk_ref[0, keys, :] += _tn(ds, qb)
                dq += _nn(ds, kb)
            dq_ref[0, rows, :] = dq

    hq = pl.BlockSpec((1, seq, HEADW), lambda h, b: (h, b, 0))
    hv = pl.BlockSpec((1, seq, VHEAD), lambda h, b: (h, b, 0))
    h1 = pl.BlockSpec((1, seq, 1), lambda h, b: (h, b, 0))
    return pl.pallas_call(
        body, name="attn_bwd", grid=(heads, t // seq),
        in_specs=[hq, hq, hv, hv, h1, h1],
        out_specs=[hq, hq, hv],
        out_shape=[jax.ShapeDtypeStruct((heads, t, HEADW), F32), jax.ShapeDtypeStruct((heads, t, HEADW), F32),
                   jax.ShapeDtypeStruct((heads, t, VHEAD), F32)],
        compiler_params=_params(("arbitrary", "arbitrary")),
    )(q, k, v, do, lse, delta)


def _out_fwd(ya, o, gb, w_out, x1, tm):
    t, w = ya.shape
    heads = o.shape[0]
    d = x1.shape[1]

    def body(ya_ref, o_ref, gb_ref, w_ref, x1_ref, x2_ref, yc_ref):
        yc_ref[:, 0:w] = ya_ref[...]
        for h in range(heads):
            lanes = slice(h * VHEAD, (h + 1) * VHEAD)
            yc_ref[:, w + h * VHEAD:w + (h + 1) * VHEAD] = _rms_fwd(o_ref[h], gb_ref[:, lanes], VHEAD).astype(BF16)
        x2_ref[...] = x1_ref[...] + _nn(yc_ref[...], w_ref[...])

    wy = w + heads * VHEAD
    row = pl.BlockSpec((tm, d), lambda i: (i, 0))
    return pl.pallas_call(
        body, name="out_fwd", grid=(t // tm,),
        in_specs=[pl.BlockSpec((tm, w), lambda i: (i, 0)), pl.BlockSpec((heads, tm, VHEAD), lambda i: (0, i, 0)),
                  pl.BlockSpec((1, heads * VHEAD), lambda i: (0, 0)), WHOLE_VMEM, row],
        out_specs=[row, pl.BlockSpec((tm, wy), lambda i: (i, 0))],
        out_shape=[jax.ShapeDtypeStruct((t, d), F32), jax.ShapeDtypeStruct((t, wy), BF16)],
        compiler_params=_params(("arbitrary",)),
    )(ya, o, gb, w_out, x1)


def _out_bwd(dx2, o, gb, w_out, w, tm):
    t, d = dx2.shape
    heads = o.shape[0]

    def body(dx_ref, o_ref, gb_ref, w_ref, dya_ref, do_ref, dl_ref, dgb_ref):
        @pl.when(pl.program_id(0) == 0)
        def _():
            dgb_ref[...] = jnp.zeros_like(dgb_ref)

        dyc = _nt(dx_ref[...].astype(BF16), w_ref[...])
        dya_ref[...] = dyc[:, 0:w]
        for h in range(heads):
            lanes = slice(h * VHEAD, (h + 1) * VHEAD)
            oh = o_ref[h]
            doh, dg = _rms_bwd(oh, gb_ref[:, lanes], dyc[:, w + h * VHEAD:w + (h + 1) * VHEAD], VHEAD)
            dgb_ref[:, lanes] += dg
            do_ref[h] = doh.astype(BF16)
            dl_ref[h] = jnp.sum(doh * oh, axis=-1, keepdims=True)

    ho = pl.BlockSpec((heads, tm, VHEAD), lambda i: (0, i, 0))
    vec = pl.BlockSpec((1, heads * VHEAD), lambda i: (0, 0))
    return pl.pallas_call(
        body, name="out_bwd", grid=(t // tm,),
        in_specs=[pl.BlockSpec((tm, d), lambda i: (i, 0)), ho, vec, WHOLE_VMEM],
        out_specs=[pl.BlockSpec((tm, w), lambda i: (i, 0)), ho, pl.BlockSpec((heads, tm, 1), lambda i: (0, i, 0)), vec],
        out_shape=[jax.ShapeDtypeStruct((t, w), F32), jax.ShapeDtypeStruct((heads, t, VHEAD), BF16),
                   jax.ShapeDtypeStruct((heads, t, 1), F32), jax.ShapeDtypeStruct((1, heads * VHEAD), F32)],
        compiler_params=_params(("arbitrary",)),
    )(dx2, o, gb, w_out)


def _loss_head(y, target, tm):
    t, d = y.shape

    def body(y_ref, t_ref, dy_ref, loss_ref):
        @pl.when(pl.program_id(0) == 0)
        def _():
            loss_ref[...] = jnp.zeros_like(loss_ref)

        err = y_ref[...] - t_ref[...]
        dy_ref[...] = err * (1.0 / d)
        part = jnp.sum(jnp.sum(err * err, axis=-1, keepdims=True) * (1.0 / d), axis=0, keepdims=True)
        loss_ref[...] += 0.5 * part

    row = pl.BlockSpec((tm, d), lambda i: (i, 0))
    return pl.pallas_call(
        body, name="loss_head", grid=(t // tm,),
        in_specs=[row, row], out_specs=[row, pl.BlockSpec((1, 1), lambda i: (0, 0))],
        out_shape=[jax.ShapeDtypeStruct((t, d), F32), jax.ShapeDtypeStruct((1, 1), F32)],
        compiler_params=_params(("arbitrary",)),
    )(y, target)


def _place():
    return lax.axis_index("x"), lax.axis_index("y"), lax.axis_index("c")


def _all_gather(arrs, name):
    n = len(arrs)

    def body(*refs):
        ins, outs = refs[:n], refs[n:2 * n]
        send_sems, recv_sems, local_sems = refs[2 * n:]
        x, y, c = _place()
        sibling = (x, y, 1 - c)
        chips = [(1 - x, y), (x, 1 - y), (1 - x, 1 - y)]

        def blk(a, px, py, pc):
            return outs[a].at[4 * px + 2 * py + pc]

        def copy(a, k, block, to, src=None):
            return pltpu.make_async_remote_copy(
                src_ref=blk(a, *block) if src is None else src, dst_ref=blk(a, *block),
                send_sem=send_sems.at[a, k], recv_sem=recv_sems.at[a, k], device_id=to, device_id_type=MESH)

        mine = [pltpu.make_async_copy(ins[a], blk(a, x, y, c), local_sems.at[a]) for a in range(n)]
        for cp in mine:
            cp.start()
        first = []
        for a in range(n):
            first.append(copy(a, 0, (x, y, c), sibling, src=ins[a]))
            first += [copy(a, 1 + j, (x, y, c), (*chip, c), src=ins[a]) for j, chip in enumerate(chips)]
        for cp in first:
            cp.start()
        passed = []
        for j, chip in enumerate(chips):
            for a in range(n):
                copy(a, 1 + j, (*chip, c), (x, y, c)).wait_recv()
                cp = copy(a, 4 + j, (*chip, c), sibling)
                cp.start()
                passed.append(cp)
        for a in range(n):
            copy(a, 0, sibling, (x, y, c)).wait_recv()
            for j, chip in enumerate(chips):
                copy(a, 4 + j, (*chip, 1 - c), (x, y, c)).wait_recv()
        for cp in first + passed:
            cp.wait_send()
        for cp in mine:
            cp.wait()

    return pl.pallas_call(
        body, name=name,
        in_specs=[ANY] * n, out_specs=[ANY] * n,
        out_shape=[jax.ShapeDtypeStruct((N_DEV,) + a.shape, a.dtype) for a in arrs],
        scratch_shapes=[pltpu.SemaphoreType.DMA((n, 7)), pltpu.SemaphoreType.DMA((n, 7)),
                        pltpu.SemaphoreType.DMA((n,))],
    )(*arrs)


HBM = pl.BlockSpec(memory_space=pltpu.HBM)
SEM = pl.BlockSpec(memory_space=pltpu.SEMAPHORE)
DATAFLOW = pltpu.SideEffectType.DATAFLOW_SIDE_EFFECTING


def _plan_copies(plan, refs, send_sems, recv_sems):
    cps = []
    for i, (sb, sblk, db, dblk, dev) in enumerate(plan(*_place())):
        cps.append(pltpu.make_async_remote_copy(
            src_ref=refs[sb] if sblk is None else refs[sb].at[sblk], dst_ref=refs[db].at[dblk],
            send_sem=send_sems.at[i], recv_sem=recv_sems.at[i], device_id=dev, device_id_type=MESH))
    return cps


def _push_start(bufs, plan, ncopy, name, deps=()):
    nb = len(bufs)

    def body(*refs):
        outs = refs[nb + len(deps):]
        for cp in _plan_copies(plan, refs[:nb], outs[0], outs[1]):
            cp.start()
        outs[-1][...] = jnp.zeros_like(outs[-1])

    res = pl.pallas_call(
        body, name=name,
        out_shape=(pltpu.SemaphoreType.DMA((ncopy,)), pltpu.SemaphoreType.DMA((ncopy,)),
                   *[pltpu.HBM(b.shape, b.dtype) for b in bufs], jax.ShapeDtypeStruct((SUBLANE, LANE), F32)),
        in_specs=[HBM] * nb + [ANY] * len(deps),
        out_specs=(SEM, SEM, *[HBM] * nb, WHOLE_VMEM),
        input_output_aliases={i: 2 + i for i in range(nb)},
        compiler_params=pltpu.CompilerParams(has_side_effects=DATAFLOW),
    )(*[pltpu.with_memory_space_constraint(b, pltpu.HBM) for b in bufs], *deps)
    return res[0], res[1], list(res[2:2 + nb]), res[-1]


def _push_wait(send_sems, recv_sems, bufs, plan, after, name):
    nb = len(bufs)

    def body(*refs):
        for cp in _plan_copies(plan, refs[:nb], refs[nb], refs[nb + 1]):
            cp.wait_send()
            cp.wait_recv()

    res = pl.pallas_call(
        body, name=name,
        out_shape=[pltpu.HBM(b.shape, b.dtype) for b in bufs],
        in_specs=[HBM] * nb + [SEM, SEM, ANY], out_specs=[HBM] * nb,
        input_output_aliases={i: i for i in range(nb)},
        compiler_params=pltpu.CompilerParams(has_side_effects=DATAFLOW),
    )(*bufs, send_sems, recv_sems, after)
    return list(res)


def _other_chips(x, y):
    return ((1 - x, y), (x, 1 - y), (1 - x, 1 - y))


class _Exchange:
    def __init__(self, bufs, plan, ncopy, name, deps=()):
        self.plan, self.name = plan, name
        self.send, self.recv, self.bufs, self.token = _push_start(bufs, plan, ncopy, name + "_start", deps)

    def wait(self, after):
        return _push_wait(self.send, self.recv, self.bufs, self.plan, after, self.name + "_wait")


def _gather_ici(shards, me, name, deps=()):
    n = len(shards)
    lands = [lax.dynamic_update_slice(lax.empty((N_DEV,) + s.shape, s.dtype), s[None], (me, 0, 0)) for s in shards]

    def plan(x, y, c):
        return [(a, None, n + a, 4 * x + 2 * y + c, (px, py, c)) for a in range(n) for px, py in _other_chips(x, y)]

    return _Exchange(list(shards) + lands, plan, 3 * n, name, deps)


def _gather_d2d(lands, name, deps=()):
    n = len(lands)

    def plan(x, y, c):
        blocks = [4 * x + 2 * y + c] + [4 * px + 2 * py + c for px, py in _other_chips(x, y)]
        return [(a, b, a, b, (x, y, 1 - c)) for a in range(n) for b in blocks]

    return _Exchange(list(lands), plan, 4 * n, name, deps)


def _reduce_d2d(grads, name, deps=()):
    n = len(grads)
    lands = [lax.empty((4,) + g.shape[1:], g.dtype) for g in grads]

    def plan(x, y, c):
        return [(a, 2 * s + (1 - c), n + a, s, (x, y, 1 - c)) for a in range(n) for s in range(4)]

    return _Exchange(list(grads) + lands, plan, 4 * n, name, deps)


def _reduce_ici(chip, name, deps=()):
    n = len(chip)
    lands = [lax.empty((3,) + g.shape[1:], g.dtype) for g in chip]

    def plan(x, y, c):
        return [(a, 2 * px + py, n + a, k, (px, py, c))
                for a in range(n) for k, (px, py) in enumerate(_other_chips(x, y))]

    return _Exchange(list(chip) + lands, plan, 3 * n, name, deps)


def _pair_add(full, got, core, name):
    _, r, cdim = full.shape
    br = _row_block(r, 512)

    def body(c_ref, f_ref, g_ref, o_ref):
        o_ref[...] = (f_ref[...].astype(F32) + g_ref[...].astype(F32)).astype(o_ref.dtype)

    return pl.pallas_call(
        body, name=name,
        grid_spec=pltpu.PrefetchScalarGridSpec(
            num_scalar_prefetch=1, grid=(4, r // br),
            in_specs=[pl.BlockSpec((1, br, cdim), lambda s, i, c_ref: (2 * s + c_ref[0], i, 0)),
                      pl.BlockSpec((1, br, cdim), lambda s, i, c_ref: (s, i, 0))],
            out_specs=pl.BlockSpec((1, br, cdim), lambda s, i, c_ref: (s, i, 0))),
        out_shape=jax.ShapeDtypeStruct((4, r, cdim), full.dtype),
        compiler_params=_params(("arbitrary", "arbitrary")),
    )(core, full, got)


def _sum_owned(chip, got, slot, rows, name):
    cdim = chip.shape[2]
    br = _row_block(rows, 512)

    def body(s_ref, c_ref, g_ref, o_ref):
        acc = c_ref[0].astype(F32)
        for k in range(3):
            acc = acc + g_ref[k].astype(F32)
        o_ref[...] = acc

    return pl.pallas_call(
        body, name=name,
        grid_spec=pltpu.PrefetchScalarGridSpec(
            num_scalar_prefetch=1, grid=(rows // br,),
            in_specs=[pl.BlockSpec((1, br, cdim), lambda i, s_ref: (s_ref[0], i, 0)),
                      pl.BlockSpec((3, br, cdim), lambda i, s_ref: (0, i, 0))],
            out_specs=pl.BlockSpec((br, cdim), lambda i, s_ref: (i, 0))),
        out_shape=jax.ShapeDtypeStruct((rows, cdim), F32),
        compiler_params=_params(("arbitrary",)),
    )(slot, chip, got)


def _sum_devices(stack):
    _, r, cdim = stack.shape

    def body(s_ref, o_ref):
        acc = s_ref[0]
        for k in range(1, N_DEV):
            acc = acc + s_ref[k]
        o_ref[...] = acc

    return pl.pallas_call(
        body, name="sum_devices", out_shape=jax.ShapeDtypeStruct((r, cdim), F32),
        compiler_params=_params(),
    )(stack)


def _adamw(w, g, m, v, name):
    r, cdim = w.shape
    br = _row_block(r, 256)

    def body(w_ref, g_ref, m_ref, v_ref, d_ref, nm_ref, nv_ref):
        g = g_ref[...]
        nm = ADAM_B1 * m_ref[...] + (1.0 - ADAM_B1) * g
        nv = ADAM_B2 * v_ref[...] + (1.0 - ADAM_B2) * (g * g)
        m_hat = nm / (1.0 - ADAM_B1 ** ADAM_STEP)
        v_hat = nv / (1.0 - ADAM_B2 ** ADAM_STEP)
        d_ref[...] = -ADAM_LR * (m_hat / (jnp.sqrt(v_hat) + ADAM_EPS) + ADAM_WD * w_ref[...])
        nm_ref[...] = nm
        nv_ref[...] = nv

    spec = pl.BlockSpec((br, cdim), lambda i: (i, 0))
    shape = jax.ShapeDtypeStruct((r, cdim), F32)
    return pl.pallas_call(
        body, name=name, grid=(r // br,), in_specs=[spec] * 4, out_specs=[spec] * 3,
        out_shape=[shape] * 3, compiler_params=_params(("arbitrary",)),
    )(w, g, m, v)


WEIGHTS = ("ffn1_norm_g", "ffn1_w_gate", "ffn1_w_up", "ffn1_w_down", "mix_norm_g", "w_in", "gmlp_v_norm_g",
           "gmlp_w_s", "gmlp_b_s", "mla_q_norm_g", "mla_w_q_up", "mla_kv_norm_g", "mla_w_kv_up", "mla_q_head_g",
           "mla_k_head_g", "gmlp_out_g", "mla_out_g", "w_out", "ffn2_norm_g", "ffn2_w_gate", "ffn2_w_up",
           "ffn2_w_down")
SHARDED = {"ffn1_w_gate": True, "ffn1_w_up": True, "ffn1_w_down": False, "w_in": True, "mla_w_q_up": True,
           "mla_w_kv_up": True, "w_out": False, "ffn2_w_gate": True, "ffn2_w_up": True, "ffn2_w_down": False}


def _col_block(m, target):
    best = LANE
    for cand in range(LANE, min(m, target) + 1, LANE):
        if m % cand == 0:
            best = cand
    return best


def _shard_rows(w, transposed, pad_to=None):
    rows = (w[0].T if transposed else w[0]).astype(BF16)
    if pad_to is not None and pad_to != rows.shape[0]:
        rows = jnp.pad(rows, ((0, pad_to - rows.shape[0]), (0, 0)))
    return rows


def _pack(parts):
    flat = []
    for p in parts:
        f = p.reshape(-1).astype(F32)
        flat.append(jnp.pad(f, (0, _round_up(f.size, LANE) - f.size)))
    flat = jnp.concatenate(flat)
    rows = _round_up(flat.size // LANE, SUBLANE)
    return jnp.pad(flat, (0, rows * LANE - flat.size)).reshape(rows, LANE)


def _unpack(packed, shapes):
    out, row = [], 0
    for shp in shapes:
        size = 1
        for s in shp:
            size *= s
        nrows = _round_up(size, LANE) // LANE
        out.append(packed[row:row + nrows].reshape(-1)[:size].reshape(shp))
        row += nrows
    return out


def kernel(x, positions, ffn1_norm_g, ffn1_w_gate, ffn1_w_up, ffn1_w_down, mix_norm_g, w_in, gmlp_v_norm_g, gmlp_w_s, gmlp_b_s, mla_q_norm_g, mla_w_q_up, mla_kv_norm_g, mla_w_kv_up, mla_q_head_g, mla_k_head_g, gmlp_out_g, mla_out_g, w_out, ffn2_norm_g, ffn2_w_gate, ffn2_w_up, ffn2_w_down, loss_target, m_ffn1_norm_g, m_ffn1_w_gate, m_ffn1_w_up, m_ffn1_w_down, m_mix_norm_g, m_w_in, m_gmlp_v_norm_g, m_gmlp_w_s, m_gmlp_b_s, m_mla_q_norm_g, m_mla_w_q_up, m_mla_kv_norm_g, m_mla_w_kv_up, m_mla_q_head_g, m_mla_k_head_g, m_gmlp_out_g, m_mla_out_g, m_w_out, m_ffn2_norm_g, m_ffn2_w_gate, m_ffn2_w_up, m_ffn2_w_down, v_ffn1_norm_g, v_ffn1_w_gate, v_ffn1_w_up, v_ffn1_w_down, v_mix_norm_g, v_w_in, v_gmlp_v_norm_g, v_gmlp_w_s, v_gmlp_b_s, v_mla_q_norm_g, v_mla_w_q_up, v_mla_kv_norm_g, v_mla_w_kv_up, v_mla_q_head_g, v_mla_k_head_g, v_gmlp_out_g, v_mla_out_g, v_w_out, v_ffn2_norm_g, v_ffn2_w_gate, v_ffn2_w_up, v_ffn2_w_down):
    wts = dict(zip(WEIGHTS, (ffn1_norm_g, ffn1_w_gate, ffn1_w_up, ffn1_w_down, mix_norm_g, w_in, gmlp_v_norm_g, gmlp_w_s, gmlp_b_s, mla_q_norm_g, mla_w_q_up, mla_kv_norm_g, mla_w_kv_up, mla_q_head_g, mla_k_head_g, gmlp_out_g, mla_out_g, w_out, ffn2_norm_g, ffn2_w_gate, ffn2_w_up, ffn2_w_down)))
    mom1 = dict(zip(WEIGHTS, (m_ffn1_norm_g, m_ffn1_w_gate, m_ffn1_w_up, m_ffn1_w_down, m_mix_norm_g, m_w_in, m_gmlp_v_norm_g, m_gmlp_w_s, m_gmlp_b_s, m_mla_q_norm_g, m_mla_w_q_up, m_mla_kv_norm_g, m_mla_w_kv_up, m_mla_q_head_g, m_mla_k_head_g, m_gmlp_out_g, m_mla_out_g, m_w_out, m_ffn2_norm_g, m_ffn2_w_gate, m_ffn2_w_up, m_ffn2_w_down)))
    mom2 = dict(zip(WEIGHTS, (v_ffn1_norm_g, v_ffn1_w_gate, v_ffn1_w_up, v_ffn1_w_down, v_mix_norm_g, v_w_in, v_gmlp_v_norm_g, v_gmlp_w_s, v_gmlp_b_s, v_mla_q_norm_g, v_mla_w_q_up, v_mla_kv_norm_g, v_mla_w_kv_up, v_mla_q_head_g, v_mla_k_head_g, v_gmlp_out_g, v_mla_out_g, v_w_out, v_ffn2_norm_g, v_ffn2_w_gate, v_ffn2_w_up, v_ffn2_w_down)))

    b_loc, seq, d = x.shape
    t = b_loc * seq
    ffs = ffn1_w_gate.shape[2]
    fp = _round_up(ffs, LANE)
    wg = gmlp_v_norm_g.shape[1]
    groups = gmlp_w_s.shape[1]
    rq, rkv = mla_q_norm_g.shape[1], mla_kv_norm_g.shape[1]
    heads = mla_out_g.shape[1]
    assert w_in.shape[2] * N_DEV == 2 * wg + rq + rkv + ROPE and mla_w_kv_up.shape[2] * N_DEV == heads * HEADW
    tm = min(512, t)
    tm_mix = min(256, t)
    blk = min(256, seq)

    xf = x.reshape(t, d)
    target = loss_target.reshape(t, d)
    pos = positions.reshape(t, 1).astype(F32)
    half = ROPE // 2
    inv_freq = 1.0 / (ROPE_THETA ** (jnp.arange(half, dtype=F32) / half))
    freq = jnp.concatenate([inv_freq, inv_freq, jnp.zeros((LANE - ROPE,), F32)])[None, :]
    lane = jnp.arange(LANE)
    masks = jnp.stack([jnp.where(lane < half, -1.0, 0.0),
                       jnp.where((lane >= half) & (lane < ROPE), 1.0, 0.0)]).astype(F32)
    gqh = jnp.pad(mla_q_head_g, ((0, 0), (0, HEADW - QK)))
    gkh = jnp.pad(mla_k_head_g, ((0, 0), (0, HEADW - QK)))
    bias = jnp.repeat(gmlp_b_s[0].T, CHUNK, axis=1)
    gouta = gmlp_out_g.reshape(1, wg)
    goutb = mla_out_g.reshape(1, heads * VHEAD)
    ws = gmlp_w_s[0]

    px, py, pc = _place()
    me = 4 * px + 2 * py + pc
    core = pc.astype(jnp.int32).reshape(1)
    slot = (2 * px + py).astype(jnp.int32).reshape(1)
    order = [n for n in WEIGHTS if n in SHARDED]
    group = {"ffn1": [n for n in order if n.startswith("ffn1")], "ffn2": [n for n in order if n.startswith("ffn2")],
             "mix": [n for n in order if not n.startswith("ffn")]}
    shard = {n: _shard_rows(wts[n], SHARDED[n], fp if n.startswith("ffn") else None) for n in order}

    def tied(arr, token):
        return arr + token[0, 0].astype(arr.dtype)

    ici1 = _gather_ici([shard[n] for n in group["ffn1"]], me, "gather_ffn1_ici")
    lands1 = ici1.wait(ici1.token)[3:]
    d2d1 = _gather_d2d(lands1, "gather_ffn1_d2d")
    ici2 = _gather_ici([shard[n] for n in group["mix"]], me, "gather_mix_ici", deps=(d2d1.token,))
    ici3 = _gather_ici([shard[n] for n in group["ffn2"]], me, "gather_ffn2_ici", deps=(ici2.token,))
    full = dict(zip(group["ffn1"], d2d1.wait(ici3.token)))

    x1, xn1, a1, b1 = _ffn_fwd(xf, ffn1_norm_g, full["ffn1_w_gate"], full["ffn1_w_up"], full["ffn1_w_down"], tm,
                               "ffn1_fwd")
    d2d2 = _gather_d2d(ici2.wait(x1)[len(group["mix"]):], "gather_mix_d2d")
    full.update(zip(group["mix"], d2d2.wait(d2d2.token)))
    win_t = jnp.pad(full["w_in"].reshape(-1, d), ((0, LANE - ROPE), (0, 0)))
    splits = (2 * wg, rq, rkv, LANE)
    wq_t = jnp.pad(full["mla_w_q_up"].reshape(heads, QK, rq), ((0, 0), (0, HEADW - QK), (0, 0)))
    wkv_t = full["mla_w_kv_up"].reshape(heads, HEADW, rkv)
    wout = full["w_out"].reshape(-1, d)
    hn, zuv, cq, ckv, krw = _inproj_fwd(x1, mix_norm_g, win_t, splits, tm)
    ya = _gmlp_fwd(zuv, gmlp_v_norm_g, ws, bias, gouta, tm_mix)
    q, k, vv = _mla_proj_fwd(cq, ckv, krw, pos, freq, masks, mla_q_norm_g, mla_kv_norm_g, wq_t, wkv_t, gqh, gkh,
                             tm_mix)
    d2d3 = _gather_d2d(ici3.wait(q)[3:], "gather_ffn2_d2d")
    o, lse = _attn_fwd(q, k, vv, seq, blk)
    x2, ycat = _out_fwd(ya, o, tied(goutb, d2d3.token), wout, x1, tm)
    full.update(zip(group["ffn2"], d2d3.wait(x2)))
    x3, xn2, a2, b2 = _ffn_fwd(x2, ffn2_norm_g, full["ffn2_w_gate"], full["ffn2_w_up"], full["ffn2_w_down"], tm,
                               "ffn2_fwd")
    dx3, loss_part = _loss_head(x3, target, tm)

    outs_g, outs_d, outs_m, outs_v = {}, {}, {}, {}

    def finish(names, chip, got):
        for n, cp, gt in zip(names, chip, got):
            rows = wts[n].shape[2] if SHARDED[n] else wts[n].shape[1]
            g = _sum_owned(cp, gt, slot, rows, "sum_owned_" + n)
            g = g.T if SHARDED[n] else g
            dlt, nm, nv = _adamw(wts[n][0], g, mom1[n][0], mom2[n][0], "adamw_" + n)
            outs_g[n], outs_d[n], outs_m[n], outs_v[n] = g[None], dlt[None], nm[None], nv[None]

    def chip_sums(names, ex, after):
        res = ex.wait(after)
        return [_pair_add(f, gt, core, "pair_add_" + n) for n, f, gt in zip(names, res[:len(names)], res[len(names):])]

    tk = min(512, t)
    grads = {}
    small = {}
    dx2, small["ffn2_norm_g"], da2, db2, h2 = _ffn_bwd(
        dx3, x2, ffn2_norm_g, a2, b2, full["ffn2_w_gate"], full["ffn2_w_up"], full["ffn2_w_down"], tm, "ffn2_bwd")
    grads["ffn2_w_gate"] = _matmul_tn(da2, xn2, fp, d, tk, BF16, "dw_ffn2_gate").reshape(N_DEV, fp, d)
    grads["ffn2_w_up"] = _matmul_tn(db2, xn2, fp, d, tk, BF16, "dw_ffn2_up").reshape(N_DEV, fp, d)
    grads["ffn2_w_down"] = _matmul_tn(h2, dx3, fp, d, tk, BF16, "dw_ffn2_down", rhs_scale=0.5).reshape(
        N_DEV, fp, d)
    red_a2 = _reduce_d2d([grads[n] for n in group["ffn2"]], "reduce_ffn2_d2d")
    dya, do, delta, small["mla_out_g"] = _out_bwd(dx2, o, tied(goutb, red_a2.token), wout, wg, tm)
    grads["w_out"] = _matmul_tn(ycat, dx2, _col_block(ycat.shape[1], 768), d, tk, BF16, "dw_out").reshape(
        N_DEV, -1, d)
    chip2 = chip_sums(group["ffn2"], red_a2, grads["w_out"])
    red_b2 = _reduce_ici(chip2, "reduce_ffn2_ici")
    dq, dk, dv = _attn_bwd(q, k, vv, do, lse, tied(delta, red_b2.token), seq, blk)
    (dcq, dckv, dkrw, dwq, dwkv, small["mla_q_norm_g"], small["mla_kv_norm_g"], dgqh, dgkh) = _mla_proj_bwd(
        dq, dk, dv, cq, ckv, krw, pos, freq, masks, mla_q_norm_g, mla_kv_norm_g, wq_t, wkv_t, gqh, gkh, tm_mix)
    small["mla_q_head_g"], small["mla_k_head_g"] = dgqh[:, :QK], dgkh[:, :QK]
    grads["mla_w_q_up"] = dwq[:, :QK].astype(BF16).reshape(N_DEV, -1, rq)
    grads["mla_w_kv_up"] = dwkv.astype(BF16).reshape(N_DEV, -1, rkv)
    dzuv, small["gmlp_w_s"], dbs, small["gmlp_v_norm_g"], small["gmlp_out_g"] = _gmlp_bwd(
        dya, zuv, gmlp_v_norm_g, ws, bias, gouta, tm_mix)
    small["gmlp_b_s"] = dbs[:, :, 0]
    dx1, small["mix_norm_g"], dzc = _inproj_bwd([dzuv, dcq, dckv, dkrw], x1, mix_norm_g, win_t, dx2, splits,
                                                tm_mix)
    dwin = _matmul_tn(dzc, hn, _col_block(dzc.shape[1], 768), d, tk, BF16, "dw_in")
    grads["w_in"] = dwin[:N_DEV * w_in.shape[2]].reshape(N_DEV, -1, d)
    res = red_b2.wait(grads["w_in"])
    red_am = _reduce_d2d([grads[n] for n in group["mix"]], "reduce_mix_d2d")
    dx0, small["ffn1_norm_g"], da1, db1, h1 = _ffn_bwd(
        dx1, xf, tied(ffn1_norm_g, red_am.token), a1, b1, full["ffn1_w_gate"], full["ffn1_w_up"],
        full["ffn1_w_down"], tm, "ffn1_bwd")
    finish(group["ffn2"], res[:3], res[3:])
    chipm = chip_sums(group["mix"], red_am, dx0)
    red_bm = _reduce_ici(chipm, "reduce_mix_ici")
    operands = {"ffn1_w_gate": (da1, xn1, None), "ffn1_w_up": (db1, xn1, None), "ffn1_w_down": (h1, dx1, 0.5)}
    token, red_a, red_b = red_bm.token, None, []
    for n in group["ffn1"]:
        lhs, rhs, scale = operands[n]
        gr = _matmul_tn(lhs, rhs, fp, d, tk, BF16, "dw_" + n, rhs_scale=scale).reshape(N_DEV, fp, d)
        if red_a is not None:
            prev, ex = red_a
            red_b.append((prev, _reduce_ici(chip_sums([prev], ex, gr), "reduce_" + prev + "_ici")))
            token = red_b[-1][1].token
        red_a = (n, _reduce_d2d([gr], "reduce_" + n + "_d2d", deps=(token,)))
    prev, ex = red_a
    red_b.append((prev, _reduce_ici(chip_sums([prev], ex, ex.token), "reduce_" + prev + "_ici")))
    res = red_bm.wait(red_b[-1][1].token)
    nm_ = len(group["mix"])
    finish(group["mix"], res[:nm_], res[nm_:])
    for n, ex in red_b:
        res = ex.wait(outs_g[group["mix"][-1]])
        finish([n], res[:1], res[1:])

    rep = [n for n in WEIGHTS if n not in SHARDED]
    packed = _pack([small[n] for n in rep] + [loss_part])
    total = _sum_devices(_all_gather([packed], "gather_small")[0])
    zero = jnp.zeros((1,), F32)
    dlt, nm, nv = _adamw(_pack([wts[n] for n in rep] + [zero]), total, _pack([mom1[n] for n in rep] + [zero]),
                         _pack([mom2[n] for n in rep] + [zero]), "adamw_small")
    shapes = [wts[n].shape for n in rep] + [(1,)]
    for n, g, dl, m1, m2 in zip(rep, _unpack(total, shapes), _unpack(dlt, shapes), _unpack(nm, shapes),
                                _unpack(nv, shapes)):
        outs_g[n], outs_d[n], outs_m[n], outs_v[n] = g, dl, m1, m2
    loss = _unpack(total, shapes)[-1].reshape(())

    return (loss, dx0.reshape(b_loc, seq, d), *[outs_g[n] for n in WEIGHTS], *[outs_d[n] for n in WEIGHTS],
            *[outs_m[n] for n in WEIGHTS], *[outs_v[n] for n in WEIGHTS])
```

```python
import functools

import jax
import jax.numpy as jnp
from jax import lax
from jax.experimental import pallas as pl
from jax.experimental.pallas import tpu as pltpu

F32 = jnp.float32
BF16 = jnp.bfloat16
EPS = 1e-6
LANE = 128
SUBLANE = 8
N_DEV = 8
VMEM_LIMIT = 60 * 1024 * 1024
NOPE = 128
ROPE = 64
VHEAD = 128
QK = NOPE + ROPE
HEADW = 2 * LANE
CHUNK = 128
ROPE_THETA = 10000.0
ADAM_LR, ADAM_B1, ADAM_B2, ADAM_EPS, ADAM_WD, ADAM_STEP = 0.001, 0.9, 0.999, 1e-08, 0.01, 10
MESH = pl.DeviceIdType.MESH
ANY = pl.BlockSpec(memory_space=pl.ANY)
WHOLE_VMEM = pl.BlockSpec(memory_space=pltpu.VMEM)


def _params(sem=None):
    return pltpu.CompilerParams(dimension_semantics=sem, vmem_limit_bytes=VMEM_LIMIT)


def _round_up(n, m):
    return -(-n // m) * m


def _row_block(rows, target):
    best = rows
    for cand in range(SUBLANE, min(rows, target) + 1, SUBLANE):
        if rows % cand == 0:
            best = cand
    return best if best <= target else rows


def _nn(a, b):
    return jnp.dot(a, b, preferred_element_type=F32)


def _nt(a, b):
    return lax.dot_general(a, b, (((1,), (1,)), ((), ())), preferred_element_type=F32)


def _tn(a, b):
    return lax.dot_general(a, b, (((0,), (0,)), ((), ())), preferred_element_type=F32)


def _rstd(x, n):
    return lax.rsqrt(jnp.sum(x * x, axis=-1, keepdims=True) * (1.0 / n) + EPS)


def _rms_fwd(x, g, n):
    return x * _rstd(x, n) * g


def _rms_bwd(x, g, dy, n):
    r = _rstd(x, n)
    xh = x * r
    dyg = dy * g
    dx = r * (dyg - xh * (jnp.sum(dyg * xh, axis=-1, keepdims=True) * (1.0 / n)))
    return dx, jnp.sum(dy * xh, axis=0, keepdims=True)


def _gelu(x):
    return 0.5 * x * (1.0 + lax.erf(x * 0.7071067811865476))


def _gelu_grad(x):
    return 0.5 * (1.0 + lax.erf(x * 0.7071067811865476)) + x * jnp.exp(-0.5 * x * x) * 0.3989422804014327


def _ffn_fwd(base, xn, g, ids, wg_t, wu_t, wd, a_prev, b_prev, tm, name):
    t, d = base.shape
    nb, fp, _ = wg_t.shape
    n = ids.shape[0]
    first = xn is None
    if a_prev is None:
        a_prev, b_prev = lax.empty((t, nb * fp), BF16), lax.empty((t, nb * fp), BF16)

    def body(ids_ref, *refs):
        if first:
            base_ref, g_ref, wg_ref, wu_ref, wd_ref, _, _, out_ref, xn_ref, a_ref, b_ref, acc = refs
        else:
            base_ref, xn_ref, wg_ref, wu_ref, wd_ref, _, _, out_ref, a_ref, b_ref, acc = refs
        j = pl.program_id(1)

        @pl.when(j == 0)
        def _():
            if first:
                xn_ref[...] = _rms_fwd(base_ref[...], g_ref[...], d).astype(BF16)
            acc[...] = jnp.zeros_like(acc)

        xnb = xn_ref[...]
        a = _nt(xnb, wg_ref[0]).astype(BF16)
        b = _nt(xnb, wu_ref[0]).astype(BF16)
        a_ref[...] = a
        b_ref[...] = b
        a = a.astype(F32)
        h = (a * jax.nn.sigmoid(a)) * b.astype(F32)
        acc[...] += _nn(h.astype(BF16), wd_ref[0])

        @pl.when(j == n - 1)
        def _():
            out_ref[...] = base_ref[...] + 0.5 * acc[...]

    wspec = pl.BlockSpec((1, fp, d), lambda i, j, ids_ref: (ids_ref[j], 0, 0))
    row = pl.BlockSpec((tm, d), lambda i, j, ids_ref: (i, 0))
    ff = pl.BlockSpec((tm, fp), lambda i, j, ids_ref: (i, ids_ref[j]))
    ffs = jax.ShapeDtypeStruct((t, nb * fp), BF16)
    second = pl.BlockSpec((1, d), lambda i, j, ids_ref: (0, 0)) if first else row
    n_in = 7
    res = pl.pallas_call(
        body, name=name,
        grid_spec=pltpu.PrefetchScalarGridSpec(
            num_scalar_prefetch=1, grid=(t // tm, n),
            in_specs=[row, second, wspec, wspec, wspec, ANY, ANY],
            out_specs=[row] + ([row] if first else []) + [ff, ff],
            scratch_shapes=[pltpu.VMEM((tm, d), F32)]),
        out_shape=[jax.ShapeDtypeStruct((t, d), F32)] + ([jax.ShapeDtypeStruct((t, d), BF16)] if first else [])
        + [ffs, ffs],
        input_output_aliases={n_in - 1: (2 if first else 1), n_in: (3 if first else 2)},
        compiler_params=_params(("arbitrary", "arbitrary")),
    )(ids, base, g if first else xn, wg_t, wu_t, wd, a_prev, b_prev)
    return res if first else (res[0], xn, res[1], res[2])


def _ffn_bwd(dout, x, g, a, b, wg_t, wu_t, wd, tm, name):
    t, d = x.shape
    nb, fp, _ = wg_t.shape

    def body(do_hbm, x_hbm, g_ref, a_ref, b_ref, wg_ref, wu_ref, wd_ref,
             dx_ref, dg_ref, da_ref, db_ref, h_ref, acc, rowbuf, dy_scr, sem):
        i, j = pl.program_id(0), pl.program_id(1)
        rows = pl.ds(pl.multiple_of(i * tm, tm), tm)

        def fetch(src):
            cp = pltpu.make_async_copy(src.at[rows, :], rowbuf, sem)
            cp.start()
            cp.wait()

        @pl.when(j == 0)
        def _():
            fetch(do_hbm)
            dy_scr[...] = (0.5 * rowbuf[...]).astype(BF16)
            acc[...] = jnp.zeros_like(acc)

        @pl.when((i == 0) & (j == 0))
        def _():
            dg_ref[...] = jnp.zeros_like(dg_ref)

        dh = _nt(dy_scr[...], wd_ref[0])
        a = a_ref[...].astype(F32)
        bb = b_ref[...].astype(F32)
        s = jax.nn.sigmoid(a)
        sl = a * s
        h_ref[...] = (sl * bb).astype(BF16)
        da = (dh * bb * (s * (1.0 + a * (1.0 - s)))).astype(BF16)
        db = (dh * sl).astype(BF16)
        da_ref[...] = da
        db_ref[...] = db
        acc[...] += _nn(da, wg_ref[0]) + _nn(db, wu_ref[0])

        @pl.when(j == nb - 1)
        def _():
            fetch(x_hbm)
            dxn, dg = _rms_bwd(rowbuf[...], g_ref[...], acc[...], d)
            dg_ref[...] += dg
            dx_ref[...] = dxn
            fetch(do_hbm)
            dx_ref[...] += rowbuf[...]

    wspec = pl.BlockSpec((1, fp, d), lambda i, j: (j, 0, 0))
    row = pl.BlockSpec((tm, d), lambda i, j: (i, 0))
    vec = pl.BlockSpec((1, d), lambda i, j: (0, 0))
    ff = pl.BlockSpec((tm, fp), lambda i, j: (i, j))
    ffs = jax.ShapeDtypeStruct((t, nb * fp), BF16)
    return pl.pallas_call(
        body, name=name, grid=(t // tm, nb),
        in_specs=[ANY, ANY, vec, ff, ff, wspec, wspec, wspec],
        out_specs=[row, vec, ff, ff, ff],
        out_shape=[jax.ShapeDtypeStruct((t, d), F32), jax.ShapeDtypeStruct((1, d), F32), ffs, ffs, ffs],
        scratch_shapes=[pltpu.VMEM((tm, d), F32), pltpu.VMEM((tm, d), F32), pltpu.VMEM((tm, d), BF16),
                        pltpu.SemaphoreType.DMA],
        compiler_params=_params(("arbitrary", "arbitrary")),
    )(dout, x, g, a, b, wg_t, wu_t, wd)


def _matmul_tn(lhs, rhs, bm, bn, tk, out_dtype, name, rhs_scale=None):
    t, m = lhs.shape
    n = rhs.shape[1]
    nk = t // tk

    def body(l_ref, r_ref, o_ref, acc):
        k = pl.program_id(2)

        @pl.when(k == 0)
        def _():
            acc[...] = jnp.zeros_like(acc)

        r = r_ref[...] if rhs_scale is None else rhs_scale * r_ref[...]
        acc[...] += _tn(l_ref[...].astype(BF16), r.astype(BF16))

        @pl.when(k == nk - 1)
        def _():
            o_ref[...] = acc[...].astype(out_dtype)

    return pl.pallas_call(
        body, name=name, grid=(m // bm, n // bn, nk),
        in_specs=[pl.BlockSpec((tk, bm), lambda i, j, k: (k, i)), pl.BlockSpec((tk, bn), lambda i, j, k: (k, j))],
        out_specs=pl.BlockSpec((bm, bn), lambda i, j, k: (i, j)),
        out_shape=jax.ShapeDtypeStruct((m, n), out_dtype),
        scratch_shapes=[pltpu.VMEM((bm, bn), F32)],
        compiler_params=_params(("arbitrary", "arbitrary", "arbitrary")),
    )(lhs, rhs)


def _inproj_fwd(x, g, w_t, splits, tm):
    t, d = x.shape
    offs = [sum(splits[:k]) for k in range(len(splits))]

    def body(x_ref, g_ref, w_ref, hn_ref, *z_refs):
        hn = _rms_fwd(x_ref[...], g_ref[...], d).astype(BF16)
        hn_ref[...] = hn
        for z_ref, o, n in zip(z_refs, offs, splits):
            z_ref[...] = _nt(hn, w_ref[o:o + n, :])

    row = pl.BlockSpec((tm, d), lambda i: (i, 0))
    return pl.pallas_call(
        body, name="inproj_fwd", grid=(t // tm,),
        in_specs=[row, pl.BlockSpec((1, d), lambda i: (0, 0)), WHOLE_VMEM],
        out_specs=[row] + [pl.BlockSpec((tm, n), lambda i: (i, 0)) for n in splits],
        out_shape=[jax.ShapeDtypeStruct((t, d), BF16)] + [jax.ShapeDtypeStruct((t, n), F32) for n in splits],
        compiler_params=_params(("arbitrary",)),
    )(x, g, w_t)


def _inproj_bwd(dzs, x, g, w_t, dres, splits, tm):
    t, d = x.shape
    offs = [sum(splits[:k]) for k in range(len(splits))]
    ni = sum(splits)
    nz = len(splits)

    def body(*refs):
        dz_refs = refs[:nz]
        x_ref, g_ref, w_ref, dres_ref, dx_ref, dg_ref, dzc_ref = refs[nz:]
        dhn = jnp.zeros((tm, d), F32)
        for dz_ref, o, n in zip(dz_refs, offs, splits):
            dz = dz_ref[...].astype(BF16)
            dzc_ref[:, o:o + n] = dz
            dhn += _nn(dz, w_ref[o:o + n, :])
        dx, dg = _rms_bwd(x_ref[...], g_ref[...], dhn, d)
        dx_ref[...] = dres_ref[...] + dx

        @pl.when(pl.program_id(0) == 0)
        def _():
            dg_ref[...] = jnp.zeros_like(dg_ref)

        dg_ref[...] += dg

    row = pl.BlockSpec((tm, d), lambda i: (i, 0))
    vec = pl.BlockSpec((1, d), lambda i: (0, 0))
    return pl.pallas_call(
        body, name="inproj_bwd", grid=(t // tm,),
        in_specs=[pl.BlockSpec((tm, n), lambda i: (i, 0)) for n in splits] + [row, vec, WHOLE_VMEM, row],
        out_specs=[row, vec, pl.BlockSpec((tm, ni), lambda i: (i, 0))],
        out_shape=[jax.ShapeDtypeStruct((t, d), F32), jax.ShapeDtypeStruct((1, d), F32),
                   jax.ShapeDtypeStruct((t, ni), BF16)],
        compiler_params=_params(("arbitrary",)),
    )(*dzs, x, g, w_t, dres)


def _tril_bf16(ws_ref, grp):
    rows = lax.broadcasted_iota(jnp.int32, (CHUNK, CHUNK), 0)
    cols = lax.broadcasted_iota(jnp.int32, (CHUNK, CHUNK), 1)
    return jnp.where(rows >= cols, ws_ref[grp], 0.0).astype(BF16)


def _gmlp_mix(zuv_ref, gv_ref, ws_ref, bias_ref, v_scr, mixed_scr, tm, w, groups):
    u = _gelu(zuv_ref[:, 0:w])
    v0 = _gelu(zuv_ref[:, w:2 * w])
    v_scr[...] = _rms_fwd(v0, gv_ref[...], w).astype(BF16)
    for grp in range(groups):
        wsm = _tril_bf16(ws_ref, grp)
        lanes = slice(grp * CHUNK, (grp + 1) * CHUNK)
        for c in range(tm // CHUNK):
            rows = slice(c * CHUNK, (c + 1) * CHUNK)
            mixed_scr[rows, lanes] = _nn(wsm, v_scr[rows, lanes]) + bias_ref[:, lanes]
    return u, v0


def _gmlp_fwd(zuv, gv, ws, bias, gout, tm):
    t, w2 = zuv.shape
    w = w2 // 2
    groups = ws.shape[0]

    def body(zuv_ref, gv_ref, ws_ref, bias_ref, go_ref, y_ref, v_scr, mixed_scr):
        u, _ = _gmlp_mix(zuv_ref, gv_ref, ws_ref, bias_ref, v_scr, mixed_scr, tm, w, groups)
        ya = u * mixed_scr[...]
        for grp in range(groups):
            lanes = slice(grp * CHUNK, (grp + 1) * CHUNK)
            y_ref[:, lanes] = _rms_fwd(ya[:, lanes], go_ref[:, lanes], CHUNK).astype(BF16)

    const2 = lambda i: (0, 0)
    return pl.pallas_call(
        body, name="gmlp_fwd", grid=(t // tm,),
        in_specs=[pl.BlockSpec((tm, w2), lambda i: (i, 0)), pl.BlockSpec((1, w), const2),
                  pl.BlockSpec((groups, CHUNK, CHUNK), lambda i: (0, 0, 0)),
                  pl.BlockSpec((CHUNK, w), const2), pl.BlockSpec((1, w), const2)],
        out_specs=pl.BlockSpec((tm, w), lambda i: (i, 0)),
        out_shape=jax.ShapeDtypeStruct((t, w), BF16),
        scratch_shapes=[pltpu.VMEM((tm, w), BF16), pltpu.VMEM((tm, w), F32)],
        compiler_params=_params(("arbitrary",)),
    )(zuv, gv, ws, bias, gout)


def _gmlp_bwd(dy, zuv, gv, ws, bias, gout, tm):
    t, w2 = zuv.shape
    w = w2 // 2
    groups = ws.shape[0]

    def body(dy_ref, zuv_ref, gv_ref, ws_ref, bias_ref, go_ref,
             dz_ref, dws_ref, dbias_ref, dgv_ref, dgo_ref, v_scr, mixed_scr, dmix_scr, dv_scr):
        @pl.when(pl.program_id(0) == 0)
        def _():
            dws_ref[...] = jnp.zeros_like(dws_ref)
            dbias_ref[...] = jnp.zeros_like(dbias_ref)
            dgv_ref[...] = jnp.zeros_like(dgv_ref)
            dgo_ref[...] = jnp.zeros_like(dgo_ref)

        u, v0 = _gmlp_mix(zuv_ref, gv_ref, ws_ref, bias_ref, v_scr, mixed_scr, tm, w, groups)
        mixed = mixed_scr[...]
        ya = u * mixed
        for grp in range(groups):
            lanes = slice(grp * CHUNK, (grp + 1) * CHUNK)
            dya, dgo = _rms_bwd(ya[:, lanes], go_ref[:, lanes], dy_ref[:, lanes], CHUNK)
            dgo_ref[:, lanes] += dgo
            dz_ref[:, lanes] = dya * mixed[:, lanes] * _gelu_grad(zuv_ref[:, lanes])
            dmix_scr[:, lanes] = dya * u[:, lanes]
        for grp in range(groups):
            wsm = _tril_bf16(ws_ref, grp)
            lanes = slice(grp * CHUNK, (grp + 1) * CHUNK)
            dws = jnp.zeros((CHUNK, CHUNK), F32)
            dbias = jnp.zeros((CHUNK, CHUNK), F32)
            for c in range(tm // CHUNK):
                rows = slice(c * CHUNK, (c + 1) * CHUNK)
                dm = dmix_scr[rows, lanes]
                dmb = dm.astype(BF16)
                dv_scr[rows, lanes] = _tn(wsm, dmb)
                dws += _nt(dmb, v_scr[rows, lanes])
                dbias += dm
            rr = lax.broadcasted_iota(jnp.int32, (CHUNK, CHUNK), 0)
            cc = lax.broadcasted_iota(jnp.int32, (CHUNK, CHUNK), 1)
            dws_ref[grp] += jnp.where(rr >= cc, dws, 0.0)
            dbias_ref[grp] += jnp.sum(dbias, axis=1, keepdims=True)
        dv0, dgv = _rms_bwd(v0, gv_ref[...], dv_scr[...], w)
        dgv_ref[...] += dgv
        dz_ref[:, w:2 * w] = dv0 * _gelu_grad(zuv_ref[:, w:2 * w])

    const2 = lambda i: (0, 0)
    const3 = lambda i: (0, 0, 0)
    return pl.pallas_call(
        body, name="gmlp_bwd", grid=(t // tm,),
        in_specs=[pl.BlockSpec((tm, w), lambda i: (i, 0)), pl.BlockSpec((tm, w2), lambda i: (i, 0)),
                  pl.BlockSpec((1, w), const2), pl.BlockSpec((groups, CHUNK, CHUNK), const3),
                  pl.BlockSpec((CHUNK, w), const2), pl.BlockSpec((1, w), const2)],
        out_specs=[pl.BlockSpec((tm, w2), lambda i: (i, 0)), pl.BlockSpec((groups, CHUNK, CHUNK), const3),
                   pl.BlockSpec((groups, CHUNK, 1), const3), pl.BlockSpec((1, w), const2), pl.BlockSpec((1, w), const2)],
        out_shape=[jax.ShapeDtypeStruct((t, w2), F32), jax.ShapeDtypeStruct((groups, CHUNK, CHUNK), F32),
                   jax.ShapeDtypeStruct((groups, CHUNK, 1), F32), jax.ShapeDtypeStruct((1, w), F32),
                   jax.ShapeDtypeStruct((1, w), F32)],
        scratch_shapes=[pltpu.VMEM((tm, w), BF16), pltpu.VMEM((tm, w), F32),
                        pltpu.VMEM((tm, w), F32), pltpu.VMEM((tm, w), F32)],
        compiler_params=_params(("arbitrary",)),
    )(dy, zuv, gv, ws, bias, gout)


def _rot(x, m_lo, m_hi):
    return pltpu.roll(x, LANE - ROPE // 2, 1) * m_lo + pltpu.roll(x, ROPE // 2, 1) * m_hi


def _rope_tables(pos_ref, freq_ref):
    ang = pos_ref[...] * freq_ref[...]
    return jnp.cos(ang), jnp.sin(ang)


def _mla_proj_fwd(cq, ckv, krw, pos, freq, masks, gq, gkv, wq_t, wkv_t, gqh, gkh, tm):
    t, rq = cq.shape
    rkv = ckv.shape[1]
    heads = wq_t.shape[0]

    def body(cq_ref, ckv_ref, kr_ref, pos_ref, freq_ref, mk_ref, gq_ref, gkv_ref, wq_ref, wkv_ref,
             gqh_ref, gkh_ref, q_ref, k_ref, v_ref):
        cos, sin = _rope_tables(pos_ref, freq_ref)
        m_lo, m_hi = mk_ref[0:1, :], mk_ref[1:2, :]
        cqn = _rms_fwd(cq_ref[...], gq_ref[...], rq).astype(BF16)
        ckvn = _rms_fwd(ckv_ref[...], gkv_ref[...], rkv).astype(BF16)
        kr = kr_ref[...]
        kr_ss = jnp.sum(kr * kr, axis=-1, keepdims=True)
        for h in range(heads):
            qh = _nt(cqn, wq_ref[h])
            qn = qh * _rstd(qh, QK) * gqh_ref[...]
            qr = qn[:, LANE:]
            q_ref[h, :, 0:LANE] = qn[:, 0:LANE].astype(BF16)
            q_ref[h, :, LANE:] = (qr * cos + _rot(qr, m_lo, m_hi) * sin).astype(BF16)
            kvh = _nt(ckvn, wkv_ref[h])
            kn = kvh[:, 0:LANE]
            rk = lax.rsqrt((jnp.sum(kn * kn, axis=-1, keepdims=True) + kr_ss) * (1.0 / QK) + EPS)
            k_ref[h, :, 0:LANE] = (kn * rk * gkh_ref[:, 0:LANE]).astype(BF16)
            krn = kr * rk * gkh_ref[:, LANE:]
            k_ref[h, :, LANE:] = (krn * cos + _rot(krn, m_lo, m_hi) * sin).astype(BF16)
            v_ref[h] = kvh[:, LANE:].astype(BF16)

    c2 = lambda i: (0, 0)
    c3 = lambda i: (0, 0, 0)
    return pl.pallas_call(
        body, name="mla_proj_fwd", grid=(t // tm,),
        in_specs=[pl.BlockSpec((tm, rq), lambda i: (i, 0)), pl.BlockSpec((tm, rkv), lambda i: (i, 0)),
                  pl.BlockSpec((tm, LANE), lambda i: (i, 0)), pl.BlockSpec((tm, 1), lambda i: (i, 0)),
                  pl.BlockSpec((1, LANE), c2), pl.BlockSpec((2, LANE), c2),
                  pl.BlockSpec((1, rq), c2), pl.BlockSpec((1, rkv), c2),
                  pl.BlockSpec((heads, HEADW, rq), c3), pl.BlockSpec((heads, HEADW, rkv), c3),
                  pl.BlockSpec((1, HEADW), c2), pl.BlockSpec((1, HEADW), c2)],
        out_specs=[pl.BlockSpec((heads, tm, HEADW), lambda i: (0, i, 0)),
                   pl.BlockSpec((heads, tm, HEADW), lambda i: (0, i, 0)),
                   pl.BlockSpec((heads, tm, VHEAD), lambda i: (0, i, 0))],
        out_shape=[jax.ShapeDtypeStruct((heads, t, HEADW), BF16), jax.ShapeDtypeStruct((heads, t, HEADW), BF16),
                   jax.ShapeDtypeStruct((heads, t, VHEAD), BF16)],
        compiler_params=_params(("arbitrary",)),
    )(cq, ckv, krw, pos, freq, masks, gq, gkv, wq_t, wkv_t, gqh, gkh)


def _mla_proj_bwd(dq, dk, dv, cq, ckv, krw, pos, freq, masks, gq, gkv, wq_t, wkv_t, gqh, gkh, tm):
    t, rq = cq.shape
    rkv = ckv.shape[1]
    heads = wq_t.shape[0]

    def body(dq_ref, dk_ref, dv_ref, cq_ref, ckv_ref, kr_ref, pos_ref, freq_ref, mk_ref, gq_ref, gkv_ref,
             wq_ref, wkv_ref, gqh_ref, gkh_ref,
             dcq_ref, dckv_ref, dkr_ref, dwq_ref, dwkv_ref, dgq_ref, dgkv_ref, dgqh_ref, dgkh_ref):
        @pl.when(pl.program_id(0) == 0)
        def _():
            for r in (dwq_ref, dwkv_ref, dgq_ref, dgkv_ref, dgqh_ref, dgkh_ref):
                r[...] = jnp.zeros_like(r)

        cos, sin = _rope_tables(pos_ref, freq_ref)
        m_lo, m_hi = mk_ref[0:1, :], mk_ref[1:2, :]

        def unrope(dy):
            return dy * cos - _rot(dy * sin, m_lo, m_hi)

        cqn = _rms_fwd(cq_ref[...], gq_ref[...], rq).astype(BF16)
        ckvn = _rms_fwd(ckv_ref[...], gkv_ref[...], rkv).astype(BF16)
        kr = kr_ref[...]
        kr_ss = jnp.sum(kr * kr, axis=-1, keepdims=True)
        dcqn = jnp.zeros((tm, rq), F32)
        dckvn = jnp.zeros((tm, rkv), F32)
        dkr = jnp.zeros((tm, LANE), F32)
        for h in range(heads):
            qh = _nt(cqn, wq_ref[h])
            dqn = jnp.concatenate([dq_ref[h, :, 0:LANE], unrope(dq_ref[h, :, LANE:])], axis=1)
            dqh, dg = _rms_bwd(qh, gqh_ref[...], dqn, QK)
            dgqh_ref[...] += dg
            dqh = dqh.astype(BF16)
            dcqn += _nn(dqh, wq_ref[h])
            dwq_ref[h] += _tn(dqh, cqn)

            kvh = _nt(ckvn, wkv_ref[h])
            kn = kvh[:, 0:LANE]
            rk = lax.rsqrt((jnp.sum(kn * kn, axis=-1, keepdims=True) + kr_ss) * (1.0 / QK) + EPS)
            dkn_n = dk_ref[h, :, 0:LANE]
            dkr_n = unrope(dk_ref[h, :, LANE:])
            knh, krh = kn * rk, kr * rk
            dgkh_ref[:, 0:LANE] += jnp.sum(dkn_n * knh, axis=0, keepdims=True)
            dgkh_ref[:, LANE:] += jnp.sum(dkr_n * krh, axis=0, keepdims=True)
            dkn_g, dkr_g = dkn_n * gkh_ref[:, 0:LANE], dkr_n * gkh_ref[:, LANE:]
            proj = (jnp.sum(dkn_g * knh, axis=-1, keepdims=True)
                    + jnp.sum(dkr_g * krh, axis=-1, keepdims=True)) * (1.0 / QK)
            dkr += rk * (dkr_g - krh * proj)
            dkvh = jnp.concatenate([rk * (dkn_g - knh * proj), dv_ref[h]], axis=1).astype(BF16)
            dckvn += _nn(dkvh, wkv_ref[h])
            dwkv_ref[h] += _tn(dkvh, ckvn)
        dkr_ref[...] = dkr
        dcq, dg = _rms_bwd(cq_ref[...], gq_ref[...], dcqn, rq)
        dcq_ref[...] = dcq
        dgq_ref[...] += dg
        dckv, dg = _rms_bwd(ckv_ref[...], gkv_ref[...], dckvn, rkv)
        dckv_ref[...] = dckv
        dgkv_ref[...] += dg

    c2 = lambda i: (0, 0)
    c3 = lambda i: (0, 0, 0)
    hq = pl.BlockSpec((heads, tm, HEADW), lambda i: (0, i, 0))
    return pl.pallas_call(
        body, name="mla_proj_bwd", grid=(t // tm,),
        in_specs=[hq, hq, pl.BlockSpec((heads, tm, VHEAD), lambda i: (0, i, 0)),
                  pl.BlockSpec((tm, rq), lambda i: (i, 0)), pl.BlockSpec((tm, rkv), lambda i: (i, 0)),
                  pl.BlockSpec((tm, LANE), lambda i: (i, 0)), pl.BlockSpec((tm, 1), lambda i: (i, 0)),
                  pl.BlockSpec((1, LANE), c2), pl.BlockSpec((2, LANE), c2),
                  pl.BlockSpec((1, rq), c2), pl.BlockSpec((1, rkv), c2),
                  pl.BlockSpec((heads, HEADW, rq), c3), pl.BlockSpec((heads, HEADW, rkv), c3),
                  pl.BlockSpec((1, HEADW), c2), pl.BlockSpec((1, HEADW), c2)],
        out_specs=[pl.BlockSpec((tm, rq), lambda i: (i, 0)), pl.BlockSpec((tm, rkv), lambda i: (i, 0)),
                   pl.BlockSpec((tm, LANE), lambda i: (i, 0)),
                   pl.BlockSpec((heads, HEADW, rq), c3), pl.BlockSpec((heads, HEADW, rkv), c3),
                   pl.BlockSpec((1, rq), c2), pl.BlockSpec((1, rkv), c2),
                   pl.BlockSpec((1, HEADW), c2), pl.BlockSpec((1, HEADW), c2)],
        out_shape=[jax.ShapeDtypeStruct((t, rq), F32), jax.ShapeDtypeStruct((t, rkv), F32),
                   jax.ShapeDtypeStruct((t, LANE), F32),
                   jax.ShapeDtypeStruct((heads, HEADW, rq), F32), jax.ShapeDtypeStruct((heads, HEADW, rkv), F32),
                   jax.ShapeDtypeStruct((1, rq), F32), jax.ShapeDtypeStruct((1, rkv), F32),
                   jax.ShapeDtypeStruct((1, HEADW), F32), jax.ShapeDtypeStruct((1, HEADW), F32)],
        compiler_params=_params(("arbitrary",)),
    )(dq, dk, dv, cq, ckv, krw, pos, freq, masks, gq, gkv, wq_t, wkv_t, gqh, gkh)


def _lower_triangle(blk):
    return lax.broadcasted_iota(jnp.int32, (blk, blk), 0) >= lax.broadcasted_iota(jnp.int32, (blk, blk), 1)


def _attn_fwd(q, k, v, seq, blk):
    heads, t, _ = q.shape
    scale = QK ** -0.5
    nblk = seq // blk

    def body(q_ref, k_ref, v_ref, o_ref, lse_ref):
        tri = _lower_triangle(blk)
        for qi in range(nblk):
            rows = slice(qi * blk, (qi + 1) * blk)
            before = slice(0, qi * blk)
            qb = q_ref[0, rows, :]
            s_d = jnp.where(tri, _nt(qb, k_ref[0, rows, :]) * scale, -1e30)
            m = jnp.max(s_d, axis=-1, keepdims=True)
            if qi:
                s_b = _nt(qb, k_ref[0, before, :]) * scale
                m = jnp.maximum(m, jnp.max(s_b, axis=-1, keepdims=True))
                p_b = jnp.exp(s_b - m)
            p_d = jnp.exp(s_d - m)
            l = jnp.sum(p_d, axis=-1, keepdims=True)
            acc = _nn(p_d.astype(BF16), v_ref[0, rows, :])
            if qi:
                l += jnp.sum(p_b, axis=-1, keepdims=True)
                acc += _nn(p_b.astype(BF16), v_ref[0, before, :])
            o_ref[0, rows, :] = acc / l
            lse_ref[0, rows, :] = m + jnp.log(l)

    return pl.pallas_call(
        body, name="attn_fwd", grid=(heads, t // seq),
        in_specs=[pl.BlockSpec((1, seq, HEADW), lambda h, b: (h, b, 0)),
                  pl.BlockSpec((1, seq, HEADW), lambda h, b: (h, b, 0)),
                  pl.BlockSpec((1, seq, VHEAD), lambda h, b: (h, b, 0))],
        out_specs=[pl.BlockSpec((1, seq, VHEAD), lambda h, b: (h, b, 0)),
                   pl.BlockSpec((1, seq, 1), lambda h, b: (h, b, 0))],
        out_shape=[jax.ShapeDtypeStruct((heads, t, VHEAD), F32), jax.ShapeDtypeStruct((heads, t, 1), F32)],
        compiler_params=_params(("arbitrary", "arbitrary")),
    )(q, k, v)


def _attn_bwd(q, k, v, do, lse, delta, seq, blk):
    heads, t, _ = q.shape
    scale = QK ** -0.5
    nblk = seq // blk

    def body(q_ref, k_ref, v_ref, do_ref, lse_ref, dl_ref, dq_ref, dk_ref, dv_ref):
        tri = _lower_triangle(blk)
        dk_ref[...] = jnp.zeros_like(dk_ref)
        dv_ref[...] = jnp.zeros_like(dv_ref)
        for qi in range(nblk):
            rows = slice(qi * blk, (qi + 1) * blk)
            qb = q_ref[0, rows, :]
            dob = do_ref[0, rows, :]
            lse_b = lse_ref[0, rows, :]
            dl_b = dl_ref[0, rows, :]
            dq = jnp.zeros((blk, HEADW), F32)
            for keys, masked in ((slice(0, qi * blk), False), (rows, True)):
                if keys.stop == keys.start:
                    continue
                kb = k_ref[0, keys, :]
                p = jnp.exp(_nt(qb, kb) * scale - lse_b)
                if masked:
                    p = jnp.where(tri, p, 0.0)
                dp = _nt(dob, v_ref[0, keys, :])
                ds = (p * (dp - dl_b) * scale).astype(BF16)
                dv_ref[0, keys, :] += _tn(p.astype(BF16), dob)
                dk_ref[0, keys, :] += _tn(ds, qb)
                dq += _nn(ds, kb)
            dq_ref[0, rows, :] = dq

    hq = pl.BlockSpec((1, seq, HEADW), lambda h, b: (h, b, 0))
    hv = pl.BlockSpec((1, seq, VHEAD), lambda h, b: (h, b, 0))
    h1 = pl.BlockSpec((1, seq, 1), lambda h, b: (h, b, 0))
    return pl.pallas_call(
        body, name="attn_bwd", grid=(heads, t // seq),
        in_specs=[hq, hq, hv, hv, h1, h1],
        out_specs=[hq, hq, hv],
        out_shape=[jax.ShapeDtypeStruct((heads, t, HEADW), F32), jax.ShapeDtypeStruct((heads, t, HEADW), F32),
                   jax.ShapeDtypeStruct((heads, t, VHEAD), F32)],
        compiler_params=_params(("arbitrary", "arbitrary")),
    )(q, k, v, do, lse, delta)


def _out_fwd(ya, o, gb, w_out, x1, tm):
    t, w = ya.shape
    heads = o.shape[0]
    d = x1.shape[1]

    def body(ya_ref, o_ref, gb_ref, w_ref, x1_ref, x2_ref, yc_ref):
        yc_ref[:, 0:w] = ya_ref[...]
        for h in range(heads):
            lanes = slice(h * VHEAD, (h + 1) * VHEAD)
            yc_ref[:, w + h * VHEAD:w + (h + 1) * VHEAD] = _rms_fwd(o_ref[h], gb_ref[:, lanes], VHEAD).astype(BF16)
        x2_ref[...] = x1_ref[...] + _nn(yc_ref[...], w_ref[...])

    wy = w + heads * VHEAD
    row = pl.BlockSpec((tm, d), lambda i: (i, 0))
    return pl.pallas_call(
        body, name="out_fwd", grid=(t // tm,),
        in_specs=[pl.BlockSpec((tm, w), lambda i: (i, 0)), pl.BlockSpec((heads, tm, VHEAD), lambda i: (0, i, 0)),
                  pl.BlockSpec((1, heads * VHEAD), lambda i: (0, 0)), WHOLE_VMEM, row],
        out_specs=[row, pl.BlockSpec((tm, wy), lambda i: (i, 0))],
        out_shape=[jax.ShapeDtypeStruct((t, d), F32), jax.ShapeDtypeStruct((t, wy), BF16)],
        compiler_params=_params(("arbitrary",)),
    )(ya, o, gb, w_out, x1)


def _out_bwd(dx2, o, gb, w_out, w, tm):
    t, d = dx2.shape
    heads = o.shape[0]

    def body(dx_ref, o_ref, gb_ref, w_ref, dya_ref, do_ref, dl_ref, dgb_ref):
        @pl.when(pl.program_id(0) == 0)
        def _():
            dgb_ref[...] = jnp.zeros_like(dgb_ref)

        dyc = _nt(dx_ref[...].astype(BF16), w_ref[...])
        dya_ref[...] = dyc[:, 0:w]
        for h in range(heads):
            lanes = slice(h * VHEAD, (h + 1) * VHEAD)
            oh = o_ref[h]
            doh, dg = _rms_bwd(oh, gb_ref[:, lanes], dyc[:, w + h * VHEAD:w + (h + 1) * VHEAD], VHEAD)
            dgb_ref[:, lanes] += dg
            do_ref[h] = doh.astype(BF16)
            dl_ref[h] = jnp.sum(doh * oh, axis=-1, keepdims=True)

    ho = pl.BlockSpec((heads, tm, VHEAD), lambda i: (0, i, 0))
    vec = pl.BlockSpec((1, heads * VHEAD), lambda i: (0, 0))
    return pl.pallas_call(
        body, name="out_bwd", grid=(t // tm,),
        in_specs=[pl.BlockSpec((tm, d), lambda i: (i, 0)), ho, vec, WHOLE_VMEM],
        out_specs=[pl.BlockSpec((tm, w), lambda i: (i, 0)), ho, pl.BlockSpec((heads, tm, 1), lambda i: (0, i, 0)), vec],
        out_shape=[jax.ShapeDtypeStruct((t, w), F32), jax.ShapeDtypeStruct((heads, t, VHEAD), BF16),
                   jax.ShapeDtypeStruct((heads, t, 1), F32), jax.ShapeDtypeStruct((1, heads * VHEAD), F32)],
        compiler_params=_params(("arbitrary",)),
    )(dx2, o, gb, w_out)


def _loss_head(y, target, tm):
    t, d = y.shape

    def body(y_ref, t_ref, dy_ref, loss_ref):
        @pl.when(pl.program_id(0) == 0)
        def _():
            loss_ref[...] = jnp.zeros_like(loss_ref)

        err = y_ref[...] - t_ref[...]
        dy_ref[...] = err * (1.0 / d)
        part = jnp.sum(jnp.sum(err * err, axis=-1, keepdims=True) * (1.0 / d), axis=0, keepdims=True)
        loss_ref[...] += 0.5 * part

    row = pl.BlockSpec((tm, d), lambda i: (i, 0))
    return pl.pallas_call(
        body, name="loss_head", grid=(t // tm,),
        in_specs=[row, row], out_specs=[row, pl.BlockSpec((1, 1), lambda i: (0, 0))],
        out_shape=[jax.ShapeDtypeStruct((t, d), F32), jax.ShapeDtypeStruct((1, 1), F32)],
        compiler_params=_params(("arbitrary",)),
    )(y, target)


def _place():
    return lax.axis_index("x"), lax.axis_index("y"), lax.axis_index("c")


HBM = pl.BlockSpec(memory_space=pltpu.HBM)
SEM = pl.BlockSpec(memory_space=pltpu.SEMAPHORE)
DATAFLOW = pltpu.SideEffectType.DATAFLOW_SIDE_EFFECTING


def _plan_copies(plan, refs, send_sems, recv_sems):
    cps = []
    for i, (sb, sblk, db, dblk, dev) in enumerate(plan(*_place())):
        cps.append(pltpu.make_async_remote_copy(
            src_ref=refs[sb] if sblk is None else refs[sb].at[sblk], dst_ref=refs[db].at[dblk],
            send_sem=send_sems.at[i], recv_sem=recv_sems.at[i], device_id=dev, device_id_type=MESH))
    return cps


def _push_start(bufs, plan, ncopy, name, deps=()):
    nb = len(bufs)

    def body(*refs):
        outs = refs[nb + len(deps):]
        for cp in _plan_copies(plan, refs[:nb], outs[0], outs[1]):
            cp.start()
        outs[-1][...] = jnp.zeros_like(outs[-1])

    res = pl.pallas_call(
        body, name=name,
        out_shape=(pltpu.SemaphoreType.DMA((ncopy,)), pltpu.SemaphoreType.DMA((ncopy,)),
                   *[pltpu.HBM(b.shape, b.dtype) for b in bufs], jax.ShapeDtypeStruct((SUBLANE, LANE), F32)),
        in_specs=[HBM] * nb + [ANY] * len(deps),
        out_specs=(SEM, SEM, *[HBM] * nb, WHOLE_VMEM),
        input_output_aliases={i: 2 + i for i in range(nb)},
        compiler_params=pltpu.CompilerParams(has_side_effects=DATAFLOW),
    )(*[pltpu.with_memory_space_constraint(b, pltpu.HBM) for b in bufs], *deps)
    return res[0], res[1], list(res[2:2 + nb]), res[-1]


def _push_wait(send_sems, recv_sems, bufs, plan, after, name):
    nb = len(bufs)

    def body(*refs):
        for cp in _plan_copies(plan, refs[:nb], refs[nb], refs[nb + 1]):
            cp.wait_send()
            cp.wait_recv()

    res = pl.pallas_call(
        body, name=name,
        out_shape=[pltpu.HBM(b.shape, b.dtype) for b in bufs],
        in_specs=[HBM] * nb + [SEM, SEM, ANY], out_specs=[HBM] * nb,
        input_output_aliases={i: i for i in range(nb)},
        compiler_params=pltpu.CompilerParams(has_side_effects=DATAFLOW),
    )(*bufs, send_sems, recv_sems, after)
    return list(res)


def _other_chips(x, y):
    return ((1 - x, y), (x, 1 - y), (1 - x, 1 - y))


class _Exchange:
    def __init__(self, bufs, plan, ncopy, name, deps=()):
        self.plan, self.name = plan, name
        self.send, self.recv, self.bufs, self.token = _push_start(bufs, plan, ncopy, name + "_start", deps)

    def wait(self, after):
        return _push_wait(self.send, self.recv, self.bufs, self.plan, after, self.name + "_wait")


class _Chain:
    def __init__(self, bufs):
        self.bufs = list(bufs)

    def start(self, plan, ncopy, name, deps=()):
        send, recv, self.bufs, token = _push_start(self.bufs, plan, ncopy, name + "_start", deps)
        return (send, recv, plan, name), token

    def wait(self, pending, after):
        send, recv, plan, name = pending
        self.bufs = _push_wait(send, recv, self.bufs, plan, after, name + "_wait")


def _gather_ici(shards, me, name, deps=()):
    n = len(shards)
    lands = [lax.dynamic_update_slice(lax.empty((N_DEV,) + s.shape, s.dtype), s[None], (me, 0, 0)) for s in shards]

    def plan(x, y, c):
        return [(a, None, n + a, 4 * x + 2 * y + c, (px, py, c)) for a in range(n) for px, py in _other_chips(x, y)]

    return _Exchange(list(shards) + lands, plan, 3 * n, name, deps)


def _gather_d2d(lands, name, deps=()):
    n = len(lands)

    def plan(x, y, c):
        blocks = [4 * x + 2 * y + c] + [4 * px + 2 * py + c for px, py in _other_chips(x, y)]
        return [(a, b, a, b, (x, y, 1 - c)) for a in range(n) for b in blocks]

    return _Exchange(list(lands), plan, 4 * n, name, deps)


def _reduce_d2d(grads, name, deps=()):
    n = len(grads)
    lands = [lax.empty((4,) + g.shape[1:], g.dtype) for g in grads]

    def plan(x, y, c):
        return [(a, 2 * s + (1 - c), n + a, s, (x, y, 1 - c)) for a in range(n) for s in range(4)]

    return _Exchange(list(grads) + lands, plan, 4 * n, name, deps)


def _reduce_ici(chip, name, deps=()):
    n = len(chip)
    lands = [lax.empty((3,) + g.shape[1:], g.dtype) for g in chip]

    def plan(x, y, c):
        return [(a, 2 * px + py, n + a, k, (px, py, c))
                for a in range(n) for k, (px, py) in enumerate(_other_chips(x, y))]

    return _Exchange(list(chip) + lands, plan, 3 * n, name, deps)


def _pair_add(full, got, core, name):
    _, r, cdim = full.shape
    br = _row_block(r, 512)

    def body(c_ref, f_ref, g_ref, o_ref):
        o_ref[...] = (f_ref[...].astype(F32) + g_ref[...].astype(F32)).astype(o_ref.dtype)

    return pl.pallas_call(
        body, name=name,
        grid_spec=pltpu.PrefetchScalarGridSpec(
            num_scalar_prefetch=1, grid=(4, r // br),
            in_specs=[pl.BlockSpec((1, br, cdim), lambda s, i, c_ref: (2 * s + c_ref[0], i, 0)),
                      pl.BlockSpec((1, br, cdim), lambda s, i, c_ref: (s, i, 0))],
            out_specs=pl.BlockSpec((1, br, cdim), lambda s, i, c_ref: (s, i, 0))),
        out_shape=jax.ShapeDtypeStruct((4, r, cdim), full.dtype),
        compiler_params=_params(("arbitrary", "arbitrary")),
    )(core, full, got)


def _sum_owned(chip, got, slot, rows, name):
    cdim = chip.shape[2]
    br = _row_block(rows, 512)

    def body(s_ref, c_ref, g_ref, o_ref):
        acc = c_ref[0].astype(F32)
        for k in range(3):
            acc = acc + g_ref[k].astype(F32)
        o_ref[...] = acc

    return pl.pallas_call(
        body, name=name,
        grid_spec=pltpu.PrefetchScalarGridSpec(
            num_scalar_prefetch=1, grid=(rows // br,),
            in_specs=[pl.BlockSpec((1, br, cdim), lambda i, s_ref: (s_ref[0], i, 0)),
                      pl.BlockSpec((3, br, cdim), lambda i, s_ref: (0, i, 0))],
            out_specs=pl.BlockSpec((br, cdim), lambda i, s_ref: (i, 0))),
        out_shape=jax.ShapeDtypeStruct((rows, cdim), F32),
        compiler_params=_params(("arbitrary",)),
    )(slot, chip, got)


def _sum_devices(stack):
    _, r, cdim = stack.shape

    def body(s_ref, o_ref):
        acc = s_ref[0]
        for k in range(1, N_DEV):
            acc = acc + s_ref[k]
        o_ref[...] = acc

    return pl.pallas_call(
        body, name="sum_devices", out_shape=jax.ShapeDtypeStruct((r, cdim), F32),
        compiler_params=_params(),
    )(stack)


def _adamw(w, g, m, v, name):
    r, cdim = w.shape
    br = _row_block(r, 256)

    def body(w_ref, g_ref, m_ref, v_ref, d_ref, nm_ref, nv_ref):
        g = g_ref[...]
        nm = ADAM_B1 * m_ref[...] + (1.0 - ADAM_B1) * g
        nv = ADAM_B2 * v_ref[...] + (1.0 - ADAM_B2) * (g * g)
        m_hat = nm / (1.0 - ADAM_B1 ** ADAM_STEP)
        v_hat = nv / (1.0 - ADAM_B2 ** ADAM_STEP)
        d_ref[...] = -ADAM_LR * (m_hat / (jnp.sqrt(v_hat) + ADAM_EPS) + ADAM_WD * w_ref[...])
        nm_ref[...] = nm
        nv_ref[...] = nv

    spec = pl.BlockSpec((br, cdim), lambda i: (i, 0))
    shape = jax.ShapeDtypeStruct((r, cdim), F32)
    return pl.pallas_call(
        body, name=name, grid=(r // br,), in_specs=[spec] * 4, out_specs=[spec] * 3,
        out_shape=[shape] * 3, compiler_params=_params(("arbitrary",)),
    )(w, g, m, v)


WEIGHTS = ("ffn1_norm_g", "ffn1_w_gate", "ffn1_w_up", "ffn1_w_down", "mix_norm_g", "w_in", "gmlp_v_norm_g",
           "gmlp_w_s", "gmlp_b_s", "mla_q_norm_g", "mla_w_q_up", "mla_kv_norm_g", "mla_w_kv_up", "mla_q_head_g",
           "mla_k_head_g", "gmlp_out_g", "mla_out_g", "w_out", "ffn2_norm_g", "ffn2_w_gate", "ffn2_w_up",
           "ffn2_w_down")
SHARDED = {"ffn1_w_gate": True, "ffn1_w_up": True, "ffn1_w_down": False, "w_in": True, "mla_w_q_up": True,
           "mla_w_kv_up": True, "w_out": False, "ffn2_w_gate": True, "ffn2_w_up": True, "ffn2_w_down": False}


def _col_block(m, target):
    best = LANE
    for cand in range(LANE, min(m, target) + 1, LANE):
        if m % cand == 0:
            best = cand
    return best


def _shard_rows(w, transposed, pad_to=None):
    rows = (w[0].T if transposed else w[0]).astype(BF16)
    if pad_to is not None and pad_to != rows.shape[0]:
        rows = jnp.pad(rows, ((0, pad_to - rows.shape[0]), (0, 0)))
    return rows


def _pack(parts):
    flat = []
    for p in parts:
        f = p.reshape(-1).astype(F32)
        flat.append(jnp.pad(f, (0, _round_up(f.size, LANE) - f.size)))
    flat = jnp.concatenate(flat)
    rows = _round_up(flat.size // LANE, SUBLANE)
    return jnp.pad(flat, (0, rows * LANE - flat.size)).reshape(rows, LANE)


def _unpack(packed, shapes):
    out, row = [], 0
    for shp in shapes:
        size = 1
        for s in shp:
            size *= s
        nrows = _round_up(size, LANE) // LANE
        out.append(packed[row:row + nrows].reshape(-1)[:size].reshape(shp))
        row += nrows
    return out


def kernel(x, positions, ffn1_norm_g, ffn1_w_gate, ffn1_w_up, ffn1_w_down, mix_norm_g, w_in, gmlp_v_norm_g, gmlp_w_s, gmlp_b_s, mla_q_norm_g, mla_w_q_up, mla_kv_norm_g, mla_w_kv_up, mla_q_head_g, mla_k_head_g, gmlp_out_g, mla_out_g, w_out, ffn2_norm_g, ffn2_w_gate, ffn2_w_up, ffn2_w_down, loss_target, m_ffn1_norm_g, m_ffn1_w_gate, m_ffn1_w_up, m_ffn1_w_down, m_mix_norm_g, m_w_in, m_gmlp_v_norm_g, m_gmlp_w_s, m_gmlp_b_s, m_mla_q_norm_g, m_mla_w_q_up, m_mla_kv_norm_g, m_mla_w_kv_up, m_mla_q_head_g, m_mla_k_head_g, m_gmlp_out_g, m_mla_out_g, m_w_out, m_ffn2_norm_g, m_ffn2_w_gate, m_ffn2_w_up, m_ffn2_w_down, v_ffn1_norm_g, v_ffn1_w_gate, v_ffn1_w_up, v_ffn1_w_down, v_mix_norm_g, v_w_in, v_gmlp_v_norm_g, v_gmlp_w_s, v_gmlp_b_s, v_mla_q_norm_g, v_mla_w_q_up, v_mla_kv_norm_g, v_mla_w_kv_up, v_mla_q_head_g, v_mla_k_head_g, v_gmlp_out_g, v_mla_out_g, v_w_out, v_ffn2_norm_g, v_ffn2_w_gate, v_ffn2_w_up, v_ffn2_w_down):
    wts = dict(zip(WEIGHTS, (ffn1_norm_g, ffn1_w_gate, ffn1_w_up, ffn1_w_down, mix_norm_g, w_in, gmlp_v_norm_g, gmlp_w_s, gmlp_b_s, mla_q_norm_g, mla_w_q_up, mla_kv_norm_g, mla_w_kv_up, mla_q_head_g, mla_k_head_g, gmlp_out_g, mla_out_g, w_out, ffn2_norm_g, ffn2_w_gate, ffn2_w_up, ffn2_w_down)))
    mom1 = dict(zip(WEIGHTS, (m_ffn1_norm_g, m_ffn1_w_gate, m_ffn1_w_up, m_ffn1_w_down, m_mix_norm_g, m_w_in, m_gmlp_v_norm_g, m_gmlp_w_s, m_gmlp_b_s, m_mla_q_norm_g, m_mla_w_q_up, m_mla_kv_norm_g, m_mla_w_kv_up, m_mla_q_head_g, m_mla_k_head_g, m_gmlp_out_g, m_mla_out_g, m_w_out, m_ffn2_norm_g, m_ffn2_w_gate, m_ffn2_w_up, m_ffn2_w_down)))
    mom2 = dict(zip(WEIGHTS, (v_ffn1_norm_g, v_ffn1_w_gate, v_ffn1_w_up, v_ffn1_w_down, v_mix_norm_g, v_w_in, v_gmlp_v_norm_g, v_gmlp_w_s, v_gmlp_b_s, v_mla_q_norm_g, v_mla_w_q_up, v_mla_kv_norm_g, v_mla_w_kv_up, v_mla_q_head_g, v_mla_k_head_g, v_gmlp_out_g, v_mla_out_g, v_w_out, v_ffn2_norm_g, v_ffn2_w_gate, v_ffn2_w_up, v_ffn2_w_down)))

    b_loc, seq, d = x.shape
    t = b_loc * seq
    ffs = ffn1_w_gate.shape[2]
    fp = _round_up(ffs, LANE)
    wg = gmlp_v_norm_g.shape[1]
    groups = gmlp_w_s.shape[1]
    rq, rkv = mla_q_norm_g.shape[1], mla_kv_norm_g.shape[1]
    heads = mla_out_g.shape[1]
    assert w_in.shape[2] * N_DEV == 2 * wg + rq + rkv + ROPE and mla_w_kv_up.shape[2] * N_DEV == heads * HEADW
    tm = min(512, t)
    tm_mix = min(256, t)
    blk = min(256, seq)

    xf = x.reshape(t, d)
    target = loss_target.reshape(t, d)
    pos = positions.reshape(t, 1).astype(F32)
    half = ROPE // 2
    inv_freq = 1.0 / (ROPE_THETA ** (jnp.arange(half, dtype=F32) / half))
    freq = jnp.concatenate([inv_freq, inv_freq, jnp.zeros((LANE - ROPE,), F32)])[None, :]
    lane = jnp.arange(LANE)
    masks = jnp.stack([jnp.where(lane < half, -1.0, 0.0),
                       jnp.where((lane >= half) & (lane < ROPE), 1.0, 0.0)]).astype(F32)
    gqh = jnp.pad(mla_q_head_g, ((0, 0), (0, HEADW - QK)))
    gkh = jnp.pad(mla_k_head_g, ((0, 0), (0, HEADW - QK)))
    bias = jnp.repeat(gmlp_b_s[0].T, CHUNK, axis=1)
    gouta = gmlp_out_g.reshape(1, wg)
    goutb = mla_out_g.reshape(1, heads * VHEAD)
    ws = gmlp_w_s[0]

    px, py, pc = _place()
    me = 4 * px + 2 * py + pc
    core = pc.astype(jnp.int32).reshape(1)
    slot = (2 * px + py).astype(jnp.int32).reshape(1)
    order = [n for n in WEIGHTS if n in SHARDED]
    group = {"ffn1": [n for n in order if n.startswith("ffn1")], "ffn2": [n for n in order if n.startswith("ffn2")],
             "mix": [n for n in order if not n.startswith("ffn")]}
    shard = {n: _shard_rows(wts[n], SHARDED[n], fp if n.startswith("ffn") else None) for n in order}

    def tied(arr, token):
        return arr + token[0, 0].astype(arr.dtype)

    n1 = len(group["ffn1"])
    sib, xnb, ynb, dgn = 4 * px + 2 * py + (1 - pc), 4 * (1 - px) + 2 * py, 4 * px + 2 * (1 - py), 4 * (1 - px) + 2 * (1 - py)
    ids_a = jnp.stack([me, sib]).astype(jnp.int32)
    ids_b = jnp.stack([xnb, xnb + 1, ynb, ynb + 1]).astype(jnp.int32)
    ids_c = jnp.stack([dgn, dgn + 1]).astype(jnp.int32)

    def plan_own(x, y, c):
        return [(a, None, n1 + a, 4 * x + 2 * y + c, (x, y, 1 - c)) for a in range(n1)]

    def plan_nbr(x, y, c):
        return [(a, None, n1 + a, 4 * x + 2 * y + c, dev) for a in range(n1) for dev in ((1 - x, y, c), (x, 1 - y, c))]

    def plan_nbr_d2d(x, y, c):
        return [(n1 + a, b, n1 + a, b, (x, y, 1 - c))
                for a in range(n1) for b in (4 * (1 - x) + 2 * y + c, 4 * x + 2 * (1 - y) + c)]

    def plan_diag(x, y, c):
        sx, sy = (1 - x) * (1 - c) + x * c, y * (1 - c) + (1 - y) * c
        tx, ty = x * (1 - c) + (1 - x) * c, (1 - y) * (1 - c) + y * c
        b = 4 * sx + 2 * sy + c
        return [(n1 + a, b, n1 + a, b, (tx, ty, c)) for a in range(n1)]

    def plan_diag_d2d(x, y, c):
        b = 4 * (1 - x) + 2 * (1 - y) + c
        return [(n1 + a, b, n1 + a, b, (x, y, 1 - c)) for a in range(n1)]

    own = [shard[n] for n in group["ffn1"]]
    ch = _Chain(own + [lax.dynamic_update_slice(lax.empty((N_DEV,) + s.shape, s.dtype), s[None], (me, 0, 0))
                       for s in own])
    p_own, token = ch.start(plan_own, n1, "gather_ffn1_own")
    p_nbr, token = ch.start(plan_nbr, 2 * n1, "gather_ffn1_nbr", deps=(token,))
    ch.wait(p_own, token)
    x1, xn1, a1, b1 = _ffn_fwd(xf, None, ffn1_norm_g, ids_a, *ch.bufs[n1:], None, None, tm, "ffn1_fwd_a")
    ch.wait(p_nbr, x1)
    p_diag, token = ch.start(plan_diag, n1, "gather_ffn1_diag")
    p_nd, token = ch.start(plan_nbr_d2d, 2 * n1, "gather_ffn1_nbr_d2d", deps=(token,))
    ch.wait(p_nd, token)
    x1, xn1, a1, b1 = _ffn_fwd(x1, xn1, None, ids_b, *ch.bufs[n1:], a1, b1, tm, "ffn1_fwd_b")
    ch.wait(p_diag, x1)
    p_dd, token = ch.start(plan_diag_d2d, n1, "gather_ffn1_diag_d2d")
    ici2 = _gather_ici([shard[n] for n in group["mix"]], me, "gather_mix_ici", deps=(token,))
    ici3 = _gather_ici([shard[n] for n in group["ffn2"]], me, "gather_ffn2_ici", deps=(ici2.token,))
    ch.wait(p_dd, ici3.token)
    full = dict(zip(group["ffn1"], ch.bufs[n1:]))
    x1, xn1, a1, b1 = _ffn_fwd(x1, xn1, None, ids_c, full["ffn1_w_gate"], full["ffn1_w_up"], full["ffn1_w_down"],
                               a1, b1, tm, "ffn1_fwd_c")
    d2d2 = _gather_d2d(ici2.wait(x1)[len(group["mix"]):], "gather_mix_d2d")
    full.update(zip(group["mix"], d2d2.wait(d2d2.token)))
    win_t = jnp.pad(full["w_in"].reshape(-1, d), ((0, LANE - ROPE), (0, 0)))
    splits = (2 * wg, rq, rkv, LANE)
    wq_t = jnp.pad(full["mla_w_q_up"].reshape(heads, QK, rq), ((0, 0), (0, HEADW - QK), (0, 0)))
    wkv_t = full["mla_w_kv_up"].reshape(heads, HEADW, rkv)
    wout = full["w_out"].reshape(-1, d)
    hn, zuv, cq, ckv, krw = _inproj_fwd(x1, mix_norm_g, win_t, splits, tm)
    ya = _gmlp_fwd(zuv, gmlp_v_norm_g, ws, bias, gouta, tm_mix)
    q, k, vv = _mla_proj_fwd(cq, ckv, krw, pos, freq, masks, mla_q_norm_g, mla_kv_norm_g, wq_t, wkv_t, gqh, gkh,
                             tm_mix)
    d2d3 = _gather_d2d(ici3.wait(q)[3:], "gather_ffn2_d2d")
    o, lse = _attn_fwd(q, k, vv, seq, blk)
    x2, ycat = _out_fwd(ya, o, tied(goutb, d2d3.token), wout, x1, tm)
    full.update(zip(group["ffn2"], d2d3.wait(x2)))
    x3, xn2, a2, b2 = _ffn_fwd(x2, None, ffn2_norm_g, jnp.arange(N_DEV, dtype=jnp.int32), full["ffn2_w_gate"],
                               full["ffn2_w_up"], full["ffn2_w_down"], None, None, tm, "ffn2_fwd")
    dx3, loss_part = _loss_head(x3, target, tm)

    outs_g, outs_d, outs_m, outs_v = {}, {}, {}, {}

    def finish(names, chip, got):
        for n, cp, gt in zip(names, chip, got):
            rows = wts[n].shape[2] if SHARDED[n] else wts[n].shape[1]
            g = _sum_owned(cp, gt, slot, rows, "sum_owned_" + n)
            g = g.T if SHARDED[n] else g
            dlt, nm, nv = _adamw(wts[n][0], g, mom1[n][0], mom2[n][0], "adamw_" + n)
            outs_g[n], outs_d[n], outs_m[n], outs_v[n] = g[None], dlt[None], nm[None], nv[None]

    def chip_sums(names, ex, after):
        res = ex.wait(after)
        return [_pair_add(f, gt, core, "pair_add_" + n) for n, f, gt in zip(names, res[:len(names)], res[len(names):])]

    tk = min(512, t)
    grads = {}
    small = {}
    dx2, small["ffn2_norm_g"], da2, db2, h2 = _ffn_bwd(
        dx3, x2, ffn2_norm_g, a2, b2, full["ffn2_w_gate"], full["ffn2_w_up"], full["ffn2_w_down"], tm, "ffn2_bwd")
    grads["ffn2_w_gate"] = _matmul_tn(da2, xn2, fp, d, tk, BF16, "dw_ffn2_gate").reshape(N_DEV, fp, d)
    grads["ffn2_w_up"] = _matmul_tn(db2, xn2, fp, d, tk, BF16, "dw_ffn2_up").reshape(N_DEV, fp, d)
    grads["ffn2_w_down"] = _matmul_tn(h2, dx3, fp, d, tk, BF16, "dw_ffn2_down", rhs_scale=0.5).reshape(
        N_DEV, fp, d)
    red_a2 = _reduce_d2d([grads[n] for n in group["ffn2"]], "reduce_ffn2_d2d")
    dya, do, delta, small["mla_out_g"] = _out_bwd(dx2, o, tied(goutb, red_a2.token), wout, wg, tm)
    grads["w_out"] = _matmul_tn(ycat, dx2, _col_block(ycat.shape[1], 768), d, tk, BF16, "dw_out").reshape(
        N_DEV, -1, d)
    chip2 = chip_sums(group["ffn2"], red_a2, grads["w_out"])
    red_b2 = _reduce_ici(chip2, "reduce_ffn2_ici")
    dq, dk, dv = _attn_bwd(q, k, vv, do, lse, tied(delta, red_b2.token), seq, blk)
    (dcq, dckv, dkrw, dwq, dwkv, small["mla_q_norm_g"], small["mla_kv_norm_g"], dgqh, dgkh) = _mla_proj_bwd(
        dq, dk, dv, cq, ckv, krw, pos, freq, masks, mla_q_norm_g, mla_kv_norm_g, wq_t, wkv_t, gqh, gkh, tm_mix)
    small["mla_q_head_g"], small["mla_k_head_g"] = dgqh[:, :QK], dgkh[:, :QK]
    grads["mla_w_q_up"] = dwq[:, :QK].astype(BF16).reshape(N_DEV, -1, rq)
    grads["mla_w_kv_up"] = dwkv.astype(BF16).reshape(N_DEV, -1, rkv)
    dzuv, small["gmlp_w_s"], dbs, small["gmlp_v_norm_g"], small["gmlp_out_g"] = _gmlp_bwd(
        dya, zuv, gmlp_v_norm_g, ws, bias, gouta, tm_mix)
    small["gmlp_b_s"] = dbs[:, :, 0]
    dx1, small["mix_norm_g"], dzc = _inproj_bwd([dzuv, dcq, dckv, dkrw], x1, mix_norm_g, win_t, dx2, splits,
                                                tm_mix)
    dwin = _matmul_tn(dzc, hn, _col_block(dzc.shape[1], 768), d, tk, BF16, "dw_in")
    grads["w_in"] = dwin[:N_DEV * w_in.shape[2]].reshape(N_DEV, -1, d)
    res_b2 = red_b2.wait(grads["w_in"])
    red_am = _reduce_d2d([grads[n] for n in group["mix"]], "reduce_mix_d2d")
    dx0, small["ffn1_norm_g"], da1, db1, h1 = _ffn_bwd(
        dx1, xf, tied(ffn1_norm_g, red_am.token), a1, b1, full["ffn1_w_gate"], full["ffn1_w_up"],
        full["ffn1_w_down"], tm, "ffn1_bwd")
    chipm = chip_sums(group["mix"], red_am, dx0)
    red_bm = _reduce_ici(chipm, "reduce_mix_ici")
    rep = [n for n in WEIGHTS if n not in SHARDED]
    small_ici = _gather_ici([_pack([small[n] for n in rep] + [loss_part])], me, "gather_small_ici",
                            deps=(red_bm.token,))
    operands = {"ffn1_w_gate": (da1, xn1, None), "ffn1_w_up": (db1, xn1, None), "ffn1_w_down": (h1, dx1, 0.5)}
    token, red_a, red_b, small_d2d = small_ici.token, None, [], None
    for n in group["ffn1"]:
        lhs, rhs, scale = operands[n]
        gr = _matmul_tn(lhs, rhs, fp, d, tk, BF16, "dw_" + n, rhs_scale=scale).reshape(N_DEV, fp, d)
        if red_a is not None:
            prev, ex = red_a
            red_b.append((prev, _reduce_ici(chip_sums([prev], ex, gr), "reduce_" + prev + "_ici")))
            token = red_b[-1][1].token
            if small_d2d is None:
                small_d2d = _gather_d2d(small_ici.wait(gr)[1:], "gather_small_d2d", deps=(token,))
                token = small_d2d.token
        red_a = (n, _reduce_d2d([gr], "reduce_" + n + "_d2d", deps=(token,)))
    prev, ex = red_a
    red_b.append((prev, _reduce_ici(chip_sums([prev], ex, ex.token), "reduce_" + prev + "_ici")))
    token = red_b[-1][1].token
    slot = slot + token[0, 0].astype(jnp.int32)
    finish(group["ffn2"], res_b2[:3], res_b2[3:])
    res = red_bm.wait(outs_g[group["ffn2"][-1]])
    nm_ = len(group["mix"])
    finish(group["mix"], res[:nm_], res[nm_:])
    total = _sum_devices(small_d2d.wait(outs_g[group["mix"][-1]])[0])
    zero = jnp.zeros((1,), F32)
    dlt, nm, nv = _adamw(_pack([wts[n] for n in rep] + [zero]), total, _pack([mom1[n] for n in rep] + [zero]),
                         _pack([mom2[n] for n in rep] + [zero]), "adamw_small")
    shapes = [wts[n].shape for n in rep] + [(1,)]
    for n, g, dl, m1, m2 in zip(rep, _unpack(total, shapes), _unpack(dlt, shapes), _unpack(nm, shapes),
                                _unpack(nv, shapes)):
        outs_g[n], outs_d[n], outs_m[n], outs_v[n] = g, dl, m1, m2
    loss = _unpack(total, shapes)[-1].reshape(())
    for n, ex in red_b:
        res = ex.wait(dlt)
        finish([n], res[:1], res[1:])

    return (loss, dx0.reshape(b_loc, seq, d), *[outs_g[n] for n in WEIGHTS], *[outs_d[n] for n in WEIGHTS],
            *[outs_m[n] for n in WEIGHTS], *[outs_v[n] for n in WEIGHTS])
```

```python
import functools

import jax
import jax.numpy as jnp
from jax import lax
from jax.experimental import pallas as pl
from jax.experimental.pallas import tpu as pltpu

F32 = jnp.float32
BF16 = jnp.bfloat16
EPS = 1e-6
LANE = 128
SUBLANE = 8
N_DEV = 8
VMEM_LIMIT = 60 * 1024 * 1024
NOPE = 128
ROPE = 64
VHEAD = 128
QK = NOPE + ROPE
HEADW = 2 * LANE
CHUNK = 128
ROPE_THETA = 10000.0
ADAM_LR, ADAM_B1, ADAM_B2, ADAM_EPS, ADAM_WD, ADAM_STEP = 0.001, 0.9, 0.999, 1e-08, 0.01, 10
MESH = pl.DeviceIdType.MESH
ANY = pl.BlockSpec(memory_space=pl.ANY)
WHOLE_VMEM = pl.BlockSpec(memory_space=pltpu.VMEM)


def _params(sem=None):
    return pltpu.CompilerParams(dimension_semantics=sem, vmem_limit_bytes=VMEM_LIMIT)


def _round_up(n, m):
    return -(-n // m) * m


def _row_block(rows, target):
    best = rows
    for cand in range(SUBLANE, min(rows, target) + 1, SUBLANE):
        if rows % cand == 0:
            best = cand
    return best if best <= target else rows


def _nn(a, b):
    return jnp.dot(a, b, preferred_element_type=F32)


def _nt(a, b):
    return lax.dot_general(a, b, (((1,), (1,)), ((), ())), preferred_element_type=F32)


def _tn(a, b):
    return lax.dot_general(a, b, (((0,), (0,)), ((), ())), preferred_element_type=F32)


def _rstd(x, n):
    return lax.rsqrt(jnp.sum(x * x, axis=-1, keepdims=True) * (1.0 / n) + EPS)


def _rms_fwd(x, g, n):
    return x * _rstd(x, n) * g


def _rms_bwd(x, g, dy, n):
    r = _rstd(x, n)
    xh = x * r
    dyg = dy * g
    dx = r * (dyg - xh * (jnp.sum(dyg * xh, axis=-1, keepdims=True) * (1.0 / n)))
    return dx, jnp.sum(dy * xh, axis=0, keepdims=True)


def _gelu(x):
    return 0.5 * x * (1.0 + lax.erf(x * 0.7071067811865476))


def _gelu_grad(x):
    return 0.5 * (1.0 + lax.erf(x * 0.7071067811865476)) + x * jnp.exp(-0.5 * x * x) * 0.3989422804014327


def _ffn_fwd(base, xn, g, ids, wg_t, wu_t, wd, saved, tm, name):
    t, d = base.shape
    nb, fp, _ = wg_t.shape
    n = ids.shape[0]
    first = xn is None
    if saved is None:
        saved = [lax.empty((t, nb * fp), BF16) for _ in range(3)]

    def body(ids_ref, *refs):
        if first:
            base_ref, g_ref, wg_ref, wu_ref, wd_ref, _, _, _, out_ref, xn_ref, gd_ref, sl_ref, h_ref, acc = refs
        else:
            base_ref, xn_ref, wg_ref, wu_ref, wd_ref, _, _, _, out_ref, gd_ref, sl_ref, h_ref, acc = refs
        j = pl.program_id(1)

        @pl.when(j == 0)
        def _():
            if first:
                xn_ref[...] = _rms_fwd(base_ref[...], g_ref[...], d).astype(BF16)
            acc[...] = jnp.zeros_like(acc)

        xnb = xn_ref[...]
        a = _nt(xnb, wg_ref[0])
        b = _nt(xnb, wu_ref[0])
        s = jax.nn.sigmoid(a)
        sl = a * s
        h = (sl * b).astype(BF16)
        gd_ref[...] = (b * (s * (1.0 + a * (1.0 - s)))).astype(BF16)
        sl_ref[...] = sl.astype(BF16)
        h_ref[...] = h
        acc[...] += _nn(h, wd_ref[0])

        @pl.when(j == n - 1)
        def _():
            out_ref[...] = base_ref[...] + 0.5 * acc[...]

    wspec = pl.BlockSpec((1, fp, d), lambda i, j, ids_ref: (ids_ref[j], 0, 0))
    row = pl.BlockSpec((tm, d), lambda i, j, ids_ref: (i, 0))
    ff = pl.BlockSpec((tm, fp), lambda i, j, ids_ref: (i, ids_ref[j]))
    ffs = jax.ShapeDtypeStruct((t, nb * fp), BF16)
    second = pl.BlockSpec((1, d), lambda i, j, ids_ref: (0, 0)) if first else row
    n_row_outs = 2 if first else 1
    res = pl.pallas_call(
        body, name=name,
        grid_spec=pltpu.PrefetchScalarGridSpec(
            num_scalar_prefetch=1, grid=(t // tm, n),
            in_specs=[row, second, wspec, wspec, wspec, ANY, ANY, ANY],
            out_specs=[row] * n_row_outs + [ff, ff, ff],
            scratch_shapes=[pltpu.VMEM((tm, d), F32)]),
        out_shape=[jax.ShapeDtypeStruct((t, d), F32)] + ([jax.ShapeDtypeStruct((t, d), BF16)] if first else [])
        + [ffs, ffs, ffs],
        input_output_aliases={6 + k: n_row_outs + k for k in range(3)},
        compiler_params=_params(("arbitrary", "arbitrary")),
    )(ids, base, g if first else xn, wg_t, wu_t, wd, *saved)
    return (res[0], res[1] if first else xn, list(res[n_row_outs:]))


def _ffn_bwd(dout, x, g, gd, sl, wg_t, wu_t, wd, tm, name):
    t, d = x.shape
    nb, fp, _ = wg_t.shape

    def body(do_hbm, x_hbm, g_ref, gd_ref, sl_ref, wg_ref, wu_ref, wd_ref,
             dx_ref, dg_ref, da_ref, db_ref, acc, rowbuf, dy_scr, sem):
        i, j = pl.program_id(0), pl.program_id(1)
        rows = pl.ds(pl.multiple_of(i * tm, tm), tm)

        def fetch(src):
            cp = pltpu.make_async_copy(src.at[rows, :], rowbuf, sem)
            cp.start()
            cp.wait()

        @pl.when(j == 0)
        def _():
            fetch(do_hbm)
            dy_scr[...] = (0.5 * rowbuf[...]).astype(BF16)
            acc[...] = jnp.zeros_like(acc)

        @pl.when((i == 0) & (j == 0))
        def _():
            dg_ref[...] = jnp.zeros_like(dg_ref)

        dh = _nt(dy_scr[...], wd_ref[0])
        da = (dh * gd_ref[...].astype(F32)).astype(BF16)
        db = (dh * sl_ref[...].astype(F32)).astype(BF16)
        da_ref[...] = da
        db_ref[...] = db
        acc[...] += _nn(da, wg_ref[0]) + _nn(db, wu_ref[0])

        @pl.when(j == nb - 1)
        def _():
            fetch(x_hbm)
            dxn, dg = _rms_bwd(rowbuf[...], g_ref[...], acc[...], d)
            dg_ref[...] += dg
            dx_ref[...] = dxn
            fetch(do_hbm)
            dx_ref[...] += rowbuf[...]

    wspec = pl.BlockSpec((1, fp, d), lambda i, j: (j, 0, 0))
    row = pl.BlockSpec((tm, d), lambda i, j: (i, 0))
    vec = pl.BlockSpec((1, d), lambda i, j: (0, 0))
    ff = pl.BlockSpec((tm, fp), lambda i, j: (i, j))
    ffs = jax.ShapeDtypeStruct((t, nb * fp), BF16)
    return pl.pallas_call(
        body, name=name, grid=(t // tm, nb),
        in_specs=[ANY, ANY, vec, ff, ff, wspec, wspec, wspec],
        out_specs=[row, vec, ff, ff],
        out_shape=[jax.ShapeDtypeStruct((t, d), F32), jax.ShapeDtypeStruct((1, d), F32), ffs, ffs],
        scratch_shapes=[pltpu.VMEM((tm, d), F32), pltpu.VMEM((tm, d), F32), pltpu.VMEM((tm, d), BF16),
                        pltpu.SemaphoreType.DMA],
        compiler_params=_params(("arbitrary", "arbitrary")),
    )(dout, x, g, gd, sl, wg_t, wu_t, wd)


def _matmul_tn(lhs, rhs, bm, bn, tk, out_dtype, name, rhs_scale=None):
    t, m = lhs.shape
    n = rhs.shape[1]
    nk = t // tk

    def body(l_ref, r_ref, o_ref, acc):
        k = pl.program_id(2)

        @pl.when(k == 0)
        def _():
            acc[...] = jnp.zeros_like(acc)

        r = r_ref[...] if rhs_scale is None else rhs_scale * r_ref[...]
        acc[...] += _tn(l_ref[...].astype(BF16), r.astype(BF16))

        @pl.when(k == nk - 1)
        def _():
            o_ref[...] = acc[...].astype(out_dtype)

    return pl.pallas_call(
        body, name=name, grid=(m // bm, n // bn, nk),
        in_specs=[pl.BlockSpec((tk, bm), lambda i, j, k: (k, i)), pl.BlockSpec((tk, bn), lambda i, j, k: (k, j))],
        out_specs=pl.BlockSpec((bm, bn), lambda i, j, k: (i, j)),
        out_shape=jax.ShapeDtypeStruct((m, n), out_dtype),
        scratch_shapes=[pltpu.VMEM((bm, bn), F32)],
        compiler_params=_params(("arbitrary", "arbitrary", "arbitrary")),
    )(lhs, rhs)


def _inproj_fwd(x, g, w_t, splits, tm):
    t, d = x.shape
    offs = [sum(splits[:k]) for k in range(len(splits))]

    def body(x_ref, g_ref, w_ref, hn_ref, *z_refs):
        hn = _rms_fwd(x_ref[...], g_ref[...], d).astype(BF16)
        hn_ref[...] = hn
        for z_ref, o, n in zip(z_refs, offs, splits):
            z_ref[...] = _nt(hn, w_ref[o:o + n, :])

    row = pl.BlockSpec((tm, d), lambda i: (i, 0))
    return pl.pallas_call(
        body, name="inproj_fwd", grid=(t // tm,),
        in_specs=[row, pl.BlockSpec((1, d), lambda i: (0, 0)), WHOLE_VMEM],
        out_specs=[row] + [pl.BlockSpec((tm, n), lambda i: (i, 0)) for n in splits],
        out_shape=[jax.ShapeDtypeStruct((t, d), BF16)] + [jax.ShapeDtypeStruct((t, n), F32) for n in splits],
        compiler_params=_params(("arbitrary",)),
    )(x, g, w_t)


def _inproj_bwd(dzs, x, g, w_t, dres, splits, tm):
    t, d = x.shape
    offs = [sum(splits[:k]) for k in range(len(splits))]
    ni = sum(splits)
    nz = len(splits)

    def body(*refs):
        dz_refs = refs[:nz]
        x_ref, g_ref, w_ref, dres_ref, dx_ref, dg_ref, dzc_ref = refs[nz:]
        dhn = jnp.zeros((tm, d), F32)
        for dz_ref, o, n in zip(dz_refs, offs, splits):
            dz = dz_ref[...].astype(BF16)
            dzc_ref[:, o:o + n] = dz
            dhn += _nn(dz, w_ref[o:o + n, :])
        dx, dg = _rms_bwd(x_ref[...], g_ref[...], dhn, d)
        dx_ref[...] = dres_ref[...] + dx

        @pl.when(pl.program_id(0) == 0)
        def _():
            dg_ref[...] = jnp.zeros_like(dg_ref)

        dg_ref[...] += dg

    row = pl.BlockSpec((tm, d), lambda i: (i, 0))
    vec = pl.BlockSpec((1, d), lambda i: (0, 0))
    return pl.pallas_call(
        body, name="inproj_bwd", grid=(t // tm,),
        in_specs=[pl.BlockSpec((tm, n), lambda i: (i, 0)) for n in splits] + [row, vec, WHOLE_VMEM, row],
        out_specs=[row, vec, pl.BlockSpec((tm, ni), lambda i: (i, 0))],
        out_shape=[jax.ShapeDtypeStruct((t, d), F32), jax.ShapeDtypeStruct((1, d), F32),
                   jax.ShapeDtypeStruct((t, ni), BF16)],
        compiler_params=_params(("arbitrary",)),
    )(*dzs, x, g, w_t, dres)


def _tril_bf16(ws_ref, grp):
    rows = lax.broadcasted_iota(jnp.int32, (CHUNK, CHUNK), 0)
    cols = lax.broadcasted_iota(jnp.int32, (CHUNK, CHUNK), 1)
    return jnp.where(rows >= cols, ws_ref[grp], 0.0).astype(BF16)


def _gmlp_mix(zuv_ref, gv_ref, ws_ref, bias_ref, v_scr, mixed_scr, tm, w, groups):
    u = _gelu(zuv_ref[:, 0:w])
    v0 = _gelu(zuv_ref[:, w:2 * w])
    v_scr[...] = _rms_fwd(v0, gv_ref[...], w).astype(BF16)
    for grp in range(groups):
        wsm = _tril_bf16(ws_ref, grp)
        lanes = slice(grp * CHUNK, (grp + 1) * CHUNK)
        for c in range(tm // CHUNK):
            rows = slice(c * CHUNK, (c + 1) * CHUNK)
            mixed_scr[rows, lanes] = _nn(wsm, v_scr[rows, lanes]) + bias_ref[:, lanes]
    return u, v0


def _gmlp_fwd(zuv, gv, ws, bias, gout, tm):
    t, w2 = zuv.shape
    w = w2 // 2
    groups = ws.shape[0]

    def body(zuv_ref, gv_ref, ws_ref, bias_ref, go_ref, y_ref, v_scr, mixed_scr):
        u, _ = _gmlp_mix(zuv_ref, gv_ref, ws_ref, bias_ref, v_scr, mixed_scr, tm, w, groups)
        ya = u * mixed_scr[...]
        for grp in range(groups):
            lanes = slice(grp * CHUNK, (grp + 1) * CHUNK)
            y_ref[:, lanes] = _rms_fwd(ya[:, lanes], go_ref[:, lanes], CHUNK).astype(BF16)

    const2 = lambda i: (0, 0)
    return pl.pallas_call(
        body, name="gmlp_fwd", grid=(t // tm,),
        in_specs=[pl.BlockSpec((tm, w2), lambda i: (i, 0)), pl.BlockSpec((1, w), const2),
                  pl.BlockSpec((groups, CHUNK, CHUNK), lambda i: (0, 0, 0)),
                  pl.BlockSpec((CHUNK, w), const2), pl.BlockSpec((1, w), const2)],
        out_specs=pl.BlockSpec((tm, w), lambda i: (i, 0)),
        out_shape=jax.ShapeDtypeStruct((t, w), BF16),
        scratch_shapes=[pltpu.VMEM((tm, w), BF16), pltpu.VMEM((tm, w), F32)],
        compiler_params=_params(("arbitrary",)),
    )(zuv, gv, ws, bias, gout)


def _gmlp_bwd(dy, zuv, gv, ws, bias, gout, tm):
    t, w2 = zuv.shape
    w = w2 // 2
    groups = ws.shape[0]

    def body(dy_ref, zuv_ref, gv_ref, ws_ref, bias_ref, go_ref,
             dz_ref, dws_ref, dbias_ref, dgv_ref, dgo_ref, v_scr, mixed_scr, dmix_scr, dv_scr):
        @pl.when(pl.program_id(0) == 0)
        def _():
            dws_ref[...] = jnp.zeros_like(dws_ref)
            dbias_ref[...] = jnp.zeros_like(dbias_ref)
            dgv_ref[...] = jnp.zeros_like(dgv_ref)
            dgo_ref[...] = jnp.zeros_like(dgo_ref)

        u, v0 = _gmlp_mix(zuv_ref, gv_ref, ws_ref, bias_ref, v_scr, mixed_scr, tm, w, groups)
        mixed = mixed_scr[...]
        ya = u * mixed
        for grp in range(groups):
            lanes = slice(grp * CHUNK, (grp + 1) * CHUNK)
            dya, dgo = _rms_bwd(ya[:, lanes], go_ref[:, lanes], dy_ref[:, lanes], CHUNK)
            dgo_ref[:, lanes] += dgo
            dz_ref[:, lanes] = dya * mixed[:, lanes] * _gelu_grad(zuv_ref[:, lanes])
            dmix_scr[:, lanes] = dya * u[:, lanes]
        for grp in range(groups):
            wsm = _tril_bf16(ws_ref, grp)
            lanes = slice(grp * CHUNK, (grp + 1) * CHUNK)
            dws = jnp.zeros((CHUNK, CHUNK), F32)
            dbias = jnp.zeros((CHUNK, CHUNK), F32)
            for c in range(tm // CHUNK):
                rows = slice(c * CHUNK, (c + 1) * CHUNK)
                dm = dmix_scr[rows, lanes]
                dmb = dm.astype(BF16)
                dv_scr[rows, lanes] = _tn(wsm, dmb)
                dws += _nt(dmb, v_scr[rows, lanes])
                dbias += dm
            rr = lax.broadcasted_iota(jnp.int32, (CHUNK, CHUNK), 0)
            cc = lax.broadcasted_iota(jnp.int32, (CHUNK, CHUNK), 1)
            dws_ref[grp] += jnp.where(rr >= cc, dws, 0.0)
            dbias_ref[grp] += jnp.sum(dbias, axis=1, keepdims=True)
        dv0, dgv = _rms_bwd(v0, gv_ref[...], dv_scr[...], w)
        dgv_ref[...] += dgv
        dz_ref[:, w:2 * w] = dv0 * _gelu_grad(zuv_ref[:, w:2 * w])

    const2 = lambda i: (0, 0)
    const3 = lambda i: (0, 0, 0)
    return pl.pallas_call(
        body, name="gmlp_bwd", grid=(t // tm,),
        in_specs=[pl.BlockSpec((tm, w), lambda i: (i, 0)), pl.BlockSpec((tm, w2), lambda i: (i, 0)),
                  pl.BlockSpec((1, w), const2), pl.BlockSpec((groups, CHUNK, CHUNK), const3),
                  pl.BlockSpec((CHUNK, w), const2), pl.BlockSpec((1, w), const2)],
        out_specs=[pl.BlockSpec((tm, w2), lambda i: (i, 0)), pl.BlockSpec((groups, CHUNK, CHUNK), const3),
                   pl.BlockSpec((groups, CHUNK, 1), const3), pl.BlockSpec((1, w), const2), pl.BlockSpec((1, w), const2)],
        out_shape=[jax.ShapeDtypeStruct((t, w2), F32), jax.ShapeDtypeStruct((groups, CHUNK, CHUNK), F32),
                   jax.ShapeDtypeStruct((groups, CHUNK, 1), F32), jax.ShapeDtypeStruct((1, w), F32),
                   jax.ShapeDtypeStruct((1, w), F32)],
        scratch_shapes=[pltpu.VMEM((tm, w), BF16), pltpu.VMEM((tm, w), F32),
                        pltpu.VMEM((tm, w), F32), pltpu.VMEM((tm, w), F32)],
        compiler_params=_params(("arbitrary",)),
    )(dy, zuv, gv, ws, bias, gout)


def _rot(x, m_lo, m_hi):
    return pltpu.roll(x, LANE - ROPE // 2, 1) * m_lo + pltpu.roll(x, ROPE // 2, 1) * m_hi


def _rope_tables(pos_ref, freq_ref):
    ang = pos_ref[...] * freq_ref[...]
    return jnp.cos(ang), jnp.sin(ang)


def _mla_proj_fwd(cq, ckv, krw, pos, freq, masks, gq, gkv, wq_t, wkv_t, gqh, gkh, tm):
    t, rq = cq.shape
    rkv = ckv.shape[1]
    heads = wq_t.shape[0]

    def body(cq_ref, ckv_ref, kr_ref, pos_ref, freq_ref, mk_ref, gq_ref, gkv_ref, wq_ref, wkv_ref,
             gqh_ref, gkh_ref, q_ref, k_ref, v_ref):
        cos, sin = _rope_tables(pos_ref, freq_ref)
        m_lo, m_hi = mk_ref[0:1, :], mk_ref[1:2, :]
        cqn = _rms_fwd(cq_ref[...], gq_ref[...], rq).astype(BF16)
        ckvn = _rms_fwd(ckv_ref[...], gkv_ref[...], rkv).astype(BF16)
        kr = kr_ref[...]
        kr_ss = jnp.sum(kr * kr, axis=-1, keepdims=True)
        for h in range(heads):
            qh = _nt(cqn, wq_ref[h])
            qn = qh * _rstd(qh, QK) * gqh_ref[...]
            qr = qn[:, LANE:]
            q_ref[h, :, 0:LANE] = qn[:, 0:LANE].astype(BF16)
            q_ref[h, :, LANE:] = (qr * cos + _rot(qr, m_lo, m_hi) * sin).astype(BF16)
            kvh = _nt(ckvn, wkv_ref[h])
            kn = kvh[:, 0:LANE]
            rk = lax.rsqrt((jnp.sum(kn * kn, axis=-1, keepdims=True) + kr_ss) * (1.0 / QK) + EPS)
            k_ref[h, :, 0:LANE] = (kn * rk * gkh_ref[:, 0:LANE]).astype(BF16)
            krn = kr * rk * gkh_ref[:, LANE:]
            k_ref[h, :, LANE:] = (krn * cos + _rot(krn, m_lo, m_hi) * sin).astype(BF16)
            v_ref[h] = kvh[:, LANE:].astype(BF16)

    c2 = lambda i: (0, 0)
    c3 = lambda i: (0, 0, 0)
    return pl.pallas_call(
        body, name="mla_proj_fwd", grid=(t // tm,),
        in_specs=[pl.BlockSpec((tm, rq), lambda i: (i, 0)), pl.BlockSpec((tm, rkv), lambda i: (i, 0)),
                  pl.BlockSpec((tm, LANE), lambda i: (i, 0)), pl.BlockSpec((tm, 1), lambda i: (i, 0)),
                  pl.BlockSpec((1, LANE), c2), pl.BlockSpec((2, LANE), c2),
                  pl.BlockSpec((1, rq), c2), pl.BlockSpec((1, rkv), c2),
                  pl.BlockSpec((heads, HEADW, rq), c3), pl.BlockSpec((heads, HEADW, rkv), c3),
                  pl.BlockSpec((1, HEADW), c2), pl.BlockSpec((1, HEADW), c2)],
        out_specs=[pl.BlockSpec((heads, tm, HEADW), lambda i: (0, i, 0)),
                   pl.BlockSpec((heads, tm, HEADW), lambda i: (0, i, 0)),
                   pl.BlockSpec((heads, tm, VHEAD), lambda i: (0, i, 0))],
        out_shape=[jax.ShapeDtypeStruct((heads, t, HEADW), BF16), jax.ShapeDtypeStruct((heads, t, HEADW), BF16),
                   jax.ShapeDtypeStruct((heads, t, VHEAD), BF16)],
        compiler_params=_params(("arbitrary",)),
    )(cq, ckv, krw, pos, freq, masks, gq, gkv, wq_t, wkv_t, gqh, gkh)


def _mla_proj_bwd(dq, dk, dv, cq, ckv, krw, pos, freq, masks, gq, gkv, wq_t, wkv_t, gqh, gkh, tm):
    t, rq = cq.shape
    rkv = ckv.shape[1]
    heads = wq_t.shape[0]

    def body(dq_ref, dk_ref, dv_ref, cq_ref, ckv_ref, kr_ref, pos_ref, freq_ref, mk_ref, gq_ref, gkv_ref,
             wq_ref, wkv_ref, gqh_ref, gkh_ref,
             dcq_ref, dckv_ref, dkr_ref, dwq_ref, dwkv_ref, dgq_ref, dgkv_ref, dgqh_ref, dgkh_ref):
        @pl.when(pl.program_id(0) == 0)
        def _():
            for r in (dwq_ref, dwkv_ref, dgq_ref, dgkv_ref, dgqh_ref, dgkh_ref):
                r[...] = jnp.zeros_like(r)

        cos, sin = _rope_tables(pos_ref, freq_ref)
        m_lo, m_hi = mk_ref[0:1, :], mk_ref[1:2, :]

        def unrope(dy):
            return dy * cos - _rot(dy * sin, m_lo, m_hi)

        cqn = _rms_fwd(cq_ref[...], gq_ref[...], rq).astype(BF16)
        ckvn = _rms_fwd(ckv_ref[...], gkv_ref[...], rkv).astype(BF16)
        kr = kr_ref[...]
        kr_ss = jnp.sum(kr * kr, axis=-1, keepdims=True)
        dcqn = jnp.zeros((tm, rq), F32)
        dckvn = jnp.zeros((tm, rkv), F32)
        dkr = jnp.zeros((tm, LANE), F32)
        for h in range(heads):
            qh = _nt(cqn, wq_ref[h])
            dqn = jnp.concatenate([dq_ref[h, :, 0:LANE], unrope(dq_ref[h, :, LANE:])], axis=1)
            dqh, dg = _rms_bwd(qh, gqh_ref[...], dqn, QK)
            dgqh_ref[...] += dg
            dqh = dqh.astype(BF16)
            dcqn += _nn(dqh, wq_ref[h])
            dwq_ref[h] += _tn(dqh, cqn)

            kvh = _nt(ckvn, wkv_ref[h])
            kn = kvh[:, 0:LANE]
            rk = lax.rsqrt((jnp.sum(kn * kn, axis=-1, keepdims=True) + kr_ss) * (1.0 / QK) + EPS)
            dkn_n = dk_ref[h, :, 0:LANE]
            dkr_n = unrope(dk_ref[h, :, LANE:])
            knh, krh = kn * rk, kr * rk
            dgkh_ref[:, 0:LANE] += jnp.sum(dkn_n * knh, axis=0, keepdims=True)
            dgkh_ref[:, LANE:] += jnp.sum(dkr_n * krh, axis=0, keepdims=True)
            dkn_g, dkr_g = dkn_n * gkh_ref[:, 0:LANE], dkr_n * gkh_ref[:, LANE:]
            proj = (jnp.sum(dkn_g * knh, axis=-1, keepdims=True)
                    + jnp.sum(dkr_g * krh, axis=-1, keepdims=True)) * (1.0 / QK)
            dkr += rk * (dkr_g - krh * proj)
            dkvh = jnp.concatenate([rk * (dkn_g - knh * proj), dv_ref[h]], axis=1).astype(BF16)
            dckvn += _nn(dkvh, wkv_ref[h])
            dwkv_ref[h] += _tn(dkvh, ckvn)
        dkr_ref[...] = dkr
        dcq, dg = _rms_bwd(cq_ref[...], gq_ref[...], dcqn, rq)
        dcq_ref[...] = dcq
        dgq_ref[...] += dg
        dckv, dg = _rms_bwd(ckv_ref[...], gkv_ref[...], dckvn, rkv)
        dckv_ref[...] = dckv
        dgkv_ref[...] += dg

    c2 = lambda i: (0, 0)
    c3 = lambda i: (0, 0, 0)
    hq = pl.BlockSpec((heads, tm, HEADW), lambda i: (0, i, 0))
    return pl.pallas_call(
        body, name="mla_proj_bwd", grid=(t // tm,),
        in_specs=[hq, hq, pl.BlockSpec((heads, tm, VHEAD), lambda i: (0, i, 0)),
                  pl.BlockSpec((tm, rq), lambda i: (i, 0)), pl.BlockSpec((tm, rkv), lambda i: (i, 0)),
                  pl.BlockSpec((tm, LANE), lambda i: (i, 0)), pl.BlockSpec((tm, 1), lambda i: (i, 0)),
                  pl.BlockSpec((1, LANE), c2), pl.BlockSpec((2, LANE), c2),
                  pl.BlockSpec((1, rq), c2), pl.BlockSpec((1, rkv), c2),
                  pl.BlockSpec((heads, HEADW, rq), c3), pl.BlockSpec((heads, HEADW, rkv), c3),
                  pl.BlockSpec((1, HEADW), c2), pl.BlockSpec((1, HEADW), c2)],
        out_specs=[pl.BlockSpec((tm, rq), lambda i: (i, 0)), pl.BlockSpec((tm, rkv), lambda i: (i, 0)),
                   pl.BlockSpec((tm, LANE), lambda i: (i, 0)),
                   pl.BlockSpec((heads, HEADW, rq), c3), pl.BlockSpec((heads, HEADW, rkv), c3),
                   pl.BlockSpec((1, rq), c2), pl.BlockSpec((1, rkv), c2),
                   pl.BlockSpec((1, HEADW), c2), pl.BlockSpec((1, HEADW), c2)],
        out_shape=[jax.ShapeDtypeStruct((t, rq), F32), jax.ShapeDtypeStruct((t, rkv), F32),
                   jax.ShapeDtypeStruct((t, LANE), F32),
                   jax.ShapeDtypeStruct((heads, HEADW, rq), F32), jax.ShapeDtypeStruct((heads, HEADW, rkv), F32),
                   jax.ShapeDtypeStruct((1, rq), F32), jax.ShapeDtypeStruct((1, rkv), F32),
                   jax.ShapeDtypeStruct((1, HEADW), F32), jax.ShapeDtypeStruct((1, HEADW), F32)],
        compiler_params=_params(("arbitrary",)),
    )(dq, dk, dv, cq, ckv, krw, pos, freq, masks, gq, gkv, wq_t, wkv_t, gqh, gkh)


def _lower_triangle(blk):
    return lax.broadcasted_iota(jnp.int32, (blk, blk), 0) >= lax.broadcasted_iota(jnp.int32, (blk, blk), 1)


def _attn_fwd(q, k, v, seq, blk):
    heads, t, _ = q.shape
    scale = QK ** -0.5
    nblk = seq // blk

    def body(q_ref, k_ref, v_ref, o_ref, lse_ref):
        tri = _lower_triangle(blk)
        for qi in range(nblk):
            rows = slice(qi * blk, (qi + 1) * blk)
            before = slice(0, qi * blk)
            qb = q_ref[0, rows, :]
            s_d = jnp.where(tri, _nt(qb, k_ref[0, rows, :]) * scale, -1e30)
            m = jnp.max(s_d, axis=-1, keepdims=True)
            if qi:
                s_b = _nt(qb, k_ref[0, before, :]) * scale
                m = jnp.maximum(m, jnp.max(s_b, axis=-1, keepdims=True))
                p_b = jnp.exp(s_b - m)
            p_d = jnp.exp(s_d - m)
            l = jnp.sum(p_d, axis=-1, keepdims=True)
            acc = _nn(p_d.astype(BF16), v_ref[0, rows, :])
            if qi:
                l += jnp.sum(p_b, axis=-1, keepdims=True)
                acc += _nn(p_b.astype(BF16), v_ref[0, before, :])
            o_ref[0, rows, :] = acc / l
            lse_ref[0, rows, :] = m + jnp.log(l)

    return pl.pallas_call(
        body, name="attn_fwd", grid=(heads, t // seq),
        in_specs=[pl.BlockSpec((1, seq, HEADW), lambda h, b: (h, b, 0)),
                  pl.BlockSpec((1, seq, HEADW), lambda h, b: (h, b, 0)),
                  pl.BlockSpec((1, seq, VHEAD), lambda h, b: (h, b, 0))],
        out_specs=[pl.BlockSpec((1, seq, VHEAD), lambda h, b: (h, b, 0)),
                   pl.BlockSpec((1, seq, 1), lambda h, b: (h, b, 0))],
        out_shape=[jax.ShapeDtypeStruct((heads, t, VHEAD), F32), jax.ShapeDtypeStruct((heads, t, 1), F32)],
        compiler_params=_params(("arbitrary", "arbitrary")),
    )(q, k, v)


def _attn_bwd(q, k, v, do, lse, delta, seq, blk):
    heads, t, _ = q.shape
    scale = QK ** -0.5
    nblk = seq // blk

    def body(q_ref, k_ref, v_ref, do_ref, lse_ref, dl_ref, dq_ref, dk_ref, dv_ref):
        tri = _lower_triangle(blk)
        dk_ref[...] = jnp.zeros_like(dk_ref)
        dv_ref[...] = jnp.zeros_like(dv_ref)
        for qi in range(nblk):
            rows = slice(qi * blk, (qi + 1) * blk)
            qb = q_ref[0, rows, :]
            dob = do_ref[0, rows, :]
            lse_b = lse_ref[0, rows, :]
            dl_b = dl_ref[0, rows, :]
            dq = jnp.zeros((blk, HEADW), F32)
            for keys, masked in ((slice(0, qi * blk), False), (rows, True)):
                if keys.stop == keys.start:
                    continue
                kb = k_ref[0, keys, :]
                p = jnp.exp(_nt(qb, kb) * scale - lse_b)
                if masked:
                    p = jnp.where(tri, p, 0.0)
                dp = _nt(dob, v_ref[0, keys, :])
                ds = (p * (dp - dl_b) * scale).astype(BF16)
                dv_ref[0, keys, :] += _tn(p.astype(BF16), dob)
                dk_ref[0, keys, :] += _tn(ds, qb)
                dq += _nn(ds, kb)
            dq_ref[0, rows, :] = dq

    hq = pl.BlockSpec((1, seq, HEADW), lambda h, b: (h, b, 0))
    hv = pl.BlockSpec((1, seq, VHEAD), lambda h, b: (h, b, 0))
    h1 = pl.BlockSpec((1, seq, 1), lambda h, b: (h, b, 0))
    return pl.pallas_call(
        body, name="attn_bwd", grid=(heads, t // seq),
        in_specs=[hq, hq, hv, hv, h1, h1],
        out_specs=[hq, hq, hv],
        out_shape=[jax.ShapeDtypeStruct((heads, t, HEADW), F32), jax.ShapeDtypeStruct((heads, t, HEADW), F32),
                   jax.ShapeDtypeStruct((heads, t, VHEAD), F32)],
        compiler_params=_params(("arbitrary", "arbitrary")),
    )(q, k, v, do, lse, delta)


def _out_fwd(ya, o, gb, w_out, x1, tm):
    t, w = ya.shape
    heads = o.shape[0]
    d = x1.shape[1]

    def body(ya_ref, o_ref, gb_ref, w_ref, x1_ref, x2_ref, yc_ref):
        yc_ref[:, 0:w] = ya_ref[...]
        for h in range(heads):
            lanes = slice(h * VHEAD, (h + 1) * VHEAD)
            yc_ref[:, w + h * VHEAD:w + (h + 1) * VHEAD] = _rms_fwd(o_ref[h], gb_ref[:, lanes], VHEAD).astype(BF16)
        x2_ref[...] = x1_ref[...] + _nn(yc_ref[...], w_ref[...])

    wy = w + heads * VHEAD
    row = pl.BlockSpec((tm, d), lambda i: (i, 0))
    return pl.pallas_call(
        body, name="out_fwd", grid=(t // tm,),
        in_specs=[pl.BlockSpec((tm, w), lambda i: (i, 0)), pl.BlockSpec((heads, tm, VHEAD), lambda i: (0, i, 0)),
                  pl.BlockSpec((1, heads * VHEAD), lambda i: (0, 0)), WHOLE_VMEM, row],
        out_specs=[row, pl.BlockSpec((tm, wy), lambda i: (i, 0))],
        out_shape=[jax.ShapeDtypeStruct((t, d), F32), jax.ShapeDtypeStruct((t, wy), BF16)],
        compiler_params=_params(("arbitrary",)),
    )(ya, o, gb, w_out, x1)


def _out_bwd(dx2, o, gb, w_out, w, tm):
    t, d = dx2.shape
    heads = o.shape[0]

    def body(dx_ref, o_ref, gb_ref, w_ref, dya_ref, do_ref, dl_ref, dgb_ref):
        @pl.when(pl.program_id(0) == 0)
        def _():
            dgb_ref[...] = jnp.zeros_like(dgb_ref)

        dyc = _nt(dx_ref[...].astype(BF16), w_ref[...])
        dya_ref[...] = dyc[:, 0:w]
        for h in range(heads):
            lanes = slice(h * VHEAD, (h + 1) * VHEAD)
            oh = o_ref[h]
            doh, dg = _rms_bwd(oh, gb_ref[:, lanes], dyc[:, w + h * VHEAD:w + (h + 1) * VHEAD], VHEAD)
            dgb_ref[:, lanes] += dg
            do_ref[h] = doh.astype(BF16)
            dl_ref[h] = jnp.sum(doh * oh, axis=-1, keepdims=True)

    ho = pl.BlockSpec((heads, tm, VHEAD), lambda i: (0, i, 0))
    vec = pl.BlockSpec((1, heads * VHEAD), lambda i: (0, 0))
    return pl.pallas_call(
        body, name="out_bwd", grid=(t // tm,),
        in_specs=[pl.BlockSpec((tm, d), lambda i: (i, 0)), ho, vec, WHOLE_VMEM],
        out_specs=[pl.BlockSpec((tm, w), lambda i: (i, 0)), ho, pl.BlockSpec((heads, tm, 1), lambda i: (0, i, 0)), vec],
        out_shape=[jax.ShapeDtypeStruct((t, w), F32), jax.ShapeDtypeStruct((heads, t, VHEAD), BF16),
                   jax.ShapeDtypeStruct((heads, t, 1), F32), jax.ShapeDtypeStruct((1, heads * VHEAD), F32)],
        compiler_params=_params(("arbitrary",)),
    )(dx2, o, gb, w_out)


def _loss_head(y, target, tm):
    t, d = y.shape

    def body(y_ref, t_ref, dy_ref, loss_ref):
        @pl.when(pl.program_id(0) == 0)
        def _():
            loss_ref[...] = jnp.zeros_like(loss_ref)

        err = y_ref[...] - t_ref[...]
        dy_ref[...] = err * (1.0 / d)
        part = jnp.sum(jnp.sum(err * err, axis=-1, keepdims=True) * (1.0 / d), axis=0, keepdims=True)
        loss_ref[...] += 0.5 * part

    row = pl.BlockSpec((tm, d), lambda i: (i, 0))
    return pl.pallas_call(
        body, name="loss_head", grid=(t // tm,),
        in_specs=[row, row], out_specs=[row, pl.BlockSpec((1, 1), lambda i: (0, 0))],
        out_shape=[jax.ShapeDtypeStruct((t, d), F32), jax.ShapeDtypeStruct((1, 1), F32)],
        compiler_params=_params(("arbitrary",)),
    )(y, target)


def _place():
    return lax.axis_index("x"), lax.axis_index("y"), lax.axis_index("c")


HBM = pl.BlockSpec(memory_space=pltpu.HBM)
SEM = pl.BlockSpec(memory_space=pltpu.SEMAPHORE)
DATAFLOW = pltpu.SideEffectType.DATAFLOW_SIDE_EFFECTING


def _plan_copies(plan, refs, send_sems, recv_sems):
    cps = []
    for i, (sb, sblk, db, dblk, dev) in enumerate(plan(*_place())):
        cps.append(pltpu.make_async_remote_copy(
            src_ref=refs[sb] if sblk is None else refs[sb].at[sblk], dst_ref=refs[db].at[dblk],
            send_sem=send_sems.at[i], recv_sem=recv_sems.at[i], device_id=dev, device_id_type=MESH))
    return cps


def _push_start(bufs, plan, ncopy, name, deps=()):
    nb = len(bufs)

    def body(*refs):
        outs = refs[nb + len(deps):]
        for cp in _plan_copies(plan, refs[:nb], outs[0], outs[1]):
            cp.start()
        outs[-1][...] = jnp.zeros_like(outs[-1])

    res = pl.pallas_call(
        body, name=name,
        out_shape=(pltpu.SemaphoreType.DMA((ncopy,)), pltpu.SemaphoreType.DMA((ncopy,)),
                   *[pltpu.HBM(b.shape, b.dtype) for b in bufs], jax.ShapeDtypeStruct((SUBLANE, LANE), F32)),
        in_specs=[HBM] * nb + [ANY] * len(deps),
        out_specs=(SEM, SEM, *[HBM] * nb, WHOLE_VMEM),
        input_output_aliases={i: 2 + i for i in range(nb)},
        compiler_params=pltpu.CompilerParams(has_side_effects=DATAFLOW),
    )(*[pltpu.with_memory_space_constraint(b, pltpu.HBM) for b in bufs], *deps)
    return res[0], res[1], list(res[2:2 + nb]), res[-1]


def _push_wait(send_sems, recv_sems, bufs, plan, after, name):
    nb = len(bufs)

    def body(*refs):
        for cp in _plan_copies(plan, refs[:nb], refs[nb], refs[nb + 1]):
            cp.wait_send()
            cp.wait_recv()

    res = pl.pallas_call(
        body, name=name,
        out_shape=[pltpu.HBM(b.shape, b.dtype) for b in bufs],
        in_specs=[HBM] * nb + [SEM, SEM, ANY], out_specs=[HBM] * nb,
        input_output_aliases={i: i for i in range(nb)},
        compiler_params=pltpu.CompilerParams(has_side_effects=DATAFLOW),
    )(*bufs, send_sems, recv_sems, after)
    return list(res)


def _other_chips(x, y):
    return ((1 - x, y), (x, 1 - y), (1 - x, 1 - y))


class _Exchange:
    def __init__(self, bufs, plan, ncopy, name, deps=()):
        self.plan, self.name = plan, name
        self.send, self.recv, self.bufs, self.token = _push_start(bufs, plan, ncopy, name + "_start", deps)

    def wait(self, after):
        return _push_wait(self.send, self.recv, self.bufs, self.plan, after, self.name + "_wait")


class _Chain:
    def __init__(self, bufs):
        self.bufs = list(bufs)

    def start(self, plan, ncopy, name, deps=()):
        send, recv, self.bufs, token = _push_start(self.bufs, plan, ncopy, name + "_start", deps)
        return (send, recv, plan, name), token

    def wait(self, pending, after):
        send, recv, plan, name = pending
        self.bufs = _push_wait(send, recv, self.bufs, plan, after, name + "_wait")


class _StagedGather:
    def __init__(self, shards, me, name):
        self.n = n = len(shards)
        self.name = name
        self.chain = _Chain(list(shards) + [
            lax.dynamic_update_slice(lax.empty((N_DEV,) + s.shape, s.dtype), s[None], (me, 0, 0)) for s in shards])
        self.pending = {}

        def to_sibling(blocks):
            return lambda x, y, c: [(n + a, b, n + a, b, (x, y, 1 - c)) for a in range(n) for b in blocks(x, y, c)]

        def nbr_blocks(x, y, c):
            return [4 * (1 - x) + 2 * y + c, 4 * x + 2 * (1 - y) + c]

        def diag(x, y, c):
            sx, sy = (1 - x) * (1 - c) + x * c, y * (1 - c) + (1 - y) * c
            tx, ty = x * (1 - c) + (1 - x) * c, (1 - y) * (1 - c) + y * c
            return [(n + a, 4 * sx + 2 * sy + c, n + a, 4 * sx + 2 * sy + c, (tx, ty, c)) for a in range(n)]

        self.plans = {
            "own": (lambda x, y, c: [(a, None, n + a, 4 * x + 2 * y + c, (x, y, 1 - c)) for a in range(n)], n),
            "nbr": (lambda x, y, c: [(a, None, n + a, 4 * x + 2 * y + c, dev) for a in range(n)
                                     for dev in ((1 - x, y, c), (x, 1 - y, c))], 2 * n),
            "diag": (diag, n),
            "nbr_d2d": (to_sibling(nbr_blocks), 2 * n),
            "own_nbr_d2d": (to_sibling(lambda x, y, c: [4 * x + 2 * y + c] + nbr_blocks(x, y, c)), 3 * n),
            "diag_d2d": (to_sibling(lambda x, y, c: [4 * (1 - x) + 2 * (1 - y) + c]), n),
        }

    def start(self, stage, deps=()):
        plan, ncopy = self.plans[stage]
        self.pending[stage], token = self.chain.start(plan, ncopy, self.name + "_" + stage, deps)
        return token

    def wait(self, stage, after):
        self.chain.wait(self.pending.pop(stage), after)

    def lands(self):
        return self.chain.bufs[self.n:]


def _gather_ici(shards, me, name, deps=()):
    n = len(shards)
    lands = [lax.dynamic_update_slice(lax.empty((N_DEV,) + s.shape, s.dtype), s[None], (me, 0, 0)) for s in shards]

    def plan(x, y, c):
        return [(a, None, n + a, 4 * x + 2 * y + c, (px, py, c)) for a in range(n) for px, py in _other_chips(x, y)]

    return _Exchange(list(shards) + lands, plan, 3 * n, name, deps)


def _gather_d2d(lands, name, deps=()):
    n = len(lands)

    def plan(x, y, c):
        blocks = [4 * x + 2 * y + c] + [4 * px + 2 * py + c for px, py in _other_chips(x, y)]
        return [(a, b, a, b, (x, y, 1 - c)) for a in range(n) for b in blocks]

    return _Exchange(list(lands), plan, 4 * n, name, deps)


def _reduce_d2d(grads, name, deps=()):
    n = len(grads)
    lands = [lax.empty((4,) + g.shape[1:], g.dtype) for g in grads]

    def plan(x, y, c):
        return [(a, 2 * s + (1 - c), n + a, s, (x, y, 1 - c)) for a in range(n) for s in range(4)]

    return _Exchange(list(grads) + lands, plan, 4 * n, name, deps)


def _reduce_ici(chip, name, deps=()):
    n = len(chip)
    lands = [lax.empty((3,) + g.shape[1:], g.dtype) for g in chip]

    def plan(x, y, c):
        return [(a, 2 * px + py, n + a, k, (px, py, c))
                for a in range(n) for k, (px, py) in enumerate(_other_chips(x, y))]

    return _Exchange(list(chip) + lands, plan, 3 * n, name, deps)


def _pair_add(full, got, core, name):
    _, r, cdim = full.shape
    br = _row_block(r, 512)

    def body(c_ref, f_ref, g_ref, o_ref):
        o_ref[...] = (f_ref[...].astype(F32) + g_ref[...].astype(F32)).astype(o_ref.dtype)

    return pl.pallas_call(
        body, name=name,
        grid_spec=pltpu.PrefetchScalarGridSpec(
            num_scalar_prefetch=1, grid=(4, r // br),
            in_specs=[pl.BlockSpec((1, br, cdim), lambda s, i, c_ref: (2 * s + c_ref[0], i, 0)),
                      pl.BlockSpec((1, br, cdim), lambda s, i, c_ref: (s, i, 0))],
            out_specs=pl.BlockSpec((1, br, cdim), lambda s, i, c_ref: (s, i, 0))),
        out_shape=jax.ShapeDtypeStruct((4, r, cdim), full.dtype),
        compiler_params=_params(("arbitrary", "arbitrary")),
    )(core, full, got)


def _sum_owned(chip, got, slot, rows, name, after):
    cdim = chip.shape[2]
    br = _row_block(rows, 512)

    def body(s_ref, c_ref, g_ref, _, o_ref):
        acc = c_ref[0].astype(F32)
        for k in range(3):
            acc = acc + g_ref[k].astype(F32)
        o_ref[...] = acc

    return pl.pallas_call(
        body, name=name,
        grid_spec=pltpu.PrefetchScalarGridSpec(
            num_scalar_prefetch=1, grid=(rows // br,),
            in_specs=[pl.BlockSpec((1, br, cdim), lambda i, s_ref: (s_ref[0], i, 0)),
                      pl.BlockSpec((3, br, cdim), lambda i, s_ref: (0, i, 0)), ANY],
            out_specs=pl.BlockSpec((br, cdim), lambda i, s_ref: (i, 0))),
        out_shape=jax.ShapeDtypeStruct((rows, cdim), F32),
        compiler_params=_params(("arbitrary",)),
    )(slot, chip, got, after)


def _sum_devices(stack):
    _, r, cdim = stack.shape

    def body(s_ref, o_ref):
        acc = s_ref[0]
        for k in range(1, N_DEV):
            acc = acc + s_ref[k]
        o_ref[...] = acc

    return pl.pallas_call(
        body, name="sum_devices", out_shape=jax.ShapeDtypeStruct((r, cdim), F32),
        compiler_params=_params(),
    )(stack)


def _adamw(w, g, m, v, name):
    r, cdim = w.shape
    br = _row_block(r, 256)

    def body(w_ref, g_ref, m_ref, v_ref, d_ref, nm_ref, nv_ref):
        g = g_ref[...]
        nm = ADAM_B1 * m_ref[...] + (1.0 - ADAM_B1) * g
        nv = ADAM_B2 * v_ref[...] + (1.0 - ADAM_B2) * (g * g)
        m_hat = nm / (1.0 - ADAM_B1 ** ADAM_STEP)
        v_hat = nv / (1.0 - ADAM_B2 ** ADAM_STEP)
        d_ref[...] = -ADAM_LR * (m_hat / (jnp.sqrt(v_hat) + ADAM_EPS) + ADAM_WD * w_ref[...])
        nm_ref[...] = nm
        nv_ref[...] = nv

    spec = pl.BlockSpec((br, cdim), lambda i: (i, 0))
    shape = jax.ShapeDtypeStruct((r, cdim), F32)
    return pl.pallas_call(
        body, name=name, grid=(r // br,), in_specs=[spec] * 4, out_specs=[spec] * 3,
        out_shape=[shape] * 3, compiler_params=_params(("arbitrary",)),
    )(w, g, m, v)


WEIGHTS = ("ffn1_norm_g", "ffn1_w_gate", "ffn1_w_up", "ffn1_w_down", "mix_norm_g", "w_in", "gmlp_v_norm_g",
           "gmlp_w_s", "gmlp_b_s", "mla_q_norm_g", "mla_w_q_up", "mla_kv_norm_g", "mla_w_kv_up", "mla_q_head_g",
           "mla_k_head_g", "gmlp_out_g", "mla_out_g", "w_out", "ffn2_norm_g", "ffn2_w_gate", "ffn2_w_up",
           "ffn2_w_down")
SHARDED = {"ffn1_w_gate": True, "ffn1_w_up": True, "ffn1_w_down": False, "w_in": True, "mla_w_q_up": True,
           "mla_w_kv_up": True, "w_out": False, "ffn2_w_gate": True, "ffn2_w_up": True, "ffn2_w_down": False}


def _col_block(m, target):
    best = LANE
    for cand in range(LANE, min(m, target) + 1, LANE):
        if m % cand == 0:
            best = cand
    return best


def _shard_rows(w, transposed, pad_to=None):
    rows = (w[0].T if transposed else w[0]).astype(BF16)
    if pad_to is not None and pad_to != rows.shape[0]:
        rows = jnp.pad(rows, ((0, pad_to - rows.shape[0]), (0, 0)))
    return rows


def _pack(parts):
    flat = []
    for p in parts:
        f = p.reshape(-1).astype(F32)
        flat.append(jnp.pad(f, (0, _round_up(f.size, LANE) - f.size)))
    flat = jnp.concatenate(flat)
    rows = _round_up(flat.size // LANE, SUBLANE)
    return jnp.pad(flat, (0, rows * LANE - flat.size)).reshape(rows, LANE)


def _unpack(packed, shapes):
    out, row = [], 0
    for shp in shapes:
        size = 1
        for s in shp:
            size *= s
        nrows = _round_up(size, LANE) // LANE
        out.append(packed[row:row + nrows].reshape(-1)[:size].reshape(shp))
        row += nrows
    return out


def kernel(x, positions, ffn1_norm_g, ffn1_w_gate, ffn1_w_up, ffn1_w_down, mix_norm_g, w_in, gmlp_v_norm_g, gmlp_w_s, gmlp_b_s, mla_q_norm_g, mla_w_q_up, mla_kv_norm_g, mla_w_kv_up, mla_q_head_g, mla_k_head_g, gmlp_out_g, mla_out_g, w_out, ffn2_norm_g, ffn2_w_gate, ffn2_w_up, ffn2_w_down, loss_target, m_ffn1_norm_g, m_ffn1_w_gate, m_ffn1_w_up, m_ffn1_w_down, m_mix_norm_g, m_w_in, m_gmlp_v_norm_g, m_gmlp_w_s, m_gmlp_b_s, m_mla_q_norm_g, m_mla_w_q_up, m_mla_kv_norm_g, m_mla_w_kv_up, m_mla_q_head_g, m_mla_k_head_g, m_gmlp_out_g, m_mla_out_g, m_w_out, m_ffn2_norm_g, m_ffn2_w_gate, m_ffn2_w_up, m_ffn2_w_down, v_ffn1_norm_g, v_ffn1_w_gate, v_ffn1_w_up, v_ffn1_w_down, v_mix_norm_g, v_w_in, v_gmlp_v_norm_g, v_gmlp_w_s, v_gmlp_b_s, v_mla_q_norm_g, v_mla_w_q_up, v_mla_kv_norm_g, v_mla_w_kv_up, v_mla_q_head_g, v_mla_k_head_g, v_gmlp_out_g, v_mla_out_g, v_w_out, v_ffn2_norm_g, v_ffn2_w_gate, v_ffn2_w_up, v_ffn2_w_down):
    wts = dict(zip(WEIGHTS, (ffn1_norm_g, ffn1_w_gate, ffn1_w_up, ffn1_w_down, mix_norm_g, w_in, gmlp_v_norm_g, gmlp_w_s, gmlp_b_s, mla_q_norm_g, mla_w_q_up, mla_kv_norm_g, mla_w_kv_up, mla_q_head_g, mla_k_head_g, gmlp_out_g, mla_out_g, w_out, ffn2_norm_g, ffn2_w_gate, ffn2_w_up, ffn2_w_down)))
    mom1 = dict(zip(WEIGHTS, (m_ffn1_norm_g, m_ffn1_w_gate, m_ffn1_w_up, m_ffn1_w_down, m_mix_norm_g, m_w_in, m_gmlp_v_norm_g, m_gmlp_w_s, m_gmlp_b_s, m_mla_q_norm_g, m_mla_w_q_up, m_mla_kv_norm_g, m_mla_w_kv_up, m_mla_q_head_g, m_mla_k_head_g, m_gmlp_out_g, m_mla_out_g, m_w_out, m_ffn2_norm_g, m_ffn2_w_gate, m_ffn2_w_up, m_ffn2_w_down)))
    mom2 = dict(zip(WEIGHTS, (v_ffn1_norm_g, v_ffn1_w_gate, v_ffn1_w_up, v_ffn1_w_down, v_mix_norm_g, v_w_in, v_gmlp_v_norm_g, v_gmlp_w_s, v_gmlp_b_s, v_mla_q_norm_g, v_mla_w_q_up, v_mla_kv_norm_g, v_mla_w_kv_up, v_mla_q_head_g, v_mla_k_head_g, v_gmlp_out_g, v_mla_out_g, v_w_out, v_ffn2_norm_g, v_ffn2_w_gate, v_ffn2_w_up, v_ffn2_w_down)))

    b_loc, seq, d = x.shape
    t = b_loc * seq
    ffs = ffn1_w_gate.shape[2]
    fp = _round_up(ffs, LANE)
    wg = gmlp_v_norm_g.shape[1]
    groups = gmlp_w_s.shape[1]
    rq, rkv = mla_q_norm_g.shape[1], mla_kv_norm_g.shape[1]
    heads = mla_out_g.shape[1]
    assert w_in.shape[2] * N_DEV == 2 * wg + rq + rkv + ROPE and mla_w_kv_up.shape[2] * N_DEV == heads * HEADW
    tm = min(512, t)
    tm_mix = min(256, t)
    blk = min(256, seq)

    xf = x.reshape(t, d)
    target = loss_target.reshape(t, d)
    pos = positions.reshape(t, 1).astype(F32)
    half = ROPE // 2
    inv_freq = 1.0 / (ROPE_THETA ** (jnp.arange(half, dtype=F32) / half))
    freq = jnp.concatenate([inv_freq, inv_freq, jnp.zeros((LANE - ROPE,), F32)])[None, :]
    lane = jnp.arange(LANE)
    masks = jnp.stack([jnp.where(lane < half, -1.0, 0.0),
                       jnp.where((lane >= half) & (lane < ROPE), 1.0, 0.0)]).astype(F32)
    gqh = jnp.pad(mla_q_head_g, ((0, 0), (0, HEADW - QK)))
    gkh = jnp.pad(mla_k_head_g, ((0, 0), (0, HEADW - QK)))
    bias = jnp.repeat(gmlp_b_s[0].T, CHUNK, axis=1)
    gouta = gmlp_out_g.reshape(1, wg)
    goutb = mla_out_g.reshape(1, heads * VHEAD)
    ws = gmlp_w_s[0]

    px, py, pc = _place()
    me = 4 * px + 2 * py + pc
    core = pc.astype(jnp.int32).reshape(1)
    slot = (2 * px + py).astype(jnp.int32).reshape(1)
    order = [n for n in WEIGHTS if n in SHARDED]
    group = {"ffn1": [n for n in order if n.startswith("ffn1")], "ffn2": [n for n in order if n.startswith("ffn2")],
             "mix": [n for n in order if not n.startswith("ffn")]}
    shard = {n: _shard_rows(wts[n], SHARDED[n], fp if n.startswith("ffn") else None) for n in order}

    def tied(arr, token):
        return arr + token[0, 0].astype(arr.dtype)

    xnb, ynb, dgn = 4 * (1 - px) + 2 * py, 4 * px + 2 * (1 - py), 4 * (1 - px) + 2 * (1 - py)
    ids_a = jnp.stack([me, 4 * px + 2 * py + (1 - pc)]).astype(jnp.int32)
    ids_b = jnp.stack([xnb, xnb + 1, ynb, ynb + 1]).astype(jnp.int32)
    ids_c = jnp.stack([dgn, dgn + 1]).astype(jnp.int32)
    g1 = _StagedGather([shard[n] for n in group["ffn1"]], me, "gather_ffn1")
    g3 = _StagedGather([shard[n] for n in group["ffn2"]], me, "gather_ffn2")
    token = g1.start("own")
    token = g1.start("nbr", deps=(token,))
    g1.wait("own", token)
    x1, xn1, kept1 = _ffn_fwd(xf, None, ffn1_norm_g, ids_a, *g1.lands(), None, tm, "ffn1_fwd_a")
    g1.wait("nbr", x1)
    token = g1.start("diag")
    ici2 = _gather_ici([shard[n] for n in group["mix"]], me, "gather_mix_ici", deps=(token,))
    token = g3.start("nbr", deps=(ici2.token,))
    token = g1.start("nbr_d2d", deps=(token,))
    g1.wait("nbr_d2d", token)
    x1, xn1, kept1 = _ffn_fwd(x1, xn1, None, ids_b, *g1.lands(), kept1, tm, "ffn1_fwd_b")
    g1.wait("diag", x1)
    token = g1.start("diag_d2d")
    g1.wait("diag_d2d", token)
    full = dict(zip(group["ffn1"], g1.lands()))
    x1, xn1, (gd1, sl1, h1) = _ffn_fwd(x1, xn1, None, ids_c, full["ffn1_w_gate"], full["ffn1_w_up"],
                                       full["ffn1_w_down"], kept1, tm, "ffn1_fwd_c")
    d2d2 = _gather_d2d(ici2.wait(x1)[len(group["mix"]):], "gather_mix_d2d")
    full.update(zip(group["mix"], d2d2.wait(d2d2.token)))
    win_t = jnp.pad(full["w_in"].reshape(-1, d), ((0, LANE - ROPE), (0, 0)))
    splits = (2 * wg, rq, rkv, LANE)
    wq_t = jnp.pad(full["mla_w_q_up"].reshape(heads, QK, rq), ((0, 0), (0, HEADW - QK), (0, 0)))
    wkv_t = full["mla_w_kv_up"].reshape(heads, HEADW, rkv)
    wout = full["w_out"].reshape(-1, d)
    hn, zuv, cq, ckv, krw = _inproj_fwd(x1, mix_norm_g, win_t, splits, tm)
    ya = _gmlp_fwd(zuv, gmlp_v_norm_g, ws, bias, gouta, tm_mix)
    g3.wait("nbr", ya)
    token = g3.start("diag")
    token = g3.start("own_nbr_d2d", deps=(token,))
    q, k, vv = _mla_proj_fwd(cq, ckv, krw, pos, freq, masks, mla_q_norm_g, mla_kv_norm_g, wq_t, wkv_t,
                             tied(gqh, token), gkh, tm_mix)
    o, lse = _attn_fwd(q, k, vv, seq, blk)
    g3.wait("diag", o)
    token = g3.start("diag_d2d")
    x2, ycat = _out_fwd(ya, o, tied(goutb, token), wout, x1, tm)
    g3.wait("own_nbr_d2d", x2)
    g3.wait("diag_d2d", x2)
    full.update(zip(group["ffn2"], g3.lands()))
    x3, xn2, (gd2, sl2, h2) = _ffn_fwd(x2, None, ffn2_norm_g, jnp.arange(N_DEV, dtype=jnp.int32),
                                       full["ffn2_w_gate"], full["ffn2_w_up"], full["ffn2_w_down"], None, tm,
                                       "ffn2_fwd")
    dx3, loss_part = _loss_head(x3, target, tm)

    outs_g, outs_d, outs_m, outs_v = {}, {}, {}, {}

    def finish(names, chip, got, after):
        for n, cp, gt in zip(names, chip, got):
            rows = wts[n].shape[2] if SHARDED[n] else wts[n].shape[1]
            g = _sum_owned(cp, gt, slot, rows, "sum_owned_" + n, after)
            g = g.T if SHARDED[n] else g
            dlt, nm, nv = _adamw(wts[n][0], g, mom1[n][0], mom2[n][0], "adamw_" + n)
            outs_g[n], outs_d[n], outs_m[n], outs_v[n] = g[None], dlt[None], nm[None], nv[None]

    def chip_sums(names, ex, after):
        res = ex.wait(after)
        return [_pair_add(f, gt, core, "pair_add_" + n) for n, f, gt in zip(names, res[:len(names)], res[len(names):])]

    tk = min(512, t)
    grads = {}
    small = {}
    dx2, small["ffn2_norm_g"], da2, db2 = _ffn_bwd(
        dx3, x2, ffn2_norm_g, gd2, sl2, full["ffn2_w_gate"], full["ffn2_w_up"], full["ffn2_w_down"], tm, "ffn2_bwd")
    grads["ffn2_w_gate"] = _matmul_tn(da2, xn2, fp, d, tk, BF16, "dw_ffn2_gate").reshape(N_DEV, fp, d)
    grads["ffn2_w_up"] = _matmul_tn(db2, xn2, fp, d, tk, BF16, "dw_ffn2_up").reshape(N_DEV, fp, d)
    grads["ffn2_w_down"] = _matmul_tn(h2, dx3, fp, d, tk, BF16, "dw_ffn2_down", rhs_scale=0.5).reshape(
        N_DEV, fp, d)
    red_a2 = _reduce_d2d([grads[n] for n in group["ffn2"]], "reduce_ffn2_d2d")
    dya, do, delta, small["mla_out_g"] = _out_bwd(dx2, o, tied(goutb, red_a2.token), wout, wg, tm)
    grads["w_out"] = _matmul_tn(ycat, dx2, _col_block(ycat.shape[1], 768), d, tk, BF16, "dw_out").reshape(
        N_DEV, -1, d)
    chip2 = chip_sums(group["ffn2"], red_a2, grads["w_out"])
    red_b2 = _reduce_ici(chip2, "reduce_ffn2_ici")
    dq, dk, dv = _attn_bwd(q, k, vv, do, lse, tied(delta, red_b2.token), seq, blk)
    (dcq, dckv, dkrw, dwq, dwkv, small["mla_q_norm_g"], small["mla_kv_norm_g"], dgqh, dgkh) = _mla_proj_bwd(
        dq, dk, dv, cq, ckv, krw, pos, freq, masks, mla_q_norm_g, mla_kv_norm_g, wq_t, wkv_t, gqh, gkh, tm_mix)
    small["mla_q_head_g"], small["mla_k_head_g"] = dgqh[:, :QK], dgkh[:, :QK]
    grads["mla_w_q_up"] = dwq[:, :QK].astype(BF16).reshape(N_DEV, -1, rq)
    grads["mla_w_kv_up"] = dwkv.astype(BF16).reshape(N_DEV, -1, rkv)
    dzuv, small["gmlp_w_s"], dbs, small["gmlp_v_norm_g"], small["gmlp_out_g"] = _gmlp_bwd(
        dya, zuv, gmlp_v_norm_g, ws, bias, gouta, tm_mix)
    small["gmlp_b_s"] = dbs[:, :, 0]
    dx1, small["mix_norm_g"], dzc = _inproj_bwd([dzuv, dcq, dckv, dkrw], x1, mix_norm_g, win_t, dx2, splits,
                                                tm_mix)
    dwin = _matmul_tn(dzc, hn, _col_block(dzc.shape[1], 768), d, tk, BF16, "dw_in")
    grads["w_in"] = dwin[:N_DEV * w_in.shape[2]].reshape(N_DEV, -1, d)
    res_b2 = red_b2.wait(grads["w_in"])
    red_am = _reduce_d2d([grads[n] for n in group["mix"]], "reduce_mix_d2d")
    dx0, small["ffn1_norm_g"], da1, db1 = _ffn_bwd(
        dx1, xf, tied(ffn1_norm_g, red_am.token), gd1, sl1, full["ffn1_w_gate"], full["ffn1_w_up"],
        full["ffn1_w_down"], tm, "ffn1_bwd")
    chipm = chip_sums(group["mix"], red_am, dx0)
    red_bm = _reduce_ici(chipm, "reduce_mix_ici")
    rep = [n for n in WEIGHTS if n not in SHARDED]
    small_ici = _gather_ici([_pack([small[n] for n in rep] + [loss_part])], me, "gather_small_ici",
                            deps=(red_bm.token,))
    operands = {"ffn1_w_gate": (da1, xn1, None), "ffn1_w_up": (db1, xn1, None), "ffn1_w_down": (h1, dx1, 0.5)}
    token, red_a, red_b, small_d2d = small_ici.token, None, [], None
    for n in group["ffn1"]:
        lhs, rhs, scale = operands[n]
        gr = _matmul_tn(lhs, rhs, fp, d, tk, BF16, "dw_" + n, rhs_scale=scale).reshape(N_DEV, fp, d)
        if red_a is not None:
            prev, ex = red_a
            red_b.append((prev, _reduce_ici(chip_sums([prev], ex, gr), "reduce_" + prev + "_ici")))
            token = red_b[-1][1].token
            if n == group["ffn1"][-1]:
                small_d2d = _gather_d2d(small_ici.wait(gr)[1:], "gather_small_d2d", deps=(token,))
                token = small_d2d.token
        red_a = (n, _reduce_d2d([gr], "reduce_" + n + "_d2d", deps=(token,)))
    prev, ex = red_a
    red_b.append((prev, _reduce_ici(chip_sums([prev], ex, ex.token), "reduce_" + prev + "_ici")))
    token = red_b[-1][1].token
    finish(group["ffn2"], res_b2[:3], res_b2[3:], token)
    res = red_bm.wait(outs_g[group["ffn2"][-1]])
    nm_ = len(group["mix"])
    finish(group["mix"], res[:nm_], res[nm_:], token)
    total = _sum_devices(small_d2d.wait(outs_g[group["mix"][-1]])[0])
    zero = jnp.zeros((1,), F32)
    dlt, nm, nv = _adamw(_pack([wts[n] for n in rep] + [zero]), total, _pack([mom1[n] for n in rep] + [zero]),
                         _pack([mom2[n] for n in rep] + [zero]), "adamw_small")
    shapes = [wts[n].shape for n in rep] + [(1,)]
    for n, g, dl, m1, m2 in zip(rep, _unpack(total, shapes), _unpack(dlt, shapes), _unpack(nm, shapes),
                                _unpack(nv, shapes)):
        outs_g[n], outs_d[n], outs_m[n], outs_v[n] = g, dl, m1, m2
    loss = _unpack(total, shapes)[-1].reshape(())
    for n, ex in red_b:
        res = ex.wait(dlt)
        finish([n], res[:1], res[1:], token)

    return (loss, dx0.reshape(b_loc, seq, d), *[outs_g[n] for n in WEIGHTS], *[outs_d[n] for n in WEIGHTS],
            *[outs_m[n] for n in WEIGHTS], *[outs_v[n] for n in WEIGHTS])
```

```python
import functools

import jax
import jax.numpy as jnp
from jax import lax
from jax.experimental import pallas as pl
from jax.experimental.pallas import tpu as pltpu

F32 = jnp.float32
BF16 = jnp.bfloat16
EPS = 1e-6
LANE = 128
SUBLANE = 8
N_DEV = 8
VMEM_LIMIT = 60 * 1024 * 1024
NOPE = 128
ROPE = 64
VHEAD = 128
QK = NOPE + ROPE
HEADW = 2 * LANE
CHUNK = 128
ROPE_THETA = 10000.0
ADAM_LR, ADAM_B1, ADAM_B2, ADAM_EPS, ADAM_WD, ADAM_STEP = 0.001, 0.9, 0.999, 1e-08, 0.01, 10
MESH = pl.DeviceIdType.MESH
ANY = pl.BlockSpec(memory_space=pl.ANY)
WHOLE_VMEM = pl.BlockSpec(memory_space=pltpu.VMEM)


def _params(sem=None):
    return pltpu.CompilerParams(dimension_semantics=sem, vmem_limit_bytes=VMEM_LIMIT)


def _round_up(n, m):
    return -(-n // m) * m


def _row_block(rows, target):
    best = rows
    for cand in range(SUBLANE, min(rows, target) + 1, SUBLANE):
        if rows % cand == 0:
            best = cand
    return best if best <= target else rows


def _nn(a, b):
    return jnp.dot(a, b, preferred_element_type=F32)


def _nt(a, b):
    return lax.dot_general(a, b, (((1,), (1,)), ((), ())), preferred_element_type=F32)


def _tn(a, b):
    return lax.dot_general(a, b, (((0,), (0,)), ((), ())), preferred_element_type=F32)


def _rstd(x, n):
    return lax.rsqrt(jnp.sum(x * x, axis=-1, keepdims=True) * (1.0 / n) + EPS)


def _rms_fwd(x, g, n):
    return x * _rstd(x, n) * g


def _rms_bwd(x, g, dy, n):
    r = _rstd(x, n)
    xh = x * r
    dyg = dy * g
    dx = r * (dyg - xh * (jnp.sum(dyg * xh, axis=-1, keepdims=True) * (1.0 / n)))
    return dx, jnp.sum(dy * xh, axis=0, keepdims=True)


def _gelu(x):
    return 0.5 * x * (1.0 + lax.erf(x * 0.7071067811865476))


def _gelu_grad(x):
    return 0.5 * (1.0 + lax.erf(x * 0.7071067811865476)) + x * jnp.exp(-0.5 * x * x) * 0.3989422804014327


def _ffn_fwd(base, xn, g, ids, wg_t, wu_t, wd, saved, tm, name):
    t, d = base.shape
    nb, fp, _ = wg_t.shape
    n = ids.shape[0]
    first = xn is None
    if saved is None:
        saved = [lax.empty((t, nb * fp), BF16) for _ in range(3)]

    def body(ids_ref, *refs):
        if first:
            base_ref, g_ref, wg_ref, wu_ref, wd_ref, _, _, _, out_ref, xn_ref, gd_ref, sl_ref, h_ref, acc = refs
        else:
            base_ref, xn_ref, wg_ref, wu_ref, wd_ref, _, _, _, out_ref, gd_ref, sl_ref, h_ref, acc = refs
        j = pl.program_id(1)

        @pl.when(j == 0)
        def _():
            if first:
                xn_ref[...] = _rms_fwd(base_ref[...], g_ref[...], d).astype(BF16)
            acc[...] = jnp.zeros_like(acc)

        xnb = xn_ref[...]
        a = _nt(xnb, wg_ref[0])
        b = _nt(xnb, wu_ref[0])
        s = jax.nn.sigmoid(a)
        sl = a * s
        h = (sl * b).astype(BF16)
        gd_ref[...] = (b * (s * (1.0 + a * (1.0 - s)))).astype(BF16)
        sl_ref[...] = sl.astype(BF16)
        h_ref[...] = h
        acc[...] += _nn(h, wd_ref[0])

        @pl.when(j == n - 1)
        def _():
            out_ref[...] = base_ref[...] + 0.5 * acc[...]

    wspec = pl.BlockSpec((1, fp, d), lambda i, j, ids_ref: (ids_ref[j], 0, 0))
    row = pl.BlockSpec((tm, d), lambda i, j, ids_ref: (i, 0))
    ff = pl.BlockSpec((tm, fp), lambda i, j, ids_ref: (i, ids_ref[j]))
    ffs = jax.ShapeDtypeStruct((t, nb * fp), BF16)
    second = pl.BlockSpec((1, d), lambda i, j, ids_ref: (0, 0)) if first else row
    n_row_outs = 2 if first else 1
    res = pl.pallas_call(
        body, name=name,
        grid_spec=pltpu.PrefetchScalarGridSpec(
            num_scalar_prefetch=1, grid=(t // tm, n),
            in_specs=[row, second, wspec, wspec, wspec, ANY, ANY, ANY],
            out_specs=[row] * n_row_outs + [ff, ff, ff],
            scratch_shapes=[pltpu.VMEM((tm, d), F32)]),
        out_shape=[jax.ShapeDtypeStruct((t, d), F32)] + ([jax.ShapeDtypeStruct((t, d), BF16)] if first else [])
        + [ffs, ffs, ffs],
        input_output_aliases={6 + k: n_row_outs + k for k in range(3)},
        compiler_params=_params(("arbitrary", "arbitrary")),
    )(ids, base, g if first else xn, wg_t, wu_t, wd, *saved)
    return (res[0], res[1] if first else xn, list(res[n_row_outs:]))


def _ffn_bwd(dout, x, g, gd, sl, wg_t, wu_t, wd, tm, name):
    t, d = x.shape
    nb, fp, _ = wg_t.shape

    def body(do_hbm, x_hbm, g_ref, gd_ref, sl_ref, wg_ref, wu_ref, wd_ref,
             dx_ref, dg_ref, da_ref, db_ref, acc, rowbuf, dy_scr, sem):
        i, j = pl.program_id(0), pl.program_id(1)
        rows = pl.ds(pl.multiple_of(i * tm, tm), tm)

        def fetch(src):
            cp = pltpu.make_async_copy(src.at[rows, :], rowbuf, sem)
            cp.start()
            cp.wait()

        @pl.when(j == 0)
        def _():
            fetch(do_hbm)
            dy_scr[...] = (0.5 * rowbuf[...]).astype(BF16)
            acc[...] = jnp.zeros_like(acc)

        @pl.when((i == 0) & (j == 0))
        def _():
            dg_ref[...] = jnp.zeros_like(dg_ref)

        dh = _nt(dy_scr[...], wd_ref[0])
        da = (dh * gd_ref[...].astype(F32)).astype(BF16)
        db = (dh * sl_ref[...].astype(F32)).astype(BF16)
        da_ref[...] = da
        db_ref[...] = db
        acc[...] += _nn(da, wg_ref[0]) + _nn(db, wu_ref[0])

        @pl.when(j == nb - 1)
        def _():
            fetch(x_hbm)
            dxn, dg = _rms_bwd(rowbuf[...], g_ref[...], acc[...], d)
            dg_ref[...] += dg
            dx_ref[...] = dxn
            fetch(do_hbm)
            dx_ref[...] += rowbuf[...]

    wspec = pl.BlockSpec((1, fp, d), lambda i, j: (j, 0, 0))
    row = pl.BlockSpec((tm, d), lambda i, j: (i, 0))
    vec = pl.BlockSpec((1, d), lambda i, j: (0, 0))
    ff = pl.BlockSpec((tm, fp), lambda i, j: (i, j))
    ffs = jax.ShapeDtypeStruct((t, nb * fp), BF16)
    return pl.pallas_call(
        body, name=name, grid=(t // tm, nb),
        in_specs=[ANY, ANY, vec, ff, ff, wspec, wspec, wspec],
        out_specs=[row, vec, ff, ff],
        out_shape=[jax.ShapeDtypeStruct((t, d), F32), jax.ShapeDtypeStruct((1, d), F32), ffs, ffs],
        scratch_shapes=[pltpu.VMEM((tm, d), F32), pltpu.VMEM((tm, d), F32), pltpu.VMEM((tm, d), BF16),
                        pltpu.SemaphoreType.DMA],
        compiler_params=_params(("arbitrary", "arbitrary")),
    )(dout, x, g, gd, sl, wg_t, wu_t, wd)


def _matmul_tn(lhs, rhs, bm, bn, tk, out_dtype, name, rhs_scale=None, deps=()):
    t, m = lhs.shape
    n = rhs.shape[1]
    nk = t // tk

    def body(l_ref, r_ref, *refs):
        o_ref, acc = refs[len(deps):]
        k = pl.program_id(2)

        @pl.when(k == 0)
        def _():
            acc[...] = jnp.zeros_like(acc)

        r = r_ref[...] if rhs_scale is None else rhs_scale * r_ref[...]
        acc[...] += _tn(l_ref[...].astype(BF16), r.astype(BF16))

        @pl.when(k == nk - 1)
        def _():
            o_ref[...] = acc[...].astype(out_dtype)

    return pl.pallas_call(
        body, name=name, grid=(m // bm, n // bn, nk),
        in_specs=[pl.BlockSpec((tk, bm), lambda i, j, k: (k, i)), pl.BlockSpec((tk, bn), lambda i, j, k: (k, j))]
        + [ANY] * len(deps),
        out_specs=pl.BlockSpec((bm, bn), lambda i, j, k: (i, j)),
        out_shape=jax.ShapeDtypeStruct((m, n), out_dtype),
        scratch_shapes=[pltpu.VMEM((bm, bn), F32)],
        compiler_params=_params(("arbitrary", "arbitrary", "arbitrary")),
    )(lhs, rhs, *deps)


def _inproj_fwd(x, g, w_t, splits, tm):
    t, d = x.shape
    offs = [sum(splits[:k]) for k in range(len(splits))]

    def body(x_ref, g_ref, w_ref, hn_ref, *z_refs):
        hn = _rms_fwd(x_ref[...], g_ref[...], d).astype(BF16)
        hn_ref[...] = hn
        for z_ref, o, n in zip(z_refs, offs, splits):
            z_ref[...] = _nt(hn, w_ref[o:o + n, :])

    row = pl.BlockSpec((tm, d), lambda i: (i, 0))
    return pl.pallas_call(
        body, name="inproj_fwd", grid=(t // tm,),
        in_specs=[row, pl.BlockSpec((1, d), lambda i: (0, 0)), WHOLE_VMEM],
        out_specs=[row] + [pl.BlockSpec((tm, n), lambda i: (i, 0)) for n in splits],
        out_shape=[jax.ShapeDtypeStruct((t, d), BF16)] + [jax.ShapeDtypeStruct((t, n), F32) for n in splits],
        compiler_params=_params(("arbitrary",)),
    )(x, g, w_t)


def _inproj_bwd(dzs, x, g, w_t, dres, splits, tm):
    t, d = x.shape
    offs = [sum(splits[:k]) for k in range(len(splits))]
    ni = sum(splits)
    nz = len(splits)

    def body(*refs):
        dz_refs = refs[:nz]
        x_ref, g_ref, w_ref, dres_ref, dx_ref, dg_ref, dzc_ref = refs[nz:]
        dhn = jnp.zeros((tm, d), F32)
        for dz_ref, o, n in zip(dz_refs, offs, splits):
            dz = dz_ref[...].astype(BF16)
            dzc_ref[:, o:o + n] = dz
            dhn += _nn(dz, w_ref[o:o + n, :])
        dx, dg = _rms_bwd(x_ref[...], g_ref[...], dhn, d)
        dx_ref[...] = dres_ref[...] + dx

        @pl.when(pl.program_id(0) == 0)
        def _():
            dg_ref[...] = jnp.zeros_like(dg_ref)

        dg_ref[...] += dg

    row = pl.BlockSpec((tm, d), lambda i: (i, 0))
    vec = pl.BlockSpec((1, d), lambda i: (0, 0))
    return pl.pallas_call(
        body, name="inproj_bwd", grid=(t // tm,),
        in_specs=[pl.BlockSpec((tm, n), lambda i: (i, 0)) for n in splits] + [row, vec, WHOLE_VMEM, row],
        out_specs=[row, vec, pl.BlockSpec((tm, ni), lambda i: (i, 0))],
        out_shape=[jax.ShapeDtypeStruct((t, d), F32), jax.ShapeDtypeStruct((1, d), F32),
                   jax.ShapeDtypeStruct((t, ni), BF16)],
        compiler_params=_params(("arbitrary",)),
    )(*dzs, x, g, w_t, dres)


def _tril_bf16(ws_ref, grp):
    rows = lax.broadcasted_iota(jnp.int32, (CHUNK, CHUNK), 0)
    cols = lax.broadcasted_iota(jnp.int32, (CHUNK, CHUNK), 1)
    return jnp.where(rows >= cols, ws_ref[grp], 0.0).astype(BF16)


def _gmlp_mix(zuv_ref, gv_ref, ws_ref, bias_ref, v_scr, mixed_scr, tm, w, groups):
    u = _gelu(zuv_ref[:, 0:w])
    v0 = _gelu(zuv_ref[:, w:2 * w])
    v_scr[...] = _rms_fwd(v0, gv_ref[...], w).astype(BF16)
    for grp in range(groups):
        wsm = _tril_bf16(ws_ref, grp)
        lanes = slice(grp * CHUNK, (grp + 1) * CHUNK)
        for c in range(tm // CHUNK):
            rows = slice(c * CHUNK, (c + 1) * CHUNK)
            mixed_scr[rows, lanes] = _nn(wsm, v_scr[rows, lanes]) + bias_ref[:, lanes]
    return u, v0


def _gmlp_fwd(zuv, gv, ws, bias, gout, tm):
    t, w2 = zuv.shape
    w = w2 // 2
    groups = ws.shape[0]

    def body(zuv_ref, gv_ref, ws_ref, bias_ref, go_ref, y_ref, v_scr, mixed_scr):
        u, _ = _gmlp_mix(zuv_ref, gv_ref, ws_ref, bias_ref, v_scr, mixed_scr, tm, w, groups)
        ya = u * mixed_scr[...]
        for grp in range(groups):
            lanes = slice(grp * CHUNK, (grp + 1) * CHUNK)
            y_ref[:, lanes] = _rms_fwd(ya[:, lanes], go_ref[:, lanes], CHUNK).astype(BF16)

    const2 = lambda i: (0, 0)
    return pl.pallas_call(
        body, name="gmlp_fwd", grid=(t // tm,),
        in_specs=[pl.BlockSpec((tm, w2), lambda i: (i, 0)), pl.BlockSpec((1, w), const2),
                  pl.BlockSpec((groups, CHUNK, CHUNK), lambda i: (0, 0, 0)),
                  pl.BlockSpec((CHUNK, w), const2), pl.BlockSpec((1, w), const2)],
        out_specs=pl.BlockSpec((tm, w), lambda i: (i, 0)),
        out_shape=jax.ShapeDtypeStruct((t, w), BF16),
        scratch_shapes=[pltpu.VMEM((tm, w), BF16), pltpu.VMEM((tm, w), F32)],
        compiler_params=_params(("arbitrary",)),
    )(zuv, gv, ws, bias, gout)


def _gmlp_bwd(dy, zuv, gv, ws, bias, gout, tm):
    t, w2 = zuv.shape
    w = w2 // 2
    groups = ws.shape[0]

    def body(dy_ref, zuv_ref, gv_ref, ws_ref, bias_ref, go_ref,
             dz_ref, dws_ref, dbias_ref, dgv_ref, dgo_ref, v_scr, mixed_scr, dmix_scr, dv_scr):
        @pl.when(pl.program_id(0) == 0)
        def _():
            dws_ref[...] = jnp.zeros_like(dws_ref)
            dbias_ref[...] = jnp.zeros_like(dbias_ref)
            dgv_ref[...] = jnp.zeros_like(dgv_ref)
            dgo_ref[...] = jnp.zeros_like(dgo_ref)

        u, v0 = _gmlp_mix(zuv_ref, gv_ref, ws_ref, bias_ref, v_scr, mixed_scr, tm, w, groups)
        mixed = mixed_scr[...]
        ya = u * mixed
        for grp in range(groups):
            lanes = slice(grp * CHUNK, (grp + 1) * CHUNK)
            dya, dgo = _rms_bwd(ya[:, lanes], go_ref[:, lanes], dy_ref[:, lanes], CHUNK)
            dgo_ref[:, lanes] += dgo
            dz_ref[:, lanes] = dya * mixed[:, lanes] * _gelu_grad(zuv_ref[:, lanes])
            dmix_scr[:, lanes] = dya * u[:, lanes]
        for grp in range(groups):
            wsm = _tril_bf16(ws_ref, grp)
            lanes = slice(grp * CHUNK, (grp + 1) * CHUNK)
            dws = jnp.zeros((CHUNK, CHUNK), F32)
            dbias = jnp.zeros((CHUNK, CHUNK), F32)
            for c in range(tm // CHUNK):
                rows = slice(c * CHUNK, (c + 1) * CHUNK)
                dm = dmix_scr[rows, lanes]
                dmb = dm.astype(BF16)
                dv_scr[rows, lanes] = _tn(wsm, dmb)
                dws += _nt(dmb, v_scr[rows, lanes])
                dbias += dm
            rr = lax.broadcasted_iota(jnp.int32, (CHUNK, CHUNK), 0)
            cc = lax.broadcasted_iota(jnp.int32, (CHUNK, CHUNK), 1)
            dws_ref[grp] += jnp.where(rr >= cc, dws, 0.0)
            dbias_ref[grp] += jnp.sum(dbias, axis=1, keepdims=True)
        dv0, dgv = _rms_bwd(v0, gv_ref[...], dv_scr[...], w)
        dgv_ref[...] += dgv
        dz_ref[:, w:2 * w] = dv0 * _gelu_grad(zuv_ref[:, w:2 * w])

    const2 = lambda i: (0, 0)
    const3 = lambda i: (0, 0, 0)
    return pl.pallas_call(
        body, name="gmlp_bwd", grid=(t // tm,),
        in_specs=[pl.BlockSpec((tm, w), lambda i: (i, 0)), pl.BlockSpec((tm, w2), lambda i: (i, 0)),
                  pl.BlockSpec((1, w), const2), pl.BlockSpec((groups, CHUNK, CHUNK), const3),
                  pl.BlockSpec((CHUNK, w), const2), pl.BlockSpec((1, w), const2)],
        out_specs=[pl.BlockSpec((tm, w2), lambda i: (i, 0)), pl.BlockSpec((groups, CHUNK, CHUNK), const3),
                   pl.BlockSpec((groups, CHUNK, 1), const3), pl.BlockSpec((1, w), const2), pl.BlockSpec((1, w), const2)],
        out_shape=[jax.ShapeDtypeStruct((t, w2), F32), jax.ShapeDtypeStruct((groups, CHUNK, CHUNK), F32),
                   jax.ShapeDtypeStruct((groups, CHUNK, 1), F32), jax.ShapeDtypeStruct((1, w), F32),
                   jax.ShapeDtypeStruct((1, w), F32)],
        scratch_shapes=[pltpu.VMEM((tm, w), BF16), pltpu.VMEM((tm, w), F32),
                        pltpu.VMEM((tm, w), F32), pltpu.VMEM((tm, w), F32)],
        compiler_params=_params(("arbitrary",)),
    )(dy, zuv, gv, ws, bias, gout)


def _rot(x, m_lo, m_hi):
    return pltpu.roll(x, LANE - ROPE // 2, 1) * m_lo + pltpu.roll(x, ROPE // 2, 1) * m_hi


def _rope_tables(pos_ref, freq_ref):
    ang = pos_ref[...] * freq_ref[...]
    return jnp.cos(ang), jnp.sin(ang)


def _mla_proj_fwd(cq, ckv, krw, pos, freq, masks, gq, gkv, wq_t, wkv_t, gqh, gkh, tm):
    t, rq = cq.shape
    rkv = ckv.shape[1]
    heads = wq_t.shape[0]

    def body(cq_ref, ckv_ref, kr_ref, pos_ref, freq_ref, mk_ref, gq_ref, gkv_ref, wq_ref, wkv_ref,
             gqh_ref, gkh_ref, q_ref, k_ref, v_ref):
        cos, sin = _rope_tables(pos_ref, freq_ref)
        m_lo, m_hi = mk_ref[0:1, :], mk_ref[1:2, :]
        cqn = _rms_fwd(cq_ref[...], gq_ref[...], rq).astype(BF16)
        ckvn = _rms_fwd(ckv_ref[...], gkv_ref[...], rkv).astype(BF16)
        kr = kr_ref[...]
        kr_ss = jnp.sum(kr * kr, axis=-1, keepdims=True)
        for h in range(heads):
            qh = _nt(cqn, wq_ref[h])
            qn = qh * _rstd(qh, QK) * gqh_ref[...]
            qr = qn[:, LANE:]
            q_ref[h, :, 0:LANE] = qn[:, 0:LANE].astype(BF16)
            q_ref[h, :, LANE:] = (qr * cos + _rot(qr, m_lo, m_hi) * sin).astype(BF16)
            kvh = _nt(ckvn, wkv_ref[h])
            kn = kvh[:, 0:LANE]
            rk = lax.rsqrt((jnp.sum(kn * kn, axis=-1, keepdims=True) + kr_ss) * (1.0 / QK) + EPS)
            k_ref[h, :, 0:LANE] = (kn * rk * gkh_ref[:, 0:LANE]).astype(BF16)
            krn = kr * rk * gkh_ref[:, LANE:]
            k_ref[h, :, LANE:] = (krn * cos + _rot(krn, m_lo, m_hi) * sin).astype(BF16)
            v_ref[h] = kvh[:, LANE:].astype(BF16)

    c2 = lambda i: (0, 0)
    c3 = lambda i: (0, 0, 0)
    return pl.pallas_call(
        body, name="mla_proj_fwd", grid=(t // tm,),
        in_specs=[pl.BlockSpec((tm, rq), lambda i: (i, 0)), pl.BlockSpec((tm, rkv), lambda i: (i, 0)),
                  pl.BlockSpec((tm, LANE), lambda i: (i, 0)), pl.BlockSpec((tm, 1), lambda i: (i, 0)),
                  pl.BlockSpec((1, LANE), c2), pl.BlockSpec((2, LANE), c2),
                  pl.BlockSpec((1, rq), c2), pl.BlockSpec((1, rkv), c2),
                  pl.BlockSpec((heads, HEADW, rq), c3), pl.BlockSpec((heads, HEADW, rkv), c3),
                  pl.BlockSpec((1, HEADW), c2), pl.BlockSpec((1, HEADW), c2)],
        out_specs=[pl.BlockSpec((heads, tm, HEADW), lambda i: (0, i, 0)),
                   pl.BlockSpec((heads, tm, HEADW), lambda i: (0, i, 0)),
                   pl.BlockSpec((heads, tm, VHEAD), lambda i: (0, i, 0))],
        out_shape=[jax.ShapeDtypeStruct((heads, t, HEADW), BF16), jax.ShapeDtypeStruct((heads, t, HEADW), BF16),
                   jax.ShapeDtypeStruct((heads, t, VHEAD), BF16)],
        compiler_params=_params(("arbitrary",)),
    )(cq, ckv, krw, pos, freq, masks, gq, gkv, wq_t, wkv_t, gqh, gkh)


def _mla_proj_bwd(dq, dk, dv, cq, ckv, krw, pos, freq, masks, gq, gkv, wq_t, wkv_t, gqh, gkh, tm):
    t, rq = cq.shape
    rkv = ckv.shape[1]
    heads = wq_t.shape[0]

    def body(dq_ref, dk_ref, dv_ref, cq_ref, ckv_ref, kr_ref, pos_ref, freq_ref, mk_ref, gq_ref, gkv_ref,
             wq_ref, wkv_ref, gqh_ref, gkh_ref,
             dcq_ref, dckv_ref, dkr_ref, dwq_ref, dwkv_ref, dgq_ref, dgkv_ref, dgqh_ref, dgkh_ref):
        @pl.when(pl.program_id(0) == 0)
        def _():
            for r in (dwq_ref, dwkv_ref, dgq_ref, dgkv_ref, dgqh_ref, dgkh_ref):
                r[...] = jnp.zeros_like(r)

        cos, sin = _rope_tables(pos_ref, freq_ref)
        m_lo, m_hi = mk_ref[0:1, :], mk_ref[1:2, :]

        def unrope(dy):
            return dy * cos - _rot(dy * sin, m_lo, m_hi)

        cqn = _rms_fwd(cq_ref[...], gq_ref[...], rq).astype(BF16)
        ckvn = _rms_fwd(ckv_ref[...], gkv_ref[...], rkv).astype(BF16)
        kr = kr_ref[...]
        kr_ss = jnp.sum(kr * kr, axis=-1, keepdims=True)
        dcqn = jnp.zeros((tm, rq), F32)
        dckvn = jnp.zeros((tm, rkv), F32)
        dkr = jnp.zeros((tm, LANE), F32)
        for h in range(heads):
            qh = _nt(cqn, wq_ref[h])
            dqn = jnp.concatenate([dq_ref[h, :, 0:LANE], unrope(dq_ref[h, :, LANE:])], axis=1)
            dqh, dg = _rms_bwd(qh, gqh_ref[...], dqn, QK)
            dgqh_ref[...] += dg
            dqh = dqh.astype(BF16)
            dcqn += _nn(dqh, wq_ref[h])
            dwq_ref[h] += _tn(dqh, cqn)

            kvh = _nt(ckvn, wkv_ref[h])
            kn = kvh[:, 0:LANE]
            rk = lax.rsqrt((jnp.sum(kn * kn, axis=-1, keepdims=True) + kr_ss) * (1.0 / QK) + EPS)
            dkn_n = dk_ref[h, :, 0:LANE]
            dkr_n = unrope(dk_ref[h, :, LANE:])
            knh, krh = kn * rk, kr * rk
            dgkh_ref[:, 0:LANE] += jnp.sum(dkn_n * knh, axis=0, keepdims=True)
            dgkh_ref[:, LANE:] += jnp.sum(dkr_n * krh, axis=0, keepdims=True)
            dkn_g, dkr_g = dkn_n * gkh_ref[:, 0:LANE], dkr_n * gkh_ref[:, LANE:]
            proj = (jnp.sum(dkn_g * knh, axis=-1, keepdims=True)
                    + jnp.sum(dkr_g * krh, axis=-1, keepdims=True)) * (1.0 / QK)
            dkr += rk * (dkr_g - krh * proj)
            dkvh = jnp.concatenate([rk * (dkn_g - knh * proj), dv_ref[h]], axis=1).astype(BF16)
            dckvn += _nn(dkvh, wkv_ref[h])
            dwkv_ref[h] += _tn(dkvh, ckvn)
        dkr_ref[...] = dkr
        dcq, dg = _rms_bwd(cq_ref[...], gq_ref[...], dcqn, rq)
        dcq_ref[...] = dcq
        dgq_ref[...] += dg
        dckv, dg = _rms_bwd(ckv_ref[...], gkv_ref[...], dckvn, rkv)
        dckv_ref[...] = dckv
        dgkv_ref[...] += dg

    c2 = lambda i: (0, 0)
    c3 = lambda i: (0, 0, 0)
    hq = pl.BlockSpec((heads, tm, HEADW), lambda i: (0, i, 0))
    return pl.pallas_call(
        body, name="mla_proj_bwd", grid=(t // tm,),
        in_specs=[hq, hq, pl.BlockSpec((heads, tm, VHEAD), lambda i: (0, i, 0)),
                  pl.BlockSpec((tm, rq), lambda i: (i, 0)), pl.BlockSpec((tm, rkv), lambda i: (i, 0)),
                  pl.BlockSpec((tm, LANE), lambda i: (i, 0)), pl.BlockSpec((tm, 1), lambda i: (i, 0)),
                  pl.BlockSpec((1, LANE), c2), pl.BlockSpec((2, LANE), c2),
                  pl.BlockSpec((1, rq), c2), pl.BlockSpec((1, rkv), c2),
                  pl.BlockSpec((heads, HEADW, rq), c3), pl.BlockSpec((heads, HEADW, rkv), c3),
                  pl.BlockSpec((1, HEADW), c2), pl.BlockSpec((1, HEADW), c2)],
        out_specs=[pl.BlockSpec((tm, rq), lambda i: (i, 0)), pl.BlockSpec((tm, rkv), lambda i: (i, 0)),
                   pl.BlockSpec((tm, LANE), lambda i: (i, 0)),
                   pl.BlockSpec((heads, HEADW, rq), c3), pl.BlockSpec((heads, HEADW, rkv), c3),
                   pl.BlockSpec((1, rq), c2), pl.BlockSpec((1, rkv), c2),
                   pl.BlockSpec((1, HEADW), c2), pl.BlockSpec((1, HEADW), c2)],
        out_shape=[jax.ShapeDtypeStruct((t, rq), F32), jax.ShapeDtypeStruct((t, rkv), F32),
                   jax.ShapeDtypeStruct((t, LANE), F32),
                   jax.ShapeDtypeStruct((heads, HEADW, rq), F32), jax.ShapeDtypeStruct((heads, HEADW, rkv), F32),
                   jax.ShapeDtypeStruct((1, rq), F32), jax.ShapeDtypeStruct((1, rkv), F32),
                   jax.ShapeDtypeStruct((1, HEADW), F32), jax.ShapeDtypeStruct((1, HEADW), F32)],
        compiler_params=_params(("arbitrary",)),
    )(dq, dk, dv, cq, ckv, krw, pos, freq, masks, gq, gkv, wq_t, wkv_t, gqh, gkh)


def _lower_triangle(blk):
    return lax.broadcasted_iota(jnp.int32, (blk, blk), 0) >= lax.broadcasted_iota(jnp.int32, (blk, blk), 1)


def _attn_fwd(q, k, v, seq, blk):
    heads, t, _ = q.shape
    scale = QK ** -0.5
    nblk = seq // blk

    def body(q_ref, k_ref, v_ref, o_ref, lse_ref):
        tri = _lower_triangle(blk)
        for qi in range(nblk):
            rows = slice(qi * blk, (qi + 1) * blk)
            before = slice(0, qi * blk)
            qb = q_ref[0, rows, :]
            s_d = jnp.where(tri, _nt(qb, k_ref[0, rows, :]) * scale, -1e30)
            m = jnp.max(s_d, axis=-1, keepdims=True)
            if qi:
                s_b = _nt(qb, k_ref[0, before, :]) * scale
                m = jnp.maximum(m, jnp.max(s_b, axis=-1, keepdims=True))
                p_b = jnp.exp(s_b - m)
            p_d = jnp.exp(s_d - m)
            l = jnp.sum(p_d, axis=-1, keepdims=True)
            acc = _nn(p_d.astype(BF16), v_ref[0, rows, :])
            if qi:
                l += jnp.sum(p_b, axis=-1, keepdims=True)
                acc += _nn(p_b.astype(BF16), v_ref[0, before, :])
            o_ref[0, rows, :] = acc / l
            lse_ref[0, rows, :] = m + jnp.log(l)

    return pl.pallas_call(
        body, name="attn_fwd", grid=(heads, t // seq),
        in_specs=[pl.BlockSpec((1, seq, HEADW), lambda h, b: (h, b, 0)),
                  pl.BlockSpec((1, seq, HEADW), lambda h, b: (h, b, 0)),
                  pl.BlockSpec((1, seq, VHEAD), lambda h, b: (h, b, 0))],
        out_specs=[pl.BlockSpec((1, seq, VHEAD), lambda h, b: (h, b, 0)),
                   pl.BlockSpec((1, seq, 1), lambda h, b: (h, b, 0))],
        out_shape=[jax.ShapeDtypeStruct((heads, t, VHEAD), F32), jax.ShapeDtypeStruct((heads, t, 1), F32)],
        compiler_params=_params(("arbitrary", "arbitrary")),
    )(q, k, v)


def _attn_bwd(q, k, v, do, lse, delta, seq, blk, after):
    heads, t, _ = q.shape
    scale = QK ** -0.5
    nblk = seq // blk

    def body(q_ref, k_ref, v_ref, do_ref, lse_ref, dl_ref, _, dq_ref, dk_ref, dv_ref):
        tri = _lower_triangle(blk)
        dk_ref[...] = jnp.zeros_like(dk_ref)
        dv_ref[...] = jnp.zeros_like(dv_ref)
        for qi in range(nblk):
            rows = slice(qi * blk, (qi + 1) * blk)
            qb = q_ref[0, rows, :]
            dob = do_ref[0, rows, :]
            lse_b = lse_ref[0, rows, :]
            dl_b = dl_ref[0, rows, :]
            dq = jnp.zeros((blk, HEADW), F32)
            for keys, masked in ((slice(0, qi * blk), False), (rows, True)):
                if keys.stop == keys.start:
                    continue
                kb = k_ref[0, keys, :]
                p = jnp.exp(_nt(qb, kb) * scale - lse_b)
                if masked:
                    p = jnp.where(tri, p, 0.0)
                dp = _nt(dob, v_ref[0, keys, :])
                ds = (p * (dp - dl_b) * scale).astype(BF16)
                dv_ref[0, keys, :] += _tn(p.astype(BF16), dob)
                dk_ref[0, keys, :] += _tn(ds, qb)
                dq += _nn(ds, kb)
            dq_ref[0, rows, :] = dq

    hq = pl.BlockSpec((1, seq, HEADW), lambda h, b: (h, b, 0))
    hv = pl.BlockSpec((1, seq, VHEAD), lambda h, b: (h, b, 0))
    h1 = pl.BlockSpec((1, seq, 1), lambda h, b: (h, b, 0))
    return pl.pallas_call(
        body, name="attn_bwd", grid=(heads, t // seq),
        in_specs=[hq, hq, hv, hv, h1, h1, ANY],
        out_specs=[hq, hq, hv],
        out_shape=[jax.ShapeDtypeStruct((heads, t, HEADW), F32), jax.ShapeDtypeStruct((heads, t, HEADW), F32),
                   jax.ShapeDtypeStruct((heads, t, VHEAD), F32)],
        compiler_params=_params(("arbitrary", "arbitrary")),
    )(q, k, v, do, lse, delta, after)


def _out_fwd(ya, o, gb, w_out, x1, tm):
    t, w = ya.shape
    heads = o.shape[0]
    d = x1.shape[1]

    def body(ya_ref, o_ref, gb_ref, w_ref, x1_ref, x2_ref, yc_ref):
        yc_ref[:, 0:w] = ya_ref[...]
        for h in range(heads):
            lanes = slice(h * VHEAD, (h + 1) * VHEAD)
            yc_ref[:, w + h * VHEAD:w + (h + 1) * VHEAD] = _rms_fwd(o_ref[h], gb_ref[:, lanes], VHEAD).astype(BF16)
        x2_ref[...] = x1_ref[...] + _nn(yc_ref[...], w_ref[...])

    wy = w + heads * VHEAD
    row = pl.BlockSpec((tm, d), lambda i: (i, 0))
    return pl.pallas_call(
        body, name="out_fwd", grid=(t // tm,),
        in_specs=[pl.BlockSpec((tm, w), lambda i: (i, 0)), pl.BlockSpec((heads, tm, VHEAD), lambda i: (0, i, 0)),
                  pl.BlockSpec((1, heads * VHEAD), lambda i: (0, 0)), WHOLE_VMEM, row],
        out_specs=[row, pl.BlockSpec((tm, wy), lambda i: (i, 0))],
        out_shape=[jax.ShapeDtypeStruct((t, d), F32), jax.ShapeDtypeStruct((t, wy), BF16)],
        compiler_params=_params(("arbitrary",)),
    )(ya, o, gb, w_out, x1)


def _out_bwd(dx2, o, gb, w_out, w, tm):
    t, d = dx2.shape
    heads = o.shape[0]

    def body(dx_ref, o_ref, gb_ref, w_ref, dya_ref, do_ref, dl_ref, dgb_ref):
        @pl.when(pl.program_id(0) == 0)
        def _():
            dgb_ref[...] = jnp.zeros_like(dgb_ref)

        dyc = _nt(dx_ref[...].astype(BF16), w_ref[...])
        dya_ref[...] = dyc[:, 0:w]
        for h in range(heads):
            lanes = slice(h * VHEAD, (h + 1) * VHEAD)
            oh = o_ref[h]
            doh, dg = _rms_bwd(oh, gb_ref[:, lanes], dyc[:, w + h * VHEAD:w + (h + 1) * VHEAD], VHEAD)
            dgb_ref[:, lanes] += dg
            do_ref[h] = doh.astype(BF16)
            dl_ref[h] = jnp.sum(doh * oh, axis=-1, keepdims=True)

    ho = pl.BlockSpec((heads, tm, VHEAD), lambda i: (0, i, 0))
    vec = pl.BlockSpec((1, heads * VHEAD), lambda i: (0, 0))
    return pl.pallas_call(
        body, name="out_bwd", grid=(t // tm,),
        in_specs=[pl.BlockSpec((tm, d), lambda i: (i, 0)), ho, vec, WHOLE_VMEM],
        out_specs=[pl.BlockSpec((tm, w), lambda i: (i, 0)), ho, pl.BlockSpec((heads, tm, 1), lambda i: (0, i, 0)), vec],
        out_shape=[jax.ShapeDtypeStruct((t, w), F32), jax.ShapeDtypeStruct((heads, t, VHEAD), BF16),
                   jax.ShapeDtypeStruct((heads, t, 1), F32), jax.ShapeDtypeStruct((1, heads * VHEAD), F32)],
        compiler_params=_params(("arbitrary",)),
    )(dx2, o, gb, w_out)


def _loss_head(y, target, tm):
    t, d = y.shape

    def body(y_ref, t_ref, dy_ref, loss_ref):
        @pl.when(pl.program_id(0) == 0)
        def _():
            loss_ref[...] = jnp.zeros_like(loss_ref)

        err = y_ref[...] - t_ref[...]
        dy_ref[...] = err * (1.0 / d)
        part = jnp.sum(jnp.sum(err * err, axis=-1, keepdims=True) * (1.0 / d), axis=0, keepdims=True)
        loss_ref[...] += 0.5 * part

    row = pl.BlockSpec((tm, d), lambda i: (i, 0))
    return pl.pallas_call(
        body, name="loss_head", grid=(t // tm,),
        in_specs=[row, row], out_specs=[row, pl.BlockSpec((1, 1), lambda i: (0, 0))],
        out_shape=[jax.ShapeDtypeStruct((t, d), F32), jax.ShapeDtypeStruct((1, 1), F32)],
        compiler_params=_params(("arbitrary",)),
    )(y, target)


def _place():
    return lax.axis_index("x"), lax.axis_index("y"), lax.axis_index("c")


HBM = pl.BlockSpec(memory_space=pltpu.HBM)
SEM = pl.BlockSpec(memory_space=pltpu.SEMAPHORE)
DATAFLOW = pltpu.SideEffectType.DATAFLOW_SIDE_EFFECTING


def _plan_copies(plan, refs, send_sems, recv_sems):
    cps = []
    for i, (sb, sblk, db, dblk, dev) in enumerate(plan(*_place())):
        cps.append(pltpu.make_async_remote_copy(
            src_ref=refs[sb] if sblk is None else refs[sb].at[sblk], dst_ref=refs[db].at[dblk],
            send_sem=send_sems.at[i], recv_sem=recv_sems.at[i], device_id=dev, device_id_type=MESH))
    return cps


def _push_start(bufs, plan, ncopy, name, deps=()):
    nb = len(bufs)

    def body(*refs):
        outs = refs[nb + len(deps):]
        for cp in _plan_copies(plan, refs[:nb], outs[0], outs[1]):
            cp.start()
        outs[-1][...] = jnp.zeros_like(outs[-1])

    res = pl.pallas_call(
        body, name=name,
        out_shape=(pltpu.SemaphoreType.DMA((ncopy,)), pltpu.SemaphoreType.DMA((ncopy,)),
                   *[pltpu.HBM(b.shape, b.dtype) for b in bufs], jax.ShapeDtypeStruct((SUBLANE, LANE), F32)),
        in_specs=[HBM] * nb + [ANY] * len(deps),
        out_specs=(SEM, SEM, *[HBM] * nb, WHOLE_VMEM),
        input_output_aliases={i: 2 + i for i in range(nb)},
        compiler_params=pltpu.CompilerParams(has_side_effects=DATAFLOW),
    )(*[pltpu.with_memory_space_constraint(b, pltpu.HBM) for b in bufs], *deps)
    return res[0], res[1], list(res[2:2 + nb]), res[-1]


def _push_wait(send_sems, recv_sems, bufs, plan, after, name):
    nb = len(bufs)

    def body(*refs):
        for cp in _plan_copies(plan, refs[:nb], refs[nb], refs[nb + 1]):
            cp.wait_send()
            cp.wait_recv()

    res = pl.pallas_call(
        body, name=name,
        out_shape=[pltpu.HBM(b.shape, b.dtype) for b in bufs],
        in_specs=[HBM] * nb + [SEM, SEM, ANY], out_specs=[HBM] * nb,
        input_output_aliases={i: i for i in range(nb)},
        compiler_params=pltpu.CompilerParams(has_side_effects=DATAFLOW),
    )(*bufs, send_sems, recv_sems, after)
    return list(res)


def _other_chips(x, y):
    return ((1 - x, y), (x, 1 - y), (1 - x, 1 - y))


class _Exchange:
    def __init__(self, bufs, plan, ncopy, name, deps=()):
        self.plan, self.name = plan, name
        self.send, self.recv, self.bufs, self.token = _push_start(bufs, plan, ncopy, name + "_start", deps)

    def wait(self, after):
        return _push_wait(self.send, self.recv, self.bufs, self.plan, after, self.name + "_wait")


class _Chain:
    def __init__(self, bufs):
        self.bufs = list(bufs)

    def start(self, plan, ncopy, name, deps=()):
        send, recv, self.bufs, token = _push_start(self.bufs, plan, ncopy, name + "_start", deps)
        return (send, recv, plan, name), token

    def wait(self, pending, after):
        send, recv, plan, name = pending
        self.bufs = _push_wait(send, recv, self.bufs, plan, after, name + "_wait")


class _StagedGather:
    def __init__(self, shards, me, name):
        self.n = n = len(shards)
        self.name = name
        self.chain = _Chain(list(shards) + [
            lax.dynamic_update_slice(lax.empty((N_DEV,) + s.shape, s.dtype), s[None], (me, 0, 0)) for s in shards])
        self.pending = {}

        def to_sibling(blocks):
            return lambda x, y, c: [(n + a, b, n + a, b, (x, y, 1 - c)) for a in range(n) for b in blocks(x, y, c)]

        def nbr_blocks(x, y, c):
            return [4 * (1 - x) + 2 * y + c, 4 * x + 2 * (1 - y) + c]

        def diag(x, y, c):
            sx, sy = (1 - x) * (1 - c) + x * c, y * (1 - c) + (1 - y) * c
            tx, ty = x * (1 - c) + (1 - x) * c, (1 - y) * (1 - c) + y * c
            return [(n + a, 4 * sx + 2 * sy + c, n + a, 4 * sx + 2 * sy + c, (tx, ty, c)) for a in range(n)]

        self.plans = {
            "own": (lambda x, y, c: [(a, None, n + a, 4 * x + 2 * y + c, (x, y, 1 - c)) for a in range(n)], n),
            "nbr": (lambda x, y, c: [(a, None, n + a, 4 * x + 2 * y + c, dev) for a in range(n)
                                     for dev in ((1 - x, y, c), (x, 1 - y, c))], 2 * n),
            "diag": (diag, n),
            "nbr_d2d": (to_sibling(nbr_blocks), 2 * n),
            "own_nbr_d2d": (to_sibling(lambda x, y, c: [4 * x + 2 * y + c] + nbr_blocks(x, y, c)), 3 * n),
            "diag_d2d": (to_sibling(lambda x, y, c: [4 * (1 - x) + 2 * (1 - y) + c]), n),
        }

    def start(self, stage, deps=()):
        plan, ncopy = self.plans[stage]
        self.pending[stage], token = self.chain.start(plan, ncopy, self.name + "_" + stage, deps)
        return token

    def wait(self, stage, after):
        self.chain.wait(self.pending.pop(stage), after)

    def lands(self):
        return self.chain.bufs[self.n:]


def _gather_ici(shards, me, name, deps=()):
    n = len(shards)
    lands = [lax.dynamic_update_slice(lax.empty((N_DEV,) + s.shape, s.dtype), s[None], (me, 0, 0)) for s in shards]

    def plan(x, y, c):
        return [(a, None, n + a, 4 * x + 2 * y + c, (px, py, c)) for a in range(n) for px, py in _other_chips(x, y)]

    return _Exchange(list(shards) + lands, plan, 3 * n, name, deps)


def _gather_d2d(lands, name, deps=()):
    n = len(lands)

    def plan(x, y, c):
        blocks = [4 * x + 2 * y + c] + [4 * px + 2 * py + c for px, py in _other_chips(x, y)]
        return [(a, b, a, b, (x, y, 1 - c)) for a in range(n) for b in blocks]

    return _Exchange(list(lands), plan, 4 * n, name, deps)


def _reduce_d2d(grads, name, deps=()):
    n = len(grads)
    lands = [lax.empty((4,) + g.shape[1:], g.dtype) for g in grads]

    def plan(x, y, c):
        return [(a, 2 * s + (1 - c), n + a, s, (x, y, 1 - c)) for a in range(n) for s in range(4)]

    return _Exchange(list(grads) + lands, plan, 4 * n, name, deps)


def _reduce_ici(chip, name, deps=()):
    n = len(chip)
    lands = [lax.empty((3,) + g.shape[1:], g.dtype) for g in chip]

    def plan(x, y, c):
        return [(a, 2 * px + py, n + a, k, (px, py, c))
                for a in range(n) for k, (px, py) in enumerate(_other_chips(x, y))]

    return _Exchange(list(chip) + lands, plan, 3 * n, name, deps)


def _pair_add(full, got, core, name):
    _, r, cdim = full.shape
    br = _row_block(r, 512)

    def body(c_ref, f_ref, g_ref, o_ref):
        o_ref[...] = (f_ref[...].astype(F32) + g_ref[...].astype(F32)).astype(o_ref.dtype)

    return pl.pallas_call(
        body, name=name,
        grid_spec=pltpu.PrefetchScalarGridSpec(
            num_scalar_prefetch=1, grid=(4, r // br),
            in_specs=[pl.BlockSpec((1, br, cdim), lambda s, i, c_ref: (2 * s + c_ref[0], i, 0)),
                      pl.BlockSpec((1, br, cdim), lambda s, i, c_ref: (s, i, 0))],
            out_specs=pl.BlockSpec((1, br, cdim), lambda s, i, c_ref: (s, i, 0))),
        out_shape=jax.ShapeDtypeStruct((4, r, cdim), full.dtype),
        compiler_params=_params(("arbitrary", "arbitrary")),
    )(core, full, got)


def _sum_owned(chip, got, slot, rows, name, after):
    cdim = chip.shape[2]
    br = _row_block(rows, 512)

    def body(s_ref, c_ref, g_ref, _, o_ref):
        acc = c_ref[0].astype(F32)
        for k in range(3):
            acc = acc + g_ref[k].astype(F32)
        o_ref[...] = acc

    return pl.pallas_call(
        body, name=name,
        grid_spec=pltpu.PrefetchScalarGridSpec(
            num_scalar_prefetch=1, grid=(rows // br,),
            in_specs=[pl.BlockSpec((1, br, cdim), lambda i, s_ref: (s_ref[0], i, 0)),
                      pl.BlockSpec((3, br, cdim), lambda i, s_ref: (0, i, 0)), ANY],
            out_specs=pl.BlockSpec((br, cdim), lambda i, s_ref: (i, 0))),
        out_shape=jax.ShapeDtypeStruct((rows, cdim), F32),
        compiler_params=_params(("arbitrary",)),
    )(slot, chip, got, after)


def _sum_devices(stack):
    _, r, cdim = stack.shape

    def body(s_ref, o_ref):
        acc = s_ref[0]
        for k in range(1, N_DEV):
            acc = acc + s_ref[k]
        o_ref[...] = acc

    return pl.pallas_call(
        body, name="sum_devices", out_shape=jax.ShapeDtypeStruct((r, cdim), F32),
        compiler_params=_params(),
    )(stack)


def _adamw(w, g, m, v, name):
    r, cdim = w.shape
    br = _row_block(r, 256)

    def body(w_ref, g_ref, m_ref, v_ref, d_ref, nm_ref, nv_ref):
        g = g_ref[...]
        nm = ADAM_B1 * m_ref[...] + (1.0 - ADAM_B1) * g
        nv = ADAM_B2 * v_ref[...] + (1.0 - ADAM_B2) * (g * g)
        m_hat = nm / (1.0 - ADAM_B1 ** ADAM_STEP)
        v_hat = nv / (1.0 - ADAM_B2 ** ADAM_STEP)
        d_ref[...] = -ADAM_LR * (m_hat / (jnp.sqrt(v_hat) + ADAM_EPS) + ADAM_WD * w_ref[...])
        nm_ref[...] = nm
        nv_ref[...] = nv

    spec = pl.BlockSpec((br, cdim), lambda i: (i, 0))
    shape = jax.ShapeDtypeStruct((r, cdim), F32)
    return pl.pallas_call(
        body, name=name, grid=(r // br,), in_specs=[spec] * 4, out_specs=[spec] * 3,
        out_shape=[shape] * 3, compiler_params=_params(("arbitrary",)),
    )(w, g, m, v)


WEIGHTS = ("ffn1_norm_g", "ffn1_w_gate", "ffn1_w_up", "ffn1_w_down", "mix_norm_g", "w_in", "gmlp_v_norm_g",
           "gmlp_w_s", "gmlp_b_s", "mla_q_norm_g", "mla_w_q_up", "mla_kv_norm_g", "mla_w_kv_up", "mla_q_head_g",
           "mla_k_head_g", "gmlp_out_g", "mla_out_g", "w_out", "ffn2_norm_g", "ffn2_w_gate", "ffn2_w_up",
           "ffn2_w_down")
SHARDED = {"ffn1_w_gate": True, "ffn1_w_up": True, "ffn1_w_down": False, "w_in": True, "mla_w_q_up": True,
           "mla_w_kv_up": True, "w_out": False, "ffn2_w_gate": True, "ffn2_w_up": True, "ffn2_w_down": False}


def _col_block(m, target):
    best = LANE
    for cand in range(LANE, min(m, target) + 1, LANE):
        if m % cand == 0:
            best = cand
    return best


def _shard_rows(w, transposed, pad_to=None):
    rows = (w[0].T if transposed else w[0]).astype(BF16)
    if pad_to is not None and pad_to != rows.shape[0]:
        rows = jnp.pad(rows, ((0, pad_to - rows.shape[0]), (0, 0)))
    return rows


def _pack(parts):
    flat = []
    for p in parts:
        f = p.reshape(-1).astype(F32)
        flat.append(jnp.pad(f, (0, _round_up(f.size, LANE) - f.size)))
    flat = jnp.concatenate(flat)
    rows = _round_up(flat.size // LANE, SUBLANE)
    return jnp.pad(flat, (0, rows * LANE - flat.size)).reshape(rows, LANE)


def _unpack(packed, shapes):
    out, row = [], 0
    for shp in shapes:
        size = 1
        for s in shp:
            size *= s
        nrows = _round_up(size, LANE) // LANE
        out.append(packed[row:row + nrows].reshape(-1)[:size].reshape(shp))
        row += nrows
    return out


def kernel(x, positions, ffn1_norm_g, ffn1_w_gate, ffn1_w_up, ffn1_w_down, mix_norm_g, w_in, gmlp_v_norm_g, gmlp_w_s, gmlp_b_s, mla_q_norm_g, mla_w_q_up, mla_kv_norm_g, mla_w_kv_up, mla_q_head_g, mla_k_head_g, gmlp_out_g, mla_out_g, w_out, ffn2_norm_g, ffn2_w_gate, ffn2_w_up, ffn2_w_down, loss_target, m_ffn1_norm_g, m_ffn1_w_gate, m_ffn1_w_up, m_ffn1_w_down, m_mix_norm_g, m_w_in, m_gmlp_v_norm_g, m_gmlp_w_s, m_gmlp_b_s, m_mla_q_norm_g, m_mla_w_q_up, m_mla_kv_norm_g, m_mla_w_kv_up, m_mla_q_head_g, m_mla_k_head_g, m_gmlp_out_g, m_mla_out_g, m_w_out, m_ffn2_norm_g, m_ffn2_w_gate, m_ffn2_w_up, m_ffn2_w_down, v_ffn1_norm_g, v_ffn1_w_gate, v_ffn1_w_up, v_ffn1_w_down, v_mix_norm_g, v_w_in, v_gmlp_v_norm_g, v_gmlp_w_s, v_gmlp_b_s, v_mla_q_norm_g, v_mla_w_q_up, v_mla_kv_norm_g, v_mla_w_kv_up, v_mla_q_head_g, v_mla_k_head_g, v_gmlp_out_g, v_mla_out_g, v_w_out, v_ffn2_norm_g, v_ffn2_w_gate, v_ffn2_w_up, v_ffn2_w_down):
    wts = dict(zip(WEIGHTS, (ffn1_norm_g, ffn1_w_gate, ffn1_w_up, ffn1_w_down, mix_norm_g, w_in, gmlp_v_norm_g, gmlp_w_s, gmlp_b_s, mla_q_norm_g, mla_w_q_up, mla_kv_norm_g, mla_w_kv_up, mla_q_head_g, mla_k_head_g, gmlp_out_g, mla_out_g, w_out, ffn2_norm_g, ffn2_w_gate, ffn2_w_up, ffn2_w_down)))
    mom1 = dict(zip(WEIGHTS, (m_ffn1_norm_g, m_ffn1_w_gate, m_ffn1_w_up, m_ffn1_w_down, m_mix_norm_g, m_w_in, m_gmlp_v_norm_g, m_gmlp_w_s, m_gmlp_b_s, m_mla_q_norm_g, m_mla_w_q_up, m_mla_kv_norm_g, m_mla_w_kv_up, m_mla_q_head_g, m_mla_k_head_g, m_gmlp_out_g, m_mla_out_g, m_w_out, m_ffn2_norm_g, m_ffn2_w_gate, m_ffn2_w_up, m_ffn2_w_down)))
    mom2 = dict(zip(WEIGHTS, (v_ffn1_norm_g, v_ffn1_w_gate, v_ffn1_w_up, v_ffn1_w_down, v_mix_norm_g, v_w_in, v_gmlp_v_norm_g, v_gmlp_w_s, v_gmlp_b_s, v_mla_q_norm_g, v_mla_w_q_up, v_mla_kv_norm_g, v_mla_w_kv_up, v_mla_q_head_g, v_mla_k_head_g, v_gmlp_out_g, v_mla_out_g, v_w_out, v_ffn2_norm_g, v_ffn2_w_gate, v_ffn2_w_up, v_ffn2_w_down)))

    b_loc, seq, d = x.shape
    t = b_loc * seq
    ffs = ffn1_w_gate.shape[2]
    fp = _round_up(ffs, LANE)
    wg = gmlp_v_norm_g.shape[1]
    groups = gmlp_w_s.shape[1]
    rq, rkv = mla_q_norm_g.shape[1], mla_kv_norm_g.shape[1]
    heads = mla_out_g.shape[1]
    assert w_in.shape[2] * N_DEV == 2 * wg + rq + rkv + ROPE and mla_w_kv_up.shape[2] * N_DEV == heads * HEADW
    tm = min(512, t)
    tm_mix = min(256, t)
    blk = min(256, seq)

    xf = x.reshape(t, d)
    target = loss_target.reshape(t, d)
    pos = positions.reshape(t, 1).astype(F32)
    half = ROPE // 2
    inv_freq = 1.0 / (ROPE_THETA ** (jnp.arange(half, dtype=F32) / half))
    freq = jnp.concatenate([inv_freq, inv_freq, jnp.zeros((LANE - ROPE,), F32)])[None, :]
    lane = jnp.arange(LANE)
    masks = jnp.stack([jnp.where(lane < half, -1.0, 0.0),
                       jnp.where((lane >= half) & (lane < ROPE), 1.0, 0.0)]).astype(F32)
    gqh = jnp.pad(mla_q_head_g, ((0, 0), (0, HEADW - QK)))
    gkh = jnp.pad(mla_k_head_g, ((0, 0), (0, HEADW - QK)))
    bias = jnp.repeat(gmlp_b_s[0].T, CHUNK, axis=1)
    gouta = gmlp_out_g.reshape(1, wg)
    goutb = mla_out_g.reshape(1, heads * VHEAD)
    ws = gmlp_w_s[0]

    px, py, pc = _place()
    me = 4 * px + 2 * py + pc
    core = pc.astype(jnp.int32).reshape(1)
    slot = (2 * px + py).astype(jnp.int32).reshape(1)
    order = [n for n in WEIGHTS if n in SHARDED]
    group = {"ffn1": [n for n in order if n.startswith("ffn1")], "ffn2": [n for n in order if n.startswith("ffn2")],
             "mix": [n for n in order if not n.startswith("ffn")]}
    shard = {n: _shard_rows(wts[n], SHARDED[n], fp) for n in group["ffn1"]}

    def tied(arr, token):
        return arr + token[0, 0].astype(arr.dtype)

    xnb, ynb, dgn = 4 * (1 - px) + 2 * py, 4 * px + 2 * (1 - py), 4 * (1 - px) + 2 * (1 - py)
    ids_a = jnp.stack([me, 4 * px + 2 * py + (1 - pc)]).astype(jnp.int32)
    ids_b = jnp.stack([xnb, xnb + 1, ynb, ynb + 1]).astype(jnp.int32)
    ids_c = jnp.stack([dgn, dgn + 1]).astype(jnp.int32)
    g1 = _StagedGather([shard[n] for n in group["ffn1"]], me, "gather_ffn1")
    token = g1.start("own")
    token = g1.start("nbr", deps=(token,))
    for n in group["mix"] + group["ffn2"]:
        shard[n] = _shard_rows(tied(wts[n], token), SHARDED[n], fp if n.startswith("ffn") else None)
    g3 = _StagedGather([shard[n] for n in group["ffn2"]], me, "gather_ffn2")
    g1.wait("own", token)
    x1, xn1, kept1 = _ffn_fwd(xf, None, ffn1_norm_g, ids_a, *g1.lands(), None, tm, "ffn1_fwd_a")
    g1.wait("nbr", x1)
    token = g1.start("diag")
    ici2 = _gather_ici([shard[n] for n in group["mix"]], me, "gather_mix_ici", deps=(token,))
    token = g3.start("nbr", deps=(ici2.token,))
    token = g1.start("nbr_d2d", deps=(token,))
    g1.wait("nbr_d2d", token)
    x1, xn1, kept1 = _ffn_fwd(x1, xn1, None, ids_b, *g1.lands(), kept1, tm, "ffn1_fwd_b")
    g1.wait("diag", x1)
    token = g1.start("diag_d2d")
    g1.wait("diag_d2d", token)
    full = dict(zip(group["ffn1"], g1.lands()))
    x1, xn1, (gd1, sl1, h1) = _ffn_fwd(x1, xn1, None, ids_c, full["ffn1_w_gate"], full["ffn1_w_up"],
                                       full["ffn1_w_down"], kept1, tm, "ffn1_fwd_c")
    d2d2 = _gather_d2d(ici2.wait(x1)[len(group["mix"]):], "gather_mix_d2d")
    full.update(zip(group["mix"], d2d2.wait(d2d2.token)))
    win_t = jnp.pad(full["w_in"].reshape(-1, d), ((0, LANE - ROPE), (0, 0)))
    splits = (2 * wg, rq, rkv, LANE)
    wq_t = jnp.pad(full["mla_w_q_up"].reshape(heads, QK, rq), ((0, 0), (0, HEADW - QK), (0, 0)))
    wkv_t = full["mla_w_kv_up"].reshape(heads, HEADW, rkv)
    wout = full["w_out"].reshape(-1, d)
    hn, zuv, cq, ckv, krw = _inproj_fwd(x1, mix_norm_g, win_t, splits, tm)
    ya = _gmlp_fwd(zuv, gmlp_v_norm_g, ws, bias, gouta, tm_mix)
    g3.wait("nbr", ya)
    token = g3.start("diag")
    token = g3.start("own_nbr_d2d", deps=(token,))
    q, k, vv = _mla_proj_fwd(cq, ckv, krw, pos, freq, masks, mla_q_norm_g, mla_kv_norm_g, wq_t, wkv_t,
                             tied(gqh, token), gkh, tm_mix)
    o, lse = _attn_fwd(q, k, vv, seq, blk)
    g3.wait("diag", o)
    token = g3.start("diag_d2d")
    x2, ycat = _out_fwd(ya, o, tied(goutb, token), wout, x1, tm)
    g3.wait("own_nbr_d2d", x2)
    g3.wait("diag_d2d", x2)
    full.update(zip(group["ffn2"], g3.lands()))
    x3, xn2, (gd2, sl2, h2) = _ffn_fwd(x2, None, ffn2_norm_g, jnp.arange(N_DEV, dtype=jnp.int32),
                                       full["ffn2_w_gate"], full["ffn2_w_up"], full["ffn2_w_down"], None, tm,
                                       "ffn2_fwd")
    dx3, loss_part = _loss_head(x3, target, tm)

    outs_g, outs_d, outs_m, outs_v = {}, {}, {}, {}

    def finish(names, chip, got, after):
        for n, cp, gt in zip(names, chip, got):
            rows = wts[n].shape[2] if SHARDED[n] else wts[n].shape[1]
            g = _sum_owned(cp, gt, slot, rows, "sum_owned_" + n, after)
            g = g.T if SHARDED[n] else g
            dlt, nm, nv = _adamw(wts[n][0], g, mom1[n][0], mom2[n][0], "adamw_" + n)
            outs_g[n], outs_d[n], outs_m[n], outs_v[n] = g[None], dlt[None], nm[None], nv[None]

    def chip_sums(names, ex, after):
        res = ex.wait(after)
        return [_pair_add(f, gt, core, "pair_add_" + n) for n, f, gt in zip(names, res[:len(names)], res[len(names):])]

    tk = min(512, t)
    grads = {}
    small = {}
    dx2, small["ffn2_norm_g"], da2, db2 = _ffn_bwd(
        dx3, x2, ffn2_norm_g, gd2, sl2, full["ffn2_w_gate"], full["ffn2_w_up"], full["ffn2_w_down"], tm, "ffn2_bwd")
    grads["ffn2_w_gate"] = _matmul_tn(da2, xn2, fp, d, tk, BF16, "dw_ffn2_gate").reshape(N_DEV, fp, d)
    grads["ffn2_w_up"] = _matmul_tn(db2, xn2, fp, d, tk, BF16, "dw_ffn2_up").reshape(N_DEV, fp, d)
    grads["ffn2_w_down"] = _matmul_tn(h2, dx3, fp, d, tk, BF16, "dw_ffn2_down", rhs_scale=0.5).reshape(
        N_DEV, fp, d)
    red_a2 = _reduce_d2d([grads[n] for n in group["ffn2"]], "reduce_ffn2_d2d")
    dya, do, delta, small["mla_out_g"] = _out_bwd(dx2, o, tied(goutb, red_a2.token), wout, wg, tm)
    grads["w_out"] = _matmul_tn(ycat, dx2, _col_block(ycat.shape[1], 768), d, tk, BF16, "dw_out").reshape(
        N_DEV, -1, d)
    chip2 = chip_sums(group["ffn2"], red_a2, dya)
    red_b2 = _reduce_ici(chip2, "reduce_ffn2_ici")
    dq, dk, dv = _attn_bwd(q, k, vv, do, lse, delta, seq, blk, red_b2.token)
    (dcq, dckv, dkrw, dwq, dwkv, small["mla_q_norm_g"], small["mla_kv_norm_g"], dgqh, dgkh) = _mla_proj_bwd(
        dq, dk, dv, cq, ckv, krw, pos, freq, masks, mla_q_norm_g, mla_kv_norm_g, wq_t, wkv_t, gqh, gkh, tm_mix)
    small["mla_q_head_g"], small["mla_k_head_g"] = dgqh[:, :QK], dgkh[:, :QK]
    grads["mla_w_q_up"] = dwq[:, :QK].astype(BF16).reshape(N_DEV, -1, rq)
    grads["mla_w_kv_up"] = dwkv.astype(BF16).reshape(N_DEV, -1, rkv)
    dzuv, small["gmlp_w_s"], dbs, small["gmlp_v_norm_g"], small["gmlp_out_g"] = _gmlp_bwd(
        dya, zuv, gmlp_v_norm_g, ws, bias, gouta, tm_mix)
    small["gmlp_b_s"] = dbs[:, :, 0]
    dx1, small["mix_norm_g"], dzc = _inproj_bwd([dzuv, dcq, dckv, dkrw], x1, mix_norm_g, win_t, dx2, splits,
                                                tm_mix)
    dwin = _matmul_tn(dzc, hn, _col_block(dzc.shape[1], 768), d, tk, BF16, "dw_in")
    grads["w_in"] = dwin[:N_DEV * w_in.shape[2]].reshape(N_DEV, -1, d)
    res_b2 = red_b2.wait(grads["w_in"])
    red_am = _reduce_d2d([grads[n] for n in group["mix"]], "reduce_mix_d2d")
    dx0, small["ffn1_norm_g"], da1, db1 = _ffn_bwd(
        dx1, xf, tied(ffn1_norm_g, red_am.token), gd1, sl1, full["ffn1_w_gate"], full["ffn1_w_up"],
        full["ffn1_w_down"], tm, "ffn1_bwd")
    chipm = chip_sums(group["mix"], red_am, dx0)
    red_bm = _reduce_ici(chipm, "reduce_mix_ici")
    rep = [n for n in WEIGHTS if n not in SHARDED]
    small_ici = _gather_ici([_pack([small[n] for n in rep] + [loss_part])], me, "gather_small_ici",
                            deps=(red_bm.token,))
    operands = {"ffn1_w_gate": (da1, xn1, None), "ffn1_w_up": (db1, xn1, None), "ffn1_w_down": (h1, dx1, 0.5)}
    token, red_a, red_b, small_d2d = small_ici.token, None, [], None
    for n in group["ffn1"]:
        lhs, rhs, scale = operands[n]
        gr = _matmul_tn(lhs, rhs, fp, d, tk, BF16, "dw_" + n, rhs_scale=scale, deps=(token,)).reshape(
            N_DEV, fp, d)
        if red_a is not None:
            prev, ex = red_a
            red_b.append((prev, _reduce_ici(chip_sums([prev], ex, gr), "reduce_" + prev + "_ici")))
            token = red_b[-1][1].token
            if n == group["ffn1"][-1]:
                small_d2d = _gather_d2d(small_ici.wait(gr)[1:], "gather_small_d2d", deps=(token,))
                token = small_d2d.token
        red_a = (n, _reduce_d2d([gr], "reduce_" + n + "_d2d", deps=(token,)))
        token = red_a[1].token
    prev, ex = red_a
    red_b.append((prev, _reduce_ici(chip_sums([prev], ex, ex.token), "reduce_" + prev + "_ici")))
    token = red_b[-1][1].token
    finish(group["ffn2"], res_b2[:3], res_b2[3:], token)
    res = red_bm.wait(outs_g[group["ffn2"][-1]])
    nm_ = len(group["mix"])
    finish(group["mix"], res[:nm_], res[nm_:], token)
    total = _sum_devices(small_d2d.wait(outs_g[group["mix"][-1]])[0])
    zero = jnp.zeros((1,), F32)
    dlt, nm, nv = _adamw(_pack([wts[n] for n in rep] + [zero]), total, _pack([mom1[n] for n in rep] + [zero]),
                         _pack([mom2[n] for n in rep] + [zero]), "adamw_small")
    shapes = [wts[n].shape for n in rep] + [(1,)]
    for n, g, dl, m1, m2 in zip(rep, _unpack(total, shapes), _unpack(dlt, shapes), _unpack(nm, shapes),
                                _unpack(nv, shapes)):
        outs_g[n], outs_d[n], outs_m[n], outs_v[n] = g, dl, m1, m2
    loss = _unpack(total, shapes)[-1].reshape(())
    for n, ex in red_b:
        res = ex.wait(dlt)
        finish([n], res[:1], res[1:], token)

    return (loss, dx0.reshape(b_loc, seq, d), *[outs_g[n] for n in WEIGHTS], *[outs_d[n] for n in WEIGHTS],
            *[outs_m[n] for n in WEIGHTS], *[outs_v[n] for n in WEIGHTS])
```

```python
import functools

import jax
import jax.numpy as jnp
from jax import lax
from jax.experimental import pallas as pl
from jax.experimental.pallas import tpu as pltpu

F32 = jnp.float32
BF16 = jnp.bfloat16
EPS = 1e-6
LANE = 128
SUBLANE = 8
N_DEV = 8
VMEM_LIMIT = 60 * 1024 * 1024
NOPE = 128
ROPE = 64
VHEAD = 128
QK = NOPE + ROPE
HEADW = 2 * LANE
CHUNK = 128
ROPE_THETA = 10000.0
ADAM_LR, ADAM_B1, ADAM_B2, ADAM_EPS, ADAM_WD, ADAM_STEP = 0.001, 0.9, 0.999, 1e-08, 0.01, 10
MESH = pl.DeviceIdType.MESH
ANY = pl.BlockSpec(memory_space=pl.ANY)
WHOLE_VMEM = pl.BlockSpec(memory_space=pltpu.VMEM)


def _params(sem=None):
    return pltpu.CompilerParams(dimension_semantics=sem, vmem_limit_bytes=VMEM_LIMIT)


def _round_up(n, m):
    return -(-n // m) * m


def _row_block(rows, target):
    best = rows
    for cand in range(SUBLANE, min(rows, target) + 1, SUBLANE):
        if rows % cand == 0:
            best = cand
    return best if best <= target else rows


def _nn(a, b):
    return jnp.dot(a, b, preferred_element_type=F32)


def _nt(a, b):
    return lax.dot_general(a, b, (((1,), (1,)), ((), ())), preferred_element_type=F32)


def _tn(a, b):
    return lax.dot_general(a, b, (((0,), (0,)), ((), ())), preferred_element_type=F32)


def _rstd(x, n):
    return lax.rsqrt(jnp.sum(x * x, axis=-1, keepdims=True) * (1.0 / n) + EPS)


def _rms_fwd(x, g, n):
    return x * _rstd(x, n) * g


def _rms_bwd(x, g, dy, n):
    r = _rstd(x, n)
    xh = x * r
    dyg = dy * g
    dx = r * (dyg - xh * (jnp.sum(dyg * xh, axis=-1, keepdims=True) * (1.0 / n)))
    return dx, jnp.sum(dy * xh, axis=0, keepdims=True)


def _gelu(x):
    return 0.5 * x * (1.0 + lax.erf(x * 0.7071067811865476))


def _gelu_grad(x):
    return 0.5 * (1.0 + lax.erf(x * 0.7071067811865476)) + x * jnp.exp(-0.5 * x * x) * 0.3989422804014327


def _ffn_fwd(base, xn, g, ids, wg_t, wu_t, wd, saved, tm, name):
    t, d = base.shape
    nb, fp, _ = wg_t.shape
    n = ids.shape[0]
    first = xn is None
    if saved is None:
        saved = [lax.empty((t, nb * fp), BF16) for _ in range(3)]

    def body(ids_ref, *refs):
        if first:
            base_ref, g_ref, wg_ref, wu_ref, wd_ref, _, _, _, out_ref, xn_ref, gd_ref, sl_ref, h_ref, acc = refs
        else:
            base_ref, xn_ref, wg_ref, wu_ref, wd_ref, _, _, _, out_ref, gd_ref, sl_ref, h_ref, acc = refs
        j = pl.program_id(1)

        @pl.when(j == 0)
        def _():
            if first:
                xn_ref[...] = _rms_fwd(base_ref[...], g_ref[...], d).astype(BF16)
            acc[...] = jnp.zeros_like(acc)

        xnb = xn_ref[...]
        a = _nt(xnb, wg_ref[0])
        b = _nt(xnb, wu_ref[0])
        s = jax.nn.sigmoid(a)
        sl = a * s
        h = (sl * b).astype(BF16)
        gd_ref[...] = (b * (s * (1.0 + a * (1.0 - s)))).astype(BF16)
        sl_ref[...] = sl.astype(BF16)
        h_ref[...] = h
        acc[...] += _nn(h, wd_ref[0])

        @pl.when(j == n - 1)
        def _():
            out_ref[...] = base_ref[...] + 0.5 * acc[...]

    wspec = pl.BlockSpec((1, fp, d), lambda i, j, ids_ref: (ids_ref[j], 0, 0))
    row = pl.BlockSpec((tm, d), lambda i, j, ids_ref: (i, 0))
    ff = pl.BlockSpec((tm, fp), lambda i, j, ids_ref: (i, ids_ref[j]))
    ffs = jax.ShapeDtypeStruct((t, nb * fp), BF16)
    second = pl.BlockSpec((1, d), lambda i, j, ids_ref: (0, 0)) if first else row
    n_row_outs = 2 if first else 1
    res = pl.pallas_call(
        body, name=name,
        grid_spec=pltpu.PrefetchScalarGridSpec(
            num_scalar_prefetch=1, grid=(t // tm, n),
            in_specs=[row, second, wspec, wspec, wspec, ANY, ANY, ANY],
            out_specs=[row] * n_row_outs + [ff, ff, ff],
            scratch_shapes=[pltpu.VMEM((tm, d), F32)]),
        out_shape=[jax.ShapeDtypeStruct((t, d), F32)] + ([jax.ShapeDtypeStruct((t, d), BF16)] if first else [])
        + [ffs, ffs, ffs],
        input_output_aliases={6 + k: n_row_outs + k for k in range(3)},
        compiler_params=_params(("arbitrary", "arbitrary")),
    )(ids, base, g if first else xn, wg_t, wu_t, wd, *saved)
    return (res[0], res[1] if first else xn, list(res[n_row_outs:]))


def _ffn_bwd(dout, x, g, gd, sl, wg_t, wu_t, wd, tm, name):
    t, d = x.shape
    nb, fp, _ = wg_t.shape

    def body(do_hbm, x_hbm, g_ref, gd_ref, sl_ref, wg_ref, wu_ref, wd_ref, wd_next_ref,
             dx_hbm, dg_ref, da_ref, db_ref, acc, rowbuf, dy_scr, dh_scr, sem):
        i, j = pl.program_id(0), pl.program_id(1)
        rows = pl.ds(pl.multiple_of(i * tm, tm), tm)

        def fetch(src):
            cp = pltpu.make_async_copy(src.at[rows, :], rowbuf, sem)
            cp.start()
            cp.wait()

        @pl.when(j == 0)
        def _():
            fetch(do_hbm)
            dy_scr[...] = (0.5 * rowbuf[...]).astype(BF16)
            acc[...] = jnp.zeros_like(acc)
            dh_scr[0] = _nt(dy_scr[...], wd_ref[0])

        @pl.when((i == 0) & (j == 0))
        def _():
            dg_ref[...] = jnp.zeros_like(dg_ref)

        dh = dh_scr[j % 2]
        dh_scr[(j + 1) % 2] = _nt(dy_scr[...], wd_next_ref[0])
        da = (dh * gd_ref[...].astype(F32)).astype(BF16)
        db = (dh * sl_ref[...].astype(F32)).astype(BF16)
        da_ref[...] = da
        db_ref[...] = db
        acc[...] += _nn(da, wg_ref[0]) + _nn(db, wu_ref[0])

        @pl.when(j == nb - 1)
        def _():
            fetch(x_hbm)
            dxn, dg = _rms_bwd(rowbuf[...], g_ref[...], acc[...], d)
            dg_ref[...] += dg
            acc[...] = dxn
            fetch(do_hbm)
            acc[...] += rowbuf[...]
            out = pltpu.make_async_copy(acc, dx_hbm.at[rows, :], sem)
            out.start()
            out.wait()

    wspec = pl.BlockSpec((1, fp, d), lambda i, j: (j, 0, 0))
    wnext = pl.BlockSpec((1, fp, d), lambda i, j: (jnp.minimum(j + 1, nb - 1), 0, 0))
    vec = pl.BlockSpec((1, d), lambda i, j: (0, 0))
    ff = pl.BlockSpec((tm, fp), lambda i, j: (i, j))
    ffs = jax.ShapeDtypeStruct((t, nb * fp), BF16)
    return pl.pallas_call(
        body, name=name, grid=(t // tm, nb),
        in_specs=[ANY, ANY, vec, ff, ff, wspec, wspec, pl.BlockSpec((1, fp, d), lambda i, j: (0, 0, 0)), wnext],
        out_specs=[ANY, vec, ff, ff],
        out_shape=[jax.ShapeDtypeStruct((t, d), F32), jax.ShapeDtypeStruct((1, d), F32), ffs, ffs],
        scratch_shapes=[pltpu.VMEM((tm, d), F32), pltpu.VMEM((tm, d), F32), pltpu.VMEM((tm, d), BF16),
                        pltpu.VMEM((2, tm, fp), F32), pltpu.SemaphoreType.DMA],
        compiler_params=_params(("arbitrary", "arbitrary")),
    )(dout, x, g, gd, sl, wg_t, wu_t, wd, wd)


def _matmul_tn(lhs, rhs, bm, bn, tk, out_dtype, name, rhs_scale=None, deps=()):
    t, m = lhs.shape
    n = rhs.shape[1]
    nk = t // tk

    def body(l_ref, r_ref, *refs):
        o_ref, acc = refs[len(deps):]
        k = pl.program_id(2)

        @pl.when(k == 0)
        def _():
            acc[...] = jnp.zeros_like(acc)

        r = r_ref[...] if rhs_scale is None else rhs_scale * r_ref[...]
        acc[...] += _tn(l_ref[...].astype(BF16), r.astype(BF16))

        @pl.when(k == nk - 1)
        def _():
            o_ref[...] = acc[...].astype(out_dtype)

    return pl.pallas_call(
        body, name=name, grid=(m // bm, n // bn, nk),
        in_specs=[pl.BlockSpec((tk, bm), lambda i, j, k: (k, i)), pl.BlockSpec((tk, bn), lambda i, j, k: (k, j))]
        + [ANY] * len(deps),
        out_specs=pl.BlockSpec((bm, bn), lambda i, j, k: (i, j)),
        out_shape=jax.ShapeDtypeStruct((m, n), out_dtype),
        scratch_shapes=[pltpu.VMEM((bm, bn), F32)],
        compiler_params=_params(("arbitrary", "arbitrary", "arbitrary")),
    )(lhs, rhs, *deps)


def _inproj_fwd(x, g, w_t, splits, tm):
    t, d = x.shape
    offs = [sum(splits[:k]) for k in range(len(splits))]

    def body(x_ref, g_ref, w_ref, hn_ref, *z_refs):
        hn = _rms_fwd(x_ref[...], g_ref[...], d).astype(BF16)
        hn_ref[...] = hn
        for z_ref, o, n in zip(z_refs, offs, splits):
            z_ref[...] = _nt(hn, w_ref[o:o + n, :])

    row = pl.BlockSpec((tm, d), lambda i: (i, 0))
    return pl.pallas_call(
        body, name="inproj_fwd", grid=(t // tm,),
        in_specs=[row, pl.BlockSpec((1, d), lambda i: (0, 0)), WHOLE_VMEM],
        out_specs=[row] + [pl.BlockSpec((tm, n), lambda i: (i, 0)) for n in splits],
        out_shape=[jax.ShapeDtypeStruct((t, d), BF16)] + [jax.ShapeDtypeStruct((t, n), F32) for n in splits],
        compiler_params=_params(("arbitrary",)),
    )(x, g, w_t)


def _inproj_bwd(dzs, x, g, w_t, dres, splits, tm):
    t, d = x.shape
    offs = [sum(splits[:k]) for k in range(len(splits))]
    ni = sum(splits)
    nz = len(splits)

    def body(*refs):
        dz_refs = refs[:nz]
        x_ref, g_ref, w_ref, dres_ref, dx_ref, dg_ref, dzc_ref = refs[nz:]
        dhn = jnp.zeros((tm, d), F32)
        for dz_ref, o, n in zip(dz_refs, offs, splits):
            dz = dz_ref[...].astype(BF16)
            dzc_ref[:, o:o + n] = dz
            dhn += _nn(dz, w_ref[o:o + n, :])
        dx, dg = _rms_bwd(x_ref[...], g_ref[...], dhn, d)
        dx_ref[...] = dres_ref[...] + dx

        @pl.when(pl.program_id(0) == 0)
        def _():
            dg_ref[...] = jnp.zeros_like(dg_ref)

        dg_ref[...] += dg

    row = pl.BlockSpec((tm, d), lambda i: (i, 0))
    vec = pl.BlockSpec((1, d), lambda i: (0, 0))
    return pl.pallas_call(
        body, name="inproj_bwd", grid=(t // tm,),
        in_specs=[pl.BlockSpec((tm, n), lambda i: (i, 0)) for n in splits] + [row, vec, WHOLE_VMEM, row],
        out_specs=[row, vec, pl.BlockSpec((tm, ni), lambda i: (i, 0))],
        out_shape=[jax.ShapeDtypeStruct((t, d), F32), jax.ShapeDtypeStruct((1, d), F32),
                   jax.ShapeDtypeStruct((t, ni), BF16)],
        compiler_params=_params(("arbitrary",)),
    )(*dzs, x, g, w_t, dres)


def _tril_bf16(ws_ref, grp):
    rows = lax.broadcasted_iota(jnp.int32, (CHUNK, CHUNK), 0)
    cols = lax.broadcasted_iota(jnp.int32, (CHUNK, CHUNK), 1)
    return jnp.where(rows >= cols, ws_ref[grp], 0.0).astype(BF16)


def _gmlp_mix(zuv_ref, gv_ref, ws_ref, bias_ref, v_scr, mixed_scr, tm, w, groups):
    u = _gelu(zuv_ref[:, 0:w])
    v0 = _gelu(zuv_ref[:, w:2 * w])
    v_scr[...] = _rms_fwd(v0, gv_ref[...], w).astype(BF16)
    for grp in range(groups):
        wsm = _tril_bf16(ws_ref, grp)
        lanes = slice(grp * CHUNK, (grp + 1) * CHUNK)
        for c in range(tm // CHUNK):
            rows = slice(c * CHUNK, (c + 1) * CHUNK)
            mixed_scr[rows, lanes] = _nn(wsm, v_scr[rows, lanes]) + bias_ref[:, lanes]
    return u, v0


def _gmlp_fwd(zuv, gv, ws, bias, gout, tm):
    t, w2 = zuv.shape
    w = w2 // 2
    groups = ws.shape[0]

    def body(zuv_ref, gv_ref, ws_ref, bias_ref, go_ref, y_ref, v_scr, mixed_scr):
        u, _ = _gmlp_mix(zuv_ref, gv_ref, ws_ref, bias_ref, v_scr, mixed_scr, tm, w, groups)
        ya = u * mixed_scr[...]
        for grp in range(groups):
            lanes = slice(grp * CHUNK, (grp + 1) * CHUNK)
            y_ref[:, lanes] = _rms_fwd(ya[:, lanes], go_ref[:, lanes], CHUNK).astype(BF16)

    const2 = lambda i: (0, 0)
    return pl.pallas_call(
        body, name="gmlp_fwd", grid=(t // tm,),
        in_specs=[pl.BlockSpec((tm, w2), lambda i: (i, 0)), pl.BlockSpec((1, w), const2),
                  pl.BlockSpec((groups, CHUNK, CHUNK), lambda i: (0, 0, 0)),
                  pl.BlockSpec((CHUNK, w), const2), pl.BlockSpec((1, w), const2)],
        out_specs=pl.BlockSpec((tm, w), lambda i: (i, 0)),
        out_shape=jax.ShapeDtypeStruct((t, w), BF16),
        scratch_shapes=[pltpu.VMEM((tm, w), BF16), pltpu.VMEM((tm, w), F32)],
        compiler_params=_params(("arbitrary",)),
    )(zuv, gv, ws, bias, gout)


def _gmlp_bwd(dy, zuv, gv, ws, bias, gout, tm):
    t, w2 = zuv.shape
    w = w2 // 2
    groups = ws.shape[0]

    def body(dy_ref, zuv_ref, gv_ref, ws_ref, bias_ref, go_ref,
             dz_ref, dws_ref, dbias_ref, dgv_ref, dgo_ref, v_scr, mixed_scr, dmix_scr, dv_scr):
        @pl.when(pl.program_id(0) == 0)
        def _():
            dws_ref[...] = jnp.zeros_like(dws_ref)
            dbias_ref[...] = jnp.zeros_like(dbias_ref)
            dgv_ref[...] = jnp.zeros_like(dgv_ref)
            dgo_ref[...] = jnp.zeros_like(dgo_ref)

        u, v0 = _gmlp_mix(zuv_ref, gv_ref, ws_ref, bias_ref, v_scr, mixed_scr, tm, w, groups)
        mixed = mixed_scr[...]
        ya = u * mixed
        for grp in range(groups):
            lanes = slice(grp * CHUNK, (grp + 1) * CHUNK)
            dya, dgo = _rms_bwd(ya[:, lanes], go_ref[:, lanes], dy_ref[:, lanes], CHUNK)
            dgo_ref[:, lanes] += dgo
            dz_ref[:, lanes] = dya * mixed[:, lanes] * _gelu_grad(zuv_ref[:, lanes])
            dmix_scr[:, lanes] = dya * u[:, lanes]
        for grp in range(groups):
            wsm = _tril_bf16(ws_ref, grp)
            lanes = slice(grp * CHUNK, (grp + 1) * CHUNK)
            dws = jnp.zeros((CHUNK, CHUNK), F32)
            dbias = jnp.zeros((CHUNK, CHUNK), F32)
            for c in range(tm // CHUNK):
                rows = slice(c * CHUNK, (c + 1) * CHUNK)
                dm = dmix_scr[rows, lanes]
                dmb = dm.astype(BF16)
                dv_scr[rows, lanes] = _tn(wsm, dmb)
                dws += _nt(dmb, v_scr[rows, lanes])
                dbias += dm
            rr = lax.broadcasted_iota(jnp.int32, (CHUNK, CHUNK), 0)
            cc = lax.broadcasted_iota(jnp.int32, (CHUNK, CHUNK), 1)
            dws_ref[grp] += jnp.where(rr >= cc, dws, 0.0)
            dbias_ref[grp] += jnp.sum(dbias, axis=1, keepdims=True)
        dv0, dgv = _rms_bwd(v0, gv_ref[...], dv_scr[...], w)
        dgv_ref[...] += dgv
        dz_ref[:, w:2 * w] = dv0 * _gelu_grad(zuv_ref[:, w:2 * w])

    const2 = lambda i: (0, 0)
    const3 = lambda i: (0, 0, 0)
    return pl.pallas_call(
        body, name="gmlp_bwd", grid=(t // tm,),
        in_specs=[pl.BlockSpec((tm, w), lambda i: (i, 0)), pl.BlockSpec((tm, w2), lambda i: (i, 0)),
                  pl.BlockSpec((1, w), const2), pl.BlockSpec((groups, CHUNK, CHUNK), const3),
                  pl.BlockSpec((CHUNK, w), const2), pl.BlockSpec((1, w), const2)],
        out_specs=[pl.BlockSpec((tm, w2), lambda i: (i, 0)), pl.BlockSpec((groups, CHUNK, CHUNK), const3),
                   pl.BlockSpec((groups, CHUNK, 1), const3), pl.BlockSpec((1, w), const2), pl.BlockSpec((1, w), const2)],
        out_shape=[jax.ShapeDtypeStruct((t, w2), F32), jax.ShapeDtypeStruct((groups, CHUNK, CHUNK), F32),
                   jax.ShapeDtypeStruct((groups, CHUNK, 1), F32), jax.ShapeDtypeStruct((1, w), F32),
                   jax.ShapeDtypeStruct((1, w), F32)],
        scratch_shapes=[pltpu.VMEM((tm, w), BF16), pltpu.VMEM((tm, w), F32),
                        pltpu.VMEM((tm, w), F32), pltpu.VMEM((tm, w), F32)],
        compiler_params=_params(("arbitrary",)),
    )(dy, zuv, gv, ws, bias, gout)


def _rot(x, m_lo, m_hi):
    return pltpu.roll(x, LANE - ROPE // 2, 1) * m_lo + pltpu.roll(x, ROPE // 2, 1) * m_hi


def _rope_tables(pos_ref, freq_ref):
    ang = pos_ref[...] * freq_ref[...]
    return jnp.cos(ang), jnp.sin(ang)


def _mla_proj_fwd(cq, ckv, krw, pos, freq, masks, gq, gkv, wq_t, wkv_t, gqh, gkh, tm):
    t, rq = cq.shape
    rkv = ckv.shape[1]
    heads = wq_t.shape[0]

    def body(cq_ref, ckv_ref, kr_ref, pos_ref, freq_ref, mk_ref, gq_ref, gkv_ref, wq_ref, wkv_ref,
             gqh_ref, gkh_ref, q_ref, k_ref, v_ref):
        cos, sin = _rope_tables(pos_ref, freq_ref)
        m_lo, m_hi = mk_ref[0:1, :], mk_ref[1:2, :]
        cqn = _rms_fwd(cq_ref[...], gq_ref[...], rq).astype(BF16)
        ckvn = _rms_fwd(ckv_ref[...], gkv_ref[...], rkv).astype(BF16)
        kr = kr_ref[...]
        kr_ss = jnp.sum(kr * kr, axis=-1, keepdims=True)
        for h in range(heads):
            qh = _nt(cqn, wq_ref[h])
            qn = qh * _rstd(qh, QK) * gqh_ref[...]
            qr = qn[:, LANE:]
            q_ref[h, :, 0:LANE] = qn[:, 0:LANE].astype(BF16)
            q_ref[h, :, LANE:] = (qr * cos + _rot(qr, m_lo, m_hi) * sin).astype(BF16)
            kvh = _nt(ckvn, wkv_ref[h])
            kn = kvh[:, 0:LANE]
            rk = lax.rsqrt((jnp.sum(kn * kn, axis=-1, keepdims=True) + kr_ss) * (1.0 / QK) + EPS)
            k_ref[h, :, 0:LANE] = (kn * rk * gkh_ref[:, 0:LANE]).astype(BF16)
            krn = kr * rk * gkh_ref[:, LANE:]
            k_ref[h, :, LANE:] = (krn * cos + _rot(krn, m_lo, m_hi) * sin).astype(BF16)
            v_ref[h] = kvh[:, LANE:].astype(BF16)

    c2 = lambda i: (0, 0)
    c3 = lambda i: (0, 0, 0)
    return pl.pallas_call(
        body, name="mla_proj_fwd", grid=(t // tm,),
        in_specs=[pl.BlockSpec((tm, rq), lambda i: (i, 0)), pl.BlockSpec((tm, rkv), lambda i: (i, 0)),
                  pl.BlockSpec((tm, LANE), lambda i: (i, 0)), pl.BlockSpec((tm, 1), lambda i: (i, 0)),
                  pl.BlockSpec((1, LANE), c2), pl.BlockSpec((2, LANE), c2),
                  pl.BlockSpec((1, rq), c2), pl.BlockSpec((1, rkv), c2),
                  pl.BlockSpec((heads, HEADW, rq), c3), pl.BlockSpec((heads, HEADW, rkv), c3),
                  pl.BlockSpec((1, HEADW), c2), pl.BlockSpec((1, HEADW), c2)],
        out_specs=[pl.BlockSpec((heads, tm, HEADW), lambda i: (0, i, 0)),
                   pl.BlockSpec((heads, tm, HEADW), lambda i: (0, i, 0)),
                   pl.BlockSpec((heads, tm, VHEAD), lambda i: (0, i, 0))],
        out_shape=[jax.ShapeDtypeStruct((heads, t, HEADW), BF16), jax.ShapeDtypeStruct((heads, t, HEADW), BF16),
                   jax.ShapeDtypeStruct((heads, t, VHEAD), BF16)],
        compiler_params=_params(("arbitrary",)),
    )(cq, ckv, krw, pos, freq, masks, gq, gkv, wq_t, wkv_t, gqh, gkh)


def _mla_proj_bwd(dq, dk, dv, cq, ckv, krw, pos, freq, masks, gq, gkv, wq_t, wkv_t, gqh, gkh, tm):
    t, rq = cq.shape
    rkv = ckv.shape[1]
    heads = wq_t.shape[0]

    def body(dq_ref, dk_ref, dv_ref, cq_ref, ckv_ref, kr_ref, pos_ref, freq_ref, mk_ref, gq_ref, gkv_ref,
             wq_ref, wkv_ref, gqh_ref, gkh_ref,
             dcq_ref, dckv_ref, dkr_ref, dwq_ref, dwkv_ref, dgq_ref, dgkv_ref, dgqh_ref, dgkh_ref):
        @pl.when(pl.program_id(0) == 0)
        def _():
            for r in (dwq_ref, dwkv_ref, dgq_ref, dgkv_ref, dgqh_ref, dgkh_ref):
                r[...] = jnp.zeros_like(r)

        cos, sin = _rope_tables(pos_ref, freq_ref)
        m_lo, m_hi = mk_ref[0:1, :], mk_ref[1:2, :]

        def unrope(dy):
            return dy * cos - _rot(dy * sin, m_lo, m_hi)

        cqn = _rms_fwd(cq_ref[...], gq_ref[...], rq).astype(BF16)
        ckvn = _rms_fwd(ckv_ref[...], gkv_ref[...], rkv).astype(BF16)
        kr = kr_ref[...]
        kr_ss = jnp.sum(kr * kr, axis=-1, keepdims=True)
        dcqn = jnp.zeros((tm, rq), F32)
        dckvn = jnp.zeros((tm, rkv), F32)
        dkr = jnp.zeros((tm, LANE), F32)
        for h in range(heads):
            qh = _nt(cqn, wq_ref[h])
            dqn = jnp.concatenate([dq_ref[h, :, 0:LANE], unrope(dq_ref[h, :, LANE:])], axis=1)
            dqh, dg = _rms_bwd(qh, gqh_ref[...], dqn, QK)
            dgqh_ref[...] += dg
            dqh = dqh.astype(BF16)
            dcqn += _nn(dqh, wq_ref[h])
            dwq_ref[h] += _tn(dqh, cqn)

            kvh = _nt(ckvn, wkv_ref[h])
            kn = kvh[:, 0:LANE]
            rk = lax.rsqrt((jnp.sum(kn * kn, axis=-1, keepdims=True) + kr_ss) * (1.0 / QK) + EPS)
            dkn_n = dk_ref[h, :, 0:LANE]
            dkr_n = unrope(dk_ref[h, :, LANE:])
            knh, krh = kn * rk, kr * rk
            dgkh_ref[:, 0:LANE] += jnp.sum(dkn_n * knh, axis=0, keepdims=True)
            dgkh_ref[:, LANE:] += jnp.sum(dkr_n * krh, axis=0, keepdims=True)
            dkn_g, dkr_g = dkn_n * gkh_ref[:, 0:LANE], dkr_n * gkh_ref[:, LANE:]
            proj = (jnp.sum(dkn_g * knh, axis=-1, keepdims=True)
                    + jnp.sum(dkr_g * krh, axis=-1, keepdims=True)) * (1.0 / QK)
            dkr += rk * (dkr_g - krh * proj)
            dkvh = jnp.concatenate([rk * (dkn_g - knh * proj), dv_ref[h]], axis=1).astype(BF16)
            dckvn += _nn(dkvh, wkv_ref[h])
            dwkv_ref[h] += _tn(dkvh, ckvn)
        dkr_ref[...] = dkr
        dcq, dg = _rms_bwd(cq_ref[...], gq_ref[...], dcqn, rq)
        dcq_ref[...] = dcq
        dgq_ref[...] += dg
        dckv, dg = _rms_bwd(ckv_ref[...], gkv_ref[...], dckvn, rkv)
        dckv_ref[...] = dckv
        dgkv_ref[...] += dg

    c2 = lambda i: (0, 0)
    c3 = lambda i: (0, 0, 0)
    hq = pl.BlockSpec((heads, tm, HEADW), lambda i: (0, i, 0))
    return pl.pallas_call(
        body, name="mla_proj_bwd", grid=(t // tm,),
        in_specs=[hq, hq, pl.BlockSpec((heads, tm, VHEAD), lambda i: (0, i, 0)),
                  pl.BlockSpec((tm, rq), lambda i: (i, 0)), pl.BlockSpec((tm, rkv), lambda i: (i, 0)),
                  pl.BlockSpec((tm, LANE), lambda i: (i, 0)), pl.BlockSpec((tm, 1), lambda i: (i, 0)),
                  pl.BlockSpec((1, LANE), c2), pl.BlockSpec((2, LANE), c2),
                  pl.BlockSpec((1, rq), c2), pl.BlockSpec((1, rkv), c2),
                  pl.BlockSpec((heads, HEADW, rq), c3), pl.BlockSpec((heads, HEADW, rkv), c3),
                  pl.BlockSpec((1, HEADW), c2), pl.BlockSpec((1, HEADW), c2)],
        out_specs=[pl.BlockSpec((tm, rq), lambda i: (i, 0)), pl.BlockSpec((tm, rkv), lambda i: (i, 0)),
                   pl.BlockSpec((tm, LANE), lambda i: (i, 0)),
                   pl.BlockSpec((heads, HEADW, rq), c3), pl.BlockSpec((heads, HEADW, rkv), c3),
                   pl.BlockSpec((1, rq), c2), pl.BlockSpec((1, rkv), c2),
                   pl.BlockSpec((1, HEADW), c2), pl.BlockSpec((1, HEADW), c2)],
        out_shape=[jax.ShapeDtypeStruct((t, rq), F32), jax.ShapeDtypeStruct((t, rkv), F32),
                   jax.ShapeDtypeStruct((t, LANE), F32),
                   jax.ShapeDtypeStruct((heads, HEADW, rq), F32), jax.ShapeDtypeStruct((heads, HEADW, rkv), F32),
                   jax.ShapeDtypeStruct((1, rq), F32), jax.ShapeDtypeStruct((1, rkv), F32),
                   jax.ShapeDtypeStruct((1, HEADW), F32), jax.ShapeDtypeStruct((1, HEADW), F32)],
        compiler_params=_params(("arbitrary",)),
    )(dq, dk, dv, cq, ckv, krw, pos, freq, masks, gq, gkv, wq_t, wkv_t, gqh, gkh)


def _lower_triangle(blk):
    return lax.broadcasted_iota(jnp.int32, (blk, blk), 0) >= lax.broadcasted_iota(jnp.int32, (blk, blk), 1)


def _attn_fwd(q, k, v, seq, blk):
    heads, t, _ = q.shape
    scale = QK ** -0.5
    nblk = seq // blk

    def body(q_ref, k_ref, v_ref, o_ref, lse_ref):
        tri = _lower_triangle(blk)
        for qi in range(nblk):
            rows = slice(qi * blk, (qi + 1) * blk)
            before = slice(0, qi * blk)
            qb = q_ref[0, rows, :]
            s_d = jnp.where(tri, _nt(qb, k_ref[0, rows, :]) * scale, -1e30)
            m = jnp.max(s_d, axis=-1, keepdims=True)
            if qi:
                s_b = _nt(qb, k_ref[0, before, :]) * scale
                m = jnp.maximum(m, jnp.max(s_b, axis=-1, keepdims=True))
                p_b = jnp.exp(s_b - m)
            p_d = jnp.exp(s_d - m)
            l = jnp.sum(p_d, axis=-1, keepdims=True)
            acc = _nn(p_d.astype(BF16), v_ref[0, rows, :])
            if qi:
                l += jnp.sum(p_b, axis=-1, keepdims=True)
                acc += _nn(p_b.astype(BF16), v_ref[0, before, :])
            o_ref[0, rows, :] = acc / l
            lse_ref[0, rows, :] = m + jnp.log(l)

    return pl.pallas_call(
        body, name="attn_fwd", grid=(heads, t // seq),
        in_specs=[pl.BlockSpec((1, seq, HEADW), lambda h, b: (h, b, 0)),
                  pl.BlockSpec((1, seq, HEADW), lambda h, b: (h, b, 0)),
                  pl.BlockSpec((1, seq, VHEAD), lambda h, b: (h, b, 0))],
        out_specs=[pl.BlockSpec((1, seq, VHEAD), lambda h, b: (h, b, 0)),
                   pl.BlockSpec((1, seq, 1), lambda h, b: (h, b, 0))],
        out_shape=[jax.ShapeDtypeStruct((heads, t, VHEAD), F32), jax.ShapeDtypeStruct((heads, t, 1), F32)],
        compiler_params=_params(("arbitrary", "arbitrary")),
    )(q, k, v)


def _attn_bwd(q, k, v, do, lse, delta, seq, blk, after):
    heads, t, _ = q.shape
    scale = QK ** -0.5
    nblk = seq // blk

    def body(q_ref, k_ref, v_ref, do_ref, lse_ref, dl_ref, _, dq_ref, dk_ref, dv_ref):
        tri = _lower_triangle(blk)
        dk_ref[...] = jnp.zeros_like(dk_ref)
        dv_ref[...] = jnp.zeros_like(dv_ref)
        for qi in range(nblk):
            rows = slice(qi * blk, (qi + 1) * blk)
            qb = q_ref[0, rows, :]
            dob = do_ref[0, rows, :]
            lse_b = lse_ref[0, rows, :]
            dl_b = dl_ref[0, rows, :]
            dq = jnp.zeros((blk, HEADW), F32)
            for keys, masked in ((slice(0, qi * blk), False), (rows, True)):
                if keys.stop == keys.start:
                    continue
                kb = k_ref[0, keys, :]
                p = jnp.exp(_nt(qb, kb) * scale - lse_b)
                if masked:
                    p = jnp.where(tri, p, 0.0)
                dp = _nt(dob, v_ref[0, keys, :])
                ds = (p * (dp - dl_b) * scale).astype(BF16)
                dv_ref[0, keys, :] += _tn(p.astype(BF16), dob)
                dk_ref[0, keys, :] += _tn(ds, qb)
                dq += _nn(ds, kb)
            dq_ref[0, rows, :] = dq

    hq = pl.BlockSpec((1, seq, HEADW), lambda h, b: (h, b, 0))
    hv = pl.BlockSpec((1, seq, VHEAD), lambda h, b: (h, b, 0))
    h1 = pl.BlockSpec((1, seq, 1), lambda h, b: (h, b, 0))
    return pl.pallas_call(
        body, name="attn_bwd", grid=(heads, t // seq),
        in_specs=[hq, hq, hv, hv, h1, h1, ANY],
        out_specs=[hq, hq, hv],
        out_shape=[jax.ShapeDtypeStruct((heads, t, HEADW), F32), jax.ShapeDtypeStruct((heads, t, HEADW), F32),
                   jax.ShapeDtypeStruct((heads, t, VHEAD), F32)],
        compiler_params=_params(("arbitrary", "arbitrary")),
    )(q, k, v, do, lse, delta, after)


def _out_fwd(ya, o, gb, w_out, x1, tm):
    t, w = ya.shape
    heads = o.shape[0]
    d = x1.shape[1]

    def body(ya_ref, o_ref, gb_ref, w_ref, x1_ref, x2_ref, yc_ref):
        yc_ref[:, 0:w] = ya_ref[...]
        for h in range(heads):
            lanes = slice(h * VHEAD, (h + 1) * VHEAD)
            yc_ref[:, w + h * VHEAD:w + (h + 1) * VHEAD] = _rms_fwd(o_ref[h], gb_ref[:, lanes], VHEAD).astype(BF16)
        x2_ref[...] = x1_ref[...] + _nn(yc_ref[...], w_ref[...])

    wy = w + heads * VHEAD
    row = pl.BlockSpec((tm, d), lambda i: (i, 0))
    return pl.pallas_call(
        body, name="out_fwd", grid=(t // tm,),
        in_specs=[pl.BlockSpec((tm, w), lambda i: (i, 0)), pl.BlockSpec((heads, tm, VHEAD), lambda i: (0, i, 0)),
                  pl.BlockSpec((1, heads * VHEAD), lambda i: (0, 0)), WHOLE_VMEM, row],
        out_specs=[row, pl.BlockSpec((tm, wy), lambda i: (i, 0))],
        out_shape=[jax.ShapeDtypeStruct((t, d), F32), jax.ShapeDtypeStruct((t, wy), BF16)],
        compiler_params=_params(("arbitrary",)),
    )(ya, o, gb, w_out, x1)


def _out_bwd(dx2, o, gb, w_out, w, tm):
    t, d = dx2.shape
    heads = o.shape[0]

    def body(dx_ref, o_ref, gb_ref, w_ref, dya_ref, do_ref, dl_ref, dgb_ref):
        @pl.when(pl.program_id(0) == 0)
        def _():
            dgb_ref[...] = jnp.zeros_like(dgb_ref)

        dyc = _nt(dx_ref[...].astype(BF16), w_ref[...])
        dya_ref[...] = dyc[:, 0:w]
        for h in range(heads):
            lanes = slice(h * VHEAD, (h + 1) * VHEAD)
            oh = o_ref[h]
            doh, dg = _rms_bwd(oh, gb_ref[:, lanes], dyc[:, w + h * VHEAD:w + (h + 1) * VHEAD], VHEAD)
            dgb_ref[:, lanes] += dg
            do_ref[h] = doh.astype(BF16)
            dl_ref[h] = jnp.sum(doh * oh, axis=-1, keepdims=True)

    ho = pl.BlockSpec((heads, tm, VHEAD), lambda i: (0, i, 0))
    vec = pl.BlockSpec((1, heads * VHEAD), lambda i: (0, 0))
    return pl.pallas_call(
        body, name="out_bwd", grid=(t // tm,),
        in_specs=[pl.BlockSpec((tm, d), lambda i: (i, 0)), ho, vec, WHOLE_VMEM],
        out_specs=[pl.BlockSpec((tm, w), lambda i: (i, 0)), ho, pl.BlockSpec((heads, tm, 1), lambda i: (0, i, 0)), vec],
        out_shape=[jax.ShapeDtypeStruct((t, w), F32), jax.ShapeDtypeStruct((heads, t, VHEAD), BF16),
                   jax.ShapeDtypeStruct((heads, t, 1), F32), jax.ShapeDtypeStruct((1, heads * VHEAD), F32)],
        compiler_params=_params(("arbitrary",)),
    )(dx2, o, gb, w_out)


def _loss_head(y, target, tm):
    t, d = y.shape

    def body(y_ref, t_ref, dy_ref, loss_ref):
        @pl.when(pl.program_id(0) == 0)
        def _():
            loss_ref[...] = jnp.zeros_like(loss_ref)

        err = y_ref[...] - t_ref[...]
        dy_ref[...] = err * (1.0 / d)
        part = jnp.sum(jnp.sum(err * err, axis=-1, keepdims=True) * (1.0 / d), axis=0, keepdims=True)
        loss_ref[...] += 0.5 * part

    row = pl.BlockSpec((tm, d), lambda i: (i, 0))
    return pl.pallas_call(
        body, name="loss_head", grid=(t // tm,),
        in_specs=[row, row], out_specs=[row, pl.BlockSpec((1, 1), lambda i: (0, 0))],
        out_shape=[jax.ShapeDtypeStruct((t, d), F32), jax.ShapeDtypeStruct((1, 1), F32)],
        compiler_params=_params(("arbitrary",)),
    )(y, target)


def _place():
    return lax.axis_index("x"), lax.axis_index("y"), lax.axis_index("c")


HBM = pl.BlockSpec(memory_space=pltpu.HBM)
SEM = pl.BlockSpec(memory_space=pltpu.SEMAPHORE)
DATAFLOW = pltpu.SideEffectType.DATAFLOW_SIDE_EFFECTING


def _plan_copies(plan, refs, send_sems, recv_sems):
    cps = []
    for i, (sb, sblk, db, dblk, dev) in enumerate(plan(*_place())):
        cps.append(pltpu.make_async_remote_copy(
            src_ref=refs[sb] if sblk is None else refs[sb].at[sblk], dst_ref=refs[db].at[dblk],
            send_sem=send_sems.at[i], recv_sem=recv_sems.at[i], device_id=dev, device_id_type=MESH))
    return cps


def _push_start(bufs, plan, ncopy, name, deps=()):
    nb = len(bufs)

    def body(*refs):
        outs = refs[nb + len(deps):]
        for cp in _plan_copies(plan, refs[:nb], outs[0], outs[1]):
            cp.start()
        outs[-1][...] = jnp.zeros_like(outs[-1])

    res = pl.pallas_call(
        body, name=name,
        out_shape=(pltpu.SemaphoreType.DMA((ncopy,)), pltpu.SemaphoreType.DMA((ncopy,)),
                   *[pltpu.HBM(b.shape, b.dtype) for b in bufs], jax.ShapeDtypeStruct((SUBLANE, LANE), F32)),
        in_specs=[HBM] * nb + [ANY] * len(deps),
        out_specs=(SEM, SEM, *[HBM] * nb, WHOLE_VMEM),
        input_output_aliases={i: 2 + i for i in range(nb)},
        compiler_params=pltpu.CompilerParams(has_side_effects=DATAFLOW),
    )(*[pltpu.with_memory_space_constraint(b, pltpu.HBM) for b in bufs], *deps)
    return res[0], res[1], list(res[2:2 + nb]), res[-1]


def _push_wait(send_sems, recv_sems, bufs, plan, after, name):
    nb = len(bufs)

    def body(*refs):
        for cp in _plan_copies(plan, refs[:nb], refs[nb], refs[nb + 1]):
            cp.wait_send()
            cp.wait_recv()

    res = pl.pallas_call(
        body, name=name,
        out_shape=[pltpu.HBM(b.shape, b.dtype) for b in bufs],
        in_specs=[HBM] * nb + [SEM, SEM, ANY], out_specs=[HBM] * nb,
        input_output_aliases={i: i for i in range(nb)},
        compiler_params=pltpu.CompilerParams(has_side_effects=DATAFLOW),
    )(*bufs, send_sems, recv_sems, after)
    return list(res)


def _other_chips(x, y):
    return ((1 - x, y), (x, 1 - y), (1 - x, 1 - y))


class _Exchange:
    def __init__(self, bufs, plan, ncopy, name, deps=()):
        self.plan, self.name = plan, name
        self.send, self.recv, self.bufs, self.token = _push_start(bufs, plan, ncopy, name + "_start", deps)

    def wait(self, after):
        return _push_wait(self.send, self.recv, self.bufs, self.plan, after, self.name + "_wait")


class _Chain:
    def __init__(self, bufs):
        self.bufs = list(bufs)

    def start(self, plan, ncopy, name, deps=()):
        send, recv, self.bufs, token = _push_start(self.bufs, plan, ncopy, name + "_start", deps)
        return (send, recv, plan, name), token

    def wait(self, pending, after):
        send, recv, plan, name = pending
        self.bufs = _push_wait(send, recv, self.bufs, plan, after, name + "_wait")


class _StagedGather:
    def __init__(self, shards, me, name):
        self.n = n = len(shards)
        self.name = name
        self.chain = _Chain(list(shards) + [
            lax.dynamic_update_slice(lax.empty((N_DEV,) + s.shape, s.dtype), s[None], (me, 0, 0)) for s in shards])
        self.pending = {}

        def to_sibling(blocks):
            return lambda x, y, c: [(n + a, b, n + a, b, (x, y, 1 - c)) for a in range(n) for b in blocks(x, y, c)]

        def nbr_blocks(x, y, c):
            return [4 * (1 - x) + 2 * y + c, 4 * x + 2 * (1 - y) + c]

        def diag(x, y, c):
            sx, sy = (1 - x) * (1 - c) + x * c, y * (1 - c) + (1 - y) * c
            tx, ty = x * (1 - c) + (1 - x) * c, (1 - y) * (1 - c) + y * c
            return [(n + a, 4 * sx + 2 * sy + c, n + a, 4 * sx + 2 * sy + c, (tx, ty, c)) for a in range(n)]

        self.plans = {
            "own": (lambda x, y, c: [(a, None, n + a, 4 * x + 2 * y + c, (x, y, 1 - c)) for a in range(n)], n),
            "nbr": (lambda x, y, c: [(a, None, n + a, 4 * x + 2 * y + c, dev) for a in range(n)
                                     for dev in ((1 - x, y, c), (x, 1 - y, c))], 2 * n),
            "diag": (diag, n),
            "nbr_d2d": (to_sibling(nbr_blocks), 2 * n),
            "own_nbr_d2d": (to_sibling(lambda x, y, c: [4 * x + 2 * y + c] + nbr_blocks(x, y, c)), 3 * n),
            "diag_d2d": (to_sibling(lambda x, y, c: [4 * (1 - x) + 2 * (1 - y) + c]), n),
        }

    def start(self, stage, deps=()):
        plan, ncopy = self.plans[stage]
        self.pending[stage], token = self.chain.start(plan, ncopy, self.name + "_" + stage, deps)
        return token

    def wait(self, stage, after):
        self.chain.wait(self.pending.pop(stage), after)

    def lands(self):
        return self.chain.bufs[self.n:]


def _gather_ici(shards, me, name, deps=()):
    n = len(shards)
    lands = [lax.dynamic_update_slice(lax.empty((N_DEV,) + s.shape, s.dtype), s[None], (me, 0, 0)) for s in shards]

    def plan(x, y, c):
        return [(a, None, n + a, 4 * x + 2 * y + c, (px, py, c)) for a in range(n) for px, py in _other_chips(x, y)]

    return _Exchange(list(shards) + lands, plan, 3 * n, name, deps)


def _gather_d2d(lands, name, deps=()):
    n = len(lands)

    def plan(x, y, c):
        blocks = [4 * x + 2 * y + c] + [4 * px + 2 * py + c for px, py in _other_chips(x, y)]
        return [(a, b, a, b, (x, y, 1 - c)) for a in range(n) for b in blocks]

    return _Exchange(list(lands), plan, 4 * n, name, deps)


def _reduce_d2d(grads, name, deps=()):
    n = len(grads)
    lands = [lax.empty((4,) + g.shape[1:], g.dtype) for g in grads]

    def plan(x, y, c):
        return [(a, 2 * s + (1 - c), n + a, s, (x, y, 1 - c)) for a in range(n) for s in range(4)]

    return _Exchange(list(grads) + lands, plan, 4 * n, name, deps)


def _reduce_ici(chip, name, deps=()):
    n = len(chip)
    lands = [lax.empty((3,) + g.shape[1:], g.dtype) for g in chip]

    def plan(x, y, c):
        return [(a, 2 * px + py, n + a, k, (px, py, c))
                for a in range(n) for k, (px, py) in enumerate(_other_chips(x, y))]

    return _Exchange(list(chip) + lands, plan, 3 * n, name, deps)


def _pair_add(full, got, core, name):
    _, r, cdim = full.shape
    br = _row_block(r, 512)

    def body(c_ref, f_ref, g_ref, o_ref):
        o_ref[...] = (f_ref[...].astype(F32) + g_ref[...].astype(F32)).astype(o_ref.dtype)

    return pl.pallas_call(
        body, name=name,
        grid_spec=pltpu.PrefetchScalarGridSpec(
            num_scalar_prefetch=1, grid=(4, r // br),
            in_specs=[pl.BlockSpec((1, br, cdim), lambda s, i, c_ref: (2 * s + c_ref[0], i, 0)),
                      pl.BlockSpec((1, br, cdim), lambda s, i, c_ref: (s, i, 0))],
            out_specs=pl.BlockSpec((1, br, cdim), lambda s, i, c_ref: (s, i, 0))),
        out_shape=jax.ShapeDtypeStruct((4, r, cdim), full.dtype),
        compiler_params=_params(("arbitrary", "arbitrary")),
    )(core, full, got)


def _sum_devices(stack):
    _, r, cdim = stack.shape

    def body(s_ref, o_ref):
        acc = s_ref[0]
        for k in range(1, N_DEV):
            acc = acc + s_ref[k]
        o_ref[...] = acc

    return pl.pallas_call(
        body, name="sum_devices", out_shape=jax.ShapeDtypeStruct((r, cdim), F32),
        compiler_params=_params(),
    )(stack)


def _adamw(w, g, m, v, name):
    r, cdim = w.shape
    br = _row_block(r, 256)

    def body(w_ref, g_ref, m_ref, v_ref, d_ref, nm_ref, nv_ref):
        g = g_ref[...]
        nm = ADAM_B1 * m_ref[...] + (1.0 - ADAM_B1) * g
        nv = ADAM_B2 * v_ref[...] + (1.0 - ADAM_B2) * (g * g)
        m_hat = nm / (1.0 - ADAM_B1 ** ADAM_STEP)
        v_hat = nv / (1.0 - ADAM_B2 ** ADAM_STEP)
        d_ref[...] = -ADAM_LR * (m_hat / (jnp.sqrt(v_hat) + ADAM_EPS) + ADAM_WD * w_ref[...])
        nm_ref[...] = nm
        nv_ref[...] = nv

    spec = pl.BlockSpec((br, cdim), lambda i: (i, 0))
    shape = jax.ShapeDtypeStruct((r, cdim), F32)
    return pl.pallas_call(
        body, name=name, grid=(r // br,), in_specs=[spec] * 4, out_specs=[spec] * 3,
        out_shape=[shape] * 3, compiler_params=_params(("arbitrary",)),
    )(w, g, m, v)


def _sum_adamw(chip, got, slot, w, m, v, name, after):
    rows, cdim = w.shape
    bc = 2 * LANE if cdim % (2 * LANE) == 0 else cdim

    def body(s_ref, c_ref, g_ref, w_ref, m_ref, v_ref, _, go_ref, d_ref, nm_ref, nv_ref):
        g = c_ref[0].astype(F32)
        for k in range(3):
            g = g + g_ref[k].astype(F32)
        nm = ADAM_B1 * m_ref[...] + (1.0 - ADAM_B1) * g
        nv = ADAM_B2 * v_ref[...] + (1.0 - ADAM_B2) * (g * g)
        m_hat = nm / (1.0 - ADAM_B1 ** ADAM_STEP)
        v_hat = nv / (1.0 - ADAM_B2 ** ADAM_STEP)
        go_ref[...] = g
        d_ref[...] = -ADAM_LR * (m_hat / (jnp.sqrt(v_hat) + ADAM_EPS) + ADAM_WD * w_ref[...])
        nm_ref[...] = nm
        nv_ref[...] = nv

    spec = pl.BlockSpec((rows, bc), lambda j, s_ref: (0, j))
    shape = jax.ShapeDtypeStruct((rows, cdim), F32)
    return pl.pallas_call(
        body, name=name,
        grid_spec=pltpu.PrefetchScalarGridSpec(
            num_scalar_prefetch=1, grid=(cdim // bc,),
            in_specs=[pl.BlockSpec((1, rows, bc), lambda j, s_ref: (s_ref[0], 0, j)),
                      pl.BlockSpec((3, rows, bc), lambda j, s_ref: (0, 0, j)), spec, spec, spec, ANY],
            out_specs=[spec] * 4),
        out_shape=[shape] * 4,
        compiler_params=_params(("arbitrary",)),
    )(slot, chip, got, w, m, v, after)


WEIGHTS = ("ffn1_norm_g", "ffn1_w_gate", "ffn1_w_up", "ffn1_w_down", "mix_norm_g", "w_in", "gmlp_v_norm_g",
           "gmlp_w_s", "gmlp_b_s", "mla_q_norm_g", "mla_w_q_up", "mla_kv_norm_g", "mla_w_kv_up", "mla_q_head_g",
           "mla_k_head_g", "gmlp_out_g", "mla_out_g", "w_out", "ffn2_norm_g", "ffn2_w_gate", "ffn2_w_up",
           "ffn2_w_down")
SHARDED = {"ffn1_w_gate": True, "ffn1_w_up": True, "ffn1_w_down": False, "w_in": True, "mla_w_q_up": True,
           "mla_w_kv_up": True, "w_out": False, "ffn2_w_gate": True, "ffn2_w_up": True, "ffn2_w_down": False}


def _col_block(m, target):
    best = LANE
    for cand in range(LANE, min(m, target) + 1, LANE):
        if m % cand == 0:
            best = cand
    return best


def _shard_rows(w, transposed, pad_to=None):
    rows = (w[0].T if transposed else w[0]).astype(BF16)
    if pad_to is not None and pad_to != rows.shape[0]:
        rows = jnp.pad(rows, ((0, pad_to - rows.shape[0]), (0, 0)))
    return rows


def _pack(parts):
    flat = []
    for p in parts:
        f = p.reshape(-1).astype(F32)
        flat.append(jnp.pad(f, (0, _round_up(f.size, LANE) - f.size)))
    flat = jnp.concatenate(flat)
    rows = _round_up(flat.size // LANE, SUBLANE)
    return jnp.pad(flat, (0, rows * LANE - flat.size)).reshape(rows, LANE)


def _unpack(packed, shapes):
    out, row = [], 0
    for shp in shapes:
        size = 1
        for s in shp:
            size *= s
        nrows = _round_up(size, LANE) // LANE
        out.append(packed[row:row + nrows].reshape(-1)[:size].reshape(shp))
        row += nrows
    return out


def kernel(x, positions, ffn1_norm_g, ffn1_w_gate, ffn1_w_up, ffn1_w_down, mix_norm_g, w_in, gmlp_v_norm_g, gmlp_w_s, gmlp_b_s, mla_q_norm_g, mla_w_q_up, mla_kv_norm_g, mla_w_kv_up, mla_q_head_g, mla_k_head_g, gmlp_out_g, mla_out_g, w_out, ffn2_norm_g, ffn2_w_gate, ffn2_w_up, ffn2_w_down, loss_target, m_ffn1_norm_g, m_ffn1_w_gate, m_ffn1_w_up, m_ffn1_w_down, m_mix_norm_g, m_w_in, m_gmlp_v_norm_g, m_gmlp_w_s, m_gmlp_b_s, m_mla_q_norm_g, m_mla_w_q_up, m_mla_kv_norm_g, m_mla_w_kv_up, m_mla_q_head_g, m_mla_k_head_g, m_gmlp_out_g, m_mla_out_g, m_w_out, m_ffn2_norm_g, m_ffn2_w_gate, m_ffn2_w_up, m_ffn2_w_down, v_ffn1_norm_g, v_ffn1_w_gate, v_ffn1_w_up, v_ffn1_w_down, v_mix_norm_g, v_w_in, v_gmlp_v_norm_g, v_gmlp_w_s, v_gmlp_b_s, v_mla_q_norm_g, v_mla_w_q_up, v_mla_kv_norm_g, v_mla_w_kv_up, v_mla_q_head_g, v_mla_k_head_g, v_gmlp_out_g, v_mla_out_g, v_w_out, v_ffn2_norm_g, v_ffn2_w_gate, v_ffn2_w_up, v_ffn2_w_down):
    wts = dict(zip(WEIGHTS, (ffn1_norm_g, ffn1_w_gate, ffn1_w_up, ffn1_w_down, mix_norm_g, w_in, gmlp_v_norm_g, gmlp_w_s, gmlp_b_s, mla_q_norm_g, mla_w_q_up, mla_kv_norm_g, mla_w_kv_up, mla_q_head_g, mla_k_head_g, gmlp_out_g, mla_out_g, w_out, ffn2_norm_g, ffn2_w_gate, ffn2_w_up, ffn2_w_down)))
    mom1 = dict(zip(WEIGHTS, (m_ffn1_norm_g, m_ffn1_w_gate, m_ffn1_w_up, m_ffn1_w_down, m_mix_norm_g, m_w_in, m_gmlp_v_norm_g, m_gmlp_w_s, m_gmlp_b_s, m_mla_q_norm_g, m_mla_w_q_up, m_mla_kv_norm_g, m_mla_w_kv_up, m_mla_q_head_g, m_mla_k_head_g, m_gmlp_out_g, m_mla_out_g, m_w_out, m_ffn2_norm_g, m_ffn2_w_gate, m_ffn2_w_up, m_ffn2_w_down)))
    mom2 = dict(zip(WEIGHTS, (v_ffn1_norm_g, v_ffn1_w_gate, v_ffn1_w_up, v_ffn1_w_down, v_mix_norm_g, v_w_in, v_gmlp_v_norm_g, v_gmlp_w_s, v_gmlp_b_s, v_mla_q_norm_g, v_mla_w_q_up, v_mla_kv_norm_g, v_mla_w_kv_up, v_mla_q_head_g, v_mla_k_head_g, v_gmlp_out_g, v_mla_out_g, v_w_out, v_ffn2_norm_g, v_ffn2_w_gate, v_ffn2_w_up, v_ffn2_w_down)))

    b_loc, seq, d = x.shape
    t = b_loc * seq
    ffs = ffn1_w_gate.shape[2]
    fp = _round_up(ffs, LANE)
    wg = gmlp_v_norm_g.shape[1]
    groups = gmlp_w_s.shape[1]
    rq, rkv = mla_q_norm_g.shape[1], mla_kv_norm_g.shape[1]
    heads = mla_out_g.shape[1]
    assert w_in.shape[2] * N_DEV == 2 * wg + rq + rkv + ROPE and mla_w_kv_up.shape[2] * N_DEV == heads * HEADW
    tm = min(512, t)
    tm_mix = min(256, t)
    blk = min(256, seq)

    xf = x.reshape(t, d)
    target = loss_target.reshape(t, d)
    pos = positions.reshape(t, 1).astype(F32)
    half = ROPE // 2
    inv_freq = 1.0 / (ROPE_THETA ** (jnp.arange(half, dtype=F32) / half))
    freq = jnp.concatenate([inv_freq, inv_freq, jnp.zeros((LANE - ROPE,), F32)])[None, :]
    lane = jnp.arange(LANE)
    masks = jnp.stack([jnp.where(lane < half, -1.0, 0.0),
                       jnp.where((lane >= half) & (lane < ROPE), 1.0, 0.0)]).astype(F32)
    gqh = jnp.pad(mla_q_head_g, ((0, 0), (0, HEADW - QK)))
    gkh = jnp.pad(mla_k_head_g, ((0, 0), (0, HEADW - QK)))
    bias = jnp.repeat(gmlp_b_s[0].T, CHUNK, axis=1)
    gouta = gmlp_out_g.reshape(1, wg)
    goutb = mla_out_g.reshape(1, heads * VHEAD)
    ws = gmlp_w_s[0]

    px, py, pc = _place()
    me = 4 * px + 2 * py + pc
    core = pc.astype(jnp.int32).reshape(1)
    slot = (2 * px + py).astype(jnp.int32).reshape(1)
    order = [n for n in WEIGHTS if n in SHARDED]
    group = {"ffn1": [n for n in order if n.startswith("ffn1")], "ffn2": [n for n in order if n.startswith("ffn2")],
             "mix": [n for n in order if not n.startswith("ffn")]}
    shard = {n: _shard_rows(wts[n], SHARDED[n], fp) for n in group["ffn1"]}

    def tied(arr, token):
        return arr + token[0, 0].astype(arr.dtype)

    xnb, ynb, dgn = 4 * (1 - px) + 2 * py, 4 * px + 2 * (1 - py), 4 * (1 - px) + 2 * (1 - py)
    ids_a = jnp.stack([me, 4 * px + 2 * py + (1 - pc)]).astype(jnp.int32)
    ids_b = jnp.stack([xnb, xnb + 1, ynb, ynb + 1]).astype(jnp.int32)
    ids_c = jnp.stack([dgn, dgn + 1]).astype(jnp.int32)
    g1 = _StagedGather([shard[n] for n in group["ffn1"]], me, "gather_ffn1")
    token = g1.start("own")
    token = g1.start("nbr", deps=(token,))
    for n in group["mix"] + group["ffn2"]:
        shard[n] = _shard_rows(tied(wts[n], token), SHARDED[n], fp if n.startswith("ffn") else None)
    g3 = _StagedGather([shard[n] for n in group["ffn2"]], me, "gather_ffn2")
    g1.wait("own", token)
    x1, xn1, kept1 = _ffn_fwd(xf, None, ffn1_norm_g, ids_a, *g1.lands(), None, tm, "ffn1_fwd_a")
    g1.wait("nbr", x1)
    token = g1.start("diag")
    ici2 = _gather_ici([shard[n] for n in group["mix"]], me, "gather_mix_ici", deps=(token,))
    token = g3.start("nbr", deps=(ici2.token,))
    token = g1.start("nbr_d2d", deps=(token,))
    g1.wait("nbr_d2d", token)
    x1, xn1, kept1 = _ffn_fwd(x1, xn1, None, ids_b, *g1.lands(), kept1, tm, "ffn1_fwd_b")
    g1.wait("diag", x1)
    token = g1.start("diag_d2d")
    g1.wait("diag_d2d", token)
    full = dict(zip(group["ffn1"], g1.lands()))
    x1, xn1, (gd1, sl1, h1) = _ffn_fwd(x1, xn1, None, ids_c, full["ffn1_w_gate"], full["ffn1_w_up"],
                                       full["ffn1_w_down"], kept1, tm, "ffn1_fwd_c")
    d2d2 = _gather_d2d(ici2.wait(x1)[len(group["mix"]):], "gather_mix_d2d")
    full.update(zip(group["mix"], d2d2.wait(d2d2.token)))
    win_t = jnp.pad(full["w_in"].reshape(-1, d), ((0, LANE - ROPE), (0, 0)))
    splits = (2 * wg, rq, rkv, LANE)
    wq_t = jnp.pad(full["mla_w_q_up"].reshape(heads, QK, rq), ((0, 0), (0, HEADW - QK), (0, 0)))
    wkv_t = full["mla_w_kv_up"].reshape(heads, HEADW, rkv)
    wout = full["w_out"].reshape(-1, d)
    hn, zuv, cq, ckv, krw = _inproj_fwd(x1, mix_norm_g, win_t, splits, tm)
    ya = _gmlp_fwd(zuv, gmlp_v_norm_g, ws, bias, gouta, tm_mix)
    g3.wait("nbr", ya)
    token = g3.start("diag")
    token = g3.start("own_nbr_d2d", deps=(token,))
    q, k, vv = _mla_proj_fwd(cq, ckv, krw, pos, freq, masks, mla_q_norm_g, mla_kv_norm_g, wq_t, wkv_t,
                             tied(gqh, token), gkh, tm_mix)
    o, lse = _attn_fwd(q, k, vv, seq, blk)
    g3.wait("diag", o)
    token = g3.start("diag_d2d")
    x2, ycat = _out_fwd(ya, o, tied(goutb, token), wout, x1, tm)
    g3.wait("own_nbr_d2d", x2)
    g3.wait("diag_d2d", x2)
    full.update(zip(group["ffn2"], g3.lands()))
    x3, xn2, (gd2, sl2, h2) = _ffn_fwd(x2, None, ffn2_norm_g, jnp.arange(N_DEV, dtype=jnp.int32),
                                       full["ffn2_w_gate"], full["ffn2_w_up"], full["ffn2_w_down"], None, tm,
                                       "ffn2_fwd")
    dx3, loss_part = _loss_head(x3, target, tm)

    outs_g, outs_d, outs_m, outs_v = {}, {}, {}, {}

    def finish(names, chip, got, after):
        for n, cp, gt in zip(names, chip, got):
            rows_of = (lambda a: a[0].T) if SHARDED[n] else (lambda a: a[0])
            res = _sum_adamw(cp, gt, slot, rows_of(wts[n]), rows_of(mom1[n]), rows_of(mom2[n]), "adamw_" + n, after)
            outs_g[n], outs_d[n], outs_m[n], outs_v[n] = [r.T[None] if SHARDED[n] else r[None] for r in res]

    def chip_sums(names, ex, after):
        res = ex.wait(after)
        return [_pair_add(f, gt, core, "pair_add_" + n) for n, f, gt in zip(names, res[:len(names)], res[len(names):])]

    tk = min(512, t)
    grads = {}
    small = {}
    dx2, small["ffn2_norm_g"], da2, db2 = _ffn_bwd(
        dx3, x2, ffn2_norm_g, gd2, sl2, full["ffn2_w_gate"], full["ffn2_w_up"], full["ffn2_w_down"], tm, "ffn2_bwd")
    grads["ffn2_w_gate"] = _matmul_tn(da2, xn2, fp, d, tk, BF16, "dw_ffn2_gate").reshape(N_DEV, fp, d)
    grads["ffn2_w_up"] = _matmul_tn(db2, xn2, fp, d, tk, BF16, "dw_ffn2_up").reshape(N_DEV, fp, d)
    grads["ffn2_w_down"] = _matmul_tn(h2, dx3, fp, d, tk, BF16, "dw_ffn2_down", rhs_scale=0.5).reshape(
        N_DEV, fp, d)
    red_a2 = _reduce_d2d([grads[n] for n in group["ffn2"]], "reduce_ffn2_d2d")
    dya, do, delta, small["mla_out_g"] = _out_bwd(dx2, o, tied(goutb, red_a2.token), wout, wg, tm)
    grads["w_out"] = _matmul_tn(ycat, dx2, _col_block(ycat.shape[1], 768), d, tk, BF16, "dw_out").reshape(
        N_DEV, -1, d)
    chip2 = chip_sums(group["ffn2"], red_a2, dya)
    red_b2 = _reduce_ici(chip2, "reduce_ffn2_ici")
    dq, dk, dv = _attn_bwd(q, k, vv, do, lse, delta, seq, blk, red_b2.token)
    (dcq, dckv, dkrw, dwq, dwkv, small["mla_q_norm_g"], small["mla_kv_norm_g"], dgqh, dgkh) = _mla_proj_bwd(
        dq, dk, dv, cq, ckv, krw, pos, freq, masks, mla_q_norm_g, mla_kv_norm_g, wq_t, wkv_t, gqh, gkh, tm_mix)
    small["mla_q_head_g"], small["mla_k_head_g"] = dgqh[:, :QK], dgkh[:, :QK]
    grads["mla_w_q_up"] = dwq[:, :QK].astype(BF16).reshape(N_DEV, -1, rq)
    grads["mla_w_kv_up"] = dwkv.astype(BF16).reshape(N_DEV, -1, rkv)
    dzuv, small["gmlp_w_s"], dbs, small["gmlp_v_norm_g"], small["gmlp_out_g"] = _gmlp_bwd(
        dya, zuv, gmlp_v_norm_g, ws, bias, gouta, tm_mix)
    small["gmlp_b_s"] = dbs[:, :, 0]
    dx1, small["mix_norm_g"], dzc = _inproj_bwd([dzuv, dcq, dckv, dkrw], x1, mix_norm_g, win_t, dx2, splits,
                                                tm_mix)
    dwin = _matmul_tn(dzc, hn, _col_block(dzc.shape[1], 768), d, tk, BF16, "dw_in")
    grads["w_in"] = dwin[:N_DEV * w_in.shape[2]].reshape(N_DEV, -1, d)
    res_b2 = red_b2.wait(grads["w_in"])
    red_am = _reduce_d2d([grads[n] for n in group["mix"]], "reduce_mix_d2d")
    dx0, small["ffn1_norm_g"], da1, db1 = _ffn_bwd(
        dx1, xf, tied(ffn1_norm_g, red_am.token), gd1, sl1, full["ffn1_w_gate"], full["ffn1_w_up"],
        full["ffn1_w_down"], tm, "ffn1_bwd")
    chipm = chip_sums(group["mix"], red_am, dx0)
    red_bm = _reduce_ici(chipm, "reduce_mix_ici")
    rep = [n for n in WEIGHTS if n not in SHARDED]
    small_ici = _gather_ici([_pack([small[n] for n in rep] + [loss_part])], me, "gather_small_ici",
                            deps=(red_bm.token,))
    operands = {"ffn1_w_gate": (da1, xn1, None), "ffn1_w_up": (db1, xn1, None), "ffn1_w_down": (h1, dx1, 0.5)}
    token, red_a, red_b, small_d2d = small_ici.token, None, [], None
    for n in group["ffn1"]:
        lhs, rhs, scale = operands[n]
        gr = _matmul_tn(lhs, rhs, fp, d, tk, BF16, "dw_" + n, rhs_scale=scale, deps=(token,)).reshape(
            N_DEV, fp, d)
        if red_a is not None:
            prev, ex = red_a
            red_b.append((prev, _reduce_ici(chip_sums([prev], ex, gr), "reduce_" + prev + "_ici")))
            token = red_b[-1][1].token
            if n == group["ffn1"][-1]:
                small_d2d = _gather_d2d(small_ici.wait(gr)[1:], "gather_small_d2d", deps=(token,))
                token = small_d2d.token
        red_a = (n, _reduce_d2d([gr], "reduce_" + n + "_d2d", deps=(token,)))
        token = red_a[1].token
    prev, ex = red_a
    red_b.append((prev, _reduce_ici(chip_sums([prev], ex, ex.token), "reduce_" + prev + "_ici")))
    token = red_b[-1][1].token
    finish(group["ffn2"], res_b2[:3], res_b2[3:], token)
    res = red_bm.wait(outs_g[group["ffn2"][-1]])
    nm_ = len(group["mix"])
    finish(group["mix"], res[:nm_], res[nm_:], token)
    total = _sum_devices(small_d2d.wait(outs_g[group["mix"][-1]])[0])
    zero = jnp.zeros((1,), F32)
    dlt, nm, nv = _adamw(_pack([wts[n] for n in rep] + [zero]), total, _pack([mom1[n] for n in rep] + [zero]),
                         _pack([mom2[n] for n in rep] + [zero]), "adamw_small")
    shapes = [wts[n].shape for n in rep] + [(1,)]
    for n, g, dl, m1, m2 in zip(rep, _unpack(total, shapes), _unpack(dlt, shapes), _unpack(nm, shapes),
                                _unpack(nv, shapes)):
        outs_g[n], outs_d[n], outs_m[n], outs_v[n] = g, dl, m1, m2
    loss = _unpack(total, shapes)[-1].reshape(())
    for n, ex in red_b:
        res = ex.wait(dlt)
        finish([n], res[:1], res[1:], token)

    return (loss, dx0.reshape(b_loc, seq, d), *[outs_g[n] for n in WEIGHTS], *[outs_d[n] for n in WEIGHTS],
            *[outs_m[n] for n in WEIGHTS], *[outs_v[n] for n in WEIGHTS])
```

```python
import functools

import jax
import jax.numpy as jnp
from jax import lax
from jax.experimental import pallas as pl
from jax.experimental.pallas import tpu as pltpu

F32 = jnp.float32
BF16 = jnp.bfloat16
EPS = 1e-6
LANE = 128
SUBLANE = 8
N_DEV = 8
VMEM_LIMIT = 60 * 1024 * 1024
NOPE = 128
ROPE = 64
VHEAD = 128
QK = NOPE + ROPE
HEADW = 2 * LANE
CHUNK = 128
ROPE_THETA = 10000.0
ADAM_LR, ADAM_B1, ADAM_B2, ADAM_EPS, ADAM_WD, ADAM_STEP = 0.001, 0.9, 0.999, 1e-08, 0.01, 10
MESH = pl.DeviceIdType.MESH
ANY = pl.BlockSpec(memory_space=pl.ANY)
WHOLE_VMEM = pl.BlockSpec(memory_space=pltpu.VMEM)


def _params(sem=None):
    return pltpu.CompilerParams(dimension_semantics=sem, vmem_limit_bytes=VMEM_LIMIT)


def _round_up(n, m):
    return -(-n // m) * m


def _row_block(rows, target):
    best = rows
    for cand in range(SUBLANE, min(rows, target) + 1, SUBLANE):
        if rows % cand == 0:
            best = cand
    return best if best <= target else rows


def _nn(a, b):
    return jnp.dot(a, b, preferred_element_type=F32)


def _nt(a, b):
    return lax.dot_general(a, b, (((1,), (1,)), ((), ())), preferred_element_type=F32)


def _tn(a, b):
    return lax.dot_general(a, b, (((0,), (0,)), ((), ())), preferred_element_type=F32)


def _rstd(x, n):
    return lax.rsqrt(jnp.sum(x * x, axis=-1, keepdims=True) * (1.0 / n) + EPS)


def _rms_fwd(x, g, n):
    return x * _rstd(x, n) * g


def _rms_bwd(x, g, dy, n):
    r = _rstd(x, n)
    xh = x * r
    dyg = dy * g
    dx = r * (dyg - xh * (jnp.sum(dyg * xh, axis=-1, keepdims=True) * (1.0 / n)))
    return dx, jnp.sum(dy * xh, axis=0, keepdims=True)


def _gelu(x):
    return 0.5 * x * (1.0 + lax.erf(x * 0.7071067811865476))


def _gelu_grad(x):
    return 0.5 * (1.0 + lax.erf(x * 0.7071067811865476)) + x * jnp.exp(-0.5 * x * x) * 0.3989422804014327


def _ffn_fwd(base, xn, g, ids, wg_t, wu_t, wd, saved, tm, name):
    t, d = base.shape
    nb, fp, _ = wg_t.shape
    n = ids.shape[0]
    first = xn is None
    if saved is None:
        saved = [lax.empty((t, nb * fp), BF16) for _ in range(3)]

    def body(ids_ref, *refs):
        if first:
            base_ref, g_ref, wg_ref, wu_ref, wd_ref, _, _, _, out_ref, xn_ref, gd_ref, sl_ref, h_ref, acc = refs
        else:
            base_ref, xn_ref, wg_ref, wu_ref, wd_ref, _, _, _, out_ref, gd_ref, sl_ref, h_ref, acc = refs
        j = pl.program_id(1)

        @pl.when(j == 0)
        def _():
            if first:
                xn_ref[...] = _rms_fwd(base_ref[...], g_ref[...], d).astype(BF16)
            acc[...] = jnp.zeros_like(acc)

        xnb = xn_ref[...]
        a = _nt(xnb, wg_ref[0])
        b = _nt(xnb, wu_ref[0])
        s = jax.nn.sigmoid(a)
        sl = a * s
        h = (sl * b).astype(BF16)
        gd_ref[...] = (b * (s * (1.0 + a * (1.0 - s)))).astype(BF16)
        sl_ref[...] = sl.astype(BF16)
        h_ref[...] = h
        acc[...] += _nn(h, wd_ref[0])

        @pl.when(j == n - 1)
        def _():
            out_ref[...] = base_ref[...] + 0.5 * acc[...]

    wspec = pl.BlockSpec((1, fp, d), lambda i, j, ids_ref: (ids_ref[j], 0, 0))
    row = pl.BlockSpec((tm, d), lambda i, j, ids_ref: (i, 0))
    ff = pl.BlockSpec((tm, fp), lambda i, j, ids_ref: (i, ids_ref[j]))
    ffs = jax.ShapeDtypeStruct((t, nb * fp), BF16)
    second = pl.BlockSpec((1, d), lambda i, j, ids_ref: (0, 0)) if first else row
    n_row_outs = 2 if first else 1
    res = pl.pallas_call(
        body, name=name,
        grid_spec=pltpu.PrefetchScalarGridSpec(
            num_scalar_prefetch=1, grid=(t // tm, n),
            in_specs=[row, second, wspec, wspec, wspec, ANY, ANY, ANY],
            out_specs=[row] * n_row_outs + [ff, ff, ff],
            scratch_shapes=[pltpu.VMEM((tm, d), F32)]),
        out_shape=[jax.ShapeDtypeStruct((t, d), F32)] + ([jax.ShapeDtypeStruct((t, d), BF16)] if first else [])
        + [ffs, ffs, ffs],
        input_output_aliases={6 + k: n_row_outs + k for k in range(3)},
        compiler_params=_params(("arbitrary", "arbitrary")),
    )(ids, base, g if first else xn, wg_t, wu_t, wd, *saved)
    return (res[0], res[1] if first else xn, list(res[n_row_outs:]))


def _ffn_bwd(dout, x, g, gd, sl, wg_t, wu_t, wd, tm, name):
    t, d = x.shape
    nb, fp, _ = wg_t.shape

    def body(do_hbm, x_hbm, g_ref, gd_ref, sl_ref, wg_ref, wu_ref, wd_ref, wd_next_ref,
             dx_hbm, dg_ref, da_ref, db_ref, acc, rowbuf, dy_scr, dh_scr, sem):
        i, j = pl.program_id(0), pl.program_id(1)
        rows = pl.ds(pl.multiple_of(i * tm, tm), tm)

        def fetch(src):
            cp = pltpu.make_async_copy(src.at[rows, :], rowbuf, sem)
            cp.start()
            cp.wait()

        @pl.when(j == 0)
        def _():
            fetch(do_hbm)
            dy_scr[...] = (0.5 * rowbuf[...]).astype(BF16)
            acc[...] = jnp.zeros_like(acc)
            dh_scr[0] = _nt(dy_scr[...], wd_ref[0])

        @pl.when((i == 0) & (j == 0))
        def _():
            dg_ref[...] = jnp.zeros_like(dg_ref)

        dh = dh_scr[j % 2]
        dh_scr[(j + 1) % 2] = _nt(dy_scr[...], wd_next_ref[0])
        da = (dh * gd_ref[...].astype(F32)).astype(BF16)
        db = (dh * sl_ref[...].astype(F32)).astype(BF16)
        da_ref[...] = da
        db_ref[...] = db
        acc[...] += _nn(da, wg_ref[0]) + _nn(db, wu_ref[0])

        @pl.when(j == nb - 1)
        def _():
            fetch(x_hbm)
            dxn, dg = _rms_bwd(rowbuf[...], g_ref[...], acc[...], d)
            dg_ref[...] += dg
            acc[...] = dxn
            fetch(do_hbm)
            acc[...] += rowbuf[...]
            out = pltpu.make_async_copy(acc, dx_hbm.at[rows, :], sem)
            out.start()
            out.wait()

    wspec = pl.BlockSpec((1, fp, d), lambda i, j: (j, 0, 0))
    wnext = pl.BlockSpec((1, fp, d), lambda i, j: (jnp.minimum(j + 1, nb - 1), 0, 0))
    vec = pl.BlockSpec((1, d), lambda i, j: (0, 0))
    ff = pl.BlockSpec((tm, fp), lambda i, j: (i, j))
    ffs = jax.ShapeDtypeStruct((t, nb * fp), BF16)
    return pl.pallas_call(
        body, name=name, grid=(t // tm, nb),
        in_specs=[ANY, ANY, vec, ff, ff, wspec, wspec, pl.BlockSpec((1, fp, d), lambda i, j: (0, 0, 0)), wnext],
        out_specs=[ANY, vec, ff, ff],
        out_shape=[jax.ShapeDtypeStruct((t, d), F32), jax.ShapeDtypeStruct((1, d), F32), ffs, ffs],
        scratch_shapes=[pltpu.VMEM((tm, d), F32), pltpu.VMEM((tm, d), F32), pltpu.VMEM((tm, d), BF16),
                        pltpu.VMEM((2, tm, fp), F32), pltpu.SemaphoreType.DMA],
        compiler_params=_params(("arbitrary", "arbitrary")),
    )(dout, x, g, gd, sl, wg_t, wu_t, wd, wd)


def _matmul_tn(lhs, rhs, bm, bn, tk, out_dtype, name, rhs_scale=None, deps=(), out_rows=None):
    t, m = lhs.shape
    n = rhs.shape[1]
    nk = t // tk
    out_rows = m if out_rows is None else out_rows

    def body(l_ref, r_ref, *refs):
        o_ref, acc = refs[len(deps):]
        k = pl.program_id(2)

        @pl.when(k == 0)
        def _():
            acc[...] = jnp.zeros_like(acc)

        r = r_ref[...] if rhs_scale is None else rhs_scale * r_ref[...]
        acc[...] += _tn(l_ref[...].astype(BF16), r.astype(BF16))

        @pl.when(k == nk - 1)
        def _():
            o_ref[...] = acc[...].astype(out_dtype)

    return pl.pallas_call(
        body, name=name, grid=(m // bm, n // bn, nk),
        in_specs=[pl.BlockSpec((tk, bm), lambda i, j, k: (k, i)), pl.BlockSpec((tk, bn), lambda i, j, k: (k, j))]
        + [ANY] * len(deps),
        out_specs=pl.BlockSpec((bm, bn), lambda i, j, k: (i, j)),
        out_shape=jax.ShapeDtypeStruct((out_rows, n), out_dtype),
        scratch_shapes=[pltpu.VMEM((bm, bn), F32)],
        compiler_params=_params(("arbitrary", "arbitrary", "arbitrary")),
    )(lhs, rhs, *deps)


def _last_rows_padded(w_ref, tail_ref, off, real):
    @pl.when(pl.program_id(0) == 0)
    def _():
        tail_ref[...] = jnp.zeros_like(tail_ref)
        tail_ref[0:real, :] = w_ref[off:off + real, :]


def _inproj_fwd(x, g, w_t, splits, tm):
    t, d = x.shape
    offs = [sum(splits[:k]) for k in range(len(splits))]
    real_last = w_t.shape[0] - offs[-1]

    def body(x_ref, g_ref, w_ref, hn_ref, *refs):
        z_refs, tail_ref = refs[:-1], refs[-1]
        _last_rows_padded(w_ref, tail_ref, offs[-1], real_last)
        hn = _rms_fwd(x_ref[...], g_ref[...], d).astype(BF16)
        hn_ref[...] = hn
        for z_ref, o, n in zip(z_refs[:-1], offs, splits):
            z_ref[...] = _nt(hn, w_ref[o:o + n, :])
        z_refs[-1][...] = _nt(hn, tail_ref[...])

    row = pl.BlockSpec((tm, d), lambda i: (i, 0))
    return pl.pallas_call(
        body, name="inproj_fwd", grid=(t // tm,),
        in_specs=[row, pl.BlockSpec((1, d), lambda i: (0, 0)), WHOLE_VMEM],
        out_specs=[row] + [pl.BlockSpec((tm, n), lambda i: (i, 0)) for n in splits],
        out_shape=[jax.ShapeDtypeStruct((t, d), BF16)] + [jax.ShapeDtypeStruct((t, n), F32) for n in splits],
        scratch_shapes=[pltpu.VMEM((splits[-1], d), BF16)],
        compiler_params=_params(("arbitrary",)),
    )(x, g, w_t)


def _inproj_bwd(dzs, x, g, w_t, dres, splits, tm):
    t, d = x.shape
    offs = [sum(splits[:k]) for k in range(len(splits))]
    ni = sum(splits)
    nz = len(splits)
    real_last = w_t.shape[0] - offs[-1]

    def body(*refs):
        dz_refs = refs[:nz]
        x_ref, g_ref, w_ref, dres_ref, dx_ref, dg_ref, dzc_ref, tail_ref = refs[nz:]
        _last_rows_padded(w_ref, tail_ref, offs[-1], real_last)
        dhn = jnp.zeros((tm, d), F32)
        for k, (dz_ref, o, n) in enumerate(zip(dz_refs, offs, splits)):
            dz = dz_ref[...].astype(BF16)
            dzc_ref[:, o:o + n] = dz
            dhn += _nn(dz, tail_ref[...] if k == nz - 1 else w_ref[o:o + n, :])
        dx, dg = _rms_bwd(x_ref[...], g_ref[...], dhn, d)
        dx_ref[...] = dres_ref[...] + dx

        @pl.when(pl.program_id(0) == 0)
        def _():
            dg_ref[...] = jnp.zeros_like(dg_ref)

        dg_ref[...] += dg

    row = pl.BlockSpec((tm, d), lambda i: (i, 0))
    vec = pl.BlockSpec((1, d), lambda i: (0, 0))
    return pl.pallas_call(
        body, name="inproj_bwd", grid=(t // tm,),
        in_specs=[pl.BlockSpec((tm, n), lambda i: (i, 0)) for n in splits] + [row, vec, WHOLE_VMEM, row],
        out_specs=[row, vec, pl.BlockSpec((tm, ni), lambda i: (i, 0))],
        out_shape=[jax.ShapeDtypeStruct((t, d), F32), jax.ShapeDtypeStruct((1, d), F32),
                   jax.ShapeDtypeStruct((t, ni), BF16)],
        scratch_shapes=[pltpu.VMEM((splits[-1], d), BF16)],
        compiler_params=_params(("arbitrary",)),
    )(*dzs, x, g, w_t, dres)


def _tril_bf16(ws_ref, grp):
    rows = lax.broadcasted_iota(jnp.int32, (CHUNK, CHUNK), 0)
    cols = lax.broadcasted_iota(jnp.int32, (CHUNK, CHUNK), 1)
    return jnp.where(rows >= cols, ws_ref[grp], 0.0).astype(BF16)


def _gmlp_mix(zuv_ref, gv_ref, ws_ref, bias_ref, v_scr, mixed_scr, tm, w, groups):
    u = _gelu(zuv_ref[:, 0:w])
    v0 = _gelu(zuv_ref[:, w:2 * w])
    v_scr[...] = _rms_fwd(v0, gv_ref[...], w).astype(BF16)
    for grp in range(groups):
        wsm = _tril_bf16(ws_ref, grp)
        lanes = slice(grp * CHUNK, (grp + 1) * CHUNK)
        for c in range(tm // CHUNK):
            rows = slice(c * CHUNK, (c + 1) * CHUNK)
            mixed_scr[rows, lanes] = _nn(wsm, v_scr[rows, lanes]) + bias_ref[:, lanes]
    return u, v0


def _gmlp_fwd(zuv, gv, ws, bias, gout, tm):
    t, w2 = zuv.shape
    w = w2 // 2
    groups = ws.shape[0]

    def body(zuv_ref, gv_ref, ws_ref, bias_ref, go_ref, y_ref, v_scr, mixed_scr):
        u, _ = _gmlp_mix(zuv_ref, gv_ref, ws_ref, bias_ref, v_scr, mixed_scr, tm, w, groups)
        ya = u * mixed_scr[...]
        for grp in range(groups):
            lanes = slice(grp * CHUNK, (grp + 1) * CHUNK)
            y_ref[:, lanes] = _rms_fwd(ya[:, lanes], go_ref[:, lanes], CHUNK).astype(BF16)

    const2 = lambda i: (0, 0)
    return pl.pallas_call(
        body, name="gmlp_fwd", grid=(t // tm,),
        in_specs=[pl.BlockSpec((tm, w2), lambda i: (i, 0)), pl.BlockSpec((1, w), const2),
                  pl.BlockSpec((groups, CHUNK, CHUNK), lambda i: (0, 0, 0)),
                  pl.BlockSpec((CHUNK, w), const2), pl.BlockSpec((1, w), const2)],
        out_specs=pl.BlockSpec((tm, w), lambda i: (i, 0)),
        out_shape=jax.ShapeDtypeStruct((t, w), BF16),
        scratch_shapes=[pltpu.VMEM((tm, w), BF16), pltpu.VMEM((tm, w), F32)],
        compiler_params=_params(("arbitrary",)),
    )(zuv, gv, ws, bias, gout)


def _gmlp_bwd(dy, zuv, gv, ws, bias, gout, tm):
    t, w2 = zuv.shape
    w = w2 // 2
    groups = ws.shape[0]

    def body(dy_ref, zuv_ref, gv_ref, ws_ref, bias_ref, go_ref,
             dz_ref, dws_ref, dbias_ref, dgv_ref, dgo_ref, v_scr, mixed_scr, dmix_scr, dv_scr):
        @pl.when(pl.program_id(0) == 0)
        def _():
            dws_ref[...] = jnp.zeros_like(dws_ref)
            dbias_ref[...] = jnp.zeros_like(dbias_ref)
            dgv_ref[...] = jnp.zeros_like(dgv_ref)
            dgo_ref[...] = jnp.zeros_like(dgo_ref)

        u, v0 = _gmlp_mix(zuv_ref, gv_ref, ws_ref, bias_ref, v_scr, mixed_scr, tm, w, groups)
        mixed = mixed_scr[...]
        ya = u * mixed
        for grp in range(groups):
            lanes = slice(grp * CHUNK, (grp + 1) * CHUNK)
            dya, dgo = _rms_bwd(ya[:, lanes], go_ref[:, lanes], dy_ref[:, lanes], CHUNK)
            dgo_ref[:, lanes] += dgo
            dz_ref[:, lanes] = dya * mixed[:, lanes] * _gelu_grad(zuv_ref[:, lanes])
            dmix_scr[:, lanes] = dya * u[:, lanes]
        for grp in range(groups):
            wsm = _tril_bf16(ws_ref, grp)
            lanes = slice(grp * CHUNK, (grp + 1) * CHUNK)
            dws = jnp.zeros((CHUNK, CHUNK), F32)
            dbias = jnp.zeros((CHUNK, CHUNK), F32)
            for c in range(tm // CHUNK):
                rows = slice(c * CHUNK, (c + 1) * CHUNK)
                dm = dmix_scr[rows, lanes]
                dmb = dm.astype(BF16)
                dv_scr[rows, lanes] = _tn(wsm, dmb)
                dws += _nt(dmb, v_scr[rows, lanes])
                dbias += dm
            rr = lax.broadcasted_iota(jnp.int32, (CHUNK, CHUNK), 0)
            cc = lax.broadcasted_iota(jnp.int32, (CHUNK, CHUNK), 1)
            dws_ref[grp] += jnp.where(rr >= cc, dws, 0.0)
            dbias_ref[grp] += jnp.sum(dbias, axis=1, keepdims=True)
        dv0, dgv = _rms_bwd(v0, gv_ref[...], dv_scr[...], w)
        dgv_ref[...] += dgv
        dz_ref[:, w:2 * w] = dv0 * _gelu_grad(zuv_ref[:, w:2 * w])

    const2 = lambda i: (0, 0)
    const3 = lambda i: (0, 0, 0)
    return pl.pallas_call(
        body, name="gmlp_bwd", grid=(t // tm,),
        in_specs=[pl.BlockSpec((tm, w), lambda i: (i, 0)), pl.BlockSpec((tm, w2), lambda i: (i, 0)),
                  pl.BlockSpec((1, w), const2), pl.BlockSpec((groups, CHUNK, CHUNK), const3),
                  pl.BlockSpec((CHUNK, w), const2), pl.BlockSpec((1, w), const2)],
        out_specs=[pl.BlockSpec((tm, w2), lambda i: (i, 0)), pl.BlockSpec((groups, CHUNK, CHUNK), const3),
                   pl.BlockSpec((groups, CHUNK, 1), const3), pl.BlockSpec((1, w), const2), pl.BlockSpec((1, w), const2)],
        out_shape=[jax.ShapeDtypeStruct((t, w2), F32), jax.ShapeDtypeStruct((groups, CHUNK, CHUNK), F32),
                   jax.ShapeDtypeStruct((groups, CHUNK, 1), F32), jax.ShapeDtypeStruct((1, w), F32),
                   jax.ShapeDtypeStruct((1, w), F32)],
        scratch_shapes=[pltpu.VMEM((tm, w), BF16), pltpu.VMEM((tm, w), F32),
                        pltpu.VMEM((tm, w), F32), pltpu.VMEM((tm, w), F32)],
        compiler_params=_params(("arbitrary",)),
    )(dy, zuv, gv, ws, bias, gout)


def _rot(x, m_lo, m_hi):
    return pltpu.roll(x, LANE - ROPE // 2, 1) * m_lo + pltpu.roll(x, ROPE // 2, 1) * m_hi


def _rope_tables(pos_ref, freq_ref):
    ang = pos_ref[...] * freq_ref[...]
    return jnp.cos(ang), jnp.sin(ang)


def _mla_proj_fwd(cq, ckv, krw, pos, freq, masks, gq, gkv, wq_t, wkv_t, gqh, gkh, tm):
    t, rq = cq.shape
    rkv = ckv.shape[1]
    heads = wq_t.shape[0]

    def body(cq_ref, ckv_ref, kr_ref, pos_ref, freq_ref, mk_ref, gq_ref, gkv_ref, wq_ref, wkv_ref,
             gqh_ref, gkh_ref, q_ref, k_ref, v_ref):
        cos, sin = _rope_tables(pos_ref, freq_ref)
        m_lo, m_hi = mk_ref[0:1, :], mk_ref[1:2, :]
        cqn = _rms_fwd(cq_ref[...], gq_ref[...], rq).astype(BF16)
        ckvn = _rms_fwd(ckv_ref[...], gkv_ref[...], rkv).astype(BF16)
        kr = kr_ref[...]
        kr_ss = jnp.sum(kr * kr, axis=-1, keepdims=True)
        for h in range(heads):
            qh = _nt(cqn, wq_ref[h])
            qn = qh * _rstd(qh, QK) * gqh_ref[...]
            qr = qn[:, LANE:]
            q_ref[h, :, 0:LANE] = qn[:, 0:LANE].astype(BF16)
            q_ref[h, :, LANE:] = (qr * cos + _rot(qr, m_lo, m_hi) * sin).astype(BF16)
            kvh = _nt(ckvn, wkv_ref[h])
            kn = kvh[:, 0:LANE]
            rk = lax.rsqrt((jnp.sum(kn * kn, axis=-1, keepdims=True) + kr_ss) * (1.0 / QK) + EPS)
            k_ref[h, :, 0:LANE] = (kn * rk * gkh_ref[:, 0:LANE]).astype(BF16)
            krn = kr * rk * gkh_ref[:, LANE:]
            k_ref[h, :, LANE:] = (krn * cos + _rot(krn, m_lo, m_hi) * sin).astype(BF16)
            v_ref[h] = kvh[:, LANE:].astype(BF16)

    c2 = lambda i: (0, 0)
    c3 = lambda i: (0, 0, 0)
    return pl.pallas_call(
        body, name="mla_proj_fwd", grid=(t // tm,),
        in_specs=[pl.BlockSpec((tm, rq), lambda i: (i, 0)), pl.BlockSpec((tm, rkv), lambda i: (i, 0)),
                  pl.BlockSpec((tm, LANE), lambda i: (i, 0)), pl.BlockSpec((tm, 1), lambda i: (i, 0)),
                  pl.BlockSpec((1, LANE), c2), pl.BlockSpec((2, LANE), c2),
                  pl.BlockSpec((1, rq), c2), pl.BlockSpec((1, rkv), c2),
                  pl.BlockSpec((heads, HEADW, rq), c3), pl.BlockSpec((heads, HEADW, rkv), c3),
                  pl.BlockSpec((1, HEADW), c2), pl.BlockSpec((1, HEADW), c2)],
        out_specs=[pl.BlockSpec((heads, tm, HEADW), lambda i: (0, i, 0)),
                   pl.BlockSpec((heads, tm, HEADW), lambda i: (0, i, 0)),
                   pl.BlockSpec((heads, tm, VHEAD), lambda i: (0, i, 0))],
        out_shape=[jax.ShapeDtypeStruct((heads, t, HEADW), BF16), jax.ShapeDtypeStruct((heads, t, HEADW), BF16),
                   jax.ShapeDtypeStruct((heads, t, VHEAD), BF16)],
        compiler_params=_params(("arbitrary",)),
    )(cq, ckv, krw, pos, freq, masks, gq, gkv, wq_t, wkv_t, gqh, gkh)


def _mla_proj_bwd(dq, dk, dv, cq, ckv, krw, pos, freq, masks, gq, gkv, wq_t, wkv_t, gqh, gkh, tm):
    t, rq = cq.shape
    rkv = ckv.shape[1]
    heads = wq_t.shape[0]

    def body(dq_ref, dk_ref, dv_ref, cq_ref, ckv_ref, kr_ref, pos_ref, freq_ref, mk_ref, gq_ref, gkv_ref,
             wq_ref, wkv_ref, gqh_ref, gkh_ref,
             dcq_ref, dckv_ref, dkr_ref, dwq_ref, dwkv_ref, dgq_ref, dgkv_ref, dgqh_ref, dgkh_ref):
        @pl.when(pl.program_id(0) == 0)
        def _():
            for r in (dwq_ref, dwkv_ref, dgq_ref, dgkv_ref, dgqh_ref, dgkh_ref):
                r[...] = jnp.zeros_like(r)

        cos, sin = _rope_tables(pos_ref, freq_ref)
        m_lo, m_hi = mk_ref[0:1, :], mk_ref[1:2, :]

        def unrope(dy):
            return dy * cos - _rot(dy * sin, m_lo, m_hi)

        cqn = _rms_fwd(cq_ref[...], gq_ref[...], rq).astype(BF16)
        ckvn = _rms_fwd(ckv_ref[...], gkv_ref[...], rkv).astype(BF16)
        kr = kr_ref[...]
        kr_ss = jnp.sum(kr * kr, axis=-1, keepdims=True)
        dcqn = jnp.zeros((tm, rq), F32)
        dckvn = jnp.zeros((tm, rkv), F32)
        dkr = jnp.zeros((tm, LANE), F32)
        for h in range(heads):
            qh = _nt(cqn, wq_ref[h])
            dqn = jnp.concatenate([dq_ref[h, :, 0:LANE], unrope(dq_ref[h, :, LANE:])], axis=1)
            dqh, dg = _rms_bwd(qh, gqh_ref[...], dqn, QK)
            dgqh_ref[...] += dg
            dqh = dqh.astype(BF16)
            dcqn += _nn(dqh, wq_ref[h])
            dwq_ref[h] += _tn(dqh, cqn)

            kvh = _nt(ckvn, wkv_ref[h])
            kn = kvh[:, 0:LANE]
            rk = lax.rsqrt((jnp.sum(kn * kn, axis=-1, keepdims=True) + kr_ss) * (1.0 / QK) + EPS)
            dkn_n = dk_ref[h, :, 0:LANE]
            dkr_n = unrope(dk_ref[h, :, LANE:])
            knh, krh = kn * rk, kr * rk
            dgkh_ref[:, 0:LANE] += jnp.sum(dkn_n * knh, axis=0, keepdims=True)
            dgkh_ref[:, LANE:] += jnp.sum(dkr_n * krh, axis=0, keepdims=True)
            dkn_g, dkr_g = dkn_n * gkh_ref[:, 0:LANE], dkr_n * gkh_ref[:, LANE:]
            proj = (jnp.sum(dkn_g * knh, axis=-1, keepdims=True)
                    + jnp.sum(dkr_g * krh, axis=-1, keepdims=True)) * (1.0 / QK)
            dkr += rk * (dkr_g - krh * proj)
            dkvh = jnp.concatenate([rk * (dkn_g - knh * proj), dv_ref[h]], axis=1).astype(BF16)
            dckvn += _nn(dkvh, wkv_ref[h])
            dwkv_ref[h] += _tn(dkvh, ckvn)
        dkr_ref[...] = dkr
        dcq, dg = _rms_bwd(cq_ref[...], gq_ref[...], dcqn, rq)
        dcq_ref[...] = dcq
        dgq_ref[...] += dg
        dckv, dg = _rms_bwd(ckv_ref[...], gkv_ref[...], dckvn, rkv)
        dckv_ref[...] = dckv
        dgkv_ref[...] += dg

    c2 = lambda i: (0, 0)
    c3 = lambda i: (0, 0, 0)
    hq = pl.BlockSpec((heads, tm, HEADW), lambda i: (0, i, 0))
    return pl.pallas_call(
        body, name="mla_proj_bwd", grid=(t // tm,),
        in_specs=[hq, hq, pl.BlockSpec((heads, tm, VHEAD), lambda i: (0, i, 0)),
                  pl.BlockSpec((tm, rq), lambda i: (i, 0)), pl.BlockSpec((tm, rkv), lambda i: (i, 0)),
                  pl.BlockSpec((tm, LANE), lambda i: (i, 0)), pl.BlockSpec((tm, 1), lambda i: (i, 0)),
                  pl.BlockSpec((1, LANE), c2), pl.BlockSpec((2, LANE), c2),
                  pl.BlockSpec((1, rq), c2), pl.BlockSpec((1, rkv), c2),
                  pl.BlockSpec((heads, HEADW, rq), c3), pl.BlockSpec((heads, HEADW, rkv), c3),
                  pl.BlockSpec((1, HEADW), c2), pl.BlockSpec((1, HEADW), c2)],
        out_specs=[pl.BlockSpec((tm, rq), lambda i: (i, 0)), pl.BlockSpec((tm, rkv), lambda i: (i, 0)),
                   pl.BlockSpec((tm, LANE), lambda i: (i, 0)),
                   pl.BlockSpec((heads, HEADW, rq), c3), pl.BlockSpec((heads, HEADW, rkv), c3),
                   pl.BlockSpec((1, rq), c2), pl.BlockSpec((1, rkv), c2),
                   pl.BlockSpec((1, HEADW), c2), pl.BlockSpec((1, HEADW), c2)],
        out_shape=[jax.ShapeDtypeStruct((t, rq), F32), jax.ShapeDtypeStruct((t, rkv), F32),
                   jax.ShapeDtypeStruct((t, LANE), F32),
                   jax.ShapeDtypeStruct((heads, HEADW, rq), F32), jax.ShapeDtypeStruct((heads, HEADW, rkv), F32),
                   jax.ShapeDtypeStruct((1, rq), F32), jax.ShapeDtypeStruct((1, rkv), F32),
                   jax.ShapeDtypeStruct((1, HEADW), F32), jax.ShapeDtypeStruct((1, HEADW), F32)],
        compiler_params=_params(("arbitrary",)),
    )(dq, dk, dv, cq, ckv, krw, pos, freq, masks, gq, gkv, wq_t, wkv_t, gqh, gkh)


def _lower_triangle(blk):
    return lax.broadcasted_iota(jnp.int32, (blk, blk), 0) >= lax.broadcasted_iota(jnp.int32, (blk, blk), 1)


def _attn_fwd(q, k, v, seq, blk):
    heads, t, _ = q.shape
    scale = QK ** -0.5
    nblk = seq // blk

    def body(q_ref, k_ref, v_ref, o_ref, lse_ref):
        tri = _lower_triangle(blk)
        for qi in range(nblk):
            rows = slice(qi * blk, (qi + 1) * blk)
            before = slice(0, qi * blk)
            qb = q_ref[0, rows, :]
            s_d = jnp.where(tri, _nt(qb, k_ref[0, rows, :]) * scale, -1e30)
            m = jnp.max(s_d, axis=-1, keepdims=True)
            if qi:
                s_b = _nt(qb, k_ref[0, before, :]) * scale
                m = jnp.maximum(m, jnp.max(s_b, axis=-1, keepdims=True))
                p_b = jnp.exp(s_b - m)
            p_d = jnp.exp(s_d - m)
            l = jnp.sum(p_d, axis=-1, keepdims=True)
            acc = _nn(p_d.astype(BF16), v_ref[0, rows, :])
            if qi:
                l += jnp.sum(p_b, axis=-1, keepdims=True)
                acc += _nn(p_b.astype(BF16), v_ref[0, before, :])
            o_ref[0, rows, :] = acc / l
            lse_ref[0, rows, :] = m + jnp.log(l)

    return pl.pallas_call(
        body, name="attn_fwd", grid=(heads, t // seq),
        in_specs=[pl.BlockSpec((1, seq, HEADW), lambda h, b: (h, b, 0)),
                  pl.BlockSpec((1, seq, HEADW), lambda h, b: (h, b, 0)),
                  pl.BlockSpec((1, seq, VHEAD), lambda h, b: (h, b, 0))],
        out_specs=[pl.BlockSpec((1, seq, VHEAD), lambda h, b: (h, b, 0)),
                   pl.BlockSpec((1, seq, 1), lambda h, b: (h, b, 0))],
        out_shape=[jax.ShapeDtypeStruct((heads, t, VHEAD), F32), jax.ShapeDtypeStruct((heads, t, 1), F32)],
        compiler_params=_params(("arbitrary", "arbitrary")),
    )(q, k, v)


def _attn_bwd(q, k, v, do, lse, delta, seq, blk, after):
    heads, t, _ = q.shape
    scale = QK ** -0.5
    nblk = seq // blk

    def body(q_ref, k_ref, v_ref, do_ref, lse_ref, dl_ref, _, dq_ref, dk_ref, dv_ref):
        tri = _lower_triangle(blk)
        dk_ref[...] = jnp.zeros_like(dk_ref)
        dv_ref[...] = jnp.zeros_like(dv_ref)
        for qi in range(nblk):
            rows = slice(qi * blk, (qi + 1) * blk)
            qb = q_ref[0, rows, :]
            dob = do_ref[0, rows, :]
            lse_b = lse_ref[0, rows, :]
            dl_b = dl_ref[0, rows, :]
            dq = jnp.zeros((blk, HEADW), F32)
            for keys, masked in ((slice(0, qi * blk), False), (rows, True)):
                if keys.stop == keys.start:
                    continue
                kb = k_ref[0, keys, :]
                p = jnp.exp(_nt(qb, kb) * scale - lse_b)
                if masked:
                    p = jnp.where(tri, p, 0.0)
                dp = _nt(dob, v_ref[0, keys, :])
                ds = (p * (dp - dl_b) * scale).astype(BF16)
                dv_ref[0, keys, :] += _tn(p.astype(BF16), dob)
                dk_ref[0, keys, :] += _tn(ds, qb)
                dq += _nn(ds, kb)
            dq_ref[0, rows, :] = dq

    hq = pl.BlockSpec((1, seq, HEADW), lambda h, b: (h, b, 0))
    hv = pl.BlockSpec((1, seq, VHEAD), lambda h, b: (h, b, 0))
    h1 = pl.BlockSpec((1, seq, 1), lambda h, b: (h, b, 0))
    return pl.pallas_call(
        body, name="attn_bwd", grid=(heads, t // seq),
        in_specs=[hq, hq, hv, hv, h1, h1, ANY],
        out_specs=[hq, hq, hv],
        out_shape=[jax.ShapeDtypeStruct((heads, t, HEADW), F32), jax.ShapeDtypeStruct((heads, t, HEADW), F32),
                   jax.ShapeDtypeStruct((heads, t, VHEAD), F32)],
        compiler_params=_params(("arbitrary", "arbitrary")),
    )(q, k, v, do, lse, delta, after)


def _out_fwd(ya, o, gb, w_out, x1, tm):
    t, w = ya.shape
    heads = o.shape[0]
    d = x1.shape[1]

    def body(ya_ref, o_ref, gb_ref, w_ref, x1_ref, x2_ref, yc_ref):
        yc_ref[:, 0:w] = ya_ref[...]
        for h in range(heads):
            lanes = slice(h * VHEAD, (h + 1) * VHEAD)
            yc_ref[:, w + h * VHEAD:w + (h + 1) * VHEAD] = _rms_fwd(o_ref[h], gb_ref[:, lanes], VHEAD).astype(BF16)
        x2_ref[...] = x1_ref[...] + _nn(yc_ref[...], w_ref[...])

    wy = w + heads * VHEAD
    row = pl.BlockSpec((tm, d), lambda i: (i, 0))
    return pl.pallas_call(
        body, name="out_fwd", grid=(t // tm,),
        in_specs=[pl.BlockSpec((tm, w), lambda i: (i, 0)), pl.BlockSpec((heads, tm, VHEAD), lambda i: (0, i, 0)),
                  pl.BlockSpec((1, heads * VHEAD), lambda i: (0, 0)), WHOLE_VMEM, row],
        out_specs=[row, pl.BlockSpec((tm, wy), lambda i: (i, 0))],
        out_shape=[jax.ShapeDtypeStruct((t, d), F32), jax.ShapeDtypeStruct((t, wy), BF16)],
        compiler_params=_params(("arbitrary",)),
    )(ya, o, gb, w_out, x1)


def _out_bwd(dx2, o, gb, w_out, w, tm):
    t, d = dx2.shape
    heads = o.shape[0]

    def body(dx_ref, o_ref, gb_ref, w_ref, dya_ref, do_ref, dl_ref, dgb_ref):
        @pl.when(pl.program_id(0) == 0)
        def _():
            dgb_ref[...] = jnp.zeros_like(dgb_ref)

        dyc = _nt(dx_ref[...].astype(BF16), w_ref[...])
        dya_ref[...] = dyc[:, 0:w]
        for h in range(heads):
            lanes = slice(h * VHEAD, (h + 1) * VHEAD)
            oh = o_ref[h]
            doh, dg = _rms_bwd(oh, gb_ref[:, lanes], dyc[:, w + h * VHEAD:w + (h + 1) * VHEAD], VHEAD)
            dgb_ref[:, lanes] += dg
            do_ref[h] = doh.astype(BF16)
            dl_ref[h] = jnp.sum(doh * oh, axis=-1, keepdims=True)

    ho = pl.BlockSpec((heads, tm, VHEAD), lambda i: (0, i, 0))
    vec = pl.BlockSpec((1, heads * VHEAD), lambda i: (0, 0))
    return pl.pallas_call(
        body, name="out_bwd", grid=(t // tm,),
        in_specs=[pl.BlockSpec((tm, d), lambda i: (i, 0)), ho, vec, WHOLE_VMEM],
        out_specs=[pl.BlockSpec((tm, w), lambda i: (i, 0)), ho, pl.BlockSpec((heads, tm, 1), lambda i: (0, i, 0)), vec],
        out_shape=[jax.ShapeDtypeStruct((t, w), F32), jax.ShapeDtypeStruct((heads, t, VHEAD), BF16),
                   jax.ShapeDtypeStruct((heads, t, 1), F32), jax.ShapeDtypeStruct((1, heads * VHEAD), F32)],
        compiler_params=_params(("arbitrary",)),
    )(dx2, o, gb, w_out)


def _loss_head(y, target, tm):
    t, d = y.shape

    def body(y_ref, t_ref, dy_ref, loss_ref):
        @pl.when(pl.program_id(0) == 0)
        def _():
            loss_ref[...] = jnp.zeros_like(loss_ref)

        err = y_ref[...] - t_ref[...]
        dy_ref[...] = err * (1.0 / d)
        part = jnp.sum(jnp.sum(err * err, axis=-1, keepdims=True) * (1.0 / d), axis=0, keepdims=True)
        loss_ref[...] += 0.5 * part

    row = pl.BlockSpec((tm, d), lambda i: (i, 0))
    return pl.pallas_call(
        body, name="loss_head", grid=(t // tm,),
        in_specs=[row, row], out_specs=[row, pl.BlockSpec((1, 1), lambda i: (0, 0))],
        out_shape=[jax.ShapeDtypeStruct((t, d), F32), jax.ShapeDtypeStruct((1, 1), F32)],
        compiler_params=_params(("arbitrary",)),
    )(y, target)


def _place():
    return lax.axis_index("x"), lax.axis_index("y"), lax.axis_index("c")


HBM = pl.BlockSpec(memory_space=pltpu.HBM)
SEM = pl.BlockSpec(memory_space=pltpu.SEMAPHORE)
DATAFLOW = pltpu.SideEffectType.DATAFLOW_SIDE_EFFECTING


def _plan_copies(plan, refs, send_sems, recv_sems):
    cps = []
    for i, (sb, sblk, db, dblk, dev) in enumerate(plan(*_place())):
        cps.append(pltpu.make_async_remote_copy(
            src_ref=refs[sb] if sblk is None else refs[sb].at[sblk], dst_ref=refs[db].at[dblk],
            send_sem=send_sems.at[i], recv_sem=recv_sems.at[i], device_id=dev, device_id_type=MESH))
    return cps


def _push_start(bufs, plan, ncopy, name, deps=()):
    nb = len(bufs)

    def body(*refs):
        outs = refs[nb + len(deps):]
        for cp in _plan_copies(plan, refs[:nb], outs[0], outs[1]):
            cp.start()
        outs[-1][...] = jnp.zeros_like(outs[-1])

    res = pl.pallas_call(
        body, name=name,
        out_shape=(pltpu.SemaphoreType.DMA((ncopy,)), pltpu.SemaphoreType.DMA((ncopy,)),
                   *[pltpu.HBM(b.shape, b.dtype) for b in bufs], jax.ShapeDtypeStruct((SUBLANE, LANE), F32)),
        in_specs=[HBM] * nb + [ANY] * len(deps),
        out_specs=(SEM, SEM, *[HBM] * nb, WHOLE_VMEM),
        input_output_aliases={i: 2 + i for i in range(nb)},
        compiler_params=pltpu.CompilerParams(has_side_effects=DATAFLOW),
    )(*[pltpu.with_memory_space_constraint(b, pltpu.HBM) for b in bufs], *deps)
    return res[0], res[1], list(res[2:2 + nb]), res[-1]


def _push_wait(send_sems, recv_sems, bufs, plan, after, name):
    nb = len(bufs)

    def body(*refs):
        for cp in _plan_copies(plan, refs[:nb], refs[nb], refs[nb + 1]):
            cp.wait_send()
            cp.wait_recv()

    res = pl.pallas_call(
        body, name=name,
        out_shape=[pltpu.HBM(b.shape, b.dtype) for b in bufs],
        in_specs=[HBM] * nb + [SEM, SEM, ANY], out_specs=[HBM] * nb,
        input_output_aliases={i: i for i in range(nb)},
        compiler_params=pltpu.CompilerParams(has_side_effects=DATAFLOW),
    )(*bufs, send_sems, recv_sems, after)
    return list(res)


def _other_chips(x, y):
    return ((1 - x, y), (x, 1 - y), (1 - x, 1 - y))


class _Exchange:
    def __init__(self, bufs, plan, ncopy, name, deps=()):
        self.plan, self.name = plan, name
        self.send, self.recv, self.bufs, self.token = _push_start(bufs, plan, ncopy, name + "_start", deps)

    def wait(self, after):
        return _push_wait(self.send, self.recv, self.bufs, self.plan, after, self.name + "_wait")


class _Chain:
    def __init__(self, bufs):
        self.bufs = list(bufs)

    def start(self, plan, ncopy, name, deps=()):
        send, recv, self.bufs, token = _push_start(self.bufs, plan, ncopy, name + "_start", deps)
        return (send, recv, plan, name), token

    def wait(self, pending, after):
        send, recv, plan, name = pending
        self.bufs = _push_wait(send, recv, self.bufs, plan, after, name + "_wait")


class _StagedGather:
    def __init__(self, shards, me, name):
        self.n = n = len(shards)
        self.name = name
        self.chain = _Chain(list(shards) + [
            lax.dynamic_update_slice(lax.empty((N_DEV,) + s.shape, s.dtype), s[None], (me, 0, 0)) for s in shards])
        self.pending = {}

        def to_sibling(blocks):
            return lambda x, y, c: [(n + a, b, n + a, b, (x, y, 1 - c)) for a in range(n) for b in blocks(x, y, c)]

        def nbr_blocks(x, y, c):
            return [4 * (1 - x) + 2 * y + c, 4 * x + 2 * (1 - y) + c]

        def diag(x, y, c):
            sx, sy = (1 - x) * (1 - c) + x * c, y * (1 - c) + (1 - y) * c
            tx, ty = x * (1 - c) + (1 - x) * c, (1 - y) * (1 - c) + y * c
            return [(n + a, 4 * sx + 2 * sy + c, n + a, 4 * sx + 2 * sy + c, (tx, ty, c)) for a in range(n)]

        self.plans = {
            "own": (lambda x, y, c: [(a, None, n + a, 4 * x + 2 * y + c, (x, y, 1 - c)) for a in range(n)], n),
            "nbr": (lambda x, y, c: [(a, None, n + a, 4 * x + 2 * y + c, dev) for a in range(n)
                                     for dev in ((1 - x, y, c), (x, 1 - y, c))], 2 * n),
            "diag": (diag, n),
            "nbr_d2d": (to_sibling(nbr_blocks), 2 * n),
            "own_nbr_d2d": (to_sibling(lambda x, y, c: [4 * x + 2 * y + c] + nbr_blocks(x, y, c)), 3 * n),
            "diag_d2d": (to_sibling(lambda x, y, c: [4 * (1 - x) + 2 * (1 - y) + c]), n),
        }

    def start(self, stage, deps=()):
        plan, ncopy = self.plans[stage]
        self.pending[stage], token = self.chain.start(plan, ncopy, self.name + "_" + stage, deps)
        return token

    def wait(self, stage, after):
        self.chain.wait(self.pending.pop(stage), after)

    def lands(self):
        return self.chain.bufs[self.n:]


def _gather_ici(shards, me, name, deps=()):
    n = len(shards)
    lands = [lax.dynamic_update_slice(lax.empty((N_DEV,) + s.shape, s.dtype), s[None], (me, 0, 0)) for s in shards]

    def plan(x, y, c):
        return [(a, None, n + a, 4 * x + 2 * y + c, (px, py, c)) for a in range(n) for px, py in _other_chips(x, y)]

    return _Exchange(list(shards) + lands, plan, 3 * n, name, deps)


def _gather_d2d(lands, name, deps=()):
    n = len(lands)

    def plan(x, y, c):
        blocks = [4 * x + 2 * y + c] + [4 * px + 2 * py + c for px, py in _other_chips(x, y)]
        return [(a, b, a, b, (x, y, 1 - c)) for a in range(n) for b in blocks]

    return _Exchange(list(lands), plan, 4 * n, name, deps)


def _reduce_d2d(grads, name, deps=()):
    n = len(grads)
    lands = [lax.empty((4,) + g.shape[1:], g.dtype) for g in grads]

    def plan(x, y, c):
        return [(a, 2 * s + (1 - c), n + a, s, (x, y, 1 - c)) for a in range(n) for s in range(4)]

    return _Exchange(list(grads) + lands, plan, 4 * n, name, deps)


def _reduce_ici(chip, name, deps=()):
    n = len(chip)
    lands = [lax.empty((3,) + g.shape[1:], g.dtype) for g in chip]

    def plan(x, y, c):
        return [(a, 2 * px + py, n + a, k, (px, py, c))
                for a in range(n) for k, (px, py) in enumerate(_other_chips(x, y))]

    return _Exchange(list(chip) + lands, plan, 3 * n, name, deps)


def _pair_add(full, got, core, name):
    _, r, cdim = full.shape
    br = _row_block(r, 512)

    def body(c_ref, f_ref, g_ref, o_ref):
        o_ref[...] = (f_ref[...].astype(F32) + g_ref[...].astype(F32)).astype(o_ref.dtype)

    return pl.pallas_call(
        body, name=name,
        grid_spec=pltpu.PrefetchScalarGridSpec(
            num_scalar_prefetch=1, grid=(4, r // br),
            in_specs=[pl.BlockSpec((1, br, cdim), lambda s, i, c_ref: (2 * s + c_ref[0], i, 0)),
                      pl.BlockSpec((1, br, cdim), lambda s, i, c_ref: (s, i, 0))],
            out_specs=pl.BlockSpec((1, br, cdim), lambda s, i, c_ref: (s, i, 0))),
        out_shape=jax.ShapeDtypeStruct((4, r, cdim), full.dtype),
        compiler_params=_params(("arbitrary", "arbitrary")),
    )(core, full, got)


def _sum_devices(stack):
    _, r, cdim = stack.shape

    def body(s_ref, o_ref):
        acc = s_ref[0]
        for k in range(1, N_DEV):
            acc = acc + s_ref[k]
        o_ref[...] = acc

    return pl.pallas_call(
        body, name="sum_devices", out_shape=jax.ShapeDtypeStruct((r, cdim), F32),
        compiler_params=_params(),
    )(stack)


def _adamw(w, g, m, v, name):
    r, cdim = w.shape
    br = _row_block(r, 256)

    def body(w_ref, g_ref, m_ref, v_ref, d_ref, nm_ref, nv_ref):
        g = g_ref[...]
        nm = ADAM_B1 * m_ref[...] + (1.0 - ADAM_B1) * g
        nv = ADAM_B2 * v_ref[...] + (1.0 - ADAM_B2) * (g * g)
        m_hat = nm / (1.0 - ADAM_B1 ** ADAM_STEP)
        v_hat = nv / (1.0 - ADAM_B2 ** ADAM_STEP)
        d_ref[...] = -ADAM_LR * (m_hat / (jnp.sqrt(v_hat) + ADAM_EPS) + ADAM_WD * w_ref[...])
        nm_ref[...] = nm
        nv_ref[...] = nv

    spec = pl.BlockSpec((br, cdim), lambda i: (i, 0))
    shape = jax.ShapeDtypeStruct((r, cdim), F32)
    return pl.pallas_call(
        body, name=name, grid=(r // br,), in_specs=[spec] * 4, out_specs=[spec] * 3,
        out_shape=[shape] * 3, compiler_params=_params(("arbitrary",)),
    )(w, g, m, v)


def _sum_adamw(chip, got, slot, w, m, v, name, after):
    rows, cdim = w.shape
    bc = 2 * LANE if cdim % (2 * LANE) == 0 else cdim

    def body(s_ref, c_ref, g_ref, w_ref, m_ref, v_ref, _, go_ref, d_ref, nm_ref, nv_ref):
        g = c_ref[0].astype(F32)
        for k in range(3):
            g = g + g_ref[k].astype(F32)
        nm = ADAM_B1 * m_ref[...] + (1.0 - ADAM_B1) * g
        nv = ADAM_B2 * v_ref[...] + (1.0 - ADAM_B2) * (g * g)
        m_hat = nm / (1.0 - ADAM_B1 ** ADAM_STEP)
        v_hat = nv / (1.0 - ADAM_B2 ** ADAM_STEP)
        go_ref[...] = g
        d_ref[...] = -ADAM_LR * (m_hat / (jnp.sqrt(v_hat) + ADAM_EPS) + ADAM_WD * w_ref[...])
        nm_ref[...] = nm
        nv_ref[...] = nv

    spec = pl.BlockSpec((rows, bc), lambda j, s_ref: (0, j))
    shape = jax.ShapeDtypeStruct((rows, cdim), F32)
    return pl.pallas_call(
        body, name=name,
        grid_spec=pltpu.PrefetchScalarGridSpec(
            num_scalar_prefetch=1, grid=(cdim // bc,),
            in_specs=[pl.BlockSpec((1, rows, bc), lambda j, s_ref: (s_ref[0], 0, j)),
                      pl.BlockSpec((3, rows, bc), lambda j, s_ref: (0, 0, j)), spec, spec, spec, ANY],
            out_specs=[spec] * 4),
        out_shape=[shape] * 4,
        compiler_params=_params(("arbitrary",)),
    )(slot, chip, got, w, m, v, after)


WEIGHTS = ("ffn1_norm_g", "ffn1_w_gate", "ffn1_w_up", "ffn1_w_down", "mix_norm_g", "w_in", "gmlp_v_norm_g",
           "gmlp_w_s", "gmlp_b_s", "mla_q_norm_g", "mla_w_q_up", "mla_kv_norm_g", "mla_w_kv_up", "mla_q_head_g",
           "mla_k_head_g", "gmlp_out_g", "mla_out_g", "w_out", "ffn2_norm_g", "ffn2_w_gate", "ffn2_w_up",
           "ffn2_w_down")
SHARDED = {"ffn1_w_gate": True, "ffn1_w_up": True, "ffn1_w_down": False, "w_in": True, "mla_w_q_up": True,
           "mla_w_kv_up": True, "w_out": False, "ffn2_w_gate": True, "ffn2_w_up": True, "ffn2_w_down": False}


def _col_block(m, target):
    best = LANE
    for cand in range(LANE, min(m, target) + 1, LANE):
        if m % cand == 0:
            best = cand
    return best


def _shard_rows(w, transposed, pad_to=None):
    rows = (w[0].T if transposed else w[0]).astype(BF16)
    if pad_to is not None and pad_to != rows.shape[0]:
        rows = jnp.pad(rows, ((0, pad_to - rows.shape[0]), (0, 0)))
    return rows


def _pack(parts):
    flat = []
    for p in parts:
        f = p.reshape(-1).astype(F32)
        flat.append(jnp.pad(f, (0, _round_up(f.size, LANE) - f.size)))
    flat = jnp.concatenate(flat)
    rows = _round_up(flat.size // LANE, SUBLANE)
    return jnp.pad(flat, (0, rows * LANE - flat.size)).reshape(rows, LANE)


def _unpack(packed, shapes):
    out, row = [], 0
    for shp in shapes:
        size = 1
        for s in shp:
            size *= s
        nrows = _round_up(size, LANE) // LANE
        out.append(packed[row:row + nrows].reshape(-1)[:size].reshape(shp))
        row += nrows
    return out


def kernel(x, positions, ffn1_norm_g, ffn1_w_gate, ffn1_w_up, ffn1_w_down, mix_norm_g, w_in, gmlp_v_norm_g, gmlp_w_s, gmlp_b_s, mla_q_norm_g, mla_w_q_up, mla_kv_norm_g, mla_w_kv_up, mla_q_head_g, mla_k_head_g, gmlp_out_g, mla_out_g, w_out, ffn2_norm_g, ffn2_w_gate, ffn2_w_up, ffn2_w_down, loss_target, m_ffn1_norm_g, m_ffn1_w_gate, m_ffn1_w_up, m_ffn1_w_down, m_mix_norm_g, m_w_in, m_gmlp_v_norm_g, m_gmlp_w_s, m_gmlp_b_s, m_mla_q_norm_g, m_mla_w_q_up, m_mla_kv_norm_g, m_mla_w_kv_up, m_mla_q_head_g, m_mla_k_head_g, m_gmlp_out_g, m_mla_out_g, m_w_out, m_ffn2_norm_g, m_ffn2_w_gate, m_ffn2_w_up, m_ffn2_w_down, v_ffn1_norm_g, v_ffn1_w_gate, v_ffn1_w_up, v_ffn1_w_down, v_mix_norm_g, v_w_in, v_gmlp_v_norm_g, v_gmlp_w_s, v_gmlp_b_s, v_mla_q_norm_g, v_mla_w_q_up, v_mla_kv_norm_g, v_mla_w_kv_up, v_mla_q_head_g, v_mla_k_head_g, v_gmlp_out_g, v_mla_out_g, v_w_out, v_ffn2_norm_g, v_ffn2_w_gate, v_ffn2_w_up, v_ffn2_w_down):
    wts = dict(zip(WEIGHTS, (ffn1_norm_g, ffn1_w_gate, ffn1_w_up, ffn1_w_down, mix_norm_g, w_in, gmlp_v_norm_g, gmlp_w_s, gmlp_b_s, mla_q_norm_g, mla_w_q_up, mla_kv_norm_g, mla_w_kv_up, mla_q_head_g, mla_k_head_g, gmlp_out_g, mla_out_g, w_out, ffn2_norm_g, ffn2_w_gate, ffn2_w_up, ffn2_w_down)))
    mom1 = dict(zip(WEIGHTS, (m_ffn1_norm_g, m_ffn1_w_gate, m_ffn1_w_up, m_ffn1_w_down, m_mix_norm_g, m_w_in, m_gmlp_v_norm_g, m_gmlp_w_s, m_gmlp_b_s, m_mla_q_norm_g, m_mla_w_q_up, m_mla_kv_norm_g, m_mla_w_kv_up, m_mla_q_head_g, m_mla_k_head_g, m_gmlp_out_g, m_mla_out_g, m_w_out, m_ffn2_norm_g, m_ffn2_w_gate, m_ffn2_w_up, m_ffn2_w_down)))
    mom2 = dict(zip(WEIGHTS, (v_ffn1_norm_g, v_ffn1_w_gate, v_ffn1_w_up, v_ffn1_w_down, v_mix_norm_g, v_w_in, v_gmlp_v_norm_g, v_gmlp_w_s, v_gmlp_b_s, v_mla_q_norm_g, v_mla_w_q_up, v_mla_kv_norm_g, v_mla_w_kv_up, v_mla_q_head_g, v_mla_k_head_g, v_gmlp_out_g, v_mla_out_g, v_w_out, v_ffn2_norm_g, v_ffn2_w_gate, v_ffn2_w_up, v_ffn2_w_down)))

    b_loc, seq, d = x.shape
    t = b_loc * seq
    ffs = ffn1_w_gate.shape[2]
    fp = _round_up(ffs, LANE)
    wg = gmlp_v_norm_g.shape[1]
    groups = gmlp_w_s.shape[1]
    rq, rkv = mla_q_norm_g.shape[1], mla_kv_norm_g.shape[1]
    heads = mla_out_g.shape[1]
    assert w_in.shape[2] * N_DEV == 2 * wg + rq + rkv + ROPE and mla_w_kv_up.shape[2] * N_DEV == heads * HEADW
    tm = min(512, t)
    tm_mix = min(256, t)
    blk = min(256, seq)

    xf = x.reshape(t, d)
    target = loss_target.reshape(t, d)
    pos = positions.reshape(t, 1).astype(F32)
    half = ROPE // 2
    inv_freq = 1.0 / (ROPE_THETA ** (jnp.arange(half, dtype=F32) / half))
    freq = jnp.concatenate([inv_freq, inv_freq, jnp.zeros((LANE - ROPE,), F32)])[None, :]
    lane = jnp.arange(LANE)
    masks = jnp.stack([jnp.where(lane < half, -1.0, 0.0),
                       jnp.where((lane >= half) & (lane < ROPE), 1.0, 0.0)]).astype(F32)
    gqh = jnp.pad(mla_q_head_g, ((0, 0), (0, HEADW - QK)))
    gkh = jnp.pad(mla_k_head_g, ((0, 0), (0, HEADW - QK)))
    bias = jnp.repeat(gmlp_b_s[0].T, CHUNK, axis=1)
    gouta = gmlp_out_g.reshape(1, wg)
    goutb = mla_out_g.reshape(1, heads * VHEAD)
    ws = gmlp_w_s[0]

    px, py, pc = _place()
    me = 4 * px + 2 * py + pc
    core = pc.astype(jnp.int32).reshape(1)
    slot = (2 * px + py).astype(jnp.int32).reshape(1)
    order = [n for n in WEIGHTS if n in SHARDED]
    group = {"ffn1": [n for n in order if n.startswith("ffn1")], "ffn2": [n for n in order if n.startswith("ffn2")],
             "mix": [n for n in order if not n.startswith("ffn")]}
    shard = {n: _shard_rows(wts[n], SHARDED[n], fp) for n in group["ffn1"]}

    def tied(arr, token):
        return arr + token[0, 0].astype(arr.dtype)

    xnb, ynb, dgn = 4 * (1 - px) + 2 * py, 4 * px + 2 * (1 - py), 4 * (1 - px) + 2 * (1 - py)
    ids_a = jnp.stack([me, 4 * px + 2 * py + (1 - pc)]).astype(jnp.int32)
    ids_b = jnp.stack([xnb, xnb + 1, ynb, ynb + 1]).astype(jnp.int32)
    ids_c = jnp.stack([dgn, dgn + 1]).astype(jnp.int32)
    g1 = _StagedGather([shard[n] for n in group["ffn1"]], me, "gather_ffn1")
    token = g1.start("own")
    token = g1.start("nbr", deps=(token,))
    for n in group["mix"] + group["ffn2"]:
        shard[n] = _shard_rows(tied(wts[n], token), SHARDED[n], fp if n.startswith("ffn") else None)
    g3 = _StagedGather([shard[n] for n in group["ffn2"]], me, "gather_ffn2")
    g1.wait("own", token)
    x1, xn1, kept1 = _ffn_fwd(xf, None, ffn1_norm_g, ids_a, *g1.lands(), None, tm, "ffn1_fwd_a")
    g1.wait("nbr", x1)
    token = g1.start("diag")
    ici2 = _gather_ici([shard[n] for n in group["mix"]], me, "gather_mix_ici", deps=(token,))
    token = g3.start("nbr", deps=(ici2.token,))
    token = g1.start("nbr_d2d", deps=(token,))
    g1.wait("nbr_d2d", token)
    x1, xn1, kept1 = _ffn_fwd(x1, xn1, None, ids_b, *g1.lands(), kept1, tm, "ffn1_fwd_b")
    g1.wait("diag", x1)
    token = g1.start("diag_d2d")
    g1.wait("diag_d2d", token)
    full = dict(zip(group["ffn1"], g1.lands()))
    x1, xn1, (gd1, sl1, h1) = _ffn_fwd(x1, xn1, None, ids_c, full["ffn1_w_gate"], full["ffn1_w_up"],
                                       full["ffn1_w_down"], kept1, tm, "ffn1_fwd_c")
    d2d2 = _gather_d2d(ici2.wait(x1)[len(group["mix"]):], "gather_mix_d2d")
    full.update(zip(group["mix"], d2d2.wait(d2d2.token)))
    win_t = full["w_in"].reshape(-1, d)
    splits = (2 * wg, rq, rkv, LANE)
    wq_t = jnp.pad(full["mla_w_q_up"].reshape(heads, QK, rq), ((0, 0), (0, HEADW - QK), (0, 0)))
    wkv_t = full["mla_w_kv_up"].reshape(heads, HEADW, rkv)
    wout = full["w_out"].reshape(-1, d)
    hn, zuv, cq, ckv, krw = _inproj_fwd(x1, mix_norm_g, win_t, splits, tm)
    ya = _gmlp_fwd(zuv, gmlp_v_norm_g, ws, bias, gouta, tm_mix)
    g3.wait("nbr", ya)
    token = g3.start("diag")
    token = g3.start("own_nbr_d2d", deps=(token,))
    q, k, vv = _mla_proj_fwd(cq, ckv, krw, pos, freq, masks, mla_q_norm_g, mla_kv_norm_g, wq_t, wkv_t,
                             tied(gqh, token), gkh, tm_mix)
    o, lse = _attn_fwd(q, k, vv, seq, blk)
    g3.wait("diag", o)
    token = g3.start("diag_d2d")
    x2, ycat = _out_fwd(ya, o, tied(goutb, token), wout, x1, tm)
    g3.wait("own_nbr_d2d", x2)
    g3.wait("diag_d2d", x2)
    full.update(zip(group["ffn2"], g3.lands()))
    x3, xn2, (gd2, sl2, h2) = _ffn_fwd(x2, None, ffn2_norm_g, jnp.arange(N_DEV, dtype=jnp.int32),
                                       full["ffn2_w_gate"], full["ffn2_w_up"], full["ffn2_w_down"], None, tm,
                                       "ffn2_fwd")
    dx3, loss_part = _loss_head(x3, target, tm)

    outs_g, outs_d, outs_m, outs_v = {}, {}, {}, {}

    def finish(names, chip, got, after):
        for n, cp, gt in zip(names, chip, got):
            rows_of = (lambda a: a[0].T) if SHARDED[n] else (lambda a: a[0])
            res = _sum_adamw(cp, gt, slot, rows_of(wts[n]), rows_of(mom1[n]), rows_of(mom2[n]), "adamw_" + n, after)
            outs_g[n], outs_d[n], outs_m[n], outs_v[n] = [r.T[None] if SHARDED[n] else r[None] for r in res]

    def chip_sums(names, ex, after):
        res = ex.wait(after)
        return [_pair_add(f, gt, core, "pair_add_" + n) for n, f, gt in zip(names, res[:len(names)], res[len(names):])]

    tk = min(1024, t)
    grads = {}
    small = {}
    dx2, small["ffn2_norm_g"], da2, db2 = _ffn_bwd(
        dx3, x2, ffn2_norm_g, gd2, sl2, full["ffn2_w_gate"], full["ffn2_w_up"], full["ffn2_w_down"], tm, "ffn2_bwd")
    grads["ffn2_w_gate"] = _matmul_tn(da2, xn2, fp, d, tk, BF16, "dw_ffn2_gate").reshape(N_DEV, fp, d)
    grads["ffn2_w_up"] = _matmul_tn(db2, xn2, fp, d, tk, BF16, "dw_ffn2_up").reshape(N_DEV, fp, d)
    grads["ffn2_w_down"] = _matmul_tn(h2, dx3, fp, d, tk, BF16, "dw_ffn2_down", rhs_scale=0.5).reshape(
        N_DEV, fp, d)
    red_a2 = _reduce_d2d([grads[n] for n in group["ffn2"]], "reduce_ffn2_d2d")
    dya, do, delta, small["mla_out_g"] = _out_bwd(dx2, o, tied(goutb, red_a2.token), wout, wg, tm)
    grads["w_out"] = _matmul_tn(ycat, dx2, _col_block(ycat.shape[1], 768), d, tk, BF16, "dw_out").reshape(
        N_DEV, -1, d)
    chip2 = chip_sums(group["ffn2"], red_a2, dya)
    red_b2 = _reduce_ici(chip2, "reduce_ffn2_ici")
    dq, dk, dv = _attn_bwd(q, k, vv, do, lse, delta, seq, blk, red_b2.token)
    (dcq, dckv, dkrw, dwq, dwkv, small["mla_q_norm_g"], small["mla_kv_norm_g"], dgqh, dgkh) = _mla_proj_bwd(
        dq, dk, dv, cq, ckv, krw, pos, freq, masks, mla_q_norm_g, mla_kv_norm_g, wq_t, wkv_t, gqh, gkh, tm_mix)
    small["mla_q_head_g"], small["mla_k_head_g"] = dgqh[:, :QK], dgkh[:, :QK]
    grads["mla_w_q_up"] = dwq[:, :QK].astype(BF16).reshape(N_DEV, -1, rq)
    grads["mla_w_kv_up"] = dwkv.astype(BF16).reshape(N_DEV, -1, rkv)
    dzuv, small["gmlp_w_s"], dbs, small["gmlp_v_norm_g"], small["gmlp_out_g"] = _gmlp_bwd(
        dya, zuv, gmlp_v_norm_g, ws, bias, gouta, tm_mix)
    small["gmlp_b_s"] = dbs[:, :, 0]
    dx1, small["mix_norm_g"], dzc = _inproj_bwd([dzuv, dcq, dckv, dkrw], x1, mix_norm_g, win_t, dx2, splits,
                                                tm_mix)
    grads["w_in"] = _matmul_tn(dzc, hn, _col_block(dzc.shape[1], 768), d, tk, BF16, "dw_in",
                               out_rows=win_t.shape[0]).reshape(N_DEV, -1, d)
    res_b2 = red_b2.wait(grads["w_in"])
    red_am = _reduce_d2d([grads[n] for n in group["mix"]], "reduce_mix_d2d")

    def ffn1_dw(n, lhs, rhs, scale, token):
        return _matmul_tn(lhs, rhs, fp, d, tk, BF16, "dw_" + n, rhs_scale=scale, deps=(token,)).reshape(N_DEV, fp, d)

    gr = ffn1_dw("ffn1_w_down", h1, dx1, 0.5, red_am.token)
    red_ad = _reduce_d2d([gr], "reduce_ffn1_w_down_d2d", deps=(red_am.token,))
    dx0, small["ffn1_norm_g"], da1, db1 = _ffn_bwd(
        dx1, xf, tied(ffn1_norm_g, red_ad.token), gd1, sl1, full["ffn1_w_gate"], full["ffn1_w_up"],
        full["ffn1_w_down"], tm, "ffn1_bwd")
    chipm = chip_sums(group["mix"], red_am, dx0)
    red_bm = _reduce_ici(chipm, "reduce_mix_ici")
    red_b = [("ffn1_w_down", _reduce_ici(chip_sums(["ffn1_w_down"], red_ad, dx0), "reduce_ffn1_w_down_ici",
                                         deps=(red_bm.token,)))]
    rep = [n for n in WEIGHTS if n not in SHARDED]
    small_ici = _gather_ici([_pack([small[n] for n in rep] + [loss_part])], me, "gather_small_ici",
                            deps=(red_b[-1][1].token,))
    gr = ffn1_dw("ffn1_w_gate", da1, xn1, None, small_ici.token)
    red_ag = _reduce_d2d([gr], "reduce_ffn1_w_gate_d2d")
    red_b.append(("ffn1_w_gate", _reduce_ici(chip_sums(["ffn1_w_gate"], red_ag, red_ag.token),
                                             "reduce_ffn1_w_gate_ici")))
    gr = ffn1_dw("ffn1_w_up", db1, xn1, None, red_b[-1][1].token)
    small_d2d = _gather_d2d(small_ici.wait(gr)[1:], "gather_small_d2d")
    red_au = _reduce_d2d([gr], "reduce_ffn1_w_up_d2d", deps=(small_d2d.token,))
    red_b.append(("ffn1_w_up", _reduce_ici(chip_sums(["ffn1_w_up"], red_au, red_au.token), "reduce_ffn1_w_up_ici")))
    token = red_b[-1][1].token
    finish(group["ffn2"], res_b2[:3], res_b2[3:], token)
    res = red_bm.wait(outs_g[group["ffn2"][-1]])
    nm_ = len(group["mix"])
    finish(group["mix"], res[:nm_], res[nm_:], token)
    total = _sum_devices(small_d2d.wait(outs_g[group["mix"][-1]])[0])
    zero = jnp.zeros((1,), F32)
    dlt, nm, nv = _adamw(_pack([wts[n] for n in rep] + [zero]), total, _pack([mom1[n] for n in rep] + [zero]),
                         _pack([mom2[n] for n in rep] + [zero]), "adamw_small")
    shapes = [wts[n].shape for n in rep] + [(1,)]
    for n, g, dl, m1, m2 in zip(rep, _unpack(total, shapes), _unpack(dlt, shapes), _unpack(nm, shapes),
                                _unpack(nv, shapes)):
        outs_g[n], outs_d[n], outs_m[n], outs_v[n] = g, dl, m1, m2
    loss = _unpack(total, shapes)[-1].reshape(())
    for n, ex in red_b:
        res = ex.wait(dlt)
        finish([n], res[:1], res[1:], token)

    return (loss, dx0.reshape(b_loc, seq, d), *[outs_g[n] for n in WEIGHTS], *[outs_d[n] for n in WEIGHTS],
            *[outs_m[n] for n in WEIGHTS], *[outs_v[n] for n in WEIGHTS])
```

```python
import functools

import jax
import jax.numpy as jnp
from jax import lax
from jax.experimental import pallas as pl
from jax.experimental.pallas import tpu as pltpu

F32 = jnp.float32
BF16 = jnp.bfloat16
EPS = 1e-6
LANE = 128
SUBLANE = 8
N_DEV = 8
VMEM_LIMIT = 60 * 1024 * 1024
NOPE = 128
ROPE = 64
VHEAD = 128
QK = NOPE + ROPE
HEADW = 2 * LANE
CHUNK = 128
ROPE_THETA = 10000.0
ADAM_LR, ADAM_B1, ADAM_B2, ADAM_EPS, ADAM_WD, ADAM_STEP = 0.001, 0.9, 0.999, 1e-08, 0.01, 10
MESH = pl.DeviceIdType.MESH
ANY = pl.BlockSpec(memory_space=pl.ANY)
WHOLE_VMEM = pl.BlockSpec(memory_space=pltpu.VMEM)


def _params(sem=None):
    return pltpu.CompilerParams(dimension_semantics=sem, vmem_limit_bytes=VMEM_LIMIT)


def _round_up(n, m):
    return -(-n // m) * m


def _row_block(rows, target):
    best = rows
    for cand in range(SUBLANE, min(rows, target) + 1, SUBLANE):
        if rows % cand == 0:
            best = cand
    return best if best <= target else rows


def _nn(a, b):
    return jnp.dot(a, b, preferred_element_type=F32)


def _nt(a, b):
    return lax.dot_general(a, b, (((1,), (1,)), ((), ())), preferred_element_type=F32)


def _tn(a, b):
    return lax.dot_general(a, b, (((0,), (0,)), ((), ())), preferred_element_type=F32)


def _rstd(x, n):
    return lax.rsqrt(jnp.sum(x * x, axis=-1, keepdims=True) * (1.0 / n) + EPS)


def _rms_fwd(x, g, n):
    return x * _rstd(x, n) * g


def _rms_bwd(x, g, dy, n):
    r = _rstd(x, n)
    xh = x * r
    dyg = dy * g
    dx = r * (dyg - xh * (jnp.sum(dyg * xh, axis=-1, keepdims=True) * (1.0 / n)))
    return dx, jnp.sum(dy * xh, axis=0, keepdims=True)


def _gelu(x):
    return 0.5 * x * (1.0 + lax.erf(x * 0.7071067811865476))


def _gelu_grad(x):
    return 0.5 * (1.0 + lax.erf(x * 0.7071067811865476)) + x * jnp.exp(-0.5 * x * x) * 0.3989422804014327


def _ffn_fwd(base, xn, g, ids, wg_t, wu_t, wd, saved, tm, name):
    t, d = base.shape
    nb, fp, _ = wg_t.shape
    n = ids.shape[0]
    first = xn is None
    if saved is None:
        saved = [lax.empty((t, nb * fp), BF16) for _ in range(3)]

    def body(ids_ref, *refs):
        if first:
            base_ref, g_ref, wg_ref, wu_ref, wd_ref, _, _, _, out_ref, xn_ref, gd_ref, sl_ref, h_ref, acc = refs
        else:
            base_ref, xn_ref, wg_ref, wu_ref, wd_ref, _, _, _, out_ref, gd_ref, sl_ref, h_ref, acc = refs
        j = pl.program_id(1)

        @pl.when(j == 0)
        def _():
            if first:
                xn_ref[...] = _rms_fwd(base_ref[...], g_ref[...], d).astype(BF16)
            acc[...] = jnp.zeros_like(acc)

        xnb = xn_ref[...]
        a = _nt(xnb, wg_ref[0])
        b = _nt(xnb, wu_ref[0])
        s = jax.nn.sigmoid(a)
        sl = a * s
        h = (sl * b).astype(BF16)
        gd_ref[...] = (b * (s * (1.0 + a * (1.0 - s)))).astype(BF16)
        sl_ref[...] = sl.astype(BF16)
        h_ref[...] = h
        acc[...] += _nn(h, wd_ref[0])

        @pl.when(j == n - 1)
        def _():
            out_ref[...] = base_ref[...] + 0.5 * acc[...]

    wspec = pl.BlockSpec((1, fp, d), lambda i, j, ids_ref: (ids_ref[j], 0, 0))
    row = pl.BlockSpec((tm, d), lambda i, j, ids_ref: (i, 0))
    ff = pl.BlockSpec((tm, fp), lambda i, j, ids_ref: (i, ids_ref[j]))
    ffs = jax.ShapeDtypeStruct((t, nb * fp), BF16)
    second = pl.BlockSpec((1, d), lambda i, j, ids_ref: (0, 0)) if first else row
    n_row_outs = 2 if first else 1
    res = pl.pallas_call(
        body, name=name,
        grid_spec=pltpu.PrefetchScalarGridSpec(
            num_scalar_prefetch=1, grid=(t // tm, n),
            in_specs=[row, second, wspec, wspec, wspec, ANY, ANY, ANY],
            out_specs=[row] * n_row_outs + [ff, ff, ff],
            scratch_shapes=[pltpu.VMEM((tm, d), F32)]),
        out_shape=[jax.ShapeDtypeStruct((t, d), F32)] + ([jax.ShapeDtypeStruct((t, d), BF16)] if first else [])
        + [ffs, ffs, ffs],
        input_output_aliases={6 + k: n_row_outs + k for k in range(3)},
        compiler_params=_params(("arbitrary", "arbitrary")),
    )(ids, base, g if first else xn, wg_t, wu_t, wd, *saved)
    return (res[0], res[1] if first else xn, list(res[n_row_outs:]))


def _ffn_bwd(dout, x, g, gd, sl, wg_t, wu_t, wd, tm, name):
    t, d = x.shape
    nb, fp, _ = wg_t.shape

    def body(do_hbm, x_hbm, g_ref, gd_ref, sl_ref, wg_ref, wu_ref, wd_ref, wd_next_ref,
             dx_hbm, dg_ref, da_ref, db_ref, acc, rowbuf, dy_scr, dh_scr, sem):
        i, j = pl.program_id(0), pl.program_id(1)
        rows = pl.ds(pl.multiple_of(i * tm, tm), tm)

        def fetch(src):
            cp = pltpu.make_async_copy(src.at[rows, :], rowbuf, sem)
            cp.start()
            cp.wait()

        @pl.when(j == 0)
        def _():
            fetch(do_hbm)
            dy_scr[...] = (0.5 * rowbuf[...]).astype(BF16)
            acc[...] = jnp.zeros_like(acc)
            dh_scr[0] = _nt(dy_scr[...], wd_ref[0])

        @pl.when((i == 0) & (j == 0))
        def _():
            dg_ref[...] = jnp.zeros_like(dg_ref)

        dh = dh_scr[j % 2]
        dh_scr[(j + 1) % 2] = _nt(dy_scr[...], wd_next_ref[0])
        da = (dh * gd_ref[...].astype(F32)).astype(BF16)
        db = (dh * sl_ref[...].astype(F32)).astype(BF16)
        da_ref[...] = da
        db_ref[...] = db
        acc[...] += _nn(da, wg_ref[0]) + _nn(db, wu_ref[0])

        @pl.when(j == nb - 1)
        def _():
            fetch(x_hbm)
            dxn, dg = _rms_bwd(rowbuf[...], g_ref[...], acc[...], d)
            dg_ref[...] += dg
            acc[...] = dxn
            fetch(do_hbm)
            acc[...] += rowbuf[...]
            out = pltpu.make_async_copy(acc, dx_hbm.at[rows, :], sem)
            out.start()
            out.wait()

    wspec = pl.BlockSpec((1, fp, d), lambda i, j: (j, 0, 0))
    wnext = pl.BlockSpec((1, fp, d), lambda i, j: (jnp.minimum(j + 1, nb - 1), 0, 0))
    vec = pl.BlockSpec((1, d), lambda i, j: (0, 0))
    ff = pl.BlockSpec((tm, fp), lambda i, j: (i, j))
    ffs = jax.ShapeDtypeStruct((t, nb * fp), BF16)
    return pl.pallas_call(
        body, name=name, grid=(t // tm, nb),
        in_specs=[ANY, ANY, vec, ff, ff, wspec, wspec, pl.BlockSpec((1, fp, d), lambda i, j: (0, 0, 0)), wnext],
        out_specs=[ANY, vec, ff, ff],
        out_shape=[jax.ShapeDtypeStruct((t, d), F32), jax.ShapeDtypeStruct((1, d), F32), ffs, ffs],
        scratch_shapes=[pltpu.VMEM((tm, d), F32), pltpu.VMEM((tm, d), F32), pltpu.VMEM((tm, d), BF16),
                        pltpu.VMEM((2, tm, fp), F32), pltpu.SemaphoreType.DMA],
        compiler_params=_params(("arbitrary", "arbitrary")),
    )(dout, x, g, gd, sl, wg_t, wu_t, wd, wd)


def _matmul_tn(lhs, rhs, bm, bn, tk, out_dtype, name, rhs_scale=None, deps=(), out_rows=None):
    t, m = lhs.shape
    n = rhs.shape[1]
    nk = t // tk
    out_rows = m if out_rows is None else out_rows

    def body(l_ref, r_ref, *refs):
        o_ref, acc = refs[len(deps):]
        k = pl.program_id(2)

        @pl.when(k == 0)
        def _():
            acc[...] = jnp.zeros_like(acc)

        r = r_ref[...] if rhs_scale is None else rhs_scale * r_ref[...]
        acc[...] += _tn(l_ref[...].astype(BF16), r.astype(BF16))

        @pl.when(k == nk - 1)
        def _():
            o_ref[...] = acc[...].astype(out_dtype)

    return pl.pallas_call(
        body, name=name, grid=(m // bm, n // bn, nk),
        in_specs=[pl.BlockSpec((tk, bm), lambda i, j, k: (k, i)), pl.BlockSpec((tk, bn), lambda i, j, k: (k, j))]
        + [ANY] * len(deps),
        out_specs=pl.BlockSpec((bm, bn), lambda i, j, k: (i, j)),
        out_shape=jax.ShapeDtypeStruct((out_rows, n), out_dtype),
        scratch_shapes=[pltpu.VMEM((bm, bn), F32)],
        compiler_params=_params(("arbitrary", "arbitrary", "arbitrary")),
    )(lhs, rhs, *deps)


def _last_rows_padded(w_ref, tail_ref, off, real):
    @pl.when(pl.program_id(0) == 0)
    def _():
        tail_ref[...] = jnp.zeros_like(tail_ref)
        tail_ref[0:real, :] = w_ref[off:off + real, :]


def _inproj_fwd(x, g, w_t, splits, tm):
    t, d = x.shape
    offs = [sum(splits[:k]) for k in range(len(splits))]
    real_last = w_t.shape[0] - offs[-1]

    def body(x_ref, g_ref, w_ref, hn_ref, *refs):
        z_refs, tail_ref = refs[:-1], refs[-1]
        _last_rows_padded(w_ref, tail_ref, offs[-1], real_last)
        hn = _rms_fwd(x_ref[...], g_ref[...], d).astype(BF16)
        hn_ref[...] = hn
        for z_ref, o, n in zip(z_refs[:-1], offs, splits):
            z_ref[...] = _nt(hn, w_ref[o:o + n, :])
        z_refs[-1][...] = _nt(hn, tail_ref[...])

    row = pl.BlockSpec((tm, d), lambda i: (i, 0))
    return pl.pallas_call(
        body, name="inproj_fwd", grid=(t // tm,),
        in_specs=[row, pl.BlockSpec((1, d), lambda i: (0, 0)), WHOLE_VMEM],
        out_specs=[row] + [pl.BlockSpec((tm, n), lambda i: (i, 0)) for n in splits],
        out_shape=[jax.ShapeDtypeStruct((t, d), BF16)] + [jax.ShapeDtypeStruct((t, n), F32) for n in splits],
        scratch_shapes=[pltpu.VMEM((splits[-1], d), BF16)],
        compiler_params=_params(("arbitrary",)),
    )(x, g, w_t)


def _inproj_bwd(dzs, x, g, w_t, dres, splits, tm):
    t, d = x.shape
    offs = [sum(splits[:k]) for k in range(len(splits))]
    ni = sum(splits)
    nz = len(splits)
    real_last = w_t.shape[0] - offs[-1]

    def body(*refs):
        dz_refs = refs[:nz]
        x_ref, g_ref, w_ref, dres_ref, dx_ref, dg_ref, dzc_ref, tail_ref = refs[nz:]
        _last_rows_padded(w_ref, tail_ref, offs[-1], real_last)
        dhn = jnp.zeros((tm, d), F32)
        for k, (dz_ref, o, n) in enumerate(zip(dz_refs, offs, splits)):
            dz = dz_ref[...].astype(BF16)
            dzc_ref[:, o:o + n] = dz
            dhn += _nn(dz, tail_ref[...] if k == nz - 1 else w_ref[o:o + n, :])
        dx, dg = _rms_bwd(x_ref[...], g_ref[...], dhn, d)
        dx_ref[...] = dres_ref[...] + dx

        @pl.when(pl.program_id(0) == 0)
        def _():
            dg_ref[...] = jnp.zeros_like(dg_ref)

        dg_ref[...] += dg

    row = pl.BlockSpec((tm, d), lambda i: (i, 0))
    vec = pl.BlockSpec((1, d), lambda i: (0, 0))
    return pl.pallas_call(
        body, name="inproj_bwd", grid=(t // tm,),
        in_specs=[pl.BlockSpec((tm, n), lambda i: (i, 0)) for n in splits] + [row, vec, WHOLE_VMEM, row],
        out_specs=[row, vec, pl.BlockSpec((tm, ni), lambda i: (i, 0))],
        out_shape=[jax.ShapeDtypeStruct((t, d), F32), jax.ShapeDtypeStruct((1, d), F32),
                   jax.ShapeDtypeStruct((t, ni), BF16)],
        scratch_shapes=[pltpu.VMEM((splits[-1], d), BF16)],
        compiler_params=_params(("arbitrary",)),
    )(*dzs, x, g, w_t, dres)


def _tril_bf16(ws_ref, grp):
    rows = lax.broadcasted_iota(jnp.int32, (CHUNK, CHUNK), 0)
    cols = lax.broadcasted_iota(jnp.int32, (CHUNK, CHUNK), 1)
    return jnp.where(rows >= cols, ws_ref[grp], 0.0).astype(BF16)


def _gmlp_mix(zuv_ref, gv_ref, ws_ref, bias_ref, v_scr, mixed_scr, tm, w, groups):
    u = _gelu(zuv_ref[:, 0:w])
    v0 = _gelu(zuv_ref[:, w:2 * w])
    v_scr[...] = _rms_fwd(v0, gv_ref[...], w).astype(BF16)
    for grp in range(groups):
        wsm = _tril_bf16(ws_ref, grp)
        lanes = slice(grp * CHUNK, (grp + 1) * CHUNK)
        for c in range(tm // CHUNK):
            rows = slice(c * CHUNK, (c + 1) * CHUNK)
            mixed_scr[rows, lanes] = _nn(wsm, v_scr[rows, lanes]) + bias_ref[:, lanes]
    return u, v0


def _gmlp_fwd(zuv, gv, ws, bias, gout, tm):
    t, w2 = zuv.shape
    w = w2 // 2
    groups = ws.shape[0]

    def body(zuv_ref, gv_ref, ws_ref, bias_ref, go_ref, y_ref, v_scr, mixed_scr):
        u, _ = _gmlp_mix(zuv_ref, gv_ref, ws_ref, bias_ref, v_scr, mixed_scr, tm, w, groups)
        ya = u * mixed_scr[...]
        for grp in range(groups):
            lanes = slice(grp * CHUNK, (grp + 1) * CHUNK)
            y_ref[:, lanes] = _rms_fwd(ya[:, lanes], go_ref[:, lanes], CHUNK).astype(BF16)

    const2 = lambda i: (0, 0)
    return pl.pallas_call(
        body, name="gmlp_fwd", grid=(t // tm,),
        in_specs=[pl.BlockSpec((tm, w2), lambda i: (i, 0)), pl.BlockSpec((1, w), const2),
                  pl.BlockSpec((groups, CHUNK, CHUNK), lambda i: (0, 0, 0)),
                  pl.BlockSpec((CHUNK, w), const2), pl.BlockSpec((1, w), const2)],
        out_specs=pl.BlockSpec((tm, w), lambda i: (i, 0)),
        out_shape=jax.ShapeDtypeStruct((t, w), BF16),
        scratch_shapes=[pltpu.VMEM((tm, w), BF16), pltpu.VMEM((tm, w), F32)],
        compiler_params=_params(("arbitrary",)),
    )(zuv, gv, ws, bias, gout)


def _gmlp_bwd(dy, zuv, gv, ws, bias, gout, tm):
    t, w2 = zuv.shape
    w = w2 // 2
    groups = ws.shape[0]

    def body(dy_ref, zuv_ref, gv_ref, ws_ref, bias_ref, go_ref,
             dz_ref, dws_ref, dbias_ref, dgv_ref, dgo_ref, v_scr, mixed_scr, dmix_scr, dv_scr):
        @pl.when(pl.program_id(0) == 0)
        def _():
            dws_ref[...] = jnp.zeros_like(dws_ref)
            dbias_ref[...] = jnp.zeros_like(dbias_ref)
            dgv_ref[...] = jnp.zeros_like(dgv_ref)
            dgo_ref[...] = jnp.zeros_like(dgo_ref)

        u, v0 = _gmlp_mix(zuv_ref, gv_ref, ws_ref, bias_ref, v_scr, mixed_scr, tm, w, groups)
        mixed = mixed_scr[...]
        ya = u * mixed
        for grp in range(groups):
            lanes = slice(grp * CHUNK, (grp + 1) * CHUNK)
            dya, dgo = _rms_bwd(ya[:, lanes], go_ref[:, lanes], dy_ref[:, lanes], CHUNK)
            dgo_ref[:, lanes] += dgo
            dz_ref[:, lanes] = dya * mixed[:, lanes] * _gelu_grad(zuv_ref[:, lanes])
            dmix_scr[:, lanes] = dya * u[:, lanes]
        for grp in range(groups):
            wsm = _tril_bf16(ws_ref, grp)
            lanes = slice(grp * CHUNK, (grp + 1) * CHUNK)
            dws = jnp.zeros((CHUNK, CHUNK), F32)
            dbias = jnp.zeros((CHUNK, CHUNK), F32)
            for c in range(tm // CHUNK):
                rows = slice(c * CHUNK, (c + 1) * CHUNK)
                dm = dmix_scr[rows, lanes]
                dmb = dm.astype(BF16)
                dv_scr[rows, lanes] = _tn(wsm, dmb)
                dws += _nt(dmb, v_scr[rows, lanes])
                dbias += dm
            rr = lax.broadcasted_iota(jnp.int32, (CHUNK, CHUNK), 0)
            cc = lax.broadcasted_iota(jnp.int32, (CHUNK, CHUNK), 1)
            dws_ref[grp] += jnp.where(rr >= cc, dws, 0.0)
            dbias_ref[grp] += jnp.sum(dbias, axis=1, keepdims=True)
        dv0, dgv = _rms_bwd(v0, gv_ref[...], dv_scr[...], w)
        dgv_ref[...] += dgv
        dz_ref[:, w:2 * w] = dv0 * _gelu_grad(zuv_ref[:, w:2 * w])

    const2 = lambda i: (0, 0)
    const3 = lambda i: (0, 0, 0)
    return pl.pallas_call(
        body, name="gmlp_bwd", grid=(t // tm,),
        in_specs=[pl.BlockSpec((tm, w), lambda i: (i, 0)), pl.BlockSpec((tm, w2), lambda i: (i, 0)),
                  pl.BlockSpec((1, w), const2), pl.BlockSpec((groups, CHUNK, CHUNK), const3),
                  pl.BlockSpec((CHUNK, w), const2), pl.BlockSpec((1, w), const2)],
        out_specs=[pl.BlockSpec((tm, w2), lambda i: (i, 0)), pl.BlockSpec((groups, CHUNK, CHUNK), const3),
                   pl.BlockSpec((groups, CHUNK, 1), const3), pl.BlockSpec((1, w), const2), pl.BlockSpec((1, w), const2)],
        out_shape=[jax.ShapeDtypeStruct((t, w2), F32), jax.ShapeDtypeStruct((groups, CHUNK, CHUNK), F32),
                   jax.ShapeDtypeStruct((groups, CHUNK, 1), F32), jax.ShapeDtypeStruct((1, w), F32),
                   jax.ShapeDtypeStruct((1, w), F32)],
        scratch_shapes=[pltpu.VMEM((tm, w), BF16), pltpu.VMEM((tm, w), F32),
                        pltpu.VMEM((tm, w), F32), pltpu.VMEM((tm, w), F32)],
        compiler_params=_params(("arbitrary",)),
    )(dy, zuv, gv, ws, bias, gout)


def _rot(x, m_lo, m_hi):
    return pltpu.roll(x, LANE - ROPE // 2, 1) * m_lo + pltpu.roll(x, ROPE // 2, 1) * m_hi


def _rope_tables(pos_ref, freq_ref):
    ang = pos_ref[...] * freq_ref[...]
    return jnp.cos(ang), jnp.sin(ang)


def _mla_proj_fwd(cq, ckv, krw, pos, freq, masks, gq, gkv, wq_t, wkv_t, gqh, gkh, tm):
    t, rq = cq.shape
    rkv = ckv.shape[1]
    heads = wq_t.shape[0]

    def body(cq_ref, ckv_ref, kr_ref, pos_ref, freq_ref, mk_ref, gq_ref, gkv_ref, wq_ref, wkv_ref,
             gqh_ref, gkh_ref, q_ref, k_ref, v_ref):
        cos, sin = _rope_tables(pos_ref, freq_ref)
        m_lo, m_hi = mk_ref[0:1, :], mk_ref[1:2, :]
        cqn = _rms_fwd(cq_ref[...], gq_ref[...], rq).astype(BF16)
        ckvn = _rms_fwd(ckv_ref[...], gkv_ref[...], rkv).astype(BF16)
        kr = kr_ref[...]
        kr_ss = jnp.sum(kr * kr, axis=-1, keepdims=True)
        for h in range(heads):
            qh = _nt(cqn, wq_ref[h])
            qn = qh * _rstd(qh, QK) * gqh_ref[...]
            qr = qn[:, LANE:]
            q_ref[h, :, 0:LANE] = qn[:, 0:LANE].astype(BF16)
            q_ref[h, :, LANE:] = (qr * cos + _rot(qr, m_lo, m_hi) * sin).astype(BF16)
            kvh = _nt(ckvn, wkv_ref[h])
            kn = kvh[:, 0:LANE]
            rk = lax.rsqrt((jnp.sum(kn * kn, axis=-1, keepdims=True) + kr_ss) * (1.0 / QK) + EPS)
            k_ref[h, :, 0:LANE] = (kn * rk * gkh_ref[:, 0:LANE]).astype(BF16)
            krn = kr * rk * gkh_ref[:, LANE:]
            k_ref[h, :, LANE:] = (krn * cos + _rot(krn, m_lo, m_hi) * sin).astype(BF16)
            v_ref[h] = kvh[:, LANE:].astype(BF16)

    c2 = lambda i: (0, 0)
    c3 = lambda i: (0, 0, 0)
    return pl.pallas_call(
        body, name="mla_proj_fwd", grid=(t // tm,),
        in_specs=[pl.BlockSpec((tm, rq), lambda i: (i, 0)), pl.BlockSpec((tm, rkv), lambda i: (i, 0)),
                  pl.BlockSpec((tm, LANE), lambda i: (i, 0)), pl.BlockSpec((tm, 1), lambda i: (i, 0)),
                  pl.BlockSpec((1, LANE), c2), pl.BlockSpec((2, LANE), c2),
                  pl.BlockSpec((1, rq), c2), pl.BlockSpec((1, rkv), c2),
                  pl.BlockSpec((heads, HEADW, rq), c3), pl.BlockSpec((heads, HEADW, rkv), c3),
                  pl.BlockSpec((1, HEADW), c2), pl.BlockSpec((1, HEADW), c2)],
        out_specs=[pl.BlockSpec((heads, tm, HEADW), lambda i: (0, i, 0)),
                   pl.BlockSpec((heads, tm, HEADW), lambda i: (0, i, 0)),
                   pl.BlockSpec((heads, tm, VHEAD), lambda i: (0, i, 0))],
        out_shape=[jax.ShapeDtypeStruct((heads, t, HEADW), BF16), jax.ShapeDtypeStruct((heads, t, HEADW), BF16),
                   jax.ShapeDtypeStruct((heads, t, VHEAD), BF16)],
        compiler_params=_params(("arbitrary",)),
    )(cq, ckv, krw, pos, freq, masks, gq, gkv, wq_t, wkv_t, gqh, gkh)


def _mla_proj_bwd(dq, dk, dv, cq, ckv, krw, pos, freq, masks, gq, gkv, wq_t, wkv_t, gqh, gkh, tm):
    t, rq = cq.shape
    rkv = ckv.shape[1]
    heads = wq_t.shape[0]

    def body(dq_ref, dk_ref, dv_ref, cq_ref, ckv_ref, kr_ref, pos_ref, freq_ref, mk_ref, gq_ref, gkv_ref,
             wq_ref, wkv_ref, gqh_ref, gkh_ref,
             dcq_ref, dckv_ref, dkr_ref, dwq_ref, dwkv_ref, dgq_ref, dgkv_ref, dgqh_ref, dgkh_ref):
        @pl.when(pl.program_id(0) == 0)
        def _():
            for r in (dwq_ref, dwkv_ref, dgq_ref, dgkv_ref, dgqh_ref, dgkh_ref):
                r[...] = jnp.zeros_like(r)

        cos, sin = _rope_tables(pos_ref, freq_ref)
        m_lo, m_hi = mk_ref[0:1, :], mk_ref[1:2, :]

        def unrope(dy):
            return dy * cos - _rot(dy * sin, m_lo, m_hi)

        cqn = _rms_fwd(cq_ref[...], gq_ref[...], rq).astype(BF16)
        ckvn = _rms_fwd(ckv_ref[...], gkv_ref[...], rkv).astype(BF16)
        kr = kr_ref[...]
        kr_ss = jnp.sum(kr * kr, axis=-1, keepdims=True)
        dcqn = jnp.zeros((tm, rq), F32)
        dckvn = jnp.zeros((tm, rkv), F32)
        dkr = jnp.zeros((tm, LANE), F32)
        for h in range(heads):
            qh = _nt(cqn, wq_ref[h])
            dqn = jnp.concatenate([dq_ref[h, :, 0:LANE], unrope(dq_ref[h, :, LANE:])], axis=1)
            dqh, dg = _rms_bwd(qh, gqh_ref[...], dqn, QK)
            dgqh_ref[...] += dg
            dqh = dqh.astype(BF16)
            dcqn += _nn(dqh, wq_ref[h])
            dwq_ref[h] += _tn(dqh, cqn)

            kvh = _nt(ckvn, wkv_ref[h])
            kn = kvh[:, 0:LANE]
            rk = lax.rsqrt((jnp.sum(kn * kn, axis=-1, keepdims=True) + kr_ss) * (1.0 / QK) + EPS)
            dkn_n = dk_ref[h, :, 0:LANE]
            dkr_n = unrope(dk_ref[h, :, LANE:])
            knh, krh = kn * rk, kr * rk
            dgkh_ref[:, 0:LANE] += jnp.sum(dkn_n * knh, axis=0, keepdims=True)
            dgkh_ref[:, LANE:] += jnp.sum(dkr_n * krh, axis=0, keepdims=True)
            dkn_g, dkr_g = dkn_n * gkh_ref[:, 0:LANE], dkr_n * gkh_ref[:, LANE:]
            proj = (jnp.sum(dkn_g * knh, axis=-1, keepdims=True)
                    + jnp.sum(dkr_g * krh, axis=-1, keepdims=True)) * (1.0 / QK)
            dkr += rk * (dkr_g - krh * proj)
            dkvh = jnp.concatenate([rk * (dkn_g - knh * proj), dv_ref[h]], axis=1).astype(BF16)
            dckvn += _nn(dkvh, wkv_ref[h])
            dwkv_ref[h] += _tn(dkvh, ckvn)
        dkr_ref[...] = dkr
        dcq, dg = _rms_bwd(cq_ref[...], gq_ref[...], dcqn, rq)
        dcq_ref[...] = dcq
        dgq_ref[...] += dg
        dckv, dg = _rms_bwd(ckv_ref[...], gkv_ref[...], dckvn, rkv)
        dckv_ref[...] = dckv
        dgkv_ref[...] += dg

    c2 = lambda i: (0, 0)
    c3 = lambda i: (0, 0, 0)
    hq = pl.BlockSpec((heads, tm, HEADW), lambda i: (0, i, 0))
    return pl.pallas_call(
        body, name="mla_proj_bwd", grid=(t // tm,),
        in_specs=[hq, hq, pl.BlockSpec((heads, tm, VHEAD), lambda i: (0, i, 0)),
                  pl.BlockSpec((tm, rq), lambda i: (i, 0)), pl.BlockSpec((tm, rkv), lambda i: (i, 0)),
                  pl.BlockSpec((tm, LANE), lambda i: (i, 0)), pl.BlockSpec((tm, 1), lambda i: (i, 0)),
                  pl.BlockSpec((1, LANE), c2), pl.BlockSpec((2, LANE), c2),
                  pl.BlockSpec((1, rq), c2), pl.BlockSpec((1, rkv), c2),
                  pl.BlockSpec((heads, HEADW, rq), c3), pl.BlockSpec((heads, HEADW, rkv), c3),
                  pl.BlockSpec((1, HEADW), c2), pl.BlockSpec((1, HEADW), c2)],
        out_specs=[pl.BlockSpec((tm, rq), lambda i: (i, 0)), pl.BlockSpec((tm, rkv), lambda i: (i, 0)),
                   pl.BlockSpec((tm, LANE), lambda i: (i, 0)),
                   pl.BlockSpec((heads, HEADW, rq), c3), pl.BlockSpec((heads, HEADW, rkv), c3),
                   pl.BlockSpec((1, rq), c2), pl.BlockSpec((1, rkv), c2),
                   pl.BlockSpec((1, HEADW), c2), pl.BlockSpec((1, HEADW), c2)],
        out_shape=[jax.ShapeDtypeStruct((t, rq), F32), jax.ShapeDtypeStruct((t, rkv), F32),
                   jax.ShapeDtypeStruct((t, LANE), F32),
                   jax.ShapeDtypeStruct((heads, HEADW, rq), F32), jax.ShapeDtypeStruct((heads, HEADW, rkv), F32),
                   jax.ShapeDtypeStruct((1, rq), F32), jax.ShapeDtypeStruct((1, rkv), F32),
                   jax.ShapeDtypeStruct((1, HEADW), F32), jax.ShapeDtypeStruct((1, HEADW), F32)],
        compiler_params=_params(("arbitrary",)),
    )(dq, dk, dv, cq, ckv, krw, pos, freq, masks, gq, gkv, wq_t, wkv_t, gqh, gkh)


def _lower_triangle(blk):
    return lax.broadcasted_iota(jnp.int32, (blk, blk), 0) >= lax.broadcasted_iota(jnp.int32, (blk, blk), 1)


def _attn_fwd(q, k, v, seq, blk):
    heads, t, _ = q.shape
    scale = QK ** -0.5
    nblk = seq // blk

    def body(q_ref, k_ref, v_ref, o_ref, lse_ref):
        tri = _lower_triangle(blk)
        for qi in range(nblk):
            rows = slice(qi * blk, (qi + 1) * blk)
            before = slice(0, qi * blk)
            qb = q_ref[0, rows, :]
            s_d = jnp.where(tri, _nt(qb, k_ref[0, rows, :]) * scale, -1e30)
            m = jnp.max(s_d, axis=-1, keepdims=True)
            if qi:
                s_b = _nt(qb, k_ref[0, before, :]) * scale
                m = jnp.maximum(m, jnp.max(s_b, axis=-1, keepdims=True))
                p_b = jnp.exp(s_b - m)
            p_d = jnp.exp(s_d - m)
            l = jnp.sum(p_d, axis=-1, keepdims=True)
            acc = _nn(p_d.astype(BF16), v_ref[0, rows, :])
            if qi:
                l += jnp.sum(p_b, axis=-1, keepdims=True)
                acc += _nn(p_b.astype(BF16), v_ref[0, before, :])
            o_ref[0, rows, :] = acc / l
            lse_ref[0, rows, :] = m + jnp.log(l)

    return pl.pallas_call(
        body, name="attn_fwd", grid=(heads, t // seq),
        in_specs=[pl.BlockSpec((1, seq, HEADW), lambda h, b: (h, b, 0)),
                  pl.BlockSpec((1, seq, HEADW), lambda h, b: (h, b, 0)),
                  pl.BlockSpec((1, seq, VHEAD), lambda h, b: (h, b, 0))],
        out_specs=[pl.BlockSpec((1, seq, VHEAD), lambda h, b: (h, b, 0)),
                   pl.BlockSpec((1, seq, 1), lambda h, b: (h, b, 0))],
        out_shape=[jax.ShapeDtypeStruct((heads, t, VHEAD), F32), jax.ShapeDtypeStruct((heads, t, 1), F32)],
        compiler_params=_params(("arbitrary", "arbitrary")),
    )(q, k, v)


def _attn_bwd(q, k, v, do, lse, delta, seq, blk, after):
    heads, t, _ = q.shape
    scale = QK ** -0.5
    nblk = seq // blk

    def body(q_ref, k_ref, v_ref, do_ref, lse_ref, dl_ref, _, dq_ref, dk_ref, dv_ref):
        tri = _lower_triangle(blk)
        dk_ref[...] = jnp.zeros_like(dk_ref)
        dv_ref[...] = jnp.zeros_like(dv_ref)
        for qi in range(nblk):
            rows = slice(qi * blk, (qi + 1) * blk)
            qb = q_ref[0, rows, :]
            dob = do_ref[0, rows, :]
            lse_b = lse_ref[0, rows, :]
            dl_b = dl_ref[0, rows, :]
            dq = jnp.zeros((blk, HEADW), F32)
            for keys, masked in ((slice(0, qi * blk), False), (rows, True)):
                if keys.stop == keys.start:
                    continue
                kb = k_ref[0, keys, :]
                p = jnp.exp(_nt(qb, kb) * scale - lse_b)
                if masked:
                    p = jnp.where(tri, p, 0.0)
                dp = _nt(dob, v_ref[0, keys, :])
                ds = (p * (dp - dl_b) * scale).astype(BF16)
                dv_ref[0, keys, :] += _tn(p.astype(BF16), dob)
                dk_ref[0, keys, :] += _tn(ds, qb)
                dq += _nn(ds, kb)
            dq_ref[0, rows, :] = dq

    hq = pl.BlockSpec((1, seq, HEADW), lambda h, b: (h, b, 0))
    hv = pl.BlockSpec((1, seq, VHEAD), lambda h, b: (h, b, 0))
    h1 = pl.BlockSpec((1, seq, 1), lambda h, b: (h, b, 0))
    return pl.pallas_call(
        body, name="attn_bwd", grid=(heads, t // seq),
        in_specs=[hq, hq, hv, hv, h1, h1, ANY],
        out_specs=[hq, hq, hv],
        out_shape=[jax.ShapeDtypeStruct((heads, t, HEADW), F32), jax.ShapeDtypeStruct((heads, t, HEADW), F32),
                   jax.ShapeDtypeStruct((heads, t, VHEAD), F32)],
        compiler_params=_params(("arbitrary", "arbitrary")),
    )(q, k, v, do, lse, delta, after)


def _out_fwd(ya, o, gb, w_out, x1, tm):
    t, w = ya.shape
    heads = o.shape[0]
    d = x1.shape[1]

    def body(ya_ref, o_ref, gb_ref, w_ref, x1_ref, x2_ref, yc_ref):
        yc_ref[:, 0:w] = ya_ref[...]
        for h in range(heads):
            lanes = slice(h * VHEAD, (h + 1) * VHEAD)
            yc_ref[:, w + h * VHEAD:w + (h + 1) * VHEAD] = _rms_fwd(o_ref[h], gb_ref[:, lanes], VHEAD).astype(BF16)
        x2_ref[...] = x1_ref[...] + _nn(yc_ref[...], w_ref[...])

    wy = w + heads * VHEAD
    row = pl.BlockSpec((tm, d), lambda i: (i, 0))
    return pl.pallas_call(
        body, name="out_fwd", grid=(t // tm,),
        in_specs=[pl.BlockSpec((tm, w), lambda i: (i, 0)), pl.BlockSpec((heads, tm, VHEAD), lambda i: (0, i, 0)),
                  pl.BlockSpec((1, heads * VHEAD), lambda i: (0, 0)), WHOLE_VMEM, row],
        out_specs=[row, pl.BlockSpec((tm, wy), lambda i: (i, 0))],
        out_shape=[jax.ShapeDtypeStruct((t, d), F32), jax.ShapeDtypeStruct((t, wy), BF16)],
        compiler_params=_params(("arbitrary",)),
    )(ya, o, gb, w_out, x1)


def _out_bwd(dx2, o, gb, w_out, w, tm):
    t, d = dx2.shape
    heads = o.shape[0]

    def body(dx_ref, o_ref, gb_ref, w_ref, dya_ref, do_ref, dl_ref, dgb_ref):
        @pl.when(pl.program_id(0) == 0)
        def _():
            dgb_ref[...] = jnp.zeros_like(dgb_ref)

        dyc = _nt(dx_ref[...].astype(BF16), w_ref[...])
        dya_ref[...] = dyc[:, 0:w]
        for h in range(heads):
            lanes = slice(h * VHEAD, (h + 1) * VHEAD)
            oh = o_ref[h]
            doh, dg = _rms_bwd(oh, gb_ref[:, lanes], dyc[:, w + h * VHEAD:w + (h + 1) * VHEAD], VHEAD)
            dgb_ref[:, lanes] += dg
            do_ref[h] = doh.astype(BF16)
            dl_ref[h] = jnp.sum(doh * oh, axis=-1, keepdims=True)

    ho = pl.BlockSpec((heads, tm, VHEAD), lambda i: (0, i, 0))
    vec = pl.BlockSpec((1, heads * VHEAD), lambda i: (0, 0))
    return pl.pallas_call(
        body, name="out_bwd", grid=(t // tm,),
        in_specs=[pl.BlockSpec((tm, d), lambda i: (i, 0)), ho, vec, WHOLE_VMEM],
        out_specs=[pl.BlockSpec((tm, w), lambda i: (i, 0)), ho, pl.BlockSpec((heads, tm, 1), lambda i: (0, i, 0)), vec],
        out_shape=[jax.ShapeDtypeStruct((t, w), F32), jax.ShapeDtypeStruct((heads, t, VHEAD), BF16),
                   jax.ShapeDtypeStruct((heads, t, 1), F32), jax.ShapeDtypeStruct((1, heads * VHEAD), F32)],
        compiler_params=_params(("arbitrary",)),
    )(dx2, o, gb, w_out)


def _loss_head(y, target, tm):
    t, d = y.shape

    def body(y_ref, t_ref, dy_ref, loss_ref):
        @pl.when(pl.program_id(0) == 0)
        def _():
            loss_ref[...] = jnp.zeros_like(loss_ref)

        err = y_ref[...] - t_ref[...]
        dy_ref[...] = err * (1.0 / d)
        part = jnp.sum(jnp.sum(err * err, axis=-1, keepdims=True) * (1.0 / d), axis=0, keepdims=True)
        loss_ref[...] += 0.5 * part

    row = pl.BlockSpec((tm, d), lambda i: (i, 0))
    return pl.pallas_call(
        body, name="loss_head", grid=(t // tm,),
        in_specs=[row, row], out_specs=[row, pl.BlockSpec((1, 1), lambda i: (0, 0))],
        out_shape=[jax.ShapeDtypeStruct((t, d), F32), jax.ShapeDtypeStruct((1, 1), F32)],
        compiler_params=_params(("arbitrary",)),
    )(y, target)


def _place():
    return lax.axis_index("x"), lax.axis_index("y"), lax.axis_index("c")


HBM = pl.BlockSpec(memory_space=pltpu.HBM)
SEM = pl.BlockSpec(memory_space=pltpu.SEMAPHORE)
DATAFLOW = pltpu.SideEffectType.DATAFLOW_SIDE_EFFECTING


def _plan_copies(plan, refs, send_sems, recv_sems):
    def block(ref, blk):
        if blk is None:
            return ref
        return ref.at[blk[0], pl.ds(0, blk[1])] if isinstance(blk, tuple) else ref.at[blk]

    cps = []
    for i, (sb, sblk, db, dblk, dev) in enumerate(plan(*_place())):
        cps.append(pltpu.make_async_remote_copy(
            src_ref=block(refs[sb], sblk), dst_ref=block(refs[db], dblk),
            send_sem=send_sems.at[i], recv_sem=recv_sems.at[i], device_id=dev, device_id_type=MESH))
    return cps


def _push_start(bufs, plan, ncopy, name, deps=()):
    nb = len(bufs)

    def body(*refs):
        outs = refs[nb + len(deps):]
        for cp in _plan_copies(plan, refs[:nb], outs[0], outs[1]):
            cp.start()
        outs[-1][...] = jnp.zeros_like(outs[-1])

    res = pl.pallas_call(
        body, name=name,
        out_shape=(pltpu.SemaphoreType.DMA((ncopy,)), pltpu.SemaphoreType.DMA((ncopy,)),
                   *[pltpu.HBM(b.shape, b.dtype) for b in bufs], jax.ShapeDtypeStruct((SUBLANE, LANE), F32)),
        in_specs=[HBM] * nb + [ANY] * len(deps),
        out_specs=(SEM, SEM, *[HBM] * nb, WHOLE_VMEM),
        input_output_aliases={i: 2 + i for i in range(nb)},
        compiler_params=pltpu.CompilerParams(has_side_effects=DATAFLOW),
    )(*[pltpu.with_memory_space_constraint(b, pltpu.HBM) for b in bufs], *deps)
    return res[0], res[1], list(res[2:2 + nb]), res[-1]


def _push_wait(send_sems, recv_sems, bufs, plan, after, name):
    nb = len(bufs)

    def body(*refs):
        for cp in _plan_copies(plan, refs[:nb], refs[nb], refs[nb + 1]):
            cp.wait_send()
            cp.wait_recv()

    res = pl.pallas_call(
        body, name=name,
        out_shape=[pltpu.HBM(b.shape, b.dtype) for b in bufs],
        in_specs=[HBM] * nb + [SEM, SEM, ANY], out_specs=[HBM] * nb,
        input_output_aliases={i: i for i in range(nb)},
        compiler_params=pltpu.CompilerParams(has_side_effects=DATAFLOW),
    )(*bufs, send_sems, recv_sems, after)
    return list(res)


def _other_chips(x, y):
    return ((1 - x, y), (x, 1 - y), (1 - x, 1 - y))


class _Exchange:
    def __init__(self, bufs, plan, ncopy, name, deps=()):
        self.plan, self.name = plan, name
        self.send, self.recv, self.bufs, self.token = _push_start(bufs, plan, ncopy, name + "_start", deps)

    def wait(self, after):
        return _push_wait(self.send, self.recv, self.bufs, self.plan, after, self.name + "_wait")


class _Chain:
    def __init__(self, bufs):
        self.bufs = list(bufs)

    def start(self, plan, ncopy, name, deps=()):
        send, recv, self.bufs, token = _push_start(self.bufs, plan, ncopy, name + "_start", deps)
        return (send, recv, plan, name), token

    def wait(self, pending, after):
        send, recv, plan, name = pending
        self.bufs = _push_wait(send, recv, self.bufs, plan, after, name + "_wait")


class _StagedGather:
    def __init__(self, shards, me, name, pad_to=None):
        self.n = n = len(shards)
        self.name = name
        rows = shards[0].shape[0]
        lands = []
        for s in shards:
            land = lax.empty((N_DEV, pad_to or rows) + s.shape[1:], s.dtype)
            if pad_to and pad_to != rows:
                land = lax.dynamic_update_slice(
                    land, jnp.zeros((N_DEV, pad_to - rows) + s.shape[1:], s.dtype), (0, rows, 0))
            lands.append(lax.dynamic_update_slice(land, s[None], (me, 0, 0)))
        self.chain = _Chain(list(shards) + lands)
        self.pending = {}

        def blk(b):
            return (b, rows) if pad_to and pad_to != rows else b

        def to_sibling(blocks):
            return lambda x, y, c: [(n + a, blk(b), n + a, blk(b), (x, y, 1 - c))
                                    for a in range(n) for b in blocks(x, y, c)]

        def nbr_blocks(x, y, c):
            return [4 * (1 - x) + 2 * y + c, 4 * x + 2 * (1 - y) + c]

        def diag(x, y, c):
            sx, sy = (1 - x) * (1 - c) + x * c, y * (1 - c) + (1 - y) * c
            tx, ty = x * (1 - c) + (1 - x) * c, (1 - y) * (1 - c) + y * c
            b = blk(4 * sx + 2 * sy + c)
            return [(n + a, b, n + a, b, (tx, ty, c)) for a in range(n)]

        self.plans = {
            "own": (lambda x, y, c: [(a, None, n + a, blk(4 * x + 2 * y + c), (x, y, 1 - c)) for a in range(n)], n),
            "nbr": (lambda x, y, c: [(a, None, n + a, blk(4 * x + 2 * y + c), dev) for a in range(n)
                                     for dev in ((1 - x, y, c), (x, 1 - y, c))], 2 * n),
            "diag": (diag, n),
            "nbr_d2d": (to_sibling(nbr_blocks), 2 * n),
            "own_nbr_d2d": (to_sibling(lambda x, y, c: [4 * x + 2 * y + c] + nbr_blocks(x, y, c)), 3 * n),
            "diag_d2d": (to_sibling(lambda x, y, c: [4 * (1 - x) + 2 * (1 - y) + c]), n),
        }

    def start(self, stage, deps=()):
        plan, ncopy = self.plans[stage]
        self.pending[stage], token = self.chain.start(plan, ncopy, self.name + "_" + stage, deps)
        return token

    def wait(self, stage, after):
        self.chain.wait(self.pending.pop(stage), after)

    def lands(self):
        return self.chain.bufs[self.n:]


def _gather_ici(shards, me, name, deps=()):
    n = len(shards)
    lands = [lax.dynamic_update_slice(lax.empty((N_DEV,) + s.shape, s.dtype), s[None], (me, 0, 0)) for s in shards]

    def plan(x, y, c):
        return [(a, None, n + a, 4 * x + 2 * y + c, (px, py, c)) for a in range(n) for px, py in _other_chips(x, y)]

    return _Exchange(list(shards) + lands, plan, 3 * n, name, deps)


def _gather_d2d(lands, name, deps=()):
    n = len(lands)

    def plan(x, y, c):
        blocks = [4 * x + 2 * y + c] + [4 * px + 2 * py + c for px, py in _other_chips(x, y)]
        return [(a, b, a, b, (x, y, 1 - c)) for a in range(n) for b in blocks]

    return _Exchange(list(lands), plan, 4 * n, name, deps)


def _reduce_d2d(grads, name, deps=(), rows=None):
    n = len(grads)
    lands = [lax.empty((4,) + g.shape[1:], g.dtype) for g in grads]

    def blk(b):
        return b if rows is None else (b, rows)

    def plan(x, y, c):
        return [(a, blk(2 * s + (1 - c)), n + a, blk(s), (x, y, 1 - c)) for a in range(n) for s in range(4)]

    return _Exchange(list(grads) + lands, plan, 4 * n, name, deps)


def _reduce_ici(chip, name, deps=(), rows=None):
    n = len(chip)
    lands = [lax.empty((3,) + g.shape[1:], g.dtype) for g in chip]

    def blk(b):
        return b if rows is None else (b, rows)

    def plan(x, y, c):
        return [(a, blk(2 * px + py), n + a, blk(k), (px, py, c))
                for a in range(n) for k, (px, py) in enumerate(_other_chips(x, y))]

    return _Exchange(list(chip) + lands, plan, 3 * n, name, deps)


def _pair_add(full, got, core, name, rows=None):
    _, r, cdim = full.shape
    br = _row_block(rows or r, 512)

    def body(c_ref, f_ref, g_ref, o_ref):
        o_ref[...] = (f_ref[...].astype(F32) + g_ref[...].astype(F32)).astype(o_ref.dtype)

    return pl.pallas_call(
        body, name=name,
        grid_spec=pltpu.PrefetchScalarGridSpec(
            num_scalar_prefetch=1, grid=(4, (rows or r) // br),
            in_specs=[pl.BlockSpec((1, br, cdim), lambda s, i, c_ref: (2 * s + c_ref[0], i, 0)),
                      pl.BlockSpec((1, br, cdim), lambda s, i, c_ref: (s, i, 0))],
            out_specs=pl.BlockSpec((1, br, cdim), lambda s, i, c_ref: (s, i, 0))),
        out_shape=jax.ShapeDtypeStruct((4, r, cdim), full.dtype),
        compiler_params=_params(("arbitrary", "arbitrary")),
    )(core, full, got)


def _sum_devices(stack):
    _, r, cdim = stack.shape

    def body(s_ref, o_ref):
        acc = s_ref[0]
        for k in range(1, N_DEV):
            acc = acc + s_ref[k]
        o_ref[...] = acc

    return pl.pallas_call(
        body, name="sum_devices", out_shape=jax.ShapeDtypeStruct((r, cdim), F32),
        compiler_params=_params(),
    )(stack)


def _adamw(w, g, m, v, name):
    r, cdim = w.shape
    br = _row_block(r, 256)

    def body(w_ref, g_ref, m_ref, v_ref, d_ref, nm_ref, nv_ref):
        g = g_ref[...]
        nm = ADAM_B1 * m_ref[...] + (1.0 - ADAM_B1) * g
        nv = ADAM_B2 * v_ref[...] + (1.0 - ADAM_B2) * (g * g)
        m_hat = nm / (1.0 - ADAM_B1 ** ADAM_STEP)
        v_hat = nv / (1.0 - ADAM_B2 ** ADAM_STEP)
        d_ref[...] = -ADAM_LR * (m_hat / (jnp.sqrt(v_hat) + ADAM_EPS) + ADAM_WD * w_ref[...])
        nm_ref[...] = nm
        nv_ref[...] = nv

    spec = pl.BlockSpec((br, cdim), lambda i: (i, 0))
    shape = jax.ShapeDtypeStruct((r, cdim), F32)
    return pl.pallas_call(
        body, name=name, grid=(r // br,), in_specs=[spec] * 4, out_specs=[spec] * 3,
        out_shape=[shape] * 3, compiler_params=_params(("arbitrary",)),
    )(w, g, m, v)


def _sum_adamw(chip, got, slot, w, m, v, name, after):
    rows, cdim = w.shape
    bc = 2 * LANE if cdim % (2 * LANE) == 0 else cdim

    def body(s_ref, c_ref, g_ref, w_ref, m_ref, v_ref, _, go_ref, d_ref, nm_ref, nv_ref):
        g = c_ref[0].astype(F32)
        for k in range(3):
            g = g + g_ref[k].astype(F32)
        nm = ADAM_B1 * m_ref[...] + (1.0 - ADAM_B1) * g
        nv = ADAM_B2 * v_ref[...] + (1.0 - ADAM_B2) * (g * g)
        m_hat = nm / (1.0 - ADAM_B1 ** ADAM_STEP)
        v_hat = nv / (1.0 - ADAM_B2 ** ADAM_STEP)
        go_ref[...] = g
        d_ref[...] = -ADAM_LR * (m_hat / (jnp.sqrt(v_hat) + ADAM_EPS) + ADAM_WD * w_ref[...])
        nm_ref[...] = nm
        nv_ref[...] = nv

    spec = pl.BlockSpec((rows, bc), lambda j, s_ref: (0, j))
    shape = jax.ShapeDtypeStruct((rows, cdim), F32)
    return pl.pallas_call(
        body, name=name,
        grid_spec=pltpu.PrefetchScalarGridSpec(
            num_scalar_prefetch=1, grid=(cdim // bc,),
            in_specs=[pl.BlockSpec((1, rows, bc), lambda j, s_ref: (s_ref[0], 0, j)),
                      pl.BlockSpec((3, rows, bc), lambda j, s_ref: (0, 0, j)), spec, spec, spec, ANY],
            out_specs=[spec] * 4),
        out_shape=[shape] * 4,
        compiler_params=_params(("arbitrary",)),
    )(slot, chip, got, w, m, v, after)


WEIGHTS = ("ffn1_norm_g", "ffn1_w_gate", "ffn1_w_up", "ffn1_w_down", "mix_norm_g", "w_in", "gmlp_v_norm_g",
           "gmlp_w_s", "gmlp_b_s", "mla_q_norm_g", "mla_w_q_up", "mla_kv_norm_g", "mla_w_kv_up", "mla_q_head_g",
           "mla_k_head_g", "gmlp_out_g", "mla_out_g", "w_out", "ffn2_norm_g", "ffn2_w_gate", "ffn2_w_up",
           "ffn2_w_down")
SHARDED = {"ffn1_w_gate": True, "ffn1_w_up": True, "ffn1_w_down": False, "w_in": True, "mla_w_q_up": True,
           "mla_w_kv_up": True, "w_out": False, "ffn2_w_gate": True, "ffn2_w_up": True, "ffn2_w_down": False}


def _col_block(m, target):
    best = LANE
    for cand in range(LANE, min(m, target) + 1, LANE):
        if m % cand == 0:
            best = cand
    return best


def _shard_rows(w, transposed, pad_to=None):
    rows = (w[0].T if transposed else w[0]).astype(BF16)
    if pad_to is not None and pad_to != rows.shape[0]:
        rows = jnp.pad(rows, ((0, pad_to - rows.shape[0]), (0, 0)))
    return rows


def _pack(parts):
    flat = []
    for p in parts:
        f = p.reshape(-1).astype(F32)
        flat.append(jnp.pad(f, (0, _round_up(f.size, LANE) - f.size)))
    flat = jnp.concatenate(flat)
    rows = _round_up(flat.size // LANE, SUBLANE)
    return jnp.pad(flat, (0, rows * LANE - flat.size)).reshape(rows, LANE)


def _unpack(packed, shapes):
    out, row = [], 0
    for shp in shapes:
        size = 1
        for s in shp:
            size *= s
        nrows = _round_up(size, LANE) // LANE
        out.append(packed[row:row + nrows].reshape(-1)[:size].reshape(shp))
        row += nrows
    return out


def kernel(x, positions, ffn1_norm_g, ffn1_w_gate, ffn1_w_up, ffn1_w_down, mix_norm_g, w_in, gmlp_v_norm_g, gmlp_w_s, gmlp_b_s, mla_q_norm_g, mla_w_q_up, mla_kv_norm_g, mla_w_kv_up, mla_q_head_g, mla_k_head_g, gmlp_out_g, mla_out_g, w_out, ffn2_norm_g, ffn2_w_gate, ffn2_w_up, ffn2_w_down, loss_target, m_ffn1_norm_g, m_ffn1_w_gate, m_ffn1_w_up, m_ffn1_w_down, m_mix_norm_g, m_w_in, m_gmlp_v_norm_g, m_gmlp_w_s, m_gmlp_b_s, m_mla_q_norm_g, m_mla_w_q_up, m_mla_kv_norm_g, m_mla_w_kv_up, m_mla_q_head_g, m_mla_k_head_g, m_gmlp_out_g, m_mla_out_g, m_w_out, m_ffn2_norm_g, m_ffn2_w_gate, m_ffn2_w_up, m_ffn2_w_down, v_ffn1_norm_g, v_ffn1_w_gate, v_ffn1_w_up, v_ffn1_w_down, v_mix_norm_g, v_w_in, v_gmlp_v_norm_g, v_gmlp_w_s, v_gmlp_b_s, v_mla_q_norm_g, v_mla_w_q_up, v_mla_kv_norm_g, v_mla_w_kv_up, v_mla_q_head_g, v_mla_k_head_g, v_gmlp_out_g, v_mla_out_g, v_w_out, v_ffn2_norm_g, v_ffn2_w_gate, v_ffn2_w_up, v_ffn2_w_down):
    wts = dict(zip(WEIGHTS, (ffn1_norm_g, ffn1_w_gate, ffn1_w_up, ffn1_w_down, mix_norm_g, w_in, gmlp_v_norm_g, gmlp_w_s, gmlp_b_s, mla_q_norm_g, mla_w_q_up, mla_kv_norm_g, mla_w_kv_up, mla_q_head_g, mla_k_head_g, gmlp_out_g, mla_out_g, w_out, ffn2_norm_g, ffn2_w_gate, ffn2_w_up, ffn2_w_down)))
    mom1 = dict(zip(WEIGHTS, (m_ffn1_norm_g, m_ffn1_w_gate, m_ffn1_w_up, m_ffn1_w_down, m_mix_norm_g, m_w_in, m_gmlp_v_norm_g, m_gmlp_w_s, m_gmlp_b_s, m_mla_q_norm_g, m_mla_w_q_up, m_mla_kv_norm_g, m_mla_w_kv_up, m_mla_q_head_g, m_mla_k_head_g, m_gmlp_out_g, m_mla_out_g, m_w_out, m_ffn2_norm_g, m_ffn2_w_gate, m_ffn2_w_up, m_ffn2_w_down)))
    mom2 = dict(zip(WEIGHTS, (v_ffn1_norm_g, v_ffn1_w_gate, v_ffn1_w_up, v_ffn1_w_down, v_mix_norm_g, v_w_in, v_gmlp_v_norm_g, v_gmlp_w_s, v_gmlp_b_s, v_mla_q_norm_g, v_mla_w_q_up, v_mla_kv_norm_g, v_mla_w_kv_up, v_mla_q_head_g, v_mla_k_head_g, v_gmlp_out_g, v_mla_out_g, v_w_out, v_ffn2_norm_g, v_ffn2_w_gate, v_ffn2_w_up, v_ffn2_w_down)))

    b_loc, seq, d = x.shape
    t = b_loc * seq
    ffs = ffn1_w_gate.shape[2]
    fp = _round_up(ffs, LANE)
    wg = gmlp_v_norm_g.shape[1]
    groups = gmlp_w_s.shape[1]
    rq, rkv = mla_q_norm_g.shape[1], mla_kv_norm_g.shape[1]
    heads = mla_out_g.shape[1]
    assert w_in.shape[2] * N_DEV == 2 * wg + rq + rkv + ROPE and mla_w_kv_up.shape[2] * N_DEV == heads * HEADW
    tm = min(512, t)
    tm_mix = min(256, t)
    blk = min(256, seq)

    xf = x.reshape(t, d)
    target = loss_target.reshape(t, d)
    pos = positions.reshape(t, 1).astype(F32)
    half = ROPE // 2
    inv_freq = 1.0 / (ROPE_THETA ** (jnp.arange(half, dtype=F32) / half))
    freq = jnp.concatenate([inv_freq, inv_freq, jnp.zeros((LANE - ROPE,), F32)])[None, :]
    lane = jnp.arange(LANE)
    masks = jnp.stack([jnp.where(lane < half, -1.0, 0.0),
                       jnp.where((lane >= half) & (lane < ROPE), 1.0, 0.0)]).astype(F32)
    gqh = jnp.pad(mla_q_head_g, ((0, 0), (0, HEADW - QK)))
    gkh = jnp.pad(mla_k_head_g, ((0, 0), (0, HEADW - QK)))
    bias = jnp.repeat(gmlp_b_s[0].T, CHUNK, axis=1)
    gouta = gmlp_out_g.reshape(1, wg)
    goutb = mla_out_g.reshape(1, heads * VHEAD)
    ws = gmlp_w_s[0]

    px, py, pc = _place()
    me = 4 * px + 2 * py + pc
    core = pc.astype(jnp.int32).reshape(1)
    slot = (2 * px + py).astype(jnp.int32).reshape(1)
    order = [n for n in WEIGHTS if n in SHARDED]
    group = {"ffn1": [n for n in order if n.startswith("ffn1")], "ffn2": [n for n in order if n.startswith("ffn2")],
             "mix": [n for n in order if not n.startswith("ffn")]}
    shard = {n: _shard_rows(wts[n], SHARDED[n]) for n in group["ffn1"]}
    frows = ffs if ffs != fp else None

    def tied(arr, token):
        return arr + token[0, 0].astype(arr.dtype)

    xnb, ynb, dgn = 4 * (1 - px) + 2 * py, 4 * px + 2 * (1 - py), 4 * (1 - px) + 2 * (1 - py)
    ids_a = jnp.stack([me, 4 * px + 2 * py + (1 - pc)]).astype(jnp.int32)
    ids_b = jnp.stack([xnb, xnb + 1, ynb, ynb + 1]).astype(jnp.int32)
    ids_c = jnp.stack([dgn, dgn + 1]).astype(jnp.int32)
    g1 = _StagedGather([shard[n] for n in group["ffn1"]], me, "gather_ffn1", pad_to=fp)
    token = g1.start("own")
    token = g1.start("nbr", deps=(token,))
    for n in group["mix"] + group["ffn2"]:
        shard[n] = _shard_rows(tied(wts[n], token), SHARDED[n])
    g3 = _StagedGather([shard[n] for n in group["ffn2"]], me, "gather_ffn2", pad_to=fp)
    g1.wait("own", token)
    x1, xn1, kept1 = _ffn_fwd(xf, None, ffn1_norm_g, ids_a, *g1.lands(), None, tm, "ffn1_fwd_a")
    g1.wait("nbr", x1)
    token = g1.start("diag")
    ici2 = _gather_ici([shard[n] for n in group["mix"]], me, "gather_mix_ici", deps=(token,))
    token = g3.start("nbr", deps=(ici2.token,))
    token = g1.start("nbr_d2d", deps=(token,))
    g1.wait("nbr_d2d", token)
    x1, xn1, kept1 = _ffn_fwd(x1, xn1, None, ids_b, *g1.lands(), kept1, tm, "ffn1_fwd_b")
    g1.wait("diag", x1)
    token = g1.start("diag_d2d")
    g1.wait("diag_d2d", token)
    full = dict(zip(group["ffn1"], g1.lands()))
    x1, xn1, (gd1, sl1, h1) = _ffn_fwd(x1, xn1, None, ids_c, full["ffn1_w_gate"], full["ffn1_w_up"],
                                       full["ffn1_w_down"], kept1, tm, "ffn1_fwd_c")
    d2d2 = _gather_d2d(ici2.wait(x1)[len(group["mix"]):], "gather_mix_d2d")
    full.update(zip(group["mix"], d2d2.wait(d2d2.token)))
    win_t = full["w_in"].reshape(-1, d)
    splits = (2 * wg, rq, rkv, LANE)
    wq_t = jnp.pad(full["mla_w_q_up"].reshape(heads, QK, rq), ((0, 0), (0, HEADW - QK), (0, 0)))
    wkv_t = full["mla_w_kv_up"].reshape(heads, HEADW, rkv)
    wout = full["w_out"].reshape(-1, d)
    hn, zuv, cq, ckv, krw = _inproj_fwd(x1, mix_norm_g, win_t, splits, tm)
    ya = _gmlp_fwd(zuv, gmlp_v_norm_g, ws, bias, gouta, tm_mix)
    g3.wait("nbr", ya)
    token = g3.start("diag")
    token = g3.start("own_nbr_d2d", deps=(token,))
    q, k, vv = _mla_proj_fwd(cq, ckv, krw, pos, freq, masks, mla_q_norm_g, mla_kv_norm_g, wq_t, wkv_t,
                             tied(gqh, token), gkh, tm_mix)
    o, lse = _attn_fwd(q, k, vv, seq, blk)
    g3.wait("diag", o)
    token = g3.start("diag_d2d")
    x2, ycat = _out_fwd(ya, o, tied(goutb, token), wout, x1, tm)
    g3.wait("own_nbr_d2d", x2)
    g3.wait("diag_d2d", x2)
    full.update(zip(group["ffn2"], g3.lands()))
    x3, xn2, (gd2, sl2, h2) = _ffn_fwd(x2, None, ffn2_norm_g, jnp.arange(N_DEV, dtype=jnp.int32),
                                       full["ffn2_w_gate"], full["ffn2_w_up"], full["ffn2_w_down"], None, tm,
                                       "ffn2_fwd")
    dx3, loss_part = _loss_head(x3, target, tm)

    outs_g, outs_d, outs_m, outs_v = {}, {}, {}, {}

    def finish(names, chip, got, after):
        for n, cp, gt in zip(names, chip, got):
            rows_of = (lambda a: a[0].T) if SHARDED[n] else (lambda a: a[0])
            res = _sum_adamw(cp, gt, slot, rows_of(wts[n]), rows_of(mom1[n]), rows_of(mom2[n]), "adamw_" + n, after)
            outs_g[n], outs_d[n], outs_m[n], outs_v[n] = [r.T[None] if SHARDED[n] else r[None] for r in res]

    def chip_sums(names, ex, after, rows=None):
        res = ex.wait(after)
        return [_pair_add(f, gt, core, "pair_add_" + n, rows)
                for n, f, gt in zip(names, res[:len(names)], res[len(names):])]

    tk = min(1024, t)
    grads = {}
    small = {}
    dx2, small["ffn2_norm_g"], da2, db2 = _ffn_bwd(
        dx3, x2, ffn2_norm_g, gd2, sl2, full["ffn2_w_gate"], full["ffn2_w_up"], full["ffn2_w_down"], tm, "ffn2_bwd")
    grads["ffn2_w_gate"] = _matmul_tn(da2, xn2, fp, d, tk, BF16, "dw_ffn2_gate").reshape(N_DEV, fp, d)
    grads["ffn2_w_up"] = _matmul_tn(db2, xn2, fp, d, tk, BF16, "dw_ffn2_up").reshape(N_DEV, fp, d)
    grads["ffn2_w_down"] = _matmul_tn(h2, dx3, fp, d, tk, BF16, "dw_ffn2_down", rhs_scale=0.5).reshape(
        N_DEV, fp, d)
    red_a2 = _reduce_d2d([grads[n] for n in group["ffn2"]], "reduce_ffn2_d2d", rows=frows)
    dya, do, delta, small["mla_out_g"] = _out_bwd(dx2, o, tied(goutb, red_a2.token), wout, wg, tm)
    grads["w_out"] = _matmul_tn(ycat, dx2, _col_block(ycat.shape[1], 768), d, tk, BF16, "dw_out").reshape(
        N_DEV, -1, d)
    chip2 = chip_sums(group["ffn2"], red_a2, dya, frows)
    red_b2 = _reduce_ici(chip2, "reduce_ffn2_ici", rows=frows)
    dq, dk, dv = _attn_bwd(q, k, vv, do, lse, delta, seq, blk, red_b2.token)
    (dcq, dckv, dkrw, dwq, dwkv, small["mla_q_norm_g"], small["mla_kv_norm_g"], dgqh, dgkh) = _mla_proj_bwd(
        dq, dk, dv, cq, ckv, krw, pos, freq, masks, mla_q_norm_g, mla_kv_norm_g, wq_t, wkv_t, gqh, gkh, tm_mix)
    small["mla_q_head_g"], small["mla_k_head_g"] = dgqh[:, :QK], dgkh[:, :QK]
    grads["mla_w_q_up"] = dwq[:, :QK].astype(BF16).reshape(N_DEV, -1, rq)
    grads["mla_w_kv_up"] = dwkv.astype(BF16).reshape(N_DEV, -1, rkv)
    dzuv, small["gmlp_w_s"], dbs, small["gmlp_v_norm_g"], small["gmlp_out_g"] = _gmlp_bwd(
        dya, zuv, gmlp_v_norm_g, ws, bias, gouta, tm_mix)
    small["gmlp_b_s"] = dbs[:, :, 0]
    dx1, small["mix_norm_g"], dzc = _inproj_bwd([dzuv, dcq, dckv, dkrw], x1, mix_norm_g, win_t, dx2, splits,
                                                tm_mix)
    grads["w_in"] = _matmul_tn(dzc, hn, _col_block(dzc.shape[1], 768), d, tk, BF16, "dw_in",
                               out_rows=win_t.shape[0]).reshape(N_DEV, -1, d)
    res_b2 = red_b2.wait(grads["w_in"])
    red_am = _reduce_d2d([grads[n] for n in group["mix"]], "reduce_mix_d2d")

    def ffn1_dw(n, lhs, rhs, scale, token):
        return _matmul_tn(lhs, rhs, fp, d, tk, BF16, "dw_" + n, rhs_scale=scale, deps=(token,)).reshape(N_DEV, fp, d)

    gr = ffn1_dw("ffn1_w_down", h1, dx1, 0.5, red_am.token)
    red_ad = _reduce_d2d([gr], "reduce_ffn1_w_down_d2d", deps=(red_am.token,), rows=frows)
    dx0, small["ffn1_norm_g"], da1, db1 = _ffn_bwd(
        dx1, xf, tied(ffn1_norm_g, red_ad.token), gd1, sl1, full["ffn1_w_gate"], full["ffn1_w_up"],
        full["ffn1_w_down"], tm, "ffn1_bwd")
    chipm = chip_sums(group["mix"], red_am, dx0)
    red_bm = _reduce_ici(chipm, "reduce_mix_ici")
    red_b = [("ffn1_w_down", _reduce_ici(chip_sums(["ffn1_w_down"], red_ad, dx0, frows), "reduce_ffn1_w_down_ici",
                                         deps=(red_bm.token,), rows=frows))]
    rep = [n for n in WEIGHTS if n not in SHARDED]
    small_ici = _gather_ici([_pack([small[n] for n in rep] + [loss_part])], me, "gather_small_ici",
                            deps=(red_b[-1][1].token,))
    gr = ffn1_dw("ffn1_w_gate", da1, xn1, None, small_ici.token)
    red_ag = _reduce_d2d([gr], "reduce_ffn1_w_gate_d2d", rows=frows)
    red_b.append(("ffn1_w_gate", _reduce_ici(chip_sums(["ffn1_w_gate"], red_ag, red_ag.token, frows),
                                             "reduce_ffn1_w_gate_ici", rows=frows)))
    gr = ffn1_dw("ffn1_w_up", db1, xn1, None, red_b[-1][1].token)
    small_d2d = _gather_d2d(small_ici.wait(gr)[1:], "gather_small_d2d")
    red_au = _reduce_d2d([gr], "reduce_ffn1_w_up_d2d", deps=(small_d2d.token,), rows=frows)
    red_b.append(("ffn1_w_up", _reduce_ici(chip_sums(["ffn1_w_up"], red_au, red_au.token, frows),
                                           "reduce_ffn1_w_up_ici", rows=frows)))
    token = red_b[-1][1].token
    finish(group["ffn2"], res_b2[:3], res_b2[3:], token)
    res = red_bm.wait(outs_g[group["ffn2"][-1]])
    nm_ = len(group["mix"])
    finish(group["mix"], res[:nm_], res[nm_:], token)
    total = _sum_devices(small_d2d.wait(outs_g[group["mix"][-1]])[0])
    zero = jnp.zeros((1,), F32)
    dlt, nm, nv = _adamw(_pack([wts[n] for n in rep] + [zero]), total, _pack([mom1[n] for n in rep] + [zero]),
                         _pack([mom2[n] for n in rep] + [zero]), "adamw_small")
    shapes = [wts[n].shape for n in rep] + [(1,)]
    for n, g, dl, m1, m2 in zip(rep, _unpack(total, shapes), _unpack(dlt, shapes), _unpack(nm, shapes),
                                _unpack(nv, shapes)):
        outs_g[n], outs_d[n], outs_m[n], outs_v[n] = g, dl, m1, m2
    loss = _unpack(total, shapes)[-1].reshape(())
    after = dlt
    for n, ex in red_b:
        res = ex.wait(after)
        finish([n], res[:1], res[1:], token)
        after = outs_v[n]

    return (loss, dx0.reshape(b_loc, seq, d), *[outs_g[n] for n in WEIGHTS], *[outs_d[n] for n in WEIGHTS],
            *[outs_m[n] for n in WEIGHTS], *[outs_v[n] for n in WEIGHTS])
```

```python
import functools

import jax
import jax.numpy as jnp
from jax import lax
from jax.experimental import pallas as pl
from jax.experimental.pallas import tpu as pltpu

F32 = jnp.float32
BF16 = jnp.bfloat16
EPS = 1e-6
LANE = 128
SUBLANE = 8
N_DEV = 8
VMEM_LIMIT = 60 * 1024 * 1024
NOPE = 128
ROPE = 64
VHEAD = 128
QK = NOPE + ROPE
HEADW = 2 * LANE
CHUNK = 128
ROPE_THETA = 10000.0
ADAM_LR, ADAM_B1, ADAM_B2, ADAM_EPS, ADAM_WD, ADAM_STEP = 0.001, 0.9, 0.999, 1e-08, 0.01, 10
MESH = pl.DeviceIdType.MESH
ANY = pl.BlockSpec(memory_space=pl.ANY)
WHOLE_VMEM = pl.BlockSpec(memory_space=pltpu.VMEM)


def _params(sem=None):
    return pltpu.CompilerParams(dimension_semantics=sem, vmem_limit_bytes=VMEM_LIMIT)


def _round_up(n, m):
    return -(-n // m) * m


def _row_block(rows, target):
    best = rows
    for cand in range(SUBLANE, min(rows, target) + 1, SUBLANE):
        if rows % cand == 0:
            best = cand
    return best if best <= target else rows


def _nn(a, b):
    return jnp.dot(a, b, preferred_element_type=F32)


def _nt(a, b):
    return lax.dot_general(a, b, (((1,), (1,)), ((), ())), preferred_element_type=F32)


def _tn(a, b):
    return lax.dot_general(a, b, (((0,), (0,)), ((), ())), preferred_element_type=F32)


def _rstd(x, n):
    return lax.rsqrt(jnp.sum(x * x, axis=-1, keepdims=True) * (1.0 / n) + EPS)


def _rms_fwd(x, g, n):
    return x * _rstd(x, n) * g


def _rms_bwd(x, g, dy, n):
    r = _rstd(x, n)
    xh = x * r
    dyg = dy * g
    dx = r * (dyg - xh * (jnp.sum(dyg * xh, axis=-1, keepdims=True) * (1.0 / n)))
    return dx, jnp.sum(dy * xh, axis=0, keepdims=True)


def _gelu(x):
    return 0.5 * x * (1.0 + lax.erf(x * 0.7071067811865476))


def _gelu_grad(x):
    return 0.5 * (1.0 + lax.erf(x * 0.7071067811865476)) + x * jnp.exp(-0.5 * x * x) * 0.3989422804014327


def _ffn_fwd(base, xn, g, ids, wg_t, wu_t, wd, saved, tm, name):
    t, d = base.shape
    nb, fp, _ = wg_t.shape
    n = ids.shape[0]
    first = xn is None
    if saved is None:
        saved = [lax.empty((t, nb * fp), BF16) for _ in range(3)]

    def body(ids_ref, *refs):
        if first:
            base_ref, g_ref, wg_ref, wu_ref, wd_ref, _, _, _, out_ref, xn_ref, gd_ref, sl_ref, h_ref, acc = refs
        else:
            base_ref, xn_ref, wg_ref, wu_ref, wd_ref, _, _, _, out_ref, gd_ref, sl_ref, h_ref, acc = refs
        j = pl.program_id(1)

        @pl.when(j == 0)
        def _():
            if first:
                xn_ref[...] = _rms_fwd(base_ref[...], g_ref[...], d).astype(BF16)
            acc[...] = jnp.zeros_like(acc)

        xnb = xn_ref[...]
        a = _nt(xnb, wg_ref[0])
        b = _nt(xnb, wu_ref[0])
        s = jax.nn.sigmoid(a)
        sl = a * s
        h = (sl * b).astype(BF16)
        gd_ref[...] = (b * (s * (1.0 + a * (1.0 - s)))).astype(BF16)
        sl_ref[...] = sl.astype(BF16)
        h_ref[...] = h
        acc[...] += _nn(h, wd_ref[0])

        @pl.when(j == n - 1)
        def _():
            out_ref[...] = base_ref[...] + 0.5 * acc[...]

    wspec = pl.BlockSpec((1, fp, d), lambda i, j, ids_ref: (ids_ref[j], 0, 0))
    row = pl.BlockSpec((tm, d), lambda i, j, ids_ref: (i, 0))
    ff = pl.BlockSpec((tm, fp), lambda i, j, ids_ref: (i, ids_ref[j]))
    ffs = jax.ShapeDtypeStruct((t, nb * fp), BF16)
    second = pl.BlockSpec((1, d), lambda i, j, ids_ref: (0, 0)) if first else row
    n_row_outs = 2 if first else 1
    res = pl.pallas_call(
        body, name=name,
        grid_spec=pltpu.PrefetchScalarGridSpec(
            num_scalar_prefetch=1, grid=(t // tm, n),
            in_specs=[row, second, wspec, wspec, wspec, ANY, ANY, ANY],
            out_specs=[row] * n_row_outs + [ff, ff, ff],
            scratch_shapes=[pltpu.VMEM((tm, d), F32)]),
        out_shape=[jax.ShapeDtypeStruct((t, d), F32)] + ([jax.ShapeDtypeStruct((t, d), BF16)] if first else [])
        + [ffs, ffs, ffs],
        input_output_aliases={6 + k: n_row_outs + k for k in range(3)},
        compiler_params=_params(("arbitrary", "arbitrary")),
    )(ids, base, g if first else xn, wg_t, wu_t, wd, *saved)
    return (res[0], res[1] if first else xn, list(res[n_row_outs:]))


def _ffn_bwd(dout, x, g, gd, sl, wg_t, wu_t, wd, tm, name):
    t, d = x.shape
    nb, fp, _ = wg_t.shape

    def body(do_hbm, x_hbm, g_ref, gd_ref, sl_ref, wg_ref, wu_ref, wd_ref, wd_next_ref,
             dx_hbm, dg_ref, da_ref, db_ref, acc, rowbuf, dy_scr, dh_scr, sem):
        i, j = pl.program_id(0), pl.program_id(1)
        rows = pl.ds(pl.multiple_of(i * tm, tm), tm)

        def fetch(src):
            cp = pltpu.make_async_copy(src.at[rows, :], rowbuf, sem)
            cp.start()
            cp.wait()

        @pl.when(j == 0)
        def _():
            fetch(do_hbm)
            dy_scr[...] = (0.5 * rowbuf[...]).astype(BF16)
            acc[...] = jnp.zeros_like(acc)
            dh_scr[0] = _nt(dy_scr[...], wd_ref[0])

        @pl.when((i == 0) & (j == 0))
        def _():
            dg_ref[...] = jnp.zeros_like(dg_ref)

        dh = dh_scr[j % 2]
        dh_scr[(j + 1) % 2] = _nt(dy_scr[...], wd_next_ref[0])
        da = (dh * gd_ref[...].astype(F32)).astype(BF16)
        db = (dh * sl_ref[...].astype(F32)).astype(BF16)
        da_ref[...] = da
        db_ref[...] = db
        acc[...] += _nn(da, wg_ref[0]) + _nn(db, wu_ref[0])

        @pl.when(j == nb - 1)
        def _():
            fetch(x_hbm)
            dxn, dg = _rms_bwd(rowbuf[...], g_ref[...], acc[...], d)
            dg_ref[...] += dg
            acc[...] = dxn
            fetch(do_hbm)
            acc[...] += rowbuf[...]
            out = pltpu.make_async_copy(acc, dx_hbm.at[rows, :], sem)
            out.start()
            out.wait()

    wspec = pl.BlockSpec((1, fp, d), lambda i, j: (j, 0, 0))
    wnext = pl.BlockSpec((1, fp, d), lambda i, j: (jnp.minimum(j + 1, nb - 1), 0, 0))
    vec = pl.BlockSpec((1, d), lambda i, j: (0, 0))
    ff = pl.BlockSpec((tm, fp), lambda i, j: (i, j))
    ffs = jax.ShapeDtypeStruct((t, nb * fp), BF16)
    return pl.pallas_call(
        body, name=name, grid=(t // tm, nb),
        in_specs=[ANY, ANY, vec, ff, ff, wspec, wspec, pl.BlockSpec((1, fp, d), lambda i, j: (0, 0, 0)), wnext],
        out_specs=[ANY, vec, ff, ff],
        out_shape=[jax.ShapeDtypeStruct((t, d), F32), jax.ShapeDtypeStruct((1, d), F32), ffs, ffs],
        scratch_shapes=[pltpu.VMEM((tm, d), F32), pltpu.VMEM((tm, d), F32), pltpu.VMEM((tm, d), BF16),
                        pltpu.VMEM((2, tm, fp), F32), pltpu.SemaphoreType.DMA],
        compiler_params=_params(("arbitrary", "arbitrary")),
    )(dout, x, g, gd, sl, wg_t, wu_t, wd, wd)


def _matmul_tn(lhs, rhs, bm, bn, tk, out_dtype, name, rhs_scale=None, deps=(), out_rows=None):
    t, m = lhs.shape
    n = rhs.shape[1]
    nk = t // tk
    out_rows = m if out_rows is None else out_rows

    def body(l_ref, r_ref, *refs):
        o_ref, acc = refs[len(deps):]
        k = pl.program_id(2)

        @pl.when(k == 0)
        def _():
            acc[...] = jnp.zeros_like(acc)

        r = r_ref[...] if rhs_scale is None else rhs_scale * r_ref[...]
        acc[...] += _tn(l_ref[...].astype(BF16), r.astype(BF16))

        @pl.when(k == nk - 1)
        def _():
            o_ref[...] = acc[...].astype(out_dtype)

    return pl.pallas_call(
        body, name=name, grid=(m // bm, n // bn, nk),
        in_specs=[pl.BlockSpec((tk, bm), lambda i, j, k: (k, i)), pl.BlockSpec((tk, bn), lambda i, j, k: (k, j))]
        + [ANY] * len(deps),
        out_specs=pl.BlockSpec((bm, bn), lambda i, j, k: (i, j)),
        out_shape=jax.ShapeDtypeStruct((out_rows, n), out_dtype),
        scratch_shapes=[pltpu.VMEM((bm, bn), F32)],
        compiler_params=_params(("arbitrary", "arbitrary", "arbitrary")),
    )(lhs, rhs, *deps)


def _last_rows_padded(w_ref, tail_ref, off, real):
    @pl.when(pl.program_id(0) == 0)
    def _():
        tail_ref[...] = jnp.zeros_like(tail_ref)
        tail_ref[0:real, :] = w_ref[off:off + real, :]


def _inproj_fwd(x, g, w_t, splits, tm):
    t, d = x.shape
    offs = [sum(splits[:k]) for k in range(len(splits))]
    real_last = w_t.shape[0] - offs[-1]

    def body(x_ref, g_ref, w_ref, hn_ref, *refs):
        z_refs, tail_ref = refs[:-1], refs[-1]
        _last_rows_padded(w_ref, tail_ref, offs[-1], real_last)
        hn = _rms_fwd(x_ref[...], g_ref[...], d).astype(BF16)
        hn_ref[...] = hn
        for z_ref, o, n in zip(z_refs[:-1], offs, splits):
            z_ref[...] = _nt(hn, w_ref[o:o + n, :])
        z_refs[-1][...] = _nt(hn, tail_ref[...])

    row = pl.BlockSpec((tm, d), lambda i: (i, 0))
    return pl.pallas_call(
        body, name="inproj_fwd", grid=(t // tm,),
        in_specs=[row, pl.BlockSpec((1, d), lambda i: (0, 0)), WHOLE_VMEM],
        out_specs=[row] + [pl.BlockSpec((tm, n), lambda i: (i, 0)) for n in splits],
        out_shape=[jax.ShapeDtypeStruct((t, d), BF16)] + [jax.ShapeDtypeStruct((t, n), F32) for n in splits],
        scratch_shapes=[pltpu.VMEM((splits[-1], d), BF16)],
        compiler_params=_params(("arbitrary",)),
    )(x, g, w_t)


def _inproj_bwd(dzs, x, g, w_t, dres, splits, tm):
    t, d = x.shape
    offs = [sum(splits[:k]) for k in range(len(splits))]
    ni = sum(splits)
    nz = len(splits)
    real_last = w_t.shape[0] - offs[-1]

    def body(*refs):
        dz_refs = refs[:nz]
        x_ref, g_ref, w_ref, dres_ref, dx_ref, dg_ref, dzc_ref, tail_ref = refs[nz:]
        _last_rows_padded(w_ref, tail_ref, offs[-1], real_last)
        dhn = jnp.zeros((tm, d), F32)
        for k, (dz_ref, o, n) in enumerate(zip(dz_refs, offs, splits)):
            dz = dz_ref[...].astype(BF16)
            dzc_ref[:, o:o + n] = dz
            dhn += _nn(dz, tail_ref[...] if k == nz - 1 else w_ref[o:o + n, :])
        dx, dg = _rms_bwd(x_ref[...], g_ref[...], dhn, d)
        dx_ref[...] = dres_ref[...] + dx

        @pl.when(pl.program_id(0) == 0)
        def _():
            dg_ref[...] = jnp.zeros_like(dg_ref)

        dg_ref[...] += dg

    row = pl.BlockSpec((tm, d), lambda i: (i, 0))
    vec = pl.BlockSpec((1, d), lambda i: (0, 0))
    return pl.pallas_call(
        body, name="inproj_bwd", grid=(t // tm,),
        in_specs=[pl.BlockSpec((tm, n), lambda i: (i, 0)) for n in splits] + [row, vec, WHOLE_VMEM, row],
        out_specs=[row, vec, pl.BlockSpec((tm, ni), lambda i: (i, 0))],
        out_shape=[jax.ShapeDtypeStruct((t, d), F32), jax.ShapeDtypeStruct((1, d), F32),
                   jax.ShapeDtypeStruct((t, ni), BF16)],
        scratch_shapes=[pltpu.VMEM((splits[-1], d), BF16)],
        compiler_params=_params(("arbitrary",)),
    )(*dzs, x, g, w_t, dres)


def _tril_bf16(ws_ref, grp):
    rows = lax.broadcasted_iota(jnp.int32, (CHUNK, CHUNK), 0)
    cols = lax.broadcasted_iota(jnp.int32, (CHUNK, CHUNK), 1)
    return jnp.where(rows >= cols, ws_ref[grp], 0.0).astype(BF16)


def _gmlp_mix(zuv_ref, gv_ref, ws_ref, bias_ref, v_scr, mixed_scr, tm, w, groups):
    u = _gelu(zuv_ref[:, 0:w])
    v0 = _gelu(zuv_ref[:, w:2 * w])
    v_scr[...] = _rms_fwd(v0, gv_ref[...], w).astype(BF16)
    for grp in range(groups):
        wsm = _tril_bf16(ws_ref, grp)
        lanes = slice(grp * CHUNK, (grp + 1) * CHUNK)
        for c in range(tm // CHUNK):
            rows = slice(c * CHUNK, (c + 1) * CHUNK)
            mixed_scr[rows, lanes] = _nn(wsm, v_scr[rows, lanes]) + bias_ref[:, lanes]
    return u, v0


def _gmlp_fwd(zuv, gv, ws, bias, gout, tm):
    t, w2 = zuv.shape
    w = w2 // 2
    groups = ws.shape[0]

    def body(zuv_ref, gv_ref, ws_ref, bias_ref, go_ref, y_ref, v_scr, mixed_scr):
        u, _ = _gmlp_mix(zuv_ref, gv_ref, ws_ref, bias_ref, v_scr, mixed_scr, tm, w, groups)
        ya = u * mixed_scr[...]
        for grp in range(groups):
            lanes = slice(grp * CHUNK, (grp + 1) * CHUNK)
            y_ref[:, lanes] = _rms_fwd(ya[:, lanes], go_ref[:, lanes], CHUNK).astype(BF16)

    const2 = lambda i: (0, 0)
    return pl.pallas_call(
        body, name="gmlp_fwd", grid=(t // tm,),
        in_specs=[pl.BlockSpec((tm, w2), lambda i: (i, 0)), pl.BlockSpec((1, w), const2),
                  pl.BlockSpec((groups, CHUNK, CHUNK), lambda i: (0, 0, 0)),
                  pl.BlockSpec((CHUNK, w), const2), pl.BlockSpec((1, w), const2)],
        out_specs=pl.BlockSpec((tm, w), lambda i: (i, 0)),
        out_shape=jax.ShapeDtypeStruct((t, w), BF16),
        scratch_shapes=[pltpu.VMEM((tm, w), BF16), pltpu.VMEM((tm, w), F32)],
        compiler_params=_params(("arbitrary",)),
    )(zuv, gv, ws, bias, gout)


def _gmlp_bwd(dy, zuv, gv, ws, bias, gout, tm):
    t, w2 = zuv.shape
    w = w2 // 2
    groups = ws.shape[0]

    def body(dy_ref, zuv_ref, gv_ref, ws_ref, bias_ref, go_ref,
             dz_ref, dws_ref, dbias_ref, dgv_ref, dgo_ref, v_scr, mixed_scr, dmix_scr, dv_scr):
        @pl.when(pl.program_id(0) == 0)
        def _():
            dws_ref[...] = jnp.zeros_like(dws_ref)
            dbias_ref[...] = jnp.zeros_like(dbias_ref)
            dgv_ref[...] = jnp.zeros_like(dgv_ref)
            dgo_ref[...] = jnp.zeros_like(dgo_ref)

        u, v0 = _gmlp_mix(zuv_ref, gv_ref, ws_ref, bias_ref, v_scr, mixed_scr, tm, w, groups)
        mixed = mixed_scr[...]
        ya = u * mixed
        for grp in range(groups):
            lanes = slice(grp * CHUNK, (grp + 1) * CHUNK)
            dya, dgo = _rms_bwd(ya[:, lanes], go_ref[:, lanes], dy_ref[:, lanes], CHUNK)
            dgo_ref[:, lanes] += dgo
            dz_ref[:, lanes] = dya * mixed[:, lanes] * _gelu_grad(zuv_ref[:, lanes])
            dmix_scr[:, lanes] = dya * u[:, lanes]
        for grp in range(groups):
            wsm = _tril_bf16(ws_ref, grp)
            lanes = slice(grp * CHUNK, (grp + 1) * CHUNK)
            dws = jnp.zeros((CHUNK, CHUNK), F32)
            dbias = jnp.zeros((CHUNK, CHUNK), F32)
            for c in range(tm // CHUNK):
                rows = slice(c * CHUNK, (c + 1) * CHUNK)
                dm = dmix_scr[rows, lanes]
                dmb = dm.astype(BF16)
                dv_scr[rows, lanes] = _tn(wsm, dmb)
                dws += _nt(dmb, v_scr[rows, lanes])
                dbias += dm
            rr = lax.broadcasted_iota(jnp.int32, (CHUNK, CHUNK), 0)
            cc = lax.broadcasted_iota(jnp.int32, (CHUNK, CHUNK), 1)
            dws_ref[grp] += jnp.where(rr >= cc, dws, 0.0)
            dbias_ref[grp] += jnp.sum(dbias, axis=1, keepdims=True)
        dv0, dgv = _rms_bwd(v0, gv_ref[...], dv_scr[...], w)
        dgv_ref[...] += dgv
        dz_ref[:, w:2 * w] = dv0 * _gelu_grad(zuv_ref[:, w:2 * w])

    const2 = lambda i: (0, 0)
    const3 = lambda i: (0, 0, 0)
    return pl.pallas_call(
        body, name="gmlp_bwd", grid=(t // tm,),
        in_specs=[pl.BlockSpec((tm, w), lambda i: (i, 0)), pl.BlockSpec((tm, w2), lambda i: (i, 0)),
                  pl.BlockSpec((1, w), const2), pl.BlockSpec((groups, CHUNK, CHUNK), const3),
                  pl.BlockSpec((CHUNK, w), const2), pl.BlockSpec((1, w), const2)],
        out_specs=[pl.BlockSpec((tm, w2), lambda i: (i, 0)), pl.BlockSpec((groups, CHUNK, CHUNK), const3),
                   pl.BlockSpec((groups, CHUNK, 1), const3), pl.BlockSpec((1, w), const2), pl.BlockSpec((1, w), const2)],
        out_shape=[jax.ShapeDtypeStruct((t, w2), F32), jax.ShapeDtypeStruct((groups, CHUNK, CHUNK), F32),
                   jax.ShapeDtypeStruct((groups, CHUNK, 1), F32), jax.ShapeDtypeStruct((1, w), F32),
                   jax.ShapeDtypeStruct((1, w), F32)],
        scratch_shapes=[pltpu.VMEM((tm, w), BF16), pltpu.VMEM((tm, w), F32),
                        pltpu.VMEM((tm, w), F32), pltpu.VMEM((tm, w), F32)],
        compiler_params=_params(("arbitrary",)),
    )(dy, zuv, gv, ws, bias, gout)


def _rot(x, m_lo, m_hi):
    return pltpu.roll(x, LANE - ROPE // 2, 1) * m_lo + pltpu.roll(x, ROPE // 2, 1) * m_hi


def _rope_tables(pos_ref, freq_ref):
    ang = pos_ref[...] * freq_ref[...]
    return jnp.cos(ang), jnp.sin(ang)


def _mla_proj_fwd(cq, ckv, krw, pos, freq, masks, gq, gkv, wq_t, wkv_t, gqh, gkh, tm):
    t, rq = cq.shape
    rkv = ckv.shape[1]
    heads = wq_t.shape[0]

    def body(cq_ref, ckv_ref, kr_ref, pos_ref, freq_ref, mk_ref, gq_ref, gkv_ref, wq_ref, wkv_ref,
             gqh_ref, gkh_ref, q_ref, k_ref, v_ref):
        cos, sin = _rope_tables(pos_ref, freq_ref)
        m_lo, m_hi = mk_ref[0:1, :], mk_ref[1:2, :]
        cqn = _rms_fwd(cq_ref[...], gq_ref[...], rq).astype(BF16)
        ckvn = _rms_fwd(ckv_ref[...], gkv_ref[...], rkv).astype(BF16)
        kr = kr_ref[...]
        kr_ss = jnp.sum(kr * kr, axis=-1, keepdims=True)
        for h in range(heads):
            qh = _nt(cqn, wq_ref[h])
            qn = qh * _rstd(qh, QK) * gqh_ref[...]
            qr = qn[:, LANE:]
            q_ref[h, :, 0:LANE] = qn[:, 0:LANE].astype(BF16)
            q_ref[h, :, LANE:] = (qr * cos + _rot(qr, m_lo, m_hi) * sin).astype(BF16)
            kvh = _nt(ckvn, wkv_ref[h])
            kn = kvh[:, 0:LANE]
            rk = lax.rsqrt((jnp.sum(kn * kn, axis=-1, keepdims=True) + kr_ss) * (1.0 / QK) + EPS)
            k_ref[h, :, 0:LANE] = (kn * rk * gkh_ref[:, 0:LANE]).astype(BF16)
            krn = kr * rk * gkh_ref[:, LANE:]
            k_ref[h, :, LANE:] = (krn * cos + _rot(krn, m_lo, m_hi) * sin).astype(BF16)
            v_ref[h] = kvh[:, LANE:].astype(BF16)

    c2 = lambda i: (0, 0)
    c3 = lambda i: (0, 0, 0)
    return pl.pallas_call(
        body, name="mla_proj_fwd", grid=(t // tm,),
        in_specs=[pl.BlockSpec((tm, rq), lambda i: (i, 0)), pl.BlockSpec((tm, rkv), lambda i: (i, 0)),
                  pl.BlockSpec((tm, LANE), lambda i: (i, 0)), pl.BlockSpec((tm, 1), lambda i: (i, 0)),
                  pl.BlockSpec((1, LANE), c2), pl.BlockSpec((2, LANE), c2),
                  pl.BlockSpec((1, rq), c2), pl.BlockSpec((1, rkv), c2),
                  pl.BlockSpec((heads, HEADW, rq), c3), pl.BlockSpec((heads, HEADW, rkv), c3),
                  pl.BlockSpec((1, HEADW), c2), pl.BlockSpec((1, HEADW), c2)],
        out_specs=[pl.BlockSpec((heads, tm, HEADW), lambda i: (0, i, 0)),
                   pl.BlockSpec((heads, tm, HEADW), lambda i: (0, i, 0)),
                   pl.BlockSpec((heads, tm, VHEAD), lambda i: (0, i, 0))],
        out_shape=[jax.ShapeDtypeStruct((heads, t, HEADW), BF16), jax.ShapeDtypeStruct((heads, t, HEADW), BF16),
                   jax.ShapeDtypeStruct((heads, t, VHEAD), BF16)],
        compiler_params=_params(("arbitrary",)),
    )(cq, ckv, krw, pos, freq, masks, gq, gkv, wq_t, wkv_t, gqh, gkh)


def _mla_proj_bwd(dq, dk, dv, cq, ckv, krw, pos, freq, masks, gq, gkv, wq_t, wkv_t, gqh, gkh, tm):
    t, rq = cq.shape
    rkv = ckv.shape[1]
    heads = wq_t.shape[0]

    def body(dq_ref, dk_ref, dv_ref, cq_ref, ckv_ref, kr_ref, pos_ref, freq_ref, mk_ref, gq_ref, gkv_ref,
             wq_ref, wkv_ref, gqh_ref, gkh_ref,
             dcq_ref, dckv_ref, dkr_ref, dwq_ref, dwkv_ref, dgq_ref, dgkv_ref, dgqh_ref, dgkh_ref):
        @pl.when(pl.program_id(0) == 0)
        def _():
            for r in (dwq_ref, dwkv_ref, dgq_ref, dgkv_ref, dgqh_ref, dgkh_ref):
                r[...] = jnp.zeros_like(r)

        cos, sin = _rope_tables(pos_ref, freq_ref)
        m_lo, m_hi = mk_ref[0:1, :], mk_ref[1:2, :]

        def unrope(dy):
            return dy * cos - _rot(dy * sin, m_lo, m_hi)

        cqn = _rms_fwd(cq_ref[...], gq_ref[...], rq).astype(BF16)
        ckvn = _rms_fwd(ckv_ref[...], gkv_ref[...], rkv).astype(BF16)
        kr = kr_ref[...]
        kr_ss = jnp.sum(kr * kr, axis=-1, keepdims=True)
        dcqn = jnp.zeros((tm, rq), F32)
        dckvn = jnp.zeros((tm, rkv), F32)
        dkr = jnp.zeros((tm, LANE), F32)
        for h in range(heads):
            qh = _nt(cqn, wq_ref[h])
            dqn = jnp.concatenate([dq_ref[h, :, 0:LANE], unrope(dq_ref[h, :, LANE:])], axis=1)
            dqh, dg = _rms_bwd(qh, gqh_ref[...], dqn, QK)
            dgqh_ref[...] += dg
            dqh = dqh.astype(BF16)
            dcqn += _nn(dqh, wq_ref[h])
            dwq_ref[h] += _tn(dqh, cqn)

            kvh = _nt(ckvn, wkv_ref[h])
            kn = kvh[:, 0:LANE]
            rk = lax.rsqrt((jnp.sum(kn * kn, axis=-1, keepdims=True) + kr_ss) * (1.0 / QK) + EPS)
            dkn_n = dk_ref[h, :, 0:LANE]
            dkr_n = unrope(dk_ref[h, :, LANE:])
            knh, krh = kn * rk, kr * rk
            dgkh_ref[:, 0:LANE] += jnp.sum(dkn_n * knh, axis=0, keepdims=True)
            dgkh_ref[:, LANE:] += jnp.sum(dkr_n * krh, axis=0, keepdims=True)
            dkn_g, dkr_g = dkn_n * gkh_ref[:, 0:LANE], dkr_n * gkh_ref[:, LANE:]
            proj = (jnp.sum(dkn_g * knh, axis=-1, keepdims=True)
                    + jnp.sum(dkr_g * krh, axis=-1, keepdims=True)) * (1.0 / QK)
            dkr += rk * (dkr_g - krh * proj)
            dkvh = jnp.concatenate([rk * (dkn_g - knh * proj), dv_ref[h]], axis=1).astype(BF16)
            dckvn += _nn(dkvh, wkv_ref[h])
            dwkv_ref[h] += _tn(dkvh, ckvn)
        dkr_ref[...] = dkr
        dcq, dg = _rms_bwd(cq_ref[...], gq_ref[...], dcqn, rq)
        dcq_ref[...] = dcq
        dgq_ref[...] += dg
        dckv, dg = _rms_bwd(ckv_ref[...], gkv_ref[...], dckvn, rkv)
        dckv_ref[...] = dckv
        dgkv_ref[...] += dg

    c2 = lambda i: (0, 0)
    c3 = lambda i: (0, 0, 0)
    hq = pl.BlockSpec((heads, tm, HEADW), lambda i: (0, i, 0))
    return pl.pallas_call(
        body, name="mla_proj_bwd", grid=(t // tm,),
        in_specs=[hq, hq, pl.BlockSpec((heads, tm, VHEAD), lambda i: (0, i, 0)),
                  pl.BlockSpec((tm, rq), lambda i: (i, 0)), pl.BlockSpec((tm, rkv), lambda i: (i, 0)),
                  pl.BlockSpec((tm, LANE), lambda i: (i, 0)), pl.BlockSpec((tm, 1), lambda i: (i, 0)),
                  pl.BlockSpec((1, LANE), c2), pl.BlockSpec((2, LANE), c2),
                  pl.BlockSpec((1, rq), c2), pl.BlockSpec((1, rkv), c2),
                  pl.BlockSpec((heads, HEADW, rq), c3), pl.BlockSpec((heads, HEADW, rkv), c3),
                  pl.BlockSpec((1, HEADW), c2), pl.BlockSpec((1, HEADW), c2)],
        out_specs=[pl.BlockSpec((tm, rq), lambda i: (i, 0)), pl.BlockSpec((tm, rkv), lambda i: (i, 0)),
                   pl.BlockSpec((tm, LANE), lambda i: (i, 0)),
                   pl.BlockSpec((heads, HEADW, rq), c3), pl.BlockSpec((heads, HEADW, rkv), c3),
                   pl.BlockSpec((1, rq), c2), pl.BlockSpec((1, rkv), c2),
                   pl.BlockSpec((1, HEADW), c2), pl.BlockSpec((1, HEADW), c2)],
        out_shape=[jax.ShapeDtypeStruct((t, rq), F32), jax.ShapeDtypeStruct((t, rkv), F32),
                   jax.ShapeDtypeStruct((t, LANE), F32),
                   jax.ShapeDtypeStruct((heads, HEADW, rq), F32), jax.ShapeDtypeStruct((heads, HEADW, rkv), F32),
                   jax.ShapeDtypeStruct((1, rq), F32), jax.ShapeDtypeStruct((1, rkv), F32),
                   jax.ShapeDtypeStruct((1, HEADW), F32), jax.ShapeDtypeStruct((1, HEADW), F32)],
        compiler_params=_params(("arbitrary",)),
    )(dq, dk, dv, cq, ckv, krw, pos, freq, masks, gq, gkv, wq_t, wkv_t, gqh, gkh)


def _lower_triangle(blk):
    return lax.broadcasted_iota(jnp.int32, (blk, blk), 0) >= lax.broadcasted_iota(jnp.int32, (blk, blk), 1)


def _attn_fwd(q, k, v, seq, blk):
    heads, t, _ = q.shape
    scale = QK ** -0.5
    nblk = seq // blk

    def body(q_ref, k_ref, v_ref, o_ref, lse_ref):
        tri = _lower_triangle(blk)
        for qi in range(nblk):
            rows = slice(qi * blk, (qi + 1) * blk)
            before = slice(0, qi * blk)
            qb = q_ref[0, rows, :]
            s_d = jnp.where(tri, _nt(qb, k_ref[0, rows, :]) * scale, -1e30)
            m = jnp.max(s_d, axis=-1, keepdims=True)
            if qi:
                s_b = _nt(qb, k_ref[0, before, :]) * scale
                m = jnp.maximum(m, jnp.max(s_b, axis=-1, keepdims=True))
                p_b = jnp.exp(s_b - m)
            p_d = jnp.exp(s_d - m)
            l = jnp.sum(p_d, axis=-1, keepdims=True)
            acc = _nn(p_d.astype(BF16), v_ref[0, rows, :])
            if qi:
                l += jnp.sum(p_b, axis=-1, keepdims=True)
                acc += _nn(p_b.astype(BF16), v_ref[0, before, :])
            o_ref[0, rows, :] = acc / l
            lse_ref[0, rows, :] = m + jnp.log(l)

    return pl.pallas_call(
        body, name="attn_fwd", grid=(heads, t // seq),
        in_specs=[pl.BlockSpec((1, seq, HEADW), lambda h, b: (h, b, 0)),
                  pl.BlockSpec((1, seq, HEADW), lambda h, b: (h, b, 0)),
                  pl.BlockSpec((1, seq, VHEAD), lambda h, b: (h, b, 0))],
        out_specs=[pl.BlockSpec((1, seq, VHEAD), lambda h, b: (h, b, 0)),
                   pl.BlockSpec((1, seq, 1), lambda h, b: (h, b, 0))],
        out_shape=[jax.ShapeDtypeStruct((heads, t, VHEAD), F32), jax.ShapeDtypeStruct((heads, t, 1), F32)],
        compiler_params=_params(("arbitrary", "arbitrary")),
    )(q, k, v)


def _attn_bwd(q, k, v, do, lse, delta, seq, blk, after):
    heads, t, _ = q.shape
    scale = QK ** -0.5
    nblk = seq // blk

    def body(q_ref, k_ref, v_ref, do_ref, lse_ref, dl_ref, _, dq_ref, dk_ref, dv_ref):
        tri = _lower_triangle(blk)
        dk_ref[...] = jnp.zeros_like(dk_ref)
        dv_ref[...] = jnp.zeros_like(dv_ref)
        for qi in range(nblk):
            rows = slice(qi * blk, (qi + 1) * blk)
            qb = q_ref[0, rows, :]
            dob = do_ref[0, rows, :]
            lse_b = lse_ref[0, rows, :]
            dl_b = dl_ref[0, rows, :]
            dq = jnp.zeros((blk, HEADW), F32)
            for keys, masked in ((slice(0, qi * blk), False), (rows, True)):
                if keys.stop == keys.start:
                    continue
                kb = k_ref[0, keys, :]
                p = jnp.exp(_nt(qb, kb) * scale - lse_b)
                if masked:
                    p = jnp.where(tri, p, 0.0)
                dp = _nt(dob, v_ref[0, keys, :])
                ds = (p * (dp - dl_b) * scale).astype(BF16)
                dv_ref[0, keys, :] += _tn(p.astype(BF16), dob)
                dk_ref[0, keys, :] += _tn(ds, qb)
                dq += _nn(ds, kb)
            dq_ref[0, rows, :] = dq

    hq = pl.BlockSpec((1, seq, HEADW), lambda h, b: (h, b, 0))
    hv = pl.BlockSpec((1, seq, VHEAD), lambda h, b: (h, b, 0))
    h1 = pl.BlockSpec((1, seq, 1), lambda h, b: (h, b, 0))
    return pl.pallas_call(
        body, name="attn_bwd", grid=(heads, t // seq),
        in_specs=[hq, hq, hv, hv, h1, h1, ANY],
        out_specs=[hq, hq, hv],
        out_shape=[jax.ShapeDtypeStruct((heads, t, HEADW), F32), jax.ShapeDtypeStruct((heads, t, HEADW), F32),
                   jax.ShapeDtypeStruct((heads, t, VHEAD), F32)],
        compiler_params=_params(("arbitrary", "arbitrary")),
    )(q, k, v, do, lse, delta, after)


def _out_fwd(ya, o, gb, w_out, x1, tm):
    t, w = ya.shape
    heads = o.shape[0]
    d = x1.shape[1]

    def body(ya_ref, o_ref, gb_ref, w_ref, x1_ref, x2_ref, yc_ref):
        yc_ref[:, 0:w] = ya_ref[...]
        for h in range(heads):
            lanes = slice(h * VHEAD, (h + 1) * VHEAD)
            yc_ref[:, w + h * VHEAD:w + (h + 1) * VHEAD] = _rms_fwd(o_ref[h], gb_ref[:, lanes], VHEAD).astype(BF16)
        x2_ref[...] = x1_ref[...] + _nn(yc_ref[...], w_ref[...])

    wy = w + heads * VHEAD
    row = pl.BlockSpec((tm, d), lambda i: (i, 0))
    return pl.pallas_call(
        body, name="out_fwd", grid=(t // tm,),
        in_specs=[pl.BlockSpec((tm, w), lambda i: (i, 0)), pl.BlockSpec((heads, tm, VHEAD), lambda i: (0, i, 0)),
                  pl.BlockSpec((1, heads * VHEAD), lambda i: (0, 0)), WHOLE_VMEM, row],
        out_specs=[row, pl.BlockSpec((tm, wy), lambda i: (i, 0))],
        out_shape=[jax.ShapeDtypeStruct((t, d), F32), jax.ShapeDtypeStruct((t, wy), BF16)],
        compiler_params=_params(("arbitrary",)),
    )(ya, o, gb, w_out, x1)


def _out_bwd(dx2, o, gb, w_out, w, tm):
    t, d = dx2.shape
    heads = o.shape[0]

    def body(dx_ref, o_ref, gb_ref, w_ref, dya_ref, do_ref, dl_ref, dgb_ref):
        @pl.when(pl.program_id(0) == 0)
        def _():
            dgb_ref[...] = jnp.zeros_like(dgb_ref)

        dyc = _nt(dx_ref[...].astype(BF16), w_ref[...])
        dya_ref[...] = dyc[:, 0:w]
        for h in range(heads):
            lanes = slice(h * VHEAD, (h + 1) * VHEAD)
            oh = o_ref[h]
            doh, dg = _rms_bwd(oh, gb_ref[:, lanes], dyc[:, w + h * VHEAD:w + (h + 1) * VHEAD], VHEAD)
            dgb_ref[:, lanes] += dg
            do_ref[h] = doh.astype(BF16)
            dl_ref[h] = jnp.sum(doh * oh, axis=-1, keepdims=True)

    ho = pl.BlockSpec((heads, tm, VHEAD), lambda i: (0, i, 0))
    vec = pl.BlockSpec((1, heads * VHEAD), lambda i: (0, 0))
    return pl.pallas_call(
        body, name="out_bwd", grid=(t // tm,),
        in_specs=[pl.BlockSpec((tm, d), lambda i: (i, 0)), ho, vec, WHOLE_VMEM],
        out_specs=[pl.BlockSpec((tm, w), lambda i: (i, 0)), ho, pl.BlockSpec((heads, tm, 1), lambda i: (0, i, 0)), vec],
        out_shape=[jax.ShapeDtypeStruct((t, w), F32), jax.ShapeDtypeStruct((heads, t, VHEAD), BF16),
                   jax.ShapeDtypeStruct((heads, t, 1), F32), jax.ShapeDtypeStruct((1, heads * VHEAD), F32)],
        compiler_params=_params(("arbitrary",)),
    )(dx2, o, gb, w_out)


def _loss_head(y, target, tm):
    t, d = y.shape

    def body(y_ref, t_ref, dy_ref, loss_ref):
        @pl.when(pl.program_id(0) == 0)
        def _():
            loss_ref[...] = jnp.zeros_like(loss_ref)

        err = y_ref[...] - t_ref[...]
        dy_ref[...] = err * (1.0 / d)
        part = jnp.sum(jnp.sum(err * err, axis=-1, keepdims=True) * (1.0 / d), axis=0, keepdims=True)
        loss_ref[...] += 0.5 * part

    row = pl.BlockSpec((tm, d), lambda i: (i, 0))
    return pl.pallas_call(
        body, name="loss_head", grid=(t // tm,),
        in_specs=[row, row], out_specs=[row, pl.BlockSpec((1, 1), lambda i: (0, 0))],
        out_shape=[jax.ShapeDtypeStruct((t, d), F32), jax.ShapeDtypeStruct((1, 1), F32)],
        compiler_params=_params(("arbitrary",)),
    )(y, target)


def _place():
    return lax.axis_index("x"), lax.axis_index("y"), lax.axis_index("c")


HBM = pl.BlockSpec(memory_space=pltpu.HBM)
SEM = pl.BlockSpec(memory_space=pltpu.SEMAPHORE)
DATAFLOW = pltpu.SideEffectType.DATAFLOW_SIDE_EFFECTING


def _plan_copies(plan, refs, send_sems, recv_sems):
    def block(ref, blk):
        if blk is None:
            return ref
        return ref.at[blk[0], pl.ds(0, blk[1])] if isinstance(blk, tuple) else ref.at[blk]

    cps = []
    for i, (sb, sblk, db, dblk, dev) in enumerate(plan(*_place())):
        cps.append(pltpu.make_async_remote_copy(
            src_ref=block(refs[sb], sblk), dst_ref=block(refs[db], dblk),
            send_sem=send_sems.at[i], recv_sem=recv_sems.at[i], device_id=dev, device_id_type=MESH))
    return cps


def _push_start(bufs, plan, ncopy, name, deps=()):
    nb = len(bufs)

    def body(*refs):
        outs = refs[nb + len(deps):]
        for cp in _plan_copies(plan, refs[:nb], outs[0], outs[1]):
            cp.start()
        outs[-1][...] = jnp.zeros_like(outs[-1])

    res = pl.pallas_call(
        body, name=name,
        out_shape=(pltpu.SemaphoreType.DMA((ncopy,)), pltpu.SemaphoreType.DMA((ncopy,)),
                   *[pltpu.HBM(b.shape, b.dtype) for b in bufs], jax.ShapeDtypeStruct((SUBLANE, LANE), F32)),
        in_specs=[HBM] * nb + [ANY] * len(deps),
        out_specs=(SEM, SEM, *[HBM] * nb, WHOLE_VMEM),
        input_output_aliases={i: 2 + i for i in range(nb)},
        compiler_params=pltpu.CompilerParams(has_side_effects=DATAFLOW),
    )(*[pltpu.with_memory_space_constraint(b, pltpu.HBM) for b in bufs], *deps)
    return res[0], res[1], list(res[2:2 + nb]), res[-1]


def _push_wait(send_sems, recv_sems, bufs, plan, after, name):
    nb = len(bufs)

    def body(*refs):
        for cp in _plan_copies(plan, refs[:nb], refs[nb], refs[nb + 1]):
            cp.wait_send()
            cp.wait_recv()

    res = pl.pallas_call(
        body, name=name,
        out_shape=[pltpu.HBM(b.shape, b.dtype) for b in bufs],
        in_specs=[HBM] * nb + [SEM, SEM, ANY], out_specs=[HBM] * nb,
        input_output_aliases={i: i for i in range(nb)},
        compiler_params=pltpu.CompilerParams(has_side_effects=DATAFLOW),
    )(*bufs, send_sems, recv_sems, after)
    return list(res)


def _other_chips(x, y):
    return ((1 - x, y), (x, 1 - y), (1 - x, 1 - y))


class _Exchange:
    def __init__(self, bufs, plan, ncopy, name, deps=()):
        self.plan, self.name = plan, name
        self.send, self.recv, self.bufs, self.token = _push_start(bufs, plan, ncopy, name + "_start", deps)

    def wait(self, after):
        return _push_wait(self.send, self.recv, self.bufs, self.plan, after, self.name + "_wait")


class _Chain:
    def __init__(self, bufs):
        self.bufs = list(bufs)

    def start(self, plan, ncopy, name, deps=()):
        send, recv, self.bufs, token = _push_start(self.bufs, plan, ncopy, name + "_start", deps)
        return (send, recv, plan, name), token

    def wait(self, pending, after):
        send, recv, plan, name = pending
        self.bufs = _push_wait(send, recv, self.bufs, plan, after, name + "_wait")


class _StagedGather:
    def __init__(self, shards, me, name, pad_to=None):
        self.n = n = len(shards)
        self.name = name
        rows = shards[0].shape[0]
        lands = []
        for s in shards:
            land = lax.empty((N_DEV, pad_to or rows) + s.shape[1:], s.dtype)
            if pad_to and pad_to != rows:
                land = lax.dynamic_update_slice(
                    land, jnp.zeros((N_DEV, pad_to - rows) + s.shape[1:], s.dtype), (0, rows, 0))
            lands.append(lax.dynamic_update_slice(land, s[None], (me, 0, 0)))
        self.chain = _Chain(list(shards) + lands)
        self.pending = {}

        def blk(b):
            return (b, rows) if pad_to and pad_to != rows else b

        def to_sibling(blocks):
            return lambda x, y, c: [(n + a, blk(b), n + a, blk(b), (x, y, 1 - c))
                                    for a in range(n) for b in blocks(x, y, c)]

        def nbr_blocks(x, y, c):
            return [4 * (1 - x) + 2 * y + c, 4 * x + 2 * (1 - y) + c]

        def diag(x, y, c):
            sx, sy = (1 - x) * (1 - c) + x * c, y * (1 - c) + (1 - y) * c
            tx, ty = x * (1 - c) + (1 - x) * c, (1 - y) * (1 - c) + y * c
            b = blk(4 * sx + 2 * sy + c)
            return [(n + a, b, n + a, b, (tx, ty, c)) for a in range(n)]

        self.plans = {
            "own": (lambda x, y, c: [(a, None, n + a, blk(4 * x + 2 * y + c), (x, y, 1 - c)) for a in range(n)], n),
            "nbr": (lambda x, y, c: [(a, None, n + a, blk(4 * x + 2 * y + c), dev) for a in range(n)
                                     for dev in ((1 - x, y, c), (x, 1 - y, c))], 2 * n),
            "diag": (diag, n),
            "nbr_d2d": (to_sibling(nbr_blocks), 2 * n),
            "own_nbr_d2d": (to_sibling(lambda x, y, c: [4 * x + 2 * y + c] + nbr_blocks(x, y, c)), 3 * n),
            "diag_d2d": (to_sibling(lambda x, y, c: [4 * (1 - x) + 2 * (1 - y) + c]), n),
        }

    def start(self, stage, deps=()):
        plan, ncopy = self.plans[stage]
        self.pending[stage], token = self.chain.start(plan, ncopy, self.name + "_" + stage, deps)
        return token

    def wait(self, stage, after):
        self.chain.wait(self.pending.pop(stage), after)

    def lands(self):
        return self.chain.bufs[self.n:]


def _gather_ici(shards, me, name, deps=()):
    n = len(shards)
    lands = [lax.dynamic_update_slice(lax.empty((N_DEV,) + s.shape, s.dtype), s[None], (me, 0, 0)) for s in shards]

    def plan(x, y, c):
        return [(a, None, n + a, 4 * x + 2 * y + c, (px, py, c)) for a in range(n) for px, py in _other_chips(x, y)]

    return _Exchange(list(shards) + lands, plan, 3 * n, name, deps)


def _gather_d2d(lands, name, deps=()):
    n = len(lands)

    def plan(x, y, c):
        blocks = [4 * x + 2 * y + c] + [4 * px + 2 * py + c for px, py in _other_chips(x, y)]
        return [(a, b, a, b, (x, y, 1 - c)) for a in range(n) for b in blocks]

    return _Exchange(list(lands), plan, 4 * n, name, deps)


def _reduce_d2d(grads, name, deps=(), rows=None):
    n = len(grads)
    lands = [lax.empty((4,) + g.shape[1:], g.dtype) for g in grads]

    def blk(b):
        return b if rows is None else (b, rows)

    def plan(x, y, c):
        return [(a, blk(2 * s + (1 - c)), n + a, blk(s), (x, y, 1 - c)) for a in range(n) for s in range(4)]

    return _Exchange(list(grads) + lands, plan, 4 * n, name, deps)


def _reduce_ici(chip, name, deps=(), rows=None):
    n = len(chip)
    lands = [lax.empty((3,) + g.shape[1:], g.dtype) for g in chip]

    def blk(b):
        return b if rows is None else (b, rows)

    def plan(x, y, c):
        return [(a, blk(2 * px + py), n + a, blk(k), (px, py, c))
                for a in range(n) for k, (px, py) in enumerate(_other_chips(x, y))]

    return _Exchange(list(chip) + lands, plan, 3 * n, name, deps)


def _pair_add(full, got, core, name, rows=None):
    _, r, cdim = full.shape
    br = _row_block(rows or r, 512)

    def body(c_ref, f_ref, g_ref, o_ref):
        o_ref[...] = (f_ref[...].astype(F32) + g_ref[...].astype(F32)).astype(o_ref.dtype)

    return pl.pallas_call(
        body, name=name,
        grid_spec=pltpu.PrefetchScalarGridSpec(
            num_scalar_prefetch=1, grid=(4, (rows or r) // br),
            in_specs=[pl.BlockSpec((1, br, cdim), lambda s, i, c_ref: (2 * s + c_ref[0], i, 0)),
                      pl.BlockSpec((1, br, cdim), lambda s, i, c_ref: (s, i, 0))],
            out_specs=pl.BlockSpec((1, br, cdim), lambda s, i, c_ref: (s, i, 0))),
        out_shape=jax.ShapeDtypeStruct((4, r, cdim), full.dtype),
        compiler_params=_params(("arbitrary", "arbitrary")),
    )(core, full, got)


def _sum_devices(stack):
    _, r, cdim = stack.shape

    def body(s_ref, o_ref):
        acc = s_ref[0]
        for k in range(1, N_DEV):
            acc = acc + s_ref[k]
        o_ref[...] = acc

    return pl.pallas_call(
        body, name="sum_devices", out_shape=jax.ShapeDtypeStruct((r, cdim), F32),
        compiler_params=_params(),
    )(stack)


def _adamw(w, g, m, v, name):
    r, cdim = w.shape
    br = _row_block(r, 256)

    def body(w_ref, g_ref, m_ref, v_ref, d_ref, nm_ref, nv_ref):
        g = g_ref[...]
        nm = ADAM_B1 * m_ref[...] + (1.0 - ADAM_B1) * g
        nv = ADAM_B2 * v_ref[...] + (1.0 - ADAM_B2) * (g * g)
        m_hat = nm / (1.0 - ADAM_B1 ** ADAM_STEP)
        v_hat = nv / (1.0 - ADAM_B2 ** ADAM_STEP)
        d_ref[...] = -ADAM_LR * (m_hat / (jnp.sqrt(v_hat) + ADAM_EPS) + ADAM_WD * w_ref[...])
        nm_ref[...] = nm
        nv_ref[...] = nv

    spec = pl.BlockSpec((br, cdim), lambda i: (i, 0))
    shape = jax.ShapeDtypeStruct((r, cdim), F32)
    return pl.pallas_call(
        body, name=name, grid=(r // br,), in_specs=[spec] * 4, out_specs=[spec] * 3,
        out_shape=[shape] * 3, compiler_params=_params(("arbitrary",)),
    )(w, g, m, v)


def _sum_adamw(chip, got, slot, w, m, v, name, after):
    rows, cdim = w.shape
    bc = 2 * LANE if cdim % (2 * LANE) == 0 else cdim

    def body(s_ref, c_ref, g_ref, w_ref, m_ref, v_ref, _, go_ref, d_ref, nm_ref, nv_ref):
        g = c_ref[0].astype(F32)
        for k in range(3):
            g = g + g_ref[k].astype(F32)
        nm = ADAM_B1 * m_ref[...] + (1.0 - ADAM_B1) * g
        nv = ADAM_B2 * v_ref[...] + (1.0 - ADAM_B2) * (g * g)
        m_hat = nm / (1.0 - ADAM_B1 ** ADAM_STEP)
        v_hat = nv / (1.0 - ADAM_B2 ** ADAM_STEP)
        go_ref[...] = g
        d_ref[...] = -ADAM_LR * (m_hat / (jnp.sqrt(v_hat) + ADAM_EPS) + ADAM_WD * w_ref[...])
        nm_ref[...] = nm
        nv_ref[...] = nv

    spec = pl.BlockSpec((rows, bc), lambda j, s_ref: (0, j))
    shape = jax.ShapeDtypeStruct((rows, cdim), F32)
    return pl.pallas_call(
        body, name=name,
        grid_spec=pltpu.PrefetchScalarGridSpec(
            num_scalar_prefetch=1, grid=(cdim // bc,),
            in_specs=[pl.BlockSpec((1, rows, bc), lambda j, s_ref: (s_ref[0], 0, j)),
                      pl.BlockSpec((3, rows, bc), lambda j, s_ref: (0, 0, j)), spec, spec, spec, ANY],
            out_specs=[spec] * 4),
        out_shape=[shape] * 4,
        compiler_params=_params(("arbitrary",)),
    )(slot, chip, got, w, m, v, after)


WEIGHTS = ("ffn1_norm_g", "ffn1_w_gate", "ffn1_w_up", "ffn1_w_down", "mix_norm_g", "w_in", "gmlp_v_norm_g",
           "gmlp_w_s", "gmlp_b_s", "mla_q_norm_g", "mla_w_q_up", "mla_kv_norm_g", "mla_w_kv_up", "mla_q_head_g",
           "mla_k_head_g", "gmlp_out_g", "mla_out_g", "w_out", "ffn2_norm_g", "ffn2_w_gate", "ffn2_w_up",
           "ffn2_w_down")
SHARDED = {"ffn1_w_gate": True, "ffn1_w_up": True, "ffn1_w_down": False, "w_in": True, "mla_w_q_up": True,
           "mla_w_kv_up": True, "w_out": False, "ffn2_w_gate": True, "ffn2_w_up": True, "ffn2_w_down": False}


def _col_block(m, target):
    best = LANE
    for cand in range(LANE, min(m, target) + 1, LANE):
        if m % cand == 0:
            best = cand
    return best


def _shard_rows(w, transposed, pad_to=None):
    rows = (w[0].T if transposed else w[0]).astype(BF16)
    if pad_to is not None and pad_to != rows.shape[0]:
        rows = jnp.pad(rows, ((0, pad_to - rows.shape[0]), (0, 0)))
    return rows


def _pack(parts):
    flat = []
    for p in parts:
        f = p.reshape(-1).astype(F32)
        flat.append(jnp.pad(f, (0, _round_up(f.size, LANE) - f.size)))
    flat = jnp.concatenate(flat)
    rows = _round_up(flat.size // LANE, SUBLANE)
    return jnp.pad(flat, (0, rows * LANE - flat.size)).reshape(rows, LANE)


def _unpack(packed, shapes):
    out, row = [], 0
    for shp in shapes:
        size = 1
        for s in shp:
            size *= s
        nrows = _round_up(size, LANE) // LANE
        out.append(packed[row:row + nrows].reshape(-1)[:size].reshape(shp))
        row += nrows
    return out


def kernel(x, positions, ffn1_norm_g, ffn1_w_gate, ffn1_w_up, ffn1_w_down, mix_norm_g, w_in, gmlp_v_norm_g, gmlp_w_s, gmlp_b_s, mla_q_norm_g, mla_w_q_up, mla_kv_norm_g, mla_w_kv_up, mla_q_head_g, mla_k_head_g, gmlp_out_g, mla_out_g, w_out, ffn2_norm_g, ffn2_w_gate, ffn2_w_up, ffn2_w_down, loss_target, m_ffn1_norm_g, m_ffn1_w_gate, m_ffn1_w_up, m_ffn1_w_down, m_mix_norm_g, m_w_in, m_gmlp_v_norm_g, m_gmlp_w_s, m_gmlp_b_s, m_mla_q_norm_g, m_mla_w_q_up, m_mla_kv_norm_g, m_mla_w_kv_up, m_mla_q_head_g, m_mla_k_head_g, m_gmlp_out_g, m_mla_out_g, m_w_out, m_ffn2_norm_g, m_ffn2_w_gate, m_ffn2_w_up, m_ffn2_w_down, v_ffn1_norm_g, v_ffn1_w_gate, v_ffn1_w_up, v_ffn1_w_down, v_mix_norm_g, v_w_in, v_gmlp_v_norm_g, v_gmlp_w_s, v_gmlp_b_s, v_mla_q_norm_g, v_mla_w_q_up, v_mla_kv_norm_g, v_mla_w_kv_up, v_mla_q_head_g, v_mla_k_head_g, v_gmlp_out_g, v_mla_out_g, v_w_out, v_ffn2_norm_g, v_ffn2_w_gate, v_ffn2_w_up, v_ffn2_w_down):
    wts = dict(zip(WEIGHTS, (ffn1_norm_g, ffn1_w_gate, ffn1_w_up, ffn1_w_down, mix_norm_g, w_in, gmlp_v_norm_g, gmlp_w_s, gmlp_b_s, mla_q_norm_g, mla_w_q_up, mla_kv_norm_g, mla_w_kv_up, mla_q_head_g, mla_k_head_g, gmlp_out_g, mla_out_g, w_out, ffn2_norm_g, ffn2_w_gate, ffn2_w_up, ffn2_w_down)))
    mom1 = dict(zip(WEIGHTS, (m_ffn1_norm_g, m_ffn1_w_gate, m_ffn1_w_up, m_ffn1_w_down, m_mix_norm_g, m_w_in, m_gmlp_v_norm_g, m_gmlp_w_s, m_gmlp_b_s, m_mla_q_norm_g, m_mla_w_q_up, m_mla_kv_norm_g, m_mla_w_kv_up, m_mla_q_head_g, m_mla_k_head_g, m_gmlp_out_g, m_mla_out_g, m_w_out, m_ffn2_norm_g, m_ffn2_w_gate, m_ffn2_w_up, m_ffn2_w_down)))
    mom2 = dict(zip(WEIGHTS, (v_ffn1_norm_g, v_ffn1_w_gate, v_ffn1_w_up, v_ffn1_w_down, v_mix_norm_g, v_w_in, v_gmlp_v_norm_g, v_gmlp_w_s, v_gmlp_b_s, v_mla_q_norm_g, v_mla_w_q_up, v_mla_kv_norm_g, v_mla_w_kv_up, v_mla_q_head_g, v_mla_k_head_g, v_gmlp_out_g, v_mla_out_g, v_w_out, v_ffn2_norm_g, v_ffn2_w_gate, v_ffn2_w_up, v_ffn2_w_down)))

    b_loc, seq, d = x.shape
    t = b_loc * seq
    ffs = ffn1_w_gate.shape[2]
    fp = _round_up(ffs, LANE)
    wg = gmlp_v_norm_g.shape[1]
    groups = gmlp_w_s.shape[1]
    rq, rkv = mla_q_norm_g.shape[1], mla_kv_norm_g.shape[1]
    heads = mla_out_g.shape[1]
    assert w_in.shape[2] * N_DEV == 2 * wg + rq + rkv + ROPE and mla_w_kv_up.shape[2] * N_DEV == heads * HEADW
    tm = min(512, t)
    tm_mix = min(256, t)
    blk = min(256, seq)

    xf = x.reshape(t, d)
    target = loss_target.reshape(t, d)
    pos = positions.reshape(t, 1).astype(F32)
    half = ROPE // 2
    inv_freq = 1.0 / (ROPE_THETA ** (jnp.arange(half, dtype=F32) / half))
    freq = jnp.concatenate([inv_freq, inv_freq, jnp.zeros((LANE - ROPE,), F32)])[None, :]
    lane = jnp.arange(LANE)
    masks = jnp.stack([jnp.where(lane < half, -1.0, 0.0),
                       jnp.where((lane >= half) & (lane < ROPE), 1.0, 0.0)]).astype(F32)
    gqh = jnp.pad(mla_q_head_g, ((0, 0), (0, HEADW - QK)))
    gkh = jnp.pad(mla_k_head_g, ((0, 0), (0, HEADW - QK)))
    bias = jnp.repeat(gmlp_b_s[0].T, CHUNK, axis=1)
    gouta = gmlp_out_g.reshape(1, wg)
    goutb = mla_out_g.reshape(1, heads * VHEAD)
    ws = gmlp_w_s[0]

    px, py, pc = _place()
    me = 4 * px + 2 * py + pc
    core = pc.astype(jnp.int32).reshape(1)
    slot = (2 * px + py).astype(jnp.int32).reshape(1)
    order = [n for n in WEIGHTS if n in SHARDED]
    group = {"ffn1": [n for n in order if n.startswith("ffn1")], "ffn2": [n for n in order if n.startswith("ffn2")],
             "mix": [n for n in order if not n.startswith("ffn")]}
    shard = {n: _shard_rows(wts[n], SHARDED[n]) for n in group["ffn1"]}
    frows = ffs if ffs != fp else None

    def tied(arr, token):
        return arr + token[0, 0].astype(arr.dtype)

    xnb, ynb, dgn = 4 * (1 - px) + 2 * py, 4 * px + 2 * (1 - py), 4 * (1 - px) + 2 * (1 - py)
    ids_a = jnp.stack([me, 4 * px + 2 * py + (1 - pc)]).astype(jnp.int32)
    ids_b = jnp.stack([xnb, xnb + 1, ynb, ynb + 1]).astype(jnp.int32)
    ids_c = jnp.stack([dgn, dgn + 1]).astype(jnp.int32)
    g1 = _StagedGather([shard[n] for n in group["ffn1"]], me, "gather_ffn1", pad_to=fp)
    token = g1.start("own")
    token = g1.start("nbr", deps=(token,))
    for n in group["mix"] + group["ffn2"]:
        shard[n] = _shard_rows(tied(wts[n], token), SHARDED[n])
    g3 = _StagedGather([shard[n] for n in group["ffn2"]], me, "gather_ffn2", pad_to=fp)
    g1.wait("own", token)
    x1, xn1, kept1 = _ffn_fwd(xf, None, ffn1_norm_g, ids_a, *g1.lands(), None, tm, "ffn1_fwd_a")
    g1.wait("nbr", x1)
    token = g1.start("diag")
    ici2 = _gather_ici([shard[n] for n in group["mix"]], me, "gather_mix_ici", deps=(token,))
    token = g3.start("nbr", deps=(ici2.token,))
    token = g1.start("nbr_d2d", deps=(token,))
    g1.wait("nbr_d2d", token)
    x1, xn1, kept1 = _ffn_fwd(x1, xn1, None, ids_b, *g1.lands(), kept1, tm, "ffn1_fwd_b")
    g1.wait("diag", x1)
    token = g1.start("diag_d2d")
    g1.wait("diag_d2d", token)
    full = dict(zip(group["ffn1"], g1.lands()))
    x1, xn1, (gd1, sl1, h1) = _ffn_fwd(x1, xn1, None, ids_c, full["ffn1_w_gate"], full["ffn1_w_up"],
                                       full["ffn1_w_down"], kept1, tm, "ffn1_fwd_c")
    d2d2 = _gather_d2d(ici2.wait(x1)[len(group["mix"]):], "gather_mix_d2d")
    full.update(zip(group["mix"], d2d2.wait(d2d2.token)))
    win_t = full["w_in"].reshape(-1, d)
    splits = (2 * wg, rq, rkv, LANE)
    wq_t = jnp.pad(full["mla_w_q_up"].reshape(heads, QK, rq), ((0, 0), (0, HEADW - QK), (0, 0)))
    wkv_t = full["mla_w_kv_up"].reshape(heads, HEADW, rkv)
    wout = full["w_out"].reshape(-1, d)
    hn, zuv, cq, ckv, krw = _inproj_fwd(x1, mix_norm_g, win_t, splits, tm)
    ya = _gmlp_fwd(zuv, gmlp_v_norm_g, ws, bias, gouta, tm_mix)
    g3.wait("nbr", ya)
    token = g3.start("diag")
    token = g3.start("own_nbr_d2d", deps=(token,))
    q, k, vv = _mla_proj_fwd(cq, ckv, krw, pos, freq, masks, mla_q_norm_g, mla_kv_norm_g, wq_t, wkv_t,
                             tied(gqh, token), gkh, tm_mix)
    o, lse = _attn_fwd(q, k, vv, seq, blk)
    g3.wait("diag", o)
    token = g3.start("diag_d2d")
    x2, ycat = _out_fwd(ya, o, tied(goutb, token), wout, x1, tm)
    g3.wait("own_nbr_d2d", x2)
    g3.wait("diag_d2d", x2)
    full.update(zip(group["ffn2"], g3.lands()))
    x3, xn2, (gd2, sl2, h2) = _ffn_fwd(x2, None, ffn2_norm_g, jnp.arange(N_DEV, dtype=jnp.int32),
                                       full["ffn2_w_gate"], full["ffn2_w_up"], full["ffn2_w_down"], None, tm,
                                       "ffn2_fwd")
    dx3, loss_part = _loss_head(x3, target, tm)

    outs_g, outs_d, outs_m, outs_v = {}, {}, {}, {}

    def finish(names, chip, got, after):
        for n, cp, gt in zip(names, chip, got):
            rows_of = (lambda a: a[0].T) if SHARDED[n] else (lambda a: a[0])
            res = _sum_adamw(cp, gt, slot, rows_of(wts[n]), rows_of(mom1[n]), rows_of(mom2[n]), "adamw_" + n, after)
            outs_g[n], outs_d[n], outs_m[n], outs_v[n] = [r.T[None] if SHARDED[n] else r[None] for r in res]
            after = res[3]
        return after

    def chip_sums(names, ex, after, rows=None):
        res = ex.wait(after)
        return [_pair_add(f, gt, core, "pair_add_" + n, rows)
                for n, f, gt in zip(names, res[:len(names)], res[len(names):])]

    tk = min(1024, t)
    grads = {}
    small = {}
    dx2, small["ffn2_norm_g"], da2, db2 = _ffn_bwd(
        dx3, x2, ffn2_norm_g, gd2, sl2, full["ffn2_w_gate"], full["ffn2_w_up"], full["ffn2_w_down"], tm, "ffn2_bwd")
    grads["ffn2_w_gate"] = _matmul_tn(da2, xn2, fp, d, tk, BF16, "dw_ffn2_gate").reshape(N_DEV, fp, d)
    grads["ffn2_w_up"] = _matmul_tn(db2, xn2, fp, d, tk, BF16, "dw_ffn2_up").reshape(N_DEV, fp, d)
    grads["ffn2_w_down"] = _matmul_tn(h2, dx3, fp, d, tk, BF16, "dw_ffn2_down", rhs_scale=0.5).reshape(
        N_DEV, fp, d)
    red_a2 = _reduce_d2d([grads[n] for n in group["ffn2"]], "reduce_ffn2_d2d", rows=frows)
    dya, do, delta, small["mla_out_g"] = _out_bwd(dx2, o, tied(goutb, red_a2.token), wout, wg, tm)
    grads["w_out"] = _matmul_tn(ycat, dx2, _col_block(ycat.shape[1], 768), d, tk, BF16, "dw_out").reshape(
        N_DEV, -1, d)
    chip2 = chip_sums(group["ffn2"], red_a2, dya, frows)
    red_b2 = _reduce_ici(chip2, "reduce_ffn2_ici", rows=frows)
    dq, dk, dv = _attn_bwd(q, k, vv, do, lse, delta, seq, blk, red_b2.token)
    (dcq, dckv, dkrw, dwq, dwkv, small["mla_q_norm_g"], small["mla_kv_norm_g"], dgqh, dgkh) = _mla_proj_bwd(
        dq, dk, dv, cq, ckv, krw, pos, freq, masks, mla_q_norm_g, mla_kv_norm_g, wq_t, wkv_t, gqh, gkh, tm_mix)
    small["mla_q_head_g"], small["mla_k_head_g"] = dgqh[:, :QK], dgkh[:, :QK]
    grads["mla_w_q_up"] = dwq[:, :QK].astype(BF16).reshape(N_DEV, -1, rq)
    grads["mla_w_kv_up"] = dwkv.astype(BF16).reshape(N_DEV, -1, rkv)
    dzuv, small["gmlp_w_s"], dbs, small["gmlp_v_norm_g"], small["gmlp_out_g"] = _gmlp_bwd(
        dya, zuv, gmlp_v_norm_g, ws, bias, gouta, tm_mix)
    small["gmlp_b_s"] = dbs[:, :, 0]
    dx1, small["mix_norm_g"], dzc = _inproj_bwd([dzuv, dcq, dckv, dkrw], x1, mix_norm_g, win_t, dx2, splits,
                                                tm_mix)
    grads["w_in"] = _matmul_tn(dzc, hn, _col_block(dzc.shape[1], 768), d, tk, BF16, "dw_in",
                               out_rows=win_t.shape[0]).reshape(N_DEV, -1, d)
    res_b2 = red_b2.wait(grads["w_in"])
    red_am = _reduce_d2d([grads[n] for n in group["mix"]], "reduce_mix_d2d")

    def ffn1_dw(n, lhs, rhs, scale, token):
        return _matmul_tn(lhs, rhs, fp, d, tk, BF16, "dw_" + n, rhs_scale=scale, deps=(token,)).reshape(N_DEV, fp, d)

    gr = ffn1_dw("ffn1_w_down", h1, dx1, 0.5, red_am.token)
    red_ad = _reduce_d2d([gr], "reduce_ffn1_w_down_d2d", deps=(red_am.token,), rows=frows)
    dx0, small["ffn1_norm_g"], da1, db1 = _ffn_bwd(
        dx1, xf, tied(ffn1_norm_g, red_ad.token), gd1, sl1, full["ffn1_w_gate"], full["ffn1_w_up"],
        full["ffn1_w_down"], tm, "ffn1_bwd")
    chipm = chip_sums(group["mix"], red_am, dx0)
    red_bm = _reduce_ici(chipm, "reduce_mix_ici")
    red_b = [("ffn1_w_down", _reduce_ici(chip_sums(["ffn1_w_down"], red_ad, dx0, frows), "reduce_ffn1_w_down_ici",
                                         deps=(red_bm.token,), rows=frows))]
    rep = [n for n in WEIGHTS if n not in SHARDED]
    small_ici = _gather_ici([_pack([small[n] for n in rep] + [loss_part])], me, "gather_small_ici",
                            deps=(red_b[-1][1].token,))
    gr = ffn1_dw("ffn1_w_gate", da1, xn1, None, small_ici.token)
    red_ag = _reduce_d2d([gr], "reduce_ffn1_w_gate_d2d", rows=frows)
    red_b.append(("ffn1_w_gate", _reduce_ici(chip_sums(["ffn1_w_gate"], red_ag, red_ag.token, frows),
                                             "reduce_ffn1_w_gate_ici", rows=frows)))
    gr = ffn1_dw("ffn1_w_up", db1, xn1, None, red_b[-1][1].token)
    small_d2d = _gather_d2d(small_ici.wait(gr)[1:], "gather_small_d2d")
    red_au = _reduce_d2d([gr], "reduce_ffn1_w_up_d2d", deps=(small_d2d.token,), rows=frows)
    red_b.append(("ffn1_w_up", _reduce_ici(chip_sums(["ffn1_w_up"], red_au, red_au.token, frows),
                                           "reduce_ffn1_w_up_ici", rows=frows)))
    after = finish(group["ffn2"], res_b2[:3], res_b2[3:], red_b[-1][1].token)
    res = red_bm.wait(after)
    nm_ = len(group["mix"])
    after = finish(group["mix"], res[:nm_], res[nm_:], after)
    total = _sum_devices(small_d2d.wait(after)[0])
    zero = jnp.zeros((1,), F32)
    dlt, nm, nv = _adamw(_pack([wts[n] for n in rep] + [zero]), total, _pack([mom1[n] for n in rep] + [zero]),
                         _pack([mom2[n] for n in rep] + [zero]), "adamw_small")
    shapes = [wts[n].shape for n in rep] + [(1,)]
    for n, g, dl, m1, m2 in zip(rep, _unpack(total, shapes), _unpack(dlt, shapes), _unpack(nm, shapes),
                                _unpack(nv, shapes)):
        outs_g[n], outs_d[n], outs_m[n], outs_v[n] = g, dl, m1, m2
    loss = _unpack(total, shapes)[-1].reshape(())
    after = dlt
    for n, ex in red_b:
        res = ex.wait(after)
        after = finish([n], res[:1], res[1:], after)

    return (loss, dx0.reshape(b_loc, seq, d), *[outs_g[n] for n in WEIGHTS], *[outs_d[n] for n in WEIGHTS],
            *[outs_m[n] for n in WEIGHTS], *[outs_v[n] for n in WEIGHTS])
```

```python
import functools

import jax
import jax.numpy as jnp
from jax import lax
from jax.experimental import pallas as pl
from jax.experimental.pallas import tpu as pltpu

F32 = jnp.float32
BF16 = jnp.bfloat16
EPS = 1e-6
LANE = 128
SUBLANE = 8
N_DEV = 8
VMEM_LIMIT = 60 * 1024 * 1024
NOPE = 128
ROPE = 64
VHEAD = 128
QK = NOPE + ROPE
HEADW = 2 * LANE
CHUNK = 128
ROPE_THETA = 10000.0
ADAM_LR, ADAM_B1, ADAM_B2, ADAM_EPS, ADAM_WD, ADAM_STEP = 0.001, 0.9, 0.999, 1e-08, 0.01, 10
MESH = pl.DeviceIdType.MESH
ANY = pl.BlockSpec(memory_space=pl.ANY)
WHOLE_VMEM = pl.BlockSpec(memory_space=pltpu.VMEM)


def _params(sem=None):
    return pltpu.CompilerParams(dimension_semantics=sem, vmem_limit_bytes=VMEM_LIMIT)


def _round_up(n, m):
    return -(-n // m) * m


def _row_block(rows, target):
    best = rows
    for cand in range(SUBLANE, min(rows, target) + 1, SUBLANE):
        if rows % cand == 0:
            best = cand
    return best if best <= target else rows


def _nn(a, b):
    return jnp.dot(a, b, preferred_element_type=F32)


def _nt(a, b):
    return lax.dot_general(a, b, (((1,), (1,)), ((), ())), preferred_element_type=F32)


def _tn(a, b):
    return lax.dot_general(a, b, (((0,), (0,)), ((), ())), preferred_element_type=F32)


def _rstd(x, n):
    return lax.rsqrt(jnp.sum(x * x, axis=-1, keepdims=True) * (1.0 / n) + EPS)


def _rms_fwd(x, g, n):
    return x * _rstd(x, n) * g


def _rms_bwd(x, g, dy, n):
    r = _rstd(x, n)
    xh = x * r
    dyg = dy * g
    dx = r * (dyg - xh * (jnp.sum(dyg * xh, axis=-1, keepdims=True) * (1.0 / n)))
    return dx, jnp.sum(dy * xh, axis=0, keepdims=True)


def _gelu(x):
    return 0.5 * x * (1.0 + lax.erf(x * 0.7071067811865476))


def _gelu_grad(x):
    return 0.5 * (1.0 + lax.erf(x * 0.7071067811865476)) + x * jnp.exp(-0.5 * x * x) * 0.3989422804014327


def _ffn_fwd(base, xn, g, ids, wg_t, wu_t, wd, saved, tm, name):
    t, d = base.shape
    nb, fp, _ = wg_t.shape
    n = ids.shape[0]
    first = xn is None
    if saved is None:
        saved = [lax.empty((t, nb * fp), BF16) for _ in range(3)]

    def body(ids_ref, *refs):
        if first:
            base_ref, g_ref, wg_ref, wu_ref, wd_ref, _, _, _, out_ref, xn_ref, gd_ref, sl_ref, h_ref, acc = refs
        else:
            base_ref, xn_ref, wg_ref, wu_ref, wd_ref, _, _, _, out_ref, gd_ref, sl_ref, h_ref, acc = refs
        j = pl.program_id(1)

        @pl.when(j == 0)
        def _():
            if first:
                xn_ref[...] = _rms_fwd(base_ref[...], g_ref[...], d).astype(BF16)
            acc[...] = jnp.zeros_like(acc)

        xnb = xn_ref[...]
        a = _nt(xnb, wg_ref[0])
        b = _nt(xnb, wu_ref[0])
        s = jax.nn.sigmoid(a)
        sl = a * s
        h = (sl * b).astype(BF16)
        gd_ref[...] = (b * (s * (1.0 + a * (1.0 - s)))).astype(BF16)
        sl_ref[...] = sl.astype(BF16)
        h_ref[...] = h
        acc[...] += _nn(h, wd_ref[0])

        @pl.when(j == n - 1)
        def _():
            out_ref[...] = base_ref[...] + 0.5 * acc[...]

    wspec = pl.BlockSpec((1, fp, d), lambda i, j, ids_ref: (ids_ref[j], 0, 0))
    row = pl.BlockSpec((tm, d), lambda i, j, ids_ref: (i, 0))
    ff = pl.BlockSpec((tm, fp), lambda i, j, ids_ref: (i, ids_ref[j]))
    ffs = jax.ShapeDtypeStruct((t, nb * fp), BF16)
    second = pl.BlockSpec((1, d), lambda i, j, ids_ref: (0, 0)) if first else row
    n_row_outs = 2 if first else 1
    res = pl.pallas_call(
        body, name=name,
        grid_spec=pltpu.PrefetchScalarGridSpec(
            num_scalar_prefetch=1, grid=(t // tm, n),
            in_specs=[row, second, wspec, wspec, wspec, ANY, ANY, ANY],
            out_specs=[row] * n_row_outs + [ff, ff, ff],
            scratch_shapes=[pltpu.VMEM((tm, d), F32)]),
        out_shape=[jax.ShapeDtypeStruct((t, d), F32)] + ([jax.ShapeDtypeStruct((t, d), BF16)] if first else [])
        + [ffs, ffs, ffs],
        input_output_aliases={6 + k: n_row_outs + k for k in range(3)},
        compiler_params=_params(("arbitrary", "arbitrary")),
    )(ids, base, g if first else xn, wg_t, wu_t, wd, *saved)
    return (res[0], res[1] if first else xn, list(res[n_row_outs:]))


def _ffn_bwd(dout, x, g, gd, sl, wg_t, wu_t, wd, tm, name):
    t, d = x.shape
    nb, fp, _ = wg_t.shape

    def body(do_hbm, x_hbm, g_ref, gd_ref, sl_ref, wg_ref, wu_ref, wd_ref, wd_next_ref,
             dx_hbm, dg_ref, da_ref, db_ref, acc, rowbuf, dy_scr, dh_scr, sem):
        i, j = pl.program_id(0), pl.program_id(1)
        rows = pl.ds(pl.multiple_of(i * tm, tm), tm)

        def fetch(src):
            cp = pltpu.make_async_copy(src.at[rows, :], rowbuf, sem)
            cp.start()
            cp.wait()

        @pl.when(j == 0)
        def _():
            fetch(do_hbm)
            dy_scr[...] = (0.5 * rowbuf[...]).astype(BF16)
            acc[...] = jnp.zeros_like(acc)
            dh_scr[0] = _nt(dy_scr[...], wd_ref[0])

        @pl.when((i == 0) & (j == 0))
        def _():
            dg_ref[...] = jnp.zeros_like(dg_ref)

        dh = dh_scr[j % 2]
        dh_scr[(j + 1) % 2] = _nt(dy_scr[...], wd_next_ref[0])
        da = (dh * gd_ref[...].astype(F32)).astype(BF16)
        db = (dh * sl_ref[...].astype(F32)).astype(BF16)
        da_ref[...] = da
        db_ref[...] = db
        acc[...] += _nn(da, wg_ref[0]) + _nn(db, wu_ref[0])

        @pl.when(j == nb - 1)
        def _():
            fetch(x_hbm)
            dxn, dg = _rms_bwd(rowbuf[...], g_ref[...], acc[...], d)
            dg_ref[...] += dg
            acc[...] = dxn
            fetch(do_hbm)
            acc[...] += rowbuf[...]
            out = pltpu.make_async_copy(acc, dx_hbm.at[rows, :], sem)
            out.start()
            out.wait()

    wspec = pl.BlockSpec((1, fp, d), lambda i, j: (j, 0, 0))
    wnext = pl.BlockSpec((1, fp, d), lambda i, j: (jnp.minimum(j + 1, nb - 1), 0, 0))
    vec = pl.BlockSpec((1, d), lambda i, j: (0, 0))
    ff = pl.BlockSpec((tm, fp), lambda i, j: (i, j))
    ffs = jax.ShapeDtypeStruct((t, nb * fp), BF16)
    return pl.pallas_call(
        body, name=name, grid=(t // tm, nb),
        in_specs=[ANY, ANY, vec, ff, ff, wspec, wspec, pl.BlockSpec((1, fp, d), lambda i, j: (0, 0, 0)), wnext],
        out_specs=[ANY, vec, ff, ff],
        out_shape=[jax.ShapeDtypeStruct((t, d), F32), jax.ShapeDtypeStruct((1, d), F32), ffs, ffs],
        scratch_shapes=[pltpu.VMEM((tm, d), F32), pltpu.VMEM((tm, d), F32), pltpu.VMEM((tm, d), BF16),
                        pltpu.VMEM((2, tm, fp), F32), pltpu.SemaphoreType.DMA],
        compiler_params=_params(("arbitrary", "arbitrary")),
    )(dout, x, g, gd, sl, wg_t, wu_t, wd, wd)


def _matmul_tn(lhs, rhs, bm, bn, tk, out_dtype, name, rhs_scale=None, deps=(), out_rows=None):
    t, m = lhs.shape
    n = rhs.shape[1]
    nk = t // tk
    out_rows = m if out_rows is None else out_rows

    def body(l_ref, r_ref, *refs):
        o_ref, acc = refs[len(deps):]
        k = pl.program_id(2)

        @pl.when(k == 0)
        def _():
            acc[...] = jnp.zeros_like(acc)

        r = r_ref[...] if rhs_scale is None else rhs_scale * r_ref[...]
        acc[...] += _tn(l_ref[...].astype(BF16), r.astype(BF16))

        @pl.when(k == nk - 1)
        def _():
            o_ref[...] = acc[...].astype(out_dtype)

    return pl.pallas_call(
        body, name=name, grid=(m // bm, n // bn, nk),
        in_specs=[pl.BlockSpec((tk, bm), lambda i, j, k: (k, i)), pl.BlockSpec((tk, bn), lambda i, j, k: (k, j))]
        + [ANY] * len(deps),
        out_specs=pl.BlockSpec((bm, bn), lambda i, j, k: (i, j)),
        out_shape=jax.ShapeDtypeStruct((out_rows, n), out_dtype),
        scratch_shapes=[pltpu.VMEM((bm, bn), F32)],
        compiler_params=_params(("arbitrary", "arbitrary", "arbitrary")),
    )(lhs, rhs, *deps)


def _last_rows_padded(w_ref, tail_ref, off, real):
    @pl.when(pl.program_id(0) == 0)
    def _():
        tail_ref[...] = jnp.zeros_like(tail_ref)
        tail_ref[0:real, :] = w_ref[off:off + real, :]


def _inproj_fwd(x, g, w_t, splits, tm):
    t, d = x.shape
    offs = [sum(splits[:k]) for k in range(len(splits))]
    real_last = w_t.shape[0] - offs[-1]

    def body(x_ref, g_ref, w_ref, hn_ref, *refs):
        z_refs, tail_ref = refs[:-1], refs[-1]
        _last_rows_padded(w_ref, tail_ref, offs[-1], real_last)
        hn = _rms_fwd(x_ref[...], g_ref[...], d).astype(BF16)
        hn_ref[...] = hn
        for z_ref, o, n in zip(z_refs[:-1], offs, splits):
            z_ref[...] = _nt(hn, w_ref[o:o + n, :])
        z_refs[-1][...] = _nt(hn, tail_ref[...])

    row = pl.BlockSpec((tm, d), lambda i: (i, 0))
    return pl.pallas_call(
        body, name="inproj_fwd", grid=(t // tm,),
        in_specs=[row, pl.BlockSpec((1, d), lambda i: (0, 0)), WHOLE_VMEM],
        out_specs=[row] + [pl.BlockSpec((tm, n), lambda i: (i, 0)) for n in splits],
        out_shape=[jax.ShapeDtypeStruct((t, d), BF16)] + [jax.ShapeDtypeStruct((t, n), F32) for n in splits],
        scratch_shapes=[pltpu.VMEM((splits[-1], d), BF16)],
        compiler_params=_params(("arbitrary",)),
    )(x, g, w_t)


def _inproj_bwd(dzs, x, g, w_t, dres, splits, tm):
    t, d = x.shape
    offs = [sum(splits[:k]) for k in range(len(splits))]
    ni = sum(splits)
    nz = len(splits)
    real_last = w_t.shape[0] - offs[-1]

    def body(*refs):
        dz_refs = refs[:nz]
        x_ref, g_ref, w_ref, dres_ref, dx_ref, dg_ref, dzc_ref, tail_ref = refs[nz:]
        _last_rows_padded(w_ref, tail_ref, offs[-1], real_last)
        dhn = jnp.zeros((tm, d), F32)
        for k, (dz_ref, o, n) in enumerate(zip(dz_refs, offs, splits)):
            dz = dz_ref[...].astype(BF16)
            dzc_ref[:, o:o + n] = dz
            dhn += _nn(dz, tail_ref[...] if k == nz - 1 else w_ref[o:o + n, :])
        dx, dg = _rms_bwd(x_ref[...], g_ref[...], dhn, d)
        dx_ref[...] = dres_ref[...] + dx

        @pl.when(pl.program_id(0) == 0)
        def _():
            dg_ref[...] = jnp.zeros_like(dg_ref)

        dg_ref[...] += dg

    row = pl.BlockSpec((tm, d), lambda i: (i, 0))
    vec = pl.BlockSpec((1, d), lambda i: (0, 0))
    return pl.pallas_call(
        body, name="inproj_bwd", grid=(t // tm,),
        in_specs=[pl.BlockSpec((tm, n), lambda i: (i, 0)) for n in splits] + [row, vec, WHOLE_VMEM, row],
        out_specs=[row, vec, pl.BlockSpec((tm, ni), lambda i: (i, 0))],
        out_shape=[jax.ShapeDtypeStruct((t, d), F32), jax.ShapeDtypeStruct((1, d), F32),
                   jax.ShapeDtypeStruct((t, ni), BF16)],
        scratch_shapes=[pltpu.VMEM((splits[-1], d), BF16)],
        compiler_params=_params(("arbitrary",)),
    )(*dzs, x, g, w_t, dres)


def _tril_bf16(ws_ref, grp):
    rows = lax.broadcasted_iota(jnp.int32, (CHUNK, CHUNK), 0)
    cols = lax.broadcasted_iota(jnp.int32, (CHUNK, CHUNK), 1)
    return jnp.where(rows >= cols, ws_ref[grp], 0.0).astype(BF16)


def _gmlp_mix(zuv_ref, gv_ref, ws_ref, bias_ref, v_scr, mixed_scr, tm, w, groups):
    u = _gelu(zuv_ref[:, 0:w])
    v0 = _gelu(zuv_ref[:, w:2 * w])
    v_scr[...] = _rms_fwd(v0, gv_ref[...], w).astype(BF16)
    for grp in range(groups):
        wsm = _tril_bf16(ws_ref, grp)
        lanes = slice(grp * CHUNK, (grp + 1) * CHUNK)
        for c in range(tm // CHUNK):
            rows = slice(c * CHUNK, (c + 1) * CHUNK)
            mixed_scr[rows, lanes] = _nn(wsm, v_scr[rows, lanes]) + bias_ref[:, lanes]
    return u, v0


def _gmlp_fwd(zuv, gv, ws, bias, gout, tm):
    t, w2 = zuv.shape
    w = w2 // 2
    groups = ws.shape[0]

    def body(zuv_ref, gv_ref, ws_ref, bias_ref, go_ref, y_ref, v_scr, mixed_scr):
        u, _ = _gmlp_mix(zuv_ref, gv_ref, ws_ref, bias_ref, v_scr, mixed_scr, tm, w, groups)
        ya = u * mixed_scr[...]
        for grp in range(groups):
            lanes = slice(grp * CHUNK, (grp + 1) * CHUNK)
            y_ref[:, lanes] = _rms_fwd(ya[:, lanes], go_ref[:, lanes], CHUNK).astype(BF16)

    const2 = lambda i: (0, 0)
    return pl.pallas_call(
        body, name="gmlp_fwd", grid=(t // tm,),
        in_specs=[pl.BlockSpec((tm, w2), lambda i: (i, 0)), pl.BlockSpec((1, w), const2),
                  pl.BlockSpec((groups, CHUNK, CHUNK), lambda i: (0, 0, 0)),
                  pl.BlockSpec((CHUNK, w), const2), pl.BlockSpec((1, w), const2)],
        out_specs=pl.BlockSpec((tm, w), lambda i: (i, 0)),
        out_shape=jax.ShapeDtypeStruct((t, w), BF16),
        scratch_shapes=[pltpu.VMEM((tm, w), BF16), pltpu.VMEM((tm, w), F32)],
        compiler_params=_params(("arbitrary",)),
    )(zuv, gv, ws, bias, gout)


def _gmlp_bwd(dy, zuv, gv, ws, bias, gout, tm):
    t, w2 = zuv.shape
    w = w2 // 2
    groups = ws.shape[0]

    def body(dy_ref, zuv_ref, gv_ref, ws_ref, bias_ref, go_ref,
             dz_ref, dws_ref, dbias_ref, dgv_ref, dgo_ref, v_scr, mixed_scr, dmix_scr, dv_scr):
        @pl.when(pl.program_id(0) == 0)
        def _():
            dws_ref[...] = jnp.zeros_like(dws_ref)
            dbias_ref[...] = jnp.zeros_like(dbias_ref)
            dgv_ref[...] = jnp.zeros_like(dgv_ref)
            dgo_ref[...] = jnp.zeros_like(dgo_ref)

        u, v0 = _gmlp_mix(zuv_ref, gv_ref, ws_ref, bias_ref, v_scr, mixed_scr, tm, w, groups)
        mixed = mixed_scr[...]
        ya = u * mixed
        for grp in range(groups):
            lanes = slice(grp * CHUNK, (grp + 1) * CHUNK)
            dya, dgo = _rms_bwd(ya[:, lanes], go_ref[:, lanes], dy_ref[:, lanes], CHUNK)
            dgo_ref[:, lanes] += dgo
            dz_ref[:, lanes] = dya * mixed[:, lanes] * _gelu_grad(zuv_ref[:, lanes])
            dmix_scr[:, lanes] = dya * u[:, lanes]
        for grp in range(groups):
            wsm = _tril_bf16(ws_ref, grp)
            lanes = slice(grp * CHUNK, (grp + 1) * CHUNK)
            dws = jnp.zeros((CHUNK, CHUNK), F32)
            dbias = jnp.zeros((CHUNK, CHUNK), F32)
            for c in range(tm // CHUNK):
                rows = slice(c * CHUNK, (c + 1) * CHUNK)
                dm = dmix_scr[rows, lanes]
                dmb = dm.astype(BF16)
                dv_scr[rows, lanes] = _tn(wsm, dmb)
                dws += _nt(dmb, v_scr[rows, lanes])
                dbias += dm
            rr = lax.broadcasted_iota(jnp.int32, (CHUNK, CHUNK), 0)
            cc = lax.broadcasted_iota(jnp.int32, (CHUNK, CHUNK), 1)
            dws_ref[grp] += jnp.where(rr >= cc, dws, 0.0)
            dbias_ref[grp] += jnp.sum(dbias, axis=1, keepdims=True)
        dv0, dgv = _rms_bwd(v0, gv_ref[...], dv_scr[...], w)
        dgv_ref[...] += dgv
        dz_ref[:, w:2 * w] = dv0 * _gelu_grad(zuv_ref[:, w:2 * w])

    const2 = lambda i: (0, 0)
    const3 = lambda i: (0, 0, 0)
    return pl.pallas_call(
        body, name="gmlp_bwd", grid=(t // tm,),
        in_specs=[pl.BlockSpec((tm, w), lambda i: (i, 0)), pl.BlockSpec((tm, w2), lambda i: (i, 0)),
                  pl.BlockSpec((1, w), const2), pl.BlockSpec((groups, CHUNK, CHUNK), const3),
                  pl.BlockSpec((CHUNK, w), const2), pl.BlockSpec((1, w), const2)],
        out_specs=[pl.BlockSpec((tm, w2), lambda i: (i, 0)), pl.BlockSpec((groups, CHUNK, CHUNK), const3),
                   pl.BlockSpec((groups, CHUNK, 1), const3), pl.BlockSpec((1, w), const2), pl.BlockSpec((1, w), const2)],
        out_shape=[jax.ShapeDtypeStruct((t, w2), F32), jax.ShapeDtypeStruct((groups, CHUNK, CHUNK), F32),
                   jax.ShapeDtypeStruct((groups, CHUNK, 1), F32), jax.ShapeDtypeStruct((1, w), F32),
                   jax.ShapeDtypeStruct((1, w), F32)],
        scratch_shapes=[pltpu.VMEM((tm, w), BF16), pltpu.VMEM((tm, w), F32),
                        pltpu.VMEM((tm, w), F32), pltpu.VMEM((tm, w), F32)],
        compiler_params=_params(("arbitrary",)),
    )(dy, zuv, gv, ws, bias, gout)


def _rot(x, m_lo, m_hi):
    return pltpu.roll(x, LANE - ROPE // 2, 1) * m_lo + pltpu.roll(x, ROPE // 2, 1) * m_hi


def _rope_tables(pos_ref, freq_ref):
    ang = pos_ref[...] * freq_ref[...]
    return jnp.cos(ang), jnp.sin(ang)


def _mla_proj_fwd(cq, ckv, krw, pos, freq, masks, gq, gkv, wq_t, wkv_t, gqh, gkh, tm):
    t, rq = cq.shape
    rkv = ckv.shape[1]
    heads = wq_t.shape[0]

    def body(cq_ref, ckv_ref, kr_ref, pos_ref, freq_ref, mk_ref, gq_ref, gkv_ref, wq_ref, wkv_ref,
             gqh_ref, gkh_ref, q_ref, k_ref, v_ref):
        cos, sin = _rope_tables(pos_ref, freq_ref)
        m_lo, m_hi = mk_ref[0:1, :], mk_ref[1:2, :]
        cqn = _rms_fwd(cq_ref[...], gq_ref[...], rq).astype(BF16)
        ckvn = _rms_fwd(ckv_ref[...], gkv_ref[...], rkv).astype(BF16)
        kr = kr_ref[...]
        kr_ss = jnp.sum(kr * kr, axis=-1, keepdims=True)
        for h in range(heads):
            qh = _nt(cqn, wq_ref[h])
            qn = qh * _rstd(qh, QK) * gqh_ref[...]
            qr = qn[:, LANE:]
            q_ref[h, :, 0:LANE] = qn[:, 0:LANE].astype(BF16)
            q_ref[h, :, LANE:] = (qr * cos + _rot(qr, m_lo, m_hi) * sin).astype(BF16)
            kvh = _nt(ckvn, wkv_ref[h])
            kn = kvh[:, 0:LANE]
            rk = lax.rsqrt((jnp.sum(kn * kn, axis=-1, keepdims=True) + kr_ss) * (1.0 / QK) + EPS)
            k_ref[h, :, 0:LANE] = (kn * rk * gkh_ref[:, 0:LANE]).astype(BF16)
            krn = kr * rk * gkh_ref[:, LANE:]
            k_ref[h, :, LANE:] = (krn * cos + _rot(krn, m_lo, m_hi) * sin).astype(BF16)
            v_ref[h] = kvh[:, LANE:].astype(BF16)

    c2 = lambda i: (0, 0)
    c3 = lambda i: (0, 0, 0)
    return pl.pallas_call(
        body, name="mla_proj_fwd", grid=(t // tm,),
        in_specs=[pl.BlockSpec((tm, rq), lambda i: (i, 0)), pl.BlockSpec((tm, rkv), lambda i: (i, 0)),
                  pl.BlockSpec((tm, LANE), lambda i: (i, 0)), pl.BlockSpec((tm, 1), lambda i: (i, 0)),
                  pl.BlockSpec((1, LANE), c2), pl.BlockSpec((2, LANE), c2),
                  pl.BlockSpec((1, rq), c2), pl.BlockSpec((1, rkv), c2),
                  pl.BlockSpec((heads, HEADW, rq), c3), pl.BlockSpec((heads, HEADW, rkv), c3),
                  pl.BlockSpec((1, HEADW), c2), pl.BlockSpec((1, HEADW), c2)],
        out_specs=[pl.BlockSpec((heads, tm, HEADW), lambda i: (0, i, 0)),
                   pl.BlockSpec((heads, tm, HEADW), lambda i: (0, i, 0)),
                   pl.BlockSpec((heads, tm, VHEAD), lambda i: (0, i, 0))],
        out_shape=[jax.ShapeDtypeStruct((heads, t, HEADW), BF16), jax.ShapeDtypeStruct((heads, t, HEADW), BF16),
                   jax.ShapeDtypeStruct((heads, t, VHEAD), BF16)],
        compiler_params=_params(("arbitrary",)),
    )(cq, ckv, krw, pos, freq, masks, gq, gkv, wq_t, wkv_t, gqh, gkh)


def _mla_proj_bwd(dq, dk, dv, cq, ckv, krw, pos, freq, masks, gq, gkv, wq_t, wkv_t, gqh, gkh, tm):
    t, rq = cq.shape
    rkv = ckv.shape[1]
    heads = wq_t.shape[0]

    def body(dq_ref, dk_ref, dv_ref, cq_ref, ckv_ref, kr_ref, pos_ref, freq_ref, mk_ref, gq_ref, gkv_ref,
             wq_ref, wkv_ref, gqh_ref, gkh_ref,
             dcq_ref, dckv_ref, dkr_ref, dwq_ref, dwkv_ref, dgq_ref, dgkv_ref, dgqh_ref, dgkh_ref):
        @pl.when(pl.program_id(0) == 0)
        def _():
            for r in (dwq_ref, dwkv_ref, dgq_ref, dgkv_ref, dgqh_ref, dgkh_ref):
                r[...] = jnp.zeros_like(r)

        cos, sin = _rope_tables(pos_ref, freq_ref)
        m_lo, m_hi = mk_ref[0:1, :], mk_ref[1:2, :]

        def unrope(dy):
            return dy * cos - _rot(dy * sin, m_lo, m_hi)

        cqn = _rms_fwd(cq_ref[...], gq_ref[...], rq).astype(BF16)
        ckvn = _rms_fwd(ckv_ref[...], gkv_ref[...], rkv).astype(BF16)
        kr = kr_ref[...]
        kr_ss = jnp.sum(kr * kr, axis=-1, keepdims=True)
        dcqn = jnp.zeros((tm, rq), F32)
        dckvn = jnp.zeros((tm, rkv), F32)
        dkr = jnp.zeros((tm, LANE), F32)
        for h in range(heads):
            qh = _nt(cqn, wq_ref[h])
            dqn = jnp.concatenate([dq_ref[h, :, 0:LANE], unrope(dq_ref[h, :, LANE:])], axis=1)
            dqh, dg = _rms_bwd(qh, gqh_ref[...], dqn, QK)
            dgqh_ref[...] += dg
            dqh = dqh.astype(BF16)
            dcqn += _nn(dqh, wq_ref[h])
            dwq_ref[h] += _tn(dqh, cqn)

            kvh = _nt(ckvn, wkv_ref[h])
            kn = kvh[:, 0:LANE]
            rk = lax.rsqrt((jnp.sum(kn * kn, axis=-1, keepdims=True) + kr_ss) * (1.0 / QK) + EPS)
            dkn_n = dk_ref[h, :, 0:LANE]
            dkr_n = unrope(dk_ref[h, :, LANE:])
            knh, krh = kn * rk, kr * rk
            dgkh_ref[:, 0:LANE] += jnp.sum(dkn_n * knh, axis=0, keepdims=True)
            dgkh_ref[:, LANE:] += jnp.sum(dkr_n * krh, axis=0, keepdims=True)
            dkn_g, dkr_g = dkn_n * gkh_ref[:, 0:LANE], dkr_n * gkh_ref[:, LANE:]
            proj = (jnp.sum(dkn_g * knh, axis=-1, keepdims=True)
                    + jnp.sum(dkr_g * krh, axis=-1, keepdims=True)) * (1.0 / QK)
            dkr += rk * (dkr_g - krh * proj)
            dkvh = jnp.concatenate([rk * (dkn_g - knh * proj), dv_ref[h]], axis=1).astype(BF16)
            dckvn += _nn(dkvh, wkv_ref[h])
            dwkv_ref[h] += _tn(dkvh, ckvn)
        dkr_ref[...] = dkr
        dcq, dg = _rms_bwd(cq_ref[...], gq_ref[...], dcqn, rq)
        dcq_ref[...] = dcq
        dgq_ref[...] += dg
        dckv, dg = _rms_bwd(ckv_ref[...], gkv_ref[...], dckvn, rkv)
        dckv_ref[...] = dckv
        dgkv_ref[...] += dg

    c2 = lambda i: (0, 0)
    c3 = lambda i: (0, 0, 0)
    hq = pl.BlockSpec((heads, tm, HEADW), lambda i: (0, i, 0))
    return pl.pallas_call(
        body, name="mla_proj_bwd", grid=(t // tm,),
        in_specs=[hq, hq, pl.BlockSpec((heads, tm, VHEAD), lambda i: (0, i, 0)),
                  pl.BlockSpec((tm, rq), lambda i: (i, 0)), pl.BlockSpec((tm, rkv), lambda i: (i, 0)),
                  pl.BlockSpec((tm, LANE), lambda i: (i, 0)), pl.BlockSpec((tm, 1), lambda i: (i, 0)),
                  pl.BlockSpec((1, LANE), c2), pl.BlockSpec((2, LANE), c2),
                  pl.BlockSpec((1, rq), c2), pl.BlockSpec((1, rkv), c2),
                  pl.BlockSpec((heads, HEADW, rq), c3), pl.BlockSpec((heads, HEADW, rkv), c3),
                  pl.BlockSpec((1, HEADW), c2), pl.BlockSpec((1, HEADW), c2)],
        out_specs=[pl.BlockSpec((tm, rq), lambda i: (i, 0)), pl.BlockSpec((tm, rkv), lambda i: (i, 0)),
                   pl.BlockSpec((tm, LANE), lambda i: (i, 0)),
                   pl.BlockSpec((heads, HEADW, rq), c3), pl.BlockSpec((heads, HEADW, rkv), c3),
                   pl.BlockSpec((1, rq), c2), pl.BlockSpec((1, rkv), c2),
                   pl.BlockSpec((1, HEADW), c2), pl.BlockSpec((1, HEADW), c2)],
        out_shape=[jax.ShapeDtypeStruct((t, rq), F32), jax.ShapeDtypeStruct((t, rkv), F32),
                   jax.ShapeDtypeStruct((t, LANE), F32),
                   jax.ShapeDtypeStruct((heads, HEADW, rq), F32), jax.ShapeDtypeStruct((heads, HEADW, rkv), F32),
                   jax.ShapeDtypeStruct((1, rq), F32), jax.ShapeDtypeStruct((1, rkv), F32),
                   jax.ShapeDtypeStruct((1, HEADW), F32), jax.ShapeDtypeStruct((1, HEADW), F32)],
        compiler_params=_params(("arbitrary",)),
    )(dq, dk, dv, cq, ckv, krw, pos, freq, masks, gq, gkv, wq_t, wkv_t, gqh, gkh)


def _lower_triangle(blk):
    return lax.broadcasted_iota(jnp.int32, (blk, blk), 0) >= lax.broadcasted_iota(jnp.int32, (blk, blk), 1)


def _attn_fwd(q, k, v, seq, blk):
    heads, t, _ = q.shape
    scale = QK ** -0.5
    nblk = seq // blk

    def body(q_ref, k_ref, v_ref, o_ref, lse_ref):
        tri = _lower_triangle(blk)
        for qi in range(nblk):
            rows = slice(qi * blk, (qi + 1) * blk)
            before = slice(0, qi * blk)
            qb = q_ref[0, rows, :]
            s_d = jnp.where(tri, _nt(qb, k_ref[0, rows, :]) * scale, -1e30)
            m = jnp.max(s_d, axis=-1, keepdims=True)
            if qi:
                s_b = _nt(qb, k_ref[0, before, :]) * scale
                m = jnp.maximum(m, jnp.max(s_b, axis=-1, keepdims=True))
                p_b = jnp.exp(s_b - m)
            p_d = jnp.exp(s_d - m)
            l = jnp.sum(p_d, axis=-1, keepdims=True)
            acc = _nn(p_d.astype(BF16), v_ref[0, rows, :])
            if qi:
                l += jnp.sum(p_b, axis=-1, keepdims=True)
                acc += _nn(p_b.astype(BF16), v_ref[0, before, :])
            o_ref[0, rows, :] = acc / l
            lse_ref[0, rows, :] = m + jnp.log(l)

    return pl.pallas_call(
        body, name="attn_fwd", grid=(heads, t // seq),
        in_specs=[pl.BlockSpec((1, seq, HEADW), lambda h, b: (h, b, 0)),
                  pl.BlockSpec((1, seq, HEADW), lambda h, b: (h, b, 0)),
                  pl.BlockSpec((1, seq, VHEAD), lambda h, b: (h, b, 0))],
        out_specs=[pl.BlockSpec((1, seq, VHEAD), lambda h, b: (h, b, 0)),
                   pl.BlockSpec((1, seq, 1), lambda h, b: (h, b, 0))],
        out_shape=[jax.ShapeDtypeStruct((heads, t, VHEAD), F32), jax.ShapeDtypeStruct((heads, t, 1), F32)],
        compiler_params=_params(("arbitrary", "arbitrary")),
    )(q, k, v)


def _attn_bwd(q, k, v, do, lse, delta, seq, blk, after):
    heads, t, _ = q.shape
    scale = QK ** -0.5
    nblk = seq // blk

    def body(q_ref, k_ref, v_ref, do_ref, lse_ref, dl_ref, _, dq_ref, dk_ref, dv_ref):
        tri = _lower_triangle(blk)
        dk_ref[...] = jnp.zeros_like(dk_ref)
        dv_ref[...] = jnp.zeros_like(dv_ref)
        for qi in range(nblk):
            rows = slice(qi * blk, (qi + 1) * blk)
            qb = q_ref[0, rows, :]
            dob = do_ref[0, rows, :]
            lse_b = lse_ref[0, rows, :]
            dl_b = dl_ref[0, rows, :]
            dq = jnp.zeros((blk, HEADW), F32)
            for keys, masked in ((slice(0, qi * blk), False), (rows, True)):
                if keys.stop == keys.start:
                    continue
                kb = k_ref[0, keys, :]
                p = jnp.exp(_nt(qb, kb) * scale - lse_b)
                if masked:
                    p = jnp.where(tri, p, 0.0)
                dp = _nt(dob, v_ref[0, keys, :])
                ds = (p * (dp - dl_b) * scale).astype(BF16)
                dv_ref[0, keys, :] += _tn(p.astype(BF16), dob)
                dk_ref[0, keys, :] += _tn(ds, qb)
                dq += _nn(ds, kb)
            dq_ref[0, rows, :] = dq

    hq = pl.BlockSpec((1, seq, HEADW), lambda h, b: (h, b, 0))
    hv = pl.BlockSpec((1, seq, VHEAD), lambda h, b: (h, b, 0))
    h1 = pl.BlockSpec((1, seq, 1), lambda h, b: (h, b, 0))
    return pl.pallas_call(
        body, name="attn_bwd", grid=(heads, t // seq),
        in_specs=[hq, hq, hv, hv, h1, h1, ANY],
        out_specs=[hq, hq, hv],
        out_shape=[jax.ShapeDtypeStruct((heads, t, HEADW), F32), jax.ShapeDtypeStruct((heads, t, HEADW), F32),
                   jax.ShapeDtypeStruct((heads, t, VHEAD), F32)],
        compiler_params=_params(("arbitrary", "arbitrary")),
    )(q, k, v, do, lse, delta, after)


def _out_fwd(ya, o, gb, w_out, x1, tm):
    t, w = ya.shape
    heads = o.shape[0]
    d = x1.shape[1]

    def body(ya_ref, o_ref, gb_ref, w_ref, x1_ref, x2_ref, yc_ref):
        yc_ref[:, 0:w] = ya_ref[...]
        for h in range(heads):
            lanes = slice(h * VHEAD, (h + 1) * VHEAD)
            yc_ref[:, w + h * VHEAD:w + (h + 1) * VHEAD] = _rms_fwd(o_ref[h], gb_ref[:, lanes], VHEAD).astype(BF16)
        x2_ref[...] = x1_ref[...] + _nn(yc_ref[...], w_ref[...])

    wy = w + heads * VHEAD
    row = pl.BlockSpec((tm, d), lambda i: (i, 0))
    return pl.pallas_call(
        body, name="out_fwd", grid=(t // tm,),
        in_specs=[pl.BlockSpec((tm, w), lambda i: (i, 0)), pl.BlockSpec((heads, tm, VHEAD), lambda i: (0, i, 0)),
                  pl.BlockSpec((1, heads * VHEAD), lambda i: (0, 0)), WHOLE_VMEM, row],
        out_specs=[row, pl.BlockSpec((tm, wy), lambda i: (i, 0))],
        out_shape=[jax.ShapeDtypeStruct((t, d), F32), jax.ShapeDtypeStruct((t, wy), BF16)],
        compiler_params=_params(("arbitrary",)),
    )(ya, o, gb, w_out, x1)


def _out_bwd(dx2, o, gb, w_out, w, tm):
    t, d = dx2.shape
    heads = o.shape[0]

    def body(dx_ref, o_ref, gb_ref, w_ref, dya_ref, do_ref, dl_ref, dgb_ref):
        @pl.when(pl.program_id(0) == 0)
        def _():
            dgb_ref[...] = jnp.zeros_like(dgb_ref)

        dyc = _nt(dx_ref[...].astype(BF16), w_ref[...])
        dya_ref[...] = dyc[:, 0:w]
        for h in range(heads):
            lanes = slice(h * VHEAD, (h + 1) * VHEAD)
            oh = o_ref[h]
            doh, dg = _rms_bwd(oh, gb_ref[:, lanes], dyc[:, w + h * VHEAD:w + (h + 1) * VHEAD], VHEAD)
            dgb_ref[:, lanes] += dg
            do_ref[h] = doh.astype(BF16)
            dl_ref[h] = jnp.sum(doh * oh, axis=-1, keepdims=True)

    ho = pl.BlockSpec((heads, tm, VHEAD), lambda i: (0, i, 0))
    vec = pl.BlockSpec((1, heads * VHEAD), lambda i: (0, 0))
    return pl.pallas_call(
        body, name="out_bwd", grid=(t // tm,),
        in_specs=[pl.BlockSpec((tm, d), lambda i: (i, 0)), ho, vec, WHOLE_VMEM],
        out_specs=[pl.BlockSpec((tm, w), lambda i: (i, 0)), ho, pl.BlockSpec((heads, tm, 1), lambda i: (0, i, 0)), vec],
        out_shape=[jax.ShapeDtypeStruct((t, w), F32), jax.ShapeDtypeStruct((heads, t, VHEAD), BF16),
                   jax.ShapeDtypeStruct((heads, t, 1), F32), jax.ShapeDtypeStruct((1, heads * VHEAD), F32)],
        compiler_params=_params(("arbitrary",)),
    )(dx2, o, gb, w_out)


def _loss_head(y, target, tm):
    t, d = y.shape

    def body(y_ref, t_ref, dy_ref, loss_ref):
        @pl.when(pl.program_id(0) == 0)
        def _():
            loss_ref[...] = jnp.zeros_like(loss_ref)

        err = y_ref[...] - t_ref[...]
        dy_ref[...] = err * (1.0 / d)
        part = jnp.sum(jnp.sum(err * err, axis=-1, keepdims=True) * (1.0 / d), axis=0, keepdims=True)
        loss_ref[...] += 0.5 * part

    row = pl.BlockSpec((tm, d), lambda i: (i, 0))
    return pl.pallas_call(
        body, name="loss_head", grid=(t // tm,),
        in_specs=[row, row], out_specs=[row, pl.BlockSpec((1, 1), lambda i: (0, 0))],
        out_shape=[jax.ShapeDtypeStruct((t, d), F32), jax.ShapeDtypeStruct((1, 1), F32)],
        compiler_params=_params(("arbitrary",)),
    )(y, target)


def _place():
    return lax.axis_index("x"), lax.axis_index("y"), lax.axis_index("c")


HBM = pl.BlockSpec(memory_space=pltpu.HBM)
SEM = pl.BlockSpec(memory_space=pltpu.SEMAPHORE)
DATAFLOW = pltpu.SideEffectType.DATAFLOW_SIDE_EFFECTING


def _plan_copies(plan, refs, send_sems, recv_sems):
    def block(ref, blk):
        if blk is None:
            return ref
        return ref.at[blk[0], pl.ds(0, blk[1])] if isinstance(blk, tuple) else ref.at[blk]

    cps = []
    for i, (sb, sblk, db, dblk, dev) in enumerate(plan(*_place())):
        cps.append(pltpu.make_async_remote_copy(
            src_ref=block(refs[sb], sblk), dst_ref=block(refs[db], dblk),
            send_sem=send_sems.at[i], recv_sem=recv_sems.at[i], device_id=dev, device_id_type=MESH))
    return cps


def _push_start(bufs, plan, ncopy, name, deps=()):
    nb = len(bufs)

    def body(*refs):
        outs = refs[nb + len(deps):]
        for cp in _plan_copies(plan, refs[:nb], outs[0], outs[1]):
            cp.start()
        outs[-1][...] = jnp.zeros_like(outs[-1])

    res = pl.pallas_call(
        body, name=name,
        out_shape=(pltpu.SemaphoreType.DMA((ncopy,)), pltpu.SemaphoreType.DMA((ncopy,)),
                   *[pltpu.HBM(b.shape, b.dtype) for b in bufs], jax.ShapeDtypeStruct((SUBLANE, LANE), F32)),
        in_specs=[HBM] * nb + [ANY] * len(deps),
        out_specs=(SEM, SEM, *[HBM] * nb, WHOLE_VMEM),
        input_output_aliases={i: 2 + i for i in range(nb)},
        compiler_params=pltpu.CompilerParams(has_side_effects=DATAFLOW),
    )(*[pltpu.with_memory_space_constraint(b, pltpu.HBM) for b in bufs], *deps)
    return res[0], res[1], list(res[2:2 + nb]), res[-1]


def _push_wait(send_sems, recv_sems, bufs, plan, after, name):
    nb = len(bufs)

    def body(*refs):
        for cp in _plan_copies(plan, refs[:nb], refs[nb], refs[nb + 1]):
            cp.wait_send()
            cp.wait_recv()

    res = pl.pallas_call(
        body, name=name,
        out_shape=[pltpu.HBM(b.shape, b.dtype) for b in bufs],
        in_specs=[HBM] * nb + [SEM, SEM, ANY], out_specs=[HBM] * nb,
        input_output_aliases={i: i for i in range(nb)},
        compiler_params=pltpu.CompilerParams(has_side_effects=DATAFLOW),
    )(*bufs, send_sems, recv_sems, after)
    return list(res)


def _other_chips(x, y):
    return ((1 - x, y), (x, 1 - y), (1 - x, 1 - y))


class _Exchange:
    def __init__(self, bufs, plan, ncopy, name, deps=()):
        self.plan, self.name = plan, name
        self.send, self.recv, self.bufs, self.token = _push_start(bufs, plan, ncopy, name + "_start", deps)

    def wait(self, after):
        return _push_wait(self.send, self.recv, self.bufs, self.plan, after, self.name + "_wait")


class _Chain:
    def __init__(self, bufs):
        self.bufs = list(bufs)

    def start(self, plan, ncopy, name, deps=()):
        send, recv, self.bufs, token = _push_start(self.bufs, plan, ncopy, name + "_start", deps)
        return (send, recv, plan, name), token

    def wait(self, pending, after):
        send, recv, plan, name = pending
        self.bufs = _push_wait(send, recv, self.bufs, plan, after, name + "_wait")


class _StagedGather:
    def __init__(self, shards, me, name, pad_to=None):
        self.n = n = len(shards)
        self.name = name
        rows = shards[0].shape[0]
        lands = []
        for s in shards:
            land = lax.empty((N_DEV, pad_to or rows) + s.shape[1:], s.dtype)
            if pad_to and pad_to != rows:
                land = lax.dynamic_update_slice(
                    land, jnp.zeros((N_DEV, pad_to - rows) + s.shape[1:], s.dtype), (0, rows, 0))
            lands.append(lax.dynamic_update_slice(land, s[None], (me, 0, 0)))
        self.chain = _Chain(list(shards) + lands)
        self.pending = {}

        def blk(b):
            return (b, rows) if pad_to and pad_to != rows else b

        def to_sibling(blocks):
            return lambda x, y, c: [(n + a, blk(b), n + a, blk(b), (x, y, 1 - c))
                                    for a in range(n) for b in blocks(x, y, c)]

        def nbr_blocks(x, y, c):
            return [4 * (1 - x) + 2 * y + c, 4 * x + 2 * (1 - y) + c]

        def diag(x, y, c):
            sx, sy = (1 - x) * (1 - c) + x * c, y * (1 - c) + (1 - y) * c
            tx, ty = x * (1 - c) + (1 - x) * c, (1 - y) * (1 - c) + y * c
            b = blk(4 * sx + 2 * sy + c)
            return [(n + a, b, n + a, b, (tx, ty, c)) for a in range(n)]

        self.plans = {
            "own": (lambda x, y, c: [(a, None, n + a, blk(4 * x + 2 * y + c), (x, y, 1 - c)) for a in range(n)], n),
            "nbr": (lambda x, y, c: [(a, None, n + a, blk(4 * x + 2 * y + c), dev) for a in range(n)
                                     for dev in ((1 - x, y, c), (x, 1 - y, c))], 2 * n),
            "diag": (diag, n),
            "nbr_d2d": (to_sibling(nbr_blocks), 2 * n),
            "own_nbr_d2d": (to_sibling(lambda x, y, c: [4 * x + 2 * y + c] + nbr_blocks(x, y, c)), 3 * n),
            "diag_d2d": (to_sibling(lambda x, y, c: [4 * (1 - x) + 2 * (1 - y) + c]), n),
        }

    def start(self, stage, deps=()):
        plan, ncopy = self.plans[stage]
        self.pending[stage], token = self.chain.start(plan, ncopy, self.name + "_" + stage, deps)
        return token

    def wait(self, stage, after):
        self.chain.wait(self.pending.pop(stage), after)

    def lands(self):
        return self.chain.bufs[self.n:]


def _gather_ici(shards, me, name, deps=()):
    n = len(shards)
    lands = [lax.dynamic_update_slice(lax.empty((N_DEV,) + s.shape, s.dtype), s[None], (me, 0, 0)) for s in shards]

    def plan(x, y, c):
        return [(a, None, n + a, 4 * x + 2 * y + c, (px, py, c)) for a in range(n) for px, py in _other_chips(x, y)]

    return _Exchange(list(shards) + lands, plan, 3 * n, name, deps)


def _gather_d2d(lands, name, deps=()):
    n = len(lands)

    def plan(x, y, c):
        blocks = [4 * x + 2 * y + c] + [4 * px + 2 * py + c for px, py in _other_chips(x, y)]
        return [(a, b, a, b, (x, y, 1 - c)) for a in range(n) for b in blocks]

    return _Exchange(list(lands), plan, 4 * n, name, deps)


def _reduce_d2d(grads, name, deps=(), rows=None):
    n = len(grads)
    lands = [lax.empty((4,) + g.shape[1:], g.dtype) for g in grads]

    def blk(b):
        return b if rows is None else (b, rows)

    def plan(x, y, c):
        return [(a, blk(2 * s + (1 - c)), n + a, blk(s), (x, y, 1 - c)) for a in range(n) for s in range(4)]

    return _Exchange(list(grads) + lands, plan, 4 * n, name, deps)


def _reduce_ici(chip, name, deps=(), rows=None):
    n = len(chip)
    lands = [lax.empty((3,) + g.shape[1:], g.dtype) for g in chip]

    def blk(b):
        return b if rows is None else (b, rows)

    def plan(x, y, c):
        return [(a, blk(2 * px + py), n + a, blk(k), (px, py, c))
                for a in range(n) for k, (px, py) in enumerate(_other_chips(x, y))]

    return _Exchange(list(chip) + lands, plan, 3 * n, name, deps)


def _pair_add(full, got, core, name, rows=None):
    _, r, cdim = full.shape
    br = _row_block(rows or r, 512)

    def body(c_ref, f_ref, g_ref, o_ref):
        o_ref[...] = (f_ref[...].astype(F32) + g_ref[...].astype(F32)).astype(o_ref.dtype)

    return pl.pallas_call(
        body, name=name,
        grid_spec=pltpu.PrefetchScalarGridSpec(
            num_scalar_prefetch=1, grid=(4, (rows or r) // br),
            in_specs=[pl.BlockSpec((1, br, cdim), lambda s, i, c_ref: (2 * s + c_ref[0], i, 0)),
                      pl.BlockSpec((1, br, cdim), lambda s, i, c_ref: (s, i, 0))],
            out_specs=pl.BlockSpec((1, br, cdim), lambda s, i, c_ref: (s, i, 0))),
        out_shape=jax.ShapeDtypeStruct((4, r, cdim), full.dtype),
        compiler_params=_params(("arbitrary", "arbitrary")),
    )(core, full, got)


def _sum_devices(stack):
    _, r, cdim = stack.shape

    def body(s_ref, o_ref):
        acc = s_ref[0]
        for k in range(1, N_DEV):
            acc = acc + s_ref[k]
        o_ref[...] = acc

    return pl.pallas_call(
        body, name="sum_devices", out_shape=jax.ShapeDtypeStruct((r, cdim), F32),
        compiler_params=_params(),
    )(stack)


def _adamw(w, g, m, v, name):
    r, cdim = w.shape
    br = _row_block(r, 256)

    def body(w_ref, g_ref, m_ref, v_ref, d_ref, nm_ref, nv_ref):
        g = g_ref[...]
        nm = ADAM_B1 * m_ref[...] + (1.0 - ADAM_B1) * g
        nv = ADAM_B2 * v_ref[...] + (1.0 - ADAM_B2) * (g * g)
        m_hat = nm / (1.0 - ADAM_B1 ** ADAM_STEP)
        v_hat = nv / (1.0 - ADAM_B2 ** ADAM_STEP)
        d_ref[...] = -ADAM_LR * (m_hat / (jnp.sqrt(v_hat) + ADAM_EPS) + ADAM_WD * w_ref[...])
        nm_ref[...] = nm
        nv_ref[...] = nv

    spec = pl.BlockSpec((br, cdim), lambda i: (i, 0))
    shape = jax.ShapeDtypeStruct((r, cdim), F32)
    return pl.pallas_call(
        body, name=name, grid=(r // br,), in_specs=[spec] * 4, out_specs=[spec] * 3,
        out_shape=[shape] * 3, compiler_params=_params(("arbitrary",)),
    )(w, g, m, v)


def _sum_adamw(chip, got, slot, w, m, v, name, after):
    rows, cdim = w.shape
    bc = 2 * LANE if cdim % (2 * LANE) == 0 else cdim

    def body(s_ref, c_ref, g_ref, w_ref, m_ref, v_ref, _, go_ref, d_ref, nm_ref, nv_ref):
        g = c_ref[0].astype(F32)
        for k in range(3):
            g = g + g_ref[k].astype(F32)
        nm = ADAM_B1 * m_ref[...] + (1.0 - ADAM_B1) * g
        nv = ADAM_B2 * v_ref[...] + (1.0 - ADAM_B2) * (g * g)
        m_hat = nm / (1.0 - ADAM_B1 ** ADAM_STEP)
        v_hat = nv / (1.0 - ADAM_B2 ** ADAM_STEP)
        go_ref[...] = g
        d_ref[...] = -ADAM_LR * (m_hat / (jnp.sqrt(v_hat) + ADAM_EPS) + ADAM_WD * w_ref[...])
        nm_ref[...] = nm
        nv_ref[...] = nv

    spec = pl.BlockSpec((rows, bc), lambda j, s_ref: (0, j))
    shape = jax.ShapeDtypeStruct((rows, cdim), F32)
    return pl.pallas_call(
        body, name=name,
        grid_spec=pltpu.PrefetchScalarGridSpec(
            num_scalar_prefetch=1, grid=(cdim // bc,),
            in_specs=[pl.BlockSpec((1, rows, bc), lambda j, s_ref: (s_ref[0], 0, j)),
                      pl.BlockSpec((3, rows, bc), lambda j, s_ref: (0, 0, j)), spec, spec, spec, ANY],
            out_specs=[spec] * 4),
        out_shape=[shape] * 4,
        compiler_params=_params(("arbitrary",)),
    )(slot, chip, got, w, m, v, after)


WEIGHTS = ("ffn1_norm_g", "ffn1_w_gate", "ffn1_w_up", "ffn1_w_down", "mix_norm_g", "w_in", "gmlp_v_norm_g",
           "gmlp_w_s", "gmlp_b_s", "mla_q_norm_g", "mla_w_q_up", "mla_kv_norm_g", "mla_w_kv_up", "mla_q_head_g",
           "mla_k_head_g", "gmlp_out_g", "mla_out_g", "w_out", "ffn2_norm_g", "ffn2_w_gate", "ffn2_w_up",
           "ffn2_w_down")
SHARDED = {"ffn1_w_gate": True, "ffn1_w_up": True, "ffn1_w_down": False, "w_in": True, "mla_w_q_up": True,
           "mla_w_kv_up": True, "w_out": False, "ffn2_w_gate": True, "ffn2_w_up": True, "ffn2_w_down": False}


def _col_block(m, target):
    best = LANE
    for cand in range(LANE, min(m, target) + 1, LANE):
        if m % cand == 0:
            best = cand
    return best


def _shard_rows(w, transposed, pad_to=None):
    rows = (w[0].T if transposed else w[0]).astype(BF16)
    if pad_to is not None and pad_to != rows.shape[0]:
        rows = jnp.pad(rows, ((0, pad_to - rows.shape[0]), (0, 0)))
    return rows


def _pack(parts):
    flat = []
    for p in parts:
        f = p.reshape(-1).astype(F32)
        flat.append(jnp.pad(f, (0, _round_up(f.size, LANE) - f.size)))
    flat = jnp.concatenate(flat)
    rows = _round_up(flat.size // LANE, SUBLANE)
    return jnp.pad(flat, (0, rows * LANE - flat.size)).reshape(rows, LANE)


def _unpack(packed, shapes):
    out, row = [], 0
    for shp in shapes:
        size = 1
        for s in shp:
            size *= s
        nrows = _round_up(size, LANE) // LANE
        out.append(packed[row:row + nrows].reshape(-1)[:size].reshape(shp))
        row += nrows
    return out


def kernel(x, positions, ffn1_norm_g, ffn1_w_gate, ffn1_w_up, ffn1_w_down, mix_norm_g, w_in, gmlp_v_norm_g, gmlp_w_s, gmlp_b_s, mla_q_norm_g, mla_w_q_up, mla_kv_norm_g, mla_w_kv_up, mla_q_head_g, mla_k_head_g, gmlp_out_g, mla_out_g, w_out, ffn2_norm_g, ffn2_w_gate, ffn2_w_up, ffn2_w_down, loss_target, m_ffn1_norm_g, m_ffn1_w_gate, m_ffn1_w_up, m_ffn1_w_down, m_mix_norm_g, m_w_in, m_gmlp_v_norm_g, m_gmlp_w_s, m_gmlp_b_s, m_mla_q_norm_g, m_mla_w_q_up, m_mla_kv_norm_g, m_mla_w_kv_up, m_mla_q_head_g, m_mla_k_head_g, m_gmlp_out_g, m_mla_out_g, m_w_out, m_ffn2_norm_g, m_ffn2_w_gate, m_ffn2_w_up, m_ffn2_w_down, v_ffn1_norm_g, v_ffn1_w_gate, v_ffn1_w_up, v_ffn1_w_down, v_mix_norm_g, v_w_in, v_gmlp_v_norm_g, v_gmlp_w_s, v_gmlp_b_s, v_mla_q_norm_g, v_mla_w_q_up, v_mla_kv_norm_g, v_mla_w_kv_up, v_mla_q_head_g, v_mla_k_head_g, v_gmlp_out_g, v_mla_out_g, v_w_out, v_ffn2_norm_g, v_ffn2_w_gate, v_ffn2_w_up, v_ffn2_w_down):
    wts = dict(zip(WEIGHTS, (ffn1_norm_g, ffn1_w_gate, ffn1_w_up, ffn1_w_down, mix_norm_g, w_in, gmlp_v_norm_g, gmlp_w_s, gmlp_b_s, mla_q_norm_g, mla_w_q_up, mla_kv_norm_g, mla_w_kv_up, mla_q_head_g, mla_k_head_g, gmlp_out_g, mla_out_g, w_out, ffn2_norm_g, ffn2_w_gate, ffn2_w_up, ffn2_w_down)))
    mom1 = dict(zip(WEIGHTS, (m_ffn1_norm_g, m_ffn1_w_gate, m_ffn1_w_up, m_ffn1_w_down, m_mix_norm_g, m_w_in, m_gmlp_v_norm_g, m_gmlp_w_s, m_gmlp_b_s, m_mla_q_norm_g, m_mla_w_q_up, m_mla_kv_norm_g, m_mla_w_kv_up, m_mla_q_head_g, m_mla_k_head_g, m_gmlp_out_g, m_mla_out_g, m_w_out, m_ffn2_norm_g, m_ffn2_w_gate, m_ffn2_w_up, m_ffn2_w_down)))
    mom2 = dict(zip(WEIGHTS, (v_ffn1_norm_g, v_ffn1_w_gate, v_ffn1_w_up, v_ffn1_w_down, v_mix_norm_g, v_w_in, v_gmlp_v_norm_g, v_gmlp_w_s, v_gmlp_b_s, v_mla_q_norm_g, v_mla_w_q_up, v_mla_kv_norm_g, v_mla_w_kv_up, v_mla_q_head_g, v_mla_k_head_g, v_gmlp_out_g, v_mla_out_g, v_w_out, v_ffn2_norm_g, v_ffn2_w_gate, v_ffn2_w_up, v_ffn2_w_down)))

    b_loc, seq, d = x.shape
    t = b_loc * seq
    ffs = ffn1_w_gate.shape[2]
    fp = _round_up(ffs, LANE)
    wg = gmlp_v_norm_g.shape[1]
    groups = gmlp_w_s.shape[1]
    rq, rkv = mla_q_norm_g.shape[1], mla_kv_norm_g.shape[1]
    heads = mla_out_g.shape[1]
    assert w_in.shape[2] * N_DEV == 2 * wg + rq + rkv + ROPE and mla_w_kv_up.shape[2] * N_DEV == heads * HEADW
    tm = min(512, t)
    tm_mix = min(256, t)
    blk = min(256, seq)

    xf = x.reshape(t, d)
    target = loss_target.reshape(t, d)
    pos = positions.reshape(t, 1).astype(F32)
    half = ROPE // 2
    inv_freq = 1.0 / (ROPE_THETA ** (jnp.arange(half, dtype=F32) / half))
    freq = jnp.concatenate([inv_freq, inv_freq, jnp.zeros((LANE - ROPE,), F32)])[None, :]
    lane = jnp.arange(LANE)
    masks = jnp.stack([jnp.where(lane < half, -1.0, 0.0),
                       jnp.where((lane >= half) & (lane < ROPE), 1.0, 0.0)]).astype(F32)
    gqh = jnp.pad(mla_q_head_g, ((0, 0), (0, HEADW - QK)))
    gkh = jnp.pad(mla_k_head_g, ((0, 0), (0, HEADW - QK)))
    bias = jnp.repeat(gmlp_b_s[0].T, CHUNK, axis=1)
    gouta = gmlp_out_g.reshape(1, wg)
    goutb = mla_out_g.reshape(1, heads * VHEAD)
    ws = gmlp_w_s[0]

    px, py, pc = _place()
    me = 4 * px + 2 * py + pc
    core = pc.astype(jnp.int32).reshape(1)
    slot = (2 * px + py).astype(jnp.int32).reshape(1)
    order = [n for n in WEIGHTS if n in SHARDED]
    group = {"ffn1": [n for n in order if n.startswith("ffn1")], "ffn2": [n for n in order if n.startswith("ffn2")],
             "mix": [n for n in order if not n.startswith("ffn")]}
    shard = {n: _shard_rows(wts[n], SHARDED[n]) for n in group["ffn1"]}
    frows = ffs if ffs != fp else None

    def tied(arr, token):
        return arr + token[0, 0].astype(arr.dtype)

    xnb, ynb, dgn = 4 * (1 - px) + 2 * py, 4 * px + 2 * (1 - py), 4 * (1 - px) + 2 * (1 - py)
    ids_a = jnp.stack([me, 4 * px + 2 * py + (1 - pc)]).astype(jnp.int32)
    ids_b = jnp.stack([xnb, xnb + 1, ynb, ynb + 1]).astype(jnp.int32)
    ids_c = jnp.stack([dgn, dgn + 1]).astype(jnp.int32)
    g1 = _StagedGather([shard[n] for n in group["ffn1"]], me, "gather_ffn1", pad_to=fp)
    token = g1.start("own")
    token = g1.start("nbr", deps=(token,))
    for n in group["mix"] + group["ffn2"]:
        shard[n] = _shard_rows(tied(wts[n], token), SHARDED[n])
    g3 = _StagedGather([shard[n] for n in group["ffn2"]], me, "gather_ffn2", pad_to=fp)
    g1.wait("own", token)
    x1, xn1, kept1 = _ffn_fwd(xf, None, ffn1_norm_g, ids_a, *g1.lands(), None, tm, "ffn1_fwd_a")
    g1.wait("nbr", x1)
    token = g1.start("diag")
    ici2 = _gather_ici([shard[n] for n in group["mix"]], me, "gather_mix_ici", deps=(token,))
    token = g3.start("nbr", deps=(ici2.token,))
    token = g1.start("nbr_d2d", deps=(token,))
    g1.wait("nbr_d2d", token)
    x1, xn1, kept1 = _ffn_fwd(x1, xn1, None, ids_b, *g1.lands(), kept1, tm, "ffn1_fwd_b")
    g1.wait("diag", x1)
    token = g1.start("diag_d2d")
    d2d2 = _gather_d2d(ici2.wait(x1)[len(group["mix"]):], "gather_mix_d2d", deps=(token,))
    g1.wait("diag_d2d", d2d2.token)
    full = dict(zip(group["ffn1"], g1.lands()))
    x1, xn1, (gd1, sl1, h1) = _ffn_fwd(x1, xn1, None, ids_c, full["ffn1_w_gate"], full["ffn1_w_up"],
                                       full["ffn1_w_down"], kept1, tm, "ffn1_fwd_c")
    full.update(zip(group["mix"], d2d2.wait(x1)))
    win_t = full["w_in"].reshape(-1, d)
    splits = (2 * wg, rq, rkv, LANE)
    wq_t = jnp.pad(full["mla_w_q_up"].reshape(heads, QK, rq), ((0, 0), (0, HEADW - QK), (0, 0)))
    wkv_t = full["mla_w_kv_up"].reshape(heads, HEADW, rkv)
    wout = full["w_out"].reshape(-1, d)
    hn, zuv, cq, ckv, krw = _inproj_fwd(x1, mix_norm_g, win_t, splits, tm)
    ya = _gmlp_fwd(zuv, gmlp_v_norm_g, ws, bias, gouta, tm_mix)
    g3.wait("nbr", ya)
    token = g3.start("diag")
    token = g3.start("own_nbr_d2d", deps=(token,))
    q, k, vv = _mla_proj_fwd(cq, ckv, krw, pos, freq, masks, mla_q_norm_g, mla_kv_norm_g, wq_t, wkv_t,
                             tied(gqh, token), gkh, tm)
    o, lse = _attn_fwd(q, k, vv, seq, blk)
    g3.wait("diag", o)
    token = g3.start("diag_d2d")
    x2, ycat = _out_fwd(ya, o, tied(goutb, token), wout, x1, tm)
    g3.wait("own_nbr_d2d", x2)
    g3.wait("diag_d2d", x2)
    full.update(zip(group["ffn2"], g3.lands()))
    x3, xn2, (gd2, sl2, h2) = _ffn_fwd(x2, None, ffn2_norm_g, jnp.arange(N_DEV, dtype=jnp.int32),
                                       full["ffn2_w_gate"], full["ffn2_w_up"], full["ffn2_w_down"], None, tm,
                                       "ffn2_fwd")
    dx3, loss_part = _loss_head(x3, target, tm)

    outs_g, outs_d, outs_m, outs_v = {}, {}, {}, {}

    def finish(names, chip, got, after):
        for n, cp, gt in zip(names, chip, got):
            rows_of = (lambda a: a[0].T) if SHARDED[n] else (lambda a: a[0])
            res = _sum_adamw(cp, gt, slot, rows_of(wts[n]), rows_of(mom1[n]), rows_of(mom2[n]), "adamw_" + n, after)
            outs_g[n], outs_d[n], outs_m[n], outs_v[n] = [r.T[None] if SHARDED[n] else r[None] for r in res]
            after = res[3]
        return after

    def chip_sums(names, ex, after, rows=None):
        res = ex.wait(after)
        return [_pair_add(f, gt, core, "pair_add_" + n, rows)
                for n, f, gt in zip(names, res[:len(names)], res[len(names):])]

    tk = min(1024, t)
    grads = {}
    small = {}
    dx2, small["ffn2_norm_g"], da2, db2 = _ffn_bwd(
        dx3, x2, ffn2_norm_g, gd2, sl2, full["ffn2_w_gate"], full["ffn2_w_up"], full["ffn2_w_down"], tm, "ffn2_bwd")
    grads["ffn2_w_gate"] = _matmul_tn(da2, xn2, fp, d, tk, BF16, "dw_ffn2_gate").reshape(N_DEV, fp, d)
    grads["ffn2_w_up"] = _matmul_tn(db2, xn2, fp, d, tk, BF16, "dw_ffn2_up").reshape(N_DEV, fp, d)
    grads["ffn2_w_down"] = _matmul_tn(h2, dx3, fp, d, tk, BF16, "dw_ffn2_down", rhs_scale=0.5).reshape(
        N_DEV, fp, d)
    red_a2 = _reduce_d2d([grads[n] for n in group["ffn2"]], "reduce_ffn2_d2d", rows=frows)
    dya, do, delta, small["mla_out_g"] = _out_bwd(dx2, o, tied(goutb, red_a2.token), wout, wg, tm)
    grads["w_out"] = _matmul_tn(ycat, dx2, _col_block(ycat.shape[1], 768), d, tk, BF16, "dw_out").reshape(
        N_DEV, -1, d)
    chip2 = chip_sums(group["ffn2"], red_a2, dya, frows)
    red_b2 = _reduce_ici(chip2, "reduce_ffn2_ici", rows=frows)
    dq, dk, dv = _attn_bwd(q, k, vv, do, lse, delta, seq, blk, red_b2.token)
    (dcq, dckv, dkrw, dwq, dwkv, small["mla_q_norm_g"], small["mla_kv_norm_g"], dgqh, dgkh) = _mla_proj_bwd(
        dq, dk, dv, cq, ckv, krw, pos, freq, masks, mla_q_norm_g, mla_kv_norm_g, wq_t, wkv_t, gqh, gkh, tm)
    small["mla_q_head_g"], small["mla_k_head_g"] = dgqh[:, :QK], dgkh[:, :QK]
    grads["mla_w_q_up"] = dwq[:, :QK].astype(BF16).reshape(N_DEV, -1, rq)
    grads["mla_w_kv_up"] = dwkv.astype(BF16).reshape(N_DEV, -1, rkv)
    dzuv, small["gmlp_w_s"], dbs, small["gmlp_v_norm_g"], small["gmlp_out_g"] = _gmlp_bwd(
        dya, zuv, gmlp_v_norm_g, ws, bias, gouta, tm_mix)
    small["gmlp_b_s"] = dbs[:, :, 0]
    dx1, small["mix_norm_g"], dzc = _inproj_bwd([dzuv, dcq, dckv, dkrw], x1, mix_norm_g, win_t, dx2, splits,
                                                tm_mix)
    grads["w_in"] = _matmul_tn(dzc, hn, _col_block(dzc.shape[1], 768), d, tk, BF16, "dw_in",
                               out_rows=win_t.shape[0]).reshape(N_DEV, -1, d)
    res_b2 = red_b2.wait(grads["w_in"])
    red_am = _reduce_d2d([grads[n] for n in group["mix"]], "reduce_mix_d2d")

    def ffn1_dw(n, lhs, rhs, scale, token):
        return _matmul_tn(lhs, rhs, fp, d, tk, BF16, "dw_" + n, rhs_scale=scale, deps=(token,)).reshape(N_DEV, fp, d)

    gr = ffn1_dw("ffn1_w_down", h1, dx1, 0.5, red_am.token)
    red_ad = _reduce_d2d([gr], "reduce_ffn1_w_down_d2d", deps=(red_am.token,), rows=frows)
    dx0, small["ffn1_norm_g"], da1, db1 = _ffn_bwd(
        dx1, xf, tied(ffn1_norm_g, red_ad.token), gd1, sl1, full["ffn1_w_gate"], full["ffn1_w_up"],
        full["ffn1_w_down"], tm, "ffn1_bwd")
    chipm = chip_sums(group["mix"], red_am, dx0)
    red_bm = _reduce_ici(chipm, "reduce_mix_ici")
    red_b = [("ffn1_w_down", _reduce_ici(chip_sums(["ffn1_w_down"], red_ad, dx0, frows), "reduce_ffn1_w_down_ici",
                                         deps=(red_bm.token,), rows=frows))]
    rep = [n for n in WEIGHTS if n not in SHARDED]
    small_ici = _gather_ici([_pack([small[n] for n in rep] + [loss_part])], me, "gather_small_ici",
                            deps=(red_b[-1][1].token,))
    gr = ffn1_dw("ffn1_w_gate", da1, xn1, None, small_ici.token)
    red_ag = _reduce_d2d([gr], "reduce_ffn1_w_gate_d2d", rows=frows)
    n2 = group["ffn2"]
    after = finish(n2[0:1], res_b2[0:1], res_b2[3:4], red_ag.token)
    red_b.append(("ffn1_w_gate", _reduce_ici(chip_sums(["ffn1_w_gate"], red_ag, after, frows),
                                             "reduce_ffn1_w_gate_ici", rows=frows)))
    gr = ffn1_dw("ffn1_w_up", db1, xn1, None, red_b[-1][1].token)
    small_d2d = _gather_d2d(small_ici.wait(gr)[1:], "gather_small_d2d")
    red_au = _reduce_d2d([gr], "reduce_ffn1_w_up_d2d", deps=(small_d2d.token,), rows=frows)
    after = finish(n2[1:2], res_b2[1:2], res_b2[4:5], red_au.token)
    red_b.append(("ffn1_w_up", _reduce_ici(chip_sums(["ffn1_w_up"], red_au, after, frows),
                                           "reduce_ffn1_w_up_ici", rows=frows)))
    after = finish(n2[2:3], res_b2[2:3], res_b2[5:6], red_b[-1][1].token)
    res = red_bm.wait(after)
    nm_ = len(group["mix"])
    after = finish(group["mix"], res[:nm_], res[nm_:], after)
    total = _sum_devices(small_d2d.wait(after)[0])
    zero = jnp.zeros((1,), F32)
    dlt, nm, nv = _adamw(_pack([wts[n] for n in rep] + [zero]), total, _pack([mom1[n] for n in rep] + [zero]),
                         _pack([mom2[n] for n in rep] + [zero]), "adamw_small")
    shapes = [wts[n].shape for n in rep] + [(1,)]
    for n, g, dl, m1, m2 in zip(rep, _unpack(total, shapes), _unpack(dlt, shapes), _unpack(nm, shapes),
                                _unpack(nv, shapes)):
        outs_g[n], outs_d[n], outs_m[n], outs_v[n] = g, dl, m1, m2
    loss = _unpack(total, shapes)[-1].reshape(())
    after = dlt
    for n, ex in red_b:
        res = ex.wait(after)
        after = finish([n], res[:1], res[1:], after)

    return (loss, dx0.reshape(b_loc, seq, d), *[outs_g[n] for n in WEIGHTS], *[outs_d[n] for n in WEIGHTS],
            *[outs_m[n] for n in WEIGHTS], *[outs_v[n] for n in WEIGHTS])
```

```python
import functools

import jax
import jax.numpy as jnp
from jax import lax
from jax.experimental import pallas as pl
from jax.experimental.pallas import tpu as pltpu

F32 = jnp.float32
BF16 = jnp.bfloat16
EPS = 1e-6
LANE = 128
SUBLANE = 8
N_DEV = 8
VMEM_LIMIT = 60 * 1024 * 1024
NOPE = 128
ROPE = 64
VHEAD = 128
QK = NOPE + ROPE
HEADW = 2 * LANE
CHUNK = 128
ROPE_THETA = 10000.0
ADAM_LR, ADAM_B1, ADAM_B2, ADAM_EPS, ADAM_WD, ADAM_STEP = 0.001, 0.9, 0.999, 1e-08, 0.01, 10
MESH = pl.DeviceIdType.MESH
ANY = pl.BlockSpec(memory_space=pl.ANY)
WHOLE_VMEM = pl.BlockSpec(memory_space=pltpu.VMEM)


def _params(sem=None):
    return pltpu.CompilerParams(dimension_semantics=sem, vmem_limit_bytes=VMEM_LIMIT)


def _round_up(n, m):
    return -(-n // m) * m


def _row_block(rows, target):
    best = rows
    for cand in range(SUBLANE, min(rows, target) + 1, SUBLANE):
        if rows % cand == 0:
            best = cand
    return best if best <= target else rows


def _nn(a, b):
    return jnp.dot(a, b, preferred_element_type=F32)


def _nt(a, b):
    return lax.dot_general(a, b, (((1,), (1,)), ((), ())), preferred_element_type=F32)


def _tn(a, b):
    return lax.dot_general(a, b, (((0,), (0,)), ((), ())), preferred_element_type=F32)


def _rstd(x, n):
    return lax.rsqrt(jnp.sum(x * x, axis=-1, keepdims=True) * (1.0 / n) + EPS)


def _rms_fwd(x, g, n):
    return x * _rstd(x, n) * g


def _rms_bwd(x, g, dy, n):
    r = _rstd(x, n)
    xh = x * r
    dyg = dy * g
    dx = r * (dyg - xh * (jnp.sum(dyg * xh, axis=-1, keepdims=True) * (1.0 / n)))
    return dx, jnp.sum(dy * xh, axis=0, keepdims=True)


def _gelu(x):
    return 0.5 * x * (1.0 + lax.erf(x * 0.7071067811865476))


def _gelu_grad(x):
    return 0.5 * (1.0 + lax.erf(x * 0.7071067811865476)) + x * jnp.exp(-0.5 * x * x) * 0.3989422804014327


def _ffn_fwd(base, xn, g, ids, wg_t, wu_t, wd, saved, tm, name):
    t, d = base.shape
    nb, fp, _ = wg_t.shape
    n = ids.shape[0]
    first = xn is None
    if saved is None:
        saved = [lax.empty((t, nb * fp), BF16) for _ in range(3)]

    def body(ids_ref, *refs):
        if first:
            base_ref, g_ref, wg_ref, wu_ref, wd_ref, _, _, _, out_ref, xn_ref, gd_ref, sl_ref, h_ref, acc = refs
        else:
            base_ref, xn_ref, wg_ref, wu_ref, wd_ref, _, _, _, out_ref, gd_ref, sl_ref, h_ref, acc = refs
        j = pl.program_id(1)

        @pl.when(j == 0)
        def _():
            if first:
                xn_ref[...] = _rms_fwd(base_ref[...], g_ref[...], d).astype(BF16)
            acc[...] = jnp.zeros_like(acc)

        xnb = xn_ref[...]
        a = _nt(xnb, wg_ref[0])
        b = _nt(xnb, wu_ref[0])
        s = jax.nn.sigmoid(a)
        sl = a * s
        h = (sl * b).astype(BF16)
        gd_ref[...] = (b * (s * (1.0 + a * (1.0 - s)))).astype(BF16)
        sl_ref[...] = sl.astype(BF16)
        h_ref[...] = h
        acc[...] += _nn(h, wd_ref[0])

        @pl.when(j == n - 1)
        def _():
            out_ref[...] = base_ref[...] + 0.5 * acc[...]

    wspec = pl.BlockSpec((1, fp, d), lambda i, j, ids_ref: (ids_ref[j], 0, 0))
    row = pl.BlockSpec((tm, d), lambda i, j, ids_ref: (i, 0))
    ff = pl.BlockSpec((tm, fp), lambda i, j, ids_ref: (i, ids_ref[j]))
    ffs = jax.ShapeDtypeStruct((t, nb * fp), BF16)
    second = pl.BlockSpec((1, d), lambda i, j, ids_ref: (0, 0)) if first else row
    n_row_outs = 2 if first else 1
    res = pl.pallas_call(
        body, name=name,
        grid_spec=pltpu.PrefetchScalarGridSpec(
            num_scalar_prefetch=1, grid=(t // tm, n),
            in_specs=[row, second, wspec, wspec, wspec, ANY, ANY, ANY],
            out_specs=[row] * n_row_outs + [ff, ff, ff],
            scratch_shapes=[pltpu.VMEM((tm, d), F32)]),
        out_shape=[jax.ShapeDtypeStruct((t, d), F32)] + ([jax.ShapeDtypeStruct((t, d), BF16)] if first else [])
        + [ffs, ffs, ffs],
        input_output_aliases={6 + k: n_row_outs + k for k in range(3)},
        compiler_params=_params(("arbitrary", "arbitrary")),
    )(ids, base, g if first else xn, wg_t, wu_t, wd, *saved)
    return (res[0], res[1] if first else xn, list(res[n_row_outs:]))


def _ffn_bwd(dout, x, g, gd, sl, wg_t, wu_t, wd, tm, name):
    t, d = x.shape
    nb, fp, _ = wg_t.shape

    def body(do_hbm, x_hbm, g_ref, gd_ref, sl_ref, wg_ref, wu_ref, wd_ref, wd_next_ref,
             dx_hbm, dg_ref, da_ref, db_ref, acc, rowbuf, dy_scr, dh_scr, sem):
        i, j = pl.program_id(0), pl.program_id(1)
        rows = pl.ds(pl.multiple_of(i * tm, tm), tm)
        get_do = pltpu.make_async_copy(do_hbm.at[rows, :], rowbuf, sem)
        get_x = pltpu.make_async_copy(x_hbm.at[rows, :], rowbuf, sem)

        @pl.when(j == 0)
        def _():
            get_do.start()
            get_do.wait()
            dy_scr[...] = (0.5 * rowbuf[...]).astype(BF16)
            acc[...] = jnp.zeros_like(acc)
            dh_scr[0] = _nt(dy_scr[...], wd_ref[0])
            get_x.start()

        @pl.when((i == 0) & (j == 0))
        def _():
            dg_ref[...] = jnp.zeros_like(dg_ref)

        dh = dh_scr[j % 2]
        dh_scr[(j + 1) % 2] = _nt(dy_scr[...], wd_next_ref[0])
        da = (dh * gd_ref[...].astype(F32)).astype(BF16)
        db = (dh * sl_ref[...].astype(F32)).astype(BF16)
        da_ref[...] = da
        db_ref[...] = db
        acc[...] += _nn(da, wg_ref[0]) + _nn(db, wu_ref[0])

        @pl.when(j == nb - 1)
        def _():
            get_x.wait()
            dxn, dg = _rms_bwd(rowbuf[...], g_ref[...], acc[...], d)
            dg_ref[...] += dg
            acc[...] = dxn
            get_do.start()
            get_do.wait()
            acc[...] += rowbuf[...]
            out = pltpu.make_async_copy(acc, dx_hbm.at[rows, :], sem)
            out.start()
            out.wait()

    wspec = pl.BlockSpec((1, fp, d), lambda i, j: (j, 0, 0))
    wnext = pl.BlockSpec((1, fp, d), lambda i, j: (jnp.minimum(j + 1, nb - 1), 0, 0))
    vec = pl.BlockSpec((1, d), lambda i, j: (0, 0))
    ff = pl.BlockSpec((tm, fp), lambda i, j: (i, j))
    ffs = jax.ShapeDtypeStruct((t, nb * fp), BF16)
    return pl.pallas_call(
        body, name=name, grid=(t // tm, nb),
        in_specs=[ANY, ANY, vec, ff, ff, wspec, wspec, pl.BlockSpec((1, fp, d), lambda i, j: (0, 0, 0)), wnext],
        out_specs=[ANY, vec, ff, ff],
        out_shape=[jax.ShapeDtypeStruct((t, d), F32), jax.ShapeDtypeStruct((1, d), F32), ffs, ffs],
        scratch_shapes=[pltpu.VMEM((tm, d), F32), pltpu.VMEM((tm, d), F32), pltpu.VMEM((tm, d), BF16),
                        pltpu.VMEM((2, tm, fp), F32), pltpu.SemaphoreType.DMA],
        compiler_params=_params(("arbitrary", "arbitrary")),
    )(dout, x, g, gd, sl, wg_t, wu_t, wd, wd)


def _matmul_tn(lhs, rhs, bm, bn, tk, out_dtype, name, rhs_scale=None, deps=(), out_rows=None):
    t, m = lhs.shape
    n = rhs.shape[1]
    nk = t // tk
    out_rows = m if out_rows is None else out_rows

    def body(l_ref, r_ref, *refs):
        o_ref, acc = refs[len(deps):]
        k = pl.program_id(2)

        @pl.when(k == 0)
        def _():
            acc[...] = jnp.zeros_like(acc)

        r = r_ref[...] if rhs_scale is None else rhs_scale * r_ref[...]
        acc[...] += _tn(l_ref[...].astype(BF16), r.astype(BF16))

        @pl.when(k == nk - 1)
        def _():
            o_ref[...] = acc[...].astype(out_dtype)

    return pl.pallas_call(
        body, name=name, grid=(m // bm, n // bn, nk),
        in_specs=[pl.BlockSpec((tk, bm), lambda i, j, k: (k, i)), pl.BlockSpec((tk, bn), lambda i, j, k: (k, j))]
        + [ANY] * len(deps),
        out_specs=pl.BlockSpec((bm, bn), lambda i, j, k: (i, j)),
        out_shape=jax.ShapeDtypeStruct((out_rows, n), out_dtype),
        scratch_shapes=[pltpu.VMEM((bm, bn), F32)],
        compiler_params=_params(("arbitrary", "arbitrary", "arbitrary")),
    )(lhs, rhs, *deps)


def _last_rows_padded(w_ref, tail_ref, off, real):
    @pl.when(pl.program_id(0) == 0)
    def _():
        tail_ref[...] = jnp.zeros_like(tail_ref)
        tail_ref[0:real, :] = w_ref[off:off + real, :]


def _inproj_fwd(x, g, w_t, splits, tm):
    t, d = x.shape
    offs = [sum(splits[:k]) for k in range(len(splits))]
    real_last = w_t.shape[0] - offs[-1]

    def body(x_ref, g_ref, w_ref, hn_ref, *refs):
        z_refs, tail_ref = refs[:-1], refs[-1]
        _last_rows_padded(w_ref, tail_ref, offs[-1], real_last)
        hn = _rms_fwd(x_ref[...], g_ref[...], d).astype(BF16)
        hn_ref[...] = hn
        for z_ref, o, n in zip(z_refs[:-1], offs, splits):
            z_ref[...] = _nt(hn, w_ref[o:o + n, :])
        z_refs[-1][...] = _nt(hn, tail_ref[...])

    row = pl.BlockSpec((tm, d), lambda i: (i, 0))
    return pl.pallas_call(
        body, name="inproj_fwd", grid=(t // tm,),
        in_specs=[row, pl.BlockSpec((1, d), lambda i: (0, 0)), WHOLE_VMEM],
        out_specs=[row] + [pl.BlockSpec((tm, n), lambda i: (i, 0)) for n in splits],
        out_shape=[jax.ShapeDtypeStruct((t, d), BF16)] + [jax.ShapeDtypeStruct((t, n), F32) for n in splits],
        scratch_shapes=[pltpu.VMEM((splits[-1], d), BF16)],
        compiler_params=_params(("arbitrary",)),
    )(x, g, w_t)


def _inproj_bwd(dzs, x, g, w_t, dres, splits, tm):
    t, d = x.shape
    offs = [sum(splits[:k]) for k in range(len(splits))]
    ni = sum(splits)
    nz = len(splits)
    real_last = w_t.shape[0] - offs[-1]

    def body(*refs):
        dz_refs = refs[:nz]
        x_ref, g_ref, w_ref, dres_ref, dx_ref, dg_ref, dzc_ref, tail_ref = refs[nz:]
        _last_rows_padded(w_ref, tail_ref, offs[-1], real_last)
        dhn = jnp.zeros((tm, d), F32)
        for k, (dz_ref, o, n) in enumerate(zip(dz_refs, offs, splits)):
            dz = dz_ref[...].astype(BF16)
            dzc_ref[:, o:o + n] = dz
            dhn += _nn(dz, tail_ref[...] if k == nz - 1 else w_ref[o:o + n, :])
        dx, dg = _rms_bwd(x_ref[...], g_ref[...], dhn, d)
        dx_ref[...] = dres_ref[...] + dx

        @pl.when(pl.program_id(0) == 0)
        def _():
            dg_ref[...] = jnp.zeros_like(dg_ref)

        dg_ref[...] += dg

    row = pl.BlockSpec((tm, d), lambda i: (i, 0))
    vec = pl.BlockSpec((1, d), lambda i: (0, 0))
    return pl.pallas_call(
        body, name="inproj_bwd", grid=(t // tm,),
        in_specs=[pl.BlockSpec((tm, n), lambda i: (i, 0)) for n in splits] + [row, vec, WHOLE_VMEM, row],
        out_specs=[row, vec, pl.BlockSpec((tm, ni), lambda i: (i, 0))],
        out_shape=[jax.ShapeDtypeStruct((t, d), F32), jax.ShapeDtypeStruct((1, d), F32),
                   jax.ShapeDtypeStruct((t, ni), BF16)],
        scratch_shapes=[pltpu.VMEM((splits[-1], d), BF16)],
        compiler_params=_params(("arbitrary",)),
    )(*dzs, x, g, w_t, dres)


def _tril_bf16(ws_ref, grp):
    rows = lax.broadcasted_iota(jnp.int32, (CHUNK, CHUNK), 0)
    cols = lax.broadcasted_iota(jnp.int32, (CHUNK, CHUNK), 1)
    return jnp.where(rows >= cols, ws_ref[grp], 0.0).astype(BF16)


def _gmlp_mix(zuv_ref, gv_ref, ws_ref, bias_ref, v_scr, mixed_scr, tm, w, groups):
    u = _gelu(zuv_ref[:, 0:w])
    v0 = _gelu(zuv_ref[:, w:2 * w])
    v_scr[...] = _rms_fwd(v0, gv_ref[...], w).astype(BF16)
    for grp in range(groups):
        wsm = _tril_bf16(ws_ref, grp)
        lanes = slice(grp * CHUNK, (grp + 1) * CHUNK)
        for c in range(tm // CHUNK):
            rows = slice(c * CHUNK, (c + 1) * CHUNK)
            mixed_scr[rows, lanes] = _nn(wsm, v_scr[rows, lanes]) + bias_ref[:, lanes]
    return u, v0


def _gmlp_fwd(zuv, gv, ws, bias, gout, tm):
    t, w2 = zuv.shape
    w = w2 // 2
    groups = ws.shape[0]

    def body(zuv_ref, gv_ref, ws_ref, bias_ref, go_ref, y_ref, v_scr, mixed_scr):
        u, _ = _gmlp_mix(zuv_ref, gv_ref, ws_ref, bias_ref, v_scr, mixed_scr, tm, w, groups)
        ya = u * mixed_scr[...]
        for grp in range(groups):
            lanes = slice(grp * CHUNK, (grp + 1) * CHUNK)
            y_ref[:, lanes] = _rms_fwd(ya[:, lanes], go_ref[:, lanes], CHUNK).astype(BF16)

    const2 = lambda i: (0, 0)
    return pl.pallas_call(
        body, name="gmlp_fwd", grid=(t // tm,),
        in_specs=[pl.BlockSpec((tm, w2), lambda i: (i, 0)), pl.BlockSpec((1, w), const2),
                  pl.BlockSpec((groups, CHUNK, CHUNK), lambda i: (0, 0, 0)),
                  pl.BlockSpec((CHUNK, w), const2), pl.BlockSpec((1, w), const2)],
        out_specs=pl.BlockSpec((tm, w), lambda i: (i, 0)),
        out_shape=jax.ShapeDtypeStruct((t, w), BF16),
        scratch_shapes=[pltpu.VMEM((tm, w), BF16), pltpu.VMEM((tm, w), F32)],
        compiler_params=_params(("arbitrary",)),
    )(zuv, gv, ws, bias, gout)


def _gmlp_bwd(dy, zuv, gv, ws, bias, gout, tm):
    t, w2 = zuv.shape
    w = w2 // 2
    groups = ws.shape[0]

    def body(dy_ref, zuv_ref, gv_ref, ws_ref, bias_ref, go_ref,
             dz_ref, dws_ref, dbias_ref, dgv_ref, dgo_ref, v_scr, mixed_scr, dmix_scr, dv_scr):
        @pl.when(pl.program_id(0) == 0)
        def _():
            dws_ref[...] = jnp.zeros_like(dws_ref)
            dbias_ref[...] = jnp.zeros_like(dbias_ref)
            dgv_ref[...] = jnp.zeros_like(dgv_ref)
            dgo_ref[...] = jnp.zeros_like(dgo_ref)

        u, v0 = _gmlp_mix(zuv_ref, gv_ref, ws_ref, bias_ref, v_scr, mixed_scr, tm, w, groups)
        mixed = mixed_scr[...]
        ya = u * mixed
        for grp in range(groups):
            lanes = slice(grp * CHUNK, (grp + 1) * CHUNK)
            dya, dgo = _rms_bwd(ya[:, lanes], go_ref[:, lanes], dy_ref[:, lanes], CHUNK)
            dgo_ref[:, lanes] += dgo
            dz_ref[:, lanes] = dya * mixed[:, lanes] * _gelu_grad(zuv_ref[:, lanes])
            dmix_scr[:, lanes] = dya * u[:, lanes]
        for grp in range(groups):
            wsm = _tril_bf16(ws_ref, grp)
            lanes = slice(grp * CHUNK, (grp + 1) * CHUNK)
            dws = jnp.zeros((CHUNK, CHUNK), F32)
            dbias = jnp.zeros((CHUNK, CHUNK), F32)
            for c in range(tm // CHUNK):
                rows = slice(c * CHUNK, (c + 1) * CHUNK)
                dm = dmix_scr[rows, lanes]
                dmb = dm.astype(BF16)
                dv_scr[rows, lanes] = _tn(wsm, dmb)
                dws += _nt(dmb, v_scr[rows, lanes])
                dbias += dm
            rr = lax.broadcasted_iota(jnp.int32, (CHUNK, CHUNK), 0)
            cc = lax.broadcasted_iota(jnp.int32, (CHUNK, CHUNK), 1)
            dws_ref[grp] += jnp.where(rr >= cc, dws, 0.0)
            dbias_ref[grp] += jnp.sum(dbias, axis=1, keepdims=True)
        dv0, dgv = _rms_bwd(v0, gv_ref[...], dv_scr[...], w)
        dgv_ref[...] += dgv
        dz_ref[:, w:2 * w] = dv0 * _gelu_grad(zuv_ref[:, w:2 * w])

    const2 = lambda i: (0, 0)
    const3 = lambda i: (0, 0, 0)
    return pl.pallas_call(
        body, name="gmlp_bwd", grid=(t // tm,),
        in_specs=[pl.BlockSpec((tm, w), lambda i: (i, 0)), pl.BlockSpec((tm, w2), lambda i: (i, 0)),
                  pl.BlockSpec((1, w), const2), pl.BlockSpec((groups, CHUNK, CHUNK), const3),
                  pl.BlockSpec((CHUNK, w), const2), pl.BlockSpec((1, w), const2)],
        out_specs=[pl.BlockSpec((tm, w2), lambda i: (i, 0)), pl.BlockSpec((groups, CHUNK, CHUNK), const3),
                   pl.BlockSpec((groups, CHUNK, 1), const3), pl.BlockSpec((1, w), const2), pl.BlockSpec((1, w), const2)],
        out_shape=[jax.ShapeDtypeStruct((t, w2), F32), jax.ShapeDtypeStruct((groups, CHUNK, CHUNK), F32),
                   jax.ShapeDtypeStruct((groups, CHUNK, 1), F32), jax.ShapeDtypeStruct((1, w), F32),
                   jax.ShapeDtypeStruct((1, w), F32)],
        scratch_shapes=[pltpu.VMEM((tm, w), BF16), pltpu.VMEM((tm, w), F32),
                        pltpu.VMEM((tm, w), F32), pltpu.VMEM((tm, w), F32)],
        compiler_params=_params(("arbitrary",)),
    )(dy, zuv, gv, ws, bias, gout)


def _rot(x, m_lo, m_hi):
    return pltpu.roll(x, LANE - ROPE // 2, 1) * m_lo + pltpu.roll(x, ROPE // 2, 1) * m_hi


def _rope_tables(pos_ref, freq_ref):
    ang = pos_ref[...] * freq_ref[...]
    return jnp.cos(ang), jnp.sin(ang)


def _mla_proj_fwd(cq, ckv, krw, pos, freq, masks, gq, gkv, wq_t, wkv_t, gqh, gkh, tm):
    t, rq = cq.shape
    rkv = ckv.shape[1]
    heads = wq_t.shape[0]

    def body(cq_ref, ckv_ref, kr_ref, pos_ref, freq_ref, mk_ref, gq_ref, gkv_ref, wq_ref, wkv_ref,
             gqh_ref, gkh_ref, q_ref, k_ref, v_ref):
        cos, sin = _rope_tables(pos_ref, freq_ref)
        m_lo, m_hi = mk_ref[0:1, :], mk_ref[1:2, :]
        cqn = _rms_fwd(cq_ref[...], gq_ref[...], rq).astype(BF16)
        ckvn = _rms_fwd(ckv_ref[...], gkv_ref[...], rkv).astype(BF16)
        kr = kr_ref[...]
        kr_ss = jnp.sum(kr * kr, axis=-1, keepdims=True)
        for h in range(heads):
            qh = _nt(cqn, wq_ref[h])
            qn = qh * _rstd(qh, QK) * gqh_ref[...]
            qr = qn[:, LANE:]
            q_ref[h, :, 0:LANE] = qn[:, 0:LANE].astype(BF16)
            q_ref[h, :, LANE:] = (qr * cos + _rot(qr, m_lo, m_hi) * sin).astype(BF16)
            kvh = _nt(ckvn, wkv_ref[h])
            kn = kvh[:, 0:LANE]
            rk = lax.rsqrt((jnp.sum(kn * kn, axis=-1, keepdims=True) + kr_ss) * (1.0 / QK) + EPS)
            k_ref[h, :, 0:LANE] = (kn * rk * gkh_ref[:, 0:LANE]).astype(BF16)
            krn = kr * rk * gkh_ref[:, LANE:]
            k_ref[h, :, LANE:] = (krn * cos + _rot(krn, m_lo, m_hi) * sin).astype(BF16)
            v_ref[h] = kvh[:, LANE:].astype(BF16)

    c2 = lambda i: (0, 0)
    c3 = lambda i: (0, 0, 0)
    return pl.pallas_call(
        body, name="mla_proj_fwd", grid=(t // tm,),
        in_specs=[pl.BlockSpec((tm, rq), lambda i: (i, 0)), pl.BlockSpec((tm, rkv), lambda i: (i, 0)),
                  pl.BlockSpec((tm, LANE), lambda i: (i, 0)), pl.BlockSpec((tm, 1), lambda i: (i, 0)),
                  pl.BlockSpec((1, LANE), c2), pl.BlockSpec((2, LANE), c2),
                  pl.BlockSpec((1, rq), c2), pl.BlockSpec((1, rkv), c2),
                  pl.BlockSpec((heads, HEADW, rq), c3), pl.BlockSpec((heads, HEADW, rkv), c3),
                  pl.BlockSpec((1, HEADW), c2), pl.BlockSpec((1, HEADW), c2)],
        out_specs=[pl.BlockSpec((heads, tm, HEADW), lambda i: (0, i, 0)),
                   pl.BlockSpec((heads, tm, HEADW), lambda i: (0, i, 0)),
                   pl.BlockSpec((heads, tm, VHEAD), lambda i: (0, i, 0))],
        out_shape=[jax.ShapeDtypeStruct((heads, t, HEADW), BF16), jax.ShapeDtypeStruct((heads, t, HEADW), BF16),
                   jax.ShapeDtypeStruct((heads, t, VHEAD), BF16)],
        compiler_params=_params(("arbitrary",)),
    )(cq, ckv, krw, pos, freq, masks, gq, gkv, wq_t, wkv_t, gqh, gkh)


def _mla_proj_bwd(dq, dk, dv, cq, ckv, krw, pos, freq, masks, gq, gkv, wq_t, wkv_t, gqh, gkh, tm):
    t, rq = cq.shape
    rkv = ckv.shape[1]
    heads = wq_t.shape[0]

    def body(dq_ref, dk_ref, dv_ref, cq_ref, ckv_ref, kr_ref, pos_ref, freq_ref, mk_ref, gq_ref, gkv_ref,
             wq_ref, wkv_ref, gqh_ref, gkh_ref,
             dcq_ref, dckv_ref, dkr_ref, dwq_ref, dwkv_ref, dgq_ref, dgkv_ref, dgqh_ref, dgkh_ref):
        @pl.when(pl.program_id(0) == 0)
        def _():
            for r in (dwq_ref, dwkv_ref, dgq_ref, dgkv_ref, dgqh_ref, dgkh_ref):
                r[...] = jnp.zeros_like(r)

        cos, sin = _rope_tables(pos_ref, freq_ref)
        m_lo, m_hi = mk_ref[0:1, :], mk_ref[1:2, :]

        def unrope(dy):
            return dy * cos - _rot(dy * sin, m_lo, m_hi)

        cqn = _rms_fwd(cq_ref[...], gq_ref[...], rq).astype(BF16)
        ckvn = _rms_fwd(ckv_ref[...], gkv_ref[...], rkv).astype(BF16)
        kr = kr_ref[...]
        kr_ss = jnp.sum(kr * kr, axis=-1, keepdims=True)
        dcqn = jnp.zeros((tm, rq), F32)
        dckvn = jnp.zeros((tm, rkv), F32)
        dkr = jnp.zeros((tm, LANE), F32)
        for h in range(heads):
            qh = _nt(cqn, wq_ref[h])
            dqn = jnp.concatenate([dq_ref[h, :, 0:LANE], unrope(dq_ref[h, :, LANE:])], axis=1)
            dqh, dg = _rms_bwd(qh, gqh_ref[...], dqn, QK)
            dgqh_ref[...] += dg
            dqh = dqh.astype(BF16)
            dcqn += _nn(dqh, wq_ref[h])
            dwq_ref[h] += _tn(dqh, cqn)

            kvh = _nt(ckvn, wkv_ref[h])
            kn = kvh[:, 0:LANE]
            rk = lax.rsqrt((jnp.sum(kn * kn, axis=-1, keepdims=True) + kr_ss) * (1.0 / QK) + EPS)
            dkn_n = dk_ref[h, :, 0:LANE]
            dkr_n = unrope(dk_ref[h, :, LANE:])
            knh, krh = kn * rk, kr * rk
            dgkh_ref[:, 0:LANE] += jnp.sum(dkn_n * knh, axis=0, keepdims=True)
            dgkh_ref[:, LANE:] += jnp.sum(dkr_n * krh, axis=0, keepdims=True)
            dkn_g, dkr_g = dkn_n * gkh_ref[:, 0:LANE], dkr_n * gkh_ref[:, LANE:]
            proj = (jnp.sum(dkn_g * knh, axis=-1, keepdims=True)
                    + jnp.sum(dkr_g * krh, axis=-1, keepdims=True)) * (1.0 / QK)
            dkr += rk * (dkr_g - krh * proj)
            dkvh = jnp.concatenate([rk * (dkn_g - knh * proj), dv_ref[h]], axis=1).astype(BF16)
            dckvn += _nn(dkvh, wkv_ref[h])
            dwkv_ref[h] += _tn(dkvh, ckvn)
        dkr_ref[...] = dkr
        dcq, dg = _rms_bwd(cq_ref[...], gq_ref[...], dcqn, rq)
        dcq_ref[...] = dcq
        dgq_ref[...] += dg
        dckv, dg = _rms_bwd(ckv_ref[...], gkv_ref[...], dckvn, rkv)
        dckv_ref[...] = dckv
        dgkv_ref[...] += dg

    c2 = lambda i: (0, 0)
    c3 = lambda i: (0, 0, 0)
    hq = pl.BlockSpec((heads, tm, HEADW), lambda i: (0, i, 0))
    return pl.pallas_call(
        body, name="mla_proj_bwd", grid=(t // tm,),
        in_specs=[hq, hq, pl.BlockSpec((heads, tm, VHEAD), lambda i: (0, i, 0)),
                  pl.BlockSpec((tm, rq), lambda i: (i, 0)), pl.BlockSpec((tm, rkv), lambda i: (i, 0)),
                  pl.BlockSpec((tm, LANE), lambda i: (i, 0)), pl.BlockSpec((tm, 1), lambda i: (i, 0)),
                  pl.BlockSpec((1, LANE), c2), pl.BlockSpec((2, LANE), c2),
                  pl.BlockSpec((1, rq), c2), pl.BlockSpec((1, rkv), c2),
                  pl.BlockSpec((heads, HEADW, rq), c3), pl.BlockSpec((heads, HEADW, rkv), c3),
                  pl.BlockSpec((1, HEADW), c2), pl.BlockSpec((1, HEADW), c2)],
        out_specs=[pl.BlockSpec((tm, rq), lambda i: (i, 0)), pl.BlockSpec((tm, rkv), lambda i: (i, 0)),
                   pl.BlockSpec((tm, LANE), lambda i: (i, 0)),
                   pl.BlockSpec((heads, HEADW, rq), c3), pl.BlockSpec((heads, HEADW, rkv), c3),
                   pl.BlockSpec((1, rq), c2), pl.BlockSpec((1, rkv), c2),
                   pl.BlockSpec((1, HEADW), c2), pl.BlockSpec((1, HEADW), c2)],
        out_shape=[jax.ShapeDtypeStruct((t, rq), F32), jax.ShapeDtypeStruct((t, rkv), F32),
                   jax.ShapeDtypeStruct((t, LANE), F32),
                   jax.ShapeDtypeStruct((heads, HEADW, rq), F32), jax.ShapeDtypeStruct((heads, HEADW, rkv), F32),
                   jax.ShapeDtypeStruct((1, rq), F32), jax.ShapeDtypeStruct((1, rkv), F32),
                   jax.ShapeDtypeStruct((1, HEADW), F32), jax.ShapeDtypeStruct((1, HEADW), F32)],
        compiler_params=_params(("arbitrary",)),
    )(dq, dk, dv, cq, ckv, krw, pos, freq, masks, gq, gkv, wq_t, wkv_t, gqh, gkh)


def _lower_triangle(blk):
    return lax.broadcasted_iota(jnp.int32, (blk, blk), 0) >= lax.broadcasted_iota(jnp.int32, (blk, blk), 1)


def _attn_fwd(q, k, v, seq, blk):
    heads, t, _ = q.shape
    scale = QK ** -0.5
    nblk = seq // blk

    def body(q_ref, k_ref, v_ref, o_ref, lse_ref):
        tri = _lower_triangle(blk)
        for qi in range(nblk):
            rows = slice(qi * blk, (qi + 1) * blk)
            before = slice(0, qi * blk)
            qb = q_ref[0, rows, :]
            s_d = jnp.where(tri, _nt(qb, k_ref[0, rows, :]) * scale, -1e30)
            m = jnp.max(s_d, axis=-1, keepdims=True)
            if qi:
                s_b = _nt(qb, k_ref[0, before, :]) * scale
                m = jnp.maximum(m, jnp.max(s_b, axis=-1, keepdims=True))
                p_b = jnp.exp(s_b - m)
            p_d = jnp.exp(s_d - m)
            l = jnp.sum(p_d, axis=-1, keepdims=True)
            acc = _nn(p_d.astype(BF16), v_ref[0, rows, :])
            if qi:
                l += jnp.sum(p_b, axis=-1, keepdims=True)
                acc += _nn(p_b.astype(BF16), v_ref[0, before, :])
            o_ref[0, rows, :] = acc / l
            lse_ref[0, rows, :] = m + jnp.log(l)

    return pl.pallas_call(
        body, name="attn_fwd", grid=(heads, t // seq),
        in_specs=[pl.BlockSpec((1, seq, HEADW), lambda h, b: (h, b, 0)),
                  pl.BlockSpec((1, seq, HEADW), lambda h, b: (h, b, 0)),
                  pl.BlockSpec((1, seq, VHEAD), lambda h, b: (h, b, 0))],
        out_specs=[pl.BlockSpec((1, seq, VHEAD), lambda h, b: (h, b, 0)),
                   pl.BlockSpec((1, seq, 1), lambda h, b: (h, b, 0))],
        out_shape=[jax.ShapeDtypeStruct((heads, t, VHEAD), F32), jax.ShapeDtypeStruct((heads, t, 1), F32)],
        compiler_params=_params(("arbitrary", "arbitrary")),
    )(q, k, v)


def _attn_bwd(q, k, v, do, lse, delta, seq, blk, after):
    heads, t, _ = q.shape
    scale = QK ** -0.5
    nblk = seq // blk

    def body(q_ref, k_ref, v_ref, do_ref, lse_ref, dl_ref, _, dq_ref, dk_ref, dv_ref):
        tri = _lower_triangle(blk)
        dk_ref[...] = jnp.zeros_like(dk_ref)
        dv_ref[...] = jnp.zeros_like(dv_ref)
        for qi in range(nblk):
            rows = slice(qi * blk, (qi + 1) * blk)
            qb = q_ref[0, rows, :]
            dob = do_ref[0, rows, :]
            lse_b = lse_ref[0, rows, :]
            dl_b = dl_ref[0, rows, :]
            dq = jnp.zeros((blk, HEADW), F32)
            for keys, masked in ((slice(0, qi * blk), False), (rows, True)):
                if keys.stop == keys.start:
                    continue
                kb = k_ref[0, keys, :]
                p = jnp.exp(_nt(qb, kb) * scale - lse_b)
                if masked:
                    p = jnp.where(tri, p, 0.0)
                dp = _nt(dob, v_ref[0, keys, :])
                ds = (p * (dp - dl_b) * scale).astype(BF16)
                dv_ref[0, keys, :] += _tn(p.astype(BF16), dob)
                dk_ref[0, keys, :] += _tn(ds, qb)
                dq += _nn(ds, kb)
            dq_ref[0, rows, :] = dq

    hq = pl.BlockSpec((1, seq, HEADW), lambda h, b: (h, b, 0))
    hv = pl.BlockSpec((1, seq, VHEAD), lambda h, b: (h, b, 0))
    h1 = pl.BlockSpec((1, seq, 1), lambda h, b: (h, b, 0))
    return pl.pallas_call(
        body, name="attn_bwd", grid=(heads, t // seq),
        in_specs=[hq, hq, hv, hv, h1, h1, ANY],
        out_specs=[hq, hq, hv],
        out_shape=[jax.ShapeDtypeStruct((heads, t, HEADW), F32), jax.ShapeDtypeStruct((heads, t, HEADW), F32),
                   jax.ShapeDtypeStruct((heads, t, VHEAD), F32)],
        compiler_params=_params(("arbitrary", "arbitrary")),
    )(q, k, v, do, lse, delta, after)


def _out_fwd(ya, o, gb, w_out, x1, tm):
    t, w = ya.shape
    heads = o.shape[0]
    d = x1.shape[1]

    def body(ya_ref, o_ref, gb_ref, w_ref, x1_ref, x2_ref, yc_ref):
        yc_ref[:, 0:w] = ya_ref[...]
        for h in range(heads):
            lanes = slice(h * VHEAD, (h + 1) * VHEAD)
            yc_ref[:, w + h * VHEAD:w + (h + 1) * VHEAD] = _rms_fwd(o_ref[h], gb_ref[:, lanes], VHEAD).astype(BF16)
        x2_ref[...] = x1_ref[...] + _nn(yc_ref[...], w_ref[...])

    wy = w + heads * VHEAD
    row = pl.BlockSpec((tm, d), lambda i: (i, 0))
    return pl.pallas_call(
        body, name="out_fwd", grid=(t // tm,),
        in_specs=[pl.BlockSpec((tm, w), lambda i: (i, 0)), pl.BlockSpec((heads, tm, VHEAD), lambda i: (0, i, 0)),
                  pl.BlockSpec((1, heads * VHEAD), lambda i: (0, 0)), WHOLE_VMEM, row],
        out_specs=[row, pl.BlockSpec((tm, wy), lambda i: (i, 0))],
        out_shape=[jax.ShapeDtypeStruct((t, d), F32), jax.ShapeDtypeStruct((t, wy), BF16)],
        compiler_params=_params(("arbitrary",)),
    )(ya, o, gb, w_out, x1)


def _out_bwd(dx2, o, gb, w_out, w, tm):
    t, d = dx2.shape
    heads = o.shape[0]

    def body(dx_ref, o_ref, gb_ref, w_ref, dya_ref, do_ref, dl_ref, dgb_ref):
        @pl.when(pl.program_id(0) == 0)
        def _():
            dgb_ref[...] = jnp.zeros_like(dgb_ref)

        dyc = _nt(dx_ref[...].astype(BF16), w_ref[...])
        dya_ref[...] = dyc[:, 0:w]
        for h in range(heads):
            lanes = slice(h * VHEAD, (h + 1) * VHEAD)
            oh = o_ref[h]
            doh, dg = _rms_bwd(oh, gb_ref[:, lanes], dyc[:, w + h * VHEAD:w + (h + 1) * VHEAD], VHEAD)
            dgb_ref[:, lanes] += dg
            do_ref[h] = doh.astype(BF16)
            dl_ref[h] = jnp.sum(doh * oh, axis=-1, keepdims=True)

    ho = pl.BlockSpec((heads, tm, VHEAD), lambda i: (0, i, 0))
    vec = pl.BlockSpec((1, heads * VHEAD), lambda i: (0, 0))
    return pl.pallas_call(
        body, name="out_bwd", grid=(t // tm,),
        in_specs=[pl.BlockSpec((tm, d), lambda i: (i, 0)), ho, vec, WHOLE_VMEM],
        out_specs=[pl.BlockSpec((tm, w), lambda i: (i, 0)), ho, pl.BlockSpec((heads, tm, 1), lambda i: (0, i, 0)), vec],
        out_shape=[jax.ShapeDtypeStruct((t, w), F32), jax.ShapeDtypeStruct((heads, t, VHEAD), BF16),
                   jax.ShapeDtypeStruct((heads, t, 1), F32), jax.ShapeDtypeStruct((1, heads * VHEAD), F32)],
        compiler_params=_params(("arbitrary",)),
    )(dx2, o, gb, w_out)


def _loss_head(y, target, tm):
    t, d = y.shape

    def body(y_ref, t_ref, dy_ref, loss_ref):
        @pl.when(pl.program_id(0) == 0)
        def _():
            loss_ref[...] = jnp.zeros_like(loss_ref)

        err = y_ref[...] - t_ref[...]
        dy_ref[...] = err * (1.0 / d)
        part = jnp.sum(jnp.sum(err * err, axis=-1, keepdims=True) * (1.0 / d), axis=0, keepdims=True)
        loss_ref[...] += 0.5 * part

    row = pl.BlockSpec((tm, d), lambda i: (i, 0))
    return pl.pallas_call(
        body, name="loss_head", grid=(t // tm,),
        in_specs=[row, row], out_specs=[row, pl.BlockSpec((1, 1), lambda i: (0, 0))],
        out_shape=[jax.ShapeDtypeStruct((t, d), F32), jax.ShapeDtypeStruct((1, 1), F32)],
        compiler_params=_params(("arbitrary",)),
    )(y, target)


def _place():
    return lax.axis_index("x"), lax.axis_index("y"), lax.axis_index("c")


HBM = pl.BlockSpec(memory_space=pltpu.HBM)
SEM = pl.BlockSpec(memory_space=pltpu.SEMAPHORE)
DATAFLOW = pltpu.SideEffectType.DATAFLOW_SIDE_EFFECTING


def _plan_copies(plan, refs, send_sems, recv_sems):
    def block(ref, blk):
        if blk is None:
            return ref
        return ref.at[blk[0], pl.ds(0, blk[1])] if isinstance(blk, tuple) else ref.at[blk]

    cps = []
    for i, (sb, sblk, db, dblk, dev) in enumerate(plan(*_place())):
        cps.append(pltpu.make_async_remote_copy(
            src_ref=block(refs[sb], sblk), dst_ref=block(refs[db], dblk),
            send_sem=send_sems.at[i], recv_sem=recv_sems.at[i], device_id=dev, device_id_type=MESH))
    return cps


def _push_start(bufs, plan, ncopy, name, deps=()):
    nb = len(bufs)

    def body(*refs):
        outs = refs[nb + len(deps):]
        for cp in _plan_copies(plan, refs[:nb], outs[0], outs[1]):
            cp.start()
        outs[-1][...] = jnp.zeros_like(outs[-1])

    res = pl.pallas_call(
        body, name=name,
        out_shape=(pltpu.SemaphoreType.DMA((ncopy,)), pltpu.SemaphoreType.DMA((ncopy,)),
                   *[pltpu.HBM(b.shape, b.dtype) for b in bufs], jax.ShapeDtypeStruct((SUBLANE, LANE), F32)),
        in_specs=[HBM] * nb + [ANY] * len(deps),
        out_specs=(SEM, SEM, *[HBM] * nb, WHOLE_VMEM),
        input_output_aliases={i: 2 + i for i in range(nb)},
        compiler_params=pltpu.CompilerParams(has_side_effects=DATAFLOW),
    )(*[pltpu.with_memory_space_constraint(b, pltpu.HBM) for b in bufs], *deps)
    return res[0], res[1], list(res[2:2 + nb]), res[-1]


def _push_wait(send_sems, recv_sems, bufs, plan, after, name):
    nb = len(bufs)

    def body(*refs):
        for cp in _plan_copies(plan, refs[:nb], refs[nb], refs[nb + 1]):
            cp.wait_send()
            cp.wait_recv()

    res = pl.pallas_call(
        body, name=name,
        out_shape=[pltpu.HBM(b.shape, b.dtype) for b in bufs],
        in_specs=[HBM] * nb + [SEM, SEM, ANY], out_specs=[HBM] * nb,
        input_output_aliases={i: i for i in range(nb)},
        compiler_params=pltpu.CompilerParams(has_side_effects=DATAFLOW),
    )(*bufs, send_sems, recv_sems, after)
    return list(res)


def _other_chips(x, y):
    return ((1 - x, y), (x, 1 - y), (1 - x, 1 - y))


class _Exchange:
    def __init__(self, bufs, plan, ncopy, name, deps=()):
        self.plan, self.name = plan, name
        self.send, self.recv, self.bufs, self.token = _push_start(bufs, plan, ncopy, name + "_start", deps)

    def wait(self, after):
        return _push_wait(self.send, self.recv, self.bufs, self.plan, after, self.name + "_wait")


class _Chain:
    def __init__(self, bufs):
        self.bufs = list(bufs)

    def start(self, plan, ncopy, name, deps=()):
        send, recv, self.bufs, token = _push_start(self.bufs, plan, ncopy, name + "_start", deps)
        return (send, recv, plan, name), token

    def wait(self, pending, after):
        send, recv, plan, name = pending
        self.bufs = _push_wait(send, recv, self.bufs, plan, after, name + "_wait")


class _StagedGather:
    def __init__(self, shards, me, name, pad_to=None):
        self.n = n = len(shards)
        self.name = name
        rows = shards[0].shape[0]
        lands = []
        for s in shards:
            land = lax.empty((N_DEV, pad_to or rows) + s.shape[1:], s.dtype)
            if pad_to and pad_to != rows:
                land = lax.dynamic_update_slice(
                    land, jnp.zeros((N_DEV, pad_to - rows) + s.shape[1:], s.dtype), (0, rows, 0))
            lands.append(lax.dynamic_update_slice(land, s[None], (me, 0, 0)))
        self.chain = _Chain(list(shards) + lands)
        self.pending = {}

        def blk(b):
            return (b, rows) if pad_to and pad_to != rows else b

        def to_sibling(blocks):
            return lambda x, y, c: [(n + a, blk(b), n + a, blk(b), (x, y, 1 - c))
                                    for a in range(n) for b in blocks(x, y, c)]

        def nbr_blocks(x, y, c):
            return [4 * (1 - x) + 2 * y + c, 4 * x + 2 * (1 - y) + c]

        def diag(x, y, c):
            sx, sy = (1 - x) * (1 - c) + x * c, y * (1 - c) + (1 - y) * c
            tx, ty = x * (1 - c) + (1 - x) * c, (1 - y) * (1 - c) + y * c
            b = blk(4 * sx + 2 * sy + c)
            return [(n + a, b, n + a, b, (tx, ty, c)) for a in range(n)]

        self.plans = {
            "own": (lambda x, y, c: [(a, None, n + a, blk(4 * x + 2 * y + c), (x, y, 1 - c)) for a in range(n)], n),
            "nbr": (lambda x, y, c: [(a, None, n + a, blk(4 * x + 2 * y + c), dev) for a in range(n)
                                     for dev in ((1 - x, y, c), (x, 1 - y, c))], 2 * n),
            "diag": (diag, n),
            "nbr_d2d": (to_sibling(nbr_blocks), 2 * n),
            "own_nbr_d2d": (to_sibling(lambda x, y, c: [4 * x + 2 * y + c] + nbr_blocks(x, y, c)), 3 * n),
            "diag_d2d": (to_sibling(lambda x, y, c: [4 * (1 - x) + 2 * (1 - y) + c]), n),
        }

    def start(self, stage, deps=()):
        plan, ncopy = self.plans[stage]
        self.pending[stage], token = self.chain.start(plan, ncopy, self.name + "_" + stage, deps)
        return token

    def wait(self, stage, after):
        self.chain.wait(self.pending.pop(stage), after)

    def lands(self):
        return self.chain.bufs[self.n:]


def _gather_ici(shards, me, name, deps=()):
    n = len(shards)
    lands = [lax.dynamic_update_slice(lax.empty((N_DEV,) + s.shape, s.dtype), s[None], (me, 0, 0)) for s in shards]

    def plan(x, y, c):
        return [(a, None, n + a, 4 * x + 2 * y + c, (px, py, c)) for a in range(n) for px, py in _other_chips(x, y)]

    return _Exchange(list(shards) + lands, plan, 3 * n, name, deps)


def _gather_d2d(lands, name, deps=()):
    n = len(lands)

    def plan(x, y, c):
        blocks = [4 * x + 2 * y + c] + [4 * px + 2 * py + c for px, py in _other_chips(x, y)]
        return [(a, b, a, b, (x, y, 1 - c)) for a in range(n) for b in blocks]

    return _Exchange(list(lands), plan, 4 * n, name, deps)


def _reduce_d2d(grads, name, deps=(), rows=None):
    n = len(grads)
    lands = [lax.empty((4,) + g.shape[1:], g.dtype) for g in grads]

    def blk(b):
        return b if rows is None else (b, rows)

    def plan(x, y, c):
        return [(a, blk(2 * s + (1 - c)), n + a, blk(s), (x, y, 1 - c)) for a in range(n) for s in range(4)]

    return _Exchange(list(grads) + lands, plan, 4 * n, name, deps)


def _reduce_ici(chip, name, deps=(), rows=None):
    n = len(chip)
    lands = [lax.empty((3,) + g.shape[1:], g.dtype) for g in chip]

    def blk(b):
        return b if rows is None else (b, rows)

    def plan(x, y, c):
        return [(a, blk(2 * px + py), n + a, blk(k), (px, py, c))
                for a in range(n) for k, (px, py) in enumerate(_other_chips(x, y))]

    return _Exchange(list(chip) + lands, plan, 3 * n, name, deps)


def _pair_add(full, got, core, name, rows=None):
    _, r, cdim = full.shape
    br = _row_block(rows or r, 512)

    def body(c_ref, f_ref, g_ref, o_ref):
        o_ref[...] = (f_ref[...].astype(F32) + g_ref[...].astype(F32)).astype(o_ref.dtype)

    return pl.pallas_call(
        body, name=name,
        grid_spec=pltpu.PrefetchScalarGridSpec(
            num_scalar_prefetch=1, grid=(4, (rows or r) // br),
            in_specs=[pl.BlockSpec((1, br, cdim), lambda s, i, c_ref: (2 * s + c_ref[0], i, 0)),
                      pl.BlockSpec((1, br, cdim), lambda s, i, c_ref: (s, i, 0))],
            out_specs=pl.BlockSpec((1, br, cdim), lambda s, i, c_ref: (s, i, 0))),
        out_shape=jax.ShapeDtypeStruct((4, r, cdim), full.dtype),
        compiler_params=_params(("arbitrary", "arbitrary")),
    )(core, full, got)


def _sum_devices(stack):
    _, r, cdim = stack.shape

    def body(s_ref, o_ref):
        acc = s_ref[0]
        for k in range(1, N_DEV):
            acc = acc + s_ref[k]
        o_ref[...] = acc

    return pl.pallas_call(
        body, name="sum_devices", out_shape=jax.ShapeDtypeStruct((r, cdim), F32),
        compiler_params=_params(),
    )(stack)


def _adamw(w, g, m, v, name):
    r, cdim = w.shape
    br = _row_block(r, 256)

    def body(w_ref, g_ref, m_ref, v_ref, d_ref, nm_ref, nv_ref):
        g = g_ref[...]
        nm = ADAM_B1 * m_ref[...] + (1.0 - ADAM_B1) * g
        nv = ADAM_B2 * v_ref[...] + (1.0 - ADAM_B2) * (g * g)
        m_hat = nm / (1.0 - ADAM_B1 ** ADAM_STEP)
        v_hat = nv / (1.0 - ADAM_B2 ** ADAM_STEP)
        d_ref[...] = -ADAM_LR * (m_hat / (jnp.sqrt(v_hat) + ADAM_EPS) + ADAM_WD * w_ref[...])
        nm_ref[...] = nm
        nv_ref[...] = nv

    spec = pl.BlockSpec((br, cdim), lambda i: (i, 0))
    shape = jax.ShapeDtypeStruct((r, cdim), F32)
    return pl.pallas_call(
        body, name=name, grid=(r // br,), in_specs=[spec] * 4, out_specs=[spec] * 3,
        out_shape=[shape] * 3, compiler_params=_params(("arbitrary",)),
    )(w, g, m, v)


def _sum_adamw(chip, got, slot, w, m, v, name, after):
    rows, cdim = w.shape
    bc = 2 * LANE if cdim % (2 * LANE) == 0 else cdim

    def body(s_ref, c_ref, g_ref, w_ref, m_ref, v_ref, _, go_ref, d_ref, nm_ref, nv_ref):
        g = c_ref[0].astype(F32)
        for k in range(3):
            g = g + g_ref[k].astype(F32)
        nm = ADAM_B1 * m_ref[...] + (1.0 - ADAM_B1) * g
        nv = ADAM_B2 * v_ref[...] + (1.0 - ADAM_B2) * (g * g)
        m_hat = nm / (1.0 - ADAM_B1 ** ADAM_STEP)
        v_hat = nv / (1.0 - ADAM_B2 ** ADAM_STEP)
        go_ref[...] = g
        d_ref[...] = -ADAM_LR * (m_hat / (jnp.sqrt(v_hat) + ADAM_EPS) + ADAM_WD * w_ref[...])
        nm_ref[...] = nm
        nv_ref[...] = nv

    spec = pl.BlockSpec((rows, bc), lambda j, s_ref: (0, j))
    shape = jax.ShapeDtypeStruct((rows, cdim), F32)
    return pl.pallas_call(
        body, name=name,
        grid_spec=pltpu.PrefetchScalarGridSpec(
            num_scalar_prefetch=1, grid=(cdim // bc,),
            in_specs=[pl.BlockSpec((1, rows, bc), lambda j, s_ref: (s_ref[0], 0, j)),
                      pl.BlockSpec((3, rows, bc), lambda j, s_ref: (0, 0, j)), spec, spec, spec, ANY],
            out_specs=[spec] * 4),
        out_shape=[shape] * 4,
        compiler_params=_params(("arbitrary",)),
    )(slot, chip, got, w, m, v, after)


WEIGHTS = ("ffn1_norm_g", "ffn1_w_gate", "ffn1_w_up", "ffn1_w_down", "mix_norm_g", "w_in", "gmlp_v_norm_g",
           "gmlp_w_s", "gmlp_b_s", "mla_q_norm_g", "mla_w_q_up", "mla_kv_norm_g", "mla_w_kv_up", "mla_q_head_g",
           "mla_k_head_g", "gmlp_out_g", "mla_out_g", "w_out", "ffn2_norm_g", "ffn2_w_gate", "ffn2_w_up",
           "ffn2_w_down")
SHARDED = {"ffn1_w_gate": True, "ffn1_w_up": True, "ffn1_w_down": False, "w_in": True, "mla_w_q_up": True,
           "mla_w_kv_up": True, "w_out": False, "ffn2_w_gate": True, "ffn2_w_up": True, "ffn2_w_down": False}


def _col_block(m, target):
    best = LANE
    for cand in range(LANE, min(m, target) + 1, LANE):
        if m % cand == 0:
            best = cand
    return best


def _shard_rows(w, transposed, pad_to=None):
    rows = (w[0].T if transposed else w[0]).astype(BF16)
    if pad_to is not None and pad_to != rows.shape[0]:
        rows = jnp.pad(rows, ((0, pad_to - rows.shape[0]), (0, 0)))
    return rows


def _pack(parts):
    flat = []
    for p in parts:
        f = p.reshape(-1).astype(F32)
        flat.append(jnp.pad(f, (0, _round_up(f.size, LANE) - f.size)))
    flat = jnp.concatenate(flat)
    rows = _round_up(flat.size // LANE, SUBLANE)
    return jnp.pad(flat, (0, rows * LANE - flat.size)).reshape(rows, LANE)


def _unpack(packed, shapes):
    out, row = [], 0
    for shp in shapes:
        size = 1
        for s in shp:
            size *= s
        nrows = _round_up(size, LANE) // LANE
        out.append(packed[row:row + nrows].reshape(-1)[:size].reshape(shp))
        row += nrows
    return out


def kernel(x, positions, ffn1_norm_g, ffn1_w_gate, ffn1_w_up, ffn1_w_down, mix_norm_g, w_in, gmlp_v_norm_g, gmlp_w_s, gmlp_b_s, mla_q_norm_g, mla_w_q_up, mla_kv_norm_g, mla_w_kv_up, mla_q_head_g, mla_k_head_g, gmlp_out_g, mla_out_g, w_out, ffn2_norm_g, ffn2_w_gate, ffn2_w_up, ffn2_w_down, loss_target, m_ffn1_norm_g, m_ffn1_w_gate, m_ffn1_w_up, m_ffn1_w_down, m_mix_norm_g, m_w_in, m_gmlp_v_norm_g, m_gmlp_w_s, m_gmlp_b_s, m_mla_q_norm_g, m_mla_w_q_up, m_mla_kv_norm_g, m_mla_w_kv_up, m_mla_q_head_g, m_mla_k_head_g, m_gmlp_out_g, m_mla_out_g, m_w_out, m_ffn2_norm_g, m_ffn2_w_gate, m_ffn2_w_up, m_ffn2_w_down, v_ffn1_norm_g, v_ffn1_w_gate, v_ffn1_w_up, v_ffn1_w_down, v_mix_norm_g, v_w_in, v_gmlp_v_norm_g, v_gmlp_w_s, v_gmlp_b_s, v_mla_q_norm_g, v_mla_w_q_up, v_mla_kv_norm_g, v_mla_w_kv_up, v_mla_q_head_g, v_mla_k_head_g, v_gmlp_out_g, v_mla_out_g, v_w_out, v_ffn2_norm_g, v_ffn2_w_gate, v_ffn2_w_up, v_ffn2_w_down):
    wts = dict(zip(WEIGHTS, (ffn1_norm_g, ffn1_w_gate, ffn1_w_up, ffn1_w_down, mix_norm_g, w_in, gmlp_v_norm_g, gmlp_w_s, gmlp_b_s, mla_q_norm_g, mla_w_q_up, mla_kv_norm_g, mla_w_kv_up, mla_q_head_g, mla_k_head_g, gmlp_out_g, mla_out_g, w_out, ffn2_norm_g, ffn2_w_gate, ffn2_w_up, ffn2_w_down)))
    mom1 = dict(zip(WEIGHTS, (m_ffn1_norm_g, m_ffn1_w_gate, m_ffn1_w_up, m_ffn1_w_down, m_mix_norm_g, m_w_in, m_gmlp_v_norm_g, m_gmlp_w_s, m_gmlp_b_s, m_mla_q_norm_g, m_mla_w_q_up, m_mla_kv_norm_g, m_mla_w_kv_up, m_mla_q_head_g, m_mla_k_head_g, m_gmlp_out_g, m_mla_out_g, m_w_out, m_ffn2_norm_g, m_ffn2_w_gate, m_ffn2_w_up, m_ffn2_w_down)))
    mom2 = dict(zip(WEIGHTS, (v_ffn1_norm_g, v_ffn1_w_gate, v_ffn1_w_up, v_ffn1_w_down, v_mix_norm_g, v_w_in, v_gmlp_v_norm_g, v_gmlp_w_s, v_gmlp_b_s, v_mla_q_norm_g, v_mla_w_q_up, v_mla_kv_norm_g, v_mla_w_kv_up, v_mla_q_head_g, v_mla_k_head_g, v_gmlp_out_g, v_mla_out_g, v_w_out, v_ffn2_norm_g, v_ffn2_w_gate, v_ffn2_w_up, v_ffn2_w_down)))

    b_loc, seq, d = x.shape
    t = b_loc * seq
    ffs = ffn1_w_gate.shape[2]
    fp = _round_up(ffs, LANE)
    wg = gmlp_v_norm_g.shape[1]
    groups = gmlp_w_s.shape[1]
    rq, rkv = mla_q_norm_g.shape[1], mla_kv_norm_g.shape[1]
    heads = mla_out_g.shape[1]
    assert w_in.shape[2] * N_DEV == 2 * wg + rq + rkv + ROPE and mla_w_kv_up.shape[2] * N_DEV == heads * HEADW
    tm = min(512, t)
    tm_mix = min(256, t)
    blk = min(256, seq)

    xf = x.reshape(t, d)
    target = loss_target.reshape(t, d)
    pos = positions.reshape(t, 1).astype(F32)
    half = ROPE // 2
    inv_freq = 1.0 / (ROPE_THETA ** (jnp.arange(half, dtype=F32) / half))
    freq = jnp.concatenate([inv_freq, inv_freq, jnp.zeros((LANE - ROPE,), F32)])[None, :]
    lane = jnp.arange(LANE)
    masks = jnp.stack([jnp.where(lane < half, -1.0, 0.0),
                       jnp.where((lane >= half) & (lane < ROPE), 1.0, 0.0)]).astype(F32)
    gqh = jnp.pad(mla_q_head_g, ((0, 0), (0, HEADW - QK)))
    gkh = jnp.pad(mla_k_head_g, ((0, 0), (0, HEADW - QK)))
    bias = jnp.repeat(gmlp_b_s[0].T, CHUNK, axis=1)
    gouta = gmlp_out_g.reshape(1, wg)
    goutb = mla_out_g.reshape(1, heads * VHEAD)
    ws = gmlp_w_s[0]

    px, py, pc = _place()
    me = 4 * px + 2 * py + pc
    core = pc.astype(jnp.int32).reshape(1)
    slot = (2 * px + py).astype(jnp.int32).reshape(1)
    order = [n for n in WEIGHTS if n in SHARDED]
    group = {"ffn1": [n for n in order if n.startswith("ffn1")], "ffn2": [n for n in order if n.startswith("ffn2")],
             "mix": [n for n in order if not n.startswith("ffn")]}
    shard = {n: _shard_rows(wts[n], SHARDED[n]) for n in group["ffn1"]}
    frows = ffs if ffs != fp else None

    def tied(arr, token):
        return arr + token[0, 0].astype(arr.dtype)

    xnb, ynb, dgn = 4 * (1 - px) + 2 * py, 4 * px + 2 * (1 - py), 4 * (1 - px) + 2 * (1 - py)
    ids_a = jnp.stack([me, 4 * px + 2 * py + (1 - pc)]).astype(jnp.int32)
    ids_b = jnp.stack([xnb, xnb + 1, ynb, ynb + 1]).astype(jnp.int32)
    ids_c = jnp.stack([dgn, dgn + 1]).astype(jnp.int32)
    g1 = _StagedGather([shard[n] for n in group["ffn1"]], me, "gather_ffn1", pad_to=fp)
    token = g1.start("own")
    token = g1.start("nbr", deps=(token,))
    for n in group["mix"] + group["ffn2"]:
        shard[n] = _shard_rows(tied(wts[n], token), SHARDED[n])
    g3 = _StagedGather([shard[n] for n in group["ffn2"]], me, "gather_ffn2", pad_to=fp)
    g1.wait("own", token)
    x1, xn1, kept1 = _ffn_fwd(xf, None, ffn1_norm_g, ids_a, *g1.lands(), None, tm, "ffn1_fwd_a")
    g1.wait("nbr", x1)
    token = g1.start("diag")
    ici2 = _gather_ici([shard[n] for n in group["mix"]], me, "gather_mix_ici", deps=(token,))
    token = g3.start("nbr", deps=(ici2.token,))
    token = g1.start("nbr_d2d", deps=(token,))
    g1.wait("nbr_d2d", token)
    x1, xn1, kept1 = _ffn_fwd(x1, xn1, None, ids_b, *g1.lands(), kept1, tm, "ffn1_fwd_b")
    g1.wait("diag", x1)
    token = g1.start("diag_d2d")
    d2d2 = _gather_d2d(ici2.wait(x1)[len(group["mix"]):], "gather_mix_d2d", deps=(token,))
    g1.wait("diag_d2d", d2d2.token)
    full = dict(zip(group["ffn1"], g1.lands()))
    x1, xn1, (gd1, sl1, h1) = _ffn_fwd(x1, xn1, None, ids_c, full["ffn1_w_gate"], full["ffn1_w_up"],
                                       full["ffn1_w_down"], kept1, tm, "ffn1_fwd_c")
    full.update(zip(group["mix"], d2d2.wait(x1)))
    win_t = full["w_in"].reshape(-1, d)
    splits = (2 * wg, rq, rkv, LANE)
    wq_t = jnp.pad(full["mla_w_q_up"].reshape(heads, QK, rq), ((0, 0), (0, HEADW - QK), (0, 0)))
    wkv_t = full["mla_w_kv_up"].reshape(heads, HEADW, rkv)
    wout = full["w_out"].reshape(-1, d)
    hn, zuv, cq, ckv, krw = _inproj_fwd(x1, mix_norm_g, win_t, splits, tm)
    ya = _gmlp_fwd(zuv, gmlp_v_norm_g, ws, bias, gouta, tm_mix)
    g3.wait("nbr", ya)
    token = g3.start("diag")
    token = g3.start("own_nbr_d2d", deps=(token,))
    q, k, vv = _mla_proj_fwd(cq, ckv, krw, pos, freq, masks, mla_q_norm_g, mla_kv_norm_g, wq_t, wkv_t,
                             tied(gqh, token), gkh, tm)
    o, lse = _attn_fwd(q, k, vv, seq, blk)
    g3.wait("diag", o)
    token = g3.start("diag_d2d")
    x2, ycat = _out_fwd(ya, o, tied(goutb, token), wout, x1, tm)
    g3.wait("own_nbr_d2d", x2)
    g3.wait("diag_d2d", x2)
    full.update(zip(group["ffn2"], g3.lands()))
    x3, xn2, (gd2, sl2, h2) = _ffn_fwd(x2, None, ffn2_norm_g, jnp.arange(N_DEV, dtype=jnp.int32),
                                       full["ffn2_w_gate"], full["ffn2_w_up"], full["ffn2_w_down"], None, tm,
                                       "ffn2_fwd")
    dx3, loss_part = _loss_head(x3, target, tm)

    outs_g, outs_d, outs_m, outs_v = {}, {}, {}, {}

    def finish(names, chip, got, after):
        for n, cp, gt in zip(names, chip, got):
            rows_of = (lambda a: a[0].T) if SHARDED[n] else (lambda a: a[0])
            res = _sum_adamw(cp, gt, slot, rows_of(wts[n]), rows_of(mom1[n]), rows_of(mom2[n]), "adamw_" + n, after)
            outs_g[n], outs_d[n], outs_m[n], outs_v[n] = [r.T[None] if SHARDED[n] else r[None] for r in res]
            after = res[3]
        return after

    def chip_sums(names, ex, after, rows=None):
        res = ex.wait(after)
        return [_pair_add(f, gt, core, "pair_add_" + n, rows)
                for n, f, gt in zip(names, res[:len(names)], res[len(names):])]

    tk = min(1024, t)
    grads = {}
    small = {}
    dx2, small["ffn2_norm_g"], da2, db2 = _ffn_bwd(
        dx3, x2, ffn2_norm_g, gd2, sl2, full["ffn2_w_gate"], full["ffn2_w_up"], full["ffn2_w_down"], tm, "ffn2_bwd")
    grads["ffn2_w_gate"] = _matmul_tn(da2, xn2, fp, d, tk, BF16, "dw_ffn2_gate").reshape(N_DEV, fp, d)
    grads["ffn2_w_up"] = _matmul_tn(db2, xn2, fp, d, tk, BF16, "dw_ffn2_up").reshape(N_DEV, fp, d)
    grads["ffn2_w_down"] = _matmul_tn(h2, dx3, fp, d, tk, BF16, "dw_ffn2_down", rhs_scale=0.5).reshape(
        N_DEV, fp, d)
    red_a2 = _reduce_d2d([grads[n] for n in group["ffn2"]], "reduce_ffn2_d2d", rows=frows)
    dya, do, delta, small["mla_out_g"] = _out_bwd(dx2, o, tied(goutb, red_a2.token), wout, wg, tm)
    grads["w_out"] = _matmul_tn(ycat, dx2, _col_block(ycat.shape[1], 768), d, tk, BF16, "dw_out").reshape(
        N_DEV, -1, d)
    chip2 = chip_sums(group["ffn2"], red_a2, dya, frows)
    red_b2 = _reduce_ici(chip2, "reduce_ffn2_ici", rows=frows)
    dq, dk, dv = _attn_bwd(q, k, vv, do, lse, delta, seq, blk, red_b2.token)
    (dcq, dckv, dkrw, dwq, dwkv, small["mla_q_norm_g"], small["mla_kv_norm_g"], dgqh, dgkh) = _mla_proj_bwd(
        dq, dk, dv, cq, ckv, krw, pos, freq, masks, mla_q_norm_g, mla_kv_norm_g, wq_t, wkv_t, gqh, gkh, tm)
    small["mla_q_head_g"], small["mla_k_head_g"] = dgqh[:, :QK], dgkh[:, :QK]
    grads["mla_w_q_up"] = dwq[:, :QK].astype(BF16).reshape(N_DEV, -1, rq)
    grads["mla_w_kv_up"] = dwkv.astype(BF16).reshape(N_DEV, -1, rkv)
    dzuv, small["gmlp_w_s"], dbs, small["gmlp_v_norm_g"], small["gmlp_out_g"] = _gmlp_bwd(
        dya, zuv, gmlp_v_norm_g, ws, bias, gouta, tm_mix)
    small["gmlp_b_s"] = dbs[:, :, 0]
    dx1, small["mix_norm_g"], dzc = _inproj_bwd([dzuv, dcq, dckv, dkrw], x1, mix_norm_g, win_t, dx2, splits,
                                                tm_mix)
    grads["w_in"] = _matmul_tn(dzc, hn, _col_block(dzc.shape[1], 768), d, tk, BF16, "dw_in",
                               out_rows=win_t.shape[0]).reshape(N_DEV, -1, d)
    res_b2 = red_b2.wait(grads["w_in"])
    red_am = _reduce_d2d([grads[n] for n in group["mix"]], "reduce_mix_d2d")

    def ffn1_dw(n, lhs, rhs, scale, token):
        return _matmul_tn(lhs, rhs, fp, d, tk, BF16, "dw_" + n, rhs_scale=scale, deps=(token,)).reshape(N_DEV, fp, d)

    gr = ffn1_dw("ffn1_w_down", h1, dx1, 0.5, red_am.token)
    red_ad = _reduce_d2d([gr], "reduce_ffn1_w_down_d2d", deps=(red_am.token,), rows=frows)
    dx0, small["ffn1_norm_g"], da1, db1 = _ffn_bwd(
        dx1, xf, tied(ffn1_norm_g, red_ad.token), gd1, sl1, full["ffn1_w_gate"], full["ffn1_w_up"],
        full["ffn1_w_down"], tm, "ffn1_bwd")
    chipm = chip_sums(group["mix"], red_am, dx0)
    red_bm = _reduce_ici(chipm, "reduce_mix_ici")
    red_b = [("ffn1_w_down", _reduce_ici(chip_sums(["ffn1_w_down"], red_ad, dx0, frows), "reduce_ffn1_w_down_ici",
                                         deps=(red_bm.token,), rows=frows))]
    rep = [n for n in WEIGHTS if n not in SHARDED]
    small_ici = _gather_ici([_pack([small[n] for n in rep] + [loss_part])], me, "gather_small_ici",
                            deps=(red_b[-1][1].token,))
    gr = ffn1_dw("ffn1_w_gate", da1, xn1, None, small_ici.token)
    red_ag = _reduce_d2d([gr], "reduce_ffn1_w_gate_d2d", rows=frows)
    n2 = group["ffn2"]
    after = finish(n2[0:1], res_b2[0:1], res_b2[3:4], red_ag.token)
    red_b.append(("ffn1_w_gate", _reduce_ici(chip_sums(["ffn1_w_gate"], red_ag, after, frows),
                                             "reduce_ffn1_w_gate_ici", rows=frows)))
    gr = ffn1_dw("ffn1_w_up", db1, xn1, None, red_b[-1][1].token)
    small_d2d = _gather_d2d(small_ici.wait(gr)[1:], "gather_small_d2d")
    red_au = _reduce_d2d([gr], "reduce_ffn1_w_up_d2d", deps=(small_d2d.token,), rows=frows)
    after = finish(n2[1:2], res_b2[1:2], res_b2[4:5], red_au.token)
    red_b.append(("ffn1_w_up", _reduce_ici(chip_sums(["ffn1_w_up"], red_au, after, frows),
                                           "reduce_ffn1_w_up_ici", rows=frows)))
    after = finish(n2[2:3], res_b2[2:3], res_b2[5:6], red_b[-1][1].token)
    res = red_bm.wait(after)
    nm_ = len(group["mix"])
    after = finish(group["mix"], res[:nm_], res[nm_:], after)
    total = _sum_devices(small_d2d.wait(after)[0])
    zero = jnp.zeros((1,), F32)
    dlt, nm, nv = _adamw(_pack([wts[n] for n in rep] + [zero]), total, _pack([mom1[n] for n in rep] + [zero]),
                         _pack([mom2[n] for n in rep] + [zero]), "adamw_small")
    shapes = [wts[n].shape for n in rep] + [(1,)]
    for n, g, dl, m1, m2 in zip(rep, _unpack(total, shapes), _unpack(dlt, shapes), _unpack(nm, shapes),
                                _unpack(nv, shapes)):
        outs_g[n], outs_d[n], outs_m[n], outs_v[n] = g, dl, m1, m2
    loss = _unpack(total, shapes)[-1].reshape(())
    after = dlt
    for n, ex in red_b:
        res = ex.wait(after)
        after = finish([n], res[:1], res[1:], after)

    return (loss, dx0.reshape(b_loc, seq, d), *[outs_g[n] for n in WEIGHTS], *[outs_d[n] for n in WEIGHTS],
            *[outs_m[n] for n in WEIGHTS], *[outs_v[n] for n in WEIGHTS])
```

```python
import functools

import jax
import jax.numpy as jnp
from jax import lax
from jax.experimental import pallas as pl
from jax.experimental.pallas import tpu as pltpu

F32 = jnp.float32
BF16 = jnp.bfloat16
EPS = 1e-6
LANE = 128
SUBLANE = 8
N_DEV = 8
VMEM_LIMIT = 60 * 1024 * 1024
NOPE = 128
ROPE = 64
VHEAD = 128
QK = NOPE + ROPE
HEADW = 2 * LANE
CHUNK = 128
ROPE_THETA = 10000.0
ADAM_LR, ADAM_B1, ADAM_B2, ADAM_EPS, ADAM_WD, ADAM_STEP = 0.001, 0.9, 0.999, 1e-08, 0.01, 10
MESH = pl.DeviceIdType.MESH
ANY = pl.BlockSpec(memory_space=pl.ANY)
WHOLE_VMEM = pl.BlockSpec(memory_space=pltpu.VMEM)


def _params(sem=None):
    return pltpu.CompilerParams(dimension_semantics=sem, vmem_limit_bytes=VMEM_LIMIT)


def _round_up(n, m):
    return -(-n // m) * m


def _row_block(rows, target):
    best = rows
    for cand in range(SUBLANE, min(rows, target) + 1, SUBLANE):
        if rows % cand == 0:
            best = cand
    return best if best <= target else rows


def _nn(a, b):
    return jnp.dot(a, b, preferred_element_type=F32)


def _nt(a, b):
    return lax.dot_general(a, b, (((1,), (1,)), ((), ())), preferred_element_type=F32)


def _tn(a, b):
    return lax.dot_general(a, b, (((0,), (0,)), ((), ())), preferred_element_type=F32)


def _rstd(x, n):
    return lax.rsqrt(jnp.sum(x * x, axis=-1, keepdims=True) * (1.0 / n) + EPS)


def _rms_fwd(x, g, n):
    return x * _rstd(x, n) * g


def _rms_bwd(x, g, dy, n):
    r = _rstd(x, n)
    xh = x * r
    dyg = dy * g
    dx = r * (dyg - xh * (jnp.sum(dyg * xh, axis=-1, keepdims=True) * (1.0 / n)))
    return dx, jnp.sum(dy * xh, axis=0, keepdims=True)


def _gelu(x):
    return 0.5 * x * (1.0 + lax.erf(x * 0.7071067811865476))


def _gelu_grad(x):
    return 0.5 * (1.0 + lax.erf(x * 0.7071067811865476)) + x * jnp.exp(-0.5 * x * x) * 0.3989422804014327


def _ffn_fwd(base, xn, g, ids, wg_t, wu_t, wd, saved, tm, name):
    t, d = base.shape
    nb, fp, _ = wg_t.shape
    n = ids.shape[0]
    first = xn is None
    if saved is None:
        saved = [lax.empty((t, nb * fp), BF16) for _ in range(3)]

    def body(ids_ref, *refs):
        if first:
            base_ref, g_ref, wg_ref, wu_ref, wd_ref, _, _, _, out_ref, xn_ref, gd_ref, sl_ref, h_ref, acc = refs
        else:
            base_ref, xn_ref, wg_ref, wu_ref, wd_ref, _, _, _, out_ref, gd_ref, sl_ref, h_ref, acc = refs
        j = pl.program_id(1)

        @pl.when(j == 0)
        def _():
            if first:
                xn_ref[...] = _rms_fwd(base_ref[...], g_ref[...], d).astype(BF16)
            acc[...] = jnp.zeros_like(acc)

        xnb = xn_ref[...]
        a = _nt(xnb, wg_ref[0])
        b = _nt(xnb, wu_ref[0])
        s = jax.nn.sigmoid(a)
        sl = a * s
        h = (sl * b).astype(BF16)
        gd_ref[...] = (b * (s * (1.0 + a * (1.0 - s)))).astype(BF16)
        sl_ref[...] = sl.astype(BF16)
        h_ref[...] = h
        acc[...] += _nn(h, wd_ref[0])

        @pl.when(j == n - 1)
        def _():
            out_ref[...] = base_ref[...] + 0.5 * acc[...]

    wspec = pl.BlockSpec((1, fp, d), lambda i, j, ids_ref: (ids_ref[j], 0, 0))
    row = pl.BlockSpec((tm, d), lambda i, j, ids_ref: (i, 0))
    ff = pl.BlockSpec((tm, fp), lambda i, j, ids_ref: (i, ids_ref[j]))
    ffs = jax.ShapeDtypeStruct((t, nb * fp), BF16)
    second = pl.BlockSpec((1, d), lambda i, j, ids_ref: (0, 0)) if first else row
    n_row_outs = 2 if first else 1
    res = pl.pallas_call(
        body, name=name,
        grid_spec=pltpu.PrefetchScalarGridSpec(
            num_scalar_prefetch=1, grid=(t // tm, n),
            in_specs=[row, second, wspec, wspec, wspec, ANY, ANY, ANY],
            out_specs=[row] * n_row_outs + [ff, ff, ff],
            scratch_shapes=[pltpu.VMEM((tm, d), F32)]),
        out_shape=[jax.ShapeDtypeStruct((t, d), F32)] + ([jax.ShapeDtypeStruct((t, d), BF16)] if first else [])
        + [ffs, ffs, ffs],
        input_output_aliases={6 + k: n_row_outs + k for k in range(3)},
        compiler_params=_params(("arbitrary", "arbitrary")),
    )(ids, base, g if first else xn, wg_t, wu_t, wd, *saved)
    return (res[0], res[1] if first else xn, list(res[n_row_outs:]))


def _ffn_bwd(dout, x, g, gd, sl, wg_t, wu_t, wd, tm, name):
    t, d = x.shape
    nb, fp, _ = wg_t.shape

    def body(do_hbm, x_hbm, g_ref, gd_ref, sl_ref, wg_ref, wu_ref, wd_ref, wd_next_ref,
             dx_hbm, dg_ref, da_ref, db_ref, acc, rowbuf, dy_scr, dh_scr, sems):
        i, j = pl.program_id(0), pl.program_id(1)
        rows = pl.ds(pl.multiple_of(i * tm, tm), tm)
        get_do = pltpu.make_async_copy(do_hbm.at[rows, :], rowbuf, sems.at[0])
        get_x = pltpu.make_async_copy(x_hbm.at[rows, :], rowbuf, sems.at[0])
        put_dx = pltpu.make_async_copy(acc, dx_hbm.at[rows, :], sems.at[1])

        @pl.when(j == 0)
        def _():
            get_do.start()

            @pl.when(i > 0)
            def _():
                put_dx.wait()

            get_do.wait()
            dy_scr[...] = (0.5 * rowbuf[...]).astype(BF16)
            acc[...] = jnp.zeros_like(acc)
            dh_scr[0] = _nt(dy_scr[...], wd_ref[0])
            get_x.start()

        @pl.when((i == 0) & (j == 0))
        def _():
            dg_ref[...] = jnp.zeros_like(dg_ref)

        dh = dh_scr[j % 2]
        dh_scr[(j + 1) % 2] = _nt(dy_scr[...], wd_next_ref[0])
        da = (dh * gd_ref[...].astype(F32)).astype(BF16)
        db = (dh * sl_ref[...].astype(F32)).astype(BF16)
        da_ref[...] = da
        db_ref[...] = db
        acc[...] += _nn(da, wg_ref[0]) + _nn(db, wu_ref[0])

        @pl.when(j == nb - 1)
        def _():
            get_x.wait()
            dxn, dg = _rms_bwd(rowbuf[...], g_ref[...], acc[...], d)
            dg_ref[...] += dg
            acc[...] = dxn
            get_do.start()
            get_do.wait()
            acc[...] += rowbuf[...]
            put_dx.start()

            @pl.when(i == t // tm - 1)
            def _():
                put_dx.wait()

    wspec = pl.BlockSpec((1, fp, d), lambda i, j: (j, 0, 0))
    wnext = pl.BlockSpec((1, fp, d), lambda i, j: (jnp.minimum(j + 1, nb - 1), 0, 0))
    vec = pl.BlockSpec((1, d), lambda i, j: (0, 0))
    ff = pl.BlockSpec((tm, fp), lambda i, j: (i, j))
    ffs = jax.ShapeDtypeStruct((t, nb * fp), BF16)
    return pl.pallas_call(
        body, name=name, grid=(t // tm, nb),
        in_specs=[ANY, ANY, vec, ff, ff, wspec, wspec, pl.BlockSpec((1, fp, d), lambda i, j: (0, 0, 0)), wnext],
        out_specs=[ANY, vec, ff, ff],
        out_shape=[jax.ShapeDtypeStruct((t, d), F32), jax.ShapeDtypeStruct((1, d), F32), ffs, ffs],
        scratch_shapes=[pltpu.VMEM((tm, d), F32), pltpu.VMEM((tm, d), F32), pltpu.VMEM((tm, d), BF16),
                        pltpu.VMEM((2, tm, fp), F32), pltpu.SemaphoreType.DMA((2,))],
        compiler_params=_params(("arbitrary", "arbitrary")),
    )(dout, x, g, gd, sl, wg_t, wu_t, wd, wd)


def _matmul_tn(lhs, rhs, bm, bn, tk, out_dtype, name, rhs_scale=None, deps=(), out_rows=None):
    t, m = lhs.shape
    n = rhs.shape[1]
    nk = t // tk
    out_rows = m if out_rows is None else out_rows

    def body(l_ref, r_ref, *refs):
        o_ref, acc = refs[len(deps):]
        k = pl.program_id(2)

        @pl.when(k == 0)
        def _():
            acc[...] = jnp.zeros_like(acc)

        r = r_ref[...] if rhs_scale is None else rhs_scale * r_ref[...]
        acc[...] += _tn(l_ref[...].astype(BF16), r.astype(BF16))

        @pl.when(k == nk - 1)
        def _():
            o_ref[...] = acc[...].astype(out_dtype)

    return pl.pallas_call(
        body, name=name, grid=(m // bm, n // bn, nk),
        in_specs=[pl.BlockSpec((tk, bm), lambda i, j, k: (k, i)), pl.BlockSpec((tk, bn), lambda i, j, k: (k, j))]
        + [ANY] * len(deps),
        out_specs=pl.BlockSpec((bm, bn), lambda i, j, k: (i, j)),
        out_shape=jax.ShapeDtypeStruct((out_rows, n), out_dtype),
        scratch_shapes=[pltpu.VMEM((bm, bn), F32)],
        compiler_params=_params(("arbitrary", "arbitrary", "arbitrary")),
    )(lhs, rhs, *deps)


def _last_rows_padded(w_ref, tail_ref, off, real):
    @pl.when(pl.program_id(0) == 0)
    def _():
        tail_ref[...] = jnp.zeros_like(tail_ref)
        tail_ref[0:real, :] = w_ref[off:off + real, :]


def _inproj_fwd(x, g, w_t, splits, tm):
    t, d = x.shape
    offs = [sum(splits[:k]) for k in range(len(splits))]
    real_last = w_t.shape[0] - offs[-1]

    def body(x_ref, g_ref, w_ref, hn_ref, *refs):
        z_refs, tail_ref = refs[:-1], refs[-1]
        _last_rows_padded(w_ref, tail_ref, offs[-1], real_last)
        hn = _rms_fwd(x_ref[...], g_ref[...], d).astype(BF16)
        hn_ref[...] = hn
        for z_ref, o, n in zip(z_refs[:-1], offs, splits):
            z_ref[...] = _nt(hn, w_ref[o:o + n, :])
        z_refs[-1][...] = _nt(hn, tail_ref[...])

    row = pl.BlockSpec((tm, d), lambda i: (i, 0))
    return pl.pallas_call(
        body, name="inproj_fwd", grid=(t // tm,),
        in_specs=[row, pl.BlockSpec((1, d), lambda i: (0, 0)), WHOLE_VMEM],
        out_specs=[row] + [pl.BlockSpec((tm, n), lambda i: (i, 0)) for n in splits],
        out_shape=[jax.ShapeDtypeStruct((t, d), BF16)] + [jax.ShapeDtypeStruct((t, n), F32) for n in splits],
        scratch_shapes=[pltpu.VMEM((splits[-1], d), BF16)],
        compiler_params=_params(("arbitrary",)),
    )(x, g, w_t)


def _inproj_bwd(dzs, x, g, w_t, dres, splits, tm):
    t, d = x.shape
    offs = [sum(splits[:k]) for k in range(len(splits))]
    ni = sum(splits)
    nz = len(splits)
    real_last = w_t.shape[0] - offs[-1]

    def body(*refs):
        dz_refs = refs[:nz]
        x_ref, g_ref, w_ref, dres_ref, dx_ref, dg_ref, dzc_ref, tail_ref = refs[nz:]
        _last_rows_padded(w_ref, tail_ref, offs[-1], real_last)
        dhn = jnp.zeros((tm, d), F32)
        for k, (dz_ref, o, n) in enumerate(zip(dz_refs, offs, splits)):
            dz = dz_ref[...].astype(BF16)
            dzc_ref[:, o:o + n] = dz
            dhn += _nn(dz, tail_ref[...] if k == nz - 1 else w_ref[o:o + n, :])
        dx, dg = _rms_bwd(x_ref[...], g_ref[...], dhn, d)
        dx_ref[...] = dres_ref[...] + dx

        @pl.when(pl.program_id(0) == 0)
        def _():
            dg_ref[...] = jnp.zeros_like(dg_ref)

        dg_ref[...] += dg

    row = pl.BlockSpec((tm, d), lambda i: (i, 0))
    vec = pl.BlockSpec((1, d), lambda i: (0, 0))
    return pl.pallas_call(
        body, name="inproj_bwd", grid=(t // tm,),
        in_specs=[pl.BlockSpec((tm, n), lambda i: (i, 0)) for n in splits] + [row, vec, WHOLE_VMEM, row],
        out_specs=[row, vec, pl.BlockSpec((tm, ni), lambda i: (i, 0))],
        out_shape=[jax.ShapeDtypeStruct((t, d), F32), jax.ShapeDtypeStruct((1, d), F32),
                   jax.ShapeDtypeStruct((t, ni), BF16)],
        scratch_shapes=[pltpu.VMEM((splits[-1], d), BF16)],
        compiler_params=_params(("arbitrary",)),
    )(*dzs, x, g, w_t, dres)


def _tril_bf16(ws_ref, grp):
    rows = lax.broadcasted_iota(jnp.int32, (CHUNK, CHUNK), 0)
    cols = lax.broadcasted_iota(jnp.int32, (CHUNK, CHUNK), 1)
    return jnp.where(rows >= cols, ws_ref[grp], 0.0).astype(BF16)


def _gmlp_mix(zuv_ref, gv_ref, ws_ref, bias_ref, v_scr, mixed_scr, tm, w, groups):
    u = _gelu(zuv_ref[:, 0:w])
    v0 = _gelu(zuv_ref[:, w:2 * w])
    v_scr[...] = _rms_fwd(v0, gv_ref[...], w).astype(BF16)
    for grp in range(groups):
        wsm = _tril_bf16(ws_ref, grp)
        lanes = slice(grp * CHUNK, (grp + 1) * CHUNK)
        for c in range(tm // CHUNK):
            rows = slice(c * CHUNK, (c + 1) * CHUNK)
            mixed_scr[rows, lanes] = _nn(wsm, v_scr[rows, lanes]) + bias_ref[:, lanes]
    return u, v0


def _gmlp_fwd(zuv, gv, ws, bias, gout, tm):
    t, w2 = zuv.shape
    w = w2 // 2
    groups = ws.shape[0]

    def body(zuv_ref, gv_ref, ws_ref, bias_ref, go_ref, y_ref, v_scr, mixed_scr):
        u, _ = _gmlp_mix(zuv_ref, gv_ref, ws_ref, bias_ref, v_scr, mixed_scr, tm, w, groups)
        ya = u * mixed_scr[...]
        for grp in range(groups):
            lanes = slice(grp * CHUNK, (grp + 1) * CHUNK)
            y_ref[:, lanes] = _rms_fwd(ya[:, lanes], go_ref[:, lanes], CHUNK).astype(BF16)

    const2 = lambda i: (0, 0)
    return pl.pallas_call(
        body, name="gmlp_fwd", grid=(t // tm,),
        in_specs=[pl.BlockSpec((tm, w2), lambda i: (i, 0)), pl.BlockSpec((1, w), const2),
                  pl.BlockSpec((groups, CHUNK, CHUNK), lambda i: (0, 0, 0)),
                  pl.BlockSpec((CHUNK, w), const2), pl.BlockSpec((1, w), const2)],
        out_specs=pl.BlockSpec((tm, w), lambda i: (i, 0)),
        out_shape=jax.ShapeDtypeStruct((t, w), BF16),
        scratch_shapes=[pltpu.VMEM((tm, w), BF16), pltpu.VMEM((tm, w), F32)],
        compiler_params=_params(("arbitrary",)),
    )(zuv, gv, ws, bias, gout)


def _gmlp_bwd(dy, zuv, gv, ws, bias, gout, tm):
    t, w2 = zuv.shape
    w = w2 // 2
    groups = ws.shape[0]

    def body(dy_ref, zuv_ref, gv_ref, ws_ref, bias_ref, go_ref,
             dz_ref, dws_ref, dbias_ref, dgv_ref, dgo_ref, v_scr, mixed_scr, dmix_scr, dv_scr):
        @pl.when(pl.program_id(0) == 0)
        def _():
            dws_ref[...] = jnp.zeros_like(dws_ref)
            dbias_ref[...] = jnp.zeros_like(dbias_ref)
            dgv_ref[...] = jnp.zeros_like(dgv_ref)
            dgo_ref[...] = jnp.zeros_like(dgo_ref)

        u, v0 = _gmlp_mix(zuv_ref, gv_ref, ws_ref, bias_ref, v_scr, mixed_scr, tm, w, groups)
        mixed = mixed_scr[...]
        ya = u * mixed
        for grp in range(groups):
            lanes = slice(grp * CHUNK, (grp + 1) * CHUNK)
            dya, dgo = _rms_bwd(ya[:, lanes], go_ref[:, lanes], dy_ref[:, lanes], CHUNK)
            dgo_ref[:, lanes] += dgo
            dz_ref[:, lanes] = dya * mixed[:, lanes] * _gelu_grad(zuv_ref[:, lanes])
            dmix_scr[:, lanes] = dya * u[:, lanes]
        for grp in range(groups):
            wsm = _tril_bf16(ws_ref, grp)
            lanes = slice(grp * CHUNK, (grp + 1) * CHUNK)
            dws = jnp.zeros((CHUNK, CHUNK), F32)
            dbias = jnp.zeros((CHUNK, CHUNK), F32)
            for c in range(tm // CHUNK):
                rows = slice(c * CHUNK, (c + 1) * CHUNK)
                dm = dmix_scr[rows, lanes]
                dmb = dm.astype(BF16)
                dv_scr[rows, lanes] = _tn(wsm, dmb)
                dws += _nt(dmb, v_scr[rows, lanes])
                dbias += dm
            rr = lax.broadcasted_iota(jnp.int32, (CHUNK, CHUNK), 0)
            cc = lax.broadcasted_iota(jnp.int32, (CHUNK, CHUNK), 1)
            dws_ref[grp] += jnp.where(rr >= cc, dws, 0.0)
            dbias_ref[grp] += jnp.sum(dbias, axis=1, keepdims=True)
        dv0, dgv = _rms_bwd(v0, gv_ref[...], dv_scr[...], w)
        dgv_ref[...] += dgv
        dz_ref[:, w:2 * w] = dv0 * _gelu_grad(zuv_ref[:, w:2 * w])

    const2 = lambda i: (0, 0)
    const3 = lambda i: (0, 0, 0)
    return pl.pallas_call(
        body, name="gmlp_bwd", grid=(t // tm,),
        in_specs=[pl.BlockSpec((tm, w), lambda i: (i, 0)), pl.BlockSpec((tm, w2), lambda i: (i, 0)),
                  pl.BlockSpec((1, w), const2), pl.BlockSpec((groups, CHUNK, CHUNK), const3),
                  pl.BlockSpec((CHUNK, w), const2), pl.BlockSpec((1, w), const2)],
        out_specs=[pl.BlockSpec((tm, w2), lambda i: (i, 0)), pl.BlockSpec((groups, CHUNK, CHUNK), const3),
                   pl.BlockSpec((groups, CHUNK, 1), const3), pl.BlockSpec((1, w), const2), pl.BlockSpec((1, w), const2)],
        out_shape=[jax.ShapeDtypeStruct((t, w2), F32), jax.ShapeDtypeStruct((groups, CHUNK, CHUNK), F32),
                   jax.ShapeDtypeStruct((groups, CHUNK, 1), F32), jax.ShapeDtypeStruct((1, w), F32),
                   jax.ShapeDtypeStruct((1, w), F32)],
        scratch_shapes=[pltpu.VMEM((tm, w), BF16), pltpu.VMEM((tm, w), F32),
                        pltpu.VMEM((tm, w), F32), pltpu.VMEM((tm, w), F32)],
        compiler_params=_params(("arbitrary",)),
    )(dy, zuv, gv, ws, bias, gout)


def _rot(x, m_lo, m_hi):
    return pltpu.roll(x, LANE - ROPE // 2, 1) * m_lo + pltpu.roll(x, ROPE // 2, 1) * m_hi


def _rope_tables(pos_ref, freq_ref):
    ang = pos_ref[...] * freq_ref[...]
    return jnp.cos(ang), jnp.sin(ang)


def _mla_proj_fwd(cq, ckv, krw, pos, freq, masks, gq, gkv, wq_t, wkv_t, gqh, gkh, tm):
    t, rq = cq.shape
    rkv = ckv.shape[1]
    heads = wq_t.shape[0]

    def body(cq_ref, ckv_ref, kr_ref, pos_ref, freq_ref, mk_ref, gq_ref, gkv_ref, wq_ref, wkv_ref,
             gqh_ref, gkh_ref, q_ref, k_ref, v_ref):
        cos, sin = _rope_tables(pos_ref, freq_ref)
        m_lo, m_hi = mk_ref[0:1, :], mk_ref[1:2, :]
        cqn = _rms_fwd(cq_ref[...], gq_ref[...], rq).astype(BF16)
        ckvn = _rms_fwd(ckv_ref[...], gkv_ref[...], rkv).astype(BF16)
        kr = kr_ref[...]
        kr_ss = jnp.sum(kr * kr, axis=-1, keepdims=True)
        for h in range(heads):
            qh = _nt(cqn, wq_ref[h])
            qn = qh * _rstd(qh, QK) * gqh_ref[...]
            qr = qn[:, LANE:]
            q_ref[h, :, 0:LANE] = qn[:, 0:LANE].astype(BF16)
            q_ref[h, :, LANE:] = (qr * cos + _rot(qr, m_lo, m_hi) * sin).astype(BF16)
            kvh = _nt(ckvn, wkv_ref[h])
            kn = kvh[:, 0:LANE]
            rk = lax.rsqrt((jnp.sum(kn * kn, axis=-1, keepdims=True) + kr_ss) * (1.0 / QK) + EPS)
            k_ref[h, :, 0:LANE] = (kn * rk * gkh_ref[:, 0:LANE]).astype(BF16)
            krn = kr * rk * gkh_ref[:, LANE:]
            k_ref[h, :, LANE:] = (krn * cos + _rot(krn, m_lo, m_hi) * sin).astype(BF16)
            v_ref[h] = kvh[:, LANE:].astype(BF16)

    c2 = lambda i: (0, 0)
    c3 = lambda i: (0, 0, 0)
    return pl.pallas_call(
        body, name="mla_proj_fwd", grid=(t // tm,),
        in_specs=[pl.BlockSpec((tm, rq), lambda i: (i, 0)), pl.BlockSpec((tm, rkv), lambda i: (i, 0)),
                  pl.BlockSpec((tm, LANE), lambda i: (i, 0)), pl.BlockSpec((tm, 1), lambda i: (i, 0)),
                  pl.BlockSpec((1, LANE), c2), pl.BlockSpec((2, LANE), c2),
                  pl.BlockSpec((1, rq), c2), pl.BlockSpec((1, rkv), c2),
                  pl.BlockSpec((heads, HEADW, rq), c3), pl.BlockSpec((heads, HEADW, rkv), c3),
                  pl.BlockSpec((1, HEADW), c2), pl.BlockSpec((1, HEADW), c2)],
        out_specs=[pl.BlockSpec((heads, tm, HEADW), lambda i: (0, i, 0)),
                   pl.BlockSpec((heads, tm, HEADW), lambda i: (0, i, 0)),
                   pl.BlockSpec((heads, tm, VHEAD), lambda i: (0, i, 0))],
        out_shape=[jax.ShapeDtypeStruct((heads, t, HEADW), BF16), jax.ShapeDtypeStruct((heads, t, HEADW), BF16),
                   jax.ShapeDtypeStruct((heads, t, VHEAD), BF16)],
        compiler_params=_params(("arbitrary",)),
    )(cq, ckv, krw, pos, freq, masks, gq, gkv, wq_t, wkv_t, gqh, gkh)


def _mla_proj_bwd(dq, dk, dv, cq, ckv, krw, pos, freq, masks, gq, gkv, wq_t, wkv_t, gqh, gkh, tm):
    t, rq = cq.shape
    rkv = ckv.shape[1]
    heads = wq_t.shape[0]

    def body(dq_ref, dk_ref, dv_ref, cq_ref, ckv_ref, kr_ref, pos_ref, freq_ref, mk_ref, gq_ref, gkv_ref,
             wq_ref, wkv_ref, gqh_ref, gkh_ref,
             dcq_ref, dckv_ref, dkr_ref, dwq_ref, dwkv_ref, dgq_ref, dgkv_ref, dgqh_ref, dgkh_ref):
        @pl.when(pl.program_id(0) == 0)
        def _():
            for r in (dwq_ref, dwkv_ref, dgq_ref, dgkv_ref, dgqh_ref, dgkh_ref):
                r[...] = jnp.zeros_like(r)

        cos, sin = _rope_tables(pos_ref, freq_ref)
        m_lo, m_hi = mk_ref[0:1, :], mk_ref[1:2, :]

        def unrope(dy):
            return dy * cos - _rot(dy * sin, m_lo, m_hi)

        cqn = _rms_fwd(cq_ref[...], gq_ref[...], rq).astype(BF16)
        ckvn = _rms_fwd(ckv_ref[...], gkv_ref[...], rkv).astype(BF16)
        kr = kr_ref[...]
        kr_ss = jnp.sum(kr * kr, axis=-1, keepdims=True)
        dcqn = jnp.zeros((tm, rq), F32)
        dckvn = jnp.zeros((tm, rkv), F32)
        dkr = jnp.zeros((tm, LANE), F32)
        for h in range(heads):
            qh = _nt(cqn, wq_ref[h])
            dqn = jnp.concatenate([dq_ref[h, :, 0:LANE], unrope(dq_ref[h, :, LANE:])], axis=1)
            dqh, dg = _rms_bwd(qh, gqh_ref[...], dqn, QK)
            dgqh_ref[...] += dg
            dqh = dqh.astype(BF16)
            dcqn += _nn(dqh, wq_ref[h])
            dwq_ref[h] += _tn(dqh, cqn)

            kvh = _nt(ckvn, wkv_ref[h])
            kn = kvh[:, 0:LANE]
            rk = lax.rsqrt((jnp.sum(kn * kn, axis=-1, keepdims=True) + kr_ss) * (1.0 / QK) + EPS)
            dkn_n = dk_ref[h, :, 0:LANE]
            dkr_n = unrope(dk_ref[h, :, LANE:])
            knh, krh = kn * rk, kr * rk
            dgkh_ref[:, 0:LANE] += jnp.sum(dkn_n * knh, axis=0, keepdims=True)
            dgkh_ref[:, LANE:] += jnp.sum(dkr_n * krh, axis=0, keepdims=True)
            dkn_g, dkr_g = dkn_n * gkh_ref[:, 0:LANE], dkr_n * gkh_ref[:, LANE:]
            proj = (jnp.sum(dkn_g * knh, axis=-1, keepdims=True)
                    + jnp.sum(dkr_g * krh, axis=-1, keepdims=True)) * (1.0 / QK)
            dkr += rk * (dkr_g - krh * proj)
            dkvh = jnp.concatenate([rk * (dkn_g - knh * proj), dv_ref[h]], axis=1).astype(BF16)
            dckvn += _nn(dkvh, wkv_ref[h])
            dwkv_ref[h] += _tn(dkvh, ckvn)
        dkr_ref[...] = dkr
        dcq, dg = _rms_bwd(cq_ref[...], gq_ref[...], dcqn, rq)
        dcq_ref[...] = dcq
        dgq_ref[...] += dg
        dckv, dg = _rms_bwd(ckv_ref[...], gkv_ref[...], dckvn, rkv)
        dckv_ref[...] = dckv
        dgkv_ref[...] += dg

    c2 = lambda i: (0, 0)
    c3 = lambda i: (0, 0, 0)
    hq = pl.BlockSpec((heads, tm, HEADW), lambda i: (0, i, 0))
    return pl.pallas_call(
        body, name="mla_proj_bwd", grid=(t // tm,),
        in_specs=[hq, hq, pl.BlockSpec((heads, tm, VHEAD), lambda i: (0, i, 0)),
                  pl.BlockSpec((tm, rq), lambda i: (i, 0)), pl.BlockSpec((tm, rkv), lambda i: (i, 0)),
                  pl.BlockSpec((tm, LANE), lambda i: (i, 0)), pl.BlockSpec((tm, 1), lambda i: (i, 0)),
                  pl.BlockSpec((1, LANE), c2), pl.BlockSpec((2, LANE), c2),
                  pl.BlockSpec((1, rq), c2), pl.BlockSpec((1, rkv), c2),
                  pl.BlockSpec((heads, HEADW, rq), c3), pl.BlockSpec((heads, HEADW, rkv), c3),
                  pl.BlockSpec((1, HEADW), c2), pl.BlockSpec((1, HEADW), c2)],
        out_specs=[pl.BlockSpec((tm, rq), lambda i: (i, 0)), pl.BlockSpec((tm, rkv), lambda i: (i, 0)),
                   pl.BlockSpec((tm, LANE), lambda i: (i, 0)),
                   pl.BlockSpec((heads, HEADW, rq), c3), pl.BlockSpec((heads, HEADW, rkv), c3),
                   pl.BlockSpec((1, rq), c2), pl.BlockSpec((1, rkv), c2),
                   pl.BlockSpec((1, HEADW), c2), pl.BlockSpec((1, HEADW), c2)],
        out_shape=[jax.ShapeDtypeStruct((t, rq), F32), jax.ShapeDtypeStruct((t, rkv), F32),
                   jax.ShapeDtypeStruct((t, LANE), F32),
                   jax.ShapeDtypeStruct((heads, HEADW, rq), F32), jax.ShapeDtypeStruct((heads, HEADW, rkv), F32),
                   jax.ShapeDtypeStruct((1, rq), F32), jax.ShapeDtypeStruct((1, rkv), F32),
                   jax.ShapeDtypeStruct((1, HEADW), F32), jax.ShapeDtypeStruct((1, HEADW), F32)],
        compiler_params=_params(("arbitrary",)),
    )(dq, dk, dv, cq, ckv, krw, pos, freq, masks, gq, gkv, wq_t, wkv_t, gqh, gkh)


def _lower_triangle(blk):
    return lax.broadcasted_iota(jnp.int32, (blk, blk), 0) >= lax.broadcasted_iota(jnp.int32, (blk, blk), 1)


def _attn_fwd(q, k, v, seq, blk):
    heads, t, _ = q.shape
    scale = QK ** -0.5
    nblk = seq // blk

    def body(q_ref, k_ref, v_ref, o_ref, lse_ref):
        tri = _lower_triangle(blk)
        for qi in range(nblk):
            rows = slice(qi * blk, (qi + 1) * blk)
            before = slice(0, qi * blk)
            qb = q_ref[0, rows, :]
            s_d = jnp.where(tri, _nt(qb, k_ref[0, rows, :]) * scale, -1e30)
            m = jnp.max(s_d, axis=-1, keepdims=True)
            if qi:
                s_b = _nt(qb, k_ref[0, before, :]) * scale
                m = jnp.maximum(m, jnp.max(s_b, axis=-1, keepdims=True))
                p_b = jnp.exp(s_b - m)
            p_d = jnp.exp(s_d - m)
            l = jnp.sum(p_d, axis=-1, keepdims=True)
            acc = _nn(p_d.astype(BF16), v_ref[0, rows, :])
            if qi:
                l += jnp.sum(p_b, axis=-1, keepdims=True)
                acc += _nn(p_b.astype(BF16), v_ref[0, before, :])
            o_ref[0, rows, :] = acc / l
            lse_ref[0, rows, :] = m + jnp.log(l)

    return pl.pallas_call(
        body, name="attn_fwd", grid=(heads, t // seq),
        in_specs=[pl.BlockSpec((1, seq, HEADW), lambda h, b: (h, b, 0)),
                  pl.BlockSpec((1, seq, HEADW), lambda h, b: (h, b, 0)),
                  pl.BlockSpec((1, seq, VHEAD), lambda h, b: (h, b, 0))],
        out_specs=[pl.BlockSpec((1, seq, VHEAD), lambda h, b: (h, b, 0)),
                   pl.BlockSpec((1, seq, 1), lambda h, b: (h, b, 0))],
        out_shape=[jax.ShapeDtypeStruct((heads, t, VHEAD), F32), jax.ShapeDtypeStruct((heads, t, 1), F32)],
        compiler_params=_params(("arbitrary", "arbitrary")),
    )(q, k, v)


def _attn_bwd(q, k, v, do, lse, delta, seq, blk, after):
    heads, t, _ = q.shape
    scale = QK ** -0.5
    nblk = seq // blk

    def body(q_ref, k_ref, v_ref, do_ref, lse_ref, dl_ref, _, dq_ref, dk_ref, dv_ref):
        tri = _lower_triangle(blk)
        dk_ref[...] = jnp.zeros_like(dk_ref)
        dv_ref[...] = jnp.zeros_like(dv_ref)
        for qi in range(nblk):
            rows = slice(qi * blk, (qi + 1) * blk)
            qb = q_ref[0, rows, :]
            dob = do_ref[0, rows, :]
            lse_b = lse_ref[0, rows, :]
            dl_b = dl_ref[0, rows, :]
            dq = jnp.zeros((blk, HEADW), F32)
            for keys, masked in ((slice(0, qi * blk), False), (rows, True)):
                if keys.stop == keys.start:
                    continue
                kb = k_ref[0, keys, :]
                p = jnp.exp(_nt(qb, kb) * scale - lse_b)
                if masked:
                    p = jnp.where(tri, p, 0.0)
                dp = _nt(dob, v_ref[0, keys, :])
                ds = (p * (dp - dl_b) * scale).astype(BF16)
                dv_ref[0, keys, :] += _tn(p.astype(BF16), dob)
                dk_ref[0, keys, :] += _tn(ds, qb)
                dq += _nn(ds, kb)
            dq_ref[0, rows, :] = dq

    hq = pl.BlockSpec((1, seq, HEADW), lambda h, b: (h, b, 0))
    hv = pl.BlockSpec((1, seq, VHEAD), lambda h, b: (h, b, 0))
    h1 = pl.BlockSpec((1, seq, 1), lambda h, b: (h, b, 0))
    return pl.pallas_call(
        body, name="attn_bwd", grid=(heads, t // seq),
        in_specs=[hq, hq, hv, hv, h1, h1, ANY],
        out_specs=[hq, hq, hv],
        out_shape=[jax.ShapeDtypeStruct((heads, t, HEADW), F32), jax.ShapeDtypeStruct((heads, t, HEADW), F32),
                   jax.ShapeDtypeStruct((heads, t, VHEAD), F32)],
        compiler_params=_params(("arbitrary", "arbitrary")),
    )(q, k, v, do, lse, delta, after)


def _out_fwd(ya, o, gb, w_out, x1, tm):
    t, w = ya.shape
    heads = o.shape[0]
    d = x1.shape[1]

    def body(ya_ref, o_ref, gb_ref, w_ref, x1_ref, x2_ref, yc_ref):
        yc_ref[:, 0:w] = ya_ref[...]
        for h in range(heads):
            lanes = slice(h * VHEAD, (h + 1) * VHEAD)
            yc_ref[:, w + h * VHEAD:w + (h + 1) * VHEAD] = _rms_fwd(o_ref[h], gb_ref[:, lanes], VHEAD).astype(BF16)
        x2_ref[...] = x1_ref[...] + _nn(yc_ref[...], w_ref[...])

    wy = w + heads * VHEAD
    row = pl.BlockSpec((tm, d), lambda i: (i, 0))
    return pl.pallas_call(
        body, name="out_fwd", grid=(t // tm,),
        in_specs=[pl.BlockSpec((tm, w), lambda i: (i, 0)), pl.BlockSpec((heads, tm, VHEAD), lambda i: (0, i, 0)),
                  pl.BlockSpec((1, heads * VHEAD), lambda i: (0, 0)), WHOLE_VMEM, row],
        out_specs=[row, pl.BlockSpec((tm, wy), lambda i: (i, 0))],
        out_shape=[jax.ShapeDtypeStruct((t, d), F32), jax.ShapeDtypeStruct((t, wy), BF16)],
        compiler_params=_params(("arbitrary",)),
    )(ya, o, gb, w_out, x1)


def _out_bwd(dx2, o, gb, w_out, w, tm):
    t, d = dx2.shape
    heads = o.shape[0]

    def body(dx_ref, o_ref, gb_ref, w_ref, dya_ref, do_ref, dl_ref, dgb_ref):
        @pl.when(pl.program_id(0) == 0)
        def _():
            dgb_ref[...] = jnp.zeros_like(dgb_ref)

        dyc = _nt(dx_ref[...].astype(BF16), w_ref[...])
        dya_ref[...] = dyc[:, 0:w]
        for h in range(heads):
            lanes = slice(h * VHEAD, (h + 1) * VHEAD)
            oh = o_ref[h]
            doh, dg = _rms_bwd(oh, gb_ref[:, lanes], dyc[:, w + h * VHEAD:w + (h + 1) * VHEAD], VHEAD)
            dgb_ref[:, lanes] += dg
            do_ref[h] = doh.astype(BF16)
            dl_ref[h] = jnp.sum(doh * oh, axis=-1, keepdims=True)

    ho = pl.BlockSpec((heads, tm, VHEAD), lambda i: (0, i, 0))
    vec = pl.BlockSpec((1, heads * VHEAD), lambda i: (0, 0))
    return pl.pallas_call(
        body, name="out_bwd", grid=(t // tm,),
        in_specs=[pl.BlockSpec((tm, d), lambda i: (i, 0)), ho, vec, WHOLE_VMEM],
        out_specs=[pl.BlockSpec((tm, w), lambda i: (i, 0)), ho, pl.BlockSpec((heads, tm, 1), lambda i: (0, i, 0)), vec],
        out_shape=[jax.ShapeDtypeStruct((t, w), F32), jax.ShapeDtypeStruct((heads, t, VHEAD), BF16),
                   jax.ShapeDtypeStruct((heads, t, 1), F32), jax.ShapeDtypeStruct((1, heads * VHEAD), F32)],
        compiler_params=_params(("arbitrary",)),
    )(dx2, o, gb, w_out)


def _loss_head(y, target, tm):
    t, d = y.shape

    def body(y_ref, t_ref, dy_ref, loss_ref):
        @pl.when(pl.program_id(0) == 0)
        def _():
            loss_ref[...] = jnp.zeros_like(loss_ref)

        err = y_ref[...] - t_ref[...]
        dy_ref[...] = err * (1.0 / d)
        part = jnp.sum(jnp.sum(err * err, axis=-1, keepdims=True) * (1.0 / d), axis=0, keepdims=True)
        loss_ref[...] += 0.5 * part

    row = pl.BlockSpec((tm, d), lambda i: (i, 0))
    return pl.pallas_call(
        body, name="loss_head", grid=(t // tm,),
        in_specs=[row, row], out_specs=[row, pl.BlockSpec((1, 1), lambda i: (0, 0))],
        out_shape=[jax.ShapeDtypeStruct((t, d), F32), jax.ShapeDtypeStruct((1, 1), F32)],
        compiler_params=_params(("arbitrary",)),
    )(y, target)


def _place():
    return lax.axis_index("x"), lax.axis_index("y"), lax.axis_index("c")


HBM = pl.BlockSpec(memory_space=pltpu.HBM)
SEM = pl.BlockSpec(memory_space=pltpu.SEMAPHORE)
DATAFLOW = pltpu.SideEffectType.DATAFLOW_SIDE_EFFECTING


def _plan_copies(plan, refs, send_sems, recv_sems):
    def block(ref, blk):
        if blk is None:
            return ref
        return ref.at[blk[0], pl.ds(0, blk[1])] if isinstance(blk, tuple) else ref.at[blk]

    cps = []
    for i, (sb, sblk, db, dblk, dev) in enumerate(plan(*_place())):
        cps.append(pltpu.make_async_remote_copy(
            src_ref=block(refs[sb], sblk), dst_ref=block(refs[db], dblk),
            send_sem=send_sems.at[i], recv_sem=recv_sems.at[i], device_id=dev, device_id_type=MESH))
    return cps


def _push_start(bufs, plan, ncopy, name, deps=()):
    nb = len(bufs)

    def body(*refs):
        outs = refs[nb + len(deps):]
        for cp in _plan_copies(plan, refs[:nb], outs[0], outs[1]):
            cp.start()
        outs[-1][...] = jnp.zeros_like(outs[-1])

    res = pl.pallas_call(
        body, name=name,
        out_shape=(pltpu.SemaphoreType.DMA((ncopy,)), pltpu.SemaphoreType.DMA((ncopy,)),
                   *[pltpu.HBM(b.shape, b.dtype) for b in bufs], jax.ShapeDtypeStruct((SUBLANE, LANE), F32)),
        in_specs=[HBM] * nb + [ANY] * len(deps),
        out_specs=(SEM, SEM, *[HBM] * nb, WHOLE_VMEM),
        input_output_aliases={i: 2 + i for i in range(nb)},
        compiler_params=pltpu.CompilerParams(has_side_effects=DATAFLOW),
    )(*[pltpu.with_memory_space_constraint(b, pltpu.HBM) for b in bufs], *deps)
    return res[0], res[1], list(res[2:2 + nb]), res[-1]


def _push_wait(send_sems, recv_sems, bufs, plan, after, name):
    nb = len(bufs)

    def body(*refs):
        for cp in _plan_copies(plan, refs[:nb], refs[nb], refs[nb + 1]):
            cp.wait_send()
            cp.wait_recv()

    res = pl.pallas_call(
        body, name=name,
        out_shape=[pltpu.HBM(b.shape, b.dtype) for b in bufs],
        in_specs=[HBM] * nb + [SEM, SEM, ANY], out_specs=[HBM] * nb,
        input_output_aliases={i: i for i in range(nb)},
        compiler_params=pltpu.CompilerParams(has_side_effects=DATAFLOW),
    )(*bufs, send_sems, recv_sems, after)
    return list(res)


def _other_chips(x, y):
    return ((1 - x, y), (x, 1 - y), (1 - x, 1 - y))


class _Exchange:
    def __init__(self, bufs, plan, ncopy, name, deps=()):
        self.plan, self.name = plan, name
        self.send, self.recv, self.bufs, self.token = _push_start(bufs, plan, ncopy, name + "_start", deps)

    def wait(self, after):
        return _push_wait(self.send, self.recv, self.bufs, self.plan, after, self.name + "_wait")


class _Chain:
    def __init__(self, bufs):
        self.bufs = list(bufs)

    def start(self, plan, ncopy, name, deps=()):
        send, recv, self.bufs, token = _push_start(self.bufs, plan, ncopy, name + "_start", deps)
        return (send, recv, plan, name), token

    def wait(self, pending, after):
        send, recv, plan, name = pending
        self.bufs = _push_wait(send, recv, self.bufs, plan, after, name + "_wait")


class _StagedGather:
    def __init__(self, shards, me, name, pad_to=None):
        self.n = n = len(shards)
        self.name = name
        rows = shards[0].shape[0]
        lands = []
        for s in shards:
            land = lax.empty((N_DEV, pad_to or rows) + s.shape[1:], s.dtype)
            if pad_to and pad_to != rows:
                land = lax.dynamic_update_slice(
                    land, jnp.zeros((N_DEV, pad_to - rows) + s.shape[1:], s.dtype), (0, rows, 0))
            lands.append(lax.dynamic_update_slice(land, s[None], (me, 0, 0)))
        self.chain = _Chain(list(shards) + lands)
        self.pending = {}

        def blk(b):
            return (b, rows) if pad_to and pad_to != rows else b

        def to_sibling(blocks):
            return lambda x, y, c: [(n + a, blk(b), n + a, blk(b), (x, y, 1 - c))
                                    for a in range(n) for b in blocks(x, y, c)]

        def nbr_blocks(x, y, c):
            return [4 * (1 - x) + 2 * y + c, 4 * x + 2 * (1 - y) + c]

        def diag(x, y, c):
            sx, sy = (1 - x) * (1 - c) + x * c, y * (1 - c) + (1 - y) * c
            tx, ty = x * (1 - c) + (1 - x) * c, (1 - y) * (1 - c) + y * c
            b = blk(4 * sx + 2 * sy + c)
            return [(n + a, b, n + a, b, (tx, ty, c)) for a in range(n)]

        self.plans = {
            "own": (lambda x, y, c: [(a, None, n + a, blk(4 * x + 2 * y + c), (x, y, 1 - c)) for a in range(n)], n),
            "nbr": (lambda x, y, c: [(a, None, n + a, blk(4 * x + 2 * y + c), dev) for a in range(n)
                                     for dev in ((1 - x, y, c), (x, 1 - y, c))], 2 * n),
            "diag": (diag, n),
            "nbr_d2d": (to_sibling(nbr_blocks), 2 * n),
            "own_nbr_d2d": (to_sibling(lambda x, y, c: [4 * x + 2 * y + c] + nbr_blocks(x, y, c)), 3 * n),
            "diag_d2d": (to_sibling(lambda x, y, c: [4 * (1 - x) + 2 * (1 - y) + c]), n),
        }

    def start(self, stage, deps=()):
        plan, ncopy = self.plans[stage]
        self.pending[stage], token = self.chain.start(plan, ncopy, self.name + "_" + stage, deps)
        return token

    def wait(self, stage, after):
        self.chain.wait(self.pending.pop(stage), after)

    def lands(self):
        return self.chain.bufs[self.n:]


def _gather_ici(shards, me, name, deps=()):
    n = len(shards)
    lands = [lax.dynamic_update_slice(lax.empty((N_DEV,) + s.shape, s.dtype), s[None], (me, 0, 0)) for s in shards]

    def plan(x, y, c):
        return [(a, None, n + a, 4 * x + 2 * y + c, (px, py, c)) for a in range(n) for px, py in _other_chips(x, y)]

    return _Exchange(list(shards) + lands, plan, 3 * n, name, deps)


def _gather_d2d(lands, name, deps=()):
    n = len(lands)

    def plan(x, y, c):
        blocks = [4 * x + 2 * y + c] + [4 * px + 2 * py + c for px, py in _other_chips(x, y)]
        return [(a, b, a, b, (x, y, 1 - c)) for a in range(n) for b in blocks]

    return _Exchange(list(lands), plan, 4 * n, name, deps)


def _reduce_d2d(grads, name, deps=(), rows=None):
    n = len(grads)
    lands = [lax.empty((4,) + g.shape[1:], g.dtype) for g in grads]

    def blk(b):
        return b if rows is None else (b, rows)

    def plan(x, y, c):
        return [(a, blk(2 * s + (1 - c)), n + a, blk(s), (x, y, 1 - c)) for a in range(n) for s in range(4)]

    return _Exchange(list(grads) + lands, plan, 4 * n, name, deps)


def _reduce_ici(chip, name, deps=(), rows=None):
    n = len(chip)
    lands = [lax.empty((3,) + g.shape[1:], g.dtype) for g in chip]

    def blk(b):
        return b if rows is None else (b, rows)

    def plan(x, y, c):
        return [(a, blk(2 * px + py), n + a, blk(k), (px, py, c))
                for a in range(n) for k, (px, py) in enumerate(_other_chips(x, y))]

    return _Exchange(list(chip) + lands, plan, 3 * n, name, deps)


def _pair_add(full, got, core, name, rows=None):
    _, r, cdim = full.shape
    br = _row_block(rows or r, 512)

    def body(c_ref, f_ref, g_ref, o_ref):
        o_ref[...] = (f_ref[...].astype(F32) + g_ref[...].astype(F32)).astype(o_ref.dtype)

    return pl.pallas_call(
        body, name=name,
        grid_spec=pltpu.PrefetchScalarGridSpec(
            num_scalar_prefetch=1, grid=(4, (rows or r) // br),
            in_specs=[pl.BlockSpec((1, br, cdim), lambda s, i, c_ref: (2 * s + c_ref[0], i, 0)),
                      pl.BlockSpec((1, br, cdim), lambda s, i, c_ref: (s, i, 0))],
            out_specs=pl.BlockSpec((1, br, cdim), lambda s, i, c_ref: (s, i, 0))),
        out_shape=jax.ShapeDtypeStruct((4, r, cdim), full.dtype),
        compiler_params=_params(("arbitrary", "arbitrary")),
    )(core, full, got)


def _sum_devices(stack):
    _, r, cdim = stack.shape

    def body(s_ref, o_ref):
        acc = s_ref[0]
        for k in range(1, N_DEV):
            acc = acc + s_ref[k]
        o_ref[...] = acc

    return pl.pallas_call(
        body, name="sum_devices", out_shape=jax.ShapeDtypeStruct((r, cdim), F32),
        compiler_params=_params(),
    )(stack)


def _adamw(w, g, m, v, name):
    r, cdim = w.shape
    br = _row_block(r, 256)

    def body(w_ref, g_ref, m_ref, v_ref, d_ref, nm_ref, nv_ref):
        g = g_ref[...]
        nm = ADAM_B1 * m_ref[...] + (1.0 - ADAM_B1) * g
        nv = ADAM_B2 * v_ref[...] + (1.0 - ADAM_B2) * (g * g)
        m_hat = nm / (1.0 - ADAM_B1 ** ADAM_STEP)
        v_hat = nv / (1.0 - ADAM_B2 ** ADAM_STEP)
        d_ref[...] = -ADAM_LR * (m_hat / (jnp.sqrt(v_hat) + ADAM_EPS) + ADAM_WD * w_ref[...])
        nm_ref[...] = nm
        nv_ref[...] = nv

    spec = pl.BlockSpec((br, cdim), lambda i: (i, 0))
    shape = jax.ShapeDtypeStruct((r, cdim), F32)
    return pl.pallas_call(
        body, name=name, grid=(r // br,), in_specs=[spec] * 4, out_specs=[spec] * 3,
        out_shape=[shape] * 3, compiler_params=_params(("arbitrary",)),
    )(w, g, m, v)


def _sum_adamw(chip, got, slot, w, m, v, name, after):
    rows, cdim = w.shape
    bc = 2 * LANE if cdim % (2 * LANE) == 0 else cdim

    def body(s_ref, c_ref, g_ref, w_ref, m_ref, v_ref, _, go_ref, d_ref, nm_ref, nv_ref):
        g = c_ref[0].astype(F32)
        for k in range(3):
            g = g + g_ref[k].astype(F32)
        nm = ADAM_B1 * m_ref[...] + (1.0 - ADAM_B1) * g
        nv = ADAM_B2 * v_ref[...] + (1.0 - ADAM_B2) * (g * g)
        m_hat = nm / (1.0 - ADAM_B1 ** ADAM_STEP)
        v_hat = nv / (1.0 - ADAM_B2 ** ADAM_STEP)
        go_ref[...] = g
        d_ref[...] = -ADAM_LR * (m_hat / (jnp.sqrt(v_hat) + ADAM_EPS) + ADAM_WD * w_ref[...])
        nm_ref[...] = nm
        nv_ref[...] = nv

    spec = pl.BlockSpec((rows, bc), lambda j, s_ref: (0, j))
    shape = jax.ShapeDtypeStruct((rows, cdim), F32)
    return pl.pallas_call(
        body, name=name,
        grid_spec=pltpu.PrefetchScalarGridSpec(
            num_scalar_prefetch=1, grid=(cdim // bc,),
            in_specs=[pl.BlockSpec((1, rows, bc), lambda j, s_ref: (s_ref[0], 0, j)),
                      pl.BlockSpec((3, rows, bc), lambda j, s_ref: (0, 0, j)), spec, spec, spec, ANY],
            out_specs=[spec] * 4),
        out_shape=[shape] * 4,
        compiler_params=_params(("arbitrary",)),
    )(slot, chip, got, w, m, v, after)


WEIGHTS = ("ffn1_norm_g", "ffn1_w_gate", "ffn1_w_up", "ffn1_w_down", "mix_norm_g", "w_in", "gmlp_v_norm_g",
           "gmlp_w_s", "gmlp_b_s", "mla_q_norm_g", "mla_w_q_up", "mla_kv_norm_g", "mla_w_kv_up", "mla_q_head_g",
           "mla_k_head_g", "gmlp_out_g", "mla_out_g", "w_out", "ffn2_norm_g", "ffn2_w_gate", "ffn2_w_up",
           "ffn2_w_down")
SHARDED = {"ffn1_w_gate": True, "ffn1_w_up": True, "ffn1_w_down": False, "w_in": True, "mla_w_q_up": True,
           "mla_w_kv_up": True, "w_out": False, "ffn2_w_gate": True, "ffn2_w_up": True, "ffn2_w_down": False}


def _col_block(m, target):
    best = LANE
    for cand in range(LANE, min(m, target) + 1, LANE):
        if m % cand == 0:
            best = cand
    return best


def _shard_rows(w, transposed, pad_to=None):
    rows = (w[0].T if transposed else w[0]).astype(BF16)
    if pad_to is not None and pad_to != rows.shape[0]:
        rows = jnp.pad(rows, ((0, pad_to - rows.shape[0]), (0, 0)))
    return rows


def _pack(parts):
    flat = []
    for p in parts:
        f = p.reshape(-1).astype(F32)
        flat.append(jnp.pad(f, (0, _round_up(f.size, LANE) - f.size)))
    flat = jnp.concatenate(flat)
    rows = _round_up(flat.size // LANE, SUBLANE)
    return jnp.pad(flat, (0, rows * LANE - flat.size)).reshape(rows, LANE)


def _unpack(packed, shapes):
    out, row = [], 0
    for shp in shapes:
        size = 1
        for s in shp:
            size *= s
        nrows = _round_up(size, LANE) // LANE
        out.append(packed[row:row + nrows].reshape(-1)[:size].reshape(shp))
        row += nrows
    return out


def kernel(x, positions, ffn1_norm_g, ffn1_w_gate, ffn1_w_up, ffn1_w_down, mix_norm_g, w_in, gmlp_v_norm_g, gmlp_w_s, gmlp_b_s, mla_q_norm_g, mla_w_q_up, mla_kv_norm_g, mla_w_kv_up, mla_q_head_g, mla_k_head_g, gmlp_out_g, mla_out_g, w_out, ffn2_norm_g, ffn2_w_gate, ffn2_w_up, ffn2_w_down, loss_target, m_ffn1_norm_g, m_ffn1_w_gate, m_ffn1_w_up, m_ffn1_w_down, m_mix_norm_g, m_w_in, m_gmlp_v_norm_g, m_gmlp_w_s, m_gmlp_b_s, m_mla_q_norm_g, m_mla_w_q_up, m_mla_kv_norm_g, m_mla_w_kv_up, m_mla_q_head_g, m_mla_k_head_g, m_gmlp_out_g, m_mla_out_g, m_w_out, m_ffn2_norm_g, m_ffn2_w_gate, m_ffn2_w_up, m_ffn2_w_down, v_ffn1_norm_g, v_ffn1_w_gate, v_ffn1_w_up, v_ffn1_w_down, v_mix_norm_g, v_w_in, v_gmlp_v_norm_g, v_gmlp_w_s, v_gmlp_b_s, v_mla_q_norm_g, v_mla_w_q_up, v_mla_kv_norm_g, v_mla_w_kv_up, v_mla_q_head_g, v_mla_k_head_g, v_gmlp_out_g, v_mla_out_g, v_w_out, v_ffn2_norm_g, v_ffn2_w_gate, v_ffn2_w_up, v_ffn2_w_down):
    wts = dict(zip(WEIGHTS, (ffn1_norm_g, ffn1_w_gate, ffn1_w_up, ffn1_w_down, mix_norm_g, w_in, gmlp_v_norm_g, gmlp_w_s, gmlp_b_s, mla_q_norm_g, mla_w_q_up, mla_kv_norm_g, mla_w_kv_up, mla_q_head_g, mla_k_head_g, gmlp_out_g, mla_out_g, w_out, ffn2_norm_g, ffn2_w_gate, ffn2_w_up, ffn2_w_down)))
    mom1 = dict(zip(WEIGHTS, (m_ffn1_norm_g, m_ffn1_w_gate, m_ffn1_w_up, m_ffn1_w_down, m_mix_norm_g, m_w_in, m_gmlp_v_norm_g, m_gmlp_w_s, m_gmlp_b_s, m_mla_q_norm_g, m_mla_w_q_up, m_mla_kv_norm_g, m_mla_w_kv_up, m_mla_q_head_g, m_mla_k_head_g, m_gmlp_out_g, m_mla_out_g, m_w_out, m_ffn2_norm_g, m_ffn2_w_gate, m_ffn2_w_up, m_ffn2_w_down)))
    mom2 = dict(zip(WEIGHTS, (v_ffn1_norm_g, v_ffn1_w_gate, v_ffn1_w_up, v_ffn1_w_down, v_mix_norm_g, v_w_in, v_gmlp_v_norm_g, v_gmlp_w_s, v_gmlp_b_s, v_mla_q_norm_g, v_mla_w_q_up, v_mla_kv_norm_g, v_mla_w_kv_up, v_mla_q_head_g, v_mla_k_head_g, v_gmlp_out_g, v_mla_out_g, v_w_out, v_ffn2_norm_g, v_ffn2_w_gate, v_ffn2_w_up, v_ffn2_w_down)))

    b_loc, seq, d = x.shape
    t = b_loc * seq
    ffs = ffn1_w_gate.shape[2]
    fp = _round_up(ffs, LANE)
    wg = gmlp_v_norm_g.shape[1]
    groups = gmlp_w_s.shape[1]
    rq, rkv = mla_q_norm_g.shape[1], mla_kv_norm_g.shape[1]
    heads = mla_out_g.shape[1]
    assert w_in.shape[2] * N_DEV == 2 * wg + rq + rkv + ROPE and mla_w_kv_up.shape[2] * N_DEV == heads * HEADW
    tm = min(512, t)
    tm_mix = min(256, t)
    blk = min(256, seq)

    xf = x.reshape(t, d)
    target = loss_target.reshape(t, d)
    pos = positions.reshape(t, 1).astype(F32)
    half = ROPE // 2
    inv_freq = 1.0 / (ROPE_THETA ** (jnp.arange(half, dtype=F32) / half))
    freq = jnp.concatenate([inv_freq, inv_freq, jnp.zeros((LANE - ROPE,), F32)])[None, :]
    lane = jnp.arange(LANE)
    masks = jnp.stack([jnp.where(lane < half, -1.0, 0.0),
                       jnp.where((lane >= half) & (lane < ROPE), 1.0, 0.0)]).astype(F32)
    gqh = jnp.pad(mla_q_head_g, ((0, 0), (0, HEADW - QK)))
    gkh = jnp.pad(mla_k_head_g, ((0, 0), (0, HEADW - QK)))
    bias = jnp.repeat(gmlp_b_s[0].T, CHUNK, axis=1)
    gouta = gmlp_out_g.reshape(1, wg)
    goutb = mla_out_g.reshape(1, heads * VHEAD)
    ws = gmlp_w_s[0]

    px, py, pc = _place()
    me = 4 * px + 2 * py + pc
    core = pc.astype(jnp.int32).reshape(1)
    slot = (2 * px + py).astype(jnp.int32).reshape(1)
    order = [n for n in WEIGHTS if n in SHARDED]
    group = {"ffn1": [n for n in order if n.startswith("ffn1")], "ffn2": [n for n in order if n.startswith("ffn2")],
             "mix": [n for n in order if not n.startswith("ffn")]}
    shard = {n: _shard_rows(wts[n], SHARDED[n]) for n in group["ffn1"]}
    frows = ffs if ffs != fp else None

    def tied(arr, token):
        return arr + token[0, 0].astype(arr.dtype)

    xnb, ynb, dgn = 4 * (1 - px) + 2 * py, 4 * px + 2 * (1 - py), 4 * (1 - px) + 2 * (1 - py)
    ids_a = jnp.stack([me, 4 * px + 2 * py + (1 - pc)]).astype(jnp.int32)
    ids_b = jnp.stack([xnb, xnb + 1, ynb, ynb + 1]).astype(jnp.int32)
    ids_c = jnp.stack([dgn, dgn + 1]).astype(jnp.int32)
    g1 = _StagedGather([shard[n] for n in group["ffn1"]], me, "gather_ffn1", pad_to=fp)
    token = g1.start("own")
    token = g1.start("nbr", deps=(token,))
    for n in group["mix"] + group["ffn2"]:
        shard[n] = _shard_rows(tied(wts[n], token), SHARDED[n])
    g3 = _StagedGather([shard[n] for n in group["ffn2"]], me, "gather_ffn2", pad_to=fp)
    g1.wait("own", token)
    x1, xn1, kept1 = _ffn_fwd(xf, None, ffn1_norm_g, ids_a, *g1.lands(), None, tm, "ffn1_fwd_a")
    g1.wait("nbr", x1)
    token = g1.start("diag")
    ici2 = _gather_ici([shard[n] for n in group["mix"]], me, "gather_mix_ici", deps=(token,))
    token = g3.start("nbr", deps=(ici2.token,))
    token = g1.start("nbr_d2d", deps=(token,))
    g1.wait("nbr_d2d", token)
    x1, xn1, kept1 = _ffn_fwd(x1, xn1, None, ids_b, *g1.lands(), kept1, tm, "ffn1_fwd_b")
    g1.wait("diag", x1)
    token = g1.start("diag_d2d")
    d2d2 = _gather_d2d(ici2.wait(x1)[len(group["mix"]):], "gather_mix_d2d", deps=(token,))
    g1.wait("diag_d2d", d2d2.token)
    full = dict(zip(group["ffn1"], g1.lands()))
    x1, xn1, (gd1, sl1, h1) = _ffn_fwd(x1, xn1, None, ids_c, full["ffn1_w_gate"], full["ffn1_w_up"],
                                       full["ffn1_w_down"], kept1, tm, "ffn1_fwd_c")
    full.update(zip(group["mix"], d2d2.wait(x1)))
    win_t = full["w_in"].reshape(-1, d)
    splits = (2 * wg, rq, rkv, LANE)
    wq_t = jnp.pad(full["mla_w_q_up"].reshape(heads, QK, rq), ((0, 0), (0, HEADW - QK), (0, 0)))
    wkv_t = full["mla_w_kv_up"].reshape(heads, HEADW, rkv)
    wout = full["w_out"].reshape(-1, d)
    hn, zuv, cq, ckv, krw = _inproj_fwd(x1, mix_norm_g, win_t, splits, tm)
    ya = _gmlp_fwd(zuv, gmlp_v_norm_g, ws, bias, gouta, tm_mix)
    g3.wait("nbr", ya)
    token = g3.start("diag")
    token = g3.start("own_nbr_d2d", deps=(token,))
    q, k, vv = _mla_proj_fwd(cq, ckv, krw, pos, freq, masks, mla_q_norm_g, mla_kv_norm_g, wq_t, wkv_t,
                             tied(gqh, token), gkh, tm)
    o, lse = _attn_fwd(q, k, vv, seq, blk)
    g3.wait("diag", o)
    token = g3.start("diag_d2d")
    x2, ycat = _out_fwd(ya, o, tied(goutb, token), wout, x1, tm)
    g3.wait("own_nbr_d2d", x2)
    g3.wait("diag_d2d", x2)
    full.update(zip(group["ffn2"], g3.lands()))
    x3, xn2, (gd2, sl2, h2) = _ffn_fwd(x2, None, ffn2_norm_g, jnp.arange(N_DEV, dtype=jnp.int32),
                                       full["ffn2_w_gate"], full["ffn2_w_up"], full["ffn2_w_down"], None, tm,
                                       "ffn2_fwd")
    dx3, loss_part = _loss_head(x3, target, tm)

    outs_g, outs_d, outs_m, outs_v = {}, {}, {}, {}

    def finish(names, chip, got, after):
        for n, cp, gt in zip(names, chip, got):
            rows_of = (lambda a: a[0].T) if SHARDED[n] else (lambda a: a[0])
            res = _sum_adamw(cp, gt, slot, rows_of(wts[n]), rows_of(mom1[n]), rows_of(mom2[n]), "adamw_" + n, after)
            outs_g[n], outs_d[n], outs_m[n], outs_v[n] = [r.T[None] if SHARDED[n] else r[None] for r in res]
            after = res[3]
        return after

    def chip_sums(names, ex, after, rows=None):
        res = ex.wait(after)
        return [_pair_add(f, gt, core, "pair_add_" + n, rows)
                for n, f, gt in zip(names, res[:len(names)], res[len(names):])]

    tk = min(1024, t)
    grads = {}
    small = {}
    dx2, small["ffn2_norm_g"], da2, db2 = _ffn_bwd(
        dx3, x2, ffn2_norm_g, gd2, sl2, full["ffn2_w_gate"], full["ffn2_w_up"], full["ffn2_w_down"], tm, "ffn2_bwd")
    grads["ffn2_w_gate"] = _matmul_tn(da2, xn2, fp, d, tk, BF16, "dw_ffn2_gate").reshape(N_DEV, fp, d)
    grads["ffn2_w_up"] = _matmul_tn(db2, xn2, fp, d, tk, BF16, "dw_ffn2_up").reshape(N_DEV, fp, d)
    grads["ffn2_w_down"] = _matmul_tn(h2, dx3, fp, d, tk, BF16, "dw_ffn2_down", rhs_scale=0.5).reshape(
        N_DEV, fp, d)
    red_a2 = _reduce_d2d([grads[n] for n in group["ffn2"]], "reduce_ffn2_d2d", rows=frows)
    dya, do, delta, small["mla_out_g"] = _out_bwd(dx2, o, tied(goutb, red_a2.token), wout, wg, tm)
    grads["w_out"] = _matmul_tn(ycat, dx2, _col_block(ycat.shape[1], 768), d, tk, BF16, "dw_out").reshape(
        N_DEV, -1, d)
    chip2 = chip_sums(group["ffn2"], red_a2, dya, frows)
    red_b2 = _reduce_ici(chip2, "reduce_ffn2_ici", rows=frows)
    dq, dk, dv = _attn_bwd(q, k, vv, do, lse, delta, seq, blk, red_b2.token)
    (dcq, dckv, dkrw, dwq, dwkv, small["mla_q_norm_g"], small["mla_kv_norm_g"], dgqh, dgkh) = _mla_proj_bwd(
        dq, dk, dv, cq, ckv, krw, pos, freq, masks, mla_q_norm_g, mla_kv_norm_g, wq_t, wkv_t, gqh, gkh, tm)
    small["mla_q_head_g"], small["mla_k_head_g"] = dgqh[:, :QK], dgkh[:, :QK]
    grads["mla_w_q_up"] = dwq[:, :QK].astype(BF16).reshape(N_DEV, -1, rq)
    grads["mla_w_kv_up"] = dwkv.astype(BF16).reshape(N_DEV, -1, rkv)
    dzuv, small["gmlp_w_s"], dbs, small["gmlp_v_norm_g"], small["gmlp_out_g"] = _gmlp_bwd(
        dya, zuv, gmlp_v_norm_g, ws, bias, gouta, tm_mix)
    small["gmlp_b_s"] = dbs[:, :, 0]
    dx1, small["mix_norm_g"], dzc = _inproj_bwd([dzuv, dcq, dckv, dkrw], x1, mix_norm_g, win_t, dx2, splits,
                                                tm_mix)
    grads["w_in"] = _matmul_tn(dzc, hn, _col_block(dzc.shape[1], 768), d, tk, BF16, "dw_in",
                               out_rows=win_t.shape[0]).reshape(N_DEV, -1, d)
    res_b2 = red_b2.wait(grads["w_in"])
    red_am = _reduce_d2d([grads[n] for n in group["mix"]], "reduce_mix_d2d")

    def ffn1_dw(n, lhs, rhs, scale, token):
        return _matmul_tn(lhs, rhs, fp, d, tk, BF16, "dw_" + n, rhs_scale=scale, deps=(token,)).reshape(N_DEV, fp, d)

    gr = ffn1_dw("ffn1_w_down", h1, dx1, 0.5, red_am.token)
    red_ad = _reduce_d2d([gr], "reduce_ffn1_w_down_d2d", deps=(red_am.token,), rows=frows)
    dx0, small["ffn1_norm_g"], da1, db1 = _ffn_bwd(
        dx1, xf, tied(ffn1_norm_g, red_ad.token), gd1, sl1, full["ffn1_w_gate"], full["ffn1_w_up"],
        full["ffn1_w_down"], tm, "ffn1_bwd")
    chipm = chip_sums(group["mix"], red_am, dx0)
    red_bm = _reduce_ici(chipm, "reduce_mix_ici")
    red_b = [("ffn1_w_down", _reduce_ici(chip_sums(["ffn1_w_down"], red_ad, dx0, frows), "reduce_ffn1_w_down_ici",
                                         deps=(red_bm.token,), rows=frows))]
    rep = [n for n in WEIGHTS if n not in SHARDED]
    small_ici = _gather_ici([_pack([small[n] for n in rep] + [loss_part])], me, "gather_small_ici",
                            deps=(red_b[-1][1].token,))
    gr = ffn1_dw("ffn1_w_gate", da1, xn1, None, small_ici.token)
    red_ag = _reduce_d2d([gr], "reduce_ffn1_w_gate_d2d", rows=frows)
    n2 = group["ffn2"]
    after = finish(n2[0:1], res_b2[0:1], res_b2[3:4], red_ag.token)
    red_b.append(("ffn1_w_gate", _reduce_ici(chip_sums(["ffn1_w_gate"], red_ag, after, frows),
                                             "reduce_ffn1_w_gate_ici", rows=frows)))
    gr = ffn1_dw("ffn1_w_up", db1, xn1, None, red_b[-1][1].token)
    small_d2d = _gather_d2d(small_ici.wait(gr)[1:], "gather_small_d2d")
    red_au = _reduce_d2d([gr], "reduce_ffn1_w_up_d2d", deps=(small_d2d.token,), rows=frows)
    after = finish(n2[1:2], res_b2[1:2], res_b2[4:5], red_au.token)
    red_b.append(("ffn1_w_up", _reduce_ici(chip_sums(["ffn1_w_up"], red_au, after, frows),
                                           "reduce_ffn1_w_up_ici", rows=frows)))
    after = finish(n2[2:3], res_b2[2:3], res_b2[5:6], red_b[-1][1].token)
    res = red_bm.wait(after)
    nm_ = len(group["mix"])
    after = finish(group["mix"], res[:nm_], res[nm_:], after)
    total = _sum_devices(small_d2d.wait(after)[0])
    zero = jnp.zeros((1,), F32)
    dlt, nm, nv = _adamw(_pack([wts[n] for n in rep] + [zero]), total, _pack([mom1[n] for n in rep] + [zero]),
                         _pack([mom2[n] for n in rep] + [zero]), "adamw_small")
    shapes = [wts[n].shape for n in rep] + [(1,)]
    for n, g, dl, m1, m2 in zip(rep, _unpack(total, shapes), _unpack(dlt, shapes), _unpack(nm, shapes),
                                _unpack(nv, shapes)):
        outs_g[n], outs_d[n], outs_m[n], outs_v[n] = g, dl, m1, m2
    loss = _unpack(total, shapes)[-1].reshape(())
    after = dlt
    for n, ex in red_b:
        res = ex.wait(after)
        after = finish([n], res[:1], res[1:], after)

    return (loss, dx0.reshape(b_loc, seq, d), *[outs_g[n] for n in WEIGHTS], *[outs_d[n] for n in WEIGHTS],
            *[outs_m[n] for n in WEIGHTS], *[outs_v[n] for n in WEIGHTS])
```

```python
import functools

import jax
import jax.numpy as jnp
from jax import lax
from jax.experimental import pallas as pl
from jax.experimental.pallas import tpu as pltpu

F32 = jnp.float32
BF16 = jnp.bfloat16
EPS = 1e-6
LANE = 128
SUBLANE = 8
N_DEV = 8
VMEM_LIMIT = 60 * 1024 * 1024
NOPE = 128
ROPE = 64
VHEAD = 128
QK = NOPE + ROPE
HEADW = 2 * LANE
CHUNK = 128
ROPE_THETA = 10000.0
ADAM_LR, ADAM_B1, ADAM_B2, ADAM_EPS, ADAM_WD, ADAM_STEP = 0.001, 0.9, 0.999, 1e-08, 0.01, 10
MESH = pl.DeviceIdType.MESH
ANY = pl.BlockSpec(memory_space=pl.ANY)
WHOLE_VMEM = pl.BlockSpec(memory_space=pltpu.VMEM)


def _params(sem=None):
    return pltpu.CompilerParams(dimension_semantics=sem, vmem_limit_bytes=VMEM_LIMIT)


def _round_up(n, m):
    return -(-n // m) * m


def _row_block(rows, target):
    best = rows
    for cand in range(SUBLANE, min(rows, target) + 1, SUBLANE):
        if rows % cand == 0:
            best = cand
    return best if best <= target else rows


def _nn(a, b):
    return jnp.dot(a, b, preferred_element_type=F32)


def _nt(a, b):
    return lax.dot_general(a, b, (((1,), (1,)), ((), ())), preferred_element_type=F32)


def _tn(a, b):
    return lax.dot_general(a, b, (((0,), (0,)), ((), ())), preferred_element_type=F32)


def _rstd(x, n):
    return lax.rsqrt(jnp.sum(x * x, axis=-1, keepdims=True) * (1.0 / n) + EPS)


def _rms_fwd(x, g, n):
    return x * _rstd(x, n) * g


def _rms_bwd(x, g, dy, n):
    r = _rstd(x, n)
    xh = x * r
    dyg = dy * g
    dx = r * (dyg - xh * (jnp.sum(dyg * xh, axis=-1, keepdims=True) * (1.0 / n)))
    return dx, jnp.sum(dy * xh, axis=0, keepdims=True)


def _gelu(x):
    return 0.5 * x * (1.0 + lax.erf(x * 0.7071067811865476))


def _gelu_grad(x):
    return 0.5 * (1.0 + lax.erf(x * 0.7071067811865476)) + x * jnp.exp(-0.5 * x * x) * 0.3989422804014327


def _ffn_fwd(base, xn, g, ids, wg_t, wu_t, wd, saved, tm, name):
    t, d = base.shape
    nb, fp, _ = wg_t.shape
    n = ids.shape[0]
    first = xn is None
    if saved is None:
        saved = [lax.empty((t, nb * fp), BF16) for _ in range(3)]

    def body(ids_ref, *refs):
        if first:
            base_ref, g_ref, wg_ref, wu_ref, wd_ref, _, _, _, out_ref, xn_ref, gd_ref, sl_ref, h_ref, acc = refs
        else:
            base_ref, xn_ref, wg_ref, wu_ref, wd_ref, _, _, _, out_ref, gd_ref, sl_ref, h_ref, acc = refs
        j = pl.program_id(1)

        @pl.when(j == 0)
        def _():
            if first:
                xn_ref[...] = _rms_fwd(base_ref[...], g_ref[...], d).astype(BF16)
            acc[...] = jnp.zeros_like(acc)

        xnb = xn_ref[...]
        a = _nt(xnb, wg_ref[0])
        b = _nt(xnb, wu_ref[0])
        s = jax.nn.sigmoid(a)
        sl = a * s
        h = (sl * b).astype(BF16)
        gd_ref[...] = (b * (s * (1.0 + a * (1.0 - s)))).astype(BF16)
        sl_ref[...] = sl.astype(BF16)
        h_ref[...] = h
        acc[...] += _nn(h, wd_ref[0])

        @pl.when(j == n - 1)
        def _():
            out_ref[...] = base_ref[...] + 0.5 * acc[...]

    wspec = pl.BlockSpec((1, fp, d), lambda i, j, ids_ref: (ids_ref[j], 0, 0))
    row = pl.BlockSpec((tm, d), lambda i, j, ids_ref: (i, 0))
    ff = pl.BlockSpec((tm, fp), lambda i, j, ids_ref: (i, ids_ref[j]))
    ffs = jax.ShapeDtypeStruct((t, nb * fp), BF16)
    second = pl.BlockSpec((1, d), lambda i, j, ids_ref: (0, 0)) if first else row
    n_row_outs = 2 if first else 1
    res = pl.pallas_call(
        body, name=name,
        grid_spec=pltpu.PrefetchScalarGridSpec(
            num_scalar_prefetch=1, grid=(t // tm, n),
            in_specs=[row, second, wspec, wspec, wspec, ANY, ANY, ANY],
            out_specs=[row] * n_row_outs + [ff, ff, ff],
            scratch_shapes=[pltpu.VMEM((tm, d), F32)]),
        out_shape=[jax.ShapeDtypeStruct((t, d), F32)] + ([jax.ShapeDtypeStruct((t, d), BF16)] if first else [])
        + [ffs, ffs, ffs],
        input_output_aliases={6 + k: n_row_outs + k for k in range(3)},
        compiler_params=_params(("arbitrary", "arbitrary")),
    )(ids, base, g if first else xn, wg_t, wu_t, wd, *saved)
    return (res[0], res[1] if first else xn, list(res[n_row_outs:]))


def _ffn_bwd(dout, x, g, gd, sl, wg_t, wu_t, wd, tm, name):
    t, d = x.shape
    nb, fp, _ = wg_t.shape

    def body(do_hbm, x_hbm, g_ref, gd_ref, sl_ref, wg_ref, wu_ref, wd_ref, wd_next_ref,
             dx_hbm, dg_ref, da_ref, db_ref, acc, rowbuf, dy_scr, dh_scr, sem):
        i, j = pl.program_id(0), pl.program_id(1)
        rows = pl.ds(pl.multiple_of(i * tm, tm), tm)
        get_do = pltpu.make_async_copy(do_hbm.at[rows, :], rowbuf, sem)
        get_x = pltpu.make_async_copy(x_hbm.at[rows, :], rowbuf, sem)

        @pl.when(j == 0)
        def _():
            get_do.start()
            get_do.wait()
            dy_scr[...] = (0.5 * rowbuf[...]).astype(BF16)
            acc[...] = jnp.zeros_like(acc)
            dh_scr[0] = _nt(dy_scr[...], wd_ref[0])
            get_x.start()

        @pl.when((i == 0) & (j == 0))
        def _():
            dg_ref[...] = jnp.zeros_like(dg_ref)

        dh = dh_scr[j % 2]
        dh_scr[(j + 1) % 2] = _nt(dy_scr[...], wd_next_ref[0])
        da = (dh * gd_ref[...].astype(F32)).astype(BF16)
        db = (dh * sl_ref[...].astype(F32)).astype(BF16)
        da_ref[...] = da
        db_ref[...] = db
        acc[...] += _nn(da, wg_ref[0]) + _nn(db, wu_ref[0])

        @pl.when(j == nb - 1)
        def _():
            get_x.wait()
            dxn, dg = _rms_bwd(rowbuf[...], g_ref[...], acc[...], d)
            dg_ref[...] += dg
            acc[...] = dxn
            get_do.start()
            get_do.wait()
            acc[...] += rowbuf[...]
            out = pltpu.make_async_copy(acc, dx_hbm.at[rows, :], sem)
            out.start()
            out.wait()

    wspec = pl.BlockSpec((1, fp, d), lambda i, j: (j, 0, 0))
    wnext = pl.BlockSpec((1, fp, d), lambda i, j: (jnp.minimum(j + 1, nb - 1), 0, 0))
    vec = pl.BlockSpec((1, d), lambda i, j: (0, 0))
    ff = pl.BlockSpec((tm, fp), lambda i, j: (i, j))
    ffs = jax.ShapeDtypeStruct((t, nb * fp), BF16)
    return pl.pallas_call(
        body, name=name, grid=(t // tm, nb),
        in_specs=[ANY, ANY, vec, ff, ff, wspec, wspec, pl.BlockSpec((1, fp, d), lambda i, j: (0, 0, 0)), wnext],
        out_specs=[ANY, vec, ff, ff],
        out_shape=[jax.ShapeDtypeStruct((t, d), F32), jax.ShapeDtypeStruct((1, d), F32), ffs, ffs],
        scratch_shapes=[pltpu.VMEM((tm, d), F32), pltpu.VMEM((tm, d), F32), pltpu.VMEM((tm, d), BF16),
                        pltpu.VMEM((2, tm, fp), F32), pltpu.SemaphoreType.DMA],
        compiler_params=_params(("arbitrary", "arbitrary")),
    )(dout, x, g, gd, sl, wg_t, wu_t, wd, wd)


def _matmul_tn(lhs, rhs, bm, bn, tk, out_dtype, name, rhs_scale=None, deps=(), out_rows=None):
    t, m = lhs.shape
    n = rhs.shape[1]
    nk = t // tk
    out_rows = m if out_rows is None else out_rows

    def body(l_ref, r_ref, *refs):
        o_ref, acc = refs[len(deps):]
        k = pl.program_id(2)

        @pl.when(k == 0)
        def _():
            acc[...] = jnp.zeros_like(acc)

        r = r_ref[...] if rhs_scale is None else rhs_scale * r_ref[...]
        acc[...] += _tn(l_ref[...].astype(BF16), r.astype(BF16))

        @pl.when(k == nk - 1)
        def _():
            o_ref[...] = acc[...].astype(out_dtype)

    return pl.pallas_call(
        body, name=name, grid=(m // bm, n // bn, nk),
        in_specs=[pl.BlockSpec((tk, bm), lambda i, j, k: (k, i)), pl.BlockSpec((tk, bn), lambda i, j, k: (k, j))]
        + [ANY] * len(deps),
        out_specs=pl.BlockSpec((bm, bn), lambda i, j, k: (i, j)),
        out_shape=jax.ShapeDtypeStruct((out_rows, n), out_dtype),
        scratch_shapes=[pltpu.VMEM((bm, bn), F32)],
        compiler_params=_params(("arbitrary", "arbitrary", "arbitrary")),
    )(lhs, rhs, *deps)


def _last_rows_padded(w_ref, tail_ref, off, real):
    @pl.when(pl.program_id(0) == 0)
    def _():
        tail_ref[...] = jnp.zeros_like(tail_ref)
        tail_ref[0:real, :] = w_ref[off:off + real, :]


def _inproj_fwd(x, g, w_t, splits, tm):
    t, d = x.shape
    offs = [sum(splits[:k]) for k in range(len(splits))]
    real_last = w_t.shape[0] - offs[-1]

    def body(x_ref, g_ref, w_ref, hn_ref, *refs):
        z_refs, tail_ref = refs[:-1], refs[-1]
        _last_rows_padded(w_ref, tail_ref, offs[-1], real_last)
        hn = _rms_fwd(x_ref[...], g_ref[...], d).astype(BF16)
        hn_ref[...] = hn
        for z_ref, o, n in zip(z_refs[:-1], offs, splits):
            z_ref[...] = _nt(hn, w_ref[o:o + n, :])
        z_refs[-1][...] = _nt(hn, tail_ref[...])

    row = pl.BlockSpec((tm, d), lambda i: (i, 0))
    return pl.pallas_call(
        body, name="inproj_fwd", grid=(t // tm,),
        in_specs=[row, pl.BlockSpec((1, d), lambda i: (0, 0)), WHOLE_VMEM],
        out_specs=[row] + [pl.BlockSpec((tm, n), lambda i: (i, 0)) for n in splits],
        out_shape=[jax.ShapeDtypeStruct((t, d), BF16)] + [jax.ShapeDtypeStruct((t, n), F32) for n in splits],
        scratch_shapes=[pltpu.VMEM((splits[-1], d), BF16)],
        compiler_params=_params(("arbitrary",)),
    )(x, g, w_t)


def _inproj_bwd(dzs, x, g, w_t, dres, splits, tm):
    t, d = x.shape
    offs = [sum(splits[:k]) for k in range(len(splits))]
    ni = sum(splits)
    nz = len(splits)
    real_last = w_t.shape[0] - offs[-1]

    def body(*refs):
        dz_refs = refs[:nz]
        x_ref, g_ref, w_ref, dres_ref, dx_ref, dg_ref, dzc_ref, tail_ref = refs[nz:]
        _last_rows_padded(w_ref, tail_ref, offs[-1], real_last)
        dhn = jnp.zeros((tm, d), F32)
        for k, (dz_ref, o, n) in enumerate(zip(dz_refs, offs, splits)):
            dz = dz_ref[...].astype(BF16)
            dzc_ref[:, o:o + n] = dz
            dhn += _nn(dz, tail_ref[...] if k == nz - 1 else w_ref[o:o + n, :])
        dx, dg = _rms_bwd(x_ref[...], g_ref[...], dhn, d)
        dx_ref[...] = dres_ref[...] + dx

        @pl.when(pl.program_id(0) == 0)
        def _():
            dg_ref[...] = jnp.zeros_like(dg_ref)

        dg_ref[...] += dg

    row = pl.BlockSpec((tm, d), lambda i: (i, 0))
    vec = pl.BlockSpec((1, d), lambda i: (0, 0))
    return pl.pallas_call(
        body, name="inproj_bwd", grid=(t // tm,),
        in_specs=[pl.BlockSpec((tm, n), lambda i: (i, 0)) for n in splits] + [row, vec, WHOLE_VMEM, row],
        out_specs=[row, vec, pl.BlockSpec((tm, ni), lambda i: (i, 0))],
        out_shape=[jax.ShapeDtypeStruct((t, d), F32), jax.ShapeDtypeStruct((1, d), F32),
                   jax.ShapeDtypeStruct((t, ni), BF16)],
        scratch_shapes=[pltpu.VMEM((splits[-1], d), BF16)],
        compiler_params=_params(("arbitrary",)),
    )(*dzs, x, g, w_t, dres)


def _tril_bf16(ws_ref, grp):
    rows = lax.broadcasted_iota(jnp.int32, (CHUNK, CHUNK), 0)
    cols = lax.broadcasted_iota(jnp.int32, (CHUNK, CHUNK), 1)
    return jnp.where(rows >= cols, ws_ref[grp], 0.0).astype(BF16)


def _gmlp_mix(zuv_ref, gv_ref, ws_ref, bias_ref, v_scr, mixed_scr, tm, w, groups):
    u = _gelu(zuv_ref[:, 0:w])
    v0 = _gelu(zuv_ref[:, w:2 * w])
    v_scr[...] = _rms_fwd(v0, gv_ref[...], w).astype(BF16)
    for grp in range(groups):
        wsm = _tril_bf16(ws_ref, grp)
        lanes = slice(grp * CHUNK, (grp + 1) * CHUNK)
        for c in range(tm // CHUNK):
            rows = slice(c * CHUNK, (c + 1) * CHUNK)
            mixed_scr[rows, lanes] = _nn(wsm, v_scr[rows, lanes]) + bias_ref[:, lanes]
    return u, v0


def _gmlp_fwd(zuv, gv, ws, bias, gout, tm):
    t, w2 = zuv.shape
    w = w2 // 2
    groups = ws.shape[0]

    def body(zuv_ref, gv_ref, ws_ref, bias_ref, go_ref, y_ref, v_scr, mixed_scr):
        u, _ = _gmlp_mix(zuv_ref, gv_ref, ws_ref, bias_ref, v_scr, mixed_scr, tm, w, groups)
        ya = u * mixed_scr[...]
        for grp in range(groups):
            lanes = slice(grp * CHUNK, (grp + 1) * CHUNK)
            y_ref[:, lanes] = _rms_fwd(ya[:, lanes], go_ref[:, lanes], CHUNK).astype(BF16)

    const2 = lambda i: (0, 0)
    return pl.pallas_call(
        body, name="gmlp_fwd", grid=(t // tm,),
        in_specs=[pl.BlockSpec((tm, w2), lambda i: (i, 0)), pl.BlockSpec((1, w), const2),
                  pl.BlockSpec((groups, CHUNK, CHUNK), lambda i: (0, 0, 0)),
                  pl.BlockSpec((CHUNK, w), const2), pl.BlockSpec((1, w), const2)],
        out_specs=pl.BlockSpec((tm, w), lambda i: (i, 0)),
        out_shape=jax.ShapeDtypeStruct((t, w), BF16),
        scratch_shapes=[pltpu.VMEM((tm, w), BF16), pltpu.VMEM((tm, w), F32)],
        compiler_params=_params(("arbitrary",)),
    )(zuv, gv, ws, bias, gout)


def _gmlp_bwd(dy, zuv, gv, ws, bias, gout, tm):
    t, w2 = zuv.shape
    w = w2 // 2
    groups = ws.shape[0]

    def body(dy_ref, zuv_ref, gv_ref, ws_ref, bias_ref, go_ref,
             dz_ref, dws_ref, dbias_ref, dgv_ref, dgo_ref, v_scr, mixed_scr, dmix_scr, dv_scr):
        @pl.when(pl.program_id(0) == 0)
        def _():
            dws_ref[...] = jnp.zeros_like(dws_ref)
            dbias_ref[...] = jnp.zeros_like(dbias_ref)
            dgv_ref[...] = jnp.zeros_like(dgv_ref)
            dgo_ref[...] = jnp.zeros_like(dgo_ref)

        u, v0 = _gmlp_mix(zuv_ref, gv_ref, ws_ref, bias_ref, v_scr, mixed_scr, tm, w, groups)
        mixed = mixed_scr[...]
        ya = u * mixed
        for grp in range(groups):
            lanes = slice(grp * CHUNK, (grp + 1) * CHUNK)
            dya, dgo = _rms_bwd(ya[:, lanes], go_ref[:, lanes], dy_ref[:, lanes], CHUNK)
            dgo_ref[:, lanes] += dgo
            dz_ref[:, lanes] = dya * mixed[:, lanes] * _gelu_grad(zuv_ref[:, lanes])
            dmix_scr[:, lanes] = dya * u[:, lanes]
        for grp in range(groups):
            wsm = _tril_bf16(ws_ref, grp)
            lanes = slice(grp * CHUNK, (grp + 1) * CHUNK)
            dws = jnp.zeros((CHUNK, CHUNK), F32)
            dbias = jnp.zeros((CHUNK, CHUNK), F32)
            for c in range(tm // CHUNK):
                rows = slice(c * CHUNK, (c + 1) * CHUNK)
                dm = dmix_scr[rows, lanes]
                dmb = dm.astype(BF16)
                dv_scr[rows, lanes] = _tn(wsm, dmb)
                dws += _nt(dmb, v_scr[rows, lanes])
                dbias += dm
            rr = lax.broadcasted_iota(jnp.int32, (CHUNK, CHUNK), 0)
            cc = lax.broadcasted_iota(jnp.int32, (CHUNK, CHUNK), 1)
            dws_ref[grp] += jnp.where(rr >= cc, dws, 0.0)
            dbias_ref[grp] += jnp.sum(dbias, axis=1, keepdims=True)
        dv0, dgv = _rms_bwd(v0, gv_ref[...], dv_scr[...], w)
        dgv_ref[...] += dgv
        dz_ref[:, w:2 * w] = dv0 * _gelu_grad(zuv_ref[:, w:2 * w])

    const2 = lambda i: (0, 0)
    const3 = lambda i: (0, 0, 0)
    return pl.pallas_call(
        body, name="gmlp_bwd", grid=(t // tm,),
        in_specs=[pl.BlockSpec((tm, w), lambda i: (i, 0)), pl.BlockSpec((tm, w2), lambda i: (i, 0)),
                  pl.BlockSpec((1, w), const2), pl.BlockSpec((groups, CHUNK, CHUNK), const3),
                  pl.BlockSpec((CHUNK, w), const2), pl.BlockSpec((1, w), const2)],
        out_specs=[pl.BlockSpec((tm, w2), lambda i: (i, 0)), pl.BlockSpec((groups, CHUNK, CHUNK), const3),
                   pl.BlockSpec((groups, CHUNK, 1), const3), pl.BlockSpec((1, w), const2), pl.BlockSpec((1, w), const2)],
        out_shape=[jax.ShapeDtypeStruct((t, w2), F32), jax.ShapeDtypeStruct((groups, CHUNK, CHUNK), F32),
                   jax.ShapeDtypeStruct((groups, CHUNK, 1), F32), jax.ShapeDtypeStruct((1, w), F32),
                   jax.ShapeDtypeStruct((1, w), F32)],
        scratch_shapes=[pltpu.VMEM((tm, w), BF16), pltpu.VMEM((tm, w), F32),
                        pltpu.VMEM((tm, w), F32), pltpu.VMEM((tm, w), F32)],
        compiler_params=_params(("arbitrary",)),
    )(dy, zuv, gv, ws, bias, gout)


def _rot(x, m_lo, m_hi):
    return pltpu.roll(x, LANE - ROPE // 2, 1) * m_lo + pltpu.roll(x, ROPE // 2, 1) * m_hi


def _rope_tables(pos_ref, freq_ref):
    ang = pos_ref[...] * freq_ref[...]
    return jnp.cos(ang), jnp.sin(ang)


def _mla_proj_fwd(cq, ckv, krw, pos, freq, masks, gq, gkv, wq_t, wkv_t, gqh, gkh, tm):
    t, rq = cq.shape
    rkv = ckv.shape[1]
    heads = wq_t.shape[0]

    def body(cq_ref, ckv_ref, kr_ref, pos_ref, freq_ref, mk_ref, gq_ref, gkv_ref, wq_ref, wkv_ref,
             gqh_ref, gkh_ref, q_ref, k_ref, v_ref):
        cos, sin = _rope_tables(pos_ref, freq_ref)
        m_lo, m_hi = mk_ref[0:1, :], mk_ref[1:2, :]
        cqn = _rms_fwd(cq_ref[...], gq_ref[...], rq).astype(BF16)
        ckvn = _rms_fwd(ckv_ref[...], gkv_ref[...], rkv).astype(BF16)
        kr = kr_ref[...]
        kr_ss = jnp.sum(kr * kr, axis=-1, keepdims=True)
        for h in range(heads):
            qh = _nt(cqn, wq_ref[h])
            qn = qh * _rstd(qh, QK) * gqh_ref[...]
            qr = qn[:, LANE:]
            q_ref[h, :, 0:LANE] = qn[:, 0:LANE].astype(BF16)
            q_ref[h, :, LANE:] = (qr * cos + _rot(qr, m_lo, m_hi) * sin).astype(BF16)
            kvh = _nt(ckvn, wkv_ref[h])
            kn = kvh[:, 0:LANE]
            rk = lax.rsqrt((jnp.sum(kn * kn, axis=-1, keepdims=True) + kr_ss) * (1.0 / QK) + EPS)
            k_ref[h, :, 0:LANE] = (kn * rk * gkh_ref[:, 0:LANE]).astype(BF16)
            krn = kr * rk * gkh_ref[:, LANE:]
            k_ref[h, :, LANE:] = (krn * cos + _rot(krn, m_lo, m_hi) * sin).astype(BF16)
            v_ref[h] = kvh[:, LANE:].astype(BF16)

    c2 = lambda i: (0, 0)
    c3 = lambda i: (0, 0, 0)
    return pl.pallas_call(
        body, name="mla_proj_fwd", grid=(t // tm,),
        in_specs=[pl.BlockSpec((tm, rq), lambda i: (i, 0)), pl.BlockSpec((tm, rkv), lambda i: (i, 0)),
                  pl.BlockSpec((tm, LANE), lambda i: (i, 0)), pl.BlockSpec((tm, 1), lambda i: (i, 0)),
                  pl.BlockSpec((1, LANE), c2), pl.BlockSpec((2, LANE), c2),
                  pl.BlockSpec((1, rq), c2), pl.BlockSpec((1, rkv), c2),
                  pl.BlockSpec((heads, HEADW, rq), c3), pl.BlockSpec((heads, HEADW, rkv), c3),
                  pl.BlockSpec((1, HEADW), c2), pl.BlockSpec((1, HEADW), c2)],
        out_specs=[pl.BlockSpec((heads, tm, HEADW), lambda i: (0, i, 0)),
                   pl.BlockSpec((heads, tm, HEADW), lambda i: (0, i, 0)),
                   pl.BlockSpec((heads, tm, VHEAD), lambda i: (0, i, 0))],
        out_shape=[jax.ShapeDtypeStruct((heads, t, HEADW), BF16), jax.ShapeDtypeStruct((heads, t, HEADW), BF16),
                   jax.ShapeDtypeStruct((heads, t, VHEAD), BF16)],
        compiler_params=_params(("arbitrary",)),
    )(cq, ckv, krw, pos, freq, masks, gq, gkv, wq_t, wkv_t, gqh, gkh)


def _mla_proj_bwd(dq, dk, dv, cq, ckv, krw, pos, freq, masks, gq, gkv, wq_t, wkv_t, gqh, gkh, tm):
    t, rq = cq.shape
    rkv = ckv.shape[1]
    heads = wq_t.shape[0]

    def body(dq_ref, dk_ref, dv_ref, cq_ref, ckv_ref, kr_ref, pos_ref, freq_ref, mk_ref, gq_ref, gkv_ref,
             wq_ref, wkv_ref, gqh_ref, gkh_ref,
             dcq_ref, dckv_ref, dkr_ref, dwq_ref, dwkv_ref, dgq_ref, dgkv_ref, dgqh_ref, dgkh_ref):
        @pl.when(pl.program_id(0) == 0)
        def _():
            for r in (dwq_ref, dwkv_ref, dgq_ref, dgkv_ref, dgqh_ref, dgkh_ref):
                r[...] = jnp.zeros_like(r)

        cos, sin = _rope_tables(pos_ref, freq_ref)
        m_lo, m_hi = mk_ref[0:1, :], mk_ref[1:2, :]

        def unrope(dy):
            return dy * cos - _rot(dy * sin, m_lo, m_hi)

        cqn = _rms_fwd(cq_ref[...], gq_ref[...], rq).astype(BF16)
        ckvn = _rms_fwd(ckv_ref[...], gkv_ref[...], rkv).astype(BF16)
        kr = kr_ref[...]
        kr_ss = jnp.sum(kr * kr, axis=-1, keepdims=True)
        dcqn = jnp.zeros((tm, rq), F32)
        dckvn = jnp.zeros((tm, rkv), F32)
        dkr = jnp.zeros((tm, LANE), F32)
        for h in range(heads):
            qh = _nt(cqn, wq_ref[h])
            dqn = jnp.concatenate([dq_ref[h, :, 0:LANE], unrope(dq_ref[h, :, LANE:])], axis=1)
            dqh, dg = _rms_bwd(qh, gqh_ref[...], dqn, QK)
            dgqh_ref[...] += dg
            dqh = dqh.astype(BF16)
            dcqn += _nn(dqh, wq_ref[h])
            dwq_ref[h] += _tn(dqh, cqn)

            kvh = _nt(ckvn, wkv_ref[h])
            kn = kvh[:, 0:LANE]
            rk = lax.rsqrt((jnp.sum(kn * kn, axis=-1, keepdims=True) + kr_ss) * (1.0 / QK) + EPS)
            dkn_n = dk_ref[h, :, 0:LANE]
            dkr_n = unrope(dk_ref[h, :, LANE:])
            knh, krh = kn * rk, kr * rk
            dgkh_ref[:, 0:LANE] += jnp.sum(dkn_n * knh, axis=0, keepdims=True)
            dgkh_ref[:, LANE:] += jnp.sum(dkr_n * krh, axis=0, keepdims=True)
            dkn_g, dkr_g = dkn_n * gkh_ref[:, 0:LANE], dkr_n * gkh_ref[:, LANE:]
            proj = (jnp.sum(dkn_g * knh, axis=-1, keepdims=True)
                    + jnp.sum(dkr_g * krh, axis=-1, keepdims=True)) * (1.0 / QK)
            dkr += rk * (dkr_g - krh * proj)
            dkvh = jnp.concatenate([rk * (dkn_g - knh * proj), dv_ref[h]], axis=1).astype(BF16)
            dckvn += _nn(dkvh, wkv_ref[h])
            dwkv_ref[h] += _tn(dkvh, ckvn)
        dkr_ref[...] = dkr
        dcq, dg = _rms_bwd(cq_ref[...], gq_ref[...], dcqn, rq)
        dcq_ref[...] = dcq
        dgq_ref[...] += dg
        dckv, dg = _rms_bwd(ckv_ref[...], gkv_ref[...], dckvn, rkv)
        dckv_ref[...] = dckv
        dgkv_ref[...] += dg

    c2 = lambda i: (0, 0)
    c3 = lambda i: (0, 0, 0)
    hq = pl.BlockSpec((heads, tm, HEADW), lambda i: (0, i, 0))
    return pl.pallas_call(
        body, name="mla_proj_bwd", grid=(t // tm,),
        in_specs=[hq, hq, pl.BlockSpec((heads, tm, VHEAD), lambda i: (0, i, 0)),
                  pl.BlockSpec((tm, rq), lambda i: (i, 0)), pl.BlockSpec((tm, rkv), lambda i: (i, 0)),
                  pl.BlockSpec((tm, LANE), lambda i: (i, 0)), pl.BlockSpec((tm, 1), lambda i: (i, 0)),
                  pl.BlockSpec((1, LANE), c2), pl.BlockSpec((2, LANE), c2),
                  pl.BlockSpec((1, rq), c2), pl.BlockSpec((1, rkv), c2),
                  pl.BlockSpec((heads, HEADW, rq), c3), pl.BlockSpec((heads, HEADW, rkv), c3),
                  pl.BlockSpec((1, HEADW), c2), pl.BlockSpec((1, HEADW), c2)],
        out_specs=[pl.BlockSpec((tm, rq), lambda i: (i, 0)), pl.BlockSpec((tm, rkv), lambda i: (i, 0)),
                   pl.BlockSpec((tm, LANE), lambda i: (i, 0)),
                   pl.BlockSpec((heads, HEADW, rq), c3), pl.BlockSpec((heads, HEADW, rkv), c3),
                   pl.BlockSpec((1, rq), c2), pl.BlockSpec((1, rkv), c2),
                   pl.BlockSpec((1, HEADW), c2), pl.BlockSpec((1, HEADW), c2)],
        out_shape=[jax.ShapeDtypeStruct((t, rq), F32), jax.ShapeDtypeStruct((t, rkv), F32),
                   jax.ShapeDtypeStruct((t, LANE), F32),
                   jax.ShapeDtypeStruct((heads, HEADW, rq), F32), jax.ShapeDtypeStruct((heads, HEADW, rkv), F32),
                   jax.ShapeDtypeStruct((1, rq), F32), jax.ShapeDtypeStruct((1, rkv), F32),
                   jax.ShapeDtypeStruct((1, HEADW), F32), jax.ShapeDtypeStruct((1, HEADW), F32)],
        compiler_params=_params(("arbitrary",)),
    )(dq, dk, dv, cq, ckv, krw, pos, freq, masks, gq, gkv, wq_t, wkv_t, gqh, gkh)


def _lower_triangle(blk):
    return lax.broadcasted_iota(jnp.int32, (blk, blk), 0) >= lax.broadcasted_iota(jnp.int32, (blk, blk), 1)


def _attn_fwd(q, k, v, seq, blk):
    heads, t, _ = q.shape
    scale = QK ** -0.5
    nblk = seq // blk

    def body(q_ref, k_ref, v_ref, o_ref, lse_ref):
        tri = _lower_triangle(blk)
        for qi in range(nblk):
            rows = slice(qi * blk, (qi + 1) * blk)
            before = slice(0, qi * blk)
            qb = q_ref[0, rows, :]
            s_d = jnp.where(tri, _nt(qb, k_ref[0, rows, :]) * scale, -1e30)
            m = jnp.max(s_d, axis=-1, keepdims=True)
            if qi:
                s_b = _nt(qb, k_ref[0, before, :]) * scale
                m = jnp.maximum(m, jnp.max(s_b, axis=-1, keepdims=True))
                p_b = jnp.exp(s_b - m)
            p_d = jnp.exp(s_d - m)
            l = jnp.sum(p_d, axis=-1, keepdims=True)
            acc = _nn(p_d.astype(BF16), v_ref[0, rows, :])
            if qi:
                l += jnp.sum(p_b, axis=-1, keepdims=True)
                acc += _nn(p_b.astype(BF16), v_ref[0, before, :])
            o_ref[0, rows, :] = acc / l
            lse_ref[0, rows, :] = m + jnp.log(l)

    return pl.pallas_call(
        body, name="attn_fwd", grid=(heads, t // seq),
        in_specs=[pl.BlockSpec((1, seq, HEADW), lambda h, b: (h, b, 0)),
                  pl.BlockSpec((1, seq, HEADW), lambda h, b: (h, b, 0)),
                  pl.BlockSpec((1, seq, VHEAD), lambda h, b: (h, b, 0))],
        out_specs=[pl.BlockSpec((1, seq, VHEAD), lambda h, b: (h, b, 0)),
                   pl.BlockSpec((1, seq, 1), lambda h, b: (h, b, 0))],
        out_shape=[jax.ShapeDtypeStruct((heads, t, VHEAD), F32), jax.ShapeDtypeStruct((heads, t, 1), F32)],
        compiler_params=_params(("arbitrary", "arbitrary")),
    )(q, k, v)


def _attn_bwd(q, k, v, do, lse, delta, seq, blk, after):
    heads, t, _ = q.shape
    scale = QK ** -0.5
    nblk = seq // blk

    def body(q_ref, k_ref, v_ref, do_ref, lse_ref, dl_ref, _, dq_ref, dk_ref, dv_ref):
        tri = _lower_triangle(blk)
        dk_ref[...] = jnp.zeros_like(dk_ref)
        dv_ref[...] = jnp.zeros_like(dv_ref)
        for qi in range(nblk):
            rows = slice(qi * blk, (qi + 1) * blk)
            qb = q_ref[0, rows, :]
            dob = do_ref[0, rows, :]
            lse_b = lse_ref[0, rows, :]
            dl_b = dl_ref[0, rows, :]
            dq = jnp.zeros((blk, HEADW), F32)
            for keys, masked in ((slice(0, qi * blk), False), (rows, True)):
                if keys.stop == keys.start:
                    continue
                kb = k_ref[0, keys, :]
                p = jnp.exp(_nt(qb, kb) * scale - lse_b)
                if masked:
                    p = jnp.where(tri, p, 0.0)
                dp = _nt(dob, v_ref[0, keys, :])
                ds = (p * (dp - dl_b) * scale).astype(BF16)
                dv_ref[0, keys, :] += _tn(p.astype(BF16), dob)
                dk_ref[0, keys, :] += _tn(ds, qb)
                dq += _nn(ds, kb)
            dq_ref[0, rows, :] = dq

    hq = pl.BlockSpec((1, seq, HEADW), lambda h, b: (h, b, 0))
    hv = pl.BlockSpec((1, seq, VHEAD), lambda h, b: (h, b, 0))
    h1 = pl.BlockSpec((1, seq, 1), lambda h, b: (h, b, 0))
    return pl.pallas_call(
        body, name="attn_bwd", grid=(heads, t // seq),
        in_specs=[hq, hq, hv, hv, h1, h1, ANY],
        out_specs=[hq, hq, hv],
        out_shape=[jax.ShapeDtypeStruct((heads, t, HEADW), F32), jax.ShapeDtypeStruct((heads, t, HEADW), F32),
                   jax.ShapeDtypeStruct((heads, t, VHEAD), F32)],
        compiler_params=_params(("arbitrary", "arbitrary")),
    )(q, k, v, do, lse, delta, after)


def _out_fwd(ya, o, gb, w_out, x1, tm):
    t, w = ya.shape
    heads = o.shape[0]
    d = x1.shape[1]

    def body(ya_ref, o_ref, gb_ref, w_ref, x1_ref, x2_ref, yc_ref):
        yc_ref[:, 0:w] = ya_ref[...]
        for h in range(heads):
            lanes = slice(h * VHEAD, (h + 1) * VHEAD)
            yc_ref[:, w + h * VHEAD:w + (h + 1) * VHEAD] = _rms_fwd(o_ref[h], gb_ref[:, lanes], VHEAD).astype(BF16)
        x2_ref[...] = x1_ref[...] + _nn(yc_ref[...], w_ref[...])

    wy = w + heads * VHEAD
    row = pl.BlockSpec((tm, d), lambda i: (i, 0))
    return pl.pallas_call(
        body, name="out_fwd", grid=(t // tm,),
        in_specs=[pl.BlockSpec((tm, w), lambda i: (i, 0)), pl.BlockSpec((heads, tm, VHEAD), lambda i: (0, i, 0)),
                  pl.BlockSpec((1, heads * VHEAD), lambda i: (0, 0)), WHOLE_VMEM, row],
        out_specs=[row, pl.BlockSpec((tm, wy), lambda i: (i, 0))],
        out_shape=[jax.ShapeDtypeStruct((t, d), F32), jax.ShapeDtypeStruct((t, wy), BF16)],
        compiler_params=_params(("arbitrary",)),
    )(ya, o, gb, w_out, x1)


def _out_bwd(dx2, o, gb, w_out, w, tm):
    t, d = dx2.shape
    heads = o.shape[0]

    def body(dx_ref, o_ref, gb_ref, w_ref, dya_ref, do_ref, dl_ref, dgb_ref):
        @pl.when(pl.program_id(0) == 0)
        def _():
            dgb_ref[...] = jnp.zeros_like(dgb_ref)

        dyc = _nt(dx_ref[...].astype(BF16), w_ref[...])
        dya_ref[...] = dyc[:, 0:w]
        for h in range(heads):
            lanes = slice(h * VHEAD, (h + 1) * VHEAD)
            oh = o_ref[h]
            doh, dg = _rms_bwd(oh, gb_ref[:, lanes], dyc[:, w + h * VHEAD:w + (h + 1) * VHEAD], VHEAD)
            dgb_ref[:, lanes] += dg
            do_ref[h] = doh.astype(BF16)
            dl_ref[h] = jnp.sum(doh * oh, axis=-1, keepdims=True)

    ho = pl.BlockSpec((heads, tm, VHEAD), lambda i: (0, i, 0))
    vec = pl.BlockSpec((1, heads * VHEAD), lambda i: (0, 0))
    return pl.pallas_call(
        body, name="out_bwd", grid=(t // tm,),
        in_specs=[pl.BlockSpec((tm, d), lambda i: (i, 0)), ho, vec, WHOLE_VMEM],
        out_specs=[pl.BlockSpec((tm, w), lambda i: (i, 0)), ho, pl.BlockSpec((heads, tm, 1), lambda i: (0, i, 0)), vec],
        out_shape=[jax.ShapeDtypeStruct((t, w), F32), jax.ShapeDtypeStruct((heads, t, VHEAD), BF16),
                   jax.ShapeDtypeStruct((heads, t, 1), F32), jax.ShapeDtypeStruct((1, heads * VHEAD), F32)],
        compiler_params=_params(("arbitrary",)),
    )(dx2, o, gb, w_out)


def _loss_head(y, target, tm):
    t, d = y.shape

    def body(y_ref, t_ref, dy_ref, loss_ref):
        @pl.when(pl.program_id(0) == 0)
        def _():
            loss_ref[...] = jnp.zeros_like(loss_ref)

        err = y_ref[...] - t_ref[...]
        dy_ref[...] = err * (1.0 / d)
        part = jnp.sum(jnp.sum(err * err, axis=-1, keepdims=True) * (1.0 / d), axis=0, keepdims=True)
        loss_ref[...] += 0.5 * part

    row = pl.BlockSpec((tm, d), lambda i: (i, 0))
    return pl.pallas_call(
        body, name="loss_head", grid=(t // tm,),
        in_specs=[row, row], out_specs=[row, pl.BlockSpec((1, 1), lambda i: (0, 0))],
        out_shape=[jax.ShapeDtypeStruct((t, d), F32), jax.ShapeDtypeStruct((1, 1), F32)],
        compiler_params=_params(("arbitrary",)),
    )(y, target)


def _place():
    return lax.axis_index("x"), lax.axis_index("y"), lax.axis_index("c")


HBM = pl.BlockSpec(memory_space=pltpu.HBM)
SEM = pl.BlockSpec(memory_space=pltpu.SEMAPHORE)
DATAFLOW = pltpu.SideEffectType.DATAFLOW_SIDE_EFFECTING


def _plan_copies(plan, refs, send_sems, recv_sems):
    def block(ref, blk):
        if blk is None:
            return ref
        return ref.at[blk[0], pl.ds(0, blk[1])] if isinstance(blk, tuple) else ref.at[blk]

    cps = []
    for i, (sb, sblk, db, dblk, dev) in enumerate(plan(*_place())):
        cps.append(pltpu.make_async_remote_copy(
            src_ref=block(refs[sb], sblk), dst_ref=block(refs[db], dblk),
            send_sem=send_sems.at[i], recv_sem=recv_sems.at[i], device_id=dev, device_id_type=MESH))
    return cps


def _push_start(bufs, plan, ncopy, name, deps=()):
    nb = len(bufs)

    def body(*refs):
        outs = refs[nb + len(deps):]
        for cp in _plan_copies(plan, refs[:nb], outs[0], outs[1]):
            cp.start()
        outs[-1][...] = jnp.zeros_like(outs[-1])

    res = pl.pallas_call(
        body, name=name,
        out_shape=(pltpu.SemaphoreType.DMA((ncopy,)), pltpu.SemaphoreType.DMA((ncopy,)),
                   *[pltpu.HBM(b.shape, b.dtype) for b in bufs], jax.ShapeDtypeStruct((SUBLANE, LANE), F32)),
        in_specs=[HBM] * nb + [ANY] * len(deps),
        out_specs=(SEM, SEM, *[HBM] * nb, WHOLE_VMEM),
        input_output_aliases={i: 2 + i for i in range(nb)},
        compiler_params=pltpu.CompilerParams(has_side_effects=DATAFLOW),
    )(*[pltpu.with_memory_space_constraint(b, pltpu.HBM) for b in bufs], *deps)
    return res[0], res[1], list(res[2:2 + nb]), res[-1]


def _push_wait(send_sems, recv_sems, bufs, plan, after, name):
    nb = len(bufs)

    def body(*refs):
        for cp in _plan_copies(plan, refs[:nb], refs[nb], refs[nb + 1]):
            cp.wait_send()
            cp.wait_recv()

    res = pl.pallas_call(
        body, name=name,
        out_shape=[pltpu.HBM(b.shape, b.dtype) for b in bufs],
        in_specs=[HBM] * nb + [SEM, SEM, ANY], out_specs=[HBM] * nb,
        input_output_aliases={i: i for i in range(nb)},
        compiler_params=pltpu.CompilerParams(has_side_effects=DATAFLOW),
    )(*bufs, send_sems, recv_sems, after)
    return list(res)


def _other_chips(x, y):
    return ((1 - x, y), (x, 1 - y), (1 - x, 1 - y))


class _Exchange:
    def __init__(self, bufs, plan, ncopy, name, deps=()):
        self.plan, self.name = plan, name
        self.send, self.recv, self.bufs, self.token = _push_start(bufs, plan, ncopy, name + "_start", deps)

    def wait(self, after):
        return _push_wait(self.send, self.recv, self.bufs, self.plan, after, self.name + "_wait")


class _Chain:
    def __init__(self, bufs):
        self.bufs = list(bufs)

    def start(self, plan, ncopy, name, deps=()):
        send, recv, self.bufs, token = _push_start(self.bufs, plan, ncopy, name + "_start", deps)
        return (send, recv, plan, name), token

    def wait(self, pending, after):
        send, recv, plan, name = pending
        self.bufs = _push_wait(send, recv, self.bufs, plan, after, name + "_wait")


class _StagedGather:
    def __init__(self, shards, me, name, pad_to=None):
        self.n = n = len(shards)
        self.name = name
        rows = shards[0].shape[0]
        lands = []
        for s in shards:
            land = lax.empty((N_DEV, pad_to or rows) + s.shape[1:], s.dtype)
            if pad_to and pad_to != rows:
                land = lax.dynamic_update_slice(
                    land, jnp.zeros((N_DEV, pad_to - rows) + s.shape[1:], s.dtype), (0, rows, 0))
            lands.append(lax.dynamic_update_slice(land, s[None], (me, 0, 0)))
        self.chain = _Chain(list(shards) + lands)
        self.pending = {}

        def blk(b):
            return (b, rows) if pad_to and pad_to != rows else b

        def to_sibling(blocks):
            return lambda x, y, c: [(n + a, blk(b), n + a, blk(b), (x, y, 1 - c))
                                    for a in range(n) for b in blocks(x, y, c)]

        def nbr_blocks(x, y, c):
            return [4 * (1 - x) + 2 * y + c, 4 * x + 2 * (1 - y) + c]

        def diag(x, y, c):
            sx, sy = (1 - x) * (1 - c) + x * c, y * (1 - c) + (1 - y) * c
            tx, ty = x * (1 - c) + (1 - x) * c, (1 - y) * (1 - c) + y * c
            b = blk(4 * sx + 2 * sy + c)
            return [(n + a, b, n + a, b, (tx, ty, c)) for a in range(n)]

        self.plans = {
            "own": (lambda x, y, c: [(a, None, n + a, blk(4 * x + 2 * y + c), (x, y, 1 - c)) for a in range(n)], n),
            "nbr": (lambda x, y, c: [(a, None, n + a, blk(4 * x + 2 * y + c), dev) for a in range(n)
                                     for dev in ((1 - x, y, c), (x, 1 - y, c))], 2 * n),
            "diag": (diag, n),
            "nbr_d2d": (to_sibling(nbr_blocks), 2 * n),
            "own_nbr_d2d": (to_sibling(lambda x, y, c: [4 * x + 2 * y + c] + nbr_blocks(x, y, c)), 3 * n),
            "diag_d2d": (to_sibling(lambda x, y, c: [4 * (1 - x) + 2 * (1 - y) + c]), n),
        }

    def start(self, stage, deps=()):
        plan, ncopy = self.plans[stage]
        self.pending[stage], token = self.chain.start(plan, ncopy, self.name + "_" + stage, deps)
        return token

    def wait(self, stage, after):
        self.chain.wait(self.pending.pop(stage), after)

    def lands(self):
        return self.chain.bufs[self.n:]


def _gather_ici(shards, me, name, deps=()):
    n = len(shards)
    lands = [lax.dynamic_update_slice(lax.empty((N_DEV,) + s.shape, s.dtype), s[None], (me, 0, 0)) for s in shards]

    def plan(x, y, c):
        return [(a, None, n + a, 4 * x + 2 * y + c, (px, py, c)) for a in range(n) for px, py in _other_chips(x, y)]

    return _Exchange(list(shards) + lands, plan, 3 * n, name, deps)


def _gather_d2d(lands, name, deps=()):
    n = len(lands)

    def plan(x, y, c):
        blocks = [4 * x + 2 * y + c] + [4 * px + 2 * py + c for px, py in _other_chips(x, y)]
        return [(a, b, a, b, (x, y, 1 - c)) for a in range(n) for b in blocks]

    return _Exchange(list(lands), plan, 4 * n, name, deps)


def _reduce_d2d(grads, name, deps=(), rows=None):
    n = len(grads)
    lands = [lax.empty((4,) + g.shape[1:], g.dtype) for g in grads]

    def blk(b):
        return b if rows is None else (b, rows)

    def plan(x, y, c):
        return [(a, blk(2 * s + (1 - c)), n + a, blk(s), (x, y, 1 - c)) for a in range(n) for s in range(4)]

    return _Exchange(list(grads) + lands, plan, 4 * n, name, deps)


def _reduce_ici(chip, name, deps=(), rows=None):
    n = len(chip)
    lands = [lax.empty((3,) + g.shape[1:], g.dtype) for g in chip]

    def blk(b):
        return b if rows is None else (b, rows)

    def plan(x, y, c):
        return [(a, blk(2 * px + py), n + a, blk(k), (px, py, c))
                for a in range(n) for k, (px, py) in enumerate(_other_chips(x, y))]

    return _Exchange(list(chip) + lands, plan, 3 * n, name, deps)


def _pair_add(full, got, core, name, rows=None):
    _, r, cdim = full.shape
    br = _row_block(rows or r, 512)

    def body(c_ref, f_ref, g_ref, o_ref):
        o_ref[...] = (f_ref[...].astype(F32) + g_ref[...].astype(F32)).astype(o_ref.dtype)

    return pl.pallas_call(
        body, name=name,
        grid_spec=pltpu.PrefetchScalarGridSpec(
            num_scalar_prefetch=1, grid=(4, (rows or r) // br),
            in_specs=[pl.BlockSpec((1, br, cdim), lambda s, i, c_ref: (2 * s + c_ref[0], i, 0)),
                      pl.BlockSpec((1, br, cdim), lambda s, i, c_ref: (s, i, 0))],
            out_specs=pl.BlockSpec((1, br, cdim), lambda s, i, c_ref: (s, i, 0))),
        out_shape=jax.ShapeDtypeStruct((4, r, cdim), full.dtype),
        compiler_params=_params(("arbitrary", "arbitrary")),
    )(core, full, got)


def _sum_devices(stack):
    _, r, cdim = stack.shape

    def body(s_ref, o_ref):
        acc = s_ref[0]
        for k in range(1, N_DEV):
            acc = acc + s_ref[k]
        o_ref[...] = acc

    return pl.pallas_call(
        body, name="sum_devices", out_shape=jax.ShapeDtypeStruct((r, cdim), F32),
        compiler_params=_params(),
    )(stack)


def _adamw(w, g, m, v, name):
    r, cdim = w.shape
    br = _row_block(r, 256)

    def body(w_ref, g_ref, m_ref, v_ref, d_ref, nm_ref, nv_ref):
        g = g_ref[...]
        nm = ADAM_B1 * m_ref[...] + (1.0 - ADAM_B1) * g
        nv = ADAM_B2 * v_ref[...] + (1.0 - ADAM_B2) * (g * g)
        m_hat = nm / (1.0 - ADAM_B1 ** ADAM_STEP)
        v_hat = nv / (1.0 - ADAM_B2 ** ADAM_STEP)
        d_ref[...] = -ADAM_LR * (m_hat / (jnp.sqrt(v_hat) + ADAM_EPS) + ADAM_WD * w_ref[...])
        nm_ref[...] = nm
        nv_ref[...] = nv

    spec = pl.BlockSpec((br, cdim), lambda i: (i, 0))
    shape = jax.ShapeDtypeStruct((r, cdim), F32)
    return pl.pallas_call(
        body, name=name, grid=(r // br,), in_specs=[spec] * 4, out_specs=[spec] * 3,
        out_shape=[shape] * 3, compiler_params=_params(("arbitrary",)),
    )(w, g, m, v)


def _sum_adamw(chip, got, slot, w, m, v, name, after):
    rows, cdim = w.shape
    bc = 2 * LANE if cdim % (2 * LANE) == 0 else cdim

    def body(s_ref, c_ref, g_ref, w_ref, m_ref, v_ref, _, go_ref, d_ref, nm_ref, nv_ref):
        g = c_ref[0].astype(F32)
        for k in range(3):
            g = g + g_ref[k].astype(F32)
        nm = ADAM_B1 * m_ref[...] + (1.0 - ADAM_B1) * g
        nv = ADAM_B2 * v_ref[...] + (1.0 - ADAM_B2) * (g * g)
        m_hat = nm / (1.0 - ADAM_B1 ** ADAM_STEP)
        v_hat = nv / (1.0 - ADAM_B2 ** ADAM_STEP)
        go_ref[...] = g
        d_ref[...] = -ADAM_LR * (m_hat / (jnp.sqrt(v_hat) + ADAM_EPS) + ADAM_WD * w_ref[...])
        nm_ref[...] = nm
        nv_ref[...] = nv

    spec = pl.BlockSpec((rows, bc), lambda j, s_ref: (0, j))
    shape = jax.ShapeDtypeStruct((rows, cdim), F32)
    return pl.pallas_call(
        body, name=name,
        grid_spec=pltpu.PrefetchScalarGridSpec(
            num_scalar_prefetch=1, grid=(cdim // bc,),
            in_specs=[pl.BlockSpec((1, rows, bc), lambda j, s_ref: (s_ref[0], 0, j)),
                      pl.BlockSpec((3, rows, bc), lambda j, s_ref: (0, 0, j)), spec, spec, spec, ANY],
            out_specs=[spec] * 4),
        out_shape=[shape] * 4,
        compiler_params=_params(("arbitrary",)),
    )(slot, chip, got, w, m, v, after)


WEIGHTS = ("ffn1_norm_g", "ffn1_w_gate", "ffn1_w_up", "ffn1_w_down", "mix_norm_g", "w_in", "gmlp_v_norm_g",
           "gmlp_w_s", "gmlp_b_s", "mla_q_norm_g", "mla_w_q_up", "mla_kv_norm_g", "mla_w_kv_up", "mla_q_head_g",
           "mla_k_head_g", "gmlp_out_g", "mla_out_g", "w_out", "ffn2_norm_g", "ffn2_w_gate", "ffn2_w_up",
           "ffn2_w_down")
SHARDED = {"ffn1_w_gate": True, "ffn1_w_up": True, "ffn1_w_down": False, "w_in": True, "mla_w_q_up": True,
           "mla_w_kv_up": True, "w_out": False, "ffn2_w_gate": True, "ffn2_w_up": True, "ffn2_w_down": False}


def _col_block(m, target):
    best = LANE
    for cand in range(LANE, min(m, target) + 1, LANE):
        if m % cand == 0:
            best = cand
    return best


def _shard_rows(w, transposed, pad_to=None):
    rows = (w[0].T if transposed else w[0]).astype(BF16)
    if pad_to is not None and pad_to != rows.shape[0]:
        rows = jnp.pad(rows, ((0, pad_to - rows.shape[0]), (0, 0)))
    return rows


def _pack(parts):
    flat = []
    for p in parts:
        f = p.reshape(-1).astype(F32)
        flat.append(jnp.pad(f, (0, _round_up(f.size, LANE) - f.size)))
    flat = jnp.concatenate(flat)
    rows = _round_up(flat.size // LANE, SUBLANE)
    return jnp.pad(flat, (0, rows * LANE - flat.size)).reshape(rows, LANE)


def _unpack(packed, shapes):
    out, row = [], 0
    for shp in shapes:
        size = 1
        for s in shp:
            size *= s
        nrows = _round_up(size, LANE) // LANE
        out.append(packed[row:row + nrows].reshape(-1)[:size].reshape(shp))
        row += nrows
    return out


def kernel(x, positions, ffn1_norm_g, ffn1_w_gate, ffn1_w_up, ffn1_w_down, mix_norm_g, w_in, gmlp_v_norm_g, gmlp_w_s, gmlp_b_s, mla_q_norm_g, mla_w_q_up, mla_kv_norm_g, mla_w_kv_up, mla_q_head_g, mla_k_head_g, gmlp_out_g, mla_out_g, w_out, ffn2_norm_g, ffn2_w_gate, ffn2_w_up, ffn2_w_down, loss_target, m_ffn1_norm_g, m_ffn1_w_gate, m_ffn1_w_up, m_ffn1_w_down, m_mix_norm_g, m_w_in, m_gmlp_v_norm_g, m_gmlp_w_s, m_gmlp_b_s, m_mla_q_norm_g, m_mla_w_q_up, m_mla_kv_norm_g, m_mla_w_kv_up, m_mla_q_head_g, m_mla_k_head_g, m_gmlp_out_g, m_mla_out_g, m_w_out, m_ffn2_norm_g, m_ffn2_w_gate, m_ffn2_w_up, m_ffn2_w_down, v_ffn1_norm_g, v_ffn1_w_gate, v_ffn1_w_up, v_ffn1_w_down, v_mix_norm_g, v_w_in, v_gmlp_v_norm_g, v_gmlp_w_s, v_gmlp_b_s, v_mla_q_norm_g, v_mla_w_q_up, v_mla_kv_norm_g, v_mla_w_kv_up, v_mla_q_head_g, v_mla_k_head_g, v_gmlp_out_g, v_mla_out_g, v_w_out, v_ffn2_norm_g, v_ffn2_w_gate, v_ffn2_w_up, v_ffn2_w_down):
    wts = dict(zip(WEIGHTS, (ffn1_norm_g, ffn1_w_gate, ffn1_w_up, ffn1_w_down, mix_norm_g, w_in, gmlp_v_norm_g, gmlp_w_s, gmlp_b_s, mla_q_norm_g, mla_w_q_up, mla_kv_norm_g, mla_w_kv_up, mla_q_head_g, mla_k_head_g, gmlp_out_g, mla_out_g, w_out, ffn2_norm_g, ffn2_w_gate, ffn2_w_up, ffn2_w_down)))
    mom1 = dict(zip(WEIGHTS, (m_ffn1_norm_g, m_ffn1_w_gate, m_ffn1_w_up, m_ffn1_w_down, m_mix_norm_g, m_w_in, m_gmlp_v_norm_g, m_gmlp_w_s, m_gmlp_b_s, m_mla_q_norm_g, m_mla_w_q_up, m_mla_kv_norm_g, m_mla_w_kv_up, m_mla_q_head_g, m_mla_k_head_g, m_gmlp_out_g, m_mla_out_g, m_w_out, m_ffn2_norm_g, m_ffn2_w_gate, m_ffn2_w_up, m_ffn2_w_down)))
    mom2 = dict(zip(WEIGHTS, (v_ffn1_norm_g, v_ffn1_w_gate, v_ffn1_w_up, v_ffn1_w_down, v_mix_norm_g, v_w_in, v_gmlp_v_norm_g, v_gmlp_w_s, v_gmlp_b_s, v_mla_q_norm_g, v_mla_w_q_up, v_mla_kv_norm_g, v_mla_w_kv_up, v_mla_q_head_g, v_mla_k_head_g, v_gmlp_out_g, v_mla_out_g, v_w_out, v_ffn2_norm_g, v_ffn2_w_gate, v_ffn2_w_up, v_ffn2_w_down)))

    b_loc, seq, d = x.shape
    t = b_loc * seq
    ffs = ffn1_w_gate.shape[2]
    fp = _round_up(ffs, LANE)
    wg = gmlp_v_norm_g.shape[1]
    groups = gmlp_w_s.shape[1]
    rq, rkv = mla_q_norm_g.shape[1], mla_kv_norm_g.shape[1]
    heads = mla_out_g.shape[1]
    assert w_in.shape[2] * N_DEV == 2 * wg + rq + rkv + ROPE and mla_w_kv_up.shape[2] * N_DEV == heads * HEADW
    tm = min(512, t)
    tm_mix = min(256, t)
    blk = min(256, seq)

    xf = x.reshape(t, d)
    target = loss_target.reshape(t, d)
    pos = positions.reshape(t, 1).astype(F32)
    half = ROPE // 2
    inv_freq = 1.0 / (ROPE_THETA ** (jnp.arange(half, dtype=F32) / half))
    freq = jnp.concatenate([inv_freq, inv_freq, jnp.zeros((LANE - ROPE,), F32)])[None, :]
    lane = jnp.arange(LANE)
    masks = jnp.stack([jnp.where(lane < half, -1.0, 0.0),
                       jnp.where((lane >= half) & (lane < ROPE), 1.0, 0.0)]).astype(F32)
    gqh = jnp.pad(mla_q_head_g, ((0, 0), (0, HEADW - QK)))
    gkh = jnp.pad(mla_k_head_g, ((0, 0), (0, HEADW - QK)))
    bias = jnp.repeat(gmlp_b_s[0].T, CHUNK, axis=1)
    gouta = gmlp_out_g.reshape(1, wg)
    goutb = mla_out_g.reshape(1, heads * VHEAD)
    ws = gmlp_w_s[0]

    px, py, pc = _place()
    me = 4 * px + 2 * py + pc
    core = pc.astype(jnp.int32).reshape(1)
    slot = (2 * px + py).astype(jnp.int32).reshape(1)
    order = [n for n in WEIGHTS if n in SHARDED]
    group = {"ffn1": [n for n in order if n.startswith("ffn1")], "ffn2": [n for n in order if n.startswith("ffn2")],
             "mix": [n for n in order if not n.startswith("ffn")]}
    shard = {n: _shard_rows(wts[n], SHARDED[n]) for n in group["ffn1"]}
    frows = ffs if ffs != fp else None

    def tied(arr, token):
        return arr + token[0, 0].astype(arr.dtype)

    xnb, ynb, dgn = 4 * (1 - px) + 2 * py, 4 * px + 2 * (1 - py), 4 * (1 - px) + 2 * (1 - py)
    ids_a = jnp.stack([me, 4 * px + 2 * py + (1 - pc)]).astype(jnp.int32)
    ids_b = jnp.stack([xnb, xnb + 1, ynb, ynb + 1]).astype(jnp.int32)
    ids_c = jnp.stack([dgn, dgn + 1]).astype(jnp.int32)
    g1 = _StagedGather([shard[n] for n in group["ffn1"]], me, "gather_ffn1", pad_to=fp)
    token = g1.start("own")
    token = g1.start("nbr", deps=(token,))
    for n in group["mix"] + group["ffn2"]:
        shard[n] = _shard_rows(tied(wts[n], token), SHARDED[n])
    g3 = _StagedGather([shard[n] for n in group["ffn2"]], me, "gather_ffn2", pad_to=fp)
    g1.wait("own", token)
    x1, xn1, kept1 = _ffn_fwd(xf, None, ffn1_norm_g, ids_a, *g1.lands(), None, tm, "ffn1_fwd_a")
    g1.wait("nbr", x1)
    token = g1.start("diag")
    ici2 = _gather_ici([shard[n] for n in group["mix"]], me, "gather_mix_ici", deps=(token,))
    token = g3.start("nbr", deps=(ici2.token,))
    token = g1.start("nbr_d2d", deps=(token,))
    g1.wait("nbr_d2d", token)
    x1, xn1, kept1 = _ffn_fwd(x1, xn1, None, ids_b, *g1.lands(), kept1, tm, "ffn1_fwd_b")
    g1.wait("diag", x1)
    token = g1.start("diag_d2d")
    d2d2 = _gather_d2d(ici2.wait(x1)[len(group["mix"]):], "gather_mix_d2d", deps=(token,))
    g1.wait("diag_d2d", d2d2.token)
    full = dict(zip(group["ffn1"], g1.lands()))
    x1, xn1, (gd1, sl1, h1) = _ffn_fwd(x1, xn1, None, ids_c, full["ffn1_w_gate"], full["ffn1_w_up"],
                                       full["ffn1_w_down"], kept1, tm, "ffn1_fwd_c")
    full.update(zip(group["mix"], d2d2.wait(x1)))
    win_t = full["w_in"].reshape(-1, d)
    splits = (2 * wg, rq, rkv, LANE)
    wq_t = jnp.pad(full["mla_w_q_up"].reshape(heads, QK, rq), ((0, 0), (0, HEADW - QK), (0, 0)))
    wkv_t = full["mla_w_kv_up"].reshape(heads, HEADW, rkv)
    wout = full["w_out"].reshape(-1, d)
    hn, zuv, cq, ckv, krw = _inproj_fwd(x1, mix_norm_g, win_t, splits, tm)
    ya = _gmlp_fwd(zuv, gmlp_v_norm_g, ws, bias, gouta, tm_mix)
    g3.wait("nbr", ya)
    token = g3.start("diag")
    token = g3.start("own_nbr_d2d", deps=(token,))
    q, k, vv = _mla_proj_fwd(cq, ckv, krw, pos, freq, masks, mla_q_norm_g, mla_kv_norm_g, wq_t, wkv_t,
                             tied(gqh, token), gkh, tm)
    o, lse = _attn_fwd(q, k, vv, seq, blk)
    g3.wait("diag", o)
    token = g3.start("diag_d2d")
    x2, ycat = _out_fwd(ya, o, tied(goutb, token), wout, x1, tm)
    g3.wait("own_nbr_d2d", x2)
    g3.wait("diag_d2d", x2)
    full.update(zip(group["ffn2"], g3.lands()))
    x3, xn2, (gd2, sl2, h2) = _ffn_fwd(x2, None, ffn2_norm_g, jnp.arange(N_DEV, dtype=jnp.int32),
                                       full["ffn2_w_gate"], full["ffn2_w_up"], full["ffn2_w_down"], None, tm,
                                       "ffn2_fwd")
    dx3, loss_part = _loss_head(x3, target, tm)

    outs_g, outs_d, outs_m, outs_v = {}, {}, {}, {}

    def finish(names, chip, got, after):
        for n, cp, gt in zip(names, chip, got):
            rows_of = (lambda a: a[0].T) if SHARDED[n] else (lambda a: a[0])
            res = _sum_adamw(cp, gt, slot, rows_of(wts[n]), rows_of(mom1[n]), rows_of(mom2[n]), "adamw_" + n, after)
            outs_g[n], outs_d[n], outs_m[n], outs_v[n] = [r.T[None] if SHARDED[n] else r[None] for r in res]
            after = res[3]
        return after

    def chip_sums(names, ex, after, rows=None):
        res = ex.wait(after)
        return [_pair_add(f, gt, core, "pair_add_" + n, rows)
                for n, f, gt in zip(names, res[:len(names)], res[len(names):])]

    tk = min(2048, t)
    grads = {}
    small = {}
    dx2, small["ffn2_norm_g"], da2, db2 = _ffn_bwd(
        dx3, x2, ffn2_norm_g, gd2, sl2, full["ffn2_w_gate"], full["ffn2_w_up"], full["ffn2_w_down"], tm, "ffn2_bwd")
    grads["ffn2_w_gate"] = _matmul_tn(da2, xn2, fp, d, tk, BF16, "dw_ffn2_gate").reshape(N_DEV, fp, d)
    grads["ffn2_w_up"] = _matmul_tn(db2, xn2, fp, d, tk, BF16, "dw_ffn2_up").reshape(N_DEV, fp, d)
    grads["ffn2_w_down"] = _matmul_tn(h2, dx3, fp, d, tk, BF16, "dw_ffn2_down", rhs_scale=0.5).reshape(
        N_DEV, fp, d)
    red_a2 = _reduce_d2d([grads[n] for n in group["ffn2"]], "reduce_ffn2_d2d", rows=frows)
    dya, do, delta, small["mla_out_g"] = _out_bwd(dx2, o, tied(goutb, red_a2.token), wout, wg, tm)
    grads["w_out"] = _matmul_tn(ycat, dx2, _col_block(ycat.shape[1], 768), d, tk, BF16, "dw_out").reshape(
        N_DEV, -1, d)
    chip2 = chip_sums(group["ffn2"], red_a2, dya, frows)
    red_b2 = _reduce_ici(chip2, "reduce_ffn2_ici", rows=frows)
    dq, dk, dv = _attn_bwd(q, k, vv, do, lse, delta, seq, blk, red_b2.token)
    (dcq, dckv, dkrw, dwq, dwkv, small["mla_q_norm_g"], small["mla_kv_norm_g"], dgqh, dgkh) = _mla_proj_bwd(
        dq, dk, dv, cq, ckv, krw, pos, freq, masks, mla_q_norm_g, mla_kv_norm_g, wq_t, wkv_t, gqh, gkh, tm)
    small["mla_q_head_g"], small["mla_k_head_g"] = dgqh[:, :QK], dgkh[:, :QK]
    grads["mla_w_q_up"] = dwq[:, :QK].astype(BF16).reshape(N_DEV, -1, rq)
    grads["mla_w_kv_up"] = dwkv.astype(BF16).reshape(N_DEV, -1, rkv)
    dzuv, small["gmlp_w_s"], dbs, small["gmlp_v_norm_g"], small["gmlp_out_g"] = _gmlp_bwd(
        dya, zuv, gmlp_v_norm_g, ws, bias, gouta, tm_mix)
    small["gmlp_b_s"] = dbs[:, :, 0]
    dx1, small["mix_norm_g"], dzc = _inproj_bwd([dzuv, dcq, dckv, dkrw], x1, mix_norm_g, win_t, dx2, splits,
                                                tm_mix)
    grads["w_in"] = _matmul_tn(dzc, hn, _col_block(dzc.shape[1], 768), d, tk, BF16, "dw_in",
                               out_rows=win_t.shape[0]).reshape(N_DEV, -1, d)
    res_b2 = red_b2.wait(grads["w_in"])
    red_am = _reduce_d2d([grads[n] for n in group["mix"]], "reduce_mix_d2d")

    def ffn1_dw(n, lhs, rhs, scale, token):
        return _matmul_tn(lhs, rhs, fp, d, tk, BF16, "dw_" + n, rhs_scale=scale, deps=(token,)).reshape(N_DEV, fp, d)

    gr = ffn1_dw("ffn1_w_down", h1, dx1, 0.5, red_am.token)
    red_ad = _reduce_d2d([gr], "reduce_ffn1_w_down_d2d", deps=(red_am.token,), rows=frows)
    dx0, small["ffn1_norm_g"], da1, db1 = _ffn_bwd(
        dx1, xf, tied(ffn1_norm_g, red_ad.token), gd1, sl1, full["ffn1_w_gate"], full["ffn1_w_up"],
        full["ffn1_w_down"], tm, "ffn1_bwd")
    chipm = chip_sums(group["mix"], red_am, dx0)
    red_bm = _reduce_ici(chipm, "reduce_mix_ici")
    red_b = [("ffn1_w_down", _reduce_ici(chip_sums(["ffn1_w_down"], red_ad, dx0, frows), "reduce_ffn1_w_down_ici",
                                         deps=(red_bm.token,), rows=frows))]
    rep = [n for n in WEIGHTS if n not in SHARDED]
    small_ici = _gather_ici([_pack([small[n] for n in rep] + [loss_part])], me, "gather_small_ici",
                            deps=(red_b[-1][1].token,))
    gr = ffn1_dw("ffn1_w_gate", da1, xn1, None, small_ici.token)
    red_ag = _reduce_d2d([gr], "reduce_ffn1_w_gate_d2d", rows=frows)
    n2 = group["ffn2"]
    after = finish(n2[0:1], res_b2[0:1], res_b2[3:4], red_ag.token)
    red_b.append(("ffn1_w_gate", _reduce_ici(chip_sums(["ffn1_w_gate"], red_ag, after, frows),
                                             "reduce_ffn1_w_gate_ici", rows=frows)))
    gr = ffn1_dw("ffn1_w_up", db1, xn1, None, red_b[-1][1].token)
    small_d2d = _gather_d2d(small_ici.wait(gr)[1:], "gather_small_d2d")
    red_au = _reduce_d2d([gr], "reduce_ffn1_w_up_d2d", deps=(small_d2d.token,), rows=frows)
    after = finish(n2[1:2], res_b2[1:2], res_b2[4:5], red_au.token)
    red_b.append(("ffn1_w_up", _reduce_ici(chip_sums(["ffn1_w_up"], red_au, after, frows),
                                           "reduce_ffn1_w_up_ici", rows=frows)))
    after = finish(n2[2:3], res_b2[2:3], res_b2[5:6], red_b[-1][1].token)
    res = red_bm.wait(after)
    nm_ = len(group["mix"])
    after = finish(group["mix"], res[:nm_], res[nm_:], after)
    total = _sum_devices(small_d2d.wait(after)[0])
    zero = jnp.zeros((1,), F32)
    dlt, nm, nv = _adamw(_pack([wts[n] for n in rep] + [zero]), total, _pack([mom1[n] for n in rep] + [zero]),
                         _pack([mom2[n] for n in rep] + [zero]), "adamw_small")
    shapes = [wts[n].shape for n in rep] + [(1,)]
    for n, g, dl, m1, m2 in zip(rep, _unpack(total, shapes), _unpack(dlt, shapes), _unpack(nm, shapes),
                                _unpack(nv, shapes)):
        outs_g[n], outs_d[n], outs_m[n], outs_v[n] = g, dl, m1, m2
    loss = _unpack(total, shapes)[-1].reshape(())
    after = dlt
    for n, ex in red_b:
        res = ex.wait(after)
        after = finish([n], res[:1], res[1:], after)

    return (loss, dx0.reshape(b_loc, seq, d), *[outs_g[n] for n in WEIGHTS], *[outs_d[n] for n in WEIGHTS],
            *[outs_m[n] for n in WEIGHTS], *[outs_v[n] for n in WEIGHTS])
```

```python
import functools

import jax
import jax.numpy as jnp
from jax import lax
from jax.experimental import pallas as pl
from jax.experimental.pallas import tpu as pltpu

F32 = jnp.float32
BF16 = jnp.bfloat16
EPS = 1e-6
LANE = 128
SUBLANE = 8
N_DEV = 8
VMEM_LIMIT = 60 * 1024 * 1024
NOPE = 128
ROPE = 64
VHEAD = 128
QK = NOPE + ROPE
HEADW = 2 * LANE
CHUNK = 128
ROPE_THETA = 10000.0
ADAM_LR, ADAM_B1, ADAM_B2, ADAM_EPS, ADAM_WD, ADAM_STEP = 0.001, 0.9, 0.999, 1e-08, 0.01, 10
MESH = pl.DeviceIdType.MESH
ANY = pl.BlockSpec(memory_space=pl.ANY)
WHOLE_VMEM = pl.BlockSpec(memory_space=pltpu.VMEM)


def _params(sem=None):
    return pltpu.CompilerParams(dimension_semantics=sem, vmem_limit_bytes=VMEM_LIMIT)


def _round_up(n, m):
    return -(-n // m) * m


def _row_block(rows, target):
    best = rows
    for cand in range(SUBLANE, min(rows, target) + 1, SUBLANE):
        if rows % cand == 0:
            best = cand
    return best if best <= target else rows


def _nn(a, b):
    return jnp.dot(a, b, preferred_element_type=F32)


def _nt(a, b):
    return lax.dot_general(a, b, (((1,), (1,)), ((), ())), preferred_element_type=F32)


def _tn(a, b):
    return lax.dot_general(a, b, (((0,), (0,)), ((), ())), preferred_element_type=F32)


def _rstd(x, n):
    return lax.rsqrt(jnp.sum(x * x, axis=-1, keepdims=True) * (1.0 / n) + EPS)


def _rms_fwd(x, g, n):
    return x * _rstd(x, n) * g


def _rms_bwd(x, g, dy, n):
    r = _rstd(x, n)
    xh = x * r
    dyg = dy * g
    dx = r * (dyg - xh * (jnp.sum(dyg * xh, axis=-1, keepdims=True) * (1.0 / n)))
    return dx, jnp.sum(dy * xh, axis=0, keepdims=True)


def _gelu(x):
    return 0.5 * x * (1.0 + lax.erf(x * 0.7071067811865476))


def _gelu_grad(x):
    return 0.5 * (1.0 + lax.erf(x * 0.7071067811865476)) + x * jnp.exp(-0.5 * x * x) * 0.3989422804014327


def _ffn_fwd(base, xn, g, ids, wg_t, wu_t, wd, saved, tm, name):
    t, d = base.shape
    nb, fp, _ = wg_t.shape
    n = ids.shape[0]
    first = xn is None
    if saved is None:
        saved = [lax.empty((t, nb * fp), BF16) for _ in range(3)]

    def body(ids_ref, *refs):
        if first:
            base_ref, g_ref, wg_ref, wu_ref, wd_ref, _, _, _, out_ref, xn_ref, gd_ref, sl_ref, h_ref, acc = refs
        else:
            base_ref, xn_ref, wg_ref, wu_ref, wd_ref, _, _, _, out_ref, gd_ref, sl_ref, h_ref, acc = refs
        j = pl.program_id(1)

        @pl.when(j == 0)
        def _():
            if first:
                xn_ref[...] = _rms_fwd(base_ref[...], g_ref[...], d).astype(BF16)
            acc[...] = jnp.zeros_like(acc)

        xnb = xn_ref[...]
        a = _nt(xnb, wg_ref[0])
        b = _nt(xnb, wu_ref[0])
        s = jax.nn.sigmoid(a)
        sl = a * s
        h = (sl * b).astype(BF16)
        gd_ref[...] = (b * (s * (1.0 + a * (1.0 - s)))).astype(BF16)
        sl_ref[...] = sl.astype(BF16)
        h_ref[...] = h
        acc[...] += _nn(h, wd_ref[0])

        @pl.when(j == n - 1)
        def _():
            out_ref[...] = base_ref[...] + 0.5 * acc[...]

    wspec = pl.BlockSpec((1, fp, d), lambda i, j, ids_ref: (ids_ref[j], 0, 0))
    row = pl.BlockSpec((tm, d), lambda i, j, ids_ref: (i, 0))
    ff = pl.BlockSpec((tm, fp), lambda i, j, ids_ref: (i, ids_ref[j]))
    ffs = jax.ShapeDtypeStruct((t, nb * fp), BF16)
    second = pl.BlockSpec((1, d), lambda i, j, ids_ref: (0, 0)) if first else row
    n_row_outs = 2 if first else 1
    res = pl.pallas_call(
        body, name=name,
        grid_spec=pltpu.PrefetchScalarGridSpec(
            num_scalar_prefetch=1, grid=(t // tm, n),
            in_specs=[row, second, wspec, wspec, wspec, ANY, ANY, ANY],
            out_specs=[row] * n_row_outs + [ff, ff, ff],
            scratch_shapes=[pltpu.VMEM((tm, d), F32)]),
        out_shape=[jax.ShapeDtypeStruct((t, d), F32)] + ([jax.ShapeDtypeStruct((t, d), BF16)] if first else [])
        + [ffs, ffs, ffs],
        input_output_aliases={6 + k: n_row_outs + k for k in range(3)},
        compiler_params=_params(("arbitrary", "arbitrary")),
    )(ids, base, g if first else xn, wg_t, wu_t, wd, *saved)
    return (res[0], res[1] if first else xn, list(res[n_row_outs:]))


def _ffn_bwd(dout, x, g, gd, sl, wg_t, wu_t, wd, tm, name):
    t, d = x.shape
    nb, fp, _ = wg_t.shape

    def body(do_hbm, x_hbm, g_ref, gd_ref, sl_ref, wg_ref, wu_ref, wd_ref, wd_next_ref,
             dx_hbm, dg_ref, da_ref, db_ref, acc, rowbuf, dy_scr, dh_scr, sem):
        i, j = pl.program_id(0), pl.program_id(1)
        rows = pl.ds(pl.multiple_of(i * tm, tm), tm)
        get_do = pltpu.make_async_copy(do_hbm.at[rows, :], rowbuf, sem)
        get_x = pltpu.make_async_copy(x_hbm.at[rows, :], rowbuf, sem)

        @pl.when(j == 0)
        def _():
            get_do.start()
            get_do.wait()
            dy_scr[...] = (0.5 * rowbuf[...]).astype(BF16)
            acc[...] = jnp.zeros_like(acc)
            dh_scr[0] = _nt(dy_scr[...], wd_ref[0])
            get_x.start()

        @pl.when((i == 0) & (j == 0))
        def _():
            dg_ref[...] = jnp.zeros_like(dg_ref)

        dh = dh_scr[j % 2]
        dh_scr[(j + 1) % 2] = _nt(dy_scr[...], wd_next_ref[0])
        da = (dh * gd_ref[...].astype(F32)).astype(BF16)
        db = (dh * sl_ref[...].astype(F32)).astype(BF16)
        da_ref[...] = da
        db_ref[...] = db
        acc[...] += _nn(da, wg_ref[0]) + _nn(db, wu_ref[0])

        @pl.when(j == nb - 1)
        def _():
            get_x.wait()
            dxn, dg = _rms_bwd(rowbuf[...], g_ref[...], acc[...], d)
            dg_ref[...] += dg
            acc[...] = dxn
            get_do.start()
            get_do.wait()
            acc[...] += rowbuf[...]
            out = pltpu.make_async_copy(acc, dx_hbm.at[rows, :], sem)
            out.start()
            out.wait()

    wspec = pl.BlockSpec((1, fp, d), lambda i, j: (j, 0, 0))
    wnext = pl.BlockSpec((1, fp, d), lambda i, j: (jnp.minimum(j + 1, nb - 1), 0, 0))
    vec = pl.BlockSpec((1, d), lambda i, j: (0, 0))
    ff = pl.BlockSpec((tm, fp), lambda i, j: (i, j))
    ffs = jax.ShapeDtypeStruct((t, nb * fp), BF16)
    return pl.pallas_call(
        body, name=name, grid=(t // tm, nb),
        in_specs=[ANY, ANY, vec, ff, ff, wspec, wspec, pl.BlockSpec((1, fp, d), lambda i, j: (0, 0, 0)), wnext],
        out_specs=[ANY, vec, ff, ff],
        out_shape=[jax.ShapeDtypeStruct((t, d), F32), jax.ShapeDtypeStruct((1, d), F32), ffs, ffs],
        scratch_shapes=[pltpu.VMEM((tm, d), F32), pltpu.VMEM((tm, d), F32), pltpu.VMEM((tm, d), BF16),
                        pltpu.VMEM((2, tm, fp), F32), pltpu.SemaphoreType.DMA],
        compiler_params=_params(("arbitrary", "arbitrary")),
    )(dout, x, g, gd, sl, wg_t, wu_t, wd, wd)


def _matmul_tn(lhs, rhs, bm, bn, tk, out_dtype, name, rhs_scale=None, deps=(), out_rows=None):
    t, m = lhs.shape
    n = rhs.shape[1]
    nk = t // tk
    out_rows = m if out_rows is None else out_rows

    def body(l_ref, r_ref, *refs):
        o_ref, acc = refs[len(deps):]
        k = pl.program_id(2)

        @pl.when(k == 0)
        def _():
            acc[...] = jnp.zeros_like(acc)

        r = r_ref[...] if rhs_scale is None else rhs_scale * r_ref[...]
        acc[...] += _tn(l_ref[...].astype(BF16), r.astype(BF16))

        @pl.when(k == nk - 1)
        def _():
            o_ref[...] = acc[...].astype(out_dtype)

    return pl.pallas_call(
        body, name=name, grid=(m // bm, n // bn, nk),
        in_specs=[pl.BlockSpec((tk, bm), lambda i, j, k: (k, i)), pl.BlockSpec((tk, bn), lambda i, j, k: (k, j))]
        + [ANY] * len(deps),
        out_specs=pl.BlockSpec((bm, bn), lambda i, j, k: (i, j)),
        out_shape=jax.ShapeDtypeStruct((out_rows, n), out_dtype),
        scratch_shapes=[pltpu.VMEM((bm, bn), F32)],
        compiler_params=_params(("arbitrary", "arbitrary", "arbitrary")),
    )(lhs, rhs, *deps)


def _matmul_tn_resident(lhs, rhs, bm, out_dtype, name, deps=()):
    t, m = lhs.shape
    n = rhs.shape[1]

    def body(l_ref, r_ref, *refs):
        refs[len(deps)][...] = _tn(l_ref[...], r_ref[...]).astype(out_dtype)

    return pl.pallas_call(
        body, name=name, grid=(m // bm,),
        in_specs=[pl.BlockSpec((t, bm), lambda i: (0, i)), WHOLE_VMEM] + [ANY] * len(deps),
        out_specs=pl.BlockSpec((bm, n), lambda i: (i, 0)),
        out_shape=jax.ShapeDtypeStruct((m, n), out_dtype),
        compiler_params=_params(("arbitrary",)),
    )(lhs, rhs, *deps)


def _last_rows_padded(w_ref, tail_ref, off, real):
    @pl.when(pl.program_id(0) == 0)
    def _():
        tail_ref[...] = jnp.zeros_like(tail_ref)
        tail_ref[0:real, :] = w_ref[off:off + real, :]


def _inproj_fwd(x, g, w_t, splits, tm):
    t, d = x.shape
    offs = [sum(splits[:k]) for k in range(len(splits))]
    real_last = w_t.shape[0] - offs[-1]

    def body(x_ref, g_ref, w_ref, hn_ref, *refs):
        z_refs, tail_ref = refs[:-1], refs[-1]
        _last_rows_padded(w_ref, tail_ref, offs[-1], real_last)
        hn = _rms_fwd(x_ref[...], g_ref[...], d).astype(BF16)
        hn_ref[...] = hn
        for z_ref, o, n in zip(z_refs[:-1], offs, splits):
            z_ref[...] = _nt(hn, w_ref[o:o + n, :])
        z_refs[-1][...] = _nt(hn, tail_ref[...])

    row = pl.BlockSpec((tm, d), lambda i: (i, 0))
    return pl.pallas_call(
        body, name="inproj_fwd", grid=(t // tm,),
        in_specs=[row, pl.BlockSpec((1, d), lambda i: (0, 0)), WHOLE_VMEM],
        out_specs=[row] + [pl.BlockSpec((tm, n), lambda i: (i, 0)) for n in splits],
        out_shape=[jax.ShapeDtypeStruct((t, d), BF16)] + [jax.ShapeDtypeStruct((t, n), F32) for n in splits],
        scratch_shapes=[pltpu.VMEM((splits[-1], d), BF16)],
        compiler_params=_params(("arbitrary",)),
    )(x, g, w_t)


def _inproj_bwd(dzs, x, g, w_t, dres, splits, tm):
    t, d = x.shape
    offs = [sum(splits[:k]) for k in range(len(splits))]
    ni = sum(splits)
    nz = len(splits)
    real_last = w_t.shape[0] - offs[-1]

    def body(*refs):
        dz_refs = refs[:nz]
        x_ref, g_ref, w_ref, dres_ref, dx_ref, dg_ref, dzc_ref, tail_ref = refs[nz:]
        _last_rows_padded(w_ref, tail_ref, offs[-1], real_last)
        dhn = jnp.zeros((tm, d), F32)
        for k, (dz_ref, o, n) in enumerate(zip(dz_refs, offs, splits)):
            dz = dz_ref[...].astype(BF16)
            dzc_ref[:, o:o + n] = dz
            dhn += _nn(dz, tail_ref[...] if k == nz - 1 else w_ref[o:o + n, :])
        dx, dg = _rms_bwd(x_ref[...], g_ref[...], dhn, d)
        dx_ref[...] = dres_ref[...] + dx

        @pl.when(pl.program_id(0) == 0)
        def _():
            dg_ref[...] = jnp.zeros_like(dg_ref)

        dg_ref[...] += dg

    row = pl.BlockSpec((tm, d), lambda i: (i, 0))
    vec = pl.BlockSpec((1, d), lambda i: (0, 0))
    return pl.pallas_call(
        body, name="inproj_bwd", grid=(t // tm,),
        in_specs=[pl.BlockSpec((tm, n), lambda i: (i, 0)) for n in splits] + [row, vec, WHOLE_VMEM, row],
        out_specs=[row, vec, pl.BlockSpec((tm, ni), lambda i: (i, 0))],
        out_shape=[jax.ShapeDtypeStruct((t, d), F32), jax.ShapeDtypeStruct((1, d), F32),
                   jax.ShapeDtypeStruct((t, ni), BF16)],
        scratch_shapes=[pltpu.VMEM((splits[-1], d), BF16)],
        compiler_params=_params(("arbitrary",)),
    )(*dzs, x, g, w_t, dres)


def _tril_bf16(ws_ref, grp):
    rows = lax.broadcasted_iota(jnp.int32, (CHUNK, CHUNK), 0)
    cols = lax.broadcasted_iota(jnp.int32, (CHUNK, CHUNK), 1)
    return jnp.where(rows >= cols, ws_ref[grp], 0.0).astype(BF16)


def _gmlp_mix(zuv_ref, gv_ref, ws_ref, bias_ref, v_scr, mixed_scr, tm, w, groups):
    u = _gelu(zuv_ref[:, 0:w])
    v0 = _gelu(zuv_ref[:, w:2 * w])
    v_scr[...] = _rms_fwd(v0, gv_ref[...], w).astype(BF16)
    for grp in range(groups):
        wsm = _tril_bf16(ws_ref, grp)
        lanes = slice(grp * CHUNK, (grp + 1) * CHUNK)
        for c in range(tm // CHUNK):
            rows = slice(c * CHUNK, (c + 1) * CHUNK)
            mixed_scr[rows, lanes] = _nn(wsm, v_scr[rows, lanes]) + bias_ref[:, lanes]
    return u, v0


def _gmlp_fwd(zuv, gv, ws, bias, gout, tm):
    t, w2 = zuv.shape
    w = w2 // 2
    groups = ws.shape[0]

    def body(zuv_ref, gv_ref, ws_ref, bias_ref, go_ref, y_ref, v_scr, mixed_scr):
        u, _ = _gmlp_mix(zuv_ref, gv_ref, ws_ref, bias_ref, v_scr, mixed_scr, tm, w, groups)
        ya = u * mixed_scr[...]
        for grp in range(groups):
            lanes = slice(grp * CHUNK, (grp + 1) * CHUNK)
            y_ref[:, lanes] = _rms_fwd(ya[:, lanes], go_ref[:, lanes], CHUNK).astype(BF16)

    const2 = lambda i: (0, 0)
    return pl.pallas_call(
        body, name="gmlp_fwd", grid=(t // tm,),
        in_specs=[pl.BlockSpec((tm, w2), lambda i: (i, 0)), pl.BlockSpec((1, w), const2),
                  pl.BlockSpec((groups, CHUNK, CHUNK), lambda i: (0, 0, 0)),
                  pl.BlockSpec((CHUNK, w), const2), pl.BlockSpec((1, w), const2)],
        out_specs=pl.BlockSpec((tm, w), lambda i: (i, 0)),
        out_shape=jax.ShapeDtypeStruct((t, w), BF16),
        scratch_shapes=[pltpu.VMEM((tm, w), BF16), pltpu.VMEM((tm, w), F32)],
        compiler_params=_params(("arbitrary",)),
    )(zuv, gv, ws, bias, gout)


def _gmlp_bwd(dy, zuv, gv, ws, bias, gout, tm):
    t, w2 = zuv.shape
    w = w2 // 2
    groups = ws.shape[0]

    def body(dy_ref, zuv_ref, gv_ref, ws_ref, bias_ref, go_ref,
             dz_ref, dws_ref, dbias_ref, dgv_ref, dgo_ref, v_scr, mixed_scr, dmix_scr, dv_scr):
        @pl.when(pl.program_id(0) == 0)
        def _():
            dws_ref[...] = jnp.zeros_like(dws_ref)
            dbias_ref[...] = jnp.zeros_like(dbias_ref)
            dgv_ref[...] = jnp.zeros_like(dgv_ref)
            dgo_ref[...] = jnp.zeros_like(dgo_ref)

        u, v0 = _gmlp_mix(zuv_ref, gv_ref, ws_ref, bias_ref, v_scr, mixed_scr, tm, w, groups)
        mixed = mixed_scr[...]
        ya = u * mixed
        for grp in range(groups):
            lanes = slice(grp * CHUNK, (grp + 1) * CHUNK)
            dya, dgo = _rms_bwd(ya[:, lanes], go_ref[:, lanes], dy_ref[:, lanes], CHUNK)
            dgo_ref[:, lanes] += dgo
            dz_ref[:, lanes] = dya * mixed[:, lanes] * _gelu_grad(zuv_ref[:, lanes])
            dmix_scr[:, lanes] = dya * u[:, lanes]
        for grp in range(groups):
            wsm = _tril_bf16(ws_ref, grp)
            lanes = slice(grp * CHUNK, (grp + 1) * CHUNK)
            dws = jnp.zeros((CHUNK, CHUNK), F32)
            dbias = jnp.zeros((CHUNK, CHUNK), F32)
            for c in range(tm // CHUNK):
                rows = slice(c * CHUNK, (c + 1) * CHUNK)
                dm = dmix_scr[rows, lanes]
                dmb = dm.astype(BF16)
                dv_scr[rows, lanes] = _tn(wsm, dmb)
                dws += _nt(dmb, v_scr[rows, lanes])
                dbias += dm
            rr = lax.broadcasted_iota(jnp.int32, (CHUNK, CHUNK), 0)
            cc = lax.broadcasted_iota(jnp.int32, (CHUNK, CHUNK), 1)
            dws_ref[grp] += jnp.where(rr >= cc, dws, 0.0)
            dbias_ref[grp] += jnp.sum(dbias, axis=1, keepdims=True)
        dv0, dgv = _rms_bwd(v0, gv_ref[...], dv_scr[...], w)
        dgv_ref[...] += dgv
        dz_ref[:, w:2 * w] = dv0 * _gelu_grad(zuv_ref[:, w:2 * w])

    const2 = lambda i: (0, 0)
    const3 = lambda i: (0, 0, 0)
    return pl.pallas_call(
        body, name="gmlp_bwd", grid=(t // tm,),
        in_specs=[pl.BlockSpec((tm, w), lambda i: (i, 0)), pl.BlockSpec((tm, w2), lambda i: (i, 0)),
                  pl.BlockSpec((1, w), const2), pl.BlockSpec((groups, CHUNK, CHUNK), const3),
                  pl.BlockSpec((CHUNK, w), const2), pl.BlockSpec((1, w), const2)],
        out_specs=[pl.BlockSpec((tm, w2), lambda i: (i, 0)), pl.BlockSpec((groups, CHUNK, CHUNK), const3),
                   pl.BlockSpec((groups, CHUNK, 1), const3), pl.BlockSpec((1, w), const2), pl.BlockSpec((1, w), const2)],
        out_shape=[jax.ShapeDtypeStruct((t, w2), F32), jax.ShapeDtypeStruct((groups, CHUNK, CHUNK), F32),
                   jax.ShapeDtypeStruct((groups, CHUNK, 1), F32), jax.ShapeDtypeStruct((1, w), F32),
                   jax.ShapeDtypeStruct((1, w), F32)],
        scratch_shapes=[pltpu.VMEM((tm, w), BF16), pltpu.VMEM((tm, w), F32),
                        pltpu.VMEM((tm, w), F32), pltpu.VMEM((tm, w), F32)],
        compiler_params=_params(("arbitrary",)),
    )(dy, zuv, gv, ws, bias, gout)


def _rot(x, m_lo, m_hi):
    return pltpu.roll(x, LANE - ROPE // 2, 1) * m_lo + pltpu.roll(x, ROPE // 2, 1) * m_hi


def _rope_tables(pos, freq, tm, after):
    t = pos.shape[0]

    def body(pos_ref, freq_ref, _, cos_ref, sin_ref):
        ang = pos_ref[...] * freq_ref[...]
        cos_ref[...] = jnp.cos(ang)
        sin_ref[...] = jnp.sin(ang)

    tab = pl.BlockSpec((tm, LANE), lambda i: (i, 0))
    return pl.pallas_call(
        body, name="rope_tables", grid=(t // tm,),
        in_specs=[pl.BlockSpec((tm, 1), lambda i: (i, 0)), pl.BlockSpec((1, LANE), lambda i: (0, 0)), ANY],
        out_specs=[tab, tab], out_shape=[jax.ShapeDtypeStruct((t, LANE), F32)] * 2,
        compiler_params=_params(("arbitrary",)),
    )(pos, freq, after)


def _mla_proj_fwd(cq, ckv, krw, cos, sin, masks, gq, gkv, wq_t, wkv_t, gqh, gkh, tm):
    t, rq = cq.shape
    rkv = ckv.shape[1]
    heads = wq_t.shape[0]

    def body(cq_ref, ckv_ref, kr_ref, cos_ref, sin_ref, mk_ref, gq_ref, gkv_ref, wq_ref, wkv_ref,
             gqh_ref, gkh_ref, q_ref, k_ref, v_ref):
        cos, sin = cos_ref[...], sin_ref[...]
        m_lo, m_hi = mk_ref[0:1, :], mk_ref[1:2, :]
        cqn = _rms_fwd(cq_ref[...], gq_ref[...], rq).astype(BF16)
        ckvn = _rms_fwd(ckv_ref[...], gkv_ref[...], rkv).astype(BF16)
        kr = kr_ref[...]
        kr_ss = jnp.sum(kr * kr, axis=-1, keepdims=True)
        for h in range(heads):
            qh = _nt(cqn, wq_ref[h])
            qn = qh * _rstd(qh, QK) * gqh_ref[...]
            qr = qn[:, LANE:]
            q_ref[h, :, 0:LANE] = qn[:, 0:LANE].astype(BF16)
            q_ref[h, :, LANE:] = (qr * cos + _rot(qr, m_lo, m_hi) * sin).astype(BF16)
            kvh = _nt(ckvn, wkv_ref[h])
            kn = kvh[:, 0:LANE]
            rk = lax.rsqrt((jnp.sum(kn * kn, axis=-1, keepdims=True) + kr_ss) * (1.0 / QK) + EPS)
            k_ref[h, :, 0:LANE] = (kn * rk * gkh_ref[:, 0:LANE]).astype(BF16)
            krn = kr * rk * gkh_ref[:, LANE:]
            k_ref[h, :, LANE:] = (krn * cos + _rot(krn, m_lo, m_hi) * sin).astype(BF16)
            v_ref[h] = kvh[:, LANE:].astype(BF16)

    c2 = lambda i: (0, 0)
    c3 = lambda i: (0, 0, 0)
    return pl.pallas_call(
        body, name="mla_proj_fwd", grid=(t // tm,),
        in_specs=[pl.BlockSpec((tm, rq), lambda i: (i, 0)), pl.BlockSpec((tm, rkv), lambda i: (i, 0)),
                  pl.BlockSpec((tm, LANE), lambda i: (i, 0)), pl.BlockSpec((tm, LANE), lambda i: (i, 0)),
                  pl.BlockSpec((tm, LANE), lambda i: (i, 0)), pl.BlockSpec((2, LANE), c2),
                  pl.BlockSpec((1, rq), c2), pl.BlockSpec((1, rkv), c2),
                  pl.BlockSpec((heads, HEADW, rq), c3), pl.BlockSpec((heads, HEADW, rkv), c3),
                  pl.BlockSpec((1, HEADW), c2), pl.BlockSpec((1, HEADW), c2)],
        out_specs=[pl.BlockSpec((heads, tm, HEADW), lambda i: (0, i, 0)),
                   pl.BlockSpec((heads, tm, HEADW), lambda i: (0, i, 0)),
                   pl.BlockSpec((heads, tm, VHEAD), lambda i: (0, i, 0))],
        out_shape=[jax.ShapeDtypeStruct((heads, t, HEADW), BF16), jax.ShapeDtypeStruct((heads, t, HEADW), BF16),
                   jax.ShapeDtypeStruct((heads, t, VHEAD), BF16)],
        compiler_params=_params(("arbitrary",)),
    )(cq, ckv, krw, cos, sin, masks, gq, gkv, wq_t, wkv_t, gqh, gkh)


def _mla_proj_bwd(dq, dk, dv, cq, ckv, krw, cos, sin, masks, gq, gkv, wq_t, wkv_t, gqh, gkh, tm):
    t, rq = cq.shape
    rkv = ckv.shape[1]
    heads = wq_t.shape[0]

    def body(dq_ref, dk_ref, dv_ref, cq_ref, ckv_ref, kr_ref, cos_ref, sin_ref, mk_ref, gq_ref, gkv_ref,
             wq_ref, wkv_ref, gqh_ref, gkh_ref,
             dcq_ref, dckv_ref, dkr_ref, dwq_ref, dwkv_ref, dgq_ref, dgkv_ref, dgqh_ref, dgkh_ref):
        @pl.when(pl.program_id(0) == 0)
        def _():
            for r in (dwq_ref, dwkv_ref, dgq_ref, dgkv_ref, dgqh_ref, dgkh_ref):
                r[...] = jnp.zeros_like(r)

        cos, sin = cos_ref[...], sin_ref[...]
        m_lo, m_hi = mk_ref[0:1, :], mk_ref[1:2, :]

        def unrope(dy):
            return dy * cos - _rot(dy * sin, m_lo, m_hi)

        cqn = _rms_fwd(cq_ref[...], gq_ref[...], rq).astype(BF16)
        ckvn = _rms_fwd(ckv_ref[...], gkv_ref[...], rkv).astype(BF16)
        kr = kr_ref[...]
        kr_ss = jnp.sum(kr * kr, axis=-1, keepdims=True)
        dcqn = jnp.zeros((tm, rq), F32)
        dckvn = jnp.zeros((tm, rkv), F32)
        dkr = jnp.zeros((tm, LANE), F32)
        for h in range(heads):
            qh = _nt(cqn, wq_ref[h])
            dqn = jnp.concatenate([dq_ref[h, :, 0:LANE], unrope(dq_ref[h, :, LANE:])], axis=1)
            dqh, dg = _rms_bwd(qh, gqh_ref[...], dqn, QK)
            dgqh_ref[...] += dg
            dqh = dqh.astype(BF16)
            dcqn += _nn(dqh, wq_ref[h])
            dwq_ref[h] += _tn(dqh, cqn)

            kvh = _nt(ckvn, wkv_ref[h])
            kn = kvh[:, 0:LANE]
            rk = lax.rsqrt((jnp.sum(kn * kn, axis=-1, keepdims=True) + kr_ss) * (1.0 / QK) + EPS)
            dkn_n = dk_ref[h, :, 0:LANE]
            dkr_n = unrope(dk_ref[h, :, LANE:])
            knh, krh = kn * rk, kr * rk
            dgkh_ref[:, 0:LANE] += jnp.sum(dkn_n * knh, axis=0, keepdims=True)
            dgkh_ref[:, LANE:] += jnp.sum(dkr_n * krh, axis=0, keepdims=True)
            dkn_g, dkr_g = dkn_n * gkh_ref[:, 0:LANE], dkr_n * gkh_ref[:, LANE:]
            proj = (jnp.sum(dkn_g * knh, axis=-1, keepdims=True)
                    + jnp.sum(dkr_g * krh, axis=-1, keepdims=True)) * (1.0 / QK)
            dkr += rk * (dkr_g - krh * proj)
            dkvh = jnp.concatenate([rk * (dkn_g - knh * proj), dv_ref[h]], axis=1).astype(BF16)
            dckvn += _nn(dkvh, wkv_ref[h])
            dwkv_ref[h] += _tn(dkvh, ckvn)
        dkr_ref[...] = dkr
        dcq, dg = _rms_bwd(cq_ref[...], gq_ref[...], dcqn, rq)
        dcq_ref[...] = dcq
        dgq_ref[...] += dg
        dckv, dg = _rms_bwd(ckv_ref[...], gkv_ref[...], dckvn, rkv)
        dckv_ref[...] = dckv
        dgkv_ref[...] += dg

    c2 = lambda i: (0, 0)
    c3 = lambda i: (0, 0, 0)
    hq = pl.BlockSpec((heads, tm, HEADW), lambda i: (0, i, 0))
    return pl.pallas_call(
        body, name="mla_proj_bwd", grid=(t // tm,),
        in_specs=[hq, hq, pl.BlockSpec((heads, tm, VHEAD), lambda i: (0, i, 0)),
                  pl.BlockSpec((tm, rq), lambda i: (i, 0)), pl.BlockSpec((tm, rkv), lambda i: (i, 0)),
                  pl.BlockSpec((tm, LANE), lambda i: (i, 0)), pl.BlockSpec((tm, LANE), lambda i: (i, 0)),
                  pl.BlockSpec((tm, LANE), lambda i: (i, 0)), pl.BlockSpec((2, LANE), c2),
                  pl.BlockSpec((1, rq), c2), pl.BlockSpec((1, rkv), c2),
                  pl.BlockSpec((heads, HEADW, rq), c3), pl.BlockSpec((heads, HEADW, rkv), c3),
                  pl.BlockSpec((1, HEADW), c2), pl.BlockSpec((1, HEADW), c2)],
        out_specs=[pl.BlockSpec((tm, rq), lambda i: (i, 0)), pl.BlockSpec((tm, rkv), lambda i: (i, 0)),
                   pl.BlockSpec((tm, LANE), lambda i: (i, 0)),
                   pl.BlockSpec((heads, HEADW, rq), c3), pl.BlockSpec((heads, HEADW, rkv), c3),
                   pl.BlockSpec((1, rq), c2), pl.BlockSpec((1, rkv), c2),
                   pl.BlockSpec((1, HEADW), c2), pl.BlockSpec((1, HEADW), c2)],
        out_shape=[jax.ShapeDtypeStruct((t, rq), F32), jax.ShapeDtypeStruct((t, rkv), F32),
                   jax.ShapeDtypeStruct((t, LANE), F32),
                   jax.ShapeDtypeStruct((heads, HEADW, rq), F32), jax.ShapeDtypeStruct((heads, HEADW, rkv), F32),
                   jax.ShapeDtypeStruct((1, rq), F32), jax.ShapeDtypeStruct((1, rkv), F32),
                   jax.ShapeDtypeStruct((1, HEADW), F32), jax.ShapeDtypeStruct((1, HEADW), F32)],
        compiler_params=_params(("arbitrary",)),
    )(dq, dk, dv, cq, ckv, krw, cos, sin, masks, gq, gkv, wq_t, wkv_t, gqh, gkh)


def _lower_triangle(blk):
    return lax.broadcasted_iota(jnp.int32, (blk, blk), 0) >= lax.broadcasted_iota(jnp.int32, (blk, blk), 1)


def _attn_fwd(q, k, v, seq, blk):
    heads, t, _ = q.shape
    scale = QK ** -0.5
    nblk = seq // blk

    def body(q_ref, k_ref, v_ref, o_ref, lse_ref):
        tri = _lower_triangle(blk)
        for qi in range(nblk):
            rows = slice(qi * blk, (qi + 1) * blk)
            before = slice(0, qi * blk)
            qb = q_ref[0, rows, :]
            s_d = jnp.where(tri, _nt(qb, k_ref[0, rows, :]) * scale, -1e30)
            m = jnp.max(s_d, axis=-1, keepdims=True)
            if qi:
                s_b = _nt(qb, k_ref[0, before, :]) * scale
                m = jnp.maximum(m, jnp.max(s_b, axis=-1, keepdims=True))
                p_b = jnp.exp(s_b - m)
            p_d = jnp.exp(s_d - m)
            l = jnp.sum(p_d, axis=-1, keepdims=True)
            acc = _nn(p_d.astype(BF16), v_ref[0, rows, :])
            if qi:
                l += jnp.sum(p_b, axis=-1, keepdims=True)
                acc += _nn(p_b.astype(BF16), v_ref[0, before, :])
            o_ref[0, rows, :] = acc / l
            lse_ref[0, rows, :] = m + jnp.log(l)

    return pl.pallas_call(
        body, name="attn_fwd", grid=(heads, t // seq),
        in_specs=[pl.BlockSpec((1, seq, HEADW), lambda h, b: (h, b, 0)),
                  pl.BlockSpec((1, seq, HEADW), lambda h, b: (h, b, 0)),
                  pl.BlockSpec((1, seq, VHEAD), lambda h, b: (h, b, 0))],
        out_specs=[pl.BlockSpec((1, seq, VHEAD), lambda h, b: (h, b, 0)),
                   pl.BlockSpec((1, seq, 1), lambda h, b: (h, b, 0))],
        out_shape=[jax.ShapeDtypeStruct((heads, t, VHEAD), F32), jax.ShapeDtypeStruct((heads, t, 1), F32)],
        compiler_params=_params(("arbitrary", "arbitrary")),
    )(q, k, v)


def _attn_bwd(q, k, v, do, lse, delta, seq, blk, after):
    heads, t, _ = q.shape
    scale = QK ** -0.5
    nblk = seq // blk

    def body(q_ref, k_ref, v_ref, do_ref, lse_ref, dl_ref, _, dq_ref, dk_ref, dv_ref):
        tri = _lower_triangle(blk)
        dk_ref[...] = jnp.zeros_like(dk_ref)
        dv_ref[...] = jnp.zeros_like(dv_ref)
        for qi in range(nblk):
            rows = slice(qi * blk, (qi + 1) * blk)
            qb = q_ref[0, rows, :]
            dob = do_ref[0, rows, :]
            lse_b = lse_ref[0, rows, :]
            dl_b = dl_ref[0, rows, :]
            dq = jnp.zeros((blk, HEADW), F32)
            for keys, masked in ((slice(0, qi * blk), False), (rows, True)):
                if keys.stop == keys.start:
                    continue
                kb = k_ref[0, keys, :]
                p = jnp.exp(_nt(qb, kb) * scale - lse_b)
                if masked:
                    p = jnp.where(tri, p, 0.0)
                dp = _nt(dob, v_ref[0, keys, :])
                ds = (p * (dp - dl_b) * scale).astype(BF16)
                dv_ref[0, keys, :] += _tn(p.astype(BF16), dob)
                dk_ref[0, keys, :] += _tn(ds, qb)
                dq += _nn(ds, kb)
            dq_ref[0, rows, :] = dq

    hq = pl.BlockSpec((1, seq, HEADW), lambda h, b: (h, b, 0))
    hv = pl.BlockSpec((1, seq, VHEAD), lambda h, b: (h, b, 0))
    h1 = pl.BlockSpec((1, seq, 1), lambda h, b: (h, b, 0))
    return pl.pallas_call(
        body, name="attn_bwd", grid=(heads, t // seq),
        in_specs=[hq, hq, hv, hv, h1, h1, ANY],
        out_specs=[hq, hq, hv],
        out_shape=[jax.ShapeDtypeStruct((heads, t, HEADW), F32), jax.ShapeDtypeStruct((heads, t, HEADW), F32),
                   jax.ShapeDtypeStruct((heads, t, VHEAD), F32)],
        compiler_params=_params(("arbitrary", "arbitrary")),
    )(q, k, v, do, lse, delta, after)


def _out_fwd(ya, o, gb, w_out, x1, tm):
    t, w = ya.shape
    heads = o.shape[0]
    d = x1.shape[1]

    def body(ya_ref, o_ref, gb_ref, w_ref, x1_ref, x2_ref, yc_ref):
        yc_ref[:, 0:w] = ya_ref[...]
        for h in range(heads):
            lanes = slice(h * VHEAD, (h + 1) * VHEAD)
            yc_ref[:, w + h * VHEAD:w + (h + 1) * VHEAD] = _rms_fwd(o_ref[h], gb_ref[:, lanes], VHEAD).astype(BF16)
        x2_ref[...] = x1_ref[...] + _nn(yc_ref[...], w_ref[...])

    wy = w + heads * VHEAD
    row = pl.BlockSpec((tm, d), lambda i: (i, 0))
    return pl.pallas_call(
        body, name="out_fwd", grid=(t // tm,),
        in_specs=[pl.BlockSpec((tm, w), lambda i: (i, 0)), pl.BlockSpec((heads, tm, VHEAD), lambda i: (0, i, 0)),
                  pl.BlockSpec((1, heads * VHEAD), lambda i: (0, 0)), WHOLE_VMEM, row],
        out_specs=[row, pl.BlockSpec((tm, wy), lambda i: (i, 0))],
        out_shape=[jax.ShapeDtypeStruct((t, d), F32), jax.ShapeDtypeStruct((t, wy), BF16)],
        compiler_params=_params(("arbitrary",)),
    )(ya, o, gb, w_out, x1)


def _out_bwd(dx2, o, gb, w_out, w, tm):
    t, d = dx2.shape
    heads = o.shape[0]

    def body(dx_ref, o_ref, gb_ref, w_ref, dya_ref, do_ref, dl_ref, dgb_ref):
        @pl.when(pl.program_id(0) == 0)
        def _():
            dgb_ref[...] = jnp.zeros_like(dgb_ref)

        dyc = _nt(dx_ref[...].astype(BF16), w_ref[...])
        dya_ref[...] = dyc[:, 0:w]
        for h in range(heads):
            lanes = slice(h * VHEAD, (h + 1) * VHEAD)
            oh = o_ref[h]
            doh, dg = _rms_bwd(oh, gb_ref[:, lanes], dyc[:, w + h * VHEAD:w + (h + 1) * VHEAD], VHEAD)
            dgb_ref[:, lanes] += dg
            do_ref[h] = doh.astype(BF16)
            dl_ref[h] = jnp.sum(doh * oh, axis=-1, keepdims=True)

    ho = pl.BlockSpec((heads, tm, VHEAD), lambda i: (0, i, 0))
    vec = pl.BlockSpec((1, heads * VHEAD), lambda i: (0, 0))
    return pl.pallas_call(
        body, name="out_bwd", grid=(t // tm,),
        in_specs=[pl.BlockSpec((tm, d), lambda i: (i, 0)), ho, vec, WHOLE_VMEM],
        out_specs=[pl.BlockSpec((tm, w), lambda i: (i, 0)), ho, pl.BlockSpec((heads, tm, 1), lambda i: (0, i, 0)), vec],
        out_shape=[jax.ShapeDtypeStruct((t, w), F32), jax.ShapeDtypeStruct((heads, t, VHEAD), BF16),
                   jax.ShapeDtypeStruct((heads, t, 1), F32), jax.ShapeDtypeStruct((1, heads * VHEAD), F32)],
        compiler_params=_params(("arbitrary",)),
    )(dx2, o, gb, w_out)


def _loss_head(y, target, tm):
    t, d = y.shape

    def body(y_ref, t_ref, dy_ref, loss_ref):
        @pl.when(pl.program_id(0) == 0)
        def _():
            loss_ref[...] = jnp.zeros_like(loss_ref)

        err = y_ref[...] - t_ref[...]
        dy_ref[...] = err * (1.0 / d)
        part = jnp.sum(jnp.sum(err * err, axis=-1, keepdims=True) * (1.0 / d), axis=0, keepdims=True)
        loss_ref[...] += 0.5 * part

    row = pl.BlockSpec((tm, d), lambda i: (i, 0))
    return pl.pallas_call(
        body, name="loss_head", grid=(t // tm,),
        in_specs=[row, row], out_specs=[row, pl.BlockSpec((1, 1), lambda i: (0, 0))],
        out_shape=[jax.ShapeDtypeStruct((t, d), F32), jax.ShapeDtypeStruct((1, 1), F32)],
        compiler_params=_params(("arbitrary",)),
    )(y, target)


def _place():
    return lax.axis_index("x"), lax.axis_index("y"), lax.axis_index("c")


HBM = pl.BlockSpec(memory_space=pltpu.HBM)
SEM = pl.BlockSpec(memory_space=pltpu.SEMAPHORE)
DATAFLOW = pltpu.SideEffectType.DATAFLOW_SIDE_EFFECTING


def _plan_copies(plan, refs, send_sems, recv_sems):
    def block(ref, blk):
        if blk is None:
            return ref
        return ref.at[blk[0], pl.ds(0, blk[1])] if isinstance(blk, tuple) else ref.at[blk]

    cps = []
    for i, (sb, sblk, db, dblk, dev) in enumerate(plan(*_place())):
        cps.append(pltpu.make_async_remote_copy(
            src_ref=block(refs[sb], sblk), dst_ref=block(refs[db], dblk),
            send_sem=send_sems.at[i], recv_sem=recv_sems.at[i], device_id=dev, device_id_type=MESH))
    return cps


def _push_start(bufs, plan, ncopy, name, deps=()):
    nb = len(bufs)

    def body(*refs):
        outs = refs[nb + len(deps):]
        for cp in _plan_copies(plan, refs[:nb], outs[0], outs[1]):
            cp.start()
        outs[-1][...] = jnp.zeros_like(outs[-1])

    res = pl.pallas_call(
        body, name=name,
        out_shape=(pltpu.SemaphoreType.DMA((ncopy,)), pltpu.SemaphoreType.DMA((ncopy,)),
                   *[pltpu.HBM(b.shape, b.dtype) for b in bufs], jax.ShapeDtypeStruct((SUBLANE, LANE), F32)),
        in_specs=[HBM] * nb + [ANY] * len(deps),
        out_specs=(SEM, SEM, *[HBM] * nb, WHOLE_VMEM),
        input_output_aliases={i: 2 + i for i in range(nb)},
        compiler_params=pltpu.CompilerParams(has_side_effects=DATAFLOW),
    )(*[pltpu.with_memory_space_constraint(b, pltpu.HBM) for b in bufs], *deps)
    return res[0], res[1], list(res[2:2 + nb]), res[-1]


def _push_wait(send_sems, recv_sems, bufs, plan, after, name):
    nb = len(bufs)

    def body(*refs):
        for cp in _plan_copies(plan, refs[:nb], refs[nb], refs[nb + 1]):
            cp.wait_send()
            cp.wait_recv()

    res = pl.pallas_call(
        body, name=name,
        out_shape=[pltpu.HBM(b.shape, b.dtype) for b in bufs],
        in_specs=[HBM] * nb + [SEM, SEM, ANY], out_specs=[HBM] * nb,
        input_output_aliases={i: i for i in range(nb)},
        compiler_params=pltpu.CompilerParams(has_side_effects=DATAFLOW),
    )(*bufs, send_sems, recv_sems, after)
    return list(res)


def _other_chips(x, y):
    return ((1 - x, y), (x, 1 - y), (1 - x, 1 - y))


class _Exchange:
    def __init__(self, bufs, plan, ncopy, name, deps=()):
        self.plan, self.name = plan, name
        self.send, self.recv, self.bufs, self.token = _push_start(bufs, plan, ncopy, name + "_start", deps)

    def wait(self, after):
        return _push_wait(self.send, self.recv, self.bufs, self.plan, after, self.name + "_wait")


class _Chain:
    def __init__(self, bufs):
        self.bufs = list(bufs)

    def start(self, plan, ncopy, name, deps=()):
        send, recv, self.bufs, token = _push_start(self.bufs, plan, ncopy, name + "_start", deps)
        return (send, recv, plan, name), token

    def wait(self, pending, after):
        send, recv, plan, name = pending
        self.bufs = _push_wait(send, recv, self.bufs, plan, after, name + "_wait")


class _StagedGather:
    def __init__(self, shards, me, name, pad_to=None):
        self.n = n = len(shards)
        self.name = name
        rows = shards[0].shape[0]
        lands = []
        for s in shards:
            land = lax.empty((N_DEV, pad_to or rows) + s.shape[1:], s.dtype)
            if pad_to and pad_to != rows:
                land = lax.dynamic_update_slice(
                    land, jnp.zeros((N_DEV, pad_to - rows) + s.shape[1:], s.dtype), (0, rows, 0))
            lands.append(lax.dynamic_update_slice(land, s[None], (me, 0, 0)))
        self.chain = _Chain(list(shards) + lands)
        self.pending = {}

        def blk(b):
            return (b, rows) if pad_to and pad_to != rows else b

        def to_sibling(blocks):
            return lambda x, y, c: [(n + a, blk(b), n + a, blk(b), (x, y, 1 - c))
                                    for a in range(n) for b in blocks(x, y, c)]

        def nbr_blocks(x, y, c):
            return [4 * (1 - x) + 2 * y + c, 4 * x + 2 * (1 - y) + c]

        def diag(x, y, c):
            sx, sy = (1 - x) * (1 - c) + x * c, y * (1 - c) + (1 - y) * c
            tx, ty = x * (1 - c) + (1 - x) * c, (1 - y) * (1 - c) + y * c
            b = blk(4 * sx + 2 * sy + c)
            return [(n + a, b, n + a, b, (tx, ty, c)) for a in range(n)]

        self.plans = {
            "own": (lambda x, y, c: [(a, None, n + a, blk(4 * x + 2 * y + c), (x, y, 1 - c)) for a in range(n)], n),
            "nbr": (lambda x, y, c: [(a, None, n + a, blk(4 * x + 2 * y + c), dev) for a in range(n)
                                     for dev in ((1 - x, y, c), (x, 1 - y, c))], 2 * n),
            "diag": (diag, n),
            "nbr_d2d": (to_sibling(nbr_blocks), 2 * n),
            "own_nbr_d2d": (to_sibling(lambda x, y, c: [4 * x + 2 * y + c] + nbr_blocks(x, y, c)), 3 * n),
            "diag_d2d": (to_sibling(lambda x, y, c: [4 * (1 - x) + 2 * (1 - y) + c]), n),
        }

    def start(self, stage, deps=()):
        plan, ncopy = self.plans[stage]
        self.pending[stage], token = self.chain.start(plan, ncopy, self.name + "_" + stage, deps)
        return token

    def wait(self, stage, after):
        self.chain.wait(self.pending.pop(stage), after)

    def lands(self):
        return self.chain.bufs[self.n:]


def _gather_ici(shards, me, name, deps=()):
    n = len(shards)
    lands = [lax.dynamic_update_slice(lax.empty((N_DEV,) + s.shape, s.dtype), s[None], (me, 0, 0)) for s in shards]

    def plan(x, y, c):
        return [(a, None, n + a, 4 * x + 2 * y + c, (px, py, c)) for a in range(n) for px, py in _other_chips(x, y)]

    return _Exchange(list(shards) + lands, plan, 3 * n, name, deps)


def _gather_d2d(lands, name, deps=()):
    n = len(lands)

    def plan(x, y, c):
        blocks = [4 * x + 2 * y + c] + [4 * px + 2 * py + c for px, py in _other_chips(x, y)]
        return [(a, b, a, b, (x, y, 1 - c)) for a in range(n) for b in blocks]

    return _Exchange(list(lands), plan, 4 * n, name, deps)


def _reduce_d2d(grads, name, deps=(), rows=None):
    n = len(grads)
    lands = [lax.empty((4,) + g.shape[1:], g.dtype) for g in grads]

    def blk(b):
        return b if rows is None else (b, rows)

    def plan(x, y, c):
        return [(a, blk(2 * s + (1 - c)), n + a, blk(s), (x, y, 1 - c)) for a in range(n) for s in range(4)]

    return _Exchange(list(grads) + lands, plan, 4 * n, name, deps)


def _reduce_ici(chip, name, deps=(), rows=None):
    n = len(chip)
    lands = [lax.empty((3,) + g.shape[1:], g.dtype) for g in chip]

    def blk(b):
        return b if rows is None else (b, rows)

    def plan(x, y, c):
        return [(a, blk(2 * px + py), n + a, blk(k), (px, py, c))
                for a in range(n) for k, (px, py) in enumerate(_other_chips(x, y))]

    return _Exchange(list(chip) + lands, plan, 3 * n, name, deps)


def _pair_add(full, got, core, name, rows=None):
    _, r, cdim = full.shape
    br = _row_block(rows or r, 512)

    def body(c_ref, f_ref, g_ref, o_ref):
        o_ref[...] = (f_ref[...].astype(F32) + g_ref[...].astype(F32)).astype(o_ref.dtype)

    return pl.pallas_call(
        body, name=name,
        grid_spec=pltpu.PrefetchScalarGridSpec(
            num_scalar_prefetch=1, grid=(4, (rows or r) // br),
            in_specs=[pl.BlockSpec((1, br, cdim), lambda s, i, c_ref: (2 * s + c_ref[0], i, 0)),
                      pl.BlockSpec((1, br, cdim), lambda s, i, c_ref: (s, i, 0))],
            out_specs=pl.BlockSpec((1, br, cdim), lambda s, i, c_ref: (s, i, 0))),
        out_shape=jax.ShapeDtypeStruct((4, r, cdim), full.dtype),
        compiler_params=_params(("arbitrary", "arbitrary")),
    )(core, full, got)


def _sum_devices(stack):
    _, r, cdim = stack.shape

    def body(s_ref, o_ref):
        acc = s_ref[0]
        for k in range(1, N_DEV):
            acc = acc + s_ref[k]
        o_ref[...] = acc

    return pl.pallas_call(
        body, name="sum_devices", out_shape=jax.ShapeDtypeStruct((r, cdim), F32),
        compiler_params=_params(),
    )(stack)


def _adamw(w, g, m, v, name):
    r, cdim = w.shape
    br = _row_block(r, 256)

    def body(w_ref, g_ref, m_ref, v_ref, d_ref, nm_ref, nv_ref):
        g = g_ref[...]
        nm = ADAM_B1 * m_ref[...] + (1.0 - ADAM_B1) * g
        nv = ADAM_B2 * v_ref[...] + (1.0 - ADAM_B2) * (g * g)
        m_hat = nm / (1.0 - ADAM_B1 ** ADAM_STEP)
        v_hat = nv / (1.0 - ADAM_B2 ** ADAM_STEP)
        d_ref[...] = -ADAM_LR * (m_hat / (jnp.sqrt(v_hat) + ADAM_EPS) + ADAM_WD * w_ref[...])
        nm_ref[...] = nm
        nv_ref[...] = nv

    spec = pl.BlockSpec((br, cdim), lambda i: (i, 0))
    shape = jax.ShapeDtypeStruct((r, cdim), F32)
    return pl.pallas_call(
        body, name=name, grid=(r // br,), in_specs=[spec] * 4, out_specs=[spec] * 3,
        out_shape=[shape] * 3, compiler_params=_params(("arbitrary",)),
    )(w, g, m, v)


def _sum_adamw(chip, got, slot, w, m, v, name, after):
    rows, cdim = w.shape
    bc = 2 * LANE if cdim % (2 * LANE) == 0 else cdim

    def body(s_ref, c_ref, g_ref, w_ref, m_ref, v_ref, _, go_ref, d_ref, nm_ref, nv_ref):
        g = c_ref[0].astype(F32)
        for k in range(3):
            g = g + g_ref[k].astype(F32)
        nm = ADAM_B1 * m_ref[...] + (1.0 - ADAM_B1) * g
        nv = ADAM_B2 * v_ref[...] + (1.0 - ADAM_B2) * (g * g)
        m_hat = nm / (1.0 - ADAM_B1 ** ADAM_STEP)
        v_hat = nv / (1.0 - ADAM_B2 ** ADAM_STEP)
        go_ref[...] = g
        d_ref[...] = -ADAM_LR * (m_hat / (jnp.sqrt(v_hat) + ADAM_EPS) + ADAM_WD * w_ref[...])
        nm_ref[...] = nm
        nv_ref[...] = nv

    spec = pl.BlockSpec((rows, bc), lambda j, s_ref: (0, j))
    shape = jax.ShapeDtypeStruct((rows, cdim), F32)
    return pl.pallas_call(
        body, name=name,
        grid_spec=pltpu.PrefetchScalarGridSpec(
            num_scalar_prefetch=1, grid=(cdim // bc,),
            in_specs=[pl.BlockSpec((1, rows, bc), lambda j, s_ref: (s_ref[0], 0, j)),
                      pl.BlockSpec((3, rows, bc), lambda j, s_ref: (0, 0, j)), spec, spec, spec, ANY],
            out_specs=[spec] * 4),
        out_shape=[shape] * 4,
        compiler_params=_params(("arbitrary",)),
    )(slot, chip, got, w, m, v, after)


WEIGHTS = ("ffn1_norm_g", "ffn1_w_gate", "ffn1_w_up", "ffn1_w_down", "mix_norm_g", "w_in", "gmlp_v_norm_g",
           "gmlp_w_s", "gmlp_b_s", "mla_q_norm_g", "mla_w_q_up", "mla_kv_norm_g", "mla_w_kv_up", "mla_q_head_g",
           "mla_k_head_g", "gmlp_out_g", "mla_out_g", "w_out", "ffn2_norm_g", "ffn2_w_gate", "ffn2_w_up",
           "ffn2_w_down")
SHARDED = {"ffn1_w_gate": True, "ffn1_w_up": True, "ffn1_w_down": False, "w_in": True, "mla_w_q_up": True,
           "mla_w_kv_up": True, "w_out": False, "ffn2_w_gate": True, "ffn2_w_up": True, "ffn2_w_down": False}


def _col_block(m, target):
    best = LANE
    for cand in range(LANE, min(m, target) + 1, LANE):
        if m % cand == 0:
            best = cand
    return best


def _shard_rows(w, transposed, pad_to=None):
    rows = (w[0].T if transposed else w[0]).astype(BF16)
    if pad_to is not None and pad_to != rows.shape[0]:
        rows = jnp.pad(rows, ((0, pad_to - rows.shape[0]), (0, 0)))
    return rows


def _pack(parts):
    flat = []
    for p in parts:
        f = p.reshape(-1).astype(F32)
        flat.append(jnp.pad(f, (0, _round_up(f.size, LANE) - f.size)))
    flat = jnp.concatenate(flat)
    rows = _round_up(flat.size // LANE, SUBLANE)
    return jnp.pad(flat, (0, rows * LANE - flat.size)).reshape(rows, LANE)


def _unpack(packed, shapes):
    out, row = [], 0
    for shp in shapes:
        size = 1
        for s in shp:
            size *= s
        nrows = _round_up(size, LANE) // LANE
        out.append(packed[row:row + nrows].reshape(-1)[:size].reshape(shp))
        row += nrows
    return out


def kernel(x, positions, ffn1_norm_g, ffn1_w_gate, ffn1_w_up, ffn1_w_down, mix_norm_g, w_in, gmlp_v_norm_g, gmlp_w_s, gmlp_b_s, mla_q_norm_g, mla_w_q_up, mla_kv_norm_g, mla_w_kv_up, mla_q_head_g, mla_k_head_g, gmlp_out_g, mla_out_g, w_out, ffn2_norm_g, ffn2_w_gate, ffn2_w_up, ffn2_w_down, loss_target, m_ffn1_norm_g, m_ffn1_w_gate, m_ffn1_w_up, m_ffn1_w_down, m_mix_norm_g, m_w_in, m_gmlp_v_norm_g, m_gmlp_w_s, m_gmlp_b_s, m_mla_q_norm_g, m_mla_w_q_up, m_mla_kv_norm_g, m_mla_w_kv_up, m_mla_q_head_g, m_mla_k_head_g, m_gmlp_out_g, m_mla_out_g, m_w_out, m_ffn2_norm_g, m_ffn2_w_gate, m_ffn2_w_up, m_ffn2_w_down, v_ffn1_norm_g, v_ffn1_w_gate, v_ffn1_w_up, v_ffn1_w_down, v_mix_norm_g, v_w_in, v_gmlp_v_norm_g, v_gmlp_w_s, v_gmlp_b_s, v_mla_q_norm_g, v_mla_w_q_up, v_mla_kv_norm_g, v_mla_w_kv_up, v_mla_q_head_g, v_mla_k_head_g, v_gmlp_out_g, v_mla_out_g, v_w_out, v_ffn2_norm_g, v_ffn2_w_gate, v_ffn2_w_up, v_ffn2_w_down):
    wts = dict(zip(WEIGHTS, (ffn1_norm_g, ffn1_w_gate, ffn1_w_up, ffn1_w_down, mix_norm_g, w_in, gmlp_v_norm_g, gmlp_w_s, gmlp_b_s, mla_q_norm_g, mla_w_q_up, mla_kv_norm_g, mla_w_kv_up, mla_q_head_g, mla_k_head_g, gmlp_out_g, mla_out_g, w_out, ffn2_norm_g, ffn2_w_gate, ffn2_w_up, ffn2_w_down)))
    mom1 = dict(zip(WEIGHTS, (m_ffn1_norm_g, m_ffn1_w_gate, m_ffn1_w_up, m_ffn1_w_down, m_mix_norm_g, m_w_in, m_gmlp_v_norm_g, m_gmlp_w_s, m_gmlp_b_s, m_mla_q_norm_g, m_mla_w_q_up, m_mla_kv_norm_g, m_mla_w_kv_up, m_mla_q_head_g, m_mla_k_head_g, m_gmlp_out_g, m_mla_out_g, m_w_out, m_ffn2_norm_g, m_ffn2_w_gate, m_ffn2_w_up, m_ffn2_w_down)))
    mom2 = dict(zip(WEIGHTS, (v_ffn1_norm_g, v_ffn1_w_gate, v_ffn1_w_up, v_ffn1_w_down, v_mix_norm_g, v_w_in, v_gmlp_v_norm_g, v_gmlp_w_s, v_gmlp_b_s, v_mla_q_norm_g, v_mla_w_q_up, v_mla_kv_norm_g, v_mla_w_kv_up, v_mla_q_head_g, v_mla_k_head_g, v_gmlp_out_g, v_mla_out_g, v_w_out, v_ffn2_norm_g, v_ffn2_w_gate, v_ffn2_w_up, v_ffn2_w_down)))

    b_loc, seq, d = x.shape
    t = b_loc * seq
    ffs = ffn1_w_gate.shape[2]
    fp = _round_up(ffs, LANE)
    wg = gmlp_v_norm_g.shape[1]
    groups = gmlp_w_s.shape[1]
    rq, rkv = mla_q_norm_g.shape[1], mla_kv_norm_g.shape[1]
    heads = mla_out_g.shape[1]
    assert w_in.shape[2] * N_DEV == 2 * wg + rq + rkv + ROPE and mla_w_kv_up.shape[2] * N_DEV == heads * HEADW
    tm = min(512, t)
    tm_mix = min(256, t)
    blk = min(256, seq)

    xf = x.reshape(t, d)
    target = loss_target.reshape(t, d)
    pos = positions.reshape(t, 1).astype(F32)
    half = ROPE // 2
    inv_freq = 1.0 / (ROPE_THETA ** (jnp.arange(half, dtype=F32) / half))
    freq = jnp.concatenate([inv_freq, inv_freq, jnp.zeros((LANE - ROPE,), F32)])[None, :]
    lane = jnp.arange(LANE)
    masks = jnp.stack([jnp.where(lane < half, -1.0, 0.0),
                       jnp.where((lane >= half) & (lane < ROPE), 1.0, 0.0)]).astype(F32)
    gqh = jnp.pad(mla_q_head_g, ((0, 0), (0, HEADW - QK)))
    gkh = jnp.pad(mla_k_head_g, ((0, 0), (0, HEADW - QK)))
    bias = jnp.repeat(gmlp_b_s[0].T, CHUNK, axis=1)
    gouta = gmlp_out_g.reshape(1, wg)
    goutb = mla_out_g.reshape(1, heads * VHEAD)
    ws = gmlp_w_s[0]

    px, py, pc = _place()
    me = 4 * px + 2 * py + pc
    core = pc.astype(jnp.int32).reshape(1)
    slot = (2 * px + py).astype(jnp.int32).reshape(1)
    order = [n for n in WEIGHTS if n in SHARDED]
    group = {"ffn1": [n for n in order if n.startswith("ffn1")], "ffn2": [n for n in order if n.startswith("ffn2")],
             "mix": [n for n in order if not n.startswith("ffn")]}
    shard = {n: _shard_rows(wts[n], SHARDED[n]) for n in group["ffn1"]}
    frows = ffs if ffs != fp else None

    def tied(arr, token):
        return arr + token[0, 0].astype(arr.dtype)

    xnb, ynb, dgn = 4 * (1 - px) + 2 * py, 4 * px + 2 * (1 - py), 4 * (1 - px) + 2 * (1 - py)
    ids_a = jnp.stack([me, 4 * px + 2 * py + (1 - pc)]).astype(jnp.int32)
    ids_b = jnp.stack([xnb, xnb + 1, ynb, ynb + 1]).astype(jnp.int32)
    ids_c = jnp.stack([dgn, dgn + 1]).astype(jnp.int32)
    g1 = _StagedGather([shard[n] for n in group["ffn1"]], me, "gather_ffn1", pad_to=fp)
    token = g1.start("own")
    token = g1.start("nbr", deps=(token,))
    for n in group["mix"] + group["ffn2"]:
        shard[n] = _shard_rows(tied(wts[n], token), SHARDED[n])
    g3 = _StagedGather([shard[n] for n in group["ffn2"]], me, "gather_ffn2", pad_to=fp)
    g1.wait("own", token)
    x1, xn1, kept1 = _ffn_fwd(xf, None, ffn1_norm_g, ids_a, *g1.lands(), None, tm, "ffn1_fwd_a")
    rope_cos, rope_sin = _rope_tables(pos, freq, tm, x1)
    g1.wait("nbr", rope_cos)
    token = g1.start("diag")
    ici2 = _gather_ici([shard[n] for n in group["mix"]], me, "gather_mix_ici", deps=(token,))
    token = g3.start("nbr", deps=(ici2.token,))
    token = g1.start("nbr_d2d", deps=(token,))
    g1.wait("nbr_d2d", token)
    x1, xn1, kept1 = _ffn_fwd(x1, xn1, None, ids_b, *g1.lands(), kept1, tm, "ffn1_fwd_b")
    g1.wait("diag", x1)
    token = g1.start("diag_d2d")
    d2d2 = _gather_d2d(ici2.wait(x1)[len(group["mix"]):], "gather_mix_d2d", deps=(token,))
    g1.wait("diag_d2d", d2d2.token)
    full = dict(zip(group["ffn1"], g1.lands()))
    x1, xn1, (gd1, sl1, h1) = _ffn_fwd(x1, xn1, None, ids_c, full["ffn1_w_gate"], full["ffn1_w_up"],
                                       full["ffn1_w_down"], kept1, tm, "ffn1_fwd_c")
    full.update(zip(group["mix"], d2d2.wait(x1)))
    win_t = full["w_in"].reshape(-1, d)
    splits = (2 * wg, rq, rkv, LANE)
    wq_t = jnp.pad(full["mla_w_q_up"].reshape(heads, QK, rq), ((0, 0), (0, HEADW - QK), (0, 0)))
    wkv_t = full["mla_w_kv_up"].reshape(heads, HEADW, rkv)
    wout = full["w_out"].reshape(-1, d)
    hn, zuv, cq, ckv, krw = _inproj_fwd(x1, mix_norm_g, win_t, splits, tm)
    ya = _gmlp_fwd(zuv, gmlp_v_norm_g, ws, bias, gouta, tm_mix)
    g3.wait("nbr", ya)
    token = g3.start("diag")
    token = g3.start("own_nbr_d2d", deps=(token,))
    q, k, vv = _mla_proj_fwd(cq, ckv, krw, rope_cos, rope_sin, masks, mla_q_norm_g, mla_kv_norm_g, wq_t, wkv_t,
                             tied(gqh, token), gkh, tm)
    o, lse = _attn_fwd(q, k, vv, seq, blk)
    g3.wait("diag", o)
    token = g3.start("diag_d2d")
    x2, ycat = _out_fwd(ya, o, tied(goutb, token), wout, x1, tm)
    g3.wait("own_nbr_d2d", x2)
    g3.wait("diag_d2d", x2)
    full.update(zip(group["ffn2"], g3.lands()))
    x3, xn2, (gd2, sl2, h2) = _ffn_fwd(x2, None, ffn2_norm_g, jnp.arange(N_DEV, dtype=jnp.int32),
                                       full["ffn2_w_gate"], full["ffn2_w_up"], full["ffn2_w_down"], None, tm,
                                       "ffn2_fwd")
    dx3, loss_part = _loss_head(x3, target, tm)

    outs_g, outs_d, outs_m, outs_v = {}, {}, {}, {}

    def finish(names, chip, got, after):
        for n, cp, gt in zip(names, chip, got):
            rows_of = (lambda a: a[0].T) if SHARDED[n] else (lambda a: a[0])
            res = _sum_adamw(cp, gt, slot, rows_of(wts[n]), rows_of(mom1[n]), rows_of(mom2[n]), "adamw_" + n, after)
            outs_g[n], outs_d[n], outs_m[n], outs_v[n] = [r.T[None] if SHARDED[n] else r[None] for r in res]
            after = res[3]
        return after

    def chip_sums(names, ex, after, rows=None):
        res = ex.wait(after)
        return [_pair_add(f, gt, core, "pair_add_" + n, rows)
                for n, f, gt in zip(names, res[:len(names)], res[len(names):])]

    tk = min(2048, t)
    grads = {}
    small = {}
    dx2, small["ffn2_norm_g"], da2, db2 = _ffn_bwd(
        dx3, x2, ffn2_norm_g, gd2, sl2, full["ffn2_w_gate"], full["ffn2_w_up"], full["ffn2_w_down"], tm, "ffn2_bwd")
    grads["ffn2_w_gate"] = _matmul_tn_resident(da2, xn2, fp, BF16, "dw_ffn2_gate").reshape(N_DEV, fp, d)
    grads["ffn2_w_up"] = _matmul_tn_resident(db2, xn2, fp, BF16, "dw_ffn2_up").reshape(N_DEV, fp, d)
    grads["ffn2_w_down"] = _matmul_tn(h2, dx3, fp, d, tk, BF16, "dw_ffn2_down", rhs_scale=0.5).reshape(
        N_DEV, fp, d)
    red_a2 = _reduce_d2d([grads[n] for n in group["ffn2"]], "reduce_ffn2_d2d", rows=frows)
    dya, do, delta, small["mla_out_g"] = _out_bwd(dx2, o, tied(goutb, red_a2.token), wout, wg, tm)
    grads["w_out"] = _matmul_tn(ycat, dx2, _col_block(ycat.shape[1], 768), d, tk, BF16, "dw_out").reshape(
        N_DEV, -1, d)
    chip2 = chip_sums(group["ffn2"], red_a2, dya, frows)
    red_b2 = _reduce_ici(chip2, "reduce_ffn2_ici", rows=frows)
    dq, dk, dv = _attn_bwd(q, k, vv, do, lse, delta, seq, blk, red_b2.token)
    (dcq, dckv, dkrw, dwq, dwkv, small["mla_q_norm_g"], small["mla_kv_norm_g"], dgqh, dgkh) = _mla_proj_bwd(
        dq, dk, dv, cq, ckv, krw, rope_cos, rope_sin, masks, mla_q_norm_g, mla_kv_norm_g, wq_t, wkv_t, gqh, gkh,
        tm)
    small["mla_q_head_g"], small["mla_k_head_g"] = dgqh[:, :QK], dgkh[:, :QK]
    grads["mla_w_q_up"] = dwq[:, :QK].astype(BF16).reshape(N_DEV, -1, rq)
    grads["mla_w_kv_up"] = dwkv.astype(BF16).reshape(N_DEV, -1, rkv)
    dzuv, small["gmlp_w_s"], dbs, small["gmlp_v_norm_g"], small["gmlp_out_g"] = _gmlp_bwd(
        dya, zuv, gmlp_v_norm_g, ws, bias, gouta, tm_mix)
    small["gmlp_b_s"] = dbs[:, :, 0]
    dx1, small["mix_norm_g"], dzc = _inproj_bwd([dzuv, dcq, dckv, dkrw], x1, mix_norm_g, win_t, dx2, splits,
                                                tm_mix)
    grads["w_in"] = _matmul_tn(dzc, hn, _col_block(dzc.shape[1], 768), d, tk, BF16, "dw_in",
                               out_rows=win_t.shape[0]).reshape(N_DEV, -1, d)
    res_b2 = red_b2.wait(grads["w_in"])
    red_am = _reduce_d2d([grads[n] for n in group["mix"]], "reduce_mix_d2d")

    def ffn1_dw(n, lhs, rhs, scale, token):
        if scale is None:
            return _matmul_tn_resident(lhs, rhs, fp, BF16, "dw_" + n, deps=(token,)).reshape(N_DEV, fp, d)
        return _matmul_tn(lhs, rhs, fp, d, tk, BF16, "dw_" + n, rhs_scale=scale, deps=(token,)).reshape(N_DEV, fp, d)

    gr = ffn1_dw("ffn1_w_down", h1, dx1, 0.5, red_am.token)
    red_ad = _reduce_d2d([gr], "reduce_ffn1_w_down_d2d", deps=(red_am.token,), rows=frows)
    dx0, small["ffn1_norm_g"], da1, db1 = _ffn_bwd(
        dx1, xf, tied(ffn1_norm_g, red_ad.token), gd1, sl1, full["ffn1_w_gate"], full["ffn1_w_up"],
        full["ffn1_w_down"], tm, "ffn1_bwd")
    chipm = chip_sums(group["mix"], red_am, dx0)
    red_bm = _reduce_ici(chipm, "reduce_mix_ici")
    red_b = [("ffn1_w_down", _reduce_ici(chip_sums(["ffn1_w_down"], red_ad, dx0, frows), "reduce_ffn1_w_down_ici",
                                         deps=(red_bm.token,), rows=frows))]
    rep = [n for n in WEIGHTS if n not in SHARDED]
    small_ici = _gather_ici([_pack([small[n] for n in rep] + [loss_part])], me, "gather_small_ici",
                            deps=(red_b[-1][1].token,))
    gr = ffn1_dw("ffn1_w_gate", da1, xn1, None, small_ici.token)
    red_ag = _reduce_d2d([gr], "reduce_ffn1_w_gate_d2d", rows=frows)
    n2 = group["ffn2"]
    after = finish(n2[0:1], res_b2[0:1], res_b2[3:4], red_ag.token)
    red_b.append(("ffn1_w_gate", _reduce_ici(chip_sums(["ffn1_w_gate"], red_ag, after, frows),
                                             "reduce_ffn1_w_gate_ici", rows=frows)))
    gr = ffn1_dw("ffn1_w_up", db1, xn1, None, red_b[-1][1].token)
    small_d2d = _gather_d2d(small_ici.wait(gr)[1:], "gather_small_d2d")
    red_au = _reduce_d2d([gr], "reduce_ffn1_w_up_d2d", deps=(small_d2d.token,), rows=frows)
    after = finish(n2[1:2], res_b2[1:2], res_b2[4:5], red_au.token)
    red_b.append(("ffn1_w_up", _reduce_ici(chip_sums(["ffn1_w_up"], red_au, after, frows),
                                           "reduce_ffn1_w_up_ici", rows=frows)))
    after = finish(n2[2:3], res_b2[2:3], res_b2[5:6], red_b[-1][1].token)
    res = red_bm.wait(after)
    nm_ = len(group["mix"])
    after = finish(group["mix"], res[:nm_], res[nm_:], after)
    total = _sum_devices(small_d2d.wait(after)[0])
    zero = jnp.zeros((1,), F32)
    dlt, nm, nv = _adamw(_pack([wts[n] for n in rep] + [zero]), total, _pack([mom1[n] for n in rep] + [zero]),
                         _pack([mom2[n] for n in rep] + [zero]), "adamw_small")
    shapes = [wts[n].shape for n in rep] + [(1,)]
    for n, g, dl, m1, m2 in zip(rep, _unpack(total, shapes), _unpack(dlt, shapes), _unpack(nm, shapes),
                                _unpack(nv, shapes)):
        outs_g[n], outs_d[n], outs_m[n], outs_v[n] = g, dl, m1, m2
    loss = _unpack(total, shapes)[-1].reshape(())
    after = dlt
    for n, ex in red_b:
        res = ex.wait(after)
        after = finish([n], res[:1], res[1:], after)

    return (loss, dx0.reshape(b_loc, seq, d), *[outs_g[n] for n in WEIGHTS], *[outs_d[n] for n in WEIGHTS],
            *[outs_m[n] for n in WEIGHTS], *[outs_v[n] for n in WEIGHTS])
```

```python
import functools

import jax
import jax.numpy as jnp
from jax import lax
from jax.experimental import pallas as pl
from jax.experimental.pallas import tpu as pltpu

F32 = jnp.float32
BF16 = jnp.bfloat16
EPS = 1e-6
LANE = 128
SUBLANE = 8
N_DEV = 8
VMEM_LIMIT = 60 * 1024 * 1024
NOPE = 128
ROPE = 64
VHEAD = 128
QK = NOPE + ROPE
HEADW = 2 * LANE
CHUNK = 128
ROPE_THETA = 10000.0
ADAM_LR, ADAM_B1, ADAM_B2, ADAM_EPS, ADAM_WD, ADAM_STEP = 0.001, 0.9, 0.999, 1e-08, 0.01, 10
MESH = pl.DeviceIdType.MESH
ANY = pl.BlockSpec(memory_space=pl.ANY)
WHOLE_VMEM = pl.BlockSpec(memory_space=pltpu.VMEM)


def _params(sem=None):
    return pltpu.CompilerParams(dimension_semantics=sem, vmem_limit_bytes=VMEM_LIMIT)


def _round_up(n, m):
    return -(-n // m) * m


def _row_block(rows, target):
    best = rows
    for cand in range(SUBLANE, min(rows, target) + 1, SUBLANE):
        if rows % cand == 0:
            best = cand
    return best if best <= target else rows


def _nn(a, b):
    return jnp.dot(a, b, preferred_element_type=F32)


def _nt(a, b):
    return lax.dot_general(a, b, (((1,), (1,)), ((), ())), preferred_element_type=F32)


def _tn(a, b):
    return lax.dot_general(a, b, (((0,), (0,)), ((), ())), preferred_element_type=F32)


def _rstd(x, n):
    return lax.rsqrt(jnp.sum(x * x, axis=-1, keepdims=True) * (1.0 / n) + EPS)


def _rms_fwd(x, g, n):
    return x * _rstd(x, n) * g


def _rms_bwd(x, g, dy, n):
    r = _rstd(x, n)
    xh = x * r
    dyg = dy * g
    dx = r * (dyg - xh * (jnp.sum(dyg * xh, axis=-1, keepdims=True) * (1.0 / n)))
    return dx, jnp.sum(dy * xh, axis=0, keepdims=True)


def _gelu(x):
    return 0.5 * x * (1.0 + lax.erf(x * 0.7071067811865476))


def _gelu_grad(x):
    return 0.5 * (1.0 + lax.erf(x * 0.7071067811865476)) + x * jnp.exp(-0.5 * x * x) * 0.3989422804014327


def _ffn_fwd(base, xn, g, ids, wg_t, wu_t, wd, saved, tm, name):
    t, d = base.shape
    nb, fp, _ = wg_t.shape
    n = ids.shape[0]
    first = xn is None
    if saved is None:
        saved = [lax.empty((t, nb * fp), BF16) for _ in range(3)]

    def body(ids_ref, *refs):
        if first:
            base_ref, g_ref, wg_ref, wu_ref, wd_ref, _, _, _, out_ref, xn_ref, gd_ref, sl_ref, h_ref, acc = refs
        else:
            base_ref, xn_ref, wg_ref, wu_ref, wd_ref, _, _, _, out_ref, gd_ref, sl_ref, h_ref, acc = refs
        j = pl.program_id(1)

        @pl.when(j == 0)
        def _():
            if first:
                xn_ref[...] = _rms_fwd(base_ref[...], g_ref[...], d).astype(BF16)
            acc[...] = jnp.zeros_like(acc)

        xnb = xn_ref[...]
        a = _nt(xnb, wg_ref[0])
        b = _nt(xnb, wu_ref[0])
        s = jax.nn.sigmoid(a)
        sl = a * s
        h = (sl * b).astype(BF16)
        gd_ref[...] = (b * (s * (1.0 + a * (1.0 - s)))).astype(BF16)
        sl_ref[...] = sl.astype(BF16)
        h_ref[...] = h
        acc[...] += _nn(h, wd_ref[0])

        @pl.when(j == n - 1)
        def _():
            out_ref[...] = base_ref[...] + 0.5 * acc[...]

    wspec = pl.BlockSpec((1, fp, d), lambda i, j, ids_ref: (ids_ref[j], 0, 0))
    row = pl.BlockSpec((tm, d), lambda i, j, ids_ref: (i, 0))
    ff = pl.BlockSpec((tm, fp), lambda i, j, ids_ref: (i, ids_ref[j]))
    ffs = jax.ShapeDtypeStruct((t, nb * fp), BF16)
    second = pl.BlockSpec((1, d), lambda i, j, ids_ref: (0, 0)) if first else row
    n_row_outs = 2 if first else 1
    res = pl.pallas_call(
        body, name=name,
        grid_spec=pltpu.PrefetchScalarGridSpec(
            num_scalar_prefetch=1, grid=(t // tm, n),
            in_specs=[row, second, wspec, wspec, wspec, ANY, ANY, ANY],
            out_specs=[row] * n_row_outs + [ff, ff, ff],
            scratch_shapes=[pltpu.VMEM((tm, d), F32)]),
        out_shape=[jax.ShapeDtypeStruct((t, d), F32)] + ([jax.ShapeDtypeStruct((t, d), BF16)] if first else [])
        + [ffs, ffs, ffs],
        input_output_aliases={6 + k: n_row_outs + k for k in range(3)},
        compiler_params=_params(("arbitrary", "arbitrary")),
    )(ids, base, g if first else xn, wg_t, wu_t, wd, *saved)
    return (res[0], res[1] if first else xn, list(res[n_row_outs:]))


def _ffn_bwd(dout, x, g, gd, sl, wg_t, wu_t, wd, tm, name):
    t, d = x.shape
    nb, fp, _ = wg_t.shape

    def body(do_hbm, x_hbm, g_ref, gd_ref, sl_ref, wg_ref, wu_ref, wd_ref, wd_next_ref,
             dx_hbm, dg_ref, da_ref, db_ref, acc, rowbuf, dy_scr, dh_scr, sem):
        i, j = pl.program_id(0), pl.program_id(1)
        rows = pl.ds(pl.multiple_of(i * tm, tm), tm)
        get_do = pltpu.make_async_copy(do_hbm.at[rows, :], rowbuf, sem)
        get_x = pltpu.make_async_copy(x_hbm.at[rows, :], rowbuf, sem)

        @pl.when(j == 0)
        def _():
            get_do.start()
            get_do.wait()
            dy_scr[...] = (0.5 * rowbuf[...]).astype(BF16)
            acc[...] = jnp.zeros_like(acc)
            dh_scr[0] = _nt(dy_scr[...], wd_ref[0])
            get_x.start()

        @pl.when((i == 0) & (j == 0))
        def _():
            dg_ref[...] = jnp.zeros_like(dg_ref)

        dh = dh_scr[j % 2]
        dh_scr[(j + 1) % 2] = _nt(dy_scr[...], wd_next_ref[0])
        da = (dh * gd_ref[...].astype(F32)).astype(BF16)
        db = (dh * sl_ref[...].astype(F32)).astype(BF16)
        da_ref[...] = da
        db_ref[...] = db
        acc[...] += _nn(da, wg_ref[0]) + _nn(db, wu_ref[0])

        @pl.when(j == nb - 1)
        def _():
            get_x.wait()
            dxn, dg = _rms_bwd(rowbuf[...], g_ref[...], acc[...], d)
            dg_ref[...] += dg
            acc[...] = dxn
            get_do.start()
            get_do.wait()
            acc[...] += rowbuf[...]
            out = pltpu.make_async_copy(acc, dx_hbm.at[rows, :], sem)
            out.start()
            out.wait()

    wspec = pl.BlockSpec((1, fp, d), lambda i, j: (j, 0, 0))
    wnext = pl.BlockSpec((1, fp, d), lambda i, j: (jnp.minimum(j + 1, nb - 1), 0, 0))
    vec = pl.BlockSpec((1, d), lambda i, j: (0, 0))
    ff = pl.BlockSpec((tm, fp), lambda i, j: (i, j))
    ffs = jax.ShapeDtypeStruct((t, nb * fp), BF16)
    return pl.pallas_call(
        body, name=name, grid=(t // tm, nb),
        in_specs=[ANY, ANY, vec, ff, ff, wspec, wspec, pl.BlockSpec((1, fp, d), lambda i, j: (0, 0, 0)), wnext],
        out_specs=[ANY, vec, ff, ff],
        out_shape=[jax.ShapeDtypeStruct((t, d), F32), jax.ShapeDtypeStruct((1, d), F32), ffs, ffs],
        scratch_shapes=[pltpu.VMEM((tm, d), F32), pltpu.VMEM((tm, d), F32), pltpu.VMEM((tm, d), BF16),
                        pltpu.VMEM((2, tm, fp), F32), pltpu.SemaphoreType.DMA],
        compiler_params=_params(("arbitrary", "arbitrary")),
    )(dout, x, g, gd, sl, wg_t, wu_t, wd, wd)


def _matmul_tn(lhs, rhs, bm, bn, tk, out_dtype, name, deps=(), out_rows=None):
    t, m = lhs.shape
    n = rhs.shape[1]
    nk = t // tk
    out_rows = m if out_rows is None else out_rows

    def body(l_ref, r_ref, *refs):
        o_ref, acc = refs[len(deps):]
        k = pl.program_id(2)

        @pl.when(k == 0)
        def _():
            acc[...] = jnp.zeros_like(acc)

        acc[...] += _tn(l_ref[...].astype(BF16), r_ref[...].astype(BF16))

        @pl.when(k == nk - 1)
        def _():
            o_ref[...] = acc[...].astype(out_dtype)

    return pl.pallas_call(
        body, name=name, grid=(m // bm, n // bn, nk),
        in_specs=[pl.BlockSpec((tk, bm), lambda i, j, k: (k, i)), pl.BlockSpec((tk, bn), lambda i, j, k: (k, j))]
        + [ANY] * len(deps),
        out_specs=pl.BlockSpec((bm, bn), lambda i, j, k: (i, j)),
        out_shape=jax.ShapeDtypeStruct((out_rows, n), out_dtype),
        scratch_shapes=[pltpu.VMEM((bm, bn), F32)],
        compiler_params=_params(("arbitrary", "arbitrary", "arbitrary")),
    )(lhs, rhs, *deps)


def _matmul_tn_resident(lhs, rhs, bm, out_dtype, name, deps=(), out_rows=None):
    t, m = lhs.shape
    n = rhs.shape[1]
    out_rows = m if out_rows is None else out_rows

    def body(l_ref, r_ref, *refs):
        refs[len(deps)][...] = _tn(l_ref[...], r_ref[...]).astype(out_dtype)

    return pl.pallas_call(
        body, name=name, grid=(m // bm,),
        in_specs=[pl.BlockSpec((t, bm), lambda i: (0, i)), WHOLE_VMEM] + [ANY] * len(deps),
        out_specs=pl.BlockSpec((bm, n), lambda i: (i, 0)),
        out_shape=jax.ShapeDtypeStruct((out_rows, n), out_dtype),
        compiler_params=_params(("arbitrary",)),
    )(lhs, rhs, *deps)


def _last_rows_padded(w_ref, tail_ref, off, real):
    @pl.when(pl.program_id(0) == 0)
    def _():
        tail_ref[...] = jnp.zeros_like(tail_ref)
        tail_ref[0:real, :] = w_ref[off:off + real, :]


def _inproj_fwd(x, g, w_t, splits, tm):
    t, d = x.shape
    offs = [sum(splits[:k]) for k in range(len(splits))]
    real_last = w_t.shape[0] - offs[-1]

    def body(x_ref, g_ref, w_ref, hn_ref, *refs):
        z_refs, tail_ref = refs[:-1], refs[-1]
        _last_rows_padded(w_ref, tail_ref, offs[-1], real_last)
        hn = _rms_fwd(x_ref[...], g_ref[...], d).astype(BF16)
        hn_ref[...] = hn
        for z_ref, o, n in zip(z_refs[:-1], offs, splits):
            z_ref[...] = _nt(hn, w_ref[o:o + n, :])
        z_refs[-1][...] = _nt(hn, tail_ref[...])

    row = pl.BlockSpec((tm, d), lambda i: (i, 0))
    return pl.pallas_call(
        body, name="inproj_fwd", grid=(t // tm,),
        in_specs=[row, pl.BlockSpec((1, d), lambda i: (0, 0)), WHOLE_VMEM],
        out_specs=[row] + [pl.BlockSpec((tm, n), lambda i: (i, 0)) for n in splits],
        out_shape=[jax.ShapeDtypeStruct((t, d), BF16)] + [jax.ShapeDtypeStruct((t, n), F32) for n in splits],
        scratch_shapes=[pltpu.VMEM((splits[-1], d), BF16)],
        compiler_params=_params(("arbitrary",)),
    )(x, g, w_t)


def _inproj_bwd(dzs, x, g, w_t, dres, splits, tm):
    t, d = x.shape
    offs = [sum(splits[:k]) for k in range(len(splits))]
    ni = sum(splits)
    nz = len(splits)
    real_last = w_t.shape[0] - offs[-1]

    def body(*refs):
        dz_refs = refs[:nz]
        x_ref, g_ref, w_ref, dres_ref, dx_ref, dg_ref, dzc_ref, half_ref, tail_ref = refs[nz:]
        _last_rows_padded(w_ref, tail_ref, offs[-1], real_last)
        dhn = jnp.zeros((tm, d), F32)
        for k, (dz_ref, o, n) in enumerate(zip(dz_refs, offs, splits)):
            dz = dz_ref[...].astype(BF16)
            dzc_ref[:, o:o + n] = dz
            dhn += _nn(dz, tail_ref[...] if k == nz - 1 else w_ref[o:o + n, :])
        dx, dg = _rms_bwd(x_ref[...], g_ref[...], dhn, d)
        dx = dres_ref[...] + dx
        dx_ref[...] = dx
        half_ref[...] = (0.5 * dx).astype(BF16)

        @pl.when(pl.program_id(0) == 0)
        def _():
            dg_ref[...] = jnp.zeros_like(dg_ref)

        dg_ref[...] += dg

    row = pl.BlockSpec((tm, d), lambda i: (i, 0))
    vec = pl.BlockSpec((1, d), lambda i: (0, 0))
    return pl.pallas_call(
        body, name="inproj_bwd", grid=(t // tm,),
        in_specs=[pl.BlockSpec((tm, n), lambda i: (i, 0)) for n in splits] + [row, vec, WHOLE_VMEM, row],
        out_specs=[row, vec, pl.BlockSpec((tm, ni), lambda i: (i, 0)), row],
        out_shape=[jax.ShapeDtypeStruct((t, d), F32), jax.ShapeDtypeStruct((1, d), F32),
                   jax.ShapeDtypeStruct((t, ni), BF16), jax.ShapeDtypeStruct((t, d), BF16)],
        scratch_shapes=[pltpu.VMEM((splits[-1], d), BF16)],
        compiler_params=_params(("arbitrary",)),
    )(*dzs, x, g, w_t, dres)


def _tril_bf16(ws_ref, grp):
    rows = lax.broadcasted_iota(jnp.int32, (CHUNK, CHUNK), 0)
    cols = lax.broadcasted_iota(jnp.int32, (CHUNK, CHUNK), 1)
    return jnp.where(rows >= cols, ws_ref[grp], 0.0).astype(BF16)


def _gmlp_mix(zuv_ref, gv_ref, ws_ref, bias_ref, v_scr, mixed_scr, tm, w, groups):
    u = _gelu(zuv_ref[:, 0:w])
    v0 = _gelu(zuv_ref[:, w:2 * w])
    v_scr[...] = _rms_fwd(v0, gv_ref[...], w).astype(BF16)
    for grp in range(groups):
        wsm = _tril_bf16(ws_ref, grp)
        lanes = slice(grp * CHUNK, (grp + 1) * CHUNK)
        for c in range(tm // CHUNK):
            rows = slice(c * CHUNK, (c + 1) * CHUNK)
            mixed_scr[rows, lanes] = _nn(wsm, v_scr[rows, lanes]) + bias_ref[:, lanes]
    return u, v0


def _gmlp_fwd(zuv, gv, ws, bias, gout, tm):
    t, w2 = zuv.shape
    w = w2 // 2
    groups = ws.shape[0]

    def body(zuv_ref, gv_ref, ws_ref, bias_ref, go_ref, y_ref, v_scr, mixed_scr):
        u, _ = _gmlp_mix(zuv_ref, gv_ref, ws_ref, bias_ref, v_scr, mixed_scr, tm, w, groups)
        ya = u * mixed_scr[...]
        for grp in range(groups):
            lanes = slice(grp * CHUNK, (grp + 1) * CHUNK)
            y_ref[:, lanes] = _rms_fwd(ya[:, lanes], go_ref[:, lanes], CHUNK).astype(BF16)

    const2 = lambda i: (0, 0)
    return pl.pallas_call(
        body, name="gmlp_fwd", grid=(t // tm,),
        in_specs=[pl.BlockSpec((tm, w2), lambda i: (i, 0)), pl.BlockSpec((1, w), const2),
                  pl.BlockSpec((groups, CHUNK, CHUNK), lambda i: (0, 0, 0)),
                  pl.BlockSpec((CHUNK, w), const2), pl.BlockSpec((1, w), const2)],
        out_specs=pl.BlockSpec((tm, w), lambda i: (i, 0)),
        out_shape=jax.ShapeDtypeStruct((t, w), BF16),
        scratch_shapes=[pltpu.VMEM((tm, w), BF16), pltpu.VMEM((tm, w), F32)],
        compiler_params=_params(("arbitrary",)),
    )(zuv, gv, ws, bias, gout)


def _gmlp_bwd(dy, zuv, gv, ws, bias, gout, tm):
    t, w2 = zuv.shape
    w = w2 // 2
    groups = ws.shape[0]

    def body(dy_ref, zuv_ref, gv_ref, ws_ref, bias_ref, go_ref,
             dz_ref, dws_ref, dbias_ref, dgv_ref, dgo_ref, v_scr, mixed_scr, dmix_scr, dv_scr):
        @pl.when(pl.program_id(0) == 0)
        def _():
            dws_ref[...] = jnp.zeros_like(dws_ref)
            dbias_ref[...] = jnp.zeros_like(dbias_ref)
            dgv_ref[...] = jnp.zeros_like(dgv_ref)
            dgo_ref[...] = jnp.zeros_like(dgo_ref)

        u, v0 = _gmlp_mix(zuv_ref, gv_ref, ws_ref, bias_ref, v_scr, mixed_scr, tm, w, groups)
        mixed = mixed_scr[...]
        ya = u * mixed
        for grp in range(groups):
            lanes = slice(grp * CHUNK, (grp + 1) * CHUNK)
            dya, dgo = _rms_bwd(ya[:, lanes], go_ref[:, lanes], dy_ref[:, lanes], CHUNK)
            dgo_ref[:, lanes] += dgo
            dz_ref[:, lanes] = dya * mixed[:, lanes] * _gelu_grad(zuv_ref[:, lanes])
            dmix_scr[:, lanes] = dya * u[:, lanes]
        for grp in range(groups):
            wsm = _tril_bf16(ws_ref, grp)
            lanes = slice(grp * CHUNK, (grp + 1) * CHUNK)
            dws = jnp.zeros((CHUNK, CHUNK), F32)
            dbias = jnp.zeros((CHUNK, CHUNK), F32)
            for c in range(tm // CHUNK):
                rows = slice(c * CHUNK, (c + 1) * CHUNK)
                dm = dmix_scr[rows, lanes]
                dmb = dm.astype(BF16)
                dv_scr[rows, lanes] = _tn(wsm, dmb)
                dws += _nt(dmb, v_scr[rows, lanes])
                dbias += dm
            rr = lax.broadcasted_iota(jnp.int32, (CHUNK, CHUNK), 0)
            cc = lax.broadcasted_iota(jnp.int32, (CHUNK, CHUNK), 1)
            dws_ref[grp] += jnp.where(rr >= cc, dws, 0.0)
            dbias_ref[grp] += jnp.sum(dbias, axis=1, keepdims=True)
        dv0, dgv = _rms_bwd(v0, gv_ref[...], dv_scr[...], w)
        dgv_ref[...] += dgv
        dz_ref[:, w:2 * w] = dv0 * _gelu_grad(zuv_ref[:, w:2 * w])

    const2 = lambda i: (0, 0)
    const3 = lambda i: (0, 0, 0)
    return pl.pallas_call(
        body, name="gmlp_bwd", grid=(t // tm,),
        in_specs=[pl.BlockSpec((tm, w), lambda i: (i, 0)), pl.BlockSpec((tm, w2), lambda i: (i, 0)),
                  pl.BlockSpec((1, w), const2), pl.BlockSpec((groups, CHUNK, CHUNK), const3),
                  pl.BlockSpec((CHUNK, w), const2), pl.BlockSpec((1, w), const2)],
        out_specs=[pl.BlockSpec((tm, w2), lambda i: (i, 0)), pl.BlockSpec((groups, CHUNK, CHUNK), const3),
                   pl.BlockSpec((groups, CHUNK, 1), const3), pl.BlockSpec((1, w), const2), pl.BlockSpec((1, w), const2)],
        out_shape=[jax.ShapeDtypeStruct((t, w2), F32), jax.ShapeDtypeStruct((groups, CHUNK, CHUNK), F32),
                   jax.ShapeDtypeStruct((groups, CHUNK, 1), F32), jax.ShapeDtypeStruct((1, w), F32),
                   jax.ShapeDtypeStruct((1, w), F32)],
        scratch_shapes=[pltpu.VMEM((tm, w), BF16), pltpu.VMEM((tm, w), F32),
                        pltpu.VMEM((tm, w), F32), pltpu.VMEM((tm, w), F32)],
        compiler_params=_params(("arbitrary",)),
    )(dy, zuv, gv, ws, bias, gout)


def _rot(x, m_lo, m_hi):
    return pltpu.roll(x, LANE - ROPE // 2, 1) * m_lo + pltpu.roll(x, ROPE // 2, 1) * m_hi


def _rope_tables(pos, freq, tm, after):
    t = pos.shape[0]

    def body(pos_ref, freq_ref, _, cos_ref, sin_ref):
        ang = pos_ref[...] * freq_ref[...]
        cos_ref[...] = jnp.cos(ang)
        sin_ref[...] = jnp.sin(ang)

    tab = pl.BlockSpec((tm, LANE), lambda i: (i, 0))
    return pl.pallas_call(
        body, name="rope_tables", grid=(t // tm,),
        in_specs=[pl.BlockSpec((tm, 1), lambda i: (i, 0)), pl.BlockSpec((1, LANE), lambda i: (0, 0)), ANY],
        out_specs=[tab, tab], out_shape=[jax.ShapeDtypeStruct((t, LANE), F32)] * 2,
        compiler_params=_params(("arbitrary",)),
    )(pos, freq, after)


def _mla_proj_fwd(cq, ckv, krw, cos, sin, masks, gq, gkv, wq_t, wkv_t, gqh, gkh, tm):
    t, rq = cq.shape
    rkv = ckv.shape[1]
    heads = wq_t.shape[0]

    def body(cq_ref, ckv_ref, kr_ref, cos_ref, sin_ref, mk_ref, gq_ref, gkv_ref, wq_ref, wkv_ref,
             gqh_ref, gkh_ref, q_ref, k_ref, v_ref):
        cos, sin = cos_ref[...], sin_ref[...]
        m_lo, m_hi = mk_ref[0:1, :], mk_ref[1:2, :]
        cqn = _rms_fwd(cq_ref[...], gq_ref[...], rq).astype(BF16)
        ckvn = _rms_fwd(ckv_ref[...], gkv_ref[...], rkv).astype(BF16)
        kr = kr_ref[...]
        kr_ss = jnp.sum(kr * kr, axis=-1, keepdims=True)
        for h in range(heads):
            qh = _nt(cqn, wq_ref[h])
            qn = qh * _rstd(qh, QK) * gqh_ref[...]
            qr = qn[:, LANE:]
            q_ref[h, :, 0:LANE] = qn[:, 0:LANE].astype(BF16)
            q_ref[h, :, LANE:] = (qr * cos + _rot(qr, m_lo, m_hi) * sin).astype(BF16)
            kvh = _nt(ckvn, wkv_ref[h])
            kn = kvh[:, 0:LANE]
            rk = lax.rsqrt((jnp.sum(kn * kn, axis=-1, keepdims=True) + kr_ss) * (1.0 / QK) + EPS)
            k_ref[h, :, 0:LANE] = (kn * rk * gkh_ref[:, 0:LANE]).astype(BF16)
            krn = kr * rk * gkh_ref[:, LANE:]
            k_ref[h, :, LANE:] = (krn * cos + _rot(krn, m_lo, m_hi) * sin).astype(BF16)
            v_ref[h] = kvh[:, LANE:].astype(BF16)

    c2 = lambda i: (0, 0)
    c3 = lambda i: (0, 0, 0)
    return pl.pallas_call(
        body, name="mla_proj_fwd", grid=(t // tm,),
        in_specs=[pl.BlockSpec((tm, rq), lambda i: (i, 0)), pl.BlockSpec((tm, rkv), lambda i: (i, 0)),
                  pl.BlockSpec((tm, LANE), lambda i: (i, 0)), pl.BlockSpec((tm, LANE), lambda i: (i, 0)),
                  pl.BlockSpec((tm, LANE), lambda i: (i, 0)), pl.BlockSpec((2, LANE), c2),
                  pl.BlockSpec((1, rq), c2), pl.BlockSpec((1, rkv), c2),
                  pl.BlockSpec((heads, HEADW, rq), c3), pl.BlockSpec((heads, HEADW, rkv), c3),
                  pl.BlockSpec((1, HEADW), c2), pl.BlockSpec((1, HEADW), c2)],
        out_specs=[pl.BlockSpec((heads, tm, HEADW), lambda i: (0, i, 0)),
                   pl.BlockSpec((heads, tm, HEADW), lambda i: (0, i, 0)),
                   pl.BlockSpec((heads, tm, VHEAD), lambda i: (0, i, 0))],
        out_shape=[jax.ShapeDtypeStruct((heads, t, HEADW), BF16), jax.ShapeDtypeStruct((heads, t, HEADW), BF16),
                   jax.ShapeDtypeStruct((heads, t, VHEAD), BF16)],
        compiler_params=_params(("arbitrary",)),
    )(cq, ckv, krw, cos, sin, masks, gq, gkv, wq_t, wkv_t, gqh, gkh)


def _mla_proj_bwd(dq, dk, dv, cq, ckv, krw, cos, sin, masks, gq, gkv, wq_t, wkv_t, gqh, gkh, tm):
    t, rq = cq.shape
    rkv = ckv.shape[1]
    heads = wq_t.shape[0]

    def body(dq_ref, dk_ref, dv_ref, cq_ref, ckv_ref, kr_ref, cos_ref, sin_ref, mk_ref, gq_ref, gkv_ref,
             wq_ref, wkv_ref, gqh_ref, gkh_ref,
             dcq_ref, dckv_ref, dkr_ref, dwq_ref, dwkv_ref, dgq_ref, dgkv_ref, dgqh_ref, dgkh_ref):
        @pl.when(pl.program_id(0) == 0)
        def _():
            for r in (dwq_ref, dwkv_ref, dgq_ref, dgkv_ref, dgqh_ref, dgkh_ref):
                r[...] = jnp.zeros_like(r)

        cos, sin = cos_ref[...], sin_ref[...]
        m_lo, m_hi = mk_ref[0:1, :], mk_ref[1:2, :]

        def unrope(dy):
            return dy * cos - _rot(dy * sin, m_lo, m_hi)

        cqn = _rms_fwd(cq_ref[...], gq_ref[...], rq).astype(BF16)
        ckvn = _rms_fwd(ckv_ref[...], gkv_ref[...], rkv).astype(BF16)
        kr = kr_ref[...]
        kr_ss = jnp.sum(kr * kr, axis=-1, keepdims=True)
        dcqn = jnp.zeros((tm, rq), F32)
        dckvn = jnp.zeros((tm, rkv), F32)
        dkr = jnp.zeros((tm, LANE), F32)
        for h in range(heads):
            qh = _nt(cqn, wq_ref[h])
            dqn = jnp.concatenate([dq_ref[h, :, 0:LANE], unrope(dq_ref[h, :, LANE:])], axis=1)
            dqh, dg = _rms_bwd(qh, gqh_ref[...], dqn, QK)
            dgqh_ref[...] += dg
            dqh = dqh.astype(BF16)
            dcqn += _nn(dqh, wq_ref[h])
            dwq_ref[h] += _tn(dqh, cqn)

            kvh = _nt(ckvn, wkv_ref[h])
            kn = kvh[:, 0:LANE]
            rk = lax.rsqrt((jnp.sum(kn * kn, axis=-1, keepdims=True) + kr_ss) * (1.0 / QK) + EPS)
            dkn_n = dk_ref[h, :, 0:LANE]
            dkr_n = unrope(dk_ref[h, :, LANE:])
            knh, krh = kn * rk, kr * rk
            dgkh_ref[:, 0:LANE] += jnp.sum(dkn_n * knh, axis=0, keepdims=True)
            dgkh_ref[:, LANE:] += jnp.sum(dkr_n * krh, axis=0, keepdims=True)
            dkn_g, dkr_g = dkn_n * gkh_ref[:, 0:LANE], dkr_n * gkh_ref[:, LANE:]
            proj = (jnp.sum(dkn_g * knh, axis=-1, keepdims=True)
                    + jnp.sum(dkr_g * krh, axis=-1, keepdims=True)) * (1.0 / QK)
            dkr += rk * (dkr_g - krh * proj)
            dkvh = jnp.concatenate([rk * (dkn_g - knh * proj), dv_ref[h]], axis=1).astype(BF16)
            dckvn += _nn(dkvh, wkv_ref[h])
            dwkv_ref[h] += _tn(dkvh, ckvn)
        dkr_ref[...] = dkr
        dcq, dg = _rms_bwd(cq_ref[...], gq_ref[...], dcqn, rq)
        dcq_ref[...] = dcq
        dgq_ref[...] += dg
        dckv, dg = _rms_bwd(ckv_ref[...], gkv_ref[...], dckvn, rkv)
        dckv_ref[...] = dckv
        dgkv_ref[...] += dg

    c2 = lambda i: (0, 0)
    c3 = lambda i: (0, 0, 0)
    hq = pl.BlockSpec((heads, tm, HEADW), lambda i: (0, i, 0))
    return pl.pallas_call(
        body, name="mla_proj_bwd", grid=(t // tm,),
        in_specs=[hq, hq, pl.BlockSpec((heads, tm, VHEAD), lambda i: (0, i, 0)),
                  pl.BlockSpec((tm, rq), lambda i: (i, 0)), pl.BlockSpec((tm, rkv), lambda i: (i, 0)),
                  pl.BlockSpec((tm, LANE), lambda i: (i, 0)), pl.BlockSpec((tm, LANE), lambda i: (i, 0)),
                  pl.BlockSpec((tm, LANE), lambda i: (i, 0)), pl.BlockSpec((2, LANE), c2),
                  pl.BlockSpec((1, rq), c2), pl.BlockSpec((1, rkv), c2),
                  pl.BlockSpec((heads, HEADW, rq), c3), pl.BlockSpec((heads, HEADW, rkv), c3),
                  pl.BlockSpec((1, HEADW), c2), pl.BlockSpec((1, HEADW), c2)],
        out_specs=[pl.BlockSpec((tm, rq), lambda i: (i, 0)), pl.BlockSpec((tm, rkv), lambda i: (i, 0)),
                   pl.BlockSpec((tm, LANE), lambda i: (i, 0)),
                   pl.BlockSpec((heads, HEADW, rq), c3), pl.BlockSpec((heads, HEADW, rkv), c3),
                   pl.BlockSpec((1, rq), c2), pl.BlockSpec((1, rkv), c2),
                   pl.BlockSpec((1, HEADW), c2), pl.BlockSpec((1, HEADW), c2)],
        out_shape=[jax.ShapeDtypeStruct((t, rq), F32), jax.ShapeDtypeStruct((t, rkv), F32),
                   jax.ShapeDtypeStruct((t, LANE), F32),
                   jax.ShapeDtypeStruct((heads, HEADW, rq), F32), jax.ShapeDtypeStruct((heads, HEADW, rkv), F32),
                   jax.ShapeDtypeStruct((1, rq), F32), jax.ShapeDtypeStruct((1, rkv), F32),
                   jax.ShapeDtypeStruct((1, HEADW), F32), jax.ShapeDtypeStruct((1, HEADW), F32)],
        compiler_params=_params(("arbitrary",)),
    )(dq, dk, dv, cq, ckv, krw, cos, sin, masks, gq, gkv, wq_t, wkv_t, gqh, gkh)


def _lower_triangle(blk):
    return lax.broadcasted_iota(jnp.int32, (blk, blk), 0) >= lax.broadcasted_iota(jnp.int32, (blk, blk), 1)


def _attn_fwd(q, k, v, seq, blk):
    heads, t, _ = q.shape
    scale = QK ** -0.5
    nblk = seq // blk

    def body(q_ref, k_ref, v_ref, o_ref, lse_ref):
        tri = _lower_triangle(blk)
        for qi in range(nblk):
            rows = slice(qi * blk, (qi + 1) * blk)
            before = slice(0, qi * blk)
            qb = q_ref[0, rows, :]
            s_d = jnp.where(tri, _nt(qb, k_ref[0, rows, :]) * scale, -1e30)
            m = jnp.max(s_d, axis=-1, keepdims=True)
            if qi:
                s_b = _nt(qb, k_ref[0, before, :]) * scale
                m = jnp.maximum(m, jnp.max(s_b, axis=-1, keepdims=True))
                p_b = jnp.exp(s_b - m)
            p_d = jnp.exp(s_d - m)
            l = jnp.sum(p_d, axis=-1, keepdims=True)
            acc = _nn(p_d.astype(BF16), v_ref[0, rows, :])
            if qi:
                l += jnp.sum(p_b, axis=-1, keepdims=True)
                acc += _nn(p_b.astype(BF16), v_ref[0, before, :])
            o_ref[0, rows, :] = acc / l
            lse_ref[0, rows, :] = m + jnp.log(l)

    return pl.pallas_call(
        body, name="attn_fwd", grid=(heads, t // seq),
        in_specs=[pl.BlockSpec((1, seq, HEADW), lambda h, b: (h, b, 0)),
                  pl.BlockSpec((1, seq, HEADW), lambda h, b: (h, b, 0)),
                  pl.BlockSpec((1, seq, VHEAD), lambda h, b: (h, b, 0))],
        out_specs=[pl.BlockSpec((1, seq, VHEAD), lambda h, b: (h, b, 0)),
                   pl.BlockSpec((1, seq, 1), lambda h, b: (h, b, 0))],
        out_shape=[jax.ShapeDtypeStruct((heads, t, VHEAD), F32), jax.ShapeDtypeStruct((heads, t, 1), F32)],
        compiler_params=_params(("arbitrary", "arbitrary")),
    )(q, k, v)


def _attn_bwd(q, k, v, do, lse, delta, seq, blk, after):
    heads, t, _ = q.shape
    scale = QK ** -0.5
    nblk = seq // blk

    def body(q_ref, k_ref, v_ref, do_ref, lse_ref, dl_ref, _, dq_ref, dk_ref, dv_ref):
        tri = _lower_triangle(blk)
        dk_ref[...] = jnp.zeros_like(dk_ref)
        dv_ref[...] = jnp.zeros_like(dv_ref)
        for qi in range(nblk):
            rows = slice(qi * blk, (qi + 1) * blk)
            qb = q_ref[0, rows, :]
            dob = do_ref[0, rows, :]
            lse_b = lse_ref[0, rows, :]
            dl_b = dl_ref[0, rows, :]
            dq = jnp.zeros((blk, HEADW), F32)
            for keys, masked in ((slice(0, qi * blk), False), (rows, True)):
                if keys.stop == keys.start:
                    continue
                kb = k_ref[0, keys, :]
                p = jnp.exp(_nt(qb, kb) * scale - lse_b)
                if masked:
                    p = jnp.where(tri, p, 0.0)
                dp = _nt(dob, v_ref[0, keys, :])
                ds = (p * (dp - dl_b) * scale).astype(BF16)
                dv_ref[0, keys, :] += _tn(p.astype(BF16), dob)
                dk_ref[0, keys, :] += _tn(ds, qb)
                dq += _nn(ds, kb)
            dq_ref[0, rows, :] = dq

    hq = pl.BlockSpec((1, seq, HEADW), lambda h, b: (h, b, 0))
    hv = pl.BlockSpec((1, seq, VHEAD), lambda h, b: (h, b, 0))
    h1 = pl.BlockSpec((1, seq, 1), lambda h, b: (h, b, 0))
    return pl.pallas_call(
        body, name="attn_bwd", grid=(heads, t // seq),
        in_specs=[hq, hq, hv, hv, h1, h1, ANY],
        out_specs=[hq, hq, hv],
        out_shape=[jax.ShapeDtypeStruct((heads, t, HEADW), F32), jax.ShapeDtypeStruct((heads, t, HEADW), F32),
                   jax.ShapeDtypeStruct((heads, t, VHEAD), F32)],
        compiler_params=_params(("arbitrary", "arbitrary")),
    )(q, k, v, do, lse, delta, after)


def _out_fwd(ya, o, gb, w_out, x1, tm):
    t, w = ya.shape
    heads = o.shape[0]
    d = x1.shape[1]

    def body(ya_ref, o_ref, gb_ref, w_ref, x1_ref, x2_ref, yc_ref):
        yc_ref[:, 0:w] = ya_ref[...]
        for h in range(heads):
            lanes = slice(h * VHEAD, (h + 1) * VHEAD)
            yc_ref[:, w + h * VHEAD:w + (h + 1) * VHEAD] = _rms_fwd(o_ref[h], gb_ref[:, lanes], VHEAD).astype(BF16)
        x2_ref[...] = x1_ref[...] + _nn(yc_ref[...], w_ref[...])

    wy = w + heads * VHEAD
    row = pl.BlockSpec((tm, d), lambda i: (i, 0))
    return pl.pallas_call(
        body, name="out_fwd", grid=(t // tm,),
        in_specs=[pl.BlockSpec((tm, w), lambda i: (i, 0)), pl.BlockSpec((heads, tm, VHEAD), lambda i: (0, i, 0)),
                  pl.BlockSpec((1, heads * VHEAD), lambda i: (0, 0)), WHOLE_VMEM, row],
        out_specs=[row, pl.BlockSpec((tm, wy), lambda i: (i, 0))],
        out_shape=[jax.ShapeDtypeStruct((t, d), F32), jax.ShapeDtypeStruct((t, wy), BF16)],
        compiler_params=_params(("arbitrary",)),
    )(ya, o, gb, w_out, x1)


def _out_bwd(dx2, o, gb, w_out, w, tm):
    t, d = dx2.shape
    heads = o.shape[0]

    def body(dx_ref, o_ref, gb_ref, w_ref, dya_ref, do_ref, dl_ref, dgb_ref):
        @pl.when(pl.program_id(0) == 0)
        def _():
            dgb_ref[...] = jnp.zeros_like(dgb_ref)

        dyc = _nt(dx_ref[...].astype(BF16), w_ref[...])
        dya_ref[...] = dyc[:, 0:w]
        for h in range(heads):
            lanes = slice(h * VHEAD, (h + 1) * VHEAD)
            oh = o_ref[h]
            doh, dg = _rms_bwd(oh, gb_ref[:, lanes], dyc[:, w + h * VHEAD:w + (h + 1) * VHEAD], VHEAD)
            dgb_ref[:, lanes] += dg
            do_ref[h] = doh.astype(BF16)
            dl_ref[h] = jnp.sum(doh * oh, axis=-1, keepdims=True)

    ho = pl.BlockSpec((heads, tm, VHEAD), lambda i: (0, i, 0))
    vec = pl.BlockSpec((1, heads * VHEAD), lambda i: (0, 0))
    return pl.pallas_call(
        body, name="out_bwd", grid=(t // tm,),
        in_specs=[pl.BlockSpec((tm, d), lambda i: (i, 0)), ho, vec, WHOLE_VMEM],
        out_specs=[pl.BlockSpec((tm, w), lambda i: (i, 0)), ho, pl.BlockSpec((heads, tm, 1), lambda i: (0, i, 0)), vec],
        out_shape=[jax.ShapeDtypeStruct((t, w), F32), jax.ShapeDtypeStruct((heads, t, VHEAD), BF16),
                   jax.ShapeDtypeStruct((heads, t, 1), F32), jax.ShapeDtypeStruct((1, heads * VHEAD), F32)],
        compiler_params=_params(("arbitrary",)),
    )(dx2, o, gb, w_out)


def _loss_head(y, target, tm):
    t, d = y.shape

    def body(y_ref, t_ref, dy_ref, half_ref, loss_ref):
        @pl.when(pl.program_id(0) == 0)
        def _():
            loss_ref[...] = jnp.zeros_like(loss_ref)

        err = y_ref[...] - t_ref[...]
        dy = err * (1.0 / d)
        dy_ref[...] = dy
        half_ref[...] = (0.5 * dy).astype(BF16)
        part = jnp.sum(jnp.sum(err * err, axis=-1, keepdims=True) * (1.0 / d), axis=0, keepdims=True)
        loss_ref[...] += 0.5 * part

    row = pl.BlockSpec((tm, d), lambda i: (i, 0))
    return pl.pallas_call(
        body, name="loss_head", grid=(t // tm,),
        in_specs=[row, row], out_specs=[row, row, pl.BlockSpec((1, 1), lambda i: (0, 0))],
        out_shape=[jax.ShapeDtypeStruct((t, d), F32), jax.ShapeDtypeStruct((t, d), BF16),
                   jax.ShapeDtypeStruct((1, 1), F32)],
        compiler_params=_params(("arbitrary",)),
    )(y, target)


def _place():
    return lax.axis_index("x"), lax.axis_index("y"), lax.axis_index("c")


HBM = pl.BlockSpec(memory_space=pltpu.HBM)
SEM = pl.BlockSpec(memory_space=pltpu.SEMAPHORE)
DATAFLOW = pltpu.SideEffectType.DATAFLOW_SIDE_EFFECTING


def _plan_copies(plan, refs, send_sems, recv_sems):
    def block(ref, blk):
        if blk is None:
            return ref
        return ref.at[blk[0], pl.ds(0, blk[1])] if isinstance(blk, tuple) else ref.at[blk]

    cps = []
    for i, (sb, sblk, db, dblk, dev) in enumerate(plan(*_place())):
        cps.append(pltpu.make_async_remote_copy(
            src_ref=block(refs[sb], sblk), dst_ref=block(refs[db], dblk),
            send_sem=send_sems.at[i], recv_sem=recv_sems.at[i], device_id=dev, device_id_type=MESH))
    return cps


def _push_start(bufs, plan, ncopy, name, deps=()):
    nb = len(bufs)

    def body(*refs):
        outs = refs[nb + len(deps):]
        for cp in _plan_copies(plan, refs[:nb], outs[0], outs[1]):
            cp.start()
        outs[-1][...] = jnp.zeros_like(outs[-1])

    res = pl.pallas_call(
        body, name=name,
        out_shape=(pltpu.SemaphoreType.DMA((ncopy,)), pltpu.SemaphoreType.DMA((ncopy,)),
                   *[pltpu.HBM(b.shape, b.dtype) for b in bufs], jax.ShapeDtypeStruct((SUBLANE, LANE), F32)),
        in_specs=[HBM] * nb + [ANY] * len(deps),
        out_specs=(SEM, SEM, *[HBM] * nb, WHOLE_VMEM),
        input_output_aliases={i: 2 + i for i in range(nb)},
        compiler_params=pltpu.CompilerParams(has_side_effects=DATAFLOW),
    )(*[pltpu.with_memory_space_constraint(b, pltpu.HBM) for b in bufs], *deps)
    return res[0], res[1], list(res[2:2 + nb]), res[-1]


def _push_wait(send_sems, recv_sems, bufs, plan, after, name):
    nb = len(bufs)

    def body(*refs):
        for cp in _plan_copies(plan, refs[:nb], refs[nb], refs[nb + 1]):
            cp.wait_send()
            cp.wait_recv()

    res = pl.pallas_call(
        body, name=name,
        out_shape=[pltpu.HBM(b.shape, b.dtype) for b in bufs],
        in_specs=[HBM] * nb + [SEM, SEM, ANY], out_specs=[HBM] * nb,
        input_output_aliases={i: i for i in range(nb)},
        compiler_params=pltpu.CompilerParams(has_side_effects=DATAFLOW),
    )(*bufs, send_sems, recv_sems, after)
    return list(res)


def _other_chips(x, y):
    return ((1 - x, y), (x, 1 - y), (1 - x, 1 - y))


class _Exchange:
    def __init__(self, bufs, plan, ncopy, name, deps=()):
        self.plan, self.name = plan, name
        self.send, self.recv, self.bufs, self.token = _push_start(bufs, plan, ncopy, name + "_start", deps)

    def wait(self, after):
        return _push_wait(self.send, self.recv, self.bufs, self.plan, after, self.name + "_wait")


class _Chain:
    def __init__(self, bufs):
        self.bufs = list(bufs)

    def start(self, plan, ncopy, name, deps=()):
        send, recv, self.bufs, token = _push_start(self.bufs, plan, ncopy, name + "_start", deps)
        return (send, recv, plan, name), token

    def wait(self, pending, after):
        send, recv, plan, name = pending
        self.bufs = _push_wait(send, recv, self.bufs, plan, after, name + "_wait")


class _StagedGather:
    def __init__(self, shards, me, name, pad_to=None):
        self.n = n = len(shards)
        self.name = name
        rows = shards[0].shape[0]
        lands = []
        for s in shards:
            land = lax.empty((N_DEV, pad_to or rows) + s.shape[1:], s.dtype)
            if pad_to and pad_to != rows:
                land = lax.dynamic_update_slice(
                    land, jnp.zeros((N_DEV, pad_to - rows) + s.shape[1:], s.dtype), (0, rows, 0))
            lands.append(lax.dynamic_update_slice(land, s[None], (me, 0, 0)))
        self.chain = _Chain(list(shards) + lands)
        self.pending = {}

        def blk(b):
            return (b, rows) if pad_to and pad_to != rows else b

        def to_sibling(blocks):
            return lambda x, y, c: [(n + a, blk(b), n + a, blk(b), (x, y, 1 - c))
                                    for a in range(n) for b in blocks(x, y, c)]

        def nbr_blocks(x, y, c):
            return [4 * (1 - x) + 2 * y + c, 4 * x + 2 * (1 - y) + c]

        def diag(x, y, c):
            sx, sy = (1 - x) * (1 - c) + x * c, y * (1 - c) + (1 - y) * c
            tx, ty = x * (1 - c) + (1 - x) * c, (1 - y) * (1 - c) + y * c
            b = blk(4 * sx + 2 * sy + c)
            return [(n + a, b, n + a, b, (tx, ty, c)) for a in range(n)]

        self.plans = {
            "own": (lambda x, y, c: [(a, None, n + a, blk(4 * x + 2 * y + c), (x, y, 1 - c)) for a in range(n)], n),
            "nbr": (lambda x, y, c: [(a, None, n + a, blk(4 * x + 2 * y + c), dev) for a in range(n)
                                     for dev in ((1 - x, y, c), (x, 1 - y, c))], 2 * n),
            "diag": (diag, n),
            "nbr_d2d": (to_sibling(nbr_blocks), 2 * n),
            "own_nbr_d2d": (to_sibling(lambda x, y, c: [4 * x + 2 * y + c] + nbr_blocks(x, y, c)), 3 * n),
            "diag_d2d": (to_sibling(lambda x, y, c: [4 * (1 - x) + 2 * (1 - y) + c]), n),
        }

    def start(self, stage, deps=()):
        plan, ncopy = self.plans[stage]
        self.pending[stage], token = self.chain.start(plan, ncopy, self.name + "_" + stage, deps)
        return token

    def wait(self, stage, after):
        self.chain.wait(self.pending.pop(stage), after)

    def lands(self):
        return self.chain.bufs[self.n:]


def _gather_ici(shards, me, name, deps=()):
    n = len(shards)
    lands = [lax.dynamic_update_slice(lax.empty((N_DEV,) + s.shape, s.dtype), s[None], (me, 0, 0)) for s in shards]

    def plan(x, y, c):
        return [(a, None, n + a, 4 * x + 2 * y + c, (px, py, c)) for a in range(n) for px, py in _other_chips(x, y)]

    return _Exchange(list(shards) + lands, plan, 3 * n, name, deps)


def _gather_d2d(lands, name, deps=()):
    n = len(lands)

    def plan(x, y, c):
        blocks = [4 * x + 2 * y + c] + [4 * px + 2 * py + c for px, py in _other_chips(x, y)]
        return [(a, b, a, b, (x, y, 1 - c)) for a in range(n) for b in blocks]

    return _Exchange(list(lands), plan, 4 * n, name, deps)


def _reduce_d2d(grads, name, deps=(), rows=None):
    n = len(grads)
    lands = [lax.empty((4,) + g.shape[1:], g.dtype) for g in grads]

    def blk(b):
        return b if rows is None else (b, rows)

    def plan(x, y, c):
        return [(a, blk(2 * s + (1 - c)), n + a, blk(s), (x, y, 1 - c)) for a in range(n) for s in range(4)]

    return _Exchange(list(grads) + lands, plan, 4 * n, name, deps)


def _reduce_ici(chip, name, deps=(), rows=None):
    n = len(chip)
    lands = [lax.empty((3,) + g.shape[1:], g.dtype) for g in chip]

    def blk(b):
        return b if rows is None else (b, rows)

    def plan(x, y, c):
        return [(a, blk(2 * px + py), n + a, blk(k), (px, py, c))
                for a in range(n) for k, (px, py) in enumerate(_other_chips(x, y))]

    return _Exchange(list(chip) + lands, plan, 3 * n, name, deps)


def _pair_add(full, got, core, name, rows=None):
    _, r, cdim = full.shape
    br = _row_block(rows or r, 512)

    def body(c_ref, f_ref, g_ref, o_ref):
        o_ref[...] = (f_ref[...].astype(F32) + g_ref[...].astype(F32)).astype(o_ref.dtype)

    return pl.pallas_call(
        body, name=name,
        grid_spec=pltpu.PrefetchScalarGridSpec(
            num_scalar_prefetch=1, grid=(4, (rows or r) // br),
            in_specs=[pl.BlockSpec((1, br, cdim), lambda s, i, c_ref: (2 * s + c_ref[0], i, 0)),
                      pl.BlockSpec((1, br, cdim), lambda s, i, c_ref: (s, i, 0))],
            out_specs=pl.BlockSpec((1, br, cdim), lambda s, i, c_ref: (s, i, 0))),
        out_shape=jax.ShapeDtypeStruct((4, r, cdim), full.dtype),
        compiler_params=_params(("arbitrary", "arbitrary")),
    )(core, full, got)


def _sum_devices(stack):
    _, r, cdim = stack.shape

    def body(s_ref, o_ref):
        acc = s_ref[0]
        for k in range(1, N_DEV):
            acc = acc + s_ref[k]
        o_ref[...] = acc

    return pl.pallas_call(
        body, name="sum_devices", out_shape=jax.ShapeDtypeStruct((r, cdim), F32),
        compiler_params=_params(),
    )(stack)


def _adamw(w, g, m, v, name):
    r, cdim = w.shape
    br = _row_block(r, 256)

    def body(w_ref, g_ref, m_ref, v_ref, d_ref, nm_ref, nv_ref):
        g = g_ref[...]
        nm = ADAM_B1 * m_ref[...] + (1.0 - ADAM_B1) * g
        nv = ADAM_B2 * v_ref[...] + (1.0 - ADAM_B2) * (g * g)
        m_hat = nm / (1.0 - ADAM_B1 ** ADAM_STEP)
        v_hat = nv / (1.0 - ADAM_B2 ** ADAM_STEP)
        d_ref[...] = -ADAM_LR * (m_hat / (jnp.sqrt(v_hat) + ADAM_EPS) + ADAM_WD * w_ref[...])
        nm_ref[...] = nm
        nv_ref[...] = nv

    spec = pl.BlockSpec((br, cdim), lambda i: (i, 0))
    shape = jax.ShapeDtypeStruct((r, cdim), F32)
    return pl.pallas_call(
        body, name=name, grid=(r // br,), in_specs=[spec] * 4, out_specs=[spec] * 3,
        out_shape=[shape] * 3, compiler_params=_params(("arbitrary",)),
    )(w, g, m, v)


def _sum_adamw(chip, got, slot, w, m, v, name, after):
    rows, cdim = w.shape
    bc = 2 * LANE if cdim % (2 * LANE) == 0 else cdim

    def body(s_ref, c_ref, g_ref, w_ref, m_ref, v_ref, _, go_ref, d_ref, nm_ref, nv_ref):
        g = c_ref[0].astype(F32)
        for k in range(3):
            g = g + g_ref[k].astype(F32)
        nm = ADAM_B1 * m_ref[...] + (1.0 - ADAM_B1) * g
        nv = ADAM_B2 * v_ref[...] + (1.0 - ADAM_B2) * (g * g)
        m_hat = nm / (1.0 - ADAM_B1 ** ADAM_STEP)
        v_hat = nv / (1.0 - ADAM_B2 ** ADAM_STEP)
        go_ref[...] = g
        d_ref[...] = -ADAM_LR * (m_hat / (jnp.sqrt(v_hat) + ADAM_EPS) + ADAM_WD * w_ref[...])
        nm_ref[...] = nm
        nv_ref[...] = nv

    spec = pl.BlockSpec((rows, bc), lambda j, s_ref: (0, j))
    shape = jax.ShapeDtypeStruct((rows, cdim), F32)
    return pl.pallas_call(
        body, name=name,
        grid_spec=pltpu.PrefetchScalarGridSpec(
            num_scalar_prefetch=1, grid=(cdim // bc,),
            in_specs=[pl.BlockSpec((1, rows, bc), lambda j, s_ref: (s_ref[0], 0, j)),
                      pl.BlockSpec((3, rows, bc), lambda j, s_ref: (0, 0, j)), spec, spec, spec, ANY],
            out_specs=[spec] * 4),
        out_shape=[shape] * 4,
        compiler_params=_params(("arbitrary",)),
    )(slot, chip, got, w, m, v, after)


WEIGHTS = ("ffn1_norm_g", "ffn1_w_gate", "ffn1_w_up", "ffn1_w_down", "mix_norm_g", "w_in", "gmlp_v_norm_g",
           "gmlp_w_s", "gmlp_b_s", "mla_q_norm_g", "mla_w_q_up", "mla_kv_norm_g", "mla_w_kv_up", "mla_q_head_g",
           "mla_k_head_g", "gmlp_out_g", "mla_out_g", "w_out", "ffn2_norm_g", "ffn2_w_gate", "ffn2_w_up",
           "ffn2_w_down")
SHARDED = {"ffn1_w_gate": True, "ffn1_w_up": True, "ffn1_w_down": False, "w_in": True, "mla_w_q_up": True,
           "mla_w_kv_up": True, "w_out": False, "ffn2_w_gate": True, "ffn2_w_up": True, "ffn2_w_down": False}


def _col_block(m, target):
    best = LANE
    for cand in range(LANE, min(m, target) + 1, LANE):
        if m % cand == 0:
            best = cand
    return best


def _shard_rows(w, transposed, pad_to=None):
    rows = (w[0].T if transposed else w[0]).astype(BF16)
    if pad_to is not None and pad_to != rows.shape[0]:
        rows = jnp.pad(rows, ((0, pad_to - rows.shape[0]), (0, 0)))
    return rows


def _pack(parts):
    flat = []
    for p in parts:
        f = p.reshape(-1).astype(F32)
        flat.append(jnp.pad(f, (0, _round_up(f.size, LANE) - f.size)))
    flat = jnp.concatenate(flat)
    rows = _round_up(flat.size // LANE, SUBLANE)
    return jnp.pad(flat, (0, rows * LANE - flat.size)).reshape(rows, LANE)


def _unpack(packed, shapes):
    out, row = [], 0
    for shp in shapes:
        size = 1
        for s in shp:
            size *= s
        nrows = _round_up(size, LANE) // LANE
        out.append(packed[row:row + nrows].reshape(-1)[:size].reshape(shp))
        row += nrows
    return out


def kernel(x, positions, ffn1_norm_g, ffn1_w_gate, ffn1_w_up, ffn1_w_down, mix_norm_g, w_in, gmlp_v_norm_g, gmlp_w_s, gmlp_b_s, mla_q_norm_g, mla_w_q_up, mla_kv_norm_g, mla_w_kv_up, mla_q_head_g, mla_k_head_g, gmlp_out_g, mla_out_g, w_out, ffn2_norm_g, ffn2_w_gate, ffn2_w_up, ffn2_w_down, loss_target, m_ffn1_norm_g, m_ffn1_w_gate, m_ffn1_w_up, m_ffn1_w_down, m_mix_norm_g, m_w_in, m_gmlp_v_norm_g, m_gmlp_w_s, m_gmlp_b_s, m_mla_q_norm_g, m_mla_w_q_up, m_mla_kv_norm_g, m_mla_w_kv_up, m_mla_q_head_g, m_mla_k_head_g, m_gmlp_out_g, m_mla_out_g, m_w_out, m_ffn2_norm_g, m_ffn2_w_gate, m_ffn2_w_up, m_ffn2_w_down, v_ffn1_norm_g, v_ffn1_w_gate, v_ffn1_w_up, v_ffn1_w_down, v_mix_norm_g, v_w_in, v_gmlp_v_norm_g, v_gmlp_w_s, v_gmlp_b_s, v_mla_q_norm_g, v_mla_w_q_up, v_mla_kv_norm_g, v_mla_w_kv_up, v_mla_q_head_g, v_mla_k_head_g, v_gmlp_out_g, v_mla_out_g, v_w_out, v_ffn2_norm_g, v_ffn2_w_gate, v_ffn2_w_up, v_ffn2_w_down):
    wts = dict(zip(WEIGHTS, (ffn1_norm_g, ffn1_w_gate, ffn1_w_up, ffn1_w_down, mix_norm_g, w_in, gmlp_v_norm_g, gmlp_w_s, gmlp_b_s, mla_q_norm_g, mla_w_q_up, mla_kv_norm_g, mla_w_kv_up, mla_q_head_g, mla_k_head_g, gmlp_out_g, mla_out_g, w_out, ffn2_norm_g, ffn2_w_gate, ffn2_w_up, ffn2_w_down)))
    mom1 = dict(zip(WEIGHTS, (m_ffn1_norm_g, m_ffn1_w_gate, m_ffn1_w_up, m_ffn1_w_down, m_mix_norm_g, m_w_in, m_gmlp_v_norm_g, m_gmlp_w_s, m_gmlp_b_s, m_mla_q_norm_g, m_mla_w_q_up, m_mla_kv_norm_g, m_mla_w_kv_up, m_mla_q_head_g, m_mla_k_head_g, m_gmlp_out_g, m_mla_out_g, m_w_out, m_ffn2_norm_g, m_ffn2_w_gate, m_ffn2_w_up, m_ffn2_w_down)))
    mom2 = dict(zip(WEIGHTS, (v_ffn1_norm_g, v_ffn1_w_gate, v_ffn1_w_up, v_ffn1_w_down, v_mix_norm_g, v_w_in, v_gmlp_v_norm_g, v_gmlp_w_s, v_gmlp_b_s, v_mla_q_norm_g, v_mla_w_q_up, v_mla_kv_norm_g, v_mla_w_kv_up, v_mla_q_head_g, v_mla_k_head_g, v_gmlp_out_g, v_mla_out_g, v_w_out, v_ffn2_norm_g, v_ffn2_w_gate, v_ffn2_w_up, v_ffn2_w_down)))

    b_loc, seq, d = x.shape
    t = b_loc * seq
    ffs = ffn1_w_gate.shape[2]
    fp = _round_up(ffs, LANE)
    wg = gmlp_v_norm_g.shape[1]
    groups = gmlp_w_s.shape[1]
    rq, rkv = mla_q_norm_g.shape[1], mla_kv_norm_g.shape[1]
    heads = mla_out_g.shape[1]
    assert w_in.shape[2] * N_DEV == 2 * wg + rq + rkv + ROPE and mla_w_kv_up.shape[2] * N_DEV == heads * HEADW
    tm = min(512, t)
    tm_mix = min(256, t)
    blk = min(256, seq)

    xf = x.reshape(t, d)
    target = loss_target.reshape(t, d)
    pos = positions.reshape(t, 1).astype(F32)
    half = ROPE // 2
    inv_freq = 1.0 / (ROPE_THETA ** (jnp.arange(half, dtype=F32) / half))
    freq = jnp.concatenate([inv_freq, inv_freq, jnp.zeros((LANE - ROPE,), F32)])[None, :]
    lane = jnp.arange(LANE)
    masks = jnp.stack([jnp.where(lane < half, -1.0, 0.0),
                       jnp.where((lane >= half) & (lane < ROPE), 1.0, 0.0)]).astype(F32)
    gqh = jnp.pad(mla_q_head_g, ((0, 0), (0, HEADW - QK)))
    gkh = jnp.pad(mla_k_head_g, ((0, 0), (0, HEADW - QK)))
    bias = jnp.repeat(gmlp_b_s[0].T, CHUNK, axis=1)
    gouta = gmlp_out_g.reshape(1, wg)
    goutb = mla_out_g.reshape(1, heads * VHEAD)
    ws = gmlp_w_s[0]

    px, py, pc = _place()
    me = 4 * px + 2 * py + pc
    core = pc.astype(jnp.int32).reshape(1)
    slot = (2 * px + py).astype(jnp.int32).reshape(1)
    order = [n for n in WEIGHTS if n in SHARDED]
    group = {"ffn1": [n for n in order if n.startswith("ffn1")], "ffn2": [n for n in order if n.startswith("ffn2")],
             "mix": [n for n in order if not n.startswith("ffn")]}
    shard = {n: _shard_rows(wts[n], SHARDED[n]) for n in group["ffn1"]}
    frows = ffs if ffs != fp else None

    def tied(arr, token):
        return arr + token[0, 0].astype(arr.dtype)

    xnb, ynb, dgn = 4 * (1 - px) + 2 * py, 4 * px + 2 * (1 - py), 4 * (1 - px) + 2 * (1 - py)
    ids_a = jnp.stack([me, 4 * px + 2 * py + (1 - pc)]).astype(jnp.int32)
    ids_b = jnp.stack([xnb, xnb + 1, ynb, ynb + 1]).astype(jnp.int32)
    ids_c = jnp.stack([dgn, dgn + 1]).astype(jnp.int32)
    g1 = _StagedGather([shard[n] for n in group["ffn1"]], me, "gather_ffn1", pad_to=fp)
    token = g1.start("own")
    token = g1.start("nbr", deps=(token,))
    for n in group["mix"] + group["ffn2"]:
        shard[n] = _shard_rows(tied(wts[n], token), SHARDED[n])
    g3 = _StagedGather([shard[n] for n in group["ffn2"]], me, "gather_ffn2", pad_to=fp)
    g1.wait("own", token)
    x1, xn1, kept1 = _ffn_fwd(xf, None, ffn1_norm_g, ids_a, *g1.lands(), None, tm, "ffn1_fwd_a")
    rope_cos, rope_sin = _rope_tables(pos, freq, tm, x1)
    g1.wait("nbr", rope_cos)
    token = g1.start("diag")
    ici2 = _gather_ici([shard[n] for n in group["mix"]], me, "gather_mix_ici", deps=(token,))
    token = g3.start("nbr", deps=(ici2.token,))
    token = g1.start("nbr_d2d", deps=(token,))
    g1.wait("nbr_d2d", token)
    x1, xn1, kept1 = _ffn_fwd(x1, xn1, None, ids_b, *g1.lands(), kept1, tm, "ffn1_fwd_b")
    g1.wait("diag", x1)
    token = g1.start("diag_d2d")
    d2d2 = _gather_d2d(ici2.wait(x1)[len(group["mix"]):], "gather_mix_d2d", deps=(token,))
    g1.wait("diag_d2d", d2d2.token)
    full = dict(zip(group["ffn1"], g1.lands()))
    x1, xn1, (gd1, sl1, h1) = _ffn_fwd(x1, xn1, None, ids_c, full["ffn1_w_gate"], full["ffn1_w_up"],
                                       full["ffn1_w_down"], kept1, tm, "ffn1_fwd_c")
    full.update(zip(group["mix"], d2d2.wait(x1)))
    win_t = full["w_in"].reshape(-1, d)
    splits = (2 * wg, rq, rkv, LANE)
    wq_t = jnp.pad(full["mla_w_q_up"].reshape(heads, QK, rq), ((0, 0), (0, HEADW - QK), (0, 0)))
    wkv_t = full["mla_w_kv_up"].reshape(heads, HEADW, rkv)
    wout = full["w_out"].reshape(-1, d)
    hn, zuv, cq, ckv, krw = _inproj_fwd(x1, mix_norm_g, win_t, splits, tm)
    ya = _gmlp_fwd(zuv, gmlp_v_norm_g, ws, bias, gouta, tm_mix)
    g3.wait("nbr", ya)
    token = g3.start("diag")
    token = g3.start("own_nbr_d2d", deps=(token,))
    q, k, vv = _mla_proj_fwd(cq, ckv, krw, rope_cos, rope_sin, masks, mla_q_norm_g, mla_kv_norm_g, wq_t, wkv_t,
                             tied(gqh, token), gkh, tm)
    o, lse = _attn_fwd(q, k, vv, seq, blk)
    g3.wait("diag", o)
    token = g3.start("diag_d2d")
    x2, ycat = _out_fwd(ya, o, tied(goutb, token), wout, x1, tm)
    g3.wait("own_nbr_d2d", x2)
    g3.wait("diag_d2d", x2)
    full.update(zip(group["ffn2"], g3.lands()))
    x3, xn2, (gd2, sl2, h2) = _ffn_fwd(x2, None, ffn2_norm_g, jnp.arange(N_DEV, dtype=jnp.int32),
                                       full["ffn2_w_gate"], full["ffn2_w_up"], full["ffn2_w_down"], None, tm,
                                       "ffn2_fwd")
    dx3, dy3_half, loss_part = _loss_head(x3, target, tm)

    outs_g, outs_d, outs_m, outs_v = {}, {}, {}, {}

    def finish(names, chip, got, after):
        for n, cp, gt in zip(names, chip, got):
            rows_of = (lambda a: a[0].T) if SHARDED[n] else (lambda a: a[0])
            res = _sum_adamw(cp, gt, slot, rows_of(wts[n]), rows_of(mom1[n]), rows_of(mom2[n]), "adamw_" + n, after)
            outs_g[n], outs_d[n], outs_m[n], outs_v[n] = [r.T[None] if SHARDED[n] else r[None] for r in res]
            after = res[3]
        return after

    def chip_sums(names, ex, after, rows=None):
        res = ex.wait(after)
        return [_pair_add(f, gt, core, "pair_add_" + n, rows)
                for n, f, gt in zip(names, res[:len(names)], res[len(names):])]

    tk = min(2048, t)
    grads = {}
    small = {}
    dx2, small["ffn2_norm_g"], da2, db2 = _ffn_bwd(
        dx3, x2, ffn2_norm_g, gd2, sl2, full["ffn2_w_gate"], full["ffn2_w_up"], full["ffn2_w_down"], tm, "ffn2_bwd")
    grads["ffn2_w_gate"] = _matmul_tn_resident(da2, xn2, fp, BF16, "dw_ffn2_gate").reshape(N_DEV, fp, d)
    grads["ffn2_w_up"] = _matmul_tn_resident(db2, xn2, fp, BF16, "dw_ffn2_up").reshape(N_DEV, fp, d)
    grads["ffn2_w_down"] = _matmul_tn_resident(h2, dy3_half, fp, BF16, "dw_ffn2_down").reshape(N_DEV, fp, d)
    red_a2 = _reduce_d2d([grads[n] for n in group["ffn2"]], "reduce_ffn2_d2d", rows=frows)
    dya, do, delta, small["mla_out_g"] = _out_bwd(dx2, o, tied(goutb, red_a2.token), wout, wg, tm)
    grads["w_out"] = _matmul_tn(ycat, dx2, _col_block(ycat.shape[1], 768), d, tk, BF16, "dw_out").reshape(
        N_DEV, -1, d)
    chip2 = chip_sums(group["ffn2"], red_a2, dya, frows)
    red_b2 = _reduce_ici(chip2, "reduce_ffn2_ici", rows=frows)
    dq, dk, dv = _attn_bwd(q, k, vv, do, lse, delta, seq, blk, red_b2.token)
    (dcq, dckv, dkrw, dwq, dwkv, small["mla_q_norm_g"], small["mla_kv_norm_g"], dgqh, dgkh) = _mla_proj_bwd(
        dq, dk, dv, cq, ckv, krw, rope_cos, rope_sin, masks, mla_q_norm_g, mla_kv_norm_g, wq_t, wkv_t, gqh, gkh,
        tm)
    small["mla_q_head_g"], small["mla_k_head_g"] = dgqh[:, :QK], dgkh[:, :QK]
    grads["mla_w_q_up"] = dwq[:, :QK].astype(BF16).reshape(N_DEV, -1, rq)
    grads["mla_w_kv_up"] = dwkv.astype(BF16).reshape(N_DEV, -1, rkv)
    dzuv, small["gmlp_w_s"], dbs, small["gmlp_v_norm_g"], small["gmlp_out_g"] = _gmlp_bwd(
        dya, zuv, gmlp_v_norm_g, ws, bias, gouta, tm_mix)
    small["gmlp_b_s"] = dbs[:, :, 0]
    dx1, small["mix_norm_g"], dzc, dy1_half = _inproj_bwd([dzuv, dcq, dckv, dkrw], x1, mix_norm_g, win_t, dx2,
                                                          splits, tm_mix)
    grads["w_in"] = _matmul_tn_resident(dzc, hn, _col_block(dzc.shape[1], 768), BF16, "dw_in",
                                        out_rows=win_t.shape[0]).reshape(N_DEV, -1, d)
    res_b2 = red_b2.wait(grads["w_in"])
    red_am = _reduce_d2d([grads[n] for n in group["mix"]], "reduce_mix_d2d")

    def ffn1_dw(n, lhs, rhs, token):
        return _matmul_tn_resident(lhs, rhs, fp, BF16, "dw_" + n, deps=(token,)).reshape(N_DEV, fp, d)

    gr = ffn1_dw("ffn1_w_down", h1, dy1_half, red_am.token)
    red_ad = _reduce_d2d([gr], "reduce_ffn1_w_down_d2d", deps=(red_am.token,), rows=frows)
    dx0, small["ffn1_norm_g"], da1, db1 = _ffn_bwd(
        dx1, xf, tied(ffn1_norm_g, red_ad.token), gd1, sl1, full["ffn1_w_gate"], full["ffn1_w_up"],
        full["ffn1_w_down"], tm, "ffn1_bwd")
    chipm = chip_sums(group["mix"], red_am, dx0)
    red_bm = _reduce_ici(chipm, "reduce_mix_ici")
    red_b = [("ffn1_w_down", _reduce_ici(chip_sums(["ffn1_w_down"], red_ad, dx0, frows), "reduce_ffn1_w_down_ici",
                                         deps=(red_bm.token,), rows=frows))]
    rep = [n for n in WEIGHTS if n not in SHARDED]
    small_ici = _gather_ici([_pack([small[n] for n in rep] + [loss_part])], me, "gather_small_ici",
                            deps=(red_b[-1][1].token,))
    gr = ffn1_dw("ffn1_w_gate", da1, xn1, small_ici.token)
    red_ag = _reduce_d2d([gr], "reduce_ffn1_w_gate_d2d", rows=frows)
    n2 = group["ffn2"]
    after = finish(n2[0:1], res_b2[0:1], res_b2[3:4], red_ag.token)
    red_b.append(("ffn1_w_gate", _reduce_ici(chip_sums(["ffn1_w_gate"], red_ag, after, frows),
                                             "reduce_ffn1_w_gate_ici", rows=frows)))
    gr = ffn1_dw("ffn1_w_up", db1, xn1, red_b[-1][1].token)
    small_d2d = _gather_d2d(small_ici.wait(gr)[1:], "gather_small_d2d")
    red_au = _reduce_d2d([gr], "reduce_ffn1_w_up_d2d", deps=(small_d2d.token,), rows=frows)
    after = finish(n2[1:2], res_b2[1:2], res_b2[4:5], red_au.token)
    red_b.append(("ffn1_w_up", _reduce_ici(chip_sums(["ffn1_w_up"], red_au, after, frows),
                                           "reduce_ffn1_w_up_ici", rows=frows)))
    after = finish(n2[2:3], res_b2[2:3], res_b2[5:6], red_b[-1][1].token)
    res = red_bm.wait(after)
    nm_ = len(group["mix"])
    after = finish(group["mix"], res[:nm_], res[nm_:], after)
    total = _sum_devices(small_d2d.wait(after)[0])
    zero = jnp.zeros((1,), F32)
    dlt, nm, nv = _adamw(_pack([wts[n] for n in rep] + [zero]), total, _pack([mom1[n] for n in rep] + [zero]),
                         _pack([mom2[n] for n in rep] + [zero]), "adamw_small")
    shapes = [wts[n].shape for n in rep] + [(1,)]
    for n, g, dl, m1, m2 in zip(rep, _unpack(total, shapes), _unpack(dlt, shapes), _unpack(nm, shapes),
                                _unpack(nv, shapes)):
        outs_g[n], outs_d[n], outs_m[n], outs_v[n] = g, dl, m1, m2
    loss = _unpack(total, shapes)[-1].reshape(())
    after = dlt
    for n, ex in red_b:
        res = ex.wait(after)
        after = finish([n], res[:1], res[1:], after)

    return (loss, dx0.reshape(b_loc, seq, d), *[outs_g[n] for n in WEIGHTS], *[outs_d[n] for n in WEIGHTS],
            *[outs_m[n] for n in WEIGHTS], *[outs_v[n] for n in WEIGHTS])
```

```python
import functools

import jax
import jax.numpy as jnp
from jax import lax
from jax.experimental import pallas as pl
from jax.experimental.pallas import tpu as pltpu

F32 = jnp.float32
BF16 = jnp.bfloat16
EPS = 1e-6
LANE = 128
SUBLANE = 8
N_DEV = 8
VMEM_LIMIT = 60 * 1024 * 1024
NOPE = 128
ROPE = 64
VHEAD = 128
QK = NOPE + ROPE
HEADW = 2 * LANE
CHUNK = 128
ROPE_THETA = 10000.0
ADAM_LR, ADAM_B1, ADAM_B2, ADAM_EPS, ADAM_WD, ADAM_STEP = 0.001, 0.9, 0.999, 1e-08, 0.01, 10
MESH = pl.DeviceIdType.MESH
ANY = pl.BlockSpec(memory_space=pl.ANY)
WHOLE_VMEM = pl.BlockSpec(memory_space=pltpu.VMEM)


def _params(sem=None):
    return pltpu.CompilerParams(dimension_semantics=sem, vmem_limit_bytes=VMEM_LIMIT)


def _round_up(n, m):
    return -(-n // m) * m


def _row_block(rows, target):
    best = rows
    for cand in range(SUBLANE, min(rows, target) + 1, SUBLANE):
        if rows % cand == 0:
            best = cand
    return best if best <= target else rows


def _nn(a, b):
    return jnp.dot(a, b, preferred_element_type=F32)


def _nt(a, b):
    return lax.dot_general(a, b, (((1,), (1,)), ((), ())), preferred_element_type=F32)


def _tn(a, b):
    return lax.dot_general(a, b, (((0,), (0,)), ((), ())), preferred_element_type=F32)


def _rstd(x, n):
    return lax.rsqrt(jnp.sum(x * x, axis=-1, keepdims=True) * (1.0 / n) + EPS)


def _rms_fwd(x, g, n):
    return x * _rstd(x, n) * g


def _rms_bwd(x, g, dy, n):
    r = _rstd(x, n)
    xh = x * r
    dyg = dy * g
    dx = r * (dyg - xh * (jnp.sum(dyg * xh, axis=-1, keepdims=True) * (1.0 / n)))
    return dx, jnp.sum(dy * xh, axis=0, keepdims=True)


def _gelu(x):
    return 0.5 * x * (1.0 + lax.erf(x * 0.7071067811865476))


def _gelu_grad(x):
    return 0.5 * (1.0 + lax.erf(x * 0.7071067811865476)) + x * jnp.exp(-0.5 * x * x) * 0.3989422804014327


def _ffn_fwd(base, xn, g, ids, wg_t, wu_t, wd, saved, tm, name):
    t, d = base.shape
    nb, fp, _ = wg_t.shape
    n = ids.shape[0]
    first = xn is None
    if saved is None:
        saved = [lax.empty((t, nb * fp), BF16) for _ in range(3)]

    def body(ids_ref, *refs):
        if first:
            base_ref, g_ref, wg_ref, wu_ref, wd_ref, _, _, _, out_ref, xn_ref, gd_ref, sl_ref, h_ref, acc = refs
        else:
            base_ref, xn_ref, wg_ref, wu_ref, wd_ref, _, _, _, out_ref, gd_ref, sl_ref, h_ref, acc = refs
        j = pl.program_id(1)

        @pl.when(j == 0)
        def _():
            if first:
                xn_ref[...] = _rms_fwd(base_ref[...], g_ref[...], d).astype(BF16)
            acc[...] = jnp.zeros_like(acc)

        xnb = xn_ref[...]
        a = _nt(xnb, wg_ref[0])
        b = _nt(xnb, wu_ref[0])
        s = jax.nn.sigmoid(a)
        sl = a * s
        h = (sl * b).astype(BF16)
        gd_ref[...] = (b * (s * (1.0 + a * (1.0 - s)))).astype(BF16)
        sl_ref[...] = sl.astype(BF16)
        h_ref[...] = h
        acc[...] += _nn(h, wd_ref[0])

        @pl.when(j == n - 1)
        def _():
            out_ref[...] = base_ref[...] + 0.5 * acc[...]

    wspec = pl.BlockSpec((1, fp, d), lambda i, j, ids_ref: (ids_ref[j], 0, 0))
    row = pl.BlockSpec((tm, d), lambda i, j, ids_ref: (i, 0))
    ff = pl.BlockSpec((tm, fp), lambda i, j, ids_ref: (i, ids_ref[j]))
    ffs = jax.ShapeDtypeStruct((t, nb * fp), BF16)
    second = pl.BlockSpec((1, d), lambda i, j, ids_ref: (0, 0)) if first else row
    n_row_outs = 2 if first else 1
    res = pl.pallas_call(
        body, name=name,
        grid_spec=pltpu.PrefetchScalarGridSpec(
            num_scalar_prefetch=1, grid=(t // tm, n),
            in_specs=[row, second, wspec, wspec, wspec, ANY, ANY, ANY],
            out_specs=[row] * n_row_outs + [ff, ff, ff],
            scratch_shapes=[pltpu.VMEM((tm, d), F32)]),
        out_shape=[jax.ShapeDtypeStruct((t, d), F32)] + ([jax.ShapeDtypeStruct((t, d), BF16)] if first else [])
        + [ffs, ffs, ffs],
        input_output_aliases={6 + k: n_row_outs + k for k in range(3)},
        compiler_params=_params(("arbitrary", "arbitrary")),
    )(ids, base, g if first else xn, wg_t, wu_t, wd, *saved)
    return (res[0], res[1] if first else xn, list(res[n_row_outs:]))


def _ffn_bwd(dout, x, g, gd, sl, wg_t, wu_t, wd, tm, name):
    t, d = x.shape
    nb, fp, _ = wg_t.shape

    def body(do_hbm, x_hbm, g_ref, gd_ref, sl_ref, wg_ref, wu_ref, wd_ref, wd_next_ref,
             dx_hbm, dg_ref, da_ref, db_ref, acc, rowbuf, dy_scr, dh_scr, sem):
        i, j = pl.program_id(0), pl.program_id(1)
        rows = pl.ds(pl.multiple_of(i * tm, tm), tm)
        get_do = pltpu.make_async_copy(do_hbm.at[rows, :], rowbuf, sem)
        get_x = pltpu.make_async_copy(x_hbm.at[rows, :], rowbuf, sem)

        @pl.when(j == 0)
        def _():
            get_do.start()
            get_do.wait()
            dy_scr[...] = (0.5 * rowbuf[...]).astype(BF16)
            acc[...] = jnp.zeros_like(acc)
            dh_scr[0] = _nt(dy_scr[...], wd_ref[0])
            get_x.start()

        @pl.when((i == 0) & (j == 0))
        def _():
            dg_ref[...] = jnp.zeros_like(dg_ref)

        dh = dh_scr[j % 2]
        dh_scr[(j + 1) % 2] = _nt(dy_scr[...], wd_next_ref[0])
        da = (dh * gd_ref[...].astype(F32)).astype(BF16)
        db = (dh * sl_ref[...].astype(F32)).astype(BF16)
        da_ref[...] = da
        db_ref[...] = db
        acc[...] += _nn(da, wg_ref[0]) + _nn(db, wu_ref[0])

        @pl.when(j == nb - 1)
        def _():
            get_x.wait()
            dxn, dg = _rms_bwd(rowbuf[...], g_ref[...], acc[...], d)
            dg_ref[...] += dg
            acc[...] = dxn
            get_do.start()
            get_do.wait()
            acc[...] += rowbuf[...]
            out = pltpu.make_async_copy(acc, dx_hbm.at[rows, :], sem)
            out.start()
            out.wait()

    wspec = pl.BlockSpec((1, fp, d), lambda i, j: (j, 0, 0))
    wnext = pl.BlockSpec((1, fp, d), lambda i, j: (jnp.minimum(j + 1, nb - 1), 0, 0))
    vec = pl.BlockSpec((1, d), lambda i, j: (0, 0))
    ff = pl.BlockSpec((tm, fp), lambda i, j: (i, j))
    ffs = jax.ShapeDtypeStruct((t, nb * fp), BF16)
    return pl.pallas_call(
        body, name=name, grid=(t // tm, nb),
        in_specs=[ANY, ANY, vec, ff, ff, wspec, wspec, pl.BlockSpec((1, fp, d), lambda i, j: (0, 0, 0)), wnext],
        out_specs=[ANY, vec, ff, ff],
        out_shape=[jax.ShapeDtypeStruct((t, d), F32), jax.ShapeDtypeStruct((1, d), F32), ffs, ffs],
        scratch_shapes=[pltpu.VMEM((tm, d), F32), pltpu.VMEM((tm, d), F32), pltpu.VMEM((tm, d), BF16),
                        pltpu.VMEM((2, tm, fp), F32), pltpu.SemaphoreType.DMA],
        compiler_params=_params(("arbitrary", "arbitrary")),
    )(dout, x, g, gd, sl, wg_t, wu_t, wd, wd)


def _matmul_tn(lhs, rhs, bm, bn, tk, out_dtype, name, deps=(), out_rows=None):
    t, m = lhs.shape
    n = rhs.shape[1]
    nk = t // tk
    out_rows = m if out_rows is None else out_rows

    def body(l_ref, r_ref, *refs):
        o_ref, acc = refs[len(deps):]
        k = pl.program_id(2)

        @pl.when(k == 0)
        def _():
            acc[...] = jnp.zeros_like(acc)

        acc[...] += _tn(l_ref[...].astype(BF16), r_ref[...].astype(BF16))

        @pl.when(k == nk - 1)
        def _():
            o_ref[...] = acc[...].astype(out_dtype)

    return pl.pallas_call(
        body, name=name, grid=(m // bm, n // bn, nk),
        in_specs=[pl.BlockSpec((tk, bm), lambda i, j, k: (k, i)), pl.BlockSpec((tk, bn), lambda i, j, k: (k, j))]
        + [ANY] * len(deps),
        out_specs=pl.BlockSpec((bm, bn), lambda i, j, k: (i, j)),
        out_shape=jax.ShapeDtypeStruct((out_rows, n), out_dtype),
        scratch_shapes=[pltpu.VMEM((bm, bn), F32)],
        compiler_params=_params(("arbitrary", "arbitrary", "arbitrary")),
    )(lhs, rhs, *deps)


def _matmul_tn_resident(lhs, rhs, bm, out_dtype, name, deps=(), out_rows=None):
    t, m = lhs.shape
    n = rhs.shape[1]
    out_rows = m if out_rows is None else out_rows

    def body(l_ref, r_ref, *refs):
        refs[len(deps)][...] = _tn(l_ref[...], r_ref[...]).astype(out_dtype)

    return pl.pallas_call(
        body, name=name, grid=(m // bm,),
        in_specs=[pl.BlockSpec((t, bm), lambda i: (0, i)), WHOLE_VMEM] + [ANY] * len(deps),
        out_specs=pl.BlockSpec((bm, n), lambda i: (i, 0)),
        out_shape=jax.ShapeDtypeStruct((out_rows, n), out_dtype),
        compiler_params=_params(("arbitrary",)),
    )(lhs, rhs, *deps)


def _matmul_tn_blocks(lhs, rhs, bm, ids, prev, name, deps=()):
    t, m = lhs.shape
    n = rhs.shape[1]
    if prev is None:
        prev = lax.empty((m, n), BF16)

    def body(ids_ref, l_ref, r_ref, *refs):
        refs[1 + len(deps)][...] = _tn(l_ref[...], r_ref[...]).astype(BF16)

    return pl.pallas_call(
        body, name=name,
        grid_spec=pltpu.PrefetchScalarGridSpec(
            num_scalar_prefetch=1, grid=(ids.shape[0],),
            in_specs=[pl.BlockSpec((t, bm), lambda i, ids_ref: (0, ids_ref[i])), WHOLE_VMEM, ANY]
            + [ANY] * len(deps),
            out_specs=pl.BlockSpec((bm, n), lambda i, ids_ref: (ids_ref[i], 0))),
        out_shape=jax.ShapeDtypeStruct((m, n), BF16),
        input_output_aliases={3: 0},
        compiler_params=_params(("arbitrary",)),
    )(ids, lhs, rhs, prev, *deps)


def _last_rows_padded(w_ref, tail_ref, off, real):
    @pl.when(pl.program_id(0) == 0)
    def _():
        tail_ref[...] = jnp.zeros_like(tail_ref)
        tail_ref[0:real, :] = w_ref[off:off + real, :]


def _inproj_fwd(x, g, w_t, splits, tm):
    t, d = x.shape
    offs = [sum(splits[:k]) for k in range(len(splits))]
    real_last = w_t.shape[0] - offs[-1]

    def body(x_ref, g_ref, w_ref, hn_ref, *refs):
        z_refs, tail_ref = refs[:-1], refs[-1]
        _last_rows_padded(w_ref, tail_ref, offs[-1], real_last)
        hn = _rms_fwd(x_ref[...], g_ref[...], d).astype(BF16)
        hn_ref[...] = hn
        for z_ref, o, n in zip(z_refs[:-1], offs, splits):
            z_ref[...] = _nt(hn, w_ref[o:o + n, :])
        z_refs[-1][...] = _nt(hn, tail_ref[...])

    row = pl.BlockSpec((tm, d), lambda i: (i, 0))
    return pl.pallas_call(
        body, name="inproj_fwd", grid=(t // tm,),
        in_specs=[row, pl.BlockSpec((1, d), lambda i: (0, 0)), WHOLE_VMEM],
        out_specs=[row] + [pl.BlockSpec((tm, n), lambda i: (i, 0)) for n in splits],
        out_shape=[jax.ShapeDtypeStruct((t, d), BF16)] + [jax.ShapeDtypeStruct((t, n), F32) for n in splits],
        scratch_shapes=[pltpu.VMEM((splits[-1], d), BF16)],
        compiler_params=_params(("arbitrary",)),
    )(x, g, w_t)


def _inproj_bwd(dzs, x, g, w_t, dres, splits, tm):
    t, d = x.shape
    offs = [sum(splits[:k]) for k in range(len(splits))]
    ni = sum(splits)
    nz = len(splits)
    real_last = w_t.shape[0] - offs[-1]

    def body(*refs):
        dz_refs = refs[:nz]
        x_ref, g_ref, w_ref, dres_ref, dx_ref, dg_ref, dzc_ref, half_ref, tail_ref = refs[nz:]
        _last_rows_padded(w_ref, tail_ref, offs[-1], real_last)
        dhn = jnp.zeros((tm, d), F32)
        for k, (dz_ref, o, n) in enumerate(zip(dz_refs, offs, splits)):
            dz = dz_ref[...].astype(BF16)
            dzc_ref[:, o:o + n] = dz
            dhn += _nn(dz, tail_ref[...] if k == nz - 1 else w_ref[o:o + n, :])
        dx, dg = _rms_bwd(x_ref[...], g_ref[...], dhn, d)
        dx = dres_ref[...] + dx
        dx_ref[...] = dx
        half_ref[...] = (0.5 * dx).astype(BF16)

        @pl.when(pl.program_id(0) == 0)
        def _():
            dg_ref[...] = jnp.zeros_like(dg_ref)

        dg_ref[...] += dg

    row = pl.BlockSpec((tm, d), lambda i: (i, 0))
    vec = pl.BlockSpec((1, d), lambda i: (0, 0))
    return pl.pallas_call(
        body, name="inproj_bwd", grid=(t // tm,),
        in_specs=[pl.BlockSpec((tm, n), lambda i: (i, 0)) for n in splits] + [row, vec, WHOLE_VMEM, row],
        out_specs=[row, vec, pl.BlockSpec((tm, ni), lambda i: (i, 0)), row],
        out_shape=[jax.ShapeDtypeStruct((t, d), F32), jax.ShapeDtypeStruct((1, d), F32),
                   jax.ShapeDtypeStruct((t, ni), BF16), jax.ShapeDtypeStruct((t, d), BF16)],
        scratch_shapes=[pltpu.VMEM((splits[-1], d), BF16)],
        compiler_params=_params(("arbitrary",)),
    )(*dzs, x, g, w_t, dres)


def _tril_bf16(ws_ref, grp):
    rows = lax.broadcasted_iota(jnp.int32, (CHUNK, CHUNK), 0)
    cols = lax.broadcasted_iota(jnp.int32, (CHUNK, CHUNK), 1)
    return jnp.where(rows >= cols, ws_ref[grp], 0.0).astype(BF16)


def _gmlp_mix(zuv_ref, gv_ref, ws_ref, bias_ref, v_scr, mixed_scr, tm, w, groups):
    u = _gelu(zuv_ref[:, 0:w])
    v0 = _gelu(zuv_ref[:, w:2 * w])
    v_scr[...] = _rms_fwd(v0, gv_ref[...], w).astype(BF16)
    for grp in range(groups):
        wsm = _tril_bf16(ws_ref, grp)
        lanes = slice(grp * CHUNK, (grp + 1) * CHUNK)
        for c in range(tm // CHUNK):
            rows = slice(c * CHUNK, (c + 1) * CHUNK)
            mixed_scr[rows, lanes] = _nn(wsm, v_scr[rows, lanes]) + bias_ref[:, lanes]
    return u, v0


def _gmlp_fwd(zuv, gv, ws, bias, gout, tm):
    t, w2 = zuv.shape
    w = w2 // 2
    groups = ws.shape[0]

    def body(zuv_ref, gv_ref, ws_ref, bias_ref, go_ref, y_ref, v_scr, mixed_scr):
        u, _ = _gmlp_mix(zuv_ref, gv_ref, ws_ref, bias_ref, v_scr, mixed_scr, tm, w, groups)
        ya = u * mixed_scr[...]
        for grp in range(groups):
            lanes = slice(grp * CHUNK, (grp + 1) * CHUNK)
            y_ref[:, lanes] = _rms_fwd(ya[:, lanes], go_ref[:, lanes], CHUNK).astype(BF16)

    const2 = lambda i: (0, 0)
    return pl.pallas_call(
        body, name="gmlp_fwd", grid=(t // tm,),
        in_specs=[pl.BlockSpec((tm, w2), lambda i: (i, 0)), pl.BlockSpec((1, w), const2),
                  pl.BlockSpec((groups, CHUNK, CHUNK), lambda i: (0, 0, 0)),
                  pl.BlockSpec((CHUNK, w), const2), pl.BlockSpec((1, w), const2)],
        out_specs=pl.BlockSpec((tm, w), lambda i: (i, 0)),
        out_shape=jax.ShapeDtypeStruct((t, w), BF16),
        scratch_shapes=[pltpu.VMEM((tm, w), BF16), pltpu.VMEM((tm, w), F32)],
        compiler_params=_params(("arbitrary",)),
    )(zuv, gv, ws, bias, gout)


def _gmlp_bwd(dy, zuv, gv, ws, bias, gout, tm):
    t, w2 = zuv.shape
    w = w2 // 2
    groups = ws.shape[0]

    def body(dy_ref, zuv_ref, gv_ref, ws_ref, bias_ref, go_ref,
             dz_ref, dws_ref, dbias_ref, dgv_ref, dgo_ref, v_scr, mixed_scr, dmix_scr, dv_scr):
        @pl.when(pl.program_id(0) == 0)
        def _():
            dws_ref[...] = jnp.zeros_like(dws_ref)
            dbias_ref[...] = jnp.zeros_like(dbias_ref)
            dgv_ref[...] = jnp.zeros_like(dgv_ref)
            dgo_ref[...] = jnp.zeros_like(dgo_ref)

        u, v0 = _gmlp_mix(zuv_ref, gv_ref, ws_ref, bias_ref, v_scr, mixed_scr, tm, w, groups)
        mixed = mixed_scr[...]
        ya = u * mixed
        for grp in range(groups):
            lanes = slice(grp * CHUNK, (grp + 1) * CHUNK)
            dya, dgo = _rms_bwd(ya[:, lanes], go_ref[:, lanes], dy_ref[:, lanes], CHUNK)
            dgo_ref[:, lanes] += dgo
            dz_ref[:, lanes] = dya * mixed[:, lanes] * _gelu_grad(zuv_ref[:, lanes])
            dmix_scr[:, lanes] = dya * u[:, lanes]
        for grp in range(groups):
            wsm = _tril_bf16(ws_ref, grp)
            lanes = slice(grp * CHUNK, (grp + 1) * CHUNK)
            dws = jnp.zeros((CHUNK, CHUNK), F32)
            dbias = jnp.zeros((CHUNK, CHUNK), F32)
            for c in range(tm // CHUNK):
                rows = slice(c * CHUNK, (c + 1) * CHUNK)
                dm = dmix_scr[rows, lanes]
                dmb = dm.astype(BF16)
                dv_scr[rows, lanes] = _tn(wsm, dmb)
                dws += _nt(dmb, v_scr[rows, lanes])
                dbias += dm
            rr = lax.broadcasted_iota(jnp.int32, (CHUNK, CHUNK), 0)
            cc = lax.broadcasted_iota(jnp.int32, (CHUNK, CHUNK), 1)
            dws_ref[grp] += jnp.where(rr >= cc, dws, 0.0)
            dbias_ref[grp] += jnp.sum(dbias, axis=1, keepdims=True)
        dv0, dgv = _rms_bwd(v0, gv_ref[...], dv_scr[...], w)
        dgv_ref[...] += dgv
        dz_ref[:, w:2 * w] = dv0 * _gelu_grad(zuv_ref[:, w:2 * w])

    const2 = lambda i: (0, 0)
    const3 = lambda i: (0, 0, 0)
    return pl.pallas_call(
        body, name="gmlp_bwd", grid=(t // tm,),
        in_specs=[pl.BlockSpec((tm, w), lambda i: (i, 0)), pl.BlockSpec((tm, w2), lambda i: (i, 0)),
                  pl.BlockSpec((1, w), const2), pl.BlockSpec((groups, CHUNK, CHUNK), const3),
                  pl.BlockSpec((CHUNK, w), const2), pl.BlockSpec((1, w), const2)],
        out_specs=[pl.BlockSpec((tm, w2), lambda i: (i, 0)), pl.BlockSpec((groups, CHUNK, CHUNK), const3),
                   pl.BlockSpec((groups, CHUNK, 1), const3), pl.BlockSpec((1, w), const2), pl.BlockSpec((1, w), const2)],
        out_shape=[jax.ShapeDtypeStruct((t, w2), F32), jax.ShapeDtypeStruct((groups, CHUNK, CHUNK), F32),
                   jax.ShapeDtypeStruct((groups, CHUNK, 1), F32), jax.ShapeDtypeStruct((1, w), F32),
                   jax.ShapeDtypeStruct((1, w), F32)],
        scratch_shapes=[pltpu.VMEM((tm, w), BF16), pltpu.VMEM((tm, w), F32),
                        pltpu.VMEM((tm, w), F32), pltpu.VMEM((tm, w), F32)],
        compiler_params=_params(("arbitrary",)),
    )(dy, zuv, gv, ws, bias, gout)


def _rot(x, m_lo, m_hi):
    return pltpu.roll(x, LANE - ROPE // 2, 1) * m_lo + pltpu.roll(x, ROPE // 2, 1) * m_hi


def _rope_tables(pos, freq, tm, after):
    t = pos.shape[0]

    def body(pos_ref, freq_ref, _, cos_ref, sin_ref):
        ang = pos_ref[...] * freq_ref[...]
        cos_ref[...] = jnp.cos(ang)
        sin_ref[...] = jnp.sin(ang)

    tab = pl.BlockSpec((tm, LANE), lambda i: (i, 0))
    return pl.pallas_call(
        body, name="rope_tables", grid=(t // tm,),
        in_specs=[pl.BlockSpec((tm, 1), lambda i: (i, 0)), pl.BlockSpec((1, LANE), lambda i: (0, 0)), ANY],
        out_specs=[tab, tab], out_shape=[jax.ShapeDtypeStruct((t, LANE), F32)] * 2,
        compiler_params=_params(("arbitrary",)),
    )(pos, freq, after)


def _mla_proj_fwd(cq, ckv, krw, cos, sin, masks, gq, gkv, wq_t, wkv_t, gqh, gkh, tm):
    t, rq = cq.shape
    rkv = ckv.shape[1]
    heads = wq_t.shape[0]

    def body(cq_ref, ckv_ref, kr_ref, cos_ref, sin_ref, mk_ref, gq_ref, gkv_ref, wq_ref, wkv_ref,
             gqh_ref, gkh_ref, q_ref, k_ref, v_ref):
        cos, sin = cos_ref[...], sin_ref[...]
        m_lo, m_hi = mk_ref[0:1, :], mk_ref[1:2, :]
        cqn = _rms_fwd(cq_ref[...], gq_ref[...], rq).astype(BF16)
        ckvn = _rms_fwd(ckv_ref[...], gkv_ref[...], rkv).astype(BF16)
        kr = kr_ref[...]
        kr_ss = jnp.sum(kr * kr, axis=-1, keepdims=True)
        for h in range(heads):
            qh = _nt(cqn, wq_ref[h])
            qn = qh * _rstd(qh, QK) * gqh_ref[...]
            qr = qn[:, LANE:]
            q_ref[h, :, 0:LANE] = qn[:, 0:LANE].astype(BF16)
            q_ref[h, :, LANE:] = (qr * cos + _rot(qr, m_lo, m_hi) * sin).astype(BF16)
            kvh = _nt(ckvn, wkv_ref[h])
            kn = kvh[:, 0:LANE]
            rk = lax.rsqrt((jnp.sum(kn * kn, axis=-1, keepdims=True) + kr_ss) * (1.0 / QK) + EPS)
            k_ref[h, :, 0:LANE] = (kn * rk * gkh_ref[:, 0:LANE]).astype(BF16)
            krn = kr * rk * gkh_ref[:, LANE:]
            k_ref[h, :, LANE:] = (krn * cos + _rot(krn, m_lo, m_hi) * sin).astype(BF16)
            v_ref[h] = kvh[:, LANE:].astype(BF16)

    c2 = lambda i: (0, 0)
    c3 = lambda i: (0, 0, 0)
    return pl.pallas_call(
        body, name="mla_proj_fwd", grid=(t // tm,),
        in_specs=[pl.BlockSpec((tm, rq), lambda i: (i, 0)), pl.BlockSpec((tm, rkv), lambda i: (i, 0)),
                  pl.BlockSpec((tm, LANE), lambda i: (i, 0)), pl.BlockSpec((tm, LANE), lambda i: (i, 0)),
                  pl.BlockSpec((tm, LANE), lambda i: (i, 0)), pl.BlockSpec((2, LANE), c2),
                  pl.BlockSpec((1, rq), c2), pl.BlockSpec((1, rkv), c2),
                  pl.BlockSpec((heads, HEADW, rq), c3), pl.BlockSpec((heads, HEADW, rkv), c3),
                  pl.BlockSpec((1, HEADW), c2), pl.BlockSpec((1, HEADW), c2)],
        out_specs=[pl.BlockSpec((heads, tm, HEADW), lambda i: (0, i, 0)),
                   pl.BlockSpec((heads, tm, HEADW), lambda i: (0, i, 0)),
                   pl.BlockSpec((heads, tm, VHEAD), lambda i: (0, i, 0))],
        out_shape=[jax.ShapeDtypeStruct((heads, t, HEADW), BF16), jax.ShapeDtypeStruct((heads, t, HEADW), BF16),
                   jax.ShapeDtypeStruct((heads, t, VHEAD), BF16)],
        compiler_params=_params(("arbitrary",)),
    )(cq, ckv, krw, cos, sin, masks, gq, gkv, wq_t, wkv_t, gqh, gkh)


def _mla_proj_bwd(dq, dk, dv, cq, ckv, krw, cos, sin, masks, gq, gkv, wq_t, wkv_t, gqh, gkh, tm):
    t, rq = cq.shape
    rkv = ckv.shape[1]
    heads = wq_t.shape[0]

    def body(dq_ref, dk_ref, dv_ref, cq_ref, ckv_ref, kr_ref, cos_ref, sin_ref, mk_ref, gq_ref, gkv_ref,
             wq_ref, wkv_ref, gqh_ref, gkh_ref,
             dcq_ref, dckv_ref, dkr_ref, dwq_ref, dwkv_ref, dgq_ref, dgkv_ref, dgqh_ref, dgkh_ref):
        @pl.when(pl.program_id(0) == 0)
        def _():
            for r in (dwq_ref, dwkv_ref, dgq_ref, dgkv_ref, dgqh_ref, dgkh_ref):
                r[...] = jnp.zeros_like(r)

        cos, sin = cos_ref[...], sin_ref[...]
        m_lo, m_hi = mk_ref[0:1, :], mk_ref[1:2, :]

        def unrope(dy):
            return dy * cos - _rot(dy * sin, m_lo, m_hi)

        cqn = _rms_fwd(cq_ref[...], gq_ref[...], rq).astype(BF16)
        ckvn = _rms_fwd(ckv_ref[...], gkv_ref[...], rkv).astype(BF16)
        kr = kr_ref[...]
        kr_ss = jnp.sum(kr * kr, axis=-1, keepdims=True)
        dcqn = jnp.zeros((tm, rq), F32)
        dckvn = jnp.zeros((tm, rkv), F32)
        dkr = jnp.zeros((tm, LANE), F32)
        for h in range(heads):
            qh = _nt(cqn, wq_ref[h])
            dqn = jnp.concatenate([dq_ref[h, :, 0:LANE], unrope(dq_ref[h, :, LANE:])], axis=1)
            dqh, dg = _rms_bwd(qh, gqh_ref[...], dqn, QK)
            dgqh_ref[...] += dg
            dqh = dqh.astype(BF16)
            dcqn += _nn(dqh, wq_ref[h])
            dwq_ref[h] += _tn(dqh, cqn)

            kvh = _nt(ckvn, wkv_ref[h])
            kn = kvh[:, 0:LANE]
            rk = lax.rsqrt((jnp.sum(kn * kn, axis=-1, keepdims=True) + kr_ss) * (1.0 / QK) + EPS)
            dkn_n = dk_ref[h, :, 0:LANE]
            dkr_n = unrope(dk_ref[h, :, LANE:])
            knh, krh = kn * rk, kr * rk
            dgkh_ref[:, 0:LANE] += jnp.sum(dkn_n * knh, axis=0, keepdims=True)
            dgkh_ref[:, LANE:] += jnp.sum(dkr_n * krh, axis=0, keepdims=True)
            dkn_g, dkr_g = dkn_n * gkh_ref[:, 0:LANE], dkr_n * gkh_ref[:, LANE:]
            proj = (jnp.sum(dkn_g * knh, axis=-1, keepdims=True)
                    + jnp.sum(dkr_g * krh, axis=-1, keepdims=True)) * (1.0 / QK)
            dkr += rk * (dkr_g - krh * proj)
            dkvh = jnp.concatenate([rk * (dkn_g - knh * proj), dv_ref[h]], axis=1).astype(BF16)
            dckvn += _nn(dkvh, wkv_ref[h])
            dwkv_ref[h] += _tn(dkvh, ckvn)
        dkr_ref[...] = dkr
        dcq, dg = _rms_bwd(cq_ref[...], gq_ref[...], dcqn, rq)
        dcq_ref[...] = dcq
        dgq_ref[...] += dg
        dckv, dg = _rms_bwd(ckv_ref[...], gkv_ref[...], dckvn, rkv)
        dckv_ref[...] = dckv
        dgkv_ref[...] += dg

    c2 = lambda i: (0, 0)
    c3 = lambda i: (0, 0, 0)
    hq = pl.BlockSpec((heads, tm, HEADW), lambda i: (0, i, 0))
    return pl.pallas_call(
        body, name="mla_proj_bwd", grid=(t // tm,),
        in_specs=[hq, hq, pl.BlockSpec((heads, tm, VHEAD), lambda i: (0, i, 0)),
                  pl.BlockSpec((tm, rq), lambda i: (i, 0)), pl.BlockSpec((tm, rkv), lambda i: (i, 0)),
                  pl.BlockSpec((tm, LANE), lambda i: (i, 0)), pl.BlockSpec((tm, LANE), lambda i: (i, 0)),
                  pl.BlockSpec((tm, LANE), lambda i: (i, 0)), pl.BlockSpec((2, LANE), c2),
                  pl.BlockSpec((1, rq), c2), pl.BlockSpec((1, rkv), c2),
                  pl.BlockSpec((heads, HEADW, rq), c3), pl.BlockSpec((heads, HEADW, rkv), c3),
                  pl.BlockSpec((1, HEADW), c2), pl.BlockSpec((1, HEADW), c2)],
        out_specs=[pl.BlockSpec((tm, rq), lambda i: (i, 0)), pl.BlockSpec((tm, rkv), lambda i: (i, 0)),
                   pl.BlockSpec((tm, LANE), lambda i: (i, 0)),
                   pl.BlockSpec((heads, HEADW, rq), c3), pl.BlockSpec((heads, HEADW, rkv), c3),
                   pl.BlockSpec((1, rq), c2), pl.BlockSpec((1, rkv), c2),
                   pl.BlockSpec((1, HEADW), c2), pl.BlockSpec((1, HEADW), c2)],
        out_shape=[jax.ShapeDtypeStruct((t, rq), F32), jax.ShapeDtypeStruct((t, rkv), F32),
                   jax.ShapeDtypeStruct((t, LANE), F32),
                   jax.ShapeDtypeStruct((heads, HEADW, rq), F32), jax.ShapeDtypeStruct((heads, HEADW, rkv), F32),
                   jax.ShapeDtypeStruct((1, rq), F32), jax.ShapeDtypeStruct((1, rkv), F32),
                   jax.ShapeDtypeStruct((1, HEADW), F32), jax.ShapeDtypeStruct((1, HEADW), F32)],
        compiler_params=_params(("arbitrary",)),
    )(dq, dk, dv, cq, ckv, krw, cos, sin, masks, gq, gkv, wq_t, wkv_t, gqh, gkh)


def _lower_triangle(blk):
    return lax.broadcasted_iota(jnp.int32, (blk, blk), 0) >= lax.broadcasted_iota(jnp.int32, (blk, blk), 1)


def _attn_fwd(q, k, v, seq, blk):
    heads, t, _ = q.shape
    scale = QK ** -0.5
    nblk = seq // blk

    def body(q_ref, k_ref, v_ref, o_ref, lse_ref):
        tri = _lower_triangle(blk)
        for qi in range(nblk):
            rows = slice(qi * blk, (qi + 1) * blk)
            before = slice(0, qi * blk)
            qb = q_ref[0, rows, :]
            s_d = jnp.where(tri, _nt(qb, k_ref[0, rows, :]) * scale, -1e30)
            m = jnp.max(s_d, axis=-1, keepdims=True)
            if qi:
                s_b = _nt(qb, k_ref[0, before, :]) * scale
                m = jnp.maximum(m, jnp.max(s_b, axis=-1, keepdims=True))
                p_b = jnp.exp(s_b - m)
            p_d = jnp.exp(s_d - m)
            l = jnp.sum(p_d, axis=-1, keepdims=True)
            acc = _nn(p_d.astype(BF16), v_ref[0, rows, :])
            if qi:
                l += jnp.sum(p_b, axis=-1, keepdims=True)
                acc += _nn(p_b.astype(BF16), v_ref[0, before, :])
            o_ref[0, rows, :] = acc / l
            lse_ref[0, rows, :] = m + jnp.log(l)

    return pl.pallas_call(
        body, name="attn_fwd", grid=(heads, t // seq),
        in_specs=[pl.BlockSpec((1, seq, HEADW), lambda h, b: (h, b, 0)),
                  pl.BlockSpec((1, seq, HEADW), lambda h, b: (h, b, 0)),
                  pl.BlockSpec((1, seq, VHEAD), lambda h, b: (h, b, 0))],
        out_specs=[pl.BlockSpec((1, seq, VHEAD), lambda h, b: (h, b, 0)),
                   pl.BlockSpec((1, seq, 1), lambda h, b: (h, b, 0))],
        out_shape=[jax.ShapeDtypeStruct((heads, t, VHEAD), F32), jax.ShapeDtypeStruct((heads, t, 1), F32)],
        compiler_params=_params(("arbitrary", "arbitrary")),
    )(q, k, v)


def _attn_bwd(q, k, v, do, lse, delta, seq, blk, after):
    heads, t, _ = q.shape
    scale = QK ** -0.5
    nblk = seq // blk

    def body(q_ref, k_ref, v_ref, do_ref, lse_ref, dl_ref, _, dq_ref, dk_ref, dv_ref):
        tri = _lower_triangle(blk)
        dk_ref[...] = jnp.zeros_like(dk_ref)
        dv_ref[...] = jnp.zeros_like(dv_ref)
        for qi in range(nblk):
            rows = slice(qi * blk, (qi + 1) * blk)
            qb = q_ref[0, rows, :]
            dob = do_ref[0, rows, :]
            lse_b = lse_ref[0, rows, :]
            dl_b = dl_ref[0, rows, :]
            dq = jnp.zeros((blk, HEADW), F32)
            for keys, masked in ((slice(0, qi * blk), False), (rows, True)):
                if keys.stop == keys.start:
                    continue
                kb = k_ref[0, keys, :]
                p = jnp.exp(_nt(qb, kb) * scale - lse_b)
                if masked:
                    p = jnp.where(tri, p, 0.0)
                dp = _nt(dob, v_ref[0, keys, :])
                ds = (p * (dp - dl_b) * scale).astype(BF16)
                dv_ref[0, keys, :] += _tn(p.astype(BF16), dob)
                dk_ref[0, keys, :] += _tn(ds, qb)
                dq += _nn(ds, kb)
            dq_ref[0, rows, :] = dq

    hq = pl.BlockSpec((1, seq, HEADW), lambda h, b: (h, b, 0))
    hv = pl.BlockSpec((1, seq, VHEAD), lambda h, b: (h, b, 0))
    h1 = pl.BlockSpec((1, seq, 1), lambda h, b: (h, b, 0))
    return pl.pallas_call(
        body, name="attn_bwd", grid=(heads, t // seq),
        in_specs=[hq, hq, hv, hv, h1, h1, ANY],
        out_specs=[hq, hq, hv],
        out_shape=[jax.ShapeDtypeStruct((heads, t, HEADW), F32), jax.ShapeDtypeStruct((heads, t, HEADW), F32),
                   jax.ShapeDtypeStruct((heads, t, VHEAD), F32)],
        compiler_params=_params(("arbitrary", "arbitrary")),
    )(q, k, v, do, lse, delta, after)


def _out_fwd(ya, o, gb, w_out, x1, tm):
    t, w = ya.shape
    heads = o.shape[0]
    d = x1.shape[1]

    def body(ya_ref, o_ref, gb_ref, w_ref, x1_ref, x2_ref, yc_ref):
        yc_ref[:, 0:w] = ya_ref[...]
        for h in range(heads):
            lanes = slice(h * VHEAD, (h + 1) * VHEAD)
            yc_ref[:, w + h * VHEAD:w + (h + 1) * VHEAD] = _rms_fwd(o_ref[h], gb_ref[:, lanes], VHEAD).astype(BF16)
        x2_ref[...] = x1_ref[...] + _nn(yc_ref[...], w_ref[...])

    wy = w + heads * VHEAD
    row = pl.BlockSpec((tm, d), lambda i: (i, 0))
    return pl.pallas_call(
        body, name="out_fwd", grid=(t // tm,),
        in_specs=[pl.BlockSpec((tm, w), lambda i: (i, 0)), pl.BlockSpec((heads, tm, VHEAD), lambda i: (0, i, 0)),
                  pl.BlockSpec((1, heads * VHEAD), lambda i: (0, 0)), WHOLE_VMEM, row],
        out_specs=[row, pl.BlockSpec((tm, wy), lambda i: (i, 0))],
        out_shape=[jax.ShapeDtypeStruct((t, d), F32), jax.ShapeDtypeStruct((t, wy), BF16)],
        compiler_params=_params(("arbitrary",)),
    )(ya, o, gb, w_out, x1)


def _out_bwd(dx2, o, gb, w_out, w, tm):
    t, d = dx2.shape
    heads = o.shape[0]

    def body(dx_ref, o_ref, gb_ref, w_ref, dya_ref, do_ref, dl_ref, dgb_ref):
        @pl.when(pl.program_id(0) == 0)
        def _():
            dgb_ref[...] = jnp.zeros_like(dgb_ref)

        dyc = _nt(dx_ref[...].astype(BF16), w_ref[...])
        dya_ref[...] = dyc[:, 0:w]
        for h in range(heads):
            lanes = slice(h * VHEAD, (h + 1) * VHEAD)
            oh = o_ref[h]
            doh, dg = _rms_bwd(oh, gb_ref[:, lanes], dyc[:, w + h * VHEAD:w + (h + 1) * VHEAD], VHEAD)
            dgb_ref[:, lanes] += dg
            do_ref[h] = doh.astype(BF16)
            dl_ref[h] = jnp.sum(doh * oh, axis=-1, keepdims=True)

    ho = pl.BlockSpec((heads, tm, VHEAD), lambda i: (0, i, 0))
    vec = pl.BlockSpec((1, heads * VHEAD), lambda i: (0, 0))
    return pl.pallas_call(
        body, name="out_bwd", grid=(t // tm,),
        in_specs=[pl.BlockSpec((tm, d), lambda i: (i, 0)), ho, vec, WHOLE_VMEM],
        out_specs=[pl.BlockSpec((tm, w), lambda i: (i, 0)), ho, pl.BlockSpec((heads, tm, 1), lambda i: (0, i, 0)), vec],
        out_shape=[jax.ShapeDtypeStruct((t, w), F32), jax.ShapeDtypeStruct((heads, t, VHEAD), BF16),
                   jax.ShapeDtypeStruct((heads, t, 1), F32), jax.ShapeDtypeStruct((1, heads * VHEAD), F32)],
        compiler_params=_params(("arbitrary",)),
    )(dx2, o, gb, w_out)


def _loss_head(y, target, tm):
    t, d = y.shape

    def body(y_ref, t_ref, dy_ref, half_ref, loss_ref):
        @pl.when(pl.program_id(0) == 0)
        def _():
            loss_ref[...] = jnp.zeros_like(loss_ref)

        err = y_ref[...] - t_ref[...]
        dy = err * (1.0 / d)
        dy_ref[...] = dy
        half_ref[...] = (0.5 * dy).astype(BF16)
        part = jnp.sum(jnp.sum(err * err, axis=-1, keepdims=True) * (1.0 / d), axis=0, keepdims=True)
        loss_ref[...] += 0.5 * part

    row = pl.BlockSpec((tm, d), lambda i: (i, 0))
    return pl.pallas_call(
        body, name="loss_head", grid=(t // tm,),
        in_specs=[row, row], out_specs=[row, row, pl.BlockSpec((1, 1), lambda i: (0, 0))],
        out_shape=[jax.ShapeDtypeStruct((t, d), F32), jax.ShapeDtypeStruct((t, d), BF16),
                   jax.ShapeDtypeStruct((1, 1), F32)],
        compiler_params=_params(("arbitrary",)),
    )(y, target)


def _place():
    return lax.axis_index("x"), lax.axis_index("y"), lax.axis_index("c")


HBM = pl.BlockSpec(memory_space=pltpu.HBM)
SEM = pl.BlockSpec(memory_space=pltpu.SEMAPHORE)
DATAFLOW = pltpu.SideEffectType.DATAFLOW_SIDE_EFFECTING


def _plan_copies(plan, refs, send_sems, recv_sems):
    def block(ref, blk):
        if blk is None:
            return ref
        return ref.at[blk[0], pl.ds(0, blk[1])] if isinstance(blk, tuple) else ref.at[blk]

    cps = []
    for i, (sb, sblk, db, dblk, dev) in enumerate(plan(*_place())):
        cps.append(pltpu.make_async_remote_copy(
            src_ref=block(refs[sb], sblk), dst_ref=block(refs[db], dblk),
            send_sem=send_sems.at[i], recv_sem=recv_sems.at[i], device_id=dev, device_id_type=MESH))
    return cps


def _push_start(bufs, plan, ncopy, name, deps=()):
    nb = len(bufs)

    def body(*refs):
        outs = refs[nb + len(deps):]
        for cp in _plan_copies(plan, refs[:nb], outs[0], outs[1]):
            cp.start()
        outs[-1][...] = jnp.zeros_like(outs[-1])

    res = pl.pallas_call(
        body, name=name,
        out_shape=(pltpu.SemaphoreType.DMA((ncopy,)), pltpu.SemaphoreType.DMA((ncopy,)),
                   *[pltpu.HBM(b.shape, b.dtype) for b in bufs], jax.ShapeDtypeStruct((SUBLANE, LANE), F32)),
        in_specs=[HBM] * nb + [ANY] * len(deps),
        out_specs=(SEM, SEM, *[HBM] * nb, WHOLE_VMEM),
        input_output_aliases={i: 2 + i for i in range(nb)},
        compiler_params=pltpu.CompilerParams(has_side_effects=DATAFLOW),
    )(*[pltpu.with_memory_space_constraint(b, pltpu.HBM) for b in bufs], *deps)
    return res[0], res[1], list(res[2:2 + nb]), res[-1]


def _push_wait(send_sems, recv_sems, bufs, plan, after, name):
    nb = len(bufs)

    def body(*refs):
        for cp in _plan_copies(plan, refs[:nb], refs[nb], refs[nb + 1]):
            cp.wait_send()
            cp.wait_recv()

    res = pl.pallas_call(
        body, name=name,
        out_shape=[pltpu.HBM(b.shape, b.dtype) for b in bufs],
        in_specs=[HBM] * nb + [SEM, SEM, ANY], out_specs=[HBM] * nb,
        input_output_aliases={i: i for i in range(nb)},
        compiler_params=pltpu.CompilerParams(has_side_effects=DATAFLOW),
    )(*bufs, send_sems, recv_sems, after)
    return list(res)


def _other_chips(x, y):
    return ((1 - x, y), (x, 1 - y), (1 - x, 1 - y))


class _Exchange:
    def __init__(self, bufs, plan, ncopy, name, deps=()):
        self.plan, self.name = plan, name
        self.send, self.recv, self.bufs, self.token = _push_start(bufs, plan, ncopy, name + "_start", deps)

    def wait(self, after):
        return _push_wait(self.send, self.recv, self.bufs, self.plan, after, self.name + "_wait")


class _Chain:
    def __init__(self, bufs):
        self.bufs = list(bufs)

    def start(self, plan, ncopy, name, deps=()):
        send, recv, self.bufs, token = _push_start(self.bufs, plan, ncopy, name + "_start", deps)
        return (send, recv, plan, name), token

    def wait(self, pending, after):
        send, recv, plan, name = pending
        self.bufs = _push_wait(send, recv, self.bufs, plan, after, name + "_wait")


class _StagedGather:
    def __init__(self, shards, me, name, pad_to=None):
        self.n = n = len(shards)
        self.name = name
        rows = shards[0].shape[0]
        lands = []
        for s in shards:
            land = lax.empty((N_DEV, pad_to or rows) + s.shape[1:], s.dtype)
            if pad_to and pad_to != rows:
                land = lax.dynamic_update_slice(
                    land, jnp.zeros((N_DEV, pad_to - rows) + s.shape[1:], s.dtype), (0, rows, 0))
            lands.append(lax.dynamic_update_slice(land, s[None], (me, 0, 0)))
        self.chain = _Chain(list(shards) + lands)
        self.pending = {}

        def blk(b):
            return (b, rows) if pad_to and pad_to != rows else b

        def to_sibling(blocks):
            return lambda x, y, c: [(n + a, blk(b), n + a, blk(b), (x, y, 1 - c))
                                    for a in range(n) for b in blocks(x, y, c)]

        def nbr_blocks(x, y, c):
            return [4 * (1 - x) + 2 * y + c, 4 * x + 2 * (1 - y) + c]

        def diag(x, y, c):
            sx, sy = (1 - x) * (1 - c) + x * c, y * (1 - c) + (1 - y) * c
            tx, ty = x * (1 - c) + (1 - x) * c, (1 - y) * (1 - c) + y * c
            b = blk(4 * sx + 2 * sy + c)
            return [(n + a, b, n + a, b, (tx, ty, c)) for a in range(n)]

        self.plans = {
            "own": (lambda x, y, c: [(a, None, n + a, blk(4 * x + 2 * y + c), (x, y, 1 - c)) for a in range(n)], n),
            "nbr": (lambda x, y, c: [(a, None, n + a, blk(4 * x + 2 * y + c), dev) for a in range(n)
                                     for dev in ((1 - x, y, c), (x, 1 - y, c))], 2 * n),
            "diag": (diag, n),
            "nbr_d2d": (to_sibling(nbr_blocks), 2 * n),
            "own_nbr_d2d": (to_sibling(lambda x, y, c: [4 * x + 2 * y + c] + nbr_blocks(x, y, c)), 3 * n),
            "diag_d2d": (to_sibling(lambda x, y, c: [4 * (1 - x) + 2 * (1 - y) + c]), n),
        }

    def start(self, stage, deps=()):
        plan, ncopy = self.plans[stage]
        self.pending[stage], token = self.chain.start(plan, ncopy, self.name + "_" + stage, deps)
        return token

    def wait(self, stage, after):
        self.chain.wait(self.pending.pop(stage), after)

    def lands(self):
        return self.chain.bufs[self.n:]


def _gather_ici(shards, me, name, deps=()):
    n = len(shards)
    lands = [lax.dynamic_update_slice(lax.empty((N_DEV,) + s.shape, s.dtype), s[None], (me, 0, 0)) for s in shards]

    def plan(x, y, c):
        return [(a, None, n + a, 4 * x + 2 * y + c, (px, py, c)) for a in range(n) for px, py in _other_chips(x, y)]

    return _Exchange(list(shards) + lands, plan, 3 * n, name, deps)


def _gather_d2d(lands, name, deps=()):
    n = len(lands)

    def plan(x, y, c):
        blocks = [4 * x + 2 * y + c] + [4 * px + 2 * py + c for px, py in _other_chips(x, y)]
        return [(a, b, a, b, (x, y, 1 - c)) for a in range(n) for b in blocks]

    return _Exchange(list(lands), plan, 4 * n, name, deps)


def _reduce_d2d(grads, name, deps=(), rows=None):
    n = len(grads)
    lands = [lax.empty((4,) + g.shape[1:], g.dtype) for g in grads]

    def blk(b):
        return b if rows is None else (b, rows)

    def plan(x, y, c):
        return [(a, blk(2 * s + (1 - c)), n + a, blk(s), (x, y, 1 - c)) for a in range(n) for s in range(4)]

    return _Exchange(list(grads) + lands, plan, 4 * n, name, deps)


def _reduce_ici(chip, name, deps=(), rows=None):
    n = len(chip)
    lands = [lax.empty((3,) + g.shape[1:], g.dtype) for g in chip]

    def blk(b):
        return b if rows is None else (b, rows)

    def plan(x, y, c):
        return [(a, blk(2 * px + py), n + a, blk(k), (px, py, c))
                for a in range(n) for k, (px, py) in enumerate(_other_chips(x, y))]

    return _Exchange(list(chip) + lands, plan, 3 * n, name, deps)


def _pair_add(full, got, core, name, rows=None):
    _, r, cdim = full.shape
    br = _row_block(rows or r, 512)

    def body(c_ref, f_ref, g_ref, o_ref):
        o_ref[...] = (f_ref[...].astype(F32) + g_ref[...].astype(F32)).astype(o_ref.dtype)

    return pl.pallas_call(
        body, name=name,
        grid_spec=pltpu.PrefetchScalarGridSpec(
            num_scalar_prefetch=1, grid=(4, (rows or r) // br),
            in_specs=[pl.BlockSpec((1, br, cdim), lambda s, i, c_ref: (2 * s + c_ref[0], i, 0)),
                      pl.BlockSpec((1, br, cdim), lambda s, i, c_ref: (s, i, 0))],
            out_specs=pl.BlockSpec((1, br, cdim), lambda s, i, c_ref: (s, i, 0))),
        out_shape=jax.ShapeDtypeStruct((4, r, cdim), full.dtype),
        compiler_params=_params(("arbitrary", "arbitrary")),
    )(core, full, got)


def _sum_devices(stack):
    _, r, cdim = stack.shape

    def body(s_ref, o_ref):
        acc = s_ref[0]
        for k in range(1, N_DEV):
            acc = acc + s_ref[k]
        o_ref[...] = acc

    return pl.pallas_call(
        body, name="sum_devices", out_shape=jax.ShapeDtypeStruct((r, cdim), F32),
        compiler_params=_params(),
    )(stack)


def _adamw(w, g, m, v, name):
    r, cdim = w.shape
    br = _row_block(r, 256)

    def body(w_ref, g_ref, m_ref, v_ref, d_ref, nm_ref, nv_ref):
        g = g_ref[...]
        nm = ADAM_B1 * m_ref[...] + (1.0 - ADAM_B1) * g
        nv = ADAM_B2 * v_ref[...] + (1.0 - ADAM_B2) * (g * g)
        m_hat = nm / (1.0 - ADAM_B1 ** ADAM_STEP)
        v_hat = nv / (1.0 - ADAM_B2 ** ADAM_STEP)
        d_ref[...] = -ADAM_LR * (m_hat / (jnp.sqrt(v_hat) + ADAM_EPS) + ADAM_WD * w_ref[...])
        nm_ref[...] = nm
        nv_ref[...] = nv

    spec = pl.BlockSpec((br, cdim), lambda i: (i, 0))
    shape = jax.ShapeDtypeStruct((r, cdim), F32)
    return pl.pallas_call(
        body, name=name, grid=(r // br,), in_specs=[spec] * 4, out_specs=[spec] * 3,
        out_shape=[shape] * 3, compiler_params=_params(("arbitrary",)),
    )(w, g, m, v)


def _sum_adamw(chip, got, slot, w, m, v, name, after):
    rows, cdim = w.shape
    bc = 2 * LANE if cdim % (2 * LANE) == 0 else cdim

    def body(s_ref, c_ref, g_ref, w_ref, m_ref, v_ref, _, go_ref, d_ref, nm_ref, nv_ref):
        g = c_ref[0].astype(F32)
        for k in range(3):
            g = g + g_ref[k].astype(F32)
        nm = ADAM_B1 * m_ref[...] + (1.0 - ADAM_B1) * g
        nv = ADAM_B2 * v_ref[...] + (1.0 - ADAM_B2) * (g * g)
        m_hat = nm / (1.0 - ADAM_B1 ** ADAM_STEP)
        v_hat = nv / (1.0 - ADAM_B2 ** ADAM_STEP)
        go_ref[...] = g
        d_ref[...] = -ADAM_LR * (m_hat / (jnp.sqrt(v_hat) + ADAM_EPS) + ADAM_WD * w_ref[...])
        nm_ref[...] = nm
        nv_ref[...] = nv

    spec = pl.BlockSpec((rows, bc), lambda j, s_ref: (0, j))
    shape = jax.ShapeDtypeStruct((rows, cdim), F32)
    return pl.pallas_call(
        body, name=name,
        grid_spec=pltpu.PrefetchScalarGridSpec(
            num_scalar_prefetch=1, grid=(cdim // bc,),
            in_specs=[pl.BlockSpec((1, rows, bc), lambda j, s_ref: (s_ref[0], 0, j)),
                      pl.BlockSpec((3, rows, bc), lambda j, s_ref: (0, 0, j)), spec, spec, spec, ANY],
            out_specs=[spec] * 4),
        out_shape=[shape] * 4,
        compiler_params=_params(("arbitrary",)),
    )(slot, chip, got, w, m, v, after)


WEIGHTS = ("ffn1_norm_g", "ffn1_w_gate", "ffn1_w_up", "ffn1_w_down", "mix_norm_g", "w_in", "gmlp_v_norm_g",
           "gmlp_w_s", "gmlp_b_s", "mla_q_norm_g", "mla_w_q_up", "mla_kv_norm_g", "mla_w_kv_up", "mla_q_head_g",
           "mla_k_head_g", "gmlp_out_g", "mla_out_g", "w_out", "ffn2_norm_g", "ffn2_w_gate", "ffn2_w_up",
           "ffn2_w_down")
SHARDED = {"ffn1_w_gate": True, "ffn1_w_up": True, "ffn1_w_down": False, "w_in": True, "mla_w_q_up": True,
           "mla_w_kv_up": True, "w_out": False, "ffn2_w_gate": True, "ffn2_w_up": True, "ffn2_w_down": False}


def _col_block(m, target):
    best = LANE
    for cand in range(LANE, min(m, target) + 1, LANE):
        if m % cand == 0:
            best = cand
    return best


def _shard_rows(w, transposed, pad_to=None):
    rows = (w[0].T if transposed else w[0]).astype(BF16)
    if pad_to is not None and pad_to != rows.shape[0]:
        rows = jnp.pad(rows, ((0, pad_to - rows.shape[0]), (0, 0)))
    return rows


def _pack(parts):
    flat = []
    for p in parts:
        f = p.reshape(-1).astype(F32)
        flat.append(jnp.pad(f, (0, _round_up(f.size, LANE) - f.size)))
    flat = jnp.concatenate(flat)
    rows = _round_up(flat.size // LANE, SUBLANE)
    return jnp.pad(flat, (0, rows * LANE - flat.size)).reshape(rows, LANE)


def _unpack(packed, shapes):
    out, row = [], 0
    for shp in shapes:
        size = 1
        for s in shp:
            size *= s
        nrows = _round_up(size, LANE) // LANE
        out.append(packed[row:row + nrows].reshape(-1)[:size].reshape(shp))
        row += nrows
    return out


def kernel(x, positions, ffn1_norm_g, ffn1_w_gate, ffn1_w_up, ffn1_w_down, mix_norm_g, w_in, gmlp_v_norm_g, gmlp_w_s, gmlp_b_s, mla_q_norm_g, mla_w_q_up, mla_kv_norm_g, mla_w_kv_up, mla_q_head_g, mla_k_head_g, gmlp_out_g, mla_out_g, w_out, ffn2_norm_g, ffn2_w_gate, ffn2_w_up, ffn2_w_down, loss_target, m_ffn1_norm_g, m_ffn1_w_gate, m_ffn1_w_up, m_ffn1_w_down, m_mix_norm_g, m_w_in, m_gmlp_v_norm_g, m_gmlp_w_s, m_gmlp_b_s, m_mla_q_norm_g, m_mla_w_q_up, m_mla_kv_norm_g, m_mla_w_kv_up, m_mla_q_head_g, m_mla_k_head_g, m_gmlp_out_g, m_mla_out_g, m_w_out, m_ffn2_norm_g, m_ffn2_w_gate, m_ffn2_w_up, m_ffn2_w_down, v_ffn1_norm_g, v_ffn1_w_gate, v_ffn1_w_up, v_ffn1_w_down, v_mix_norm_g, v_w_in, v_gmlp_v_norm_g, v_gmlp_w_s, v_gmlp_b_s, v_mla_q_norm_g, v_mla_w_q_up, v_mla_kv_norm_g, v_mla_w_kv_up, v_mla_q_head_g, v_mla_k_head_g, v_gmlp_out_g, v_mla_out_g, v_w_out, v_ffn2_norm_g, v_ffn2_w_gate, v_ffn2_w_up, v_ffn2_w_down):
    wts = dict(zip(WEIGHTS, (ffn1_norm_g, ffn1_w_gate, ffn1_w_up, ffn1_w_down, mix_norm_g, w_in, gmlp_v_norm_g, gmlp_w_s, gmlp_b_s, mla_q_norm_g, mla_w_q_up, mla_kv_norm_g, mla_w_kv_up, mla_q_head_g, mla_k_head_g, gmlp_out_g, mla_out_g, w_out, ffn2_norm_g, ffn2_w_gate, ffn2_w_up, ffn2_w_down)))
    mom1 = dict(zip(WEIGHTS, (m_ffn1_norm_g, m_ffn1_w_gate, m_ffn1_w_up, m_ffn1_w_down, m_mix_norm_g, m_w_in, m_gmlp_v_norm_g, m_gmlp_w_s, m_gmlp_b_s, m_mla_q_norm_g, m_mla_w_q_up, m_mla_kv_norm_g, m_mla_w_kv_up, m_mla_q_head_g, m_mla_k_head_g, m_gmlp_out_g, m_mla_out_g, m_w_out, m_ffn2_norm_g, m_ffn2_w_gate, m_ffn2_w_up, m_ffn2_w_down)))
    mom2 = dict(zip(WEIGHTS, (v_ffn1_norm_g, v_ffn1_w_gate, v_ffn1_w_up, v_ffn1_w_down, v_mix_norm_g, v_w_in, v_gmlp_v_norm_g, v_gmlp_w_s, v_gmlp_b_s, v_mla_q_norm_g, v_mla_w_q_up, v_mla_kv_norm_g, v_mla_w_kv_up, v_mla_q_head_g, v_mla_k_head_g, v_gmlp_out_g, v_mla_out_g, v_w_out, v_ffn2_norm_g, v_ffn2_w_gate, v_ffn2_w_up, v_ffn2_w_down)))

    b_loc, seq, d = x.shape
    t = b_loc * seq
    ffs = ffn1_w_gate.shape[2]
    fp = _round_up(ffs, LANE)
    wg = gmlp_v_norm_g.shape[1]
    groups = gmlp_w_s.shape[1]
    rq, rkv = mla_q_norm_g.shape[1], mla_kv_norm_g.shape[1]
    heads = mla_out_g.shape[1]
    assert w_in.shape[2] * N_DEV == 2 * wg + rq + rkv + ROPE and mla_w_kv_up.shape[2] * N_DEV == heads * HEADW
    tm = min(512, t)
    tm_mix = min(256, t)
    blk = min(256, seq)

    xf = x.reshape(t, d)
    target = loss_target.reshape(t, d)
    pos = positions.reshape(t, 1).astype(F32)
    half = ROPE // 2
    inv_freq = 1.0 / (ROPE_THETA ** (jnp.arange(half, dtype=F32) / half))
    freq = jnp.concatenate([inv_freq, inv_freq, jnp.zeros((LANE - ROPE,), F32)])[None, :]
    lane = jnp.arange(LANE)
    masks = jnp.stack([jnp.where(lane < half, -1.0, 0.0),
                       jnp.where((lane >= half) & (lane < ROPE), 1.0, 0.0)]).astype(F32)
    gqh = jnp.pad(mla_q_head_g, ((0, 0), (0, HEADW - QK)))
    gkh = jnp.pad(mla_k_head_g, ((0, 0), (0, HEADW - QK)))
    bias = jnp.repeat(gmlp_b_s[0].T, CHUNK, axis=1)
    gouta = gmlp_out_g.reshape(1, wg)
    goutb = mla_out_g.reshape(1, heads * VHEAD)
    ws = gmlp_w_s[0]

    px, py, pc = _place()
    me = 4 * px + 2 * py + pc
    core = pc.astype(jnp.int32).reshape(1)
    slot = (2 * px + py).astype(jnp.int32).reshape(1)
    order = [n for n in WEIGHTS if n in SHARDED]
    group = {"ffn1": [n for n in order if n.startswith("ffn1")], "ffn2": [n for n in order if n.startswith("ffn2")],
             "mix": [n for n in order if not n.startswith("ffn")]}
    shard = {n: _shard_rows(wts[n], SHARDED[n]) for n in group["ffn1"]}
    frows = ffs if ffs != fp else None

    def tied(arr, token):
        return arr + token[0, 0].astype(arr.dtype)

    xnb, ynb, dgn = 4 * (1 - px) + 2 * py, 4 * px + 2 * (1 - py), 4 * (1 - px) + 2 * (1 - py)
    ids_a = jnp.stack([me, 4 * px + 2 * py + (1 - pc)]).astype(jnp.int32)
    ids_b = jnp.stack([xnb, xnb + 1, ynb, ynb + 1]).astype(jnp.int32)
    ids_c = jnp.stack([dgn, dgn + 1]).astype(jnp.int32)
    g1 = _StagedGather([shard[n] for n in group["ffn1"]], me, "gather_ffn1", pad_to=fp)
    token = g1.start("own")
    token = g1.start("nbr", deps=(token,))
    for n in group["mix"] + group["ffn2"]:
        shard[n] = _shard_rows(tied(wts[n], token), SHARDED[n])
    g3 = _StagedGather([shard[n] for n in group["ffn2"]], me, "gather_ffn2", pad_to=fp)
    g1.wait("own", token)
    x1, xn1, kept1 = _ffn_fwd(xf, None, ffn1_norm_g, ids_a, *g1.lands(), None, tm, "ffn1_fwd_a")
    rope_cos, rope_sin = _rope_tables(pos, freq, tm, x1)
    g1.wait("nbr", rope_cos)
    token = g1.start("diag")
    ici2 = _gather_ici([shard[n] for n in group["mix"]], me, "gather_mix_ici", deps=(token,))
    token = g3.start("nbr", deps=(ici2.token,))
    token = g1.start("nbr_d2d", deps=(token,))
    g1.wait("nbr_d2d", token)
    x1, xn1, kept1 = _ffn_fwd(x1, xn1, None, ids_b, *g1.lands(), kept1, tm, "ffn1_fwd_b")
    g1.wait("diag", x1)
    token = g1.start("diag_d2d")
    d2d2 = _gather_d2d(ici2.wait(x1)[len(group["mix"]):], "gather_mix_d2d", deps=(token,))
    g1.wait("diag_d2d", d2d2.token)
    full = dict(zip(group["ffn1"], g1.lands()))
    x1, xn1, (gd1, sl1, h1) = _ffn_fwd(x1, xn1, None, ids_c, full["ffn1_w_gate"], full["ffn1_w_up"],
                                       full["ffn1_w_down"], kept1, tm, "ffn1_fwd_c")
    full.update(zip(group["mix"], d2d2.wait(x1)))
    win_t = full["w_in"].reshape(-1, d)
    splits = (2 * wg, rq, rkv, LANE)
    wq_t = jnp.pad(full["mla_w_q_up"].reshape(heads, QK, rq), ((0, 0), (0, HEADW - QK), (0, 0)))
    wkv_t = full["mla_w_kv_up"].reshape(heads, HEADW, rkv)
    wout = full["w_out"].reshape(-1, d)
    hn, zuv, cq, ckv, krw = _inproj_fwd(x1, mix_norm_g, win_t, splits, tm)
    ya = _gmlp_fwd(zuv, gmlp_v_norm_g, ws, bias, gouta, tm_mix)
    g3.wait("nbr", ya)
    token = g3.start("diag")
    token = g3.start("own_nbr_d2d", deps=(token,))
    q, k, vv = _mla_proj_fwd(cq, ckv, krw, rope_cos, rope_sin, masks, mla_q_norm_g, mla_kv_norm_g, wq_t, wkv_t,
                             tied(gqh, token), gkh, tm)
    o, lse = _attn_fwd(q, k, vv, seq, blk)
    g3.wait("diag", o)
    token = g3.start("diag_d2d")
    x2, ycat = _out_fwd(ya, o, tied(goutb, token), wout, x1, tm)
    g3.wait("own_nbr_d2d", x2)
    g3.wait("diag_d2d", x2)
    full.update(zip(group["ffn2"], g3.lands()))
    x3, xn2, (gd2, sl2, h2) = _ffn_fwd(x2, None, ffn2_norm_g, jnp.arange(N_DEV, dtype=jnp.int32),
                                       full["ffn2_w_gate"], full["ffn2_w_up"], full["ffn2_w_down"], None, tm,
                                       "ffn2_fwd")
    dx3, dy3_half, loss_part = _loss_head(x3, target, tm)

    outs_g, outs_d, outs_m, outs_v = {}, {}, {}, {}

    def finish(names, chip, got, after):
        for n, cp, gt in zip(names, chip, got):
            rows_of = (lambda a: a[0].T) if SHARDED[n] else (lambda a: a[0])
            res = _sum_adamw(cp, gt, slot, rows_of(wts[n]), rows_of(mom1[n]), rows_of(mom2[n]), "adamw_" + n, after)
            outs_g[n], outs_d[n], outs_m[n], outs_v[n] = [r.T[None] if SHARDED[n] else r[None] for r in res]
            after = res[3]
        return after

    def chip_sums(names, ex, after, rows=None):
        res = ex.wait(after)
        return [_pair_add(f, gt, core, "pair_add_" + n, rows)
                for n, f, gt in zip(names, res[:len(names)], res[len(names):])]

    tk = min(2048, t)
    grads = {}
    small = {}
    dx2, small["ffn2_norm_g"], da2, db2 = _ffn_bwd(
        dx3, x2, ffn2_norm_g, gd2, sl2, full["ffn2_w_gate"], full["ffn2_w_up"], full["ffn2_w_down"], tm, "ffn2_bwd")
    grads["ffn2_w_gate"] = _matmul_tn_resident(da2, xn2, fp, BF16, "dw_ffn2_gate").reshape(N_DEV, fp, d)
    grads["ffn2_w_up"] = _matmul_tn_resident(db2, xn2, fp, BF16, "dw_ffn2_up").reshape(N_DEV, fp, d)
    grads["ffn2_w_down"] = _matmul_tn_resident(h2, dy3_half, fp, BF16, "dw_ffn2_down").reshape(N_DEV, fp, d)
    red_a2 = _reduce_d2d([grads[n] for n in group["ffn2"]], "reduce_ffn2_d2d", rows=frows)
    dya, do, delta, small["mla_out_g"] = _out_bwd(dx2, o, tied(goutb, red_a2.token), wout, wg, tm)
    grads["w_out"] = _matmul_tn(ycat, dx2, _col_block(ycat.shape[1], 768), d, tk, BF16, "dw_out").reshape(
        N_DEV, -1, d)
    chip2 = chip_sums(group["ffn2"], red_a2, dya, frows)
    red_b2 = _reduce_ici(chip2, "reduce_ffn2_ici", rows=frows)
    dq, dk, dv = _attn_bwd(q, k, vv, do, lse, delta, seq, blk, red_b2.token)
    (dcq, dckv, dkrw, dwq, dwkv, small["mla_q_norm_g"], small["mla_kv_norm_g"], dgqh, dgkh) = _mla_proj_bwd(
        dq, dk, dv, cq, ckv, krw, rope_cos, rope_sin, masks, mla_q_norm_g, mla_kv_norm_g, wq_t, wkv_t, gqh, gkh,
        tm)
    small["mla_q_head_g"], small["mla_k_head_g"] = dgqh[:, :QK], dgkh[:, :QK]
    grads["mla_w_q_up"] = dwq[:, :QK].astype(BF16).reshape(N_DEV, -1, rq)
    grads["mla_w_kv_up"] = dwkv.astype(BF16).reshape(N_DEV, -1, rkv)
    dzuv, small["gmlp_w_s"], dbs, small["gmlp_v_norm_g"], small["gmlp_out_g"] = _gmlp_bwd(
        dya, zuv, gmlp_v_norm_g, ws, bias, gouta, tm_mix)
    small["gmlp_b_s"] = dbs[:, :, 0]
    dx1, small["mix_norm_g"], dzc, dy1_half = _inproj_bwd([dzuv, dcq, dckv, dkrw], x1, mix_norm_g, win_t, dx2,
                                                          splits, tm_mix)
    grads["w_in"] = _matmul_tn_resident(dzc, hn, _col_block(dzc.shape[1], 768), BF16, "dw_in",
                                        out_rows=win_t.shape[0]).reshape(N_DEV, -1, d)
    res_b2 = red_b2.wait(grads["w_in"])
    red_am = _reduce_d2d([grads[n] for n in group["mix"]], "reduce_mix_d2d")

    def ffn1_dw(n, lhs, rhs, token):
        return _matmul_tn_resident(lhs, rhs, fp, BF16, "dw_" + n, deps=(token,)).reshape(N_DEV, fp, d)

    gr = ffn1_dw("ffn1_w_down", h1, dy1_half, red_am.token)
    red_ad = _reduce_d2d([gr], "reduce_ffn1_w_down_d2d", deps=(red_am.token,), rows=frows)
    dx0, small["ffn1_norm_g"], da1, db1 = _ffn_bwd(
        dx1, xf, tied(ffn1_norm_g, red_ad.token), gd1, sl1, full["ffn1_w_gate"], full["ffn1_w_up"],
        full["ffn1_w_down"], tm, "ffn1_bwd")
    chipm = chip_sums(group["mix"], red_am, dx0)
    red_bm = _reduce_ici(chipm, "reduce_mix_ici")
    red_b = [("ffn1_w_down", _reduce_ici(chip_sums(["ffn1_w_down"], red_ad, dx0, frows), "reduce_ffn1_w_down_ici",
                                         deps=(red_bm.token,), rows=frows))]
    rep = [n for n in WEIGHTS if n not in SHARDED]
    small_ici = _gather_ici([_pack([small[n] for n in rep] + [loss_part])], me, "gather_small_ici",
                            deps=(red_b[-1][1].token,))
    gr = ffn1_dw("ffn1_w_gate", da1, xn1, small_ici.token)
    red_ag = _reduce_d2d([gr], "reduce_ffn1_w_gate_d2d", rows=frows)
    n2 = group["ffn2"]
    after = finish(n2[0:1], res_b2[0:1], res_b2[3:4], red_ag.token)
    red_b.append(("ffn1_w_gate", _reduce_ici(chip_sums(["ffn1_w_gate"], red_ag, after, frows),
                                             "reduce_ffn1_w_gate_ici", rows=frows)))
    to_sibling = (2 * jnp.arange(4) + (1 - pc)).astype(jnp.int32)
    kept = (2 * jnp.arange(4) + pc).astype(jnp.int32)
    gr = _matmul_tn_blocks(db1, xn1, fp, to_sibling, None, "dw_ffn1_w_up_sent", deps=(red_b[-1][1].token,))
    small_d2d = _gather_d2d(small_ici.wait(gr)[1:], "gather_small_d2d")
    red_au = _reduce_d2d([gr.reshape(N_DEV, fp, d)], "reduce_ffn1_w_up_d2d", deps=(small_d2d.token,), rows=frows)
    gr = _matmul_tn_blocks(db1, xn1, fp, kept, red_au.bufs[0].reshape(-1, d), "dw_ffn1_w_up_kept",
                           deps=(red_au.token,))
    red_au.bufs[0] = gr.reshape(N_DEV, fp, d)
    red_b.append(("ffn1_w_up", _reduce_ici(chip_sums(["ffn1_w_up"], red_au, gr, frows),
                                           "reduce_ffn1_w_up_ici", rows=frows)))
    after = finish(n2[1:3], res_b2[1:3], res_b2[4:6], red_b[-1][1].token)
    res = red_bm.wait(after)
    nm_ = len(group["mix"])
    after = finish(group["mix"], res[:nm_], res[nm_:], after)
    total = _sum_devices(small_d2d.wait(after)[0])
    zero = jnp.zeros((1,), F32)
    dlt, nm, nv = _adamw(_pack([wts[n] for n in rep] + [zero]), total, _pack([mom1[n] for n in rep] + [zero]),
                         _pack([mom2[n] for n in rep] + [zero]), "adamw_small")
    shapes = [wts[n].shape for n in rep] + [(1,)]
    for n, g, dl, m1, m2 in zip(rep, _unpack(total, shapes), _unpack(dlt, shapes), _unpack(nm, shapes),
                                _unpack(nv, shapes)):
        outs_g[n], outs_d[n], outs_m[n], outs_v[n] = g, dl, m1, m2
    loss = _unpack(total, shapes)[-1].reshape(())
    after = dlt
    for n, ex in red_b:
        res = ex.wait(after)
        after = finish([n], res[:1], res[1:], after)

    return (loss, dx0.reshape(b_loc, seq, d), *[outs_g[n] for n in WEIGHTS], *[outs_d[n] for n in WEIGHTS],
            *[outs_m[n] for n in WEIGHTS], *[outs_v[n] for n in WEIGHTS])
```

```python
import functools

import jax
import jax.numpy as jnp
from jax import lax
from jax.experimental import pallas as pl
from jax.experimental.pallas import tpu as pltpu

F32 = jnp.float32
BF16 = jnp.bfloat16
EPS = 1e-6
LANE = 128
SUBLANE = 8
N_DEV = 8
VMEM_LIMIT = 60 * 1024 * 1024
NOPE = 128
ROPE = 64
VHEAD = 128
QK = NOPE + ROPE
HEADW = 2 * LANE
CHUNK = 128
ROPE_THETA = 10000.0
ADAM_LR, ADAM_B1, ADAM_B2, ADAM_EPS, ADAM_WD, ADAM_STEP = 0.001, 0.9, 0.999, 1e-08, 0.01, 10
MESH = pl.DeviceIdType.MESH
ANY = pl.BlockSpec(memory_space=pl.ANY)
WHOLE_VMEM = pl.BlockSpec(memory_space=pltpu.VMEM)


def _params(sem=None):
    return pltpu.CompilerParams(dimension_semantics=sem, vmem_limit_bytes=VMEM_LIMIT)


def _round_up(n, m):
    return -(-n // m) * m


def _row_block(rows, target):
    best = rows
    for cand in range(SUBLANE, min(rows, target) + 1, SUBLANE):
        if rows % cand == 0:
            best = cand
    return best if best <= target else rows


def _nn(a, b):
    return jnp.dot(a, b, preferred_element_type=F32)


def _nt(a, b):
    return lax.dot_general(a, b, (((1,), (1,)), ((), ())), preferred_element_type=F32)


def _tn(a, b):
    return lax.dot_general(a, b, (((0,), (0,)), ((), ())), preferred_element_type=F32)


def _rstd(x, n):
    return lax.rsqrt(jnp.sum(x * x, axis=-1, keepdims=True) * (1.0 / n) + EPS)


def _rms_fwd(x, g, n):
    return x * _rstd(x, n) * g


def _rms_bwd(x, g, dy, n):
    r = _rstd(x, n)
    xh = x * r
    dyg = dy * g
    dx = r * (dyg - xh * (jnp.sum(dyg * xh, axis=-1, keepdims=True) * (1.0 / n)))
    return dx, jnp.sum(dy * xh, axis=0, keepdims=True)


def _gelu(x):
    return 0.5 * x * (1.0 + lax.erf(x * 0.7071067811865476))


def _gelu_grad(x):
    return 0.5 * (1.0 + lax.erf(x * 0.7071067811865476)) + x * jnp.exp(-0.5 * x * x) * 0.3989422804014327


def _ffn_fwd(base, xn, g, ids, wg_t, wu_t, wd, saved, tm, name):
    t, d = base.shape
    nb, fp, _ = wg_t.shape
    n = ids.shape[0]
    first = xn is None
    if saved is None:
        saved = [lax.empty((t, nb * fp), BF16) for _ in range(3)]

    def body(ids_ref, *refs):
        if first:
            base_ref, g_ref, wg_ref, wu_ref, wd_ref, _, _, _, out_ref, xn_ref, gd_ref, sl_ref, h_ref, acc = refs
        else:
            base_ref, xn_ref, wg_ref, wu_ref, wd_ref, _, _, _, out_ref, gd_ref, sl_ref, h_ref, acc = refs
        j = pl.program_id(1)

        @pl.when(j == 0)
        def _():
            if first:
                xn_ref[...] = _rms_fwd(base_ref[...], g_ref[...], d).astype(BF16)
            acc[...] = jnp.zeros_like(acc)

        xnb = xn_ref[...]
        a = _nt(xnb, wg_ref[0])
        b = _nt(xnb, wu_ref[0])
        s = jax.nn.sigmoid(a)
        sl = a * s
        h = (sl * b).astype(BF16)
        gd_ref[...] = (b * (s * (1.0 + a * (1.0 - s)))).astype(BF16)
        sl_ref[...] = sl.astype(BF16)
        h_ref[...] = h
        acc[...] += _nn(h, wd_ref[0])

        @pl.when(j == n - 1)
        def _():
            out_ref[...] = base_ref[...] + 0.5 * acc[...]

    wspec = pl.BlockSpec((1, fp, d), lambda i, j, ids_ref: (ids_ref[j], 0, 0))
    row = pl.BlockSpec((tm, d), lambda i, j, ids_ref: (i, 0))
    ff = pl.BlockSpec((tm, fp), lambda i, j, ids_ref: (i, ids_ref[j]))
    ffs = jax.ShapeDtypeStruct((t, nb * fp), BF16)
    second = pl.BlockSpec((1, d), lambda i, j, ids_ref: (0, 0)) if first else row
    n_row_outs = 2 if first else 1
    res = pl.pallas_call(
        body, name=name,
        grid_spec=pltpu.PrefetchScalarGridSpec(
            num_scalar_prefetch=1, grid=(t // tm, n),
            in_specs=[row, second, wspec, wspec, wspec, ANY, ANY, ANY],
            out_specs=[row] * n_row_outs + [ff, ff, ff],
            scratch_shapes=[pltpu.VMEM((tm, d), F32)]),
        out_shape=[jax.ShapeDtypeStruct((t, d), F32)] + ([jax.ShapeDtypeStruct((t, d), BF16)] if first else [])
        + [ffs, ffs, ffs],
        input_output_aliases={6 + k: n_row_outs + k for k in range(3)},
        compiler_params=_params(("arbitrary", "arbitrary")),
    )(ids, base, g if first else xn, wg_t, wu_t, wd, *saved)
    return (res[0], res[1] if first else xn, list(res[n_row_outs:]))


def _ffn_bwd(dout, x, g, gd, sl, wg_t, wu_t, wd, tm, name):
    t, d = x.shape
    nb, fp, _ = wg_t.shape

    def body(do_hbm, x_hbm, g_ref, gd_ref, sl_ref, wg_ref, wu_ref, wd_ref, wd_next_ref,
             dx_hbm, dg_ref, da_ref, db_ref, acc, rowbuf, dy_scr, dh_scr, sem):
        i, j = pl.program_id(0), pl.program_id(1)
        rows = pl.ds(pl.multiple_of(i * tm, tm), tm)
        get_do = pltpu.make_async_copy(do_hbm.at[rows, :], rowbuf, sem)
        get_x = pltpu.make_async_copy(x_hbm.at[rows, :], rowbuf, sem)

        @pl.when(j == 0)
        def _():
            get_do.start()
            get_do.wait()
            dy_scr[...] = (0.5 * rowbuf[...]).astype(BF16)
            acc[...] = jnp.zeros_like(acc)
            dh_scr[0] = _nt(dy_scr[...], wd_ref[0])
            get_x.start()

        @pl.when((i == 0) & (j == 0))
        def _():
            dg_ref[...] = jnp.zeros_like(dg_ref)

        dh = dh_scr[j % 2]
        dh_scr[(j + 1) % 2] = _nt(dy_scr[...], wd_next_ref[0])
        da = (dh * gd_ref[...].astype(F32)).astype(BF16)
        db = (dh * sl_ref[...].astype(F32)).astype(BF16)
        da_ref[...] = da
        db_ref[...] = db
        acc[...] += _nn(da, wg_ref[0]) + _nn(db, wu_ref[0])

        @pl.when(j == nb - 1)
        def _():
            get_x.wait()
            dxn, dg = _rms_bwd(rowbuf[...], g_ref[...], acc[...], d)
            dg_ref[...] += dg
            acc[...] = dxn
            get_do.start()
            get_do.wait()
            acc[...] += rowbuf[...]
            out = pltpu.make_async_copy(acc, dx_hbm.at[rows, :], sem)
            out.start()
            out.wait()

    wspec = pl.BlockSpec((1, fp, d), lambda i, j: (j, 0, 0))
    wnext = pl.BlockSpec((1, fp, d), lambda i, j: (jnp.minimum(j + 1, nb - 1), 0, 0))
    vec = pl.BlockSpec((1, d), lambda i, j: (0, 0))
    ff = pl.BlockSpec((tm, fp), lambda i, j: (i, j))
    ffs = jax.ShapeDtypeStruct((t, nb * fp), BF16)
    return pl.pallas_call(
        body, name=name, grid=(t // tm, nb),
        in_specs=[ANY, ANY, vec, ff, ff, wspec, wspec, pl.BlockSpec((1, fp, d), lambda i, j: (0, 0, 0)), wnext],
        out_specs=[ANY, vec, ff, ff],
        out_shape=[jax.ShapeDtypeStruct((t, d), F32), jax.ShapeDtypeStruct((1, d), F32), ffs, ffs],
        scratch_shapes=[pltpu.VMEM((tm, d), F32), pltpu.VMEM((tm, d), F32), pltpu.VMEM((tm, d), BF16),
                        pltpu.VMEM((2, tm, fp), F32), pltpu.SemaphoreType.DMA],
        compiler_params=_params(("arbitrary", "arbitrary")),
    )(dout, x, g, gd, sl, wg_t, wu_t, wd, wd)


def _matmul_tn(lhs, rhs, bm, bn, tk, out_dtype, name, deps=(), out_rows=None):
    t, m = lhs.shape
    n = rhs.shape[1]
    nk = t // tk
    out_rows = m if out_rows is None else out_rows

    def body(l_ref, r_ref, *refs):
        o_ref, acc = refs[len(deps):]
        k = pl.program_id(2)

        @pl.when(k == 0)
        def _():
            acc[...] = jnp.zeros_like(acc)

        acc[...] += _tn(l_ref[...].astype(BF16), r_ref[...].astype(BF16))

        @pl.when(k == nk - 1)
        def _():
            o_ref[...] = acc[...].astype(out_dtype)

    return pl.pallas_call(
        body, name=name, grid=(m // bm, n // bn, nk),
        in_specs=[pl.BlockSpec((tk, bm), lambda i, j, k: (k, i)), pl.BlockSpec((tk, bn), lambda i, j, k: (k, j))]
        + [ANY] * len(deps),
        out_specs=pl.BlockSpec((bm, bn), lambda i, j, k: (i, j)),
        out_shape=jax.ShapeDtypeStruct((out_rows, n), out_dtype),
        scratch_shapes=[pltpu.VMEM((bm, bn), F32)],
        compiler_params=_params(("arbitrary", "arbitrary", "arbitrary")),
    )(lhs, rhs, *deps)


def _matmul_tn_resident(lhs, rhs, bm, out_dtype, name, deps=(), out_rows=None):
    t, m = lhs.shape
    n = rhs.shape[1]
    out_rows = m if out_rows is None else out_rows

    def body(l_ref, r_ref, *refs):
        refs[len(deps)][...] = _tn(l_ref[...], r_ref[...]).astype(out_dtype)

    return pl.pallas_call(
        body, name=name, grid=(m // bm,),
        in_specs=[pl.BlockSpec((t, bm), lambda i: (0, i)), WHOLE_VMEM] + [ANY] * len(deps),
        out_specs=pl.BlockSpec((bm, n), lambda i: (i, 0)),
        out_shape=jax.ShapeDtypeStruct((out_rows, n), out_dtype),
        compiler_params=_params(("arbitrary",)),
    )(lhs, rhs, *deps)


def _matmul_tn_blocks(lhs, rhs, bm, ids, prev, name, deps=()):
    t, m = lhs.shape
    n = rhs.shape[1]
    if prev is None:
        prev = lax.empty((m, n), BF16)

    def body(ids_ref, l_ref, r_ref, *refs):
        refs[1 + len(deps)][...] = _tn(l_ref[...], r_ref[...]).astype(BF16)

    return pl.pallas_call(
        body, name=name,
        grid_spec=pltpu.PrefetchScalarGridSpec(
            num_scalar_prefetch=1, grid=(ids.shape[0],),
            in_specs=[pl.BlockSpec((t, bm), lambda i, ids_ref: (0, ids_ref[i])), WHOLE_VMEM, ANY]
            + [ANY] * len(deps),
            out_specs=pl.BlockSpec((bm, n), lambda i, ids_ref: (ids_ref[i], 0))),
        out_shape=jax.ShapeDtypeStruct((m, n), BF16),
        input_output_aliases={3: 0},
        compiler_params=_params(("arbitrary",)),
    )(ids, lhs, rhs, prev, *deps)


def _last_rows_padded(w_ref, tail_ref, off, real):
    @pl.when(pl.program_id(0) == 0)
    def _():
        tail_ref[...] = jnp.zeros_like(tail_ref)
        tail_ref[0:real, :] = w_ref[off:off + real, :]


def _inproj_fwd(x, g, w_t, splits, tm):
    t, d = x.shape
    offs = [sum(splits[:k]) for k in range(len(splits))]
    real_last = w_t.shape[0] - offs[-1]

    def body(x_ref, g_ref, w_ref, hn_ref, *refs):
        z_refs, tail_ref = refs[:-1], refs[-1]
        _last_rows_padded(w_ref, tail_ref, offs[-1], real_last)
        hn = _rms_fwd(x_ref[...], g_ref[...], d).astype(BF16)
        hn_ref[...] = hn
        for z_ref, o, n in zip(z_refs[:-1], offs, splits):
            z_ref[...] = _nt(hn, w_ref[o:o + n, :])
        z_refs[-1][...] = _nt(hn, tail_ref[...])

    row = pl.BlockSpec((tm, d), lambda i: (i, 0))
    return pl.pallas_call(
        body, name="inproj_fwd", grid=(t // tm,),
        in_specs=[row, pl.BlockSpec((1, d), lambda i: (0, 0)), WHOLE_VMEM],
        out_specs=[row] + [pl.BlockSpec((tm, n), lambda i: (i, 0)) for n in splits],
        out_shape=[jax.ShapeDtypeStruct((t, d), BF16)] + [jax.ShapeDtypeStruct((t, n), F32) for n in splits],
        scratch_shapes=[pltpu.VMEM((splits[-1], d), BF16)],
        compiler_params=_params(("arbitrary",)),
    )(x, g, w_t)


def _inproj_bwd(dzs, x, g, w_t, dres, splits, tm):
    t, d = x.shape
    offs = [sum(splits[:k]) for k in range(len(splits))]
    ni = sum(splits)
    nz = len(splits)
    real_last = w_t.shape[0] - offs[-1]

    def body(*refs):
        dz_refs = refs[:nz]
        x_ref, g_ref, w_ref, dres_ref, dx_ref, dg_ref, dzc_ref, half_ref, tail_ref = refs[nz:]
        _last_rows_padded(w_ref, tail_ref, offs[-1], real_last)
        dhn = jnp.zeros((tm, d), F32)
        for k, (dz_ref, o, n) in enumerate(zip(dz_refs, offs, splits)):
            dz = dz_ref[...].astype(BF16)
            dzc_ref[:, o:o + n] = dz
            dhn += _nn(dz, tail_ref[...] if k == nz - 1 else w_ref[o:o + n, :])
        dx, dg = _rms_bwd(x_ref[...], g_ref[...], dhn, d)
        dx = dres_ref[...] + dx
        dx_ref[...] = dx
        half_ref[...] = (0.5 * dx).astype(BF16)

        @pl.when(pl.program_id(0) == 0)
        def _():
            dg_ref[...] = jnp.zeros_like(dg_ref)

        dg_ref[...] += dg

    row = pl.BlockSpec((tm, d), lambda i: (i, 0))
    vec = pl.BlockSpec((1, d), lambda i: (0, 0))
    return pl.pallas_call(
        body, name="inproj_bwd", grid=(t // tm,),
        in_specs=[pl.BlockSpec((tm, n), lambda i: (i, 0)) for n in splits] + [row, vec, WHOLE_VMEM, row],
        out_specs=[row, vec, pl.BlockSpec((tm, ni), lambda i: (i, 0)), row],
        out_shape=[jax.ShapeDtypeStruct((t, d), F32), jax.ShapeDtypeStruct((1, d), F32),
                   jax.ShapeDtypeStruct((t, ni), BF16), jax.ShapeDtypeStruct((t, d), BF16)],
        scratch_shapes=[pltpu.VMEM((splits[-1], d), BF16)],
        compiler_params=_params(("arbitrary",)),
    )(*dzs, x, g, w_t, dres)


def _tril_bf16(ws_ref, grp):
    rows = lax.broadcasted_iota(jnp.int32, (CHUNK, CHUNK), 0)
    cols = lax.broadcasted_iota(jnp.int32, (CHUNK, CHUNK), 1)
    return jnp.where(rows >= cols, ws_ref[grp], 0.0).astype(BF16)


def _gmlp_mix(zuv_ref, gv_ref, ws_ref, bias_ref, v_scr, mixed_scr, tm, w, groups):
    u = _gelu(zuv_ref[:, 0:w])
    v0 = _gelu(zuv_ref[:, w:2 * w])
    v_scr[...] = _rms_fwd(v0, gv_ref[...], w).astype(BF16)
    for grp in range(groups):
        wsm = _tril_bf16(ws_ref, grp)
        lanes = slice(grp * CHUNK, (grp + 1) * CHUNK)
        for c in range(tm // CHUNK):
            rows = slice(c * CHUNK, (c + 1) * CHUNK)
            mixed_scr[rows, lanes] = _nn(wsm, v_scr[rows, lanes]) + bias_ref[:, lanes]
    return u, v0


def _gmlp_fwd(zuv, gv, ws, bias, gout, tm):
    t, w2 = zuv.shape
    w = w2 // 2
    groups = ws.shape[0]

    def body(zuv_ref, gv_ref, ws_ref, bias_ref, go_ref, y_ref, v_scr, mixed_scr):
        u, _ = _gmlp_mix(zuv_ref, gv_ref, ws_ref, bias_ref, v_scr, mixed_scr, tm, w, groups)
        ya = u * mixed_scr[...]
        for grp in range(groups):
            lanes = slice(grp * CHUNK, (grp + 1) * CHUNK)
            y_ref[:, lanes] = _rms_fwd(ya[:, lanes], go_ref[:, lanes], CHUNK).astype(BF16)

    const2 = lambda i: (0, 0)
    return pl.pallas_call(
        body, name="gmlp_fwd", grid=(t // tm,),
        in_specs=[pl.BlockSpec((tm, w2), lambda i: (i, 0)), pl.BlockSpec((1, w), const2),
                  pl.BlockSpec((groups, CHUNK, CHUNK), lambda i: (0, 0, 0)),
                  pl.BlockSpec((CHUNK, w), const2), pl.BlockSpec((1, w), const2)],
        out_specs=pl.BlockSpec((tm, w), lambda i: (i, 0)),
        out_shape=jax.ShapeDtypeStruct((t, w), BF16),
        scratch_shapes=[pltpu.VMEM((tm, w), BF16), pltpu.VMEM((tm, w), F32)],
        compiler_params=_params(("arbitrary",)),
    )(zuv, gv, ws, bias, gout)


def _gmlp_bwd(dy, zuv, gv, ws, bias, gout, tm):
    t, w2 = zuv.shape
    w = w2 // 2
    groups = ws.shape[0]

    def body(dy_ref, zuv_ref, gv_ref, ws_ref, bias_ref, go_ref,
             dz_ref, dws_ref, dbias_ref, dgv_ref, dgo_ref, v_scr, mixed_scr, dmix_scr, dv_scr):
        @pl.when(pl.program_id(0) == 0)
        def _():
            dws_ref[...] = jnp.zeros_like(dws_ref)
            dbias_ref[...] = jnp.zeros_like(dbias_ref)
            dgv_ref[...] = jnp.zeros_like(dgv_ref)
            dgo_ref[...] = jnp.zeros_like(dgo_ref)

        u, v0 = _gmlp_mix(zuv_ref, gv_ref, ws_ref, bias_ref, v_scr, mixed_scr, tm, w, groups)
        mixed = mixed_scr[...]
        ya = u * mixed
        for grp in range(groups):
            lanes = slice(grp * CHUNK, (grp + 1) * CHUNK)
            dya, dgo = _rms_bwd(ya[:, lanes], go_ref[:, lanes], dy_ref[:, lanes], CHUNK)
            dgo_ref[:, lanes] += dgo
            dz_ref[:, lanes] = dya * mixed[:, lanes] * _gelu_grad(zuv_ref[:, lanes])
            dmix_scr[:, lanes] = dya * u[:, lanes]
        for grp in range(groups):
            wsm = _tril_bf16(ws_ref, grp)
            lanes = slice(grp * CHUNK, (grp + 1) * CHUNK)
            dws = jnp.zeros((CHUNK, CHUNK), F32)
            dbias = jnp.zeros((CHUNK, CHUNK), F32)
            for c in range(tm // CHUNK):
                rows = slice(c * CHUNK, (c + 1) * CHUNK)
                dm = dmix_scr[rows, lanes]
                dmb = dm.astype(BF16)
                dv_scr[rows, lanes] = _tn(wsm, dmb)
                dws += _nt(dmb, v_scr[rows, lanes])
                dbias += dm
            rr = lax.broadcasted_iota(jnp.int32, (CHUNK, CHUNK), 0)
            cc = lax.broadcasted_iota(jnp.int32, (CHUNK, CHUNK), 1)
            dws_ref[grp] += jnp.where(rr >= cc, dws, 0.0)
            dbias_ref[grp] += jnp.sum(dbias, axis=1, keepdims=True)
        dv0, dgv = _rms_bwd(v0, gv_ref[...], dv_scr[...], w)
        dgv_ref[...] += dgv
        dz_ref[:, w:2 * w] = dv0 * _gelu_grad(zuv_ref[:, w:2 * w])

    const2 = lambda i: (0, 0)
    const3 = lambda i: (0, 0, 0)
    return pl.pallas_call(
        body, name="gmlp_bwd", grid=(t // tm,),
        in_specs=[pl.BlockSpec((tm, w), lambda i: (i, 0)), pl.BlockSpec((tm, w2), lambda i: (i, 0)),
                  pl.BlockSpec((1, w), const2), pl.BlockSpec((groups, CHUNK, CHUNK), const3),
                  pl.BlockSpec((CHUNK, w), const2), pl.BlockSpec((1, w), const2)],
        out_specs=[pl.BlockSpec((tm, w2), lambda i: (i, 0)), pl.BlockSpec((groups, CHUNK, CHUNK), const3),
                   pl.BlockSpec((groups, CHUNK, 1), const3), pl.BlockSpec((1, w), const2), pl.BlockSpec((1, w), const2)],
        out_shape=[jax.ShapeDtypeStruct((t, w2), F32), jax.ShapeDtypeStruct((groups, CHUNK, CHUNK), F32),
                   jax.ShapeDtypeStruct((groups, CHUNK, 1), F32), jax.ShapeDtypeStruct((1, w), F32),
                   jax.ShapeDtypeStruct((1, w), F32)],
        scratch_shapes=[pltpu.VMEM((tm, w), BF16), pltpu.VMEM((tm, w), F32),
                        pltpu.VMEM((tm, w), F32), pltpu.VMEM((tm, w), F32)],
        compiler_params=_params(("arbitrary",)),
    )(dy, zuv, gv, ws, bias, gout)


def _rot(x, m_lo, m_hi):
    return pltpu.roll(x, LANE - ROPE // 2, 1) * m_lo + pltpu.roll(x, ROPE // 2, 1) * m_hi


def _rope_tables(pos, freq, tm, after):
    t = pos.shape[0]

    def body(pos_ref, freq_ref, _, cos_ref, sin_ref):
        ang = pos_ref[...] * freq_ref[...]
        cos_ref[...] = jnp.cos(ang)
        sin_ref[...] = jnp.sin(ang)

    tab = pl.BlockSpec((tm, LANE), lambda i: (i, 0))
    return pl.pallas_call(
        body, name="rope_tables", grid=(t // tm,),
        in_specs=[pl.BlockSpec((tm, 1), lambda i: (i, 0)), pl.BlockSpec((1, LANE), lambda i: (0, 0)), ANY],
        out_specs=[tab, tab], out_shape=[jax.ShapeDtypeStruct((t, LANE), F32)] * 2,
        compiler_params=_params(("arbitrary",)),
    )(pos, freq, after)


def _mla_proj_fwd(cq, ckv, krw, cos, sin, masks, gq, gkv, wq_t, wkv_t, gqh, gkh, tm):
    t, rq = cq.shape
    rkv = ckv.shape[1]
    heads = wq_t.shape[0]

    def body(cq_ref, ckv_ref, kr_ref, cos_ref, sin_ref, mk_ref, gq_ref, gkv_ref, wq_ref, wkv_ref,
             gqh_ref, gkh_ref, q_ref, k_ref, v_ref):
        cos, sin = cos_ref[...], sin_ref[...]
        m_lo, m_hi = mk_ref[0:1, :], mk_ref[1:2, :]
        cqn = _rms_fwd(cq_ref[...], gq_ref[...], rq).astype(BF16)
        ckvn = _rms_fwd(ckv_ref[...], gkv_ref[...], rkv).astype(BF16)
        kr = kr_ref[...]
        kr_ss = jnp.sum(kr * kr, axis=-1, keepdims=True)
        for h in range(heads):
            qh = _nt(cqn, wq_ref[h])
            qn = qh * _rstd(qh, QK) * gqh_ref[...]
            qr = qn[:, LANE:]
            q_ref[h, :, 0:LANE] = qn[:, 0:LANE].astype(BF16)
            q_ref[h, :, LANE:] = (qr * cos + _rot(qr, m_lo, m_hi) * sin).astype(BF16)
            kvh = _nt(ckvn, wkv_ref[h])
            kn = kvh[:, 0:LANE]
            rk = lax.rsqrt((jnp.sum(kn * kn, axis=-1, keepdims=True) + kr_ss) * (1.0 / QK) + EPS)
            k_ref[h, :, 0:LANE] = (kn * rk * gkh_ref[:, 0:LANE]).astype(BF16)
            krn = kr * rk * gkh_ref[:, LANE:]
            k_ref[h, :, LANE:] = (krn * cos + _rot(krn, m_lo, m_hi) * sin).astype(BF16)
            v_ref[h] = kvh[:, LANE:].astype(BF16)

    c2 = lambda i: (0, 0)
    c3 = lambda i: (0, 0, 0)
    return pl.pallas_call(
        body, name="mla_proj_fwd", grid=(t // tm,),
        in_specs=[pl.BlockSpec((tm, rq), lambda i: (i, 0)), pl.BlockSpec((tm, rkv), lambda i: (i, 0)),
                  pl.BlockSpec((tm, LANE), lambda i: (i, 0)), pl.BlockSpec((tm, LANE), lambda i: (i, 0)),
                  pl.BlockSpec((tm, LANE), lambda i: (i, 0)), pl.BlockSpec((2, LANE), c2),
                  pl.BlockSpec((1, rq), c2), pl.BlockSpec((1, rkv), c2),
                  pl.BlockSpec((heads, HEADW, rq), c3), pl.BlockSpec((heads, HEADW, rkv), c3),
                  pl.BlockSpec((1, HEADW), c2), pl.BlockSpec((1, HEADW), c2)],
        out_specs=[pl.BlockSpec((heads, tm, HEADW), lambda i: (0, i, 0)),
                   pl.BlockSpec((heads, tm, HEADW), lambda i: (0, i, 0)),
                   pl.BlockSpec((heads, tm, VHEAD), lambda i: (0, i, 0))],
        out_shape=[jax.ShapeDtypeStruct((heads, t, HEADW), BF16), jax.ShapeDtypeStruct((heads, t, HEADW), BF16),
                   jax.ShapeDtypeStruct((heads, t, VHEAD), BF16)],
        compiler_params=_params(("arbitrary",)),
    )(cq, ckv, krw, cos, sin, masks, gq, gkv, wq_t, wkv_t, gqh, gkh)


def _mla_proj_bwd(dq, dk, dv, cq, ckv, krw, cos, sin, masks, gq, gkv, wq_t, wkv_t, gqh, gkh, tm):
    t, rq = cq.shape
    rkv = ckv.shape[1]
    heads = wq_t.shape[0]

    def body(dq_ref, dk_ref, dv_ref, cq_ref, ckv_ref, kr_ref, cos_ref, sin_ref, mk_ref, gq_ref, gkv_ref,
             wq_ref, wkv_ref, gqh_ref, gkh_ref,
             dcq_ref, dckv_ref, dkr_ref, dwq_ref, dwkv_ref, dgq_ref, dgkv_ref, dgqh_ref, dgkh_ref):
        @pl.when(pl.program_id(0) == 0)
        def _():
            for r in (dwq_ref, dwkv_ref, dgq_ref, dgkv_ref, dgqh_ref, dgkh_ref):
                r[...] = jnp.zeros_like(r)

        cos, sin = cos_ref[...], sin_ref[...]
        m_lo, m_hi = mk_ref[0:1, :], mk_ref[1:2, :]

        def unrope(dy):
            return dy * cos - _rot(dy * sin, m_lo, m_hi)

        cqn = _rms_fwd(cq_ref[...], gq_ref[...], rq).astype(BF16)
        ckvn = _rms_fwd(ckv_ref[...], gkv_ref[...], rkv).astype(BF16)
        kr = kr_ref[...]
        kr_ss = jnp.sum(kr * kr, axis=-1, keepdims=True)
        dcqn = jnp.zeros((tm, rq), F32)
        dckvn = jnp.zeros((tm, rkv), F32)
        dkr = jnp.zeros((tm, LANE), F32)
        for h in range(heads):
            qh = _nt(cqn, wq_ref[h])
            dqn = jnp.concatenate([dq_ref[h, :, 0:LANE], unrope(dq_ref[h, :, LANE:])], axis=1)
            dqh, dg = _rms_bwd(qh, gqh_ref[...], dqn, QK)
            dgqh_ref[...] += dg
            dqh = dqh.astype(BF16)
            dcqn += _nn(dqh, wq_ref[h])
            dwq_ref[h] += _tn(dqh, cqn)

            kvh = _nt(ckvn, wkv_ref[h])
            kn = kvh[:, 0:LANE]
            rk = lax.rsqrt((jnp.sum(kn * kn, axis=-1, keepdims=True) + kr_ss) * (1.0 / QK) + EPS)
            dkn_n = dk_ref[h, :, 0:LANE]
            dkr_n = unrope(dk_ref[h, :, LANE:])
            knh, krh = kn * rk, kr * rk
            dgkh_ref[:, 0:LANE] += jnp.sum(dkn_n * knh, axis=0, keepdims=True)
            dgkh_ref[:, LANE:] += jnp.sum(dkr_n * krh, axis=0, keepdims=True)
            dkn_g, dkr_g = dkn_n * gkh_ref[:, 0:LANE], dkr_n * gkh_ref[:, LANE:]
            proj = (jnp.sum(dkn_g * knh, axis=-1, keepdims=True)
                    + jnp.sum(dkr_g * krh, axis=-1, keepdims=True)) * (1.0 / QK)
            dkr += rk * (dkr_g - krh * proj)
            dkvh = jnp.concatenate([rk * (dkn_g - knh * proj), dv_ref[h]], axis=1).astype(BF16)
            dckvn += _nn(dkvh, wkv_ref[h])
            dwkv_ref[h] += _tn(dkvh, ckvn)
        dkr_ref[...] = dkr
        dcq, dg = _rms_bwd(cq_ref[...], gq_ref[...], dcqn, rq)
        dcq_ref[...] = dcq
        dgq_ref[...] += dg
        dckv, dg = _rms_bwd(ckv_ref[...], gkv_ref[...], dckvn, rkv)
        dckv_ref[...] = dckv
        dgkv_ref[...] += dg

    c2 = lambda i: (0, 0)
    c3 = lambda i: (0, 0, 0)
    hq = pl.BlockSpec((heads, tm, HEADW), lambda i: (0, i, 0))
    return pl.pallas_call(
        body, name="mla_proj_bwd", grid=(t // tm,),
        in_specs=[hq, hq, pl.BlockSpec((heads, tm, VHEAD), lambda i: (0, i, 0)),
                  pl.BlockSpec((tm, rq), lambda i: (i, 0)), pl.BlockSpec((tm, rkv), lambda i: (i, 0)),
                  pl.BlockSpec((tm, LANE), lambda i: (i, 0)), pl.BlockSpec((tm, LANE), lambda i: (i, 0)),
                  pl.BlockSpec((tm, LANE), lambda i: (i, 0)), pl.BlockSpec((2, LANE), c2),
                  pl.BlockSpec((1, rq), c2), pl.BlockSpec((1, rkv), c2),
                  pl.BlockSpec((heads, HEADW, rq), c3), pl.BlockSpec((heads, HEADW, rkv), c3),
                  pl.BlockSpec((1, HEADW), c2), pl.BlockSpec((1, HEADW), c2)],
        out_specs=[pl.BlockSpec((tm, rq), lambda i: (i, 0)), pl.BlockSpec((tm, rkv), lambda i: (i, 0)),
                   pl.BlockSpec((tm, LANE), lambda i: (i, 0)),
                   pl.BlockSpec((heads, HEADW, rq), c3), pl.BlockSpec((heads, HEADW, rkv), c3),
                   pl.BlockSpec((1, rq), c2), pl.BlockSpec((1, rkv), c2),
                   pl.BlockSpec((1, HEADW), c2), pl.BlockSpec((1, HEADW), c2)],
        out_shape=[jax.ShapeDtypeStruct((t, rq), F32), jax.ShapeDtypeStruct((t, rkv), F32),
                   jax.ShapeDtypeStruct((t, LANE), F32),
                   jax.ShapeDtypeStruct((heads, HEADW, rq), F32), jax.ShapeDtypeStruct((heads, HEADW, rkv), F32),
                   jax.ShapeDtypeStruct((1, rq), F32), jax.ShapeDtypeStruct((1, rkv), F32),
                   jax.ShapeDtypeStruct((1, HEADW), F32), jax.ShapeDtypeStruct((1, HEADW), F32)],
        compiler_params=_params(("arbitrary",)),
    )(dq, dk, dv, cq, ckv, krw, cos, sin, masks, gq, gkv, wq_t, wkv_t, gqh, gkh)


def _lower_triangle(blk):
    return lax.broadcasted_iota(jnp.int32, (blk, blk), 0) >= lax.broadcasted_iota(jnp.int32, (blk, blk), 1)


def _attn_fwd(q, k, v, seq, blk):
    heads, t, _ = q.shape
    scale = QK ** -0.5
    nblk = seq // blk

    def body(q_ref, k_ref, v_ref, o_ref, lse_ref):
        tri = _lower_triangle(blk)
        for qi in range(nblk):
            rows = slice(qi * blk, (qi + 1) * blk)
            before = slice(0, qi * blk)
            qb = q_ref[0, rows, :]
            s_d = jnp.where(tri, _nt(qb, k_ref[0, rows, :]) * scale, -1e30)
            m = jnp.max(s_d, axis=-1, keepdims=True)
            if qi:
                s_b = _nt(qb, k_ref[0, before, :]) * scale
                m = jnp.maximum(m, jnp.max(s_b, axis=-1, keepdims=True))
                p_b = jnp.exp(s_b - m)
            p_d = jnp.exp(s_d - m)
            l = jnp.sum(p_d, axis=-1, keepdims=True)
            acc = _nn(p_d.astype(BF16), v_ref[0, rows, :])
            if qi:
                l += jnp.sum(p_b, axis=-1, keepdims=True)
                acc += _nn(p_b.astype(BF16), v_ref[0, before, :])
            o_ref[0, rows, :] = acc / l
            lse_ref[0, rows, :] = m + jnp.log(l)

    return pl.pallas_call(
        body, name="attn_fwd", grid=(heads, t // seq),
        in_specs=[pl.BlockSpec((1, seq, HEADW), lambda h, b: (h, b, 0)),
                  pl.BlockSpec((1, seq, HEADW), lambda h, b: (h, b, 0)),
                  pl.BlockSpec((1, seq, VHEAD), lambda h, b: (h, b, 0))],
        out_specs=[pl.BlockSpec((1, seq, VHEAD), lambda h, b: (h, b, 0)),
                   pl.BlockSpec((1, seq, 1), lambda h, b: (h, b, 0))],
        out_shape=[jax.ShapeDtypeStruct((heads, t, VHEAD), F32), jax.ShapeDtypeStruct((heads, t, 1), F32)],
        compiler_params=_params(("arbitrary", "arbitrary")),
    )(q, k, v)


def _attn_bwd(q, k, v, do, lse, delta, seq, blk, after):
    heads, t, _ = q.shape
    scale = QK ** -0.5
    nblk = seq // blk

    def body(q_ref, k_ref, v_ref, do_ref, lse_ref, dl_ref, _, dq_ref, dk_ref, dv_ref):
        tri = _lower_triangle(blk)
        dk_ref[...] = jnp.zeros_like(dk_ref)
        dv_ref[...] = jnp.zeros_like(dv_ref)
        for qi in range(nblk):
            rows = slice(qi * blk, (qi + 1) * blk)
            qb = q_ref[0, rows, :]
            dob = do_ref[0, rows, :]
            lse_b = lse_ref[0, rows, :]
            dl_b = dl_ref[0, rows, :]
            dq = jnp.zeros((blk, HEADW), F32)
            for keys, masked in ((slice(0, qi * blk), False), (rows, True)):
                if keys.stop == keys.start:
                    continue
                kb = k_ref[0, keys, :]
                p = jnp.exp(_nt(qb, kb) * scale - lse_b)
                if masked:
                    p = jnp.where(tri, p, 0.0)
                dp = _nt(dob, v_ref[0, keys, :])
                ds = (p * (dp - dl_b) * scale).astype(BF16)
                dv_ref[0, keys, :] += _tn(p.astype(BF16), dob)
                dk_ref[0, keys, :] += _tn(ds, qb)
                dq += _nn(ds, kb)
            dq_ref[0, rows, :] = dq

    hq = pl.BlockSpec((1, seq, HEADW), lambda h, b: (h, b, 0))
    hv = pl.BlockSpec((1, seq, VHEAD), lambda h, b: (h, b, 0))
    h1 = pl.BlockSpec((1, seq, 1), lambda h, b: (h, b, 0))
    return pl.pallas_call(
        body, name="attn_bwd", grid=(heads, t // seq),
        in_specs=[hq, hq, hv, hv, h1, h1, ANY],
        out_specs=[hq, hq, hv],
        out_shape=[jax.ShapeDtypeStruct((heads, t, HEADW), F32), jax.ShapeDtypeStruct((heads, t, HEADW), F32),
                   jax.ShapeDtypeStruct((heads, t, VHEAD), F32)],
        compiler_params=_params(("arbitrary", "arbitrary")),
    )(q, k, v, do, lse, delta, after)


def _out_fwd(ya, o, gb, w_out, x1, tm):
    t, w = ya.shape
    heads = o.shape[0]
    d = x1.shape[1]

    def body(ya_ref, o_ref, gb_ref, w_ref, x1_ref, x2_ref, yc_ref):
        yc_ref[:, 0:w] = ya_ref[...]
        for h in range(heads):
            lanes = slice(h * VHEAD, (h + 1) * VHEAD)
            yc_ref[:, w + h * VHEAD:w + (h + 1) * VHEAD] = _rms_fwd(o_ref[h], gb_ref[:, lanes], VHEAD).astype(BF16)
        x2_ref[...] = x1_ref[...] + _nn(yc_ref[...], w_ref[...])

    wy = w + heads * VHEAD
    row = pl.BlockSpec((tm, d), lambda i: (i, 0))
    return pl.pallas_call(
        body, name="out_fwd", grid=(t // tm,),
        in_specs=[pl.BlockSpec((tm, w), lambda i: (i, 0)), pl.BlockSpec((heads, tm, VHEAD), lambda i: (0, i, 0)),
                  pl.BlockSpec((1, heads * VHEAD), lambda i: (0, 0)), WHOLE_VMEM, row],
        out_specs=[row, pl.BlockSpec((tm, wy), lambda i: (i, 0))],
        out_shape=[jax.ShapeDtypeStruct((t, d), F32), jax.ShapeDtypeStruct((t, wy), BF16)],
        compiler_params=_params(("arbitrary",)),
    )(ya, o, gb, w_out, x1)


def _out_bwd(dx2, o, gb, w_out, w, tm):
    t, d = dx2.shape
    heads = o.shape[0]

    def body(dx_ref, o_ref, gb_ref, w_ref, dya_ref, do_ref, dl_ref, dgb_ref):
        @pl.when(pl.program_id(0) == 0)
        def _():
            dgb_ref[...] = jnp.zeros_like(dgb_ref)

        dyc = _nt(dx_ref[...].astype(BF16), w_ref[...])
        dya_ref[...] = dyc[:, 0:w]
        for h in range(heads):
            lanes = slice(h * VHEAD, (h + 1) * VHEAD)
            oh = o_ref[h]
            doh, dg = _rms_bwd(oh, gb_ref[:, lanes], dyc[:, w + h * VHEAD:w + (h + 1) * VHEAD], VHEAD)
            dgb_ref[:, lanes] += dg
            do_ref[h] = doh.astype(BF16)
            dl_ref[h] = jnp.sum(doh * oh, axis=-1, keepdims=True)

    ho = pl.BlockSpec((heads, tm, VHEAD), lambda i: (0, i, 0))
    vec = pl.BlockSpec((1, heads * VHEAD), lambda i: (0, 0))
    return pl.pallas_call(
        body, name="out_bwd", grid=(t // tm,),
        in_specs=[pl.BlockSpec((tm, d), lambda i: (i, 0)), ho, vec, WHOLE_VMEM],
        out_specs=[pl.BlockSpec((tm, w), lambda i: (i, 0)), ho, pl.BlockSpec((heads, tm, 1), lambda i: (0, i, 0)), vec],
        out_shape=[jax.ShapeDtypeStruct((t, w), F32), jax.ShapeDtypeStruct((heads, t, VHEAD), BF16),
                   jax.ShapeDtypeStruct((heads, t, 1), F32), jax.ShapeDtypeStruct((1, heads * VHEAD), F32)],
        compiler_params=_params(("arbitrary",)),
    )(dx2, o, gb, w_out)


def _loss_head(y, target, tm):
    t, d = y.shape

    def body(y_ref, t_ref, dy_ref, half_ref, loss_ref):
        @pl.when(pl.program_id(0) == 0)
        def _():
            loss_ref[...] = jnp.zeros_like(loss_ref)

        err = y_ref[...] - t_ref[...]
        dy = err * (1.0 / d)
        dy_ref[...] = dy
        half_ref[...] = (0.5 * dy).astype(BF16)
        part = jnp.sum(jnp.sum(err * err, axis=-1, keepdims=True) * (1.0 / d), axis=0, keepdims=True)
        loss_ref[...] += 0.5 * part

    row = pl.BlockSpec((tm, d), lambda i: (i, 0))
    return pl.pallas_call(
        body, name="loss_head", grid=(t // tm,),
        in_specs=[row, row], out_specs=[row, row, pl.BlockSpec((1, 1), lambda i: (0, 0))],
        out_shape=[jax.ShapeDtypeStruct((t, d), F32), jax.ShapeDtypeStruct((t, d), BF16),
                   jax.ShapeDtypeStruct((1, 1), F32)],
        compiler_params=_params(("arbitrary",)),
    )(y, target)


def _place():
    return lax.axis_index("x"), lax.axis_index("y"), lax.axis_index("c")


HBM = pl.BlockSpec(memory_space=pltpu.HBM)
SEM = pl.BlockSpec(memory_space=pltpu.SEMAPHORE)
DATAFLOW = pltpu.SideEffectType.DATAFLOW_SIDE_EFFECTING


def _plan_copies(plan, refs, send_sems, recv_sems):
    def block(ref, blk):
        if blk is None:
            return ref
        return ref.at[blk[0], pl.ds(0, blk[1])] if isinstance(blk, tuple) else ref.at[blk]

    cps = []
    for i, (sb, sblk, db, dblk, dev) in enumerate(plan(*_place())):
        cps.append(pltpu.make_async_remote_copy(
            src_ref=block(refs[sb], sblk), dst_ref=block(refs[db], dblk),
            send_sem=send_sems.at[i], recv_sem=recv_sems.at[i], device_id=dev, device_id_type=MESH))
    return cps


def _push_start(bufs, plan, ncopy, name, deps=()):
    nb = len(bufs)

    def body(*refs):
        outs = refs[nb + len(deps):]
        for cp in _plan_copies(plan, refs[:nb], outs[0], outs[1]):
            cp.start()
        outs[-1][...] = jnp.zeros_like(outs[-1])

    res = pl.pallas_call(
        body, name=name,
        out_shape=(pltpu.SemaphoreType.DMA((ncopy,)), pltpu.SemaphoreType.DMA((ncopy,)),
                   *[pltpu.HBM(b.shape, b.dtype) for b in bufs], jax.ShapeDtypeStruct((SUBLANE, LANE), F32)),
        in_specs=[HBM] * nb + [ANY] * len(deps),
        out_specs=(SEM, SEM, *[HBM] * nb, WHOLE_VMEM),
        input_output_aliases={i: 2 + i for i in range(nb)},
        compiler_params=pltpu.CompilerParams(has_side_effects=DATAFLOW),
    )(*[pltpu.with_memory_space_constraint(b, pltpu.HBM) for b in bufs], *deps)
    return res[0], res[1], list(res[2:2 + nb]), res[-1]


def _push_wait(send_sems, recv_sems, bufs, plan, after, name):
    nb = len(bufs)

    def body(*refs):
        for cp in _plan_copies(plan, refs[:nb], refs[nb], refs[nb + 1]):
            cp.wait_send()
            cp.wait_recv()

    res = pl.pallas_call(
        body, name=name,
        out_shape=[pltpu.HBM(b.shape, b.dtype) for b in bufs],
        in_specs=[HBM] * nb + [SEM, SEM, ANY], out_specs=[HBM] * nb,
        input_output_aliases={i: i for i in range(nb)},
        compiler_params=pltpu.CompilerParams(has_side_effects=DATAFLOW),
    )(*bufs, send_sems, recv_sems, after)
    return list(res)


def _other_chips(x, y):
    return ((1 - x, y), (x, 1 - y), (1 - x, 1 - y))


class _Exchange:
    def __init__(self, bufs, plan, ncopy, name, deps=()):
        self.plan, self.name = plan, name
        self.send, self.recv, self.bufs, self.token = _push_start(bufs, plan, ncopy, name + "_start", deps)

    def wait(self, after):
        return _push_wait(self.send, self.recv, self.bufs, self.plan, after, self.name + "_wait")


class _Chain:
    def __init__(self, bufs):
        self.bufs = list(bufs)

    def start(self, plan, ncopy, name, deps=()):
        send, recv, self.bufs, token = _push_start(self.bufs, plan, ncopy, name + "_start", deps)
        return (send, recv, plan, name), token

    def wait(self, pending, after):
        send, recv, plan, name = pending
        self.bufs = _push_wait(send, recv, self.bufs, plan, after, name + "_wait")


class _StagedGather:
    def __init__(self, shards, me, name, pad_to=None):
        self.n = n = len(shards)
        self.name = name
        rows = shards[0].shape[0]
        lands = []
        for s in shards:
            land = lax.empty((N_DEV, pad_to or rows) + s.shape[1:], s.dtype)
            if pad_to and pad_to != rows:
                land = lax.dynamic_update_slice(
                    land, jnp.zeros((N_DEV, pad_to - rows) + s.shape[1:], s.dtype), (0, rows, 0))
            lands.append(lax.dynamic_update_slice(land, s[None], (me, 0, 0)))
        self.chain = _Chain(list(shards) + lands)
        self.pending = {}

        def blk(b):
            return (b, rows) if pad_to and pad_to != rows else b

        def to_sibling(blocks):
            return lambda x, y, c: [(n + a, blk(b), n + a, blk(b), (x, y, 1 - c))
                                    for a in range(n) for b in blocks(x, y, c)]

        def nbr_blocks(x, y, c):
            return [4 * (1 - x) + 2 * y + c, 4 * x + 2 * (1 - y) + c]

        def diag(x, y, c):
            sx, sy = (1 - x) * (1 - c) + x * c, y * (1 - c) + (1 - y) * c
            tx, ty = x * (1 - c) + (1 - x) * c, (1 - y) * (1 - c) + y * c
            b = blk(4 * sx + 2 * sy + c)
            return [(n + a, b, n + a, b, (tx, ty, c)) for a in range(n)]

        self.plans = {
            "own": (lambda x, y, c: [(a, None, n + a, blk(4 * x + 2 * y + c), (x, y, 1 - c)) for a in range(n)], n),
            "nbr": (lambda x, y, c: [(a, None, n + a, blk(4 * x + 2 * y + c), dev) for a in range(n)
                                     for dev in ((1 - x, y, c), (x, 1 - y, c))], 2 * n),
            "diag": (diag, n),
            "nbr_d2d": (to_sibling(nbr_blocks), 2 * n),
            "own_nbr_d2d": (to_sibling(lambda x, y, c: [4 * x + 2 * y + c] + nbr_blocks(x, y, c)), 3 * n),
            "diag_d2d": (to_sibling(lambda x, y, c: [4 * (1 - x) + 2 * (1 - y) + c]), n),
        }

    def start(self, stage, deps=()):
        plan, ncopy = self.plans[stage]
        self.pending[stage], token = self.chain.start(plan, ncopy, self.name + "_" + stage, deps)
        return token

    def wait(self, stage, after):
        self.chain.wait(self.pending.pop(stage), after)

    def lands(self):
        return self.chain.bufs[self.n:]


def _gather_ici(shards, me, name, deps=()):
    n = len(shards)
    lands = [lax.dynamic_update_slice(lax.empty((N_DEV,) + s.shape, s.dtype), s[None], (me, 0, 0)) for s in shards]

    def plan(x, y, c):
        return [(a, None, n + a, 4 * x + 2 * y + c, (px, py, c)) for a in range(n) for px, py in _other_chips(x, y)]

    return _Exchange(list(shards) + lands, plan, 3 * n, name, deps)


def _gather_d2d(lands, name, deps=()):
    n = len(lands)

    def plan(x, y, c):
        blocks = [4 * x + 2 * y + c] + [4 * px + 2 * py + c for px, py in _other_chips(x, y)]
        return [(a, b, a, b, (x, y, 1 - c)) for a in range(n) for b in blocks]

    return _Exchange(list(lands), plan, 4 * n, name, deps)


def _reduce_d2d(grads, name, deps=(), rows=None):
    n = len(grads)
    lands = [lax.empty((4,) + g.shape[1:], g.dtype) for g in grads]

    def blk(b):
        return b if rows is None else (b, rows)

    def plan(x, y, c):
        return [(a, blk(2 * s + (1 - c)), n + a, blk(s), (x, y, 1 - c)) for a in range(n) for s in range(4)]

    return _Exchange(list(grads) + lands, plan, 4 * n, name, deps)


def _reduce_ici(chip, name, deps=(), rows=None):
    n = len(chip)
    lands = [lax.empty((3,) + g.shape[1:], g.dtype) for g in chip]

    def blk(b):
        return b if rows is None else (b, rows)

    def plan(x, y, c):
        return [(a, blk(2 * px + py), n + a, blk(k), (px, py, c))
                for a in range(n) for k, (px, py) in enumerate(_other_chips(x, y))]

    return _Exchange(list(chip) + lands, plan, 3 * n, name, deps)


def _pair_add(fulls, gots, core, name, rows=None):
    n = len(fulls)
    _, r, cdim = fulls[0].shape
    br = _row_block(rows or r, 512)

    def body(c_ref, *refs):
        for f_ref, g_ref, o_ref in zip(refs[:n], refs[n:2 * n], refs[2 * n:]):
            o_ref[...] = (f_ref[...].astype(F32) + g_ref[...].astype(F32)).astype(o_ref.dtype)

    mine = pl.BlockSpec((1, br, cdim), lambda s, i, c_ref: (2 * s + c_ref[0], i, 0))
    slot = pl.BlockSpec((1, br, cdim), lambda s, i, c_ref: (s, i, 0))
    return pl.pallas_call(
        body, name=name,
        grid_spec=pltpu.PrefetchScalarGridSpec(
            num_scalar_prefetch=1, grid=(4, (rows or r) // br),
            in_specs=[mine] * n + [slot] * n, out_specs=[slot] * n),
        out_shape=[jax.ShapeDtypeStruct((4, r, cdim), f.dtype) for f in fulls],
        compiler_params=_params(("arbitrary", "arbitrary")),
    )(core, *fulls, *gots)


def _sum_devices(stack):
    _, r, cdim = stack.shape

    def body(s_ref, o_ref):
        acc = s_ref[0]
        for k in range(1, N_DEV):
            acc = acc + s_ref[k]
        o_ref[...] = acc

    return pl.pallas_call(
        body, name="sum_devices", out_shape=jax.ShapeDtypeStruct((r, cdim), F32),
        compiler_params=_params(),
    )(stack)


def _adamw(w, g, m, v, name):
    r, cdim = w.shape
    br = _row_block(r, 256)

    def body(w_ref, g_ref, m_ref, v_ref, d_ref, nm_ref, nv_ref):
        g = g_ref[...]
        nm = ADAM_B1 * m_ref[...] + (1.0 - ADAM_B1) * g
        nv = ADAM_B2 * v_ref[...] + (1.0 - ADAM_B2) * (g * g)
        m_hat = nm / (1.0 - ADAM_B1 ** ADAM_STEP)
        v_hat = nv / (1.0 - ADAM_B2 ** ADAM_STEP)
        d_ref[...] = -ADAM_LR * (m_hat / (jnp.sqrt(v_hat) + ADAM_EPS) + ADAM_WD * w_ref[...])
        nm_ref[...] = nm
        nv_ref[...] = nv

    spec = pl.BlockSpec((br, cdim), lambda i: (i, 0))
    shape = jax.ShapeDtypeStruct((r, cdim), F32)
    return pl.pallas_call(
        body, name=name, grid=(r // br,), in_specs=[spec] * 4, out_specs=[spec] * 3,
        out_shape=[shape] * 3, compiler_params=_params(("arbitrary",)),
    )(w, g, m, v)


def _sum_adamw(chip, got, slot, w, m, v, name, after):
    rows, cdim = w.shape
    bc = 2 * LANE if cdim % (2 * LANE) == 0 else cdim

    def body(s_ref, c_ref, g_ref, w_ref, m_ref, v_ref, _, go_ref, d_ref, nm_ref, nv_ref):
        g = c_ref[0].astype(F32)
        for k in range(3):
            g = g + g_ref[k].astype(F32)
        nm = ADAM_B1 * m_ref[...] + (1.0 - ADAM_B1) * g
        nv = ADAM_B2 * v_ref[...] + (1.0 - ADAM_B2) * (g * g)
        m_hat = nm / (1.0 - ADAM_B1 ** ADAM_STEP)
        v_hat = nv / (1.0 - ADAM_B2 ** ADAM_STEP)
        go_ref[...] = g
        d_ref[...] = -ADAM_LR * (m_hat / (jnp.sqrt(v_hat) + ADAM_EPS) + ADAM_WD * w_ref[...])
        nm_ref[...] = nm
        nv_ref[...] = nv

    spec = pl.BlockSpec((rows, bc), lambda j, s_ref: (0, j))
    shape = jax.ShapeDtypeStruct((rows, cdim), F32)
    return pl.pallas_call(
        body, name=name,
        grid_spec=pltpu.PrefetchScalarGridSpec(
            num_scalar_prefetch=1, grid=(cdim // bc,),
            in_specs=[pl.BlockSpec((1, rows, bc), lambda j, s_ref: (s_ref[0], 0, j)),
                      pl.BlockSpec((3, rows, bc), lambda j, s_ref: (0, 0, j)), spec, spec, spec, ANY],
            out_specs=[spec] * 4),
        out_shape=[shape] * 4,
        compiler_params=_params(("arbitrary",)),
    )(slot, chip, got, w, m, v, after)


WEIGHTS = ("ffn1_norm_g", "ffn1_w_gate", "ffn1_w_up", "ffn1_w_down", "mix_norm_g", "w_in", "gmlp_v_norm_g",
           "gmlp_w_s", "gmlp_b_s", "mla_q_norm_g", "mla_w_q_up", "mla_kv_norm_g", "mla_w_kv_up", "mla_q_head_g",
           "mla_k_head_g", "gmlp_out_g", "mla_out_g", "w_out", "ffn2_norm_g", "ffn2_w_gate", "ffn2_w_up",
           "ffn2_w_down")
SHARDED = {"ffn1_w_gate": True, "ffn1_w_up": True, "ffn1_w_down": False, "w_in": True, "mla_w_q_up": True,
           "mla_w_kv_up": True, "w_out": False, "ffn2_w_gate": True, "ffn2_w_up": True, "ffn2_w_down": False}


def _col_block(m, target):
    best = LANE
    for cand in range(LANE, min(m, target) + 1, LANE):
        if m % cand == 0:
            best = cand
    return best


def _shard_rows(w, transposed, pad_to=None):
    rows = (w[0].T if transposed else w[0]).astype(BF16)
    if pad_to is not None and pad_to != rows.shape[0]:
        rows = jnp.pad(rows, ((0, pad_to - rows.shape[0]), (0, 0)))
    return rows


def _pack(parts):
    flat = []
    for p in parts:
        f = p.reshape(-1).astype(F32)
        flat.append(jnp.pad(f, (0, _round_up(f.size, LANE) - f.size)))
    flat = jnp.concatenate(flat)
    rows = _round_up(flat.size // LANE, SUBLANE)
    return jnp.pad(flat, (0, rows * LANE - flat.size)).reshape(rows, LANE)


def _unpack(packed, shapes):
    out, row = [], 0
    for shp in shapes:
        size = 1
        for s in shp:
            size *= s
        nrows = _round_up(size, LANE) // LANE
        out.append(packed[row:row + nrows].reshape(-1)[:size].reshape(shp))
        row += nrows
    return out


def kernel(x, positions, ffn1_norm_g, ffn1_w_gate, ffn1_w_up, ffn1_w_down, mix_norm_g, w_in, gmlp_v_norm_g, gmlp_w_s, gmlp_b_s, mla_q_norm_g, mla_w_q_up, mla_kv_norm_g, mla_w_kv_up, mla_q_head_g, mla_k_head_g, gmlp_out_g, mla_out_g, w_out, ffn2_norm_g, ffn2_w_gate, ffn2_w_up, ffn2_w_down, loss_target, m_ffn1_norm_g, m_ffn1_w_gate, m_ffn1_w_up, m_ffn1_w_down, m_mix_norm_g, m_w_in, m_gmlp_v_norm_g, m_gmlp_w_s, m_gmlp_b_s, m_mla_q_norm_g, m_mla_w_q_up, m_mla_kv_norm_g, m_mla_w_kv_up, m_mla_q_head_g, m_mla_k_head_g, m_gmlp_out_g, m_mla_out_g, m_w_out, m_ffn2_norm_g, m_ffn2_w_gate, m_ffn2_w_up, m_ffn2_w_down, v_ffn1_norm_g, v_ffn1_w_gate, v_ffn1_w_up, v_ffn1_w_down, v_mix_norm_g, v_w_in, v_gmlp_v_norm_g, v_gmlp_w_s, v_gmlp_b_s, v_mla_q_norm_g, v_mla_w_q_up, v_mla_kv_norm_g, v_mla_w_kv_up, v_mla_q_head_g, v_mla_k_head_g, v_gmlp_out_g, v_mla_out_g, v_w_out, v_ffn2_norm_g, v_ffn2_w_gate, v_ffn2_w_up, v_ffn2_w_down):
    wts = dict(zip(WEIGHTS, (ffn1_norm_g, ffn1_w_gate, ffn1_w_up, ffn1_w_down, mix_norm_g, w_in, gmlp_v_norm_g, gmlp_w_s, gmlp_b_s, mla_q_norm_g, mla_w_q_up, mla_kv_norm_g, mla_w_kv_up, mla_q_head_g, mla_k_head_g, gmlp_out_g, mla_out_g, w_out, ffn2_norm_g, ffn2_w_gate, ffn2_w_up, ffn2_w_down)))
    mom1 = dict(zip(WEIGHTS, (m_ffn1_norm_g, m_ffn1_w_gate, m_ffn1_w_up, m_ffn1_w_down, m_mix_norm_g, m_w_in, m_gmlp_v_norm_g, m_gmlp_w_s, m_gmlp_b_s, m_mla_q_norm_g, m_mla_w_q_up, m_mla_kv_norm_g, m_mla_w_kv_up, m_mla_q_head_g, m_mla_k_head_g, m_gmlp_out_g, m_mla_out_g, m_w_out, m_ffn2_norm_g, m_ffn2_w_gate, m_ffn2_w_up, m_ffn2_w_down)))
    mom2 = dict(zip(WEIGHTS, (v_ffn1_norm_g, v_ffn1_w_gate, v_ffn1_w_up, v_ffn1_w_down, v_mix_norm_g, v_w_in, v_gmlp_v_norm_g, v_gmlp_w_s, v_gmlp_b_s, v_mla_q_norm_g, v_mla_w_q_up, v_mla_kv_norm_g, v_mla_w_kv_up, v_mla_q_head_g, v_mla_k_head_g, v_gmlp_out_g, v_mla_out_g, v_w_out, v_ffn2_norm_g, v_ffn2_w_gate, v_ffn2_w_up, v_ffn2_w_down)))

    b_loc, seq, d = x.shape
    t = b_loc * seq
    ffs = ffn1_w_gate.shape[2]
    fp = _round_up(ffs, LANE)
    wg = gmlp_v_norm_g.shape[1]
    groups = gmlp_w_s.shape[1]
    rq, rkv = mla_q_norm_g.shape[1], mla_kv_norm_g.shape[1]
    heads = mla_out_g.shape[1]
    assert w_in.shape[2] * N_DEV == 2 * wg + rq + rkv + ROPE and mla_w_kv_up.shape[2] * N_DEV == heads * HEADW
    tm = min(512, t)
    tm_mix = min(256, t)
    blk = min(256, seq)

    xf = x.reshape(t, d)
    target = loss_target.reshape(t, d)
    pos = positions.reshape(t, 1).astype(F32)
    half = ROPE // 2
    inv_freq = 1.0 / (ROPE_THETA ** (jnp.arange(half, dtype=F32) / half))
    freq = jnp.concatenate([inv_freq, inv_freq, jnp.zeros((LANE - ROPE,), F32)])[None, :]
    lane = jnp.arange(LANE)
    masks = jnp.stack([jnp.where(lane < half, -1.0, 0.0),
                       jnp.where((lane >= half) & (lane < ROPE), 1.0, 0.0)]).astype(F32)
    gqh = jnp.pad(mla_q_head_g, ((0, 0), (0, HEADW - QK)))
    gkh = jnp.pad(mla_k_head_g, ((0, 0), (0, HEADW - QK)))
    bias = jnp.repeat(gmlp_b_s[0].T, CHUNK, axis=1)
    gouta = gmlp_out_g.reshape(1, wg)
    goutb = mla_out_g.reshape(1, heads * VHEAD)
    ws = gmlp_w_s[0]

    px, py, pc = _place()
    me = 4 * px + 2 * py + pc
    core = pc.astype(jnp.int32).reshape(1)
    slot = (2 * px + py).astype(jnp.int32).reshape(1)
    order = [n for n in WEIGHTS if n in SHARDED]
    group = {"ffn1": [n for n in order if n.startswith("ffn1")], "ffn2": [n for n in order if n.startswith("ffn2")],
             "mix": [n for n in order if not n.startswith("ffn")]}
    shard = {n: _shard_rows(wts[n], SHARDED[n]) for n in group["ffn1"]}
    frows = ffs if ffs != fp else None

    def tied(arr, token):
        return arr + token[0, 0].astype(arr.dtype)

    xnb, ynb, dgn = 4 * (1 - px) + 2 * py, 4 * px + 2 * (1 - py), 4 * (1 - px) + 2 * (1 - py)
    ids_a = jnp.stack([me, 4 * px + 2 * py + (1 - pc)]).astype(jnp.int32)
    ids_b = jnp.stack([xnb, xnb + 1, ynb, ynb + 1]).astype(jnp.int32)
    ids_c = jnp.stack([dgn, dgn + 1]).astype(jnp.int32)
    g1 = _StagedGather([shard[n] for n in group["ffn1"]], me, "gather_ffn1", pad_to=fp)
    token = g1.start("own")
    token = g1.start("nbr", deps=(token,))
    for n in group["mix"] + group["ffn2"]:
        shard[n] = _shard_rows(tied(wts[n], token), SHARDED[n])
    g3 = _StagedGather([shard[n] for n in group["ffn2"]], me, "gather_ffn2", pad_to=fp)
    g1.wait("own", token)
    x1, xn1, kept1 = _ffn_fwd(xf, None, ffn1_norm_g, ids_a, *g1.lands(), None, tm, "ffn1_fwd_a")
    rope_cos, rope_sin = _rope_tables(pos, freq, tm, x1)
    g1.wait("nbr", rope_cos)
    token = g1.start("diag")
    ici2 = _gather_ici([shard[n] for n in group["mix"]], me, "gather_mix_ici", deps=(token,))
    token = g3.start("nbr", deps=(ici2.token,))
    token = g1.start("nbr_d2d", deps=(token,))
    g1.wait("nbr_d2d", token)
    x1, xn1, kept1 = _ffn_fwd(x1, xn1, None, ids_b, *g1.lands(), kept1, tm, "ffn1_fwd_b")
    g1.wait("diag", x1)
    token = g1.start("diag_d2d")
    d2d2 = _gather_d2d(ici2.wait(x1)[len(group["mix"]):], "gather_mix_d2d", deps=(token,))
    g1.wait("diag_d2d", d2d2.token)
    full = dict(zip(group["ffn1"], g1.lands()))
    x1, xn1, (gd1, sl1, h1) = _ffn_fwd(x1, xn1, None, ids_c, full["ffn1_w_gate"], full["ffn1_w_up"],
                                       full["ffn1_w_down"], kept1, tm, "ffn1_fwd_c")
    full.update(zip(group["mix"], d2d2.wait(x1)))
    win_t = full["w_in"].reshape(-1, d)
    splits = (2 * wg, rq, rkv, LANE)
    wq_t = jnp.pad(full["mla_w_q_up"].reshape(heads, QK, rq), ((0, 0), (0, HEADW - QK), (0, 0)))
    wkv_t = full["mla_w_kv_up"].reshape(heads, HEADW, rkv)
    wout = full["w_out"].reshape(-1, d)
    hn, zuv, cq, ckv, krw = _inproj_fwd(x1, mix_norm_g, win_t, splits, tm)
    ya = _gmlp_fwd(zuv, gmlp_v_norm_g, ws, bias, gouta, tm_mix)
    g3.wait("nbr", ya)
    token = g3.start("diag")
    token = g3.start("own_nbr_d2d", deps=(token,))
    q, k, vv = _mla_proj_fwd(cq, ckv, krw, rope_cos, rope_sin, masks, mla_q_norm_g, mla_kv_norm_g, wq_t, wkv_t,
                             tied(gqh, token), gkh, tm)
    o, lse = _attn_fwd(q, k, vv, seq, blk)
    g3.wait("diag", o)
    token = g3.start("diag_d2d")
    x2, ycat = _out_fwd(ya, o, tied(goutb, token), wout, x1, tm)
    g3.wait("own_nbr_d2d", x2)
    g3.wait("diag_d2d", x2)
    full.update(zip(group["ffn2"], g3.lands()))
    x3, xn2, (gd2, sl2, h2) = _ffn_fwd(x2, None, ffn2_norm_g, jnp.arange(N_DEV, dtype=jnp.int32),
                                       full["ffn2_w_gate"], full["ffn2_w_up"], full["ffn2_w_down"], None, tm,
                                       "ffn2_fwd")
    dx3, dy3_half, loss_part = _loss_head(x3, target, tm)

    outs_g, outs_d, outs_m, outs_v = {}, {}, {}, {}

    def finish(names, chip, got, after):
        for n, cp, gt in zip(names, chip, got):
            rows_of = (lambda a: a[0].T) if SHARDED[n] else (lambda a: a[0])
            res = _sum_adamw(cp, gt, slot, rows_of(wts[n]), rows_of(mom1[n]), rows_of(mom2[n]), "adamw_" + n, after)
            outs_g[n], outs_d[n], outs_m[n], outs_v[n] = [r.T[None] if SHARDED[n] else r[None] for r in res]
            after = res[3]
        return after

    def chip_sums(names, ex, after, rows=None):
        res = ex.wait(after)
        fulls, gots = res[:len(names)], res[len(names):]
        if len({f.shape for f in fulls}) == 1:
            return list(_pair_add(fulls, gots, core, "pair_add_" + names[0], rows))
        return [_pair_add([f], [gt], core, "pair_add_" + n, rows)[0] for n, f, gt in zip(names, fulls, gots)]

    tk = min(2048, t)
    grads = {}
    small = {}
    dx2, small["ffn2_norm_g"], da2, db2 = _ffn_bwd(
        dx3, x2, ffn2_norm_g, gd2, sl2, full["ffn2_w_gate"], full["ffn2_w_up"], full["ffn2_w_down"], tm, "ffn2_bwd")
    grads["ffn2_w_gate"] = _matmul_tn_resident(da2, xn2, fp, BF16, "dw_ffn2_gate").reshape(N_DEV, fp, d)
    grads["ffn2_w_up"] = _matmul_tn_resident(db2, xn2, fp, BF16, "dw_ffn2_up").reshape(N_DEV, fp, d)
    grads["ffn2_w_down"] = _matmul_tn_resident(h2, dy3_half, fp, BF16, "dw_ffn2_down").reshape(N_DEV, fp, d)
    red_a2 = _reduce_d2d([grads[n] for n in group["ffn2"]], "reduce_ffn2_d2d", rows=frows)
    dya, do, delta, small["mla_out_g"] = _out_bwd(dx2, o, tied(goutb, red_a2.token), wout, wg, tm)
    grads["w_out"] = _matmul_tn(ycat, dx2, _col_block(ycat.shape[1], 768), d, tk, BF16, "dw_out").reshape(
        N_DEV, -1, d)
    chip2 = chip_sums(group["ffn2"], red_a2, dya, frows)
    red_b2 = _reduce_ici(chip2, "reduce_ffn2_ici", rows=frows)
    dq, dk, dv = _attn_bwd(q, k, vv, do, lse, delta, seq, blk, red_b2.token)
    (dcq, dckv, dkrw, dwq, dwkv, small["mla_q_norm_g"], small["mla_kv_norm_g"], dgqh, dgkh) = _mla_proj_bwd(
        dq, dk, dv, cq, ckv, krw, rope_cos, rope_sin, masks, mla_q_norm_g, mla_kv_norm_g, wq_t, wkv_t, gqh, gkh,
        tm)
    small["mla_q_head_g"], small["mla_k_head_g"] = dgqh[:, :QK], dgkh[:, :QK]
    grads["mla_w_q_up"] = dwq[:, :QK].astype(BF16).reshape(N_DEV, -1, rq)
    grads["mla_w_kv_up"] = dwkv.astype(BF16).reshape(N_DEV, -1, rkv)
    dzuv, small["gmlp_w_s"], dbs, small["gmlp_v_norm_g"], small["gmlp_out_g"] = _gmlp_bwd(
        dya, zuv, gmlp_v_norm_g, ws, bias, gouta, tm_mix)
    small["gmlp_b_s"] = dbs[:, :, 0]
    dx1, small["mix_norm_g"], dzc, dy1_half = _inproj_bwd([dzuv, dcq, dckv, dkrw], x1, mix_norm_g, win_t, dx2,
                                                          splits, tm_mix)
    grads["w_in"] = _matmul_tn_resident(dzc, hn, _col_block(dzc.shape[1], 768), BF16, "dw_in",
                                        out_rows=win_t.shape[0]).reshape(N_DEV, -1, d)
    res_b2 = red_b2.wait(grads["w_in"])
    red_am = _reduce_d2d([grads[n] for n in group["mix"]], "reduce_mix_d2d")

    def ffn1_dw(n, lhs, rhs, token):
        return _matmul_tn_resident(lhs, rhs, fp, BF16, "dw_" + n, deps=(token,)).reshape(N_DEV, fp, d)

    gr = ffn1_dw("ffn1_w_down", h1, dy1_half, red_am.token)
    red_ad = _reduce_d2d([gr], "reduce_ffn1_w_down_d2d", deps=(red_am.token,), rows=frows)
    dx0, small["ffn1_norm_g"], da1, db1 = _ffn_bwd(
        dx1, xf, tied(ffn1_norm_g, red_ad.token), gd1, sl1, full["ffn1_w_gate"], full["ffn1_w_up"],
        full["ffn1_w_down"], tm, "ffn1_bwd")
    chipm = chip_sums(group["mix"], red_am, dx0)
    red_bm = _reduce_ici(chipm, "reduce_mix_ici")
    red_b = [("ffn1_w_down", _reduce_ici(chip_sums(["ffn1_w_down"], red_ad, dx0, frows), "reduce_ffn1_w_down_ici",
                                         deps=(red_bm.token,), rows=frows))]
    rep = [n for n in WEIGHTS if n not in SHARDED]
    small_ici = _gather_ici([_pack([small[n] for n in rep] + [loss_part])], me, "gather_small_ici",
                            deps=(red_b[-1][1].token,))
    gr = ffn1_dw("ffn1_w_gate", da1, xn1, small_ici.token)
    red_ag = _reduce_d2d([gr], "reduce_ffn1_w_gate_d2d", rows=frows)
    n2 = group["ffn2"]
    after = finish(n2[0:1], res_b2[0:1], res_b2[3:4], red_ag.token)
    red_b.append(("ffn1_w_gate", _reduce_ici(chip_sums(["ffn1_w_gate"], red_ag, after, frows),
                                             "reduce_ffn1_w_gate_ici", rows=frows)))
    to_sibling = (2 * jnp.arange(4) + (1 - pc)).astype(jnp.int32)
    kept = (2 * jnp.arange(4) + pc).astype(jnp.int32)
    gr = _matmul_tn_blocks(db1, xn1, fp, to_sibling, None, "dw_ffn1_w_up_sent", deps=(red_b[-1][1].token,))
    small_d2d = _gather_d2d(small_ici.wait(gr)[1:], "gather_small_d2d")
    red_au = _reduce_d2d([gr.reshape(N_DEV, fp, d)], "reduce_ffn1_w_up_d2d", deps=(small_d2d.token,), rows=frows)
    gr = _matmul_tn_blocks(db1, xn1, fp, kept, red_au.bufs[0].reshape(-1, d), "dw_ffn1_w_up_kept",
                           deps=(red_au.token,))
    red_au.bufs[0] = gr.reshape(N_DEV, fp, d)
    red_b.append(("ffn1_w_up", _reduce_ici(chip_sums(["ffn1_w_up"], red_au, gr, frows),
                                           "reduce_ffn1_w_up_ici", rows=frows)))
    after = finish(n2[1:3], res_b2[1:3], res_b2[4:6], red_b[-1][1].token)
    res = red_bm.wait(after)
    nm_ = len(group["mix"])
    after = finish(group["mix"], res[:nm_], res[nm_:], after)
    total = _sum_devices(small_d2d.wait(after)[0])
    zero = jnp.zeros((1,), F32)
    dlt, nm, nv = _adamw(_pack([wts[n] for n in rep] + [zero]), total, _pack([mom1[n] for n in rep] + [zero]),
                         _pack([mom2[n] for n in rep] + [zero]), "adamw_small")
    shapes = [wts[n].shape for n in rep] + [(1,)]
    for n, g, dl, m1, m2 in zip(rep, _unpack(total, shapes), _unpack(dlt, shapes), _unpack(nm, shapes),
                                _unpack(nv, shapes)):
        outs_g[n], outs_d[n], outs_m[n], outs_v[n] = g, dl, m1, m2
    loss = _unpack(total, shapes)[-1].reshape(())
    after = dlt
    for n, ex in red_b:
        res = ex.wait(after)
        after = finish([n], res[:1], res[1:], after)

    return (loss, dx0.reshape(b_loc, seq, d), *[outs_g[n] for n in WEIGHTS], *[outs_d[n] for n in WEIGHTS],
            *[outs_m[n] for n in WEIGHTS], *[outs_v[n] for n in WEIGHTS])
```
